```python
import math
import jax, jax.numpy as jnp
from jax import lax
import numpy as np

D_MODEL = 2048
BATCH = 8
SEQ = 4096
DEPTH = 1

CHUNK = 64
D_MIX = 2 * D_MODEL
D_SSD = D_MIX // 2
D_LRU = D_MIX - D_SSD
SSD_HEAD_DIM = 64
SSD_HEADS = D_SSD // SSD_HEAD_DIM
SSD_GROUPS = 8
SSD_STATE = 128
D_XBC = D_SSD + 2 * SSD_GROUPS * SSD_STATE
LRU_HEADS = 16
LRU_BLOCK = D_LRU // LRU_HEADS
LRU_C = 8.0
CONV_WIDTH = 4
D_FF = 4 * D_MODEL
D_IN = D_SSD + D_XBC + SSD_HEADS + 2 * D_LRU
EPS = 1e-6

kernel_name = "hybrid_ssd_rglru_parallel_block"


def rms_norm(x, g):
    xf = x.astype(jnp.float32)
    y = xf * lax.rsqrt(jnp.mean(xf * xf, axis=-1, keepdims=True) + EPS)
    return (y * g.astype(jnp.float32)).astype(x.dtype)


def causal_dwconv(x, w, b):
    c = x.shape[-1]
    y = lax.conv_general_dilated(
        x, w[:, None, :].astype(x.dtype), window_strides=(1,),
        padding=[(w.shape[0] - 1, 0)],
        dimension_numbers=("NWC", "WIO", "NWC"), feature_group_count=c)
    return y + b.astype(x.dtype)


def ssd_scan(xh, dt, a, bm, cm, d_skip):
    b_, l_, h_, p_ = xh.shape
    g_, n_ = bm.shape[2], bm.shape[3]
    k_ = h_ // g_
    c_ = l_ // CHUNK
    x6 = (xh * dt[..., None]).reshape(b_, c_, CHUNK, g_, k_, p_)
    adt = (dt * a).reshape(b_, c_, CHUNK, g_, k_)
    acs = jnp.cumsum(adt, axis=2)
    bc = bm.reshape(b_, c_, CHUNK, g_, n_)
    cc = cm.reshape(b_, c_, CHUNK, g_, n_)
    causal = jnp.tril(jnp.ones((CHUNK, CHUNK), dtype=bool))[None, None, :, :, None, None]
    seg = acs[:, :, :, None] - acs[:, :, None, :]
    decay = jnp.exp(jnp.where(causal, seg, -jnp.inf))
    scores = jnp.einsum("bclgn,bcsgn->bclsg", cc, bc)
    y_diag = jnp.einsum("bclsg,bclsgk,bcsgkp->bclgkp", scores, decay, x6)
    decay_to_end = jnp.exp(acs[:, :, -1:] - acs)
    states = jnp.einsum("bcsgn,bcsgk,bcsgkp->bcgkpn", bc, decay_to_end, x6)
    chunk_decay = jnp.exp(acs[:, :, -1])

    def step(h, inp):
        s, dcy = inp
        return h * dcy[..., None, None] + s, h

    h0 = jnp.zeros((b_, g_, k_, p_, n_), dtype=xh.dtype)
    _, prev = lax.scan(step, h0, (jnp.moveaxis(states, 1, 0), jnp.moveaxis(chunk_decay, 1, 0)))
    prev = jnp.moveaxis(prev, 0, 1)
    y_off = jnp.einsum("bclgn,bcgkpn,bclgk->bclgkp", cc, prev, jnp.exp(acs))
    y = (y_diag + y_off).reshape(b_, l_, h_, p_)
    return y + xh * d_skip[:, None]


def rg_lru(xl, w_a, b_a, w_x, b_x, lam):
    b_, l_, _ = xl.shape
    xb = xl.reshape(b_, l_, LRU_HEADS, LRU_BLOCK)
    r = jax.nn.sigmoid(jnp.einsum("blhi,hij->blhj", xb, w_a) + b_a).reshape(b_, l_, D_LRU)
    i = jax.nn.sigmoid(jnp.einsum("blhi,hij->blhj", xb, w_x) + b_x).reshape(b_, l_, D_LRU)
    log_a = -LRU_C * r * jax.nn.softplus(-lam)
    a = jnp.exp(log_a)
    u = jnp.sqrt(-jnp.expm1(2.0 * log_a)) * (i * xl)

    def combine(e1, e2):
        a1, b1 = e1
        a2, b2 = e2
        return a1 * a2, a2 * b1 + b2

    _, h = lax.associative_scan(combine, (a, u), axis=1)
    return h


def _fwd_setup_inputs(seed: int = 0) -> dict:
    key = jax.random.key(seed)
    ks = jax.random.split(key, 32)
    f32 = jnp.float32
    nrm = lambda k, s, sc: jax.random.normal(k, s, f32) * sc
    gain = lambda k, s: 1.0 + 0.02 * jax.random.normal(k, s, f32)
    L = DEPTH
    dt0 = jnp.exp(jax.random.uniform(ks[10], (L, SSD_HEADS), f32, math.log(1e-3), math.log(1e-1)))
    a0 = jax.random.uniform(ks[14], (L, D_LRU), f32, 0.9, 0.999)
    a_base = jnp.exp(jnp.log(a0) / LRU_C)
    return {
        "x": jax.random.normal(ks[0], (BATCH, SEQ, D_MODEL), f32),
        "pre_mix_norm": gain(ks[1], (L, D_MODEL)),
        "w_in": nrm(ks[2], (L, D_MODEL, D_IN), D_MODEL ** -0.5),
        "ssd_conv_w": nrm(ks[3], (L, CONV_WIDTH, D_XBC), CONV_WIDTH ** -0.5),
        "ssd_conv_b": nrm(ks[4], (L, D_XBC), 0.01),
        "ssd_dt_bias": dt0 + jnp.log(-jnp.expm1(-dt0)),
        "ssd_a_log": jnp.log(jax.random.uniform(ks[11], (L, SSD_HEADS), f32, 1.0, 16.0)),
        "ssd_d": gain(ks[12], (L, SSD_HEADS)),
        "ssd_norm": gain(ks[13], (L, D_SSD)),
        "lru_conv_w": nrm(ks[5], (L, CONV_WIDTH, D_LRU), CONV_WIDTH ** -0.5),
        "lru_conv_b": nrm(ks[6], (L, D_LRU), 0.01),
        "lru_w_a": nrm(ks[7], (L, LRU_HEADS, LRU_BLOCK, LRU_BLOCK), LRU_BLOCK ** -0.5),
        "lru_b_a": nrm(ks[8], (L, LRU_HEADS, LRU_BLOCK), 0.01),
        "lru_w_x": nrm(ks[9], (L, LRU_HEADS, LRU_BLOCK, LRU_BLOCK), LRU_BLOCK ** -0.5),
        "lru_b_x": nrm(ks[15], (L, LRU_HEADS, LRU_BLOCK), 0.01),
        "lru_lambda": jnp.log(a_base) - jnp.log1p(-a_base),
        "lru_norm": gain(ks[16], (L, D_LRU)),
        "w_out": nrm(ks[17], (L, D_MIX, D_MODEL), D_MIX ** -0.5),
        "post_mix_norm": gain(ks[18], (L, D_MODEL)),
        "pre_mlp_norm": gain(ks[19], (L, D_MODEL)),
        "w_mlp_in": nrm(ks[20], (L, D_MODEL, D_FF), D_MODEL ** -0.5),
        "w_mlp_out": nrm(ks[21], (L, D_FF, D_MODEL), D_FF ** -0.5),
        "post_mlp_norm": gain(ks[22], (L, D_MODEL)),
    }


def _fwd_reference(x, pre_mix_norm, w_in, ssd_conv_w, ssd_conv_b, ssd_dt_bias, ssd_a_log, ssd_d,
              ssd_norm, lru_conv_w, lru_conv_b, lru_w_a, lru_b_a, lru_w_x, lru_b_x, lru_lambda,
              lru_norm, w_out, post_mix_norm, pre_mlp_norm, w_mlp_in, w_mlp_out, post_mlp_norm):
    f32 = jnp.float32
    b_, l_, _ = x.shape
    split_at = np.cumsum([D_SSD, D_XBC, SSD_HEADS, D_LRU]).tolist()
    for li in range(DEPTH):
        h = rms_norm(x, pre_mix_norm[li])
        proj = h @ w_in[li].astype(h.dtype)
        z, xbc, dt_raw, gate_lru, x_lru = jnp.split(proj, split_at, axis=-1)

        xbc = jax.nn.silu(causal_dwconv(xbc, ssd_conv_w[li], ssd_conv_b[li])).astype(f32)
        xs, bm, cm = jnp.split(xbc, [D_SSD, D_SSD + SSD_GROUPS * SSD_STATE], axis=-1)
        dt = jax.nn.softplus(dt_raw.astype(f32) + ssd_dt_bias[li].astype(f32))
        a = -jnp.exp(ssd_a_log[li].astype(f32))
        y_ssd = ssd_scan(xs.reshape(b_, l_, SSD_HEADS, SSD_HEAD_DIM), dt, a,
                         bm.reshape(b_, l_, SSD_GROUPS, SSD_STATE),
                         cm.reshape(b_, l_, SSD_GROUPS, SSD_STATE), ssd_d[li].astype(f32))
        y_ssd = y_ssd.reshape(b_, l_, D_SSD) * jax.nn.silu(z.astype(f32))
        yg = y_ssd.reshape(b_, l_, SSD_GROUPS, D_SSD // SSD_GROUPS)
        yg = yg * lax.rsqrt(jnp.mean(yg * yg, axis=-1, keepdims=True) + EPS)
        y_ssd = (yg.reshape(b_, l_, D_SSD) * ssd_norm[li].astype(f32)).astype(x.dtype)

        xl = causal_dwconv(x_lru, lru_conv_w[li], lru_conv_b[li]).astype(f32)
        hl = rg_lru(xl, lru_w_a[li].astype(f32), lru_b_a[li].astype(f32),
                    lru_w_x[li].astype(f32), lru_b_x[li].astype(f32), lru_lambda[li].astype(f32))
        y_lru = rms_norm(hl * jax.nn.gelu(gate_lru.astype(f32)), lru_norm[li]).astype(x.dtype)

        mix = jnp.concatenate([y_ssd, y_lru], axis=-1) @ w_out[li].astype(x.dtype)
        x = x + rms_norm(mix, post_mix_norm[li])

        hm = rms_norm(x, pre_mlp_norm[li]) @ w_mlp_in[li].astype(x.dtype)
        hm = jnp.square(jax.nn.relu(hm)) @ w_mlp_out[li].astype(x.dtype)
        x = x + rms_norm(hm, post_mlp_norm[li])
    return x


import jax as _jax
import jax.numpy as _jnp

TWIN_FORMAT = 'train_step'
FWD_PARAMS = ['x', 'pre_mix_norm', 'w_in', 'ssd_conv_w', 'ssd_conv_b', 'ssd_dt_bias', 'ssd_a_log', 'ssd_d', 'ssd_norm', 'lru_conv_w', 'lru_conv_b', 'lru_w_a', 'lru_b_a', 'lru_w_x', 'lru_b_x', 'lru_lambda', 'lru_norm', 'w_out', 'post_mix_norm', 'pre_mlp_norm', 'w_mlp_in', 'w_mlp_out', 'post_mlp_norm']
TWIN_WEIGHTS = ['pre_mix_norm', 'w_in', 'ssd_conv_w', 'ssd_conv_b', 'ssd_dt_bias', 'ssd_a_log', 'ssd_d', 'ssd_norm', 'lru_conv_w', 'lru_conv_b', 'lru_w_a', 'lru_b_a', 'lru_w_x', 'lru_b_x', 'lru_lambda', 'lru_norm', 'w_out', 'post_mix_norm', 'pre_mlp_norm', 'w_mlp_in', 'w_mlp_out', 'post_mlp_norm']
TWIN_DIFF_INPUT = 'x'
TWIN_INPUTS = ['x', 'pre_mix_norm', 'w_in', 'ssd_conv_w', 'ssd_conv_b', 'ssd_dt_bias', 'ssd_a_log', 'ssd_d', 'ssd_norm', 'lru_conv_w', 'lru_conv_b', 'lru_w_a', 'lru_b_a', 'lru_w_x', 'lru_b_x', 'lru_lambda', 'lru_norm', 'w_out', 'post_mix_norm', 'pre_mlp_norm', 'w_mlp_in', 'w_mlp_out', 'post_mlp_norm', 'loss_target', 'm_pre_mix_norm', 'm_w_in', 'm_ssd_conv_w', 'm_ssd_conv_b', 'm_ssd_dt_bias', 'm_ssd_a_log', 'm_ssd_d', 'm_ssd_norm', 'm_lru_conv_w', 'm_lru_conv_b', 'm_lru_w_a', 'm_lru_b_a', 'm_lru_w_x', 'm_lru_b_x', 'm_lru_lambda', 'm_lru_norm', 'm_w_out', 'm_post_mix_norm', 'm_pre_mlp_norm', 'm_w_mlp_in', 'm_w_mlp_out', 'm_post_mlp_norm', 'v_pre_mix_norm', 'v_w_in', 'v_ssd_conv_w', 'v_ssd_conv_b', 'v_ssd_dt_bias', 'v_ssd_a_log', 'v_ssd_d', 'v_ssd_norm', 'v_lru_conv_w', 'v_lru_conv_b', 'v_lru_w_a', 'v_lru_b_a', 'v_lru_w_x', 'v_lru_b_x', 'v_lru_lambda', 'v_lru_norm', 'v_w_out', 'v_post_mix_norm', 'v_pre_mlp_norm', 'v_w_mlp_in', 'v_w_mlp_out', 'v_post_mlp_norm']
TWIN_OUTPUTS = ['loss', 'grad_x', 'grad_pre_mix_norm', 'grad_w_in', 'grad_ssd_conv_w', 'grad_ssd_conv_b', 'grad_ssd_dt_bias', 'grad_ssd_a_log', 'grad_ssd_d', 'grad_ssd_norm', 'grad_lru_conv_w', 'grad_lru_conv_b', 'grad_lru_w_a', 'grad_lru_b_a', 'grad_lru_w_x', 'grad_lru_b_x', 'grad_lru_lambda', 'grad_lru_norm', 'grad_w_out', 'grad_post_mix_norm', 'grad_pre_mlp_norm', 'grad_w_mlp_in', 'grad_w_mlp_out', 'grad_post_mlp_norm', 'delta_pre_mix_norm', 'delta_w_in', 'delta_ssd_conv_w', 'delta_ssd_conv_b', 'delta_ssd_dt_bias', 'delta_ssd_a_log', 'delta_ssd_d', 'delta_ssd_norm', 'delta_lru_conv_w', 'delta_lru_conv_b', 'delta_lru_w_a', 'delta_lru_b_a', 'delta_lru_w_x', 'delta_lru_b_x', 'delta_lru_lambda', 'delta_lru_norm', 'delta_w_out', 'delta_post_mix_norm', 'delta_pre_mlp_norm', 'delta_w_mlp_in', 'delta_w_mlp_out', 'delta_post_mlp_norm', 'new_m_pre_mix_norm', 'new_m_w_in', 'new_m_ssd_conv_w', 'new_m_ssd_conv_b', 'new_m_ssd_dt_bias', 'new_m_ssd_a_log', 'new_m_ssd_d', 'new_m_ssd_norm', 'new_m_lru_conv_w', 'new_m_lru_conv_b', 'new_m_lru_w_a', 'new_m_lru_b_a', 'new_m_lru_w_x', 'new_m_lru_b_x', 'new_m_lru_lambda', 'new_m_lru_norm', 'new_m_w_out', 'new_m_post_mix_norm', 'new_m_pre_mlp_norm', 'new_m_w_mlp_in', 'new_m_w_mlp_out', 'new_m_post_mlp_norm', 'new_v_pre_mix_norm', 'new_v_w_in', 'new_v_ssd_conv_w', 'new_v_ssd_conv_b', 'new_v_ssd_dt_bias', 'new_v_ssd_a_log', 'new_v_ssd_d', 'new_v_ssd_norm', 'new_v_lru_conv_w', 'new_v_lru_conv_b', 'new_v_lru_w_a', 'new_v_lru_b_a', 'new_v_lru_w_x', 'new_v_lru_b_x', 'new_v_lru_lambda', 'new_v_lru_norm', 'new_v_w_out', 'new_v_post_mix_norm', 'new_v_pre_mlp_norm', 'new_v_w_mlp_in', 'new_v_w_mlp_out', 'new_v_post_mlp_norm']
TWIN_LEAF_KINDS = {'loss': 'loss', 'grad_x': 'grad_x', 'grad_pre_mix_norm': 'grad_w', 'grad_w_in': 'grad_w', 'grad_ssd_conv_w': 'grad_w', 'grad_ssd_conv_b': 'grad_w', 'grad_ssd_dt_bias': 'grad_w', 'grad_ssd_a_log': 'grad_w', 'grad_ssd_d': 'grad_w', 'grad_ssd_norm': 'grad_w', 'grad_lru_conv_w': 'grad_w', 'grad_lru_conv_b': 'grad_w', 'grad_lru_w_a': 'grad_w', 'grad_lru_b_a': 'grad_w', 'grad_lru_w_x': 'grad_w', 'grad_lru_b_x': 'grad_w', 'grad_lru_lambda': 'grad_w', 'grad_lru_norm': 'grad_w', 'grad_w_out': 'grad_w', 'grad_post_mix_norm': 'grad_w', 'grad_pre_mlp_norm': 'grad_w', 'grad_w_mlp_in': 'grad_w', 'grad_w_mlp_out': 'grad_w', 'grad_post_mlp_norm': 'grad_w', 'delta_pre_mix_norm': 'delta_w', 'delta_w_in': 'delta_w', 'delta_ssd_conv_w': 'delta_w', 'delta_ssd_conv_b': 'delta_w', 'delta_ssd_dt_bias': 'delta_w', 'delta_ssd_a_log': 'delta_w', 'delta_ssd_d': 'delta_w', 'delta_ssd_norm': 'delta_w', 'delta_lru_conv_w': 'delta_w', 'delta_lru_conv_b': 'delta_w', 'delta_lru_w_a': 'delta_w', 'delta_lru_b_a': 'delta_w', 'delta_lru_w_x': 'delta_w', 'delta_lru_b_x': 'delta_w', 'delta_lru_lambda': 'delta_w', 'delta_lru_norm': 'delta_w', 'delta_w_out': 'delta_w', 'delta_post_mix_norm': 'delta_w', 'delta_pre_mlp_norm': 'delta_w', 'delta_w_mlp_in': 'delta_w', 'delta_w_mlp_out': 'delta_w', 'delta_post_mlp_norm': 'delta_w', 'new_m_pre_mix_norm': 'new_m', 'new_m_w_in': 'new_m', 'new_m_ssd_conv_w': 'new_m', 'new_m_ssd_conv_b': 'new_m', 'new_m_ssd_dt_bias': 'new_m', 'new_m_ssd_a_log': 'new_m', 'new_m_ssd_d': 'new_m', 'new_m_ssd_norm': 'new_m', 'new_m_lru_conv_w': 'new_m', 'new_m_lru_conv_b': 'new_m', 'new_m_lru_w_a': 'new_m', 'new_m_lru_b_a': 'new_m', 'new_m_lru_w_x': 'new_m', 'new_m_lru_b_x': 'new_m', 'new_m_lru_lambda': 'new_m', 'new_m_lru_norm': 'new_m', 'new_m_w_out': 'new_m', 'new_m_post_mix_norm': 'new_m', 'new_m_pre_mlp_norm': 'new_m', 'new_m_w_mlp_in': 'new_m', 'new_m_w_mlp_out': 'new_m', 'new_m_post_mlp_norm': 'new_m', 'new_v_pre_mix_norm': 'new_v', 'new_v_w_in': 'new_v', 'new_v_ssd_conv_w': 'new_v', 'new_v_ssd_conv_b': 'new_v', 'new_v_ssd_dt_bias': 'new_v', 'new_v_ssd_a_log': 'new_v', 'new_v_ssd_d': 'new_v', 'new_v_ssd_norm': 'new_v', 'new_v_lru_conv_w': 'new_v', 'new_v_lru_conv_b': 'new_v', 'new_v_lru_w_a': 'new_v', 'new_v_lru_b_a': 'new_v', 'new_v_lru_w_x': 'new_v', 'new_v_lru_b_x': 'new_v', 'new_v_lru_lambda': 'new_v', 'new_v_lru_norm': 'new_v', 'new_v_w_out': 'new_v', 'new_v_post_mix_norm': 'new_v', 'new_v_pre_mlp_norm': 'new_v', 'new_v_w_mlp_in': 'new_v', 'new_v_w_mlp_out': 'new_v', 'new_v_post_mlp_norm': 'new_v'}


def _forward(args):
    return _fwd_reference(*[args[k] for k in FWD_PARAMS])


def _output_shape():
    def fwd():
        inp = _fwd_setup_inputs(0)
        return _fwd_reference(*[inp[k] for k in FWD_PARAMS])
    out = _jax.eval_shape(fwd)
    return out.shape, out.dtype

N_MICROBATCH = 1
ADAM_LR = 0.001
ADAM_B1 = 0.9
ADAM_B2 = 0.999
ADAM_EPS = 1e-08
ADAM_WD = 0.01
ADAM_STEP = 10
PER_EXAMPLE_BATCH_AXIS = {'x': 0, 'loss_target': 0}
SHARED_INPUTS = []
_WEIGHT_DTYPES = {'pre_mix_norm': _jnp.float32, 'w_in': _jnp.float32, 'ssd_conv_w': _jnp.float32, 'ssd_conv_b': _jnp.float32, 'ssd_dt_bias': _jnp.float32, 'ssd_a_log': _jnp.float32, 'ssd_d': _jnp.float32, 'ssd_norm': _jnp.float32, 'lru_conv_w': _jnp.float32, 'lru_conv_b': _jnp.float32, 'lru_w_a': _jnp.float32, 'lru_b_a': _jnp.float32, 'lru_w_x': _jnp.float32, 'lru_b_x': _jnp.float32, 'lru_lambda': _jnp.float32, 'lru_norm': _jnp.float32, 'w_out': _jnp.float32, 'post_mix_norm': _jnp.float32, 'pre_mlp_norm': _jnp.float32, 'w_mlp_in': _jnp.float32, 'w_mlp_out': _jnp.float32, 'post_mlp_norm': _jnp.float32}
MOMENT_SCALE = {'pre_mix_norm': 3.965839e-01, 'w_in': 1.614968e-01, 'ssd_conv_w': 3.958604e-01, 'ssd_conv_b': 1.428390e+00, 'ssd_dt_bias': 5.308270e-01, 'ssd_a_log': 2.105018e+00, 'ssd_d': 1.606735e+00, 'ssd_norm': 8.913899e-01, 'lru_conv_w': 8.240854e-01, 'lru_conv_b': 1.401555e+01, 'lru_w_a': 3.551681e-01, 'lru_b_a': 2.323511e-01, 'lru_w_x': 6.536589e-01, 'lru_b_x': 2.042485e-01, 'lru_lambda': 3.714919e-01, 'lru_norm': 1.037743e+00, 'w_out': 1.316462e+00, 'post_mix_norm': 1.604036e+01, 'pre_mlp_norm': 4.340867e-01, 'w_mlp_in': 2.166453e-01, 'w_mlp_out': 1.365550e+00, 'post_mlp_norm': 1.643034e+01}


def _to_microbatches(a, axis):
    t = _jnp.moveaxis(a, axis, 0)
    t = t.reshape((N_MICROBATCH, t.shape[0] // N_MICROBATCH) + t.shape[1:])
    return _jnp.moveaxis(t, 1, axis + 1)


def setup_inputs(seed: int = 0) -> dict:
    inp = _fwd_setup_inputs(seed)
    key = _jax.random.fold_in(_jax.random.key(seed), 7919)
    shape, _ = _output_shape()
    out = dict(inp)
    out["loss_target"] = _jax.random.normal(_jax.random.fold_in(key, 0), shape, _jnp.float32)
    for i, name in enumerate(TWIN_WEIGHTS):
        w = inp[name].astype(_jnp.float32)
        if MOMENT_SCALE is None:
            s = _jnp.sqrt(_jnp.mean(_jnp.square(w)) + 1e-30)
        else:
            s = MOMENT_SCALE[name]
        km, kv = _jax.random.split(_jax.random.fold_in(key, i + 1))
        out[name] = w
        out["m_" + name] = s * _jax.random.normal(km, w.shape, _jnp.float32)
        out["v_" + name] = (s * s) * _jax.random.uniform(kv, w.shape, _jnp.float32, 0.5, 1.5)
    if N_MICROBATCH > 1:
        for name, axis in PER_EXAMPLE_BATCH_AXIS.items():
            out[name] = _to_microbatches(out[name], axis)
    return {'x': out['x'], 'pre_mix_norm': out['pre_mix_norm'], 'w_in': out['w_in'], 'ssd_conv_w': out['ssd_conv_w'], 'ssd_conv_b': out['ssd_conv_b'], 'ssd_dt_bias': out['ssd_dt_bias'], 'ssd_a_log': out['ssd_a_log'], 'ssd_d': out['ssd_d'], 'ssd_norm': out['ssd_norm'], 'lru_conv_w': out['lru_conv_w'], 'lru_conv_b': out['lru_conv_b'], 'lru_w_a': out['lru_w_a'], 'lru_b_a': out['lru_b_a'], 'lru_w_x': out['lru_w_x'], 'lru_b_x': out['lru_b_x'], 'lru_lambda': out['lru_lambda'], 'lru_norm': out['lru_norm'], 'w_out': out['w_out'], 'post_mix_norm': out['post_mix_norm'], 'pre_mlp_norm': out['pre_mlp_norm'], 'w_mlp_in': out['w_mlp_in'], 'w_mlp_out': out['w_mlp_out'], 'post_mlp_norm': out['post_mlp_norm'], 'loss_target': out['loss_target'], 'm_pre_mix_norm': out['m_pre_mix_norm'], 'm_w_in': out['m_w_in'], 'm_ssd_conv_w': out['m_ssd_conv_w'], 'm_ssd_conv_b': out['m_ssd_conv_b'], 'm_ssd_dt_bias': out['m_ssd_dt_bias'], 'm_ssd_a_log': out['m_ssd_a_log'], 'm_ssd_d': out['m_ssd_d'], 'm_ssd_norm': out['m_ssd_norm'], 'm_lru_conv_w': out['m_lru_conv_w'], 'm_lru_conv_b': out['m_lru_conv_b'], 'm_lru_w_a': out['m_lru_w_a'], 'm_lru_b_a': out['m_lru_b_a'], 'm_lru_w_x': out['m_lru_w_x'], 'm_lru_b_x': out['m_lru_b_x'], 'm_lru_lambda': out['m_lru_lambda'], 'm_lru_norm': out['m_lru_norm'], 'm_w_out': out['m_w_out'], 'm_post_mix_norm': out['m_post_mix_norm'], 'm_pre_mlp_norm': out['m_pre_mlp_norm'], 'm_w_mlp_in': out['m_w_mlp_in'], 'm_w_mlp_out': out['m_w_mlp_out'], 'm_post_mlp_norm': out['m_post_mlp_norm'], 'v_pre_mix_norm': out['v_pre_mix_norm'], 'v_w_in': out['v_w_in'], 'v_ssd_conv_w': out['v_ssd_conv_w'], 'v_ssd_conv_b': out['v_ssd_conv_b'], 'v_ssd_dt_bias': out['v_ssd_dt_bias'], 'v_ssd_a_log': out['v_ssd_a_log'], 'v_ssd_d': out['v_ssd_d'], 'v_ssd_norm': out['v_ssd_norm'], 'v_lru_conv_w': out['v_lru_conv_w'], 'v_lru_conv_b': out['v_lru_conv_b'], 'v_lru_w_a': out['v_lru_w_a'], 'v_lru_b_a': out['v_lru_b_a'], 'v_lru_w_x': out['v_lru_w_x'], 'v_lru_b_x': out['v_lru_b_x'], 'v_lru_lambda': out['v_lru_lambda'], 'v_lru_norm': out['v_lru_norm'], 'v_w_out': out['v_w_out'], 'v_post_mix_norm': out['v_post_mix_norm'], 'v_pre_mlp_norm': out['v_pre_mlp_norm'], 'v_w_mlp_in': out['v_w_mlp_in'], 'v_w_mlp_out': out['v_w_mlp_out'], 'v_post_mlp_norm': out['v_post_mlp_norm']}


def _loss(weights, diff, rest, loss_target):
    with _jax.named_scope("forward"):
        args = {**rest, TWIN_DIFF_INPUT: diff, **{k: w.astype(_WEIGHT_DTYPES[k]) for k, w in weights.items()}}
        y = _forward(args)
    with _jax.named_scope("loss_head"):
        err = _jnp.square(y.astype(_jnp.float32) - loss_target)
        return 0.5 * _jnp.sum(_jnp.mean(err, axis=-1)) if err.ndim else 0.5 * err


def _adamw(w, g, m, v):
    m = ADAM_B1 * m + (1.0 - ADAM_B1) * g
    v = ADAM_B2 * v + (1.0 - ADAM_B2) * _jnp.square(g)
    m_hat = m / (1.0 - ADAM_B1 ** ADAM_STEP)
    v_hat = v / (1.0 - ADAM_B2 ** ADAM_STEP)
    delta = -ADAM_LR * (m_hat / (_jnp.sqrt(v_hat) + ADAM_EPS) + ADAM_WD * w)
    return delta, m, v


def reference(x, pre_mix_norm, w_in, ssd_conv_w, ssd_conv_b, ssd_dt_bias, ssd_a_log, ssd_d, ssd_norm, lru_conv_w, lru_conv_b, lru_w_a, lru_b_a, lru_w_x, lru_b_x, lru_lambda, lru_norm, w_out, post_mix_norm, pre_mlp_norm, w_mlp_in, w_mlp_out, post_mlp_norm, loss_target, m_pre_mix_norm, m_w_in, m_ssd_conv_w, m_ssd_conv_b, m_ssd_dt_bias, m_ssd_a_log, m_ssd_d, m_ssd_norm, m_lru_conv_w, m_lru_conv_b, m_lru_w_a, m_lru_b_a, m_lru_w_x, m_lru_b_x, m_lru_lambda, m_lru_norm, m_w_out, m_post_mix_norm, m_pre_mlp_norm, m_w_mlp_in, m_w_mlp_out, m_post_mlp_norm, v_pre_mix_norm, v_w_in, v_ssd_conv_w, v_ssd_conv_b, v_ssd_dt_bias, v_ssd_a_log, v_ssd_d, v_ssd_norm, v_lru_conv_w, v_lru_conv_b, v_lru_w_a, v_lru_b_a, v_lru_w_x, v_lru_b_x, v_lru_lambda, v_lru_norm, v_w_out, v_post_mix_norm, v_pre_mlp_norm, v_w_mlp_in, v_w_mlp_out, v_post_mlp_norm):
    given = dict(x=x, pre_mix_norm=pre_mix_norm, w_in=w_in, ssd_conv_w=ssd_conv_w, ssd_conv_b=ssd_conv_b, ssd_dt_bias=ssd_dt_bias, ssd_a_log=ssd_a_log, ssd_d=ssd_d, ssd_norm=ssd_norm, lru_conv_w=lru_conv_w, lru_conv_b=lru_conv_b, lru_w_a=lru_w_a, lru_b_a=lru_b_a, lru_w_x=lru_w_x, lru_b_x=lru_b_x, lru_lambda=lru_lambda, lru_norm=lru_norm, w_out=w_out, post_mix_norm=post_mix_norm, pre_mlp_norm=pre_mlp_norm, w_mlp_in=w_mlp_in, w_mlp_out=w_mlp_out, post_mlp_norm=post_mlp_norm, loss_target=loss_target, m_pre_mix_norm=m_pre_mix_norm, m_w_in=m_w_in, m_ssd_conv_w=m_ssd_conv_w, m_ssd_conv_b=m_ssd_conv_b, m_ssd_dt_bias=m_ssd_dt_bias, m_ssd_a_log=m_ssd_a_log, m_ssd_d=m_ssd_d, m_ssd_norm=m_ssd_norm, m_lru_conv_w=m_lru_conv_w, m_lru_conv_b=m_lru_conv_b, m_lru_w_a=m_lru_w_a, m_lru_b_a=m_lru_b_a, m_lru_w_x=m_lru_w_x, m_lru_b_x=m_lru_b_x, m_lru_lambda=m_lru_lambda, m_lru_norm=m_lru_norm, m_w_out=m_w_out, m_post_mix_norm=m_post_mix_norm, m_pre_mlp_norm=m_pre_mlp_norm, m_w_mlp_in=m_w_mlp_in, m_w_mlp_out=m_w_mlp_out, m_post_mlp_norm=m_post_mlp_norm, v_pre_mix_norm=v_pre_mix_norm, v_w_in=v_w_in, v_ssd_conv_w=v_ssd_conv_w, v_ssd_conv_b=v_ssd_conv_b, v_ssd_dt_bias=v_ssd_dt_bias, v_ssd_a_log=v_ssd_a_log, v_ssd_d=v_ssd_d, v_ssd_norm=v_ssd_norm, v_lru_conv_w=v_lru_conv_w, v_lru_conv_b=v_lru_conv_b, v_lru_w_a=v_lru_w_a, v_lru_b_a=v_lru_b_a, v_lru_w_x=v_lru_w_x, v_lru_b_x=v_lru_b_x, v_lru_lambda=v_lru_lambda, v_lru_norm=v_lru_norm, v_w_out=v_w_out, v_post_mix_norm=v_post_mix_norm, v_pre_mlp_norm=v_pre_mlp_norm, v_w_mlp_in=v_w_mlp_in, v_w_mlp_out=v_w_mlp_out, v_post_mlp_norm=v_post_mlp_norm)
    weights = {n: given[n] for n in TWIN_WEIGHTS}
    shared = {n: given[n] for n in SHARED_INPUTS}
    per_example = {n: given[n] for n in ['x']}
    grad_fn = _jax.value_and_grad(_loss, argnums=(0, 1))

    def one_microbatch(ex, loss_target):
        ex = dict(ex)
        diff = ex.pop(TWIN_DIFF_INPUT)
        return grad_fn(weights, diff, {**shared, **ex}, loss_target)

    if N_MICROBATCH == 1:
        loss, (grad_w, grad_x) = one_microbatch(per_example, given["loss_target"])
    else:
        def body(carry, xs):
            loss_sum, grad_sum = carry
            l_k, (gw_k, gx_k) = one_microbatch(xs[0], xs[1])
            with _jax.named_scope("update"):
                return (loss_sum + l_k, _jax.tree.map(_jnp.add, grad_sum, gw_k)), gx_k

        init = (_jnp.zeros((), _jnp.float32), _jax.tree.map(_jnp.zeros_like, weights))
        (loss, grad_w), grad_x = _jax.lax.scan(body, init, (per_example, given["loss_target"]))
    with _jax.named_scope("update"):
        delta_w, new_m, new_v = {}, {}, {}
        for n in TWIN_WEIGHTS:
            delta_w[n], new_m[n], new_v[n] = _adamw(weights[n], grad_w[n], given["m_" + n], given["v_" + n])
    return (loss, grad_x, *[grad_w[n] for n in TWIN_WEIGHTS], *[delta_w[n] for n in TWIN_WEIGHTS],
            *[new_m[n] for n in TWIN_WEIGHTS], *[new_v[n] for n in TWIN_WEIGHTS])
```

```python
import functools

import jax
import jax.numpy as jnp
from jax import lax
from jax.experimental import pallas as pl
from jax.experimental.pallas import tpu as pltpu

F32, BF16 = jnp.float32, jnp.bfloat16
S = jax.ShapeDtypeStruct
MESH = pl.DeviceIdType.MESH

SSD_GROUPS = 8
LRU_C = 8.0
EPS = 1e-6
CONV_WIDTH = 4
ADAM_LR, ADAM_B1, ADAM_B2, ADAM_EPS, ADAM_WD, ADAM_STEP = 0.001, 0.9, 0.999, 1e-08, 0.01, 10

LANES = 128
SUBLANES = 8
VMEM_LIMIT = 56 * 1024 * 1024
N_DEV = 8
SMALL_W = 512
HI = lax.Precision.HIGHEST


def _pcall(body, **kw):
    return pl.pallas_call(body, **kw)


def _cparams(sem=None, **kw):
    return pltpu.CompilerParams(dimension_semantics=sem, vmem_limit_bytes=VMEM_LIMIT, **kw)


def _pick(n, cands):
    for c in cands:
        if c <= n and n % c == 0:
            return c
    return n


def _rt(tr, w, cb=0, n=None):
    if n is None:
        return pl.BlockSpec((tr, w), lambda i: (i, cb))
    return pl.BlockSpec((tr, w), lambda i: (n - 1 - i, cb))


def _halo_prev(tr, w, cb=0, n=None):
    k = tr // SUBLANES
    if n is None:
        return pl.BlockSpec((SUBLANES, w), lambda i: (jnp.maximum(i * k - 1, 0), cb))
    return pl.BlockSpec((SUBLANES, w), lambda i: (jnp.maximum((n - 1 - i) * k - 1, 0), cb))


def _halo_next(tr, w, nt, cb=0, n=None):
    k = tr // SUBLANES
    last = nt * k - 1
    if n is None:
        return pl.BlockSpec((SUBLANES, w), lambda i: (jnp.minimum((i + 1) * k, last), cb))
    return pl.BlockSpec((SUBLANES, w), lambda i: (jnp.minimum((n - i) * k, last), cb))


def _full(a):
    nd = a.ndim
    return pl.BlockSpec(a.shape, lambda i: (0,) * nd)


def _rows_call(name, fn, n_tiles, arrays, in_specs, out_tiled, out_acc, reverse=False):
    n_in, n_t = len(arrays), len(out_tiled)

    def body(*refs):
        i = pl.program_id(0)
        ti = (n_tiles - 1 - i) if reverse else i
        touts, aouts = fn(ti, *[r[...] for r in refs[:n_in]])
        for r, v in zip(refs[n_in:n_in + n_t], touts):
            r[...] = v.astype(r.dtype)
        accs = refs[n_in + n_t:]
        if accs:
            @pl.when(i == 0)
            def _():
                for r in accs:
                    r[...] = jnp.zeros_like(r)
            for r, v in zip(accs, aouts):
                r[...] += v

    out_shape = [S(sh, dt) for sh, dt, _ in out_tiled] + [S(sh, F32) for sh in out_acc]
    out_specs = [sp for _, _, sp in out_tiled]
    for sh in out_acc:
        out_specs.append(pl.BlockSpec(sh, lambda i, nd=len(sh): (0,) * nd))
    return _pcall(body, name=name, grid=(n_tiles,), in_specs=in_specs, out_specs=out_specs,
                  out_shape=out_shape, compiler_params=_cparams(("arbitrary",)))(*arrays)


def _rms(x, g):
    return x * lax.rsqrt(jnp.mean(x * x, axis=-1, keepdims=True) + EPS) * g


def _colsum(v):
    return jnp.sum(v, axis=0, keepdims=True)


_TILE_MN = (1152, 1024, 896, 768, 640, 512, 384, 256, 128)


def _mm(name, a, b, *, ta=False, tb=False, outs=((F32, None),), extra=None, tm=None, tn=None, tk=None):
    M, K = (a.shape[1], a.shape[0]) if ta else a.shape
    N = b.shape[0] if tb else b.shape[1]
    tm = tm or _pick(M, (512, 256, 128))
    tn = tn or _pick(N, _TILE_MN)
    tk = tk or _pick(K, (1024, 896, 768, 640, 512, 384, 256, 128))
    nk = K // tk
    dn = (((0 if ta else 1,), (1 if tb else 0,)), ((), ()))
    n_extra = 0 if extra is None else 1

    def body(*refs):
        a_ref, b_ref = refs[0], refs[1]
        e_ref = refs[2] if n_extra else None
        o_refs = refs[2 + n_extra:2 + n_extra + len(outs)]
        acc = refs[-1]
        k = pl.program_id(2)

        @pl.when(k == 0)
        def _():
            acc[...] = jnp.zeros_like(acc)

        acc[...] += lax.dot_general(a_ref[...].astype(BF16), b_ref[...].astype(BF16), dn,
                                    preferred_element_type=F32)

        @pl.when(k == nk - 1)
        def _():
            r = acc[...]
            e = e_ref[...] if n_extra else None
            for o, (_, f) in zip(o_refs, outs):
                o[...] = (r if f is None else f(r, e)).astype(o.dtype)

    a_spec = pl.BlockSpec((tk, tm), lambda i, j, k: (k, i)) if ta else pl.BlockSpec((tm, tk), lambda i, j, k: (i, k))
    b_spec = pl.BlockSpec((tn, tk), lambda i, j, k: (j, k)) if tb else pl.BlockSpec((tk, tn), lambda i, j, k: (k, j))
    o_spec = pl.BlockSpec((tm, tn), lambda i, j, k: (i, j))
    in_specs = [a_spec, b_spec] + ([o_spec] if n_extra else [])
    args = [a, b] + ([extra] if n_extra else [])
    res = _pcall(body, name=name, grid=(M // tm, N // tn, nk), in_specs=in_specs,
                 out_specs=[o_spec] * len(outs), out_shape=[S((M, N), dt) for dt, _ in outs],
                 scratch_shapes=[pltpu.VMEM((tm, tn), F32)],
                 compiler_params=_cparams(("parallel", "parallel", "arbitrary")))(*args)
    return res


def _shift_down(x, halo, s):
    if s == 0:
        return x
    r = pltpu.roll(x, s, 0)
    hr = pltpu.roll(halo, s, 0)
    row = lax.broadcasted_iota(jnp.int32, halo.shape, 0)
    top = jnp.where(row < s, hr, r[:SUBLANES])
    if x.shape[0] == SUBLANES:
        return top
    return jnp.concatenate([top, r[SUBLANES:]], axis=0)


def _shift_up(x, nxt, s):
    if s == 0:
        return x
    n = x.shape[0]
    r = pltpu.roll(x, n - s, 0)
    nr = pltpu.roll(nxt, SUBLANES - s, 0)
    row = lax.broadcasted_iota(jnp.int32, nxt.shape, 0)
    bot = jnp.where(row >= SUBLANES - s, nr, r[n - SUBLANES:])
    if n == SUBLANES:
        return bot
    return jnp.concatenate([r[:n - SUBLANES], bot], axis=0)


def _conv_pre(x, halo, w, b):
    acc = b + w[CONV_WIDTH - 1:CONV_WIDTH, :] * x
    for k in range(CONV_WIDTH - 1):
        acc = acc + w[k:k + 1, :] * _shift_down(x, halo, CONV_WIDTH - 1 - k)
    return acc


def _silu_grad(p):
    s = jax.nn.sigmoid(p)
    return s * (1.0 + p * (1.0 - s))


def _conv_bwd_tile(i, n_tiles, x, hprev, xnext, d, dnext, w, b, silu):
    hprev = jnp.where(i == 0, 0.0, hprev)
    if silu:
        d = d * _silu_grad(_conv_pre(x, hprev, w, b))
        pre_next = _conv_pre(xnext, x[x.shape[0] - SUBLANES:], w, b)
        dnext = dnext * _silu_grad(pre_next)
    dnext = jnp.where(i == n_tiles - 1, 0.0, dnext)
    dx = w[CONV_WIDTH - 1:CONV_WIDTH, :] * d
    row8 = lax.broadcasted_iota(jnp.int32, (SUBLANES, x.shape[1]), 0)
    dw8 = jnp.where(row8 == CONV_WIDTH - 1, _colsum(d * x), 0.0)
    for k in range(CONV_WIDTH - 1):
        s = CONV_WIDTH - 1 - k
        dx = dx + w[k:k + 1, :] * _shift_up(d, dnext, s)
        dw8 = dw8 + jnp.where(row8 == k, _colsum(d * _shift_down(x, hprev, s)), 0.0)
    return dx, dw8, _colsum(d)


def _ssd_dims(xbc_act, n_heads):
    T, XBC = xbc_act.shape
    GN = XBC // 4
    DS = XBC - 2 * GN
    G = SSD_GROUPS
    N = GN // G
    P = DS // n_heads
    K = n_heads // G
    return T, XBC, DS, GN, G, N, P, K


def _ssd_common(dt, alog, Q):
    a = -jnp.exp(alog)
    adt = dt * a
    li = lax.broadcasted_iota(jnp.int32, (Q, Q), 0)
    si = lax.broadcasted_iota(jnp.int32, (Q, Q), 1)
    causal = li >= si
    ltri = causal.astype(F32)
    acs = jnp.dot(ltri, adt, precision=HI, preferred_element_type=F32)
    acs_row = lax.dot_general(adt, ltri, (((0,), (1,)), ((), ())), precision=HI,
                              preferred_element_type=F32)
    return a, adt, causal, ltri, acs, acs_row


def _expander(g, K, P, W):
    r = lax.broadcasted_iota(jnp.int32, (LANES, W), 0)
    c = lax.broadcasted_iota(jnp.int32, (LANES, W), 1)
    return (c // P + g * K == r).astype(F32)


def _dotb(a, b, dn=(((1,), (0,)), ((), ()))):
    return lax.dot_general(a.astype(BF16), b.astype(BF16), dn, preferred_element_type=F32)


_NT = (((1,), (1,)), ((), ()))
_TN = (((0,), (0,)), ((), ()))


def _ssd_fwd(xbc_act, dt, alog, dskip, n_heads, Q):
    T, XBC, DS, GN, G, N, P, K = _ssd_dims(xbc_act, n_heads)
    W = K * P
    nc = T // Q

    def body(xs_ref, b_ref, c_ref, dt_ref, alog_ref, d_ref, y_ref, hp_ref, h_scr):
        ci = pl.program_id(0)

        @pl.when(ci == 0)
        def _():
            h_scr[...] = jnp.zeros_like(h_scr)

        dtv = dt_ref[...]
        a, adt, causal, ltri, acs, acs_row = _ssd_common(dtv, alog_ref[...], Q)
        lane_head = lax.broadcasted_iota(jnp.int32, (Q, W), 1) // P
        for g in range(G):
            eg = _expander(g, K, P, W)
            dtb = jnp.dot(dtv, eg, precision=HI, preferred_element_type=F32)
            acsb = jnp.dot(acs, eg, precision=HI, preferred_element_type=F32)
            lastb = acsb[Q - 1:Q, :]
            db = jnp.dot(jnp.broadcast_to(d_ref[...], (SUBLANES, LANES)), eg, precision=HI,
                         preferred_element_type=F32)[0:1, :]
            xg = xs_ref[:, g * W:(g + 1) * W]
            bg = b_ref[:, g * N:(g + 1) * N]
            cg = c_ref[:, g * N:(g + 1) * N]
            xt = xg * dtb
            sc = _dotb(cg, bg, _NT)
            yd = jnp.zeros((Q, W), F32)
            for k in range(K):
                h = g * K + k
                seg = acs[:, h:h + 1] - acs_row[h:h + 1, :]
                lh = jnp.where(causal, jnp.exp(jnp.minimum(seg, 0.0)), 0.0)
                xk = jnp.where(lane_head == k, xt, 0.0)
                yd = yd + _dotb(sc * lh, xk)
            hp = h_scr[g]
            yoff = _dotb(cg, hp) * jnp.exp(acsb)
            y_ref[:, g * W:(g + 1) * W] = yd + yoff + xg * db
            e_end = jnp.exp(lastb - acsb)
            st = _dotb(bg, xt * e_end, _TN)
            hp_ref[0, g] = hp
            h_scr[g] = jnp.exp(lastb) * hp + st

    cb = DS // GN
    in_specs = [pl.BlockSpec((Q, DS), lambda c: (c, 0)),
                pl.BlockSpec((Q, GN), lambda c: (c, cb)),
                pl.BlockSpec((Q, GN), lambda c: (c, cb + 1)),
                pl.BlockSpec((Q, LANES), lambda c: (c, 0)),
                pl.BlockSpec((1, LANES), lambda c: (0, 0)),
                pl.BlockSpec((1, LANES), lambda c: (0, 0))]
    out_specs = [pl.BlockSpec((Q, DS), lambda c: (c, 0)),
                 pl.BlockSpec((1, G, N, W), lambda c: (c, 0, 0, 0))]
    return _pcall(body, name="ssd_fwd", grid=(nc,), in_specs=in_specs, out_specs=out_specs,
                  out_shape=[S((T, DS), F32), S((nc, G, N, W), F32)],
                  scratch_shapes=[pltpu.VMEM((G, N, W), F32)],
                  compiler_params=_cparams(("arbitrary",)))(xbc_act, xbc_act, xbc_act, dt, alog, dskip)


def _ssd_bwd(xbc_act, dt, alog, dskip, hprev, dy, n_heads, Q):
    T, XBC, DS, GN, G, N, P, K = _ssd_dims(xbc_act, n_heads)
    W = K * P
    nc = T // Q

    def body(xs_ref, b_ref, c_ref, dt_ref, alog_ref, d_ref, hp_ref, dy_ref,
             dxbc_ref, ddt_ref, dalog_ref, dd_ref, dh_scr):
        ci = pl.program_id(0)

        @pl.when(ci == 0)
        def _():
            dh_scr[...] = jnp.zeros_like(dh_scr)
            dalog_ref[...] = jnp.zeros_like(dalog_ref)
            dd_ref[...] = jnp.zeros_like(dd_ref)

        dtv = dt_ref[...]
        a, adt, causal, ltri, acs, acs_row = _ssd_common(dtv, alog_ref[...], Q)
        lane_head = lax.broadcasted_iota(jnp.int32, (Q, W), 1) // P
        lane128 = lax.broadcasted_iota(jnp.int32, (Q, LANES), 1)
        sub128 = lax.broadcasted_iota(jnp.int32, (LANES, Q), 0)
        rowq = lax.broadcasted_iota(jnp.int32, (Q, W), 0)
        dacs = jnp.zeros((Q, LANES), F32)
        dacs_row = jnp.zeros((LANES, Q), F32)
        ddt = jnp.zeros((Q, LANES), F32)
        dd_acc = jnp.zeros((1, LANES), F32)
        for g in range(G):
            eg = _expander(g, K, P, W)
            dtb = jnp.dot(dtv, eg, precision=HI, preferred_element_type=F32)
            acsb = jnp.dot(acs, eg, precision=HI, preferred_element_type=F32)
            lastb = acsb[Q - 1:Q, :]
            db = jnp.dot(jnp.broadcast_to(d_ref[...], (SUBLANES, LANES)), eg, precision=HI,
                         preferred_element_type=F32)[0:1, :]
            xg = xs_ref[:, g * W:(g + 1) * W]
            bg = b_ref[:, g * N:(g + 1) * N]
            cg = c_ref[:, g * N:(g + 1) * N]
            dyg = dy_ref[:, g * W:(g + 1) * W]
            hp = hp_ref[0, g]
            dhn = dh_scr[g]
            xt = xg * dtb
            sc = _dotb(cg, bg, _NT)
            eacs = jnp.exp(acsb)
            e_end = jnp.exp(lastb - acsb)
            elast = jnp.exp(lastb)

            wv = dyg * eacs
            dcg = _dotb(wv, hp, _NT)
            dhp = _dotb(cg, wv, _TN) + elast * dhn
            dacsb = dyg * (_dotb(cg, hp) * eacs)

            xe = xt * e_end
            dbg = _dotb(xe, dhn, _NT)
            v = _dotb(bg, dhn)
            dxt = v * e_end
            de = v * xe
            dacsb = dacsb - de
            dlastb = _colsum(de) + elast * jnp.sum(dhn * hp, axis=0, keepdims=True)

            dsc = jnp.zeros((Q, Q), F32)
            for k in range(K):
                h = g * K + k
                seg = acs[:, h:h + 1] - acs_row[h:h + 1, :]
                lh = jnp.where(causal, jnp.exp(jnp.minimum(seg, 0.0)), 0.0)
                mh = sc * lh
                dyk = jnp.where(lane_head == k, dyg, 0.0)
                dxt = dxt + jnp.where(lane_head == k, _dotb(mh, dyg, _TN), 0.0)
                dm = _dotb(dyk, xt, _NT)
                dsc = dsc + dm * lh
                gm = dm * mh
                dacs = dacs + jnp.where(lane128 == h, jnp.sum(gm, axis=1, keepdims=True), 0.0)
                dacs_row = dacs_row - jnp.where(sub128 == h, jnp.sum(gm, axis=0, keepdims=True), 0.0)
            dcg = dcg + _dotb(dsc, bg)
            dbg = dbg + _dotb(dsc, cg, _TN)

            dacsb = dacsb + jnp.where(rowq == Q - 1, dlastb, 0.0)
            dacs = dacs + lax.dot_general(dacsb, eg, _NT, precision=HI, preferred_element_type=F32)
            ddt = ddt + lax.dot_general(dxt * xg, eg, _NT, precision=HI, preferred_element_type=F32)
            dd_acc = dd_acc + lax.dot_general(jnp.broadcast_to(_colsum(dyg * xg), (SUBLANES, W)), eg, _NT,
                                              precision=HI, preferred_element_type=F32)[0:1, :]
            dxbc_ref[:, g * W:(g + 1) * W] = dxt * dtb + dyg * db
            dxbc_ref[:, DS + g * N:DS + (g + 1) * N] = dbg
            dxbc_ref[:, DS + GN + g * N:DS + GN + (g + 1) * N] = dcg
            dh_scr[g] = dhp

        eye = (lax.broadcasted_iota(jnp.int32, (LANES, LANES), 0) ==
               lax.broadcasted_iota(jnp.int32, (LANES, LANES), 1)).astype(F32)
        dacs = dacs + lax.dot_general(dacs_row, eye, _TN, precision=HI, preferred_element_type=F32)
        dadt = lax.dot_general(ltri, dacs, _TN, precision=HI, preferred_element_type=F32)
        ddt_ref[...] = ddt + dadt * a
        dalog_ref[...] += _colsum(dadt * dtv) * a
        dd_ref[...] += dd_acc

    cb = DS // GN
    rv = lambda c: nc - 1 - c
    in_specs = [pl.BlockSpec((Q, DS), lambda c: (rv(c), 0)),
                pl.BlockSpec((Q, GN), lambda c: (rv(c), cb)),
                pl.BlockSpec((Q, GN), lambda c: (rv(c), cb + 1)),
                pl.BlockSpec((Q, LANES), lambda c: (rv(c), 0)),
                pl.BlockSpec((1, LANES), lambda c: (0, 0)),
                pl.BlockSpec((1, LANES), lambda c: (0, 0)),
                pl.BlockSpec((1, G, N, W), lambda c: (rv(c), 0, 0, 0)),
                pl.BlockSpec((Q, DS), lambda c: (rv(c), 0))]
    out_specs = [pl.BlockSpec((Q, XBC), lambda c: (rv(c), 0)),
                 pl.BlockSpec((Q, LANES), lambda c: (rv(c), 0)),
                 pl.BlockSpec((1, LANES), lambda c: (0, 0)),
                 pl.BlockSpec((1, LANES), lambda c: (0, 0))]
    return _pcall(body, name="ssd_bwd", grid=(nc,), in_specs=in_specs, out_specs=out_specs,
                  out_shape=[S((T, XBC), F32), S((T, LANES), F32), S((1, LANES), F32), S((1, LANES), F32)],
                  scratch_shapes=[pltpu.VMEM((G, N, W), F32)],
                  compiler_params=_cparams(("arbitrary",)))(
                      xbc_act, xbc_act, xbc_act, dt, alog, dskip, hprev, dy)


def _blockdiag(x, w_ref, dn=(((1,), (0,)), ((), ()))):
    H, B, _ = w_ref.shape
    return jnp.concatenate([_dotb(x[:, h * B:(h + 1) * B], w_ref[h], dn) for h in range(H)], axis=1)


def _lru_elem(xl, r_pre, i_pre, lam):
    r = jax.nn.sigmoid(r_pre)
    i = jax.nn.sigmoid(i_pre)
    log_a = -LRU_C * r * jax.nn.softplus(-lam)
    a = jnp.exp(log_a)
    u = jnp.sqrt(1.0 - jnp.exp(2.0 * log_a)) * (i * xl)
    return a, u


def _lru_gates_fwd(xl, w_a, b_a, w_x, b_x, lam, tr):
    T, DL = xl.shape

    def body(xl_ref, wa_ref, ba_ref, wx_ref, bx_ref, lam_ref, a_ref, u_ref):
        x = xl_ref[...]
        r_pre = _blockdiag(x, wa_ref) + ba_ref[...]
        i_pre = _blockdiag(x, wx_ref) + bx_ref[...]
        a, u = _lru_elem(x, r_pre, i_pre, lam_ref[...])
        a_ref[...] = a
        u_ref[...] = u

    w3 = pl.BlockSpec(w_a.shape, lambda i: (0, 0, 0))
    vec = pl.BlockSpec((1, DL), lambda i: (0, 0))
    return _pcall(body, name="lru_gates_fwd", grid=(T // tr,),
                  in_specs=[_rt(tr, DL), w3, vec, w3, vec, vec],
                  out_specs=[_rt(tr, DL), _rt(tr, DL)], out_shape=[S((T, DL), F32), S((T, DL), F32)],
                  compiler_params=_cparams(("parallel",)))(xl, w_a, b_a, w_x, b_x, lam)


def _lru_gates_bwd(xl, w_a, b_a, w_x, b_x, lam, da, du, tr):
    T, DL = xl.shape
    H, B, _ = w_a.shape

    def body(xl_ref, wa_ref, ba_ref, wx_ref, bx_ref, lam_ref, da_ref, du_ref,
             dxl_ref, dwa_ref, dba_ref, dwx_ref, dbx_ref, dlam_ref):
        @pl.when(pl.program_id(0) == 0)
        def _():
            for r in (dwa_ref, dba_ref, dwx_ref, dbx_ref, dlam_ref):
                r[...] = jnp.zeros_like(r)

        x = xl_ref[...]
        r_pre = _blockdiag(x, wa_ref) + ba_ref[...]
        i_pre = _blockdiag(x, wx_ref) + bx_ref[...]
        _, vjp = jax.vjp(_lru_elem, x, r_pre, i_pre, lam_ref[...])
        dx, dr, di, dlam = vjp((da_ref[...], du_ref[...]))
        dxl_ref[...] = dx + _blockdiag(dr, wa_ref, _NT) + _blockdiag(di, wx_ref, _NT)
        for h in range(H):
            xh = x[:, h * B:(h + 1) * B]
            dwa_ref[h] += _dotb(xh, dr[:, h * B:(h + 1) * B], _TN)
            dwx_ref[h] += _dotb(xh, di[:, h * B:(h + 1) * B], _TN)
        dba_ref[...] += _colsum(dr)
        dbx_ref[...] += _colsum(di)
        dlam_ref[...] += dlam

    w3 = pl.BlockSpec(w_a.shape, lambda i: (0, 0, 0))
    vec = pl.BlockSpec((1, DL), lambda i: (0, 0))
    return _pcall(body, name="lru_gates_bwd", grid=(T // tr,),
                  in_specs=[_rt(tr, DL), w3, vec, w3, vec, vec, _rt(tr, DL), _rt(tr, DL)],
                  out_specs=[_rt(tr, DL), w3, vec, w3, vec, vec],
                  out_shape=[S((T, DL), F32), S(w_a.shape, F32), S((1, DL), F32), S(w_a.shape, F32),
                             S((1, DL), F32), S((1, DL), F32)],
                  compiler_params=_cparams(("arbitrary",)))(xl, w_a, b_a, w_x, b_x, lam, da, du)


def _scan_tile(a, u, up):
    n = a.shape[0]
    row = lax.broadcasted_iota(jnp.int32, a.shape, 0)
    d = 1
    while d < n:
        if up:
            keep = row < n - d
            a_s = jnp.where(keep, pltpu.roll(a, n - d, 0), 1.0)
            u_s = jnp.where(keep, pltpu.roll(u, n - d, 0), 0.0)
        else:
            keep = row >= d
            a_s = jnp.where(keep, pltpu.roll(a, d, 0), 1.0)
            u_s = jnp.where(keep, pltpu.roll(u, d, 0), 0.0)
        u = a * u_s + u
        a = a * a_s
        d *= 2
    return a, u


def _lru_scan_fwd(a, u, tr):
    T, DL = a.shape

    def body(a_ref, u_ref, h_ref, carry):
        @pl.when(pl.program_id(0) == 0)
        def _():
            carry[...] = jnp.zeros_like(carry)

        A, U = _scan_tile(a_ref[...], u_ref[...], up=False)
        h = U + A * carry[0:1, :]
        h_ref[...] = h
        carry[...] = jnp.broadcast_to(h[tr - 1:tr, :], carry.shape)

    return _pcall(body, name="lru_scan_fwd", grid=(T // tr,), in_specs=[_rt(tr, DL), _rt(tr, DL)],
                  out_specs=_rt(tr, DL), out_shape=S((T, DL), F32),
                  scratch_shapes=[pltpu.VMEM((SUBLANES, DL), F32)],
                  compiler_params=_cparams(("arbitrary",)))(a, u)


def _lru_scan_bwd(a, h, dh, tr):
    T, DL = a.shape
    n = T // tr

    def body(a_ref, an_ref, h_ref, hp_ref, dh_ref, du_ref, da_ref, carry):
        i = pl.program_id(0)
        ti = n - 1 - i

        @pl.when(i == 0)
        def _():
            carry[...] = jnp.zeros_like(carry)

        a_next = _shift_up(a_ref[...], jnp.where(ti == n - 1, 0.0, an_ref[...]), 1)
        A, U = _scan_tile(a_next, dh_ref[...], up=True)
        g = U + A * carry[0:1, :]
        du_ref[...] = g
        h_prev = _shift_down(h_ref[...], jnp.where(ti == 0, 0.0, hp_ref[...]), 1)
        da_ref[...] = g * h_prev
        carry[...] = jnp.broadcast_to(g[0:1, :], carry.shape)

    return _pcall(body, name="lru_scan_bwd", grid=(n,),
                  in_specs=[_rt(tr, DL, 0, n), _halo_next(tr, DL, n, 0, n), _rt(tr, DL, 0, n),
                            _halo_prev(tr, DL, 0, n), _rt(tr, DL, 0, n)],
                  out_specs=[_rt(tr, DL, 0, n), _rt(tr, DL, 0, n)],
                  out_shape=[S((T, DL), F32), S((T, DL), F32)],
                  scratch_shapes=[pltpu.VMEM((SUBLANES, DL), F32)],
                  compiler_params=_cparams(("arbitrary",)))(a, a, h, h, dh)


def _adamw(w, g, m, v):
    m = ADAM_B1 * m + (1.0 - ADAM_B1) * g
    v = ADAM_B2 * v + (1.0 - ADAM_B2) * (g * g)
    m_hat = m / (1.0 - ADAM_B1 ** ADAM_STEP)
    v_hat = v / (1.0 - ADAM_B2 ** ADAM_STEP)
    delta = -ADAM_LR * (m_hat / (jnp.sqrt(v_hat) + ADAM_EPS) + ADAM_WD * w)
    return delta, m, v


def _adamw_big(name, w, m, v, part, recv, chip_idx):
    R, C = w.shape
    tr = _pick(R, (256, 128, 64, 32, 16))

    def body(ci_ref, w_ref, m_ref, v_ref, p_ref, r0, r1, r2, g_ref, d_ref, nm_ref, nv_ref):
        g = ((p_ref[...].astype(F32) + r0[...].astype(F32)) + r1[...].astype(F32)) + r2[...].astype(F32)
        d, nm, nv = _adamw(w_ref[...], g, m_ref[...], v_ref[...])
        g_ref[...] = g
        d_ref[...] = d
        nm_ref[...] = nm
        nv_ref[...] = nv

    t2 = pl.BlockSpec((tr, C), lambda i, ci: (i, 0))
    r_spec = lambda s: pl.BlockSpec((None, tr, C), lambda i, ci: (s, i, 0))
    gs = pltpu.PrefetchScalarGridSpec(
        num_scalar_prefetch=1, grid=(R // tr,),
        in_specs=[t2, t2, t2, pl.BlockSpec((None, tr, C), lambda i, ci: (ci[0], i, 0)),
                  r_spec(0), r_spec(1), r_spec(2)],
        out_specs=[t2, t2, t2, t2])
    return _pcall(body, name=name, grid_spec=gs, out_shape=[S((R, C), F32)] * 4,
                  compiler_params=_cparams(("parallel",)))(chip_idx, w, m, v, part, recv, recv, recv)


def _adamw_small(w, g, m, v):
    R, C = w.shape

    def body(w_ref, g_ref, m_ref, v_ref, d_ref, nm_ref, nv_ref):
        d, nm, nv = _adamw(w_ref[...], g_ref[...], m_ref[...], v_ref[...])
        d_ref[...] = d
        nm_ref[...] = nm
        nv_ref[...] = nv

    return _pcall(body, name="adamw_small", out_shape=[S((R, C), F32)] * 3,
                  compiler_params=_cparams())(w, g, m, v)


def _sum8(parts):
    _, R, C = parts.shape

    def body(p_ref, o_ref):
        acc = p_ref[0]
        for k in range(1, N_DEV):
            acc = acc + p_ref[k]
        o_ref[...] = acc

    return _pcall(body, name="sum_small_grads", out_shape=S((R, C), F32), compiler_params=_cparams())(parts)


def _pair_sum(name, full, recv, c_idx):
    _, R, C = full.shape
    tr = _pick(R, (256, 128, 64, 32, 16))

    def body(c_ref, f_ref, r_ref, o_ref):
        o_ref[...] = (f_ref[...].astype(F32) + r_ref[...].astype(F32)).astype(o_ref.dtype)

    gs = pltpu.PrefetchScalarGridSpec(
        num_scalar_prefetch=1, grid=(4, R // tr),
        in_specs=[pl.BlockSpec((None, tr, C), lambda j, i, c: (2 * j + c[0], i, 0)),
                  pl.BlockSpec((None, tr, C), lambda j, i, c: (j, i, 0))],
        out_specs=pl.BlockSpec((None, tr, C), lambda j, i, c: (j, i, 0)))
    return _pcall(body, name=name, grid_spec=gs, out_shape=S((4, R, C), BF16),
                  compiler_params=_cparams(("parallel", "parallel")))(c_idx, full, recv)


def _cast_bf16(name, w):
    R, C = w.shape
    tr = _pick(R, (256, 128, 64, 32, 16))

    def body(w_ref, o_ref):
        o_ref[...] = w_ref[...].astype(BF16)

    sp = pl.BlockSpec((tr, C), lambda i: (i, 0))
    return _pcall(body, name=name, grid=(R // tr,), in_specs=[sp], out_specs=sp, out_shape=S((R, C), BF16),
                  compiler_params=_cparams(("parallel",)))(w)


_ANY = pl.BlockSpec(memory_space=pl.ANY)


def _position():
    return lax.axis_index("x"), lax.axis_index("y"), lax.axis_index("c")


def _allgather(name, shards):
    n = len(shards)

    def body(*refs):
        ins, outs = refs[:n], refs[n:2 * n]
        send, recv, lsem = refs[2 * n:]
        x, y, c = _position()
        me, sib = (x, y, c), (x, y, 1 - c)
        chips = [(1 - x, y), (x, 1 - y), (1 - x, 1 - y)]

        def copy(a, k, block, to, src=None):
            bx, by, bc = block
            dst = outs[a].at[4 * bx + 2 * by + bc]
            return pltpu.make_async_remote_copy(
                src_ref=dst if src is None else src, dst_ref=dst, send_sem=send.at[a, k], recv_sem=recv.at[a, k],
                device_id=to, device_id_type=MESH)

        mine = [pltpu.make_async_copy(ins[a], outs[a].at[4 * x + 2 * y + c], lsem.at[a]) for a in range(n)]
        for cp in mine:
            cp.start()
        first = []
        for a in range(n):
            first.append(copy(a, 0, me, sib, src=ins[a]))
            first += [copy(a, 1 + j, me, (*chip, c), src=ins[a]) for j, chip in enumerate(chips)]
        for cp in first:
            cp.start()
        passed = []
        for j, chip in enumerate(chips):
            for a in range(n):
                copy(a, 1 + j, (*chip, c), me).wait_recv()
                cp = copy(a, 4 + j, (*chip, c), sib)
                cp.start()
                passed.append(cp)
        for a in range(n):
            copy(a, 0, sib, me).wait_recv()
        for j, chip in enumerate(chips):
            for a in range(n):
                copy(a, 4 + j, (*chip, 1 - c), me).wait_recv()
        for cp in first + passed:
            cp.wait_send()
        for cp in mine:
            cp.wait()

    return _pcall(body, name=name, in_specs=[_ANY] * n, out_specs=[_ANY] * n,
                  out_shape=[S((N_DEV,) + s.shape, s.dtype) for s in shards],
                  scratch_shapes=[pltpu.SemaphoreType.DMA((n, 7)), pltpu.SemaphoreType.DMA((n, 7)),
                                  pltpu.SemaphoreType.DMA((n,))])(*shards)


def _rs_sibling(name, fulls):
    n = len(fulls)

    def body(*refs):
        ins, outs = refs[:n], refs[n:2 * n]
        send, recv = refs[2 * n:]
        x, y, c = _position()
        copies = []
        for a in range(n):
            for j in range(4):
                copies.append(pltpu.make_async_remote_copy(
                    src_ref=ins[a].at[2 * j + (1 - c)], dst_ref=outs[a].at[j], send_sem=send.at[a, j],
                    recv_sem=recv.at[a, j], device_id=(x, y, 1 - c), device_id_type=MESH))
        for cp in copies:
            cp.start()
        for cp in copies:
            cp.wait()

    return _pcall(body, name=name, in_specs=[_ANY] * n, out_specs=[_ANY] * n,
                  out_shape=[S((4,) + f.shape[1:], f.dtype) for f in fulls],
                  scratch_shapes=[pltpu.SemaphoreType.DMA((n, 4)), pltpu.SemaphoreType.DMA((n, 4))])(*fulls)


def _rs_chips(name, parts):
    n = len(parts)

    def body(*refs):
        ins, outs = refs[:n], refs[n:2 * n]
        send, recv = refs[2 * n:]
        x, y, c = _position()
        chips = [(1 - x, y), (x, 1 - y), (1 - x, 1 - y)]
        copies = []
        for a in range(n):
            for j, (px, py) in enumerate(chips):
                copies.append(pltpu.make_async_remote_copy(
                    src_ref=ins[a].at[2 * px + py], dst_ref=outs[a].at[j], send_sem=send.at[a, j],
                    recv_sem=recv.at[a, j], device_id=(px, py, c), device_id_type=MESH))
        for cp in copies:
            cp.start()
        for cp in copies:
            cp.wait()

    return _pcall(body, name=name, in_specs=[_ANY] * n, out_specs=[_ANY] * n,
                  out_shape=[S((3,) + p.shape[1:], p.dtype) for p in parts],
                  scratch_shapes=[pltpu.SemaphoreType.DMA((n, 3)), pltpu.SemaphoreType.DMA((n, 3))])(*parts)


def _pad_lanes(v):
    return jnp.pad(v, ((0, 0), (0, LANES - v.shape[1])))


def _flat_rows(pieces):
    flat = jnp.concatenate([p.reshape(-1) for p in pieces])
    rows = -(-flat.shape[0] // (SMALL_W * SUBLANES)) * SUBLANES
    return jnp.pad(flat, (0, rows * SMALL_W - flat.shape[0])).reshape(rows, SMALL_W)


def _unflat(buf, shapes):
    flat = buf.reshape(-1)
    out, off = [], 0
    for sh in shapes:
        n = 1
        for d in sh:
            n *= d
        out.append(flat[off:off + n].reshape(sh))
        off += n
    return out


def kernel(x, pre_mix_norm, w_in, ssd_conv_w, ssd_conv_b, ssd_dt_bias, ssd_a_log, ssd_d, ssd_norm, lru_conv_w, lru_conv_b, lru_w_a, lru_b_a, lru_w_x, lru_b_x, lru_lambda, lru_norm, w_out, post_mix_norm, pre_mlp_norm, w_mlp_in, w_mlp_out, post_mlp_norm, loss_target, m_pre_mix_norm, m_w_in, m_ssd_conv_w, m_ssd_conv_b, m_ssd_dt_bias, m_ssd_a_log, m_ssd_d, m_ssd_norm, m_lru_conv_w, m_lru_conv_b, m_lru_w_a, m_lru_b_a, m_lru_w_x, m_lru_b_x, m_lru_lambda, m_lru_norm, m_w_out, m_post_mix_norm, m_pre_mlp_norm, m_w_mlp_in, m_w_mlp_out, m_post_mlp_norm, v_pre_mix_norm, v_w_in, v_ssd_conv_w, v_ssd_conv_b, v_ssd_dt_bias, v_ssd_a_log, v_ssd_d, v_ssd_norm, v_lru_conv_w, v_lru_conv_b, v_lru_w_a, v_lru_b_a, v_lru_w_x, v_lru_b_x, v_lru_lambda, v_lru_norm, v_w_out, v_post_mix_norm, v_pre_mlp_norm, v_w_mlp_in, v_w_mlp_out, v_post_mlp_norm):
    names = ['pre_mix_norm', 'w_in', 'ssd_conv_w', 'ssd_conv_b', 'ssd_dt_bias', 'ssd_a_log', 'ssd_d', 'ssd_norm',
             'lru_conv_w', 'lru_conv_b', 'lru_w_a', 'lru_b_a', 'lru_w_x', 'lru_b_x', 'lru_lambda', 'lru_norm',
             'w_out', 'post_mix_norm', 'pre_mlp_norm', 'w_mlp_in', 'w_mlp_out', 'post_mlp_norm']
    loc = locals()
    W = {n: loc[n] for n in names}
    Mo = {n: loc["m_" + n] for n in names}
    Vo = {n: loc["v_" + n] for n in names}
    big = ['w_in', 'w_out', 'w_mlp_in', 'w_mlp_out']

    px, py, pc = _position()
    dev = 4 * px + 2 * py + pc
    c_idx = jnp.reshape(pc, (1,)).astype(jnp.int32)
    chip_idx = jnp.reshape(2 * px + py, (1,)).astype(jnp.int32)

    _, T, D = x.shape
    x2 = x.reshape(T, D)
    tgt = loss_target.reshape(T, D)
    n_heads = ssd_dt_bias.shape[1]
    XBC = ssd_conv_b.shape[1]
    GN = XBC // 4
    DS = XBC - 2 * GN
    DL = lru_norm.shape[1]
    DFF = w_mlp_in.shape[2] * N_DEV
    DIN = w_in.shape[2] * N_DEV
    NP = XBC + DS + 2 * DL + LANES
    assert DS % GN == 0 and XBC % DS == 0 and DS == DL and n_heads <= LANES
    cb_z, cb_gate, cb_xl, cb_dt = XBC // DS, XBC // DS + 1, XBC // DS + 2, (XBC + DS + 2 * DL) // LANES
    tr = min(256, T // 2)
    trw = min(128, T // 2)
    nt, ntw = T // tr, T // trw
    Q = min(128, T // 2)

    sh = {n: _cast_bf16("cast_" + n, W[n][0]) for n in big}
    g_in, g_out, g_mi, g_mo, g_cs, g_cl = _allgather(
        "allgather_weights", [sh[n] for n in big] + [ssd_conv_w[0], lru_conv_w[0]])
    conv_s = jnp.transpose(g_cs, (1, 0, 2)).reshape(CONV_WIDTH, XBC)
    conv_l = jnp.transpose(g_cl, (1, 0, 2)).reshape(CONV_WIDTH, DL)
    w_in_full = jnp.transpose(g_in, (1, 0, 2)).reshape(D, DIN)
    o_z, o_xbc, o_dt, o_gate, o_xl = 0, DS, DS + XBC, DS + XBC + n_heads, DS + XBC + n_heads + DL
    wp = jnp.concatenate([w_in_full[:, o_xbc:o_xbc + XBC], w_in_full[:, o_z:o_z + DS],
                          w_in_full[:, o_gate:o_gate + DL], w_in_full[:, o_xl:o_xl + DL],
                          _pad_lanes(w_in_full[:, o_dt:o_dt + n_heads])], axis=1)
    w_out_f = g_out.reshape(DS + DL, D)
    w_mi_f = jnp.transpose(g_mi, (1, 0, 2)).reshape(D, DFF)
    w_mo_f = g_mo.reshape(DFF, D)

    dt_bias = _pad_lanes(ssd_dt_bias)
    a_log = _pad_lanes(ssd_a_log)
    d_skip = _pad_lanes(ssd_d)
    wa_b, wx_b = lru_w_a[0].astype(BF16), lru_w_x[0].astype(BF16)
    b_a, b_x = lru_b_a.reshape(1, DL), lru_b_x.reshape(1, DL)

    def f_norm_in(i, xv, g):
        return (_rms(xv, g),), ()
    (h,) = _rows_call("norm_in", f_norm_in, nt, [x2, pre_mix_norm], [_rt(tr, D), _full(pre_mix_norm)],
                      [((T, D), BF16, _rt(tr, D))], [])

    (proj,) = _mm("proj", h, wp)

    def f_ssd_pre(i, xbc, halo, dtr, w, b, dtb):
        pre = _conv_pre(xbc, jnp.where(i == 0, 0.0, halo), w, b)
        return (pre * jax.nn.sigmoid(pre), jax.nn.softplus(dtr + dtb)), ()
    xbc_act, dt = _rows_call(
        "ssd_pre", f_ssd_pre, ntw, [proj, proj, proj, conv_s, ssd_conv_b, dt_bias],
        [_rt(trw, XBC), _halo_prev(trw, XBC), _rt(trw, LANES, cb_dt), _full(conv_s), _full(ssd_conv_b), _full(dt_bias)],
        [((T, XBC), F32, _rt(trw, XBC)), ((T, LANES), F32, _rt(trw, LANES))], [])

    y_ssd, h_prev = _ssd_fwd(xbc_act, dt, a_log, d_skip, n_heads, Q)

    gw = DS // SSD_GROUPS

    def ssd_post(y, z, g):
        yz = y * jax.nn.silu(z)
        parts = []
        for k in range(SSD_GROUPS):
            yk = yz[:, k * gw:(k + 1) * gw]
            parts.append(yk * lax.rsqrt(jnp.mean(yk * yk, axis=-1, keepdims=True) + EPS))
        return jnp.concatenate(parts, axis=-1) * g

    def f_ssd_post(i, y, z, g):
        return (ssd_post(y, z, g),), ()
    (y_ssd_n,) = _rows_call("ssd_post", f_ssd_post, nt, [y_ssd, proj, ssd_norm],
                            [_rt(tr, DS), _rt(tr, DS, cb_z), _full(ssd_norm)], [((T, DS), BF16, _rt(tr, DS))], [])

    def f_lru_pre(i, xv, halo, w, b):
        return (_conv_pre(xv, jnp.where(i == 0, 0.0, halo), w, b),), ()
    (xl,) = _rows_call("lru_pre", f_lru_pre, nt, [proj, proj, conv_l, lru_conv_b],
                       [_rt(tr, DL, cb_xl), _halo_prev(tr, DL, cb_xl), _full(conv_l), _full(lru_conv_b)],
                       [((T, DL), F32, _rt(tr, DL))], [])

    a_lru, u_lru = _lru_gates_fwd(xl, wa_b, b_a, wx_b, b_x, lru_lambda, tr)
    h_lru = _lru_scan_fwd(a_lru, u_lru, tr)

    def lru_post(hv, gate, g):
        return _rms(hv * jax.nn.gelu(gate), g)

    def f_lru_post(i, hv, gate, g):
        return (lru_post(hv, gate, g),), ()
    (y_lru_n,) = _rows_call("lru_post", f_lru_post, nt, [h_lru, proj, lru_norm],
                            [_rt(tr, DL), _rt(tr, DL, cb_gate), _full(lru_norm)], [((T, DL), BF16, _rt(tr, DL))], [])

    mixcat = jnp.concatenate([y_ssd_n, y_lru_n], axis=1)
    (mix,) = _mm("mix", mixcat, w_out_f)

    def f_post_mix(i, xv, mx, gpm, gpl):
        x1 = xv + _rms(mx, gpm)
        return (x1, _rms(x1, gpl)), ()
    x1, hn = _rows_call("post_mix", f_post_mix, nt, [x2, mix, post_mix_norm, pre_mlp_norm],
                        [_rt(tr, D), _rt(tr, D), _full(post_mix_norm), _full(pre_mlp_norm)],
                        [((T, D), F32, _rt(tr, D)), ((T, D), BF16, _rt(tr, D))], [])

    hm, act = _mm("mlp_in", hn, w_mi_f,
                  outs=((F32, None), (BF16, lambda r, e: jnp.square(jnp.maximum(r, 0.0)))))
    (hm2,) = _mm("mlp_out", act, w_mo_f)

    def f_final(i, x1v, hm2v, g, tg):
        def fwd(hv, gv):
            return x1v + _rms(hv, gv)
        x2v, vjp = jax.vjp(fwd, hm2v, g)
        err = x2v - tg
        dx2 = err * (1.0 / D)
        dh, dg = vjp(dx2)
        loss = jnp.full((1, LANES), 0.5 / D, F32) * jnp.sum(err * err)
        return (dx2, dh), (dg, loss)
    dx1a, dhm2, g_post_mlp, loss_part = _rows_call(
        "loss_head", f_final, nt, [x1, hm2, post_mlp_norm, tgt],
        [_rt(tr, D), _rt(tr, D), _full(post_mlp_norm), _rt(tr, D)],
        [((T, D), F32, _rt(tr, D)), ((T, D), BF16, _rt(tr, D))], [(1, D), (1, LANES)])

    (dhm,) = _mm("d_mlp_act", dhm2, w_mo_f, tb=True, extra=hm,
                 outs=((BF16, lambda r, e: r * (2.0 * jnp.maximum(e, 0.0))),))
    (gw_mo,) = _mm("dw_mlp_out", act, dhm2, ta=True, outs=((BF16, None),))
    (dhn,) = _mm("d_mlp_in", dhm, w_mi_f, tb=True)
    (gw_mi,) = _mm("dw_mlp_in", hn, dhm, ta=True, outs=((BF16, None),))

    def f_post_mix_bwd(i, x1v, mx, gpm, gpl, dhnv, dxa):
        _, vjp1 = jax.vjp(_rms, x1v, gpl)
        dx1, dgpl = vjp1(dhnv)
        dx1 = dx1 + dxa
        _, vjp2 = jax.vjp(_rms, mx, gpm)
        dmx, dgpm = vjp2(dx1)
        return (dx1, dmx), (dgpl, dgpm)
    dx1, dmix, g_pre_mlp, g_post_mix = _rows_call(
        "post_mix_bwd", f_post_mix_bwd, nt, [x1, mix, post_mix_norm, pre_mlp_norm, dhn, dx1a],
        [_rt(tr, D), _rt(tr, D), _full(post_mix_norm), _full(pre_mlp_norm), _rt(tr, D), _rt(tr, D)],
        [((T, D), F32, _rt(tr, D)), ((T, D), BF16, _rt(tr, D))], [(1, D), (1, D)])

    (dmixcat,) = _mm("d_mix", dmix, w_out_f, tb=True)
    (gw_out,) = _mm("dw_out", mixcat, dmix, ta=True, outs=((BF16, None),))
    cb_l = DS // DL

    def f_lru_post_bwd(i, hv, gate, g, dy):
        _, vjp = jax.vjp(lru_post, hv, gate, g)
        dh_, dgate, dg = vjp(dy)
        return (dh_, dgate), (dg,)
    dh_lru, dgate, g_lru_norm = _rows_call(
        "lru_post_bwd", f_lru_post_bwd, nt, [h_lru, proj, lru_norm, dmixcat],
        [_rt(tr, DL), _rt(tr, DL, cb_gate), _full(lru_norm), _rt(tr, DL, cb_l)],
        [((T, DL), F32, _rt(tr, DL)), ((T, DL), BF16, _rt(tr, DL))], [(1, DL)])

    du_lru, da_lru = _lru_scan_bwd(a_lru, h_lru, dh_lru, tr)
    dxl, g_wa, g_ba, g_wx, g_bx, g_lam = _lru_gates_bwd(xl, wa_b, b_a, wx_b, b_x, lru_lambda, da_lru, du_lru, tr)

    def f_lru_pre_bwd(i, xv, hp, xn, d, dn, w, b):
        dx, dw8, db = _conv_bwd_tile(i, nt, xv, hp, xn, d, dn, w, b, silu=False)
        return (dx,), (dw8, db)
    dx_lru, g_convl8, g_convl_b = _rows_call(
        "lru_pre_bwd", f_lru_pre_bwd, nt, [proj, proj, proj, dxl, dxl, conv_l, lru_conv_b],
        [_rt(tr, DL, cb_xl), _halo_prev(tr, DL, cb_xl), _halo_next(tr, DL, nt, cb_xl), _rt(tr, DL),
         _halo_next(tr, DL, nt), _full(conv_l), _full(lru_conv_b)],
        [((T, DL), BF16, _rt(tr, DL))], [(SUBLANES, DL), (1, DL)])

    def f_ssd_post_bwd(i, y, z, g, dy):
        _, vjp = jax.vjp(ssd_post, y, z, g)
        dy_, dz, dg = vjp(dy)
        return (dy_, dz), (dg,)
    dy_ssd, dz, g_ssd_norm = _rows_call(
        "ssd_post_bwd", f_ssd_post_bwd, nt, [y_ssd, proj, ssd_norm, dmixcat],
        [_rt(tr, DS), _rt(tr, DS, cb_z), _full(ssd_norm), _rt(tr, DS, 0)],
        [((T, DS), F32, _rt(tr, DS)), ((T, DS), BF16, _rt(tr, DS))], [(1, DS)])

    dxbc_act, ddt, g_alog, g_dskip = _ssd_bwd(xbc_act, dt, a_log, d_skip, h_prev, dy_ssd, n_heads, Q)

    def f_ssd_pre_bwd(i, xv, hp, xn, d, dn, dtr, ddtv, w, b, dtb):
        dx, dw8, db = _conv_bwd_tile(i, ntw, xv, hp, xn, d, dn, w, b, silu=True)
        ddtr = ddtv * jax.nn.sigmoid(dtr + dtb)
        return (dx, ddtr), (dw8, db, _colsum(ddtr))
    dxbc, ddt_raw, g_convs8, g_convs_b, g_dtb = _rows_call(
        "ssd_pre_bwd", f_ssd_pre_bwd, ntw,
        [proj, proj, proj, dxbc_act, dxbc_act, proj, ddt, conv_s, ssd_conv_b, dt_bias],
        [_rt(trw, XBC), _halo_prev(trw, XBC), _halo_next(trw, XBC, ntw), _rt(trw, XBC), _halo_next(trw, XBC, ntw),
         _rt(trw, LANES, cb_dt), _rt(trw, LANES), _full(conv_s), _full(ssd_conv_b), _full(dt_bias)],
        [((T, XBC), BF16, _rt(trw, XBC)), ((T, LANES), BF16, _rt(trw, LANES))],
        [(SUBLANES, XBC), (1, XBC), (1, LANES)])

    dproj = jnp.concatenate([dxbc, dz, dgate, dx_lru, ddt_raw], axis=1)
    (dh,) = _mm("d_proj", dproj, wp, tb=True)
    (gwp,) = _mm("dw_proj", h, dproj, ta=True, outs=((BF16, None),))

    def f_norm_in_bwd(i, xv, g, dhv, dxa):
        _, vjp = jax.vjp(_rms, xv, g)
        dx, dg = vjp(dhv)
        return (dx + dxa,), (dg,)
    grad_x, g_pre_mix = _rows_call(
        "norm_in_bwd", f_norm_in_bwd, nt, [x2, pre_mix_norm, dh, dx1],
        [_rt(tr, D), _full(pre_mix_norm), _rt(tr, D), _rt(tr, D)], [((T, D), F32, _rt(tr, D))], [(1, D)])

    small = {
        'pre_mix_norm': g_pre_mix, 'ssd_conv_w': g_convs8[:CONV_WIDTH], 'ssd_conv_b': g_convs_b,
        'ssd_dt_bias': g_dtb[:, :n_heads], 'ssd_a_log': g_alog[:, :n_heads], 'ssd_d': g_dskip[:, :n_heads],
        'ssd_norm': g_ssd_norm, 'lru_conv_w': g_convl8[:CONV_WIDTH], 'lru_conv_b': g_convl_b,
        'lru_w_a': g_wa, 'lru_b_a': g_ba, 'lru_w_x': g_wx, 'lru_b_x': g_bx, 'lru_lambda': g_lam,
        'lru_norm': g_lru_norm, 'post_mix_norm': g_post_mix, 'pre_mlp_norm': g_pre_mlp,
        'post_mlp_norm': g_post_mlp, 'loss': loss_part[:, :1],
    }
    small_names = list(small)
    (gathered,) = _allgather("allgather_small_grads", [_flat_rows([small[n] for n in small_names])])
    summed = dict(zip(small_names, _unflat(_sum8(gathered), [small[n].shape for n in small_names])))
    loss = summed.pop('loss').reshape(())
    for n, full_w in (('ssd_conv_w', XBC), ('lru_conv_w', DL)):
        wdt = full_w // N_DEV
        summed[n] = lax.dynamic_slice_in_dim(summed[n], dev * wdt, wdt, axis=1)
    small_params = [n for n in names if n not in big]
    grads = {n: summed[n].reshape(W[n].shape) for n in small_params}
    d_s, m_s, v_s = _adamw_small(_flat_rows([W[n] for n in small_params]), _flat_rows([grads[n] for n in small_params]),
                                 _flat_rows([Mo[n] for n in small_params]), _flat_rows([Vo[n] for n in small_params]))
    shapes = [W[n].shape for n in small_params]
    delta = dict(zip(small_params, _unflat(d_s, shapes)))
    new_m = dict(zip(small_params, _unflat(m_s, shapes)))
    new_v = dict(zip(small_params, _unflat(v_s, shapes)))

    gw_in_full = jnp.concatenate([gwp[:, XBC:XBC + DS], gwp[:, :XBC], gwp[:, NP - LANES:NP - LANES + n_heads],
                                  gwp[:, XBC + DS:XBC + DS + DL], gwp[:, XBC + DS + DL:XBC + DS + 2 * DL]], axis=1)
    fulls = [jnp.transpose(gw_in_full.reshape(D, N_DEV, DIN // N_DEV), (1, 0, 2)),
             gw_out.reshape(N_DEV, -1, D),
             jnp.transpose(gw_mi.reshape(D, N_DEV, DFF // N_DEV), (1, 0, 2)),
             gw_mo.reshape(N_DEV, DFF // N_DEV, D)]
    from_sib = _rs_sibling("reduce_scatter_sibling", fulls)
    pair = [_pair_sum("pair_sum_" + n, f, r, c_idx) for n, f, r in zip(big, fulls, from_sib)]
    from_chips = _rs_chips("reduce_scatter_chips", pair)
    for n, p, r in zip(big, pair, from_chips):
        g, d, nm, nv = _adamw_big("adamw_" + n, W[n][0], Mo[n][0], Vo[n][0], p, r, chip_idx)
        grads[n], delta[n], new_m[n], new_v[n] = g[None], d[None], nm[None], nv[None]

    return (loss, grad_x.reshape(x.shape), *[grads[n] for n in names], *[delta[n] for n in names],
            *[new_m[n] for n in names], *[new_v[n] for n in names])
```

```python
import functools

import jax
import jax.numpy as jnp
from jax import lax
from jax.experimental import pallas as pl
from jax.experimental.pallas import tpu as pltpu

F32, BF16 = jnp.float32, jnp.bfloat16
S = jax.ShapeDtypeStruct
MESH = pl.DeviceIdType.MESH

SSD_GROUPS = 8
LRU_C = 8.0
EPS = 1e-6
CONV_WIDTH = 4
ADAM_LR, ADAM_B1, ADAM_B2, ADAM_EPS, ADAM_WD, ADAM_STEP = 0.001, 0.9, 0.999, 1e-08, 0.01, 10

LANES = 128
SUBLANES = 8
VMEM_LIMIT = 56 * 1024 * 1024
N_DEV = 8
SMALL_W = 512
HI = lax.Precision.HIGHEST


def _pcall(body, **kw):
    return pl.pallas_call(body, **kw)


def _cparams(sem=None, **kw):
    return pltpu.CompilerParams(dimension_semantics=sem, vmem_limit_bytes=VMEM_LIMIT, **kw)


def _pick(n, cands):
    for c in cands:
        if c <= n and n % c == 0:
            return c
    return n


def _rt(tr, w, cb=0, n=None):
    if n is None:
        return pl.BlockSpec((tr, w), lambda i: (i, cb))
    return pl.BlockSpec((tr, w), lambda i: (n - 1 - i, cb))


def _halo_prev(tr, w, cb=0, n=None):
    k = tr // SUBLANES
    if n is None:
        return pl.BlockSpec((SUBLANES, w), lambda i: (jnp.maximum(i * k - 1, 0), cb))
    return pl.BlockSpec((SUBLANES, w), lambda i: (jnp.maximum((n - 1 - i) * k - 1, 0), cb))


def _halo_next(tr, w, nt, cb=0, n=None):
    k = tr // SUBLANES
    last = nt * k - 1
    if n is None:
        return pl.BlockSpec((SUBLANES, w), lambda i: (jnp.minimum((i + 1) * k, last), cb))
    return pl.BlockSpec((SUBLANES, w), lambda i: (jnp.minimum((n - i) * k, last), cb))


def _full(a):
    nd = a.ndim
    return pl.BlockSpec(a.shape, lambda i: (0,) * nd)


def _rows_call(name, fn, n_tiles, arrays, in_specs, out_tiled, out_acc, into=None):
    n_in, n_t = len(arrays), len(out_tiled)
    n_skip = 0 if into is None else 1

    def body(*refs):
        i = pl.program_id(0)
        touts, aouts = fn(i, *[r[...] for r in refs[:n_in]])
        refs = refs[n_skip:]
        for r, v in zip(refs[n_in:n_in + n_t], touts):
            r[...] = v.astype(r.dtype)
        accs = refs[n_in + n_t:]
        if accs:
            @pl.when(i == 0)
            def _():
                for r in accs:
                    r[...] = jnp.zeros_like(r)
            for r, v in zip(accs, aouts):
                r[...] += v

    out_shape = [S(sh, dt) for sh, dt, _ in out_tiled] + [S(sh, F32) for sh in out_acc]
    out_specs = [sp for _, _, sp in out_tiled]
    for sh in out_acc:
        out_specs.append(pl.BlockSpec(sh, lambda i, nd=len(sh): (0,) * nd))
    if into is None:
        return _pcall(body, name=name, grid=(n_tiles,), in_specs=in_specs, out_specs=out_specs,
                      out_shape=out_shape, compiler_params=_cparams(("arbitrary",)))(*arrays)
    return _pcall(body, name=name, grid=(n_tiles,), in_specs=list(in_specs) + [_ANY], out_specs=out_specs,
                  out_shape=out_shape, input_output_aliases={n_in: 0},
                  compiler_params=_cparams(("arbitrary",)))(*arrays, into)


def _rms(x, g):
    return x * lax.rsqrt(jnp.mean(x * x, axis=-1, keepdims=True) + EPS) * g


def _colsum(v):
    return jnp.sum(v, axis=0, keepdims=True)


_TILES = (1152, 1024, 896, 768, 640, 512, 384, 256, 128)
_K_TILES = (4096, 3456, 3072, 2688, 2048, 1536, 1344, 1152, 1024, 896, 768, 640, 512, 384, 256, 128)


def _mm(name, a, b, *, ta=False, tb=False, outs=((F32, None),), extra=None, out_blocks=None, tm=None, tn=None, tk=None):
    M, K = (a.shape[1], a.shape[0]) if ta else a.shape
    b3 = b.ndim == 3
    if b3:
        nb_b, brows, bcols = b.shape
        N = brows if tb else nb_b * bcols
    else:
        N = b.shape[0] if tb else b.shape[1]
    n_lim = N if out_blocks is None else N // out_blocks
    if b3 and not tb:
        n_lim = min(n_lim, bcols)
    tm = tm or _pick(M, _TILES[1:])
    tn = tn or _pick(n_lim, _TILES)
    tk = tk or _pick(bcols if (b3 and tb) else K, _K_TILES)
    nk = K // tk
    assert M % tm == 0 and N % tn == 0 and K % tk == 0
    dn = (((0 if ta else 1,), (1 if tb else 0,)), ((), ()))
    n_extra = 0 if extra is None else 1
    n_out = len(outs)

    def body(*refs):
        a_ref, b_ref = refs[0], refs[1]
        e_ref = refs[2] if n_extra else None
        o_refs = refs[2 + n_extra:2 + n_extra + n_out]

        def finish(r):
            e = e_ref[...] if n_extra else None
            for o, (_, f) in zip(o_refs, outs):
                o[...] = (r if f is None else f(r, e)).astype(o.dtype)

        part = lax.dot_general(a_ref[...], b_ref[...], dn, preferred_element_type=F32)
        if nk == 1:
            finish(part)
            return
        acc = refs[-1]
        k = pl.program_id(2)

        @pl.when(k == 0)
        def _():
            acc[...] = part

        @pl.when(jnp.logical_and(k > 0, k < nk - 1))
        def _():
            acc[...] += part

        @pl.when(k == nk - 1)
        def _():
            finish(acc[...] + part)

    a_spec = pl.BlockSpec((tk, tm), lambda i, j, k: (k, i)) if ta else pl.BlockSpec((tm, tk), lambda i, j, k: (i, k))
    if not b3:
        b_spec = pl.BlockSpec((tn, tk), lambda i, j, k: (j, k)) if tb else pl.BlockSpec((tk, tn), lambda i, j, k: (k, j))
    elif tb:
        per = bcols // tk
        b_spec = pl.BlockSpec((None, tn, tk), lambda i, j, k: (k // per, j, k % per))
    else:
        per = bcols // tn
        b_spec = pl.BlockSpec((None, tk, tn), lambda i, j, k: (j // per, k, j % per))
    o_spec = pl.BlockSpec((tm, tn), lambda i, j, k: (i, j))
    if out_blocks is None:
        out_specs, out_shape = [o_spec] * n_out, [S((M, N), dt) for dt, _ in outs]
    else:
        per_o = N // out_blocks // tn
        ob_spec = pl.BlockSpec((None, tm, tn), lambda i, j, k: (j // per_o, i, j % per_o))
        out_specs, out_shape = [ob_spec] * n_out, [S((out_blocks, M, N // out_blocks), dt) for dt, _ in outs]
    in_specs = [a_spec, b_spec] + ([o_spec] if n_extra else [])
    args = [a, b] + ([extra] if n_extra else [])
    return _pcall(body, name=name, grid=(M // tm, N // tn, nk), in_specs=in_specs, out_specs=out_specs,
                  out_shape=out_shape, scratch_shapes=[pltpu.VMEM((tm, tn), F32)] if nk > 1 else [],
                  compiler_params=_cparams(("parallel", "parallel", "arbitrary")))(*args)


def _shift_down(x, halo, s):
    if s == 0:
        return x
    r = pltpu.roll(x, s, 0)
    hr = pltpu.roll(halo, s, 0)
    row = lax.broadcasted_iota(jnp.int32, halo.shape, 0)
    top = jnp.where(row < s, hr, r[:SUBLANES])
    if x.shape[0] == SUBLANES:
        return top
    return jnp.concatenate([top, r[SUBLANES:]], axis=0)


def _shift_up(x, nxt, s):
    if s == 0:
        return x
    n = x.shape[0]
    r = pltpu.roll(x, n - s, 0)
    nr = pltpu.roll(nxt, SUBLANES - s, 0)
    row = lax.broadcasted_iota(jnp.int32, nxt.shape, 0)
    bot = jnp.where(row >= SUBLANES - s, nr, r[n - SUBLANES:])
    if n == SUBLANES:
        return bot
    return jnp.concatenate([r[:n - SUBLANES], bot], axis=0)


def _conv_pre(x, halo, w, b):
    acc = b + w[CONV_WIDTH - 1:CONV_WIDTH, :] * x
    for k in range(CONV_WIDTH - 1):
        acc = acc + w[k:k + 1, :] * _shift_down(x, halo, CONV_WIDTH - 1 - k)
    return acc


def _silu_grad(p):
    s = jax.nn.sigmoid(p)
    return s * (1.0 + p * (1.0 - s))


def _conv_bwd_tile(i, n_tiles, x, hprev, xnext, d, dnext, w, b, silu):
    hprev = jnp.where(i == 0, 0.0, hprev)
    if silu:
        d = d * _silu_grad(_conv_pre(x, hprev, w, b))
        pre_next = _conv_pre(xnext, x[x.shape[0] - SUBLANES:], w, b)
        dnext = dnext * _silu_grad(pre_next)
    dnext = jnp.where(i == n_tiles - 1, 0.0, dnext)
    dx = w[CONV_WIDTH - 1:CONV_WIDTH, :] * d
    row8 = lax.broadcasted_iota(jnp.int32, (SUBLANES, x.shape[1]), 0)
    dw8 = jnp.where(row8 == CONV_WIDTH - 1, _colsum(d * x), 0.0)
    for k in range(CONV_WIDTH - 1):
        s = CONV_WIDTH - 1 - k
        dx = dx + w[k:k + 1, :] * _shift_up(d, dnext, s)
        dw8 = dw8 + jnp.where(row8 == k, _colsum(d * _shift_down(x, hprev, s)), 0.0)
    return dx, dw8, _colsum(d)


def _ssd_dims(xbc_act, n_heads):
    T, XBC = xbc_act.shape
    GN = XBC // 4
    DS = XBC - 2 * GN
    G = SSD_GROUPS
    N = GN // G
    P = DS // n_heads
    K = n_heads // G
    return T, XBC, DS, GN, G, N, P, K


def _ssd_common(dt, alog, Q):
    a = -jnp.exp(alog)
    adt = dt * a
    li = lax.broadcasted_iota(jnp.int32, (Q, Q), 0)
    si = lax.broadcasted_iota(jnp.int32, (Q, Q), 1)
    causal = li >= si
    ltri = causal.astype(F32)
    acs = jnp.dot(ltri, adt, precision=HI, preferred_element_type=F32)
    acs_row = lax.dot_general(adt, ltri, (((0,), (1,)), ((), ())), precision=HI,
                              preferred_element_type=F32)
    return a, adt, causal, ltri, acs, acs_row


def _expander(g, K, P, W):
    r = lax.broadcasted_iota(jnp.int32, (LANES, W), 0)
    c = lax.broadcasted_iota(jnp.int32, (LANES, W), 1)
    return (c // P + g * K == r).astype(F32)


def _dotb(a, b, dn=(((1,), (0,)), ((), ()))):
    return lax.dot_general(a.astype(BF16), b.astype(BF16), dn, preferred_element_type=F32)


_NT = (((1,), (1,)), ((), ()))
_TN = (((0,), (0,)), ((), ()))


def _ssd_fwd(xbc_act, dt, alog, dskip, n_heads, Q):
    T, XBC, DS, GN, G, N, P, K = _ssd_dims(xbc_act, n_heads)
    W = K * P
    nc = T // Q

    def body(xs_ref, b_ref, c_ref, dt_ref, alog_ref, d_ref, y_ref, hp_ref, h_scr):
        ci = pl.program_id(0)

        @pl.when(ci == 0)
        def _():
            h_scr[...] = jnp.zeros_like(h_scr)

        dtv = dt_ref[...]
        a, adt, causal, ltri, acs, acs_row = _ssd_common(dtv, alog_ref[...], Q)
        lane_head = lax.broadcasted_iota(jnp.int32, (Q, W), 1) // P
        for g in range(G):
            eg = _expander(g, K, P, W)
            dtb = jnp.dot(dtv, eg, precision=HI, preferred_element_type=F32)
            acsb = jnp.dot(acs, eg, precision=HI, preferred_element_type=F32)
            lastb = acsb[Q - 1:Q, :]
            db = jnp.dot(jnp.broadcast_to(d_ref[...], (SUBLANES, LANES)), eg, precision=HI,
                         preferred_element_type=F32)[0:1, :]
            xg = xs_ref[:, g * W:(g + 1) * W]
            bg = b_ref[:, g * N:(g + 1) * N]
            cg = c_ref[:, g * N:(g + 1) * N]
            xt = xg * dtb
            sc = _dotb(cg, bg, _NT)
            yd = jnp.zeros((Q, W), F32)
            for k in range(K):
                h = g * K + k
                seg = acs[:, h:h + 1] - acs_row[h:h + 1, :]
                lh = jnp.where(causal, jnp.exp(jnp.minimum(seg, 0.0)), 0.0)
                xk = jnp.where(lane_head == k, xt, 0.0)
                yd = yd + _dotb(sc * lh, xk)
            hp = h_scr[g]
            yoff = _dotb(cg, hp) * jnp.exp(acsb)
            y_ref[:, g * W:(g + 1) * W] = yd + yoff + xg * db
            e_end = jnp.exp(lastb - acsb)
            st = _dotb(bg, xt * e_end, _TN)
            hp_ref[0, g] = hp
            h_scr[g] = jnp.exp(lastb) * hp + st

    cb = DS // GN
    in_specs = [pl.BlockSpec((Q, DS), lambda c: (c, 0)),
                pl.BlockSpec((Q, GN), lambda c: (c, cb)),
                pl.BlockSpec((Q, GN), lambda c: (c, cb + 1)),
                pl.BlockSpec((Q, LANES), lambda c: (c, 0)),
                pl.BlockSpec((1, LANES), lambda c: (0, 0)),
                pl.BlockSpec((1, LANES), lambda c: (0, 0))]
    out_specs = [pl.BlockSpec((Q, DS), lambda c: (c, 0)),
                 pl.BlockSpec((1, G, N, W), lambda c: (c, 0, 0, 0))]
    return _pcall(body, name="ssd_fwd", grid=(nc,), in_specs=in_specs, out_specs=out_specs,
                  out_shape=[S((T, DS), F32), S((nc, G, N, W), F32)],
                  scratch_shapes=[pltpu.VMEM((G, N, W), F32)],
                  compiler_params=_cparams(("arbitrary",)))(xbc_act, xbc_act, xbc_act, dt, alog, dskip)


def _ssd_bwd(xbc_act, dt, alog, dskip, hprev, dy, n_heads, Q):
    T, XBC, DS, GN, G, N, P, K = _ssd_dims(xbc_act, n_heads)
    W = K * P
    nc = T // Q

    def body(xs_ref, b_ref, c_ref, dt_ref, alog_ref, d_ref, hp_ref, dy_ref,
             dxbc_ref, ddt_ref, dalog_ref, dd_ref, dh_scr):
        ci = pl.program_id(0)

        @pl.when(ci == 0)
        def _():
            dh_scr[...] = jnp.zeros_like(dh_scr)
            dalog_ref[...] = jnp.zeros_like(dalog_ref)
            dd_ref[...] = jnp.zeros_like(dd_ref)

        dtv = dt_ref[...]
        a, adt, causal, ltri, acs, acs_row = _ssd_common(dtv, alog_ref[...], Q)
        lane_head = lax.broadcasted_iota(jnp.int32, (Q, W), 1) // P
        lane128 = lax.broadcasted_iota(jnp.int32, (Q, LANES), 1)
        sub128 = lax.broadcasted_iota(jnp.int32, (LANES, Q), 0)
        rowq = lax.broadcasted_iota(jnp.int32, (Q, W), 0)
        dacs = jnp.zeros((Q, LANES), F32)
        dacs_row = jnp.zeros((LANES, Q), F32)
        ddt = jnp.zeros((Q, LANES), F32)
        dd_acc = jnp.zeros((1, LANES), F32)
        for g in range(G):
            eg = _expander(g, K, P, W)
            dtb = jnp.dot(dtv, eg, precision=HI, preferred_element_type=F32)
            acsb = jnp.dot(acs, eg, precision=HI, preferred_element_type=F32)
            lastb = acsb[Q - 1:Q, :]
            db = jnp.dot(jnp.broadcast_to(d_ref[...], (SUBLANES, LANES)), eg, precision=HI,
                         preferred_element_type=F32)[0:1, :]
            xg = xs_ref[:, g * W:(g + 1) * W]
            bg = b_ref[:, g * N:(g + 1) * N]
            cg = c_ref[:, g * N:(g + 1) * N]
            dyg = dy_ref[:, g * W:(g + 1) * W]
            hp = hp_ref[0, g]
            dhn = dh_scr[g]
            xt = xg * dtb
            sc = _dotb(cg, bg, _NT)
            eacs = jnp.exp(acsb)
            e_end = jnp.exp(lastb - acsb)
            elast = jnp.exp(lastb)

            wv = dyg * eacs
            dcg = _dotb(wv, hp, _NT)
            dhp = _dotb(cg, wv, _TN) + elast * dhn
            dacsb = dyg * (_dotb(cg, hp) * eacs)

            xe = xt * e_end
            dbg = _dotb(xe, dhn, _NT)
            v = _dotb(bg, dhn)
            dxt = v * e_end
            de = v * xe
            dacsb = dacsb - de
            dlastb = _colsum(de) + elast * jnp.sum(dhn * hp, axis=0, keepdims=True)

            dsc = jnp.zeros((Q, Q), F32)
            for k in range(K):
                h = g * K + k
                seg = acs[:, h:h + 1] - acs_row[h:h + 1, :]
                lh = jnp.where(causal, jnp.exp(jnp.minimum(seg, 0.0)), 0.0)
                mh = sc * lh
                dyk = jnp.where(lane_head == k, dyg, 0.0)
                dxt = dxt + jnp.where(lane_head == k, _dotb(mh, dyg, _TN), 0.0)
                dm = _dotb(dyk, xt, _NT)
                dsc = dsc + dm * lh
                gm = dm * mh
                dacs = dacs + jnp.where(lane128 == h, jnp.sum(gm, axis=1, keepdims=True), 0.0)
                dacs_row = dacs_row - jnp.where(sub128 == h, jnp.sum(gm, axis=0, keepdims=True), 0.0)
            dcg = dcg + _dotb(dsc, bg)
            dbg = dbg + _dotb(dsc, cg, _TN)

            dacsb = dacsb + jnp.where(rowq == Q - 1, dlastb, 0.0)
            dacs = dacs + lax.dot_general(dacsb, eg, _NT, precision=HI, preferred_element_type=F32)
            ddt = ddt + lax.dot_general(dxt * xg, eg, _NT, precision=HI, preferred_element_type=F32)
            dd_acc = dd_acc + lax.dot_general(jnp.broadcast_to(_colsum(dyg * xg), (SUBLANES, W)), eg, _NT,
                                              precision=HI, preferred_element_type=F32)[0:1, :]
            dxbc_ref[:, g * W:(g + 1) * W] = dxt * dtb + dyg * db
            dxbc_ref[:, DS + g * N:DS + (g + 1) * N] = dbg
            dxbc_ref[:, DS + GN + g * N:DS + GN + (g + 1) * N] = dcg
            dh_scr[g] = dhp

        eye = (lax.broadcasted_iota(jnp.int32, (LANES, LANES), 0) ==
               lax.broadcasted_iota(jnp.int32, (LANES, LANES), 1)).astype(F32)
        dacs = dacs + lax.dot_general(dacs_row, eye, _TN, precision=HI, preferred_element_type=F32)
        dadt = lax.dot_general(ltri, dacs, _TN, precision=HI, preferred_element_type=F32)
        ddt_ref[...] = ddt + dadt * a
        dalog_ref[...] += _colsum(dadt * dtv) * a
        dd_ref[...] += dd_acc

    cb = DS // GN
    rv = lambda c: nc - 1 - c
    in_specs = [pl.BlockSpec((Q, DS), lambda c: (rv(c), 0)),
                pl.BlockSpec((Q, GN), lambda c: (rv(c), cb)),
                pl.BlockSpec((Q, GN), lambda c: (rv(c), cb + 1)),
                pl.BlockSpec((Q, LANES), lambda c: (rv(c), 0)),
                pl.BlockSpec((1, LANES), lambda c: (0, 0)),
                pl.BlockSpec((1, LANES), lambda c: (0, 0)),
                pl.BlockSpec((1, G, N, W), lambda c: (rv(c), 0, 0, 0)),
                pl.BlockSpec((Q, DS), lambda c: (rv(c), 0))]
    out_specs = [pl.BlockSpec((Q, XBC), lambda c: (rv(c), 0)),
                 pl.BlockSpec((Q, LANES), lambda c: (rv(c), 0)),
                 pl.BlockSpec((1, LANES), lambda c: (0, 0)),
                 pl.BlockSpec((1, LANES), lambda c: (0, 0))]
    return _pcall(body, name="ssd_bwd", grid=(nc,), in_specs=in_specs, out_specs=out_specs,
                  out_shape=[S((T, XBC), F32), S((T, LANES), F32), S((1, LANES), F32), S((1, LANES), F32)],
                  scratch_shapes=[pltpu.VMEM((G, N, W), F32)],
                  compiler_params=_cparams(("arbitrary",)))(
                      xbc_act, xbc_act, xbc_act, dt, alog, dskip, hprev, dy)


def _blockdiag(x, w_ref, dn=(((1,), (0,)), ((), ()))):
    H, B, _ = w_ref.shape
    return jnp.concatenate([_dotb(x[:, h * B:(h + 1) * B], w_ref[h], dn) for h in range(H)], axis=1)


def _lru_elem(xl, r_pre, i_pre, lam):
    r = jax.nn.sigmoid(r_pre)
    i = jax.nn.sigmoid(i_pre)
    log_a = -LRU_C * r * jax.nn.softplus(-lam)
    a = jnp.exp(log_a)
    u = jnp.sqrt(1.0 - jnp.exp(2.0 * log_a)) * (i * xl)
    return a, u


def _lru_gates_fwd(xl, w_a, b_a, w_x, b_x, lam, tr):
    T, DL = xl.shape

    def body(xl_ref, wa_ref, ba_ref, wx_ref, bx_ref, lam_ref, a_ref, u_ref):
        x = xl_ref[...]
        r_pre = _blockdiag(x, wa_ref) + ba_ref[...]
        i_pre = _blockdiag(x, wx_ref) + bx_ref[...]
        a, u = _lru_elem(x, r_pre, i_pre, lam_ref[...])
        a_ref[...] = a
        u_ref[...] = u

    w3 = pl.BlockSpec(w_a.shape, lambda i: (0, 0, 0))
    vec = pl.BlockSpec((1, DL), lambda i: (0, 0))
    return _pcall(body, name="lru_gates_fwd", grid=(T // tr,),
                  in_specs=[_rt(tr, DL), w3, vec, w3, vec, vec],
                  out_specs=[_rt(tr, DL), _rt(tr, DL)], out_shape=[S((T, DL), F32), S((T, DL), F32)],
                  compiler_params=_cparams(("parallel",)))(xl, w_a, b_a, w_x, b_x, lam)


def _lru_gates_bwd(xl, w_a, b_a, w_x, b_x, lam, da, du, tr):
    T, DL = xl.shape
    H, B, _ = w_a.shape

    def body(xl_ref, wa_ref, ba_ref, wx_ref, bx_ref, lam_ref, da_ref, du_ref,
             dxl_ref, dwa_ref, dba_ref, dwx_ref, dbx_ref, dlam_ref):
        @pl.when(pl.program_id(0) == 0)
        def _():
            for r in (dwa_ref, dba_ref, dwx_ref, dbx_ref, dlam_ref):
                r[...] = jnp.zeros_like(r)

        x = xl_ref[...]
        r_pre = _blockdiag(x, wa_ref) + ba_ref[...]
        i_pre = _blockdiag(x, wx_ref) + bx_ref[...]
        _, vjp = jax.vjp(_lru_elem, x, r_pre, i_pre, lam_ref[...])
        dx, dr, di, dlam = vjp((da_ref[...], du_ref[...]))
        dxl_ref[...] = dx + _blockdiag(dr, wa_ref, _NT) + _blockdiag(di, wx_ref, _NT)
        for h in range(H):
            xh = x[:, h * B:(h + 1) * B]
            dwa_ref[h] += _dotb(xh, dr[:, h * B:(h + 1) * B], _TN)
            dwx_ref[h] += _dotb(xh, di[:, h * B:(h + 1) * B], _TN)
        dba_ref[...] += _colsum(dr)
        dbx_ref[...] += _colsum(di)
        dlam_ref[...] += dlam

    w3 = pl.BlockSpec(w_a.shape, lambda i: (0, 0, 0))
    vec = pl.BlockSpec((1, DL), lambda i: (0, 0))
    return _pcall(body, name="lru_gates_bwd", grid=(T // tr,),
                  in_specs=[_rt(tr, DL), w3, vec, w3, vec, vec, _rt(tr, DL), _rt(tr, DL)],
                  out_specs=[_rt(tr, DL), w3, vec, w3, vec, vec],
                  out_shape=[S((T, DL), F32), S(w_a.shape, F32), S((1, DL), F32), S(w_a.shape, F32),
                             S((1, DL), F32), S((1, DL), F32)],
                  compiler_params=_cparams(("arbitrary",)))(xl, w_a, b_a, w_x, b_x, lam, da, du)


def _scan_tile(a, u, up):
    n = a.shape[0]
    row = lax.broadcasted_iota(jnp.int32, a.shape, 0)
    d = 1
    while d < n:
        if up:
            keep = row < n - d
            a_s = jnp.where(keep, pltpu.roll(a, n - d, 0), 1.0)
            u_s = jnp.where(keep, pltpu.roll(u, n - d, 0), 0.0)
        else:
            keep = row >= d
            a_s = jnp.where(keep, pltpu.roll(a, d, 0), 1.0)
            u_s = jnp.where(keep, pltpu.roll(u, d, 0), 0.0)
        u = a * u_s + u
        a = a * a_s
        d *= 2
    return a, u


def _lru_scan_fwd(a, u, tr):
    T, DL = a.shape

    def body(a_ref, u_ref, h_ref, carry):
        @pl.when(pl.program_id(0) == 0)
        def _():
            carry[...] = jnp.zeros_like(carry)

        A, U = _scan_tile(a_ref[...], u_ref[...], up=False)
        h = U + A * carry[0:1, :]
        h_ref[...] = h
        carry[...] = jnp.broadcast_to(h[tr - 1:tr, :], carry.shape)

    return _pcall(body, name="lru_scan_fwd", grid=(T // tr,), in_specs=[_rt(tr, DL), _rt(tr, DL)],
                  out_specs=_rt(tr, DL), out_shape=S((T, DL), F32),
                  scratch_shapes=[pltpu.VMEM((SUBLANES, DL), F32)],
                  compiler_params=_cparams(("arbitrary",)))(a, u)


def _lru_scan_bwd(a, h, dh, tr):
    T, DL = a.shape
    n = T // tr

    def body(a_ref, an_ref, h_ref, hp_ref, dh_ref, du_ref, da_ref, carry):
        i = pl.program_id(0)
        ti = n - 1 - i

        @pl.when(i == 0)
        def _():
            carry[...] = jnp.zeros_like(carry)

        a_next = _shift_up(a_ref[...], jnp.where(ti == n - 1, 0.0, an_ref[...]), 1)
        A, U = _scan_tile(a_next, dh_ref[...], up=True)
        g = U + A * carry[0:1, :]
        du_ref[...] = g
        h_prev = _shift_down(h_ref[...], jnp.where(ti == 0, 0.0, hp_ref[...]), 1)
        da_ref[...] = g * h_prev
        carry[...] = jnp.broadcast_to(g[0:1, :], carry.shape)

    return _pcall(body, name="lru_scan_bwd", grid=(n,),
                  in_specs=[_rt(tr, DL, 0, n), _halo_next(tr, DL, n, 0, n), _rt(tr, DL, 0, n),
                            _halo_prev(tr, DL, 0, n), _rt(tr, DL, 0, n)],
                  out_specs=[_rt(tr, DL, 0, n), _rt(tr, DL, 0, n)],
                  out_shape=[S((T, DL), F32), S((T, DL), F32)],
                  scratch_shapes=[pltpu.VMEM((SUBLANES, DL), F32)],
                  compiler_params=_cparams(("arbitrary",)))(a, a, h, h, dh)


def _adamw(w, g, m, v):
    m = ADAM_B1 * m + (1.0 - ADAM_B1) * g
    v = ADAM_B2 * v + (1.0 - ADAM_B2) * (g * g)
    m_hat = m / (1.0 - ADAM_B1 ** ADAM_STEP)
    v_hat = v / (1.0 - ADAM_B2 ** ADAM_STEP)
    delta = -ADAM_LR * (m_hat / (jnp.sqrt(v_hat) + ADAM_EPS) + ADAM_WD * w)
    return delta, m, v


def _adamw_big(name, w, m, v, part, recv, chip_idx):
    _, R, C = w.shape
    tr = _pick(R, (256, 128, 64, 32, 16))

    def body(ci_ref, w_ref, m_ref, v_ref, p_ref, r0, r1, r2, g_ref, d_ref, nm_ref, nv_ref):
        g = ((p_ref[...].astype(F32) + r0[...].astype(F32)) + r1[...].astype(F32)) + r2[...].astype(F32)
        d, nm, nv = _adamw(w_ref[...], g, m_ref[...], v_ref[...])
        g_ref[...] = g
        d_ref[...] = d
        nm_ref[...] = nm
        nv_ref[...] = nv

    r_spec = lambda s: pl.BlockSpec((None, tr, C), lambda i, ci: (s, i, 0))
    t2 = r_spec(0)
    gs = pltpu.PrefetchScalarGridSpec(
        num_scalar_prefetch=1, grid=(R // tr,),
        in_specs=[t2, t2, t2, pl.BlockSpec((None, tr, C), lambda i, ci: (ci[0], i, 0)),
                  r_spec(0), r_spec(1), r_spec(2)],
        out_specs=[t2, t2, t2, t2])
    return _pcall(body, name=name, grid_spec=gs, out_shape=[S((1, R, C), F32)] * 4,
                  compiler_params=_cparams(("parallel",)))(chip_idx, w, m, v, part, recv, recv, recv)


def _adamw_small(ws, gs, ms, vs):
    n = len(ws)

    def body(*refs):
        for k in range(n):
            d, nm, nv = _adamw(refs[k][...], refs[n + k][...], refs[2 * n + k][...], refs[3 * n + k][...])
            refs[4 * n + k][...] = d
            refs[5 * n + k][...] = nm
            refs[6 * n + k][...] = nv

    res = _pcall(body, name="adamw_small", out_shape=[S(w.shape, F32) for w in ws] * 3,
                 compiler_params=_cparams())(*ws, *gs, *ms, *vs)
    return res[:n], res[n:2 * n], res[2 * n:]


def _sum8(name, parts):
    _, R, C = parts.shape

    def body(p_ref, o_ref):
        acc = p_ref[0]
        for k in range(1, N_DEV):
            acc = acc + p_ref[k]
        o_ref[...] = acc

    return _pcall(body, name=name, out_shape=S((R, C), F32), compiler_params=_cparams())(parts)


def _pair_sum(name, full, recv, c_idx):
    _, R, C = full.shape
    tr = _pick(R, (256, 128, 64, 32, 16))

    def body(c_ref, f_ref, r_ref, o_ref):
        o_ref[...] = (f_ref[...].astype(F32) + r_ref[...].astype(F32)).astype(o_ref.dtype)

    gs = pltpu.PrefetchScalarGridSpec(
        num_scalar_prefetch=1, grid=(4, R // tr),
        in_specs=[pl.BlockSpec((None, tr, C), lambda j, i, c: (2 * j + c[0], i, 0)),
                  pl.BlockSpec((None, tr, C), lambda j, i, c: (j, i, 0))],
        out_specs=pl.BlockSpec((None, tr, C), lambda j, i, c: (j, i, 0)))
    return _pcall(body, name=name, grid_spec=gs, out_shape=S((4, R, C), BF16),
                  compiler_params=_cparams(("parallel", "parallel")))(c_idx, full, recv)


def _cast_bf16(name, w):
    _, R, C = w.shape
    tr = _pick(R, (256, 128, 64, 32, 16))

    def body(w_ref, o_ref):
        o_ref[...] = w_ref[...].astype(BF16)

    sp = pl.BlockSpec((tr, C), lambda i: (i, 0))
    return _pcall(body, name=name, grid=(R // tr,), in_specs=[pl.BlockSpec((None, tr, C), lambda i: (0, i, 0))],
                  out_specs=sp, out_shape=S((R, C), BF16),
                  compiler_params=_cparams(("parallel",)))(w)


_ANY = pl.BlockSpec(memory_space=pl.ANY)


def _position():
    return lax.axis_index("x"), lax.axis_index("y"), lax.axis_index("c")


def _allgather(name, shards):
    n = len(shards)

    def body(*refs):
        ins, outs = refs[:n], refs[n:2 * n]
        send, recv, lsem = refs[2 * n:]
        x, y, c = _position()
        me, sib = (x, y, c), (x, y, 1 - c)
        chips = [(1 - x, y), (x, 1 - y), (1 - x, 1 - y)]

        def copy(a, k, block, to, src=None):
            bx, by, bc = block
            dst = outs[a].at[4 * bx + 2 * by + bc]
            return pltpu.make_async_remote_copy(
                src_ref=dst if src is None else src, dst_ref=dst, send_sem=send.at[a, k], recv_sem=recv.at[a, k],
                device_id=to, device_id_type=MESH)

        mine = [pltpu.make_async_copy(ins[a], outs[a].at[4 * x + 2 * y + c], lsem.at[a]) for a in range(n)]
        for cp in mine:
            cp.start()
        first = []
        for a in range(n):
            first.append(copy(a, 0, me, sib, src=ins[a]))
            first += [copy(a, 1 + j, me, (*chip, c), src=ins[a]) for j, chip in enumerate(chips)]
        for cp in first:
            cp.start()
        passed = []
        for j, chip in enumerate(chips):
            for a in range(n):
                copy(a, 1 + j, (*chip, c), me).wait_recv()
                cp = copy(a, 4 + j, (*chip, c), sib)
                cp.start()
                passed.append(cp)
        for a in range(n):
            copy(a, 0, sib, me).wait_recv()
        for j, chip in enumerate(chips):
            for a in range(n):
                copy(a, 4 + j, (*chip, 1 - c), me).wait_recv()
        for cp in first + passed:
            cp.wait_send()
        for cp in mine:
            cp.wait()

    return _pcall(body, name=name, in_specs=[_ANY] * n, out_specs=[_ANY] * n,
                  out_shape=[S((N_DEV,) + s.shape, s.dtype) for s in shards],
                  scratch_shapes=[pltpu.SemaphoreType.DMA((n, 7)), pltpu.SemaphoreType.DMA((n, 7)),
                                  pltpu.SemaphoreType.DMA((n,))])(*shards)


def _rs_sibling(name, fulls):
    n = len(fulls)

    def body(*refs):
        ins, outs = refs[:n], refs[n:2 * n]
        send, recv = refs[2 * n:]
        x, y, c = _position()
        copies = []
        for a in range(n):
            for j in range(4):
                copies.append(pltpu.make_async_remote_copy(
                    src_ref=ins[a].at[2 * j + (1 - c)], dst_ref=outs[a].at[j], send_sem=send.at[a, j],
                    recv_sem=recv.at[a, j], device_id=(x, y, 1 - c), device_id_type=MESH))
        for cp in copies:
            cp.start()
        for cp in copies:
            cp.wait()

    return _pcall(body, name=name, in_specs=[_ANY] * n, out_specs=[_ANY] * n,
                  out_shape=[S((4,) + f.shape[1:], f.dtype) for f in fulls],
                  scratch_shapes=[pltpu.SemaphoreType.DMA((n, 4)), pltpu.SemaphoreType.DMA((n, 4))])(*fulls)


def _rs_chips(name, parts):
    n = len(parts)

    def body(*refs):
        ins, outs = refs[:n], refs[n:2 * n]
        send, recv = refs[2 * n:]
        x, y, c = _position()
        chips = [(1 - x, y), (x, 1 - y), (1 - x, 1 - y)]
        copies = []
        for a in range(n):
            for j, (px, py) in enumerate(chips):
                copies.append(pltpu.make_async_remote_copy(
                    src_ref=ins[a].at[2 * px + py], dst_ref=outs[a].at[j], send_sem=send.at[a, j],
                    recv_sem=recv.at[a, j], device_id=(px, py, c), device_id_type=MESH))
        for cp in copies:
            cp.start()
        for cp in copies:
            cp.wait()

    return _pcall(body, name=name, in_specs=[_ANY] * n, out_specs=[_ANY] * n,
                  out_shape=[S((3,) + p.shape[1:], p.dtype) for p in parts],
                  scratch_shapes=[pltpu.SemaphoreType.DMA((n, 3)), pltpu.SemaphoreType.DMA((n, 3))])(*parts)


def _pad_lanes(v):
    return jnp.pad(v, ((0, 0), (0, LANES - v.shape[1])))


def _flat_rows(pieces):
    flat = jnp.concatenate([p.reshape(-1) for p in pieces])
    rows = -(-flat.shape[0] // (SMALL_W * SUBLANES)) * SUBLANES
    return jnp.pad(flat, (0, rows * SMALL_W - flat.shape[0])).reshape(rows, SMALL_W)


def _unflat(buf, shapes):
    flat = buf.reshape(-1)
    out, off = [], 0
    for sh in shapes:
        n = 1
        for d in sh:
            n *= d
        out.append(flat[off:off + n].reshape(sh))
        off += n
    return out


def kernel(x, pre_mix_norm, w_in, ssd_conv_w, ssd_conv_b, ssd_dt_bias, ssd_a_log, ssd_d, ssd_norm, lru_conv_w, lru_conv_b, lru_w_a, lru_b_a, lru_w_x, lru_b_x, lru_lambda, lru_norm, w_out, post_mix_norm, pre_mlp_norm, w_mlp_in, w_mlp_out, post_mlp_norm, loss_target, m_pre_mix_norm, m_w_in, m_ssd_conv_w, m_ssd_conv_b, m_ssd_dt_bias, m_ssd_a_log, m_ssd_d, m_ssd_norm, m_lru_conv_w, m_lru_conv_b, m_lru_w_a, m_lru_b_a, m_lru_w_x, m_lru_b_x, m_lru_lambda, m_lru_norm, m_w_out, m_post_mix_norm, m_pre_mlp_norm, m_w_mlp_in, m_w_mlp_out, m_post_mlp_norm, v_pre_mix_norm, v_w_in, v_ssd_conv_w, v_ssd_conv_b, v_ssd_dt_bias, v_ssd_a_log, v_ssd_d, v_ssd_norm, v_lru_conv_w, v_lru_conv_b, v_lru_w_a, v_lru_b_a, v_lru_w_x, v_lru_b_x, v_lru_lambda, v_lru_norm, v_w_out, v_post_mix_norm, v_pre_mlp_norm, v_w_mlp_in, v_w_mlp_out, v_post_mlp_norm):
    names = ['pre_mix_norm', 'w_in', 'ssd_conv_w', 'ssd_conv_b', 'ssd_dt_bias', 'ssd_a_log', 'ssd_d', 'ssd_norm',
             'lru_conv_w', 'lru_conv_b', 'lru_w_a', 'lru_b_a', 'lru_w_x', 'lru_b_x', 'lru_lambda', 'lru_norm',
             'w_out', 'post_mix_norm', 'pre_mlp_norm', 'w_mlp_in', 'w_mlp_out', 'post_mlp_norm']
    loc = locals()
    W = {n: loc[n] for n in names}
    Mo = {n: loc["m_" + n] for n in names}
    Vo = {n: loc["v_" + n] for n in names}
    big = ['w_in', 'w_out', 'w_mlp_in', 'w_mlp_out']

    px, py, pc = _position()
    dev = 4 * px + 2 * py + pc
    c_idx = jnp.reshape(pc, (1,)).astype(jnp.int32)
    chip_idx = jnp.reshape(2 * px + py, (1,)).astype(jnp.int32)

    _, T, D = x.shape
    x2 = x.reshape(T, D)
    tgt = loss_target.reshape(T, D)
    n_heads = ssd_dt_bias.shape[1]
    XBC = ssd_conv_b.shape[1]
    GN = XBC // 4
    DS = XBC - 2 * GN
    DL = lru_norm.shape[1]
    DFF = w_mlp_in.shape[2] * N_DEV
    DIN = w_in.shape[2] * N_DEV
    NP = XBC + DS + 2 * DL + LANES
    assert DS % GN == 0 and XBC % DS == 0 and DS == DL and n_heads <= LANES
    cb_z, cb_gate, cb_xl, cb_dt = XBC // DS, XBC // DS + 1, XBC // DS + 2, (XBC + DS + 2 * DL) // LANES
    tr = min(256, T // 2)
    trw = min(128, T // 2)
    nt, ntw = T // tr, T // trw
    Q = min(128, T // 2)

    sh = {n: _cast_bf16("cast_" + n, W[n]) for n in big}
    g_in, g_out, g_mi, g_mo, g_cs, g_cl = _allgather(
        "allgather_weights", [sh[n] for n in big] + [ssd_conv_w[0], lru_conv_w[0]])
    conv_s = jnp.transpose(g_cs, (1, 0, 2)).reshape(CONV_WIDTH, XBC)
    conv_l = jnp.transpose(g_cl, (1, 0, 2)).reshape(CONV_WIDTH, DL)
    wb = DIN // N_DEV
    o_z, o_xbc, o_dt, o_gate, o_xl = 0, DS, DS + XBC, DS + XBC + n_heads, DS + XBC + n_heads + DL
    segs = [(o_xbc, o_xbc + XBC, 0), (o_z, o_z + DS, XBC), (o_gate, o_gate + DL, XBC + DS),
            (o_xl, o_xl + DL, XBC + DS + DL), (o_dt, o_dt + n_heads, NP - LANES)]

    def ref_cols(lo, hi):
        out = []
        while lo < hi:
            k = lo // wb
            e = min(hi, (k + 1) * wb)
            out.append(g_in[k, :, lo - k * wb:e - k * wb])
            lo = e
        return out

    def my_cols(g, lo, hi):
        out = []
        for a, b, m in sorted(segs):
            s, e = max(lo, a), min(hi, b)
            if s < e:
                out.append(g[:, m + s - a:m + e - a])
        return out

    wp = jnp.concatenate([p for a, b, _ in segs for p in ref_cols(a, b)]
                         + [jnp.zeros((D, LANES - n_heads), BF16)], axis=1)
    w_out_f = g_out.reshape(DS + DL, D)
    w_mi_f = jnp.transpose(g_mi, (1, 0, 2)).reshape(D, DFF)
    w_mo_f = g_mo.reshape(DFF, D)

    dt_bias = _pad_lanes(ssd_dt_bias)
    a_log = _pad_lanes(ssd_a_log)
    d_skip = _pad_lanes(ssd_d)
    wa_b, wx_b = lru_w_a[0].astype(BF16), lru_w_x[0].astype(BF16)
    b_a, b_x = lru_b_a.reshape(1, DL), lru_b_x.reshape(1, DL)

    def f_norm_in(i, xv, g):
        return (_rms(xv, g),), ()
    (h,) = _rows_call("norm_in", f_norm_in, nt, [x2, pre_mix_norm], [_rt(tr, D), _full(pre_mix_norm)],
                      [((T, D), BF16, _rt(tr, D))], [])

    (proj,) = _mm("proj", h, wp)

    def f_ssd_pre(i, xbc, halo, dtr, w, b, dtb):
        pre = _conv_pre(xbc, jnp.where(i == 0, 0.0, halo), w, b)
        return (pre * jax.nn.sigmoid(pre), jax.nn.softplus(dtr + dtb)), ()
    xbc_act, dt = _rows_call(
        "ssd_pre", f_ssd_pre, ntw, [proj, proj, proj, conv_s, ssd_conv_b, dt_bias],
        [_rt(trw, XBC), _halo_prev(trw, XBC), _rt(trw, LANES, cb_dt), _full(conv_s), _full(ssd_conv_b), _full(dt_bias)],
        [((T, XBC), F32, _rt(trw, XBC)), ((T, LANES), F32, _rt(trw, LANES))], [])

    y_ssd, h_prev = _ssd_fwd(xbc_act, dt, a_log, d_skip, n_heads, Q)

    gw = DS // SSD_GROUPS

    def ssd_post(y, z, g):
        yz = y * jax.nn.silu(z)
        parts = []
        for k in range(SSD_GROUPS):
            yk = yz[:, k * gw:(k + 1) * gw]
            parts.append(yk * lax.rsqrt(jnp.mean(yk * yk, axis=-1, keepdims=True) + EPS))
        return jnp.concatenate(parts, axis=-1) * g

    def f_ssd_post(i, y, z, g):
        return (ssd_post(y, z, g),), ()
    (mixcat,) = _rows_call("ssd_post", f_ssd_post, nt, [y_ssd, proj, ssd_norm],
                           [_rt(tr, DS), _rt(tr, DS, cb_z), _full(ssd_norm)], [((T, DS + DL), BF16, _rt(tr, DS))], [])

    def f_lru_pre(i, xv, halo, w, b):
        return (_conv_pre(xv, jnp.where(i == 0, 0.0, halo), w, b),), ()
    (xl,) = _rows_call("lru_pre", f_lru_pre, nt, [proj, proj, conv_l, lru_conv_b],
                       [_rt(tr, DL, cb_xl), _halo_prev(tr, DL, cb_xl), _full(conv_l), _full(lru_conv_b)],
                       [((T, DL), F32, _rt(tr, DL))], [])

    a_lru, u_lru = _lru_gates_fwd(xl, wa_b, b_a, wx_b, b_x, lru_lambda, tr)
    h_lru = _lru_scan_fwd(a_lru, u_lru, tr)

    def lru_post(hv, gate, g):
        return _rms(hv * jax.nn.gelu(gate), g)

    def f_lru_post(i, hv, gate, g):
        return (lru_post(hv, gate, g),), ()
    cb_l = DS // DL
    (mixcat,) = _rows_call("lru_post", f_lru_post, nt, [h_lru, proj, lru_norm],
                           [_rt(tr, DL), _rt(tr, DL, cb_gate), _full(lru_norm)],
                           [((T, DS + DL), BF16, _rt(tr, DL, cb_l))], [], into=mixcat)
    (mix,) = _mm("mix", mixcat, w_out_f)

    def f_post_mix(i, xv, mx, gpm, gpl):
        x1 = xv + _rms(mx, gpm)
        return (x1, _rms(x1, gpl)), ()
    x1, hn = _rows_call("post_mix", f_post_mix, nt, [x2, mix, post_mix_norm, pre_mlp_norm],
                        [_rt(tr, D), _rt(tr, D), _full(post_mix_norm), _full(pre_mlp_norm)],
                        [((T, D), F32, _rt(tr, D)), ((T, D), BF16, _rt(tr, D))], [])

    hm, act = _mm("mlp_in", hn, w_mi_f,
                  outs=((F32, None), (BF16, lambda r, e: jnp.square(jnp.maximum(r, 0.0)))))
    (hm2,) = _mm("mlp_out", act, w_mo_f)

    def f_final(i, x1v, hm2v, g, tg):
        def fwd(hv, gv):
            return x1v + _rms(hv, gv)
        x2v, vjp = jax.vjp(fwd, hm2v, g)
        err = x2v - tg
        dx2 = err * (1.0 / D)
        dh, dg = vjp(dx2)
        loss = jnp.full((1, LANES), 0.5 / D, F32) * jnp.sum(err * err)
        return (dx2, dh), (dg, loss)
    dx1a, dhm2, g_post_mlp, loss_part = _rows_call(
        "loss_head", f_final, nt, [x1, hm2, post_mlp_norm, tgt],
        [_rt(tr, D), _rt(tr, D), _full(post_mlp_norm), _rt(tr, D)],
        [((T, D), F32, _rt(tr, D)), ((T, D), BF16, _rt(tr, D))], [(1, D), (1, LANES)])

    (dhm,) = _mm("d_mlp_act", dhm2, w_mo_f, tb=True, extra=hm,
                 outs=((BF16, lambda r, e: r * (2.0 * jnp.maximum(e, 0.0))),))
    (gw_mo,) = _mm("dw_mlp_out", act, dhm2, ta=True, outs=((BF16, None),))
    (dhn,) = _mm("d_mlp_in", dhm, w_mi_f, tb=True)
    (gw_mi,) = _mm("dw_mlp_in", hn, dhm, ta=True, outs=((BF16, None),), out_blocks=N_DEV)

    def f_post_mix_bwd(i, x1v, mx, gpm, gpl, dhnv, dxa):
        _, vjp1 = jax.vjp(_rms, x1v, gpl)
        dx1, dgpl = vjp1(dhnv)
        dx1 = dx1 + dxa
        _, vjp2 = jax.vjp(_rms, mx, gpm)
        dmx, dgpm = vjp2(dx1)
        return (dx1, dmx), (dgpl, dgpm)
    dx1, dmix, g_pre_mlp, g_post_mix = _rows_call(
        "post_mix_bwd", f_post_mix_bwd, nt, [x1, mix, post_mix_norm, pre_mlp_norm, dhn, dx1a],
        [_rt(tr, D), _rt(tr, D), _full(post_mix_norm), _full(pre_mlp_norm), _rt(tr, D), _rt(tr, D)],
        [((T, D), F32, _rt(tr, D)), ((T, D), BF16, _rt(tr, D))], [(1, D), (1, D)])

    (dmixcat,) = _mm("d_mix", dmix, w_out_f, tb=True)
    (gw_out,) = _mm("dw_out", mixcat, dmix, ta=True, outs=((BF16, None),))

    def f_lru_post_bwd(i, hv, gate, g, dy):
        _, vjp = jax.vjp(lru_post, hv, gate, g)
        dh_, dgate, dg = vjp(dy)
        return (dgate, dh_), (dg,)
    dproj, dh_lru, g_lru_norm = _rows_call(
        "lru_post_bwd", f_lru_post_bwd, nt, [h_lru, proj, lru_norm, dmixcat],
        [_rt(tr, DL), _rt(tr, DL, cb_gate), _full(lru_norm), _rt(tr, DL, cb_l)],
        [((T, NP), BF16, _rt(tr, DL, cb_gate)), ((T, DL), F32, _rt(tr, DL))], [(1, DL)])

    du_lru, da_lru = _lru_scan_bwd(a_lru, h_lru, dh_lru, tr)
    dxl, g_wa, g_ba, g_wx, g_bx, g_lam = _lru_gates_bwd(xl, wa_b, b_a, wx_b, b_x, lru_lambda, da_lru, du_lru, tr)

    def f_lru_pre_bwd(i, xv, hp, xn, d, dn, w, b):
        dx, dw8, db = _conv_bwd_tile(i, nt, xv, hp, xn, d, dn, w, b, silu=False)
        return (dx,), (dw8, db)
    dproj, g_convl8, g_convl_b = _rows_call(
        "lru_pre_bwd", f_lru_pre_bwd, nt, [proj, proj, proj, dxl, dxl, conv_l, lru_conv_b],
        [_rt(tr, DL, cb_xl), _halo_prev(tr, DL, cb_xl), _halo_next(tr, DL, nt, cb_xl), _rt(tr, DL),
         _halo_next(tr, DL, nt), _full(conv_l), _full(lru_conv_b)],
        [((T, NP), BF16, _rt(tr, DL, cb_xl))], [(SUBLANES, DL), (1, DL)], into=dproj)

    def f_ssd_post_bwd(i, y, z, g, dy):
        _, vjp = jax.vjp(ssd_post, y, z, g)
        dy_, dz, dg = vjp(dy)
        return (dz, dy_), (dg,)
    dproj, dy_ssd, g_ssd_norm = _rows_call(
        "ssd_post_bwd", f_ssd_post_bwd, nt, [y_ssd, proj, ssd_norm, dmixcat],
        [_rt(tr, DS), _rt(tr, DS, cb_z), _full(ssd_norm), _rt(tr, DS, 0)],
        [((T, NP), BF16, _rt(tr, DS, cb_z)), ((T, DS), F32, _rt(tr, DS))], [(1, DS)], into=dproj)

    dxbc_act, ddt, g_alog, g_dskip = _ssd_bwd(xbc_act, dt, a_log, d_skip, h_prev, dy_ssd, n_heads, Q)

    def f_ssd_pre_bwd(i, xv, hp, xn, d, dn, dtr, ddtv, w, b, dtb):
        dx, dw8, db = _conv_bwd_tile(i, ntw, xv, hp, xn, d, dn, w, b, silu=True)
        ddtr = ddtv * jax.nn.sigmoid(dtr + dtb)
        return (dx, ddtr), (dw8, db, _colsum(ddtr))
    dproj, ddt_raw, g_convs8, g_convs_b, g_dtb = _rows_call(
        "ssd_pre_bwd", f_ssd_pre_bwd, ntw,
        [proj, proj, proj, dxbc_act, dxbc_act, proj, ddt, conv_s, ssd_conv_b, dt_bias],
        [_rt(trw, XBC), _halo_prev(trw, XBC), _halo_next(trw, XBC, ntw), _rt(trw, XBC), _halo_next(trw, XBC, ntw),
         _rt(trw, LANES, cb_dt), _rt(trw, LANES), _full(conv_s), _full(ssd_conv_b), _full(dt_bias)],
        [((T, NP), BF16, _rt(trw, XBC)), ((T, LANES), BF16, _rt(trw, LANES))],
        [(SUBLANES, XBC), (1, XBC), (1, LANES)], into=dproj)

    def f_place(i, v):
        return (v,), ()
    (dproj,) = _rows_call("place_ddt", f_place, 1, [ddt_raw], [_rt(T, LANES)],
                          [((T, NP), BF16, _rt(T, LANES, cb_dt))], [], into=dproj)
    (gwp,) = _mm("dw_proj", h, dproj, ta=True, outs=((BF16, None),))
    (dh,) = _mm("d_proj", dproj, wp, tb=True)

    def f_norm_in_bwd(i, xv, g, dhv, dxa):
        _, vjp = jax.vjp(_rms, xv, g)
        dx, dg = vjp(dhv)
        return (dx + dxa,), (dg,)
    grad_x, g_pre_mix = _rows_call(
        "norm_in_bwd", f_norm_in_bwd, nt, [x2, pre_mix_norm, dh, dx1],
        [_rt(tr, D), _full(pre_mix_norm), _rt(tr, D), _rt(tr, D)], [((T, D), F32, _rt(tr, D))], [(1, D)])

    small = {
        'pre_mix_norm': g_pre_mix, 'ssd_conv_w': g_convs8[:CONV_WIDTH], 'ssd_conv_b': g_convs_b,
        'ssd_dt_bias': g_dtb[:, :n_heads], 'ssd_a_log': g_alog[:, :n_heads], 'ssd_d': g_dskip[:, :n_heads],
        'ssd_norm': g_ssd_norm, 'lru_conv_w': g_convl8[:CONV_WIDTH], 'lru_conv_b': g_convl_b,
        'lru_w_a': g_wa, 'lru_b_a': g_ba, 'lru_w_x': g_wx, 'lru_b_x': g_bx, 'lru_lambda': g_lam,
        'lru_norm': g_lru_norm, 'post_mix_norm': g_post_mix, 'pre_mlp_norm': g_pre_mlp,
        'post_mlp_norm': g_post_mlp, 'loss': loss_part[:, :1],
    }
    wide = ['lru_w_a', 'lru_w_x']
    narrow = [n for n in small if n not in wide]
    lb = lru_w_a.shape[-1]
    g_narrow, g_wa8, g_wx8 = _allgather(
        "allgather_small_grads", [_flat_rows([small[n] for n in narrow]), g_wa.reshape(-1, lb), g_wx.reshape(-1, lb)])
    summed = dict(zip(narrow, _unflat(_sum8("sum_small_grads", g_narrow), [small[n].shape for n in narrow])))
    summed['lru_w_a'] = _sum8("sum_lru_w_a_grads", g_wa8)
    summed['lru_w_x'] = _sum8("sum_lru_w_x_grads", g_wx8)
    loss = summed.pop('loss').reshape(())
    for n, full_w in (('ssd_conv_w', XBC), ('lru_conv_w', DL)):
        wdt = full_w // N_DEV
        summed[n] = lax.dynamic_slice_in_dim(summed[n], dev * wdt, wdt, axis=1)
    small_params = [n for n in names if n not in big]
    as2d = lambda a: a.reshape(-1, a.shape[-1])
    res = _adamw_small([as2d(W[n]) for n in small_params],
                       [summed[n].reshape(as2d(W[n]).shape) for n in small_params],
                       [as2d(Mo[n]) for n in small_params], [as2d(Vo[n]) for n in small_params])
    grads = {n: summed[n].reshape(W[n].shape) for n in small_params}
    delta, new_m, new_v = ({n: r.reshape(W[n].shape) for n, r in zip(small_params, rs)} for rs in res)

    fulls = [jnp.stack([jnp.concatenate(my_cols(gwp, k * wb, (k + 1) * wb), axis=1) for k in range(N_DEV)]),
             gw_out.reshape(N_DEV, -1, D), gw_mi, gw_mo.reshape(N_DEV, DFF // N_DEV, D)]
    from_sib = _rs_sibling("reduce_scatter_sibling", fulls)
    pair = [_pair_sum("pair_sum_" + n, f, r, c_idx) for n, f, r in zip(big, fulls, from_sib)]
    from_chips = _rs_chips("reduce_scatter_chips", pair)
    for n, p, r in zip(big, pair, from_chips):
        grads[n], delta[n], new_m[n], new_v[n] = _adamw_big("adamw_" + n, W[n], Mo[n], Vo[n], p, r, chip_idx)

    return (loss, grad_x.reshape(x.shape), *[grads[n] for n in names], *[delta[n] for n in names],
            *[new_m[n] for n in names], *[new_v[n] for n in names])
```

```python
import functools

import jax
import jax.numpy as jnp
from jax import lax
from jax.experimental import pallas as pl
from jax.experimental.pallas import tpu as pltpu

F32, BF16 = jnp.float32, jnp.bfloat16
S = jax.ShapeDtypeStruct
MESH = pl.DeviceIdType.MESH

SSD_GROUPS = 8
LRU_C = 8.0
EPS = 1e-6
CONV_WIDTH = 4
ADAM_LR, ADAM_B1, ADAM_B2, ADAM_EPS, ADAM_WD, ADAM_STEP = 0.001, 0.9, 0.999, 1e-08, 0.01, 10

LANES = 128
SUBLANES = 8
VMEM_LIMIT = 56 * 1024 * 1024
N_DEV = 8
SMALL_W = 512
HI = lax.Precision.HIGHEST


def _pcall(body, **kw):
    return pl.pallas_call(body, **kw)


def _cparams(sem=None, **kw):
    return pltpu.CompilerParams(dimension_semantics=sem, vmem_limit_bytes=VMEM_LIMIT, **kw)


def _pick(n, cands):
    for c in cands:
        if c <= n and n % c == 0:
            return c
    return n


def _rt(tr, w, cb=0, n=None):
    if n is None:
        return pl.BlockSpec((tr, w), lambda i: (i, cb))
    return pl.BlockSpec((tr, w), lambda i: (n - 1 - i, cb))


def _halo_prev(tr, w, cb=0, n=None):
    k = tr // SUBLANES
    if n is None:
        return pl.BlockSpec((SUBLANES, w), lambda i: (jnp.maximum(i * k - 1, 0), cb))
    return pl.BlockSpec((SUBLANES, w), lambda i: (jnp.maximum((n - 1 - i) * k - 1, 0), cb))


def _halo_next(tr, w, nt, cb=0, n=None):
    k = tr // SUBLANES
    last = nt * k - 1
    if n is None:
        return pl.BlockSpec((SUBLANES, w), lambda i: (jnp.minimum((i + 1) * k, last), cb))
    return pl.BlockSpec((SUBLANES, w), lambda i: (jnp.minimum((n - i) * k, last), cb))


def _full(a):
    nd = a.ndim
    return pl.BlockSpec(a.shape, lambda i: (0,) * nd)


def _rows_call(name, fn, n_tiles, arrays, in_specs, out_tiled, out_acc, into=None, deps=()):
    n_in, n_t = len(arrays), len(out_tiled)
    n_skip = len(deps) + (0 if into is None else 1)

    def body(*refs):
        i = pl.program_id(0)
        touts, aouts = fn(i, *[r[...] for r in refs[:n_in]])
        refs = refs[n_skip:]
        for r, v in zip(refs[n_in:n_in + n_t], touts):
            r[...] = v.astype(r.dtype)
        accs = refs[n_in + n_t:]
        if accs:
            @pl.when(i == 0)
            def _():
                for r in accs:
                    r[...] = jnp.zeros_like(r)
            for r, v in zip(accs, aouts):
                r[...] += v

    out_shape = [S(sh, dt) for sh, dt, _ in out_tiled] + [S(sh, F32) for sh in out_acc]
    out_specs = [sp for _, _, sp in out_tiled]
    for sh in out_acc:
        out_specs.append(pl.BlockSpec(sh, lambda i, nd=len(sh): (0,) * nd))
    in_specs = list(in_specs) + [_ANY] * len(deps)
    if into is None:
        return _pcall(body, name=name, grid=(n_tiles,), in_specs=in_specs, out_specs=out_specs,
                      out_shape=out_shape, compiler_params=_cparams(("arbitrary",)))(*arrays, *deps)
    return _pcall(body, name=name, grid=(n_tiles,), in_specs=in_specs + [_ANY], out_specs=out_specs,
                  out_shape=out_shape, input_output_aliases={n_in + len(deps): 0},
                  compiler_params=_cparams(("arbitrary",)))(*arrays, *deps, into)


def _rms(x, g):
    return x * lax.rsqrt(jnp.mean(x * x, axis=-1, keepdims=True) + EPS) * g


def _colsum(v):
    return jnp.sum(v, axis=0, keepdims=True)


_TILES = (1152, 1024, 896, 768, 640, 512, 384, 256, 128)
_K_TILES = (4096, 3456, 3072, 2688, 2048, 1536, 1344, 1152, 1024, 896, 768, 640, 512, 384, 256, 128)


def _mm(name, a, b, *, ta=False, tb=False, outs=((F32, None),), extra=None, out_blocks=None, tm=None, tn=None, tk=None,
        deps=()):
    M, K = (a.shape[1], a.shape[0]) if ta else a.shape
    b3 = b.ndim == 3
    if b3:
        nb_b, brows, bcols = b.shape
        N = brows if tb else nb_b * bcols
    else:
        N = b.shape[0] if tb else b.shape[1]
    n_lim = N if out_blocks is None else N // out_blocks
    if b3 and not tb:
        n_lim = min(n_lim, bcols)
    tm = tm or _pick(M, _TILES[1:])
    tn = tn or _pick(n_lim, _TILES)
    tk = tk or _pick(bcols if (b3 and tb) else K, _K_TILES)
    nk = K // tk
    assert M % tm == 0 and N % tn == 0 and K % tk == 0
    dn = (((0 if ta else 1,), (1 if tb else 0,)), ((), ()))
    n_extra = 0 if extra is None else 1
    n_out = len(outs)

    def body(*refs):
        a_ref, b_ref = refs[0], refs[1]
        e_ref = refs[2] if n_extra else None
        o_refs = refs[2 + n_extra + len(deps):2 + n_extra + len(deps) + n_out]

        def finish(r):
            e = e_ref[...] if n_extra else None
            for o, (_, f) in zip(o_refs, outs):
                o[...] = (r if f is None else f(r, e)).astype(o.dtype)

        part = lax.dot_general(a_ref[...], b_ref[...], dn, preferred_element_type=F32)
        if nk == 1:
            finish(part)
            return
        acc = refs[-1]
        k = pl.program_id(2)

        @pl.when(k == 0)
        def _():
            acc[...] = part

        @pl.when(jnp.logical_and(k > 0, k < nk - 1))
        def _():
            acc[...] += part

        @pl.when(k == nk - 1)
        def _():
            finish(acc[...] + part)

    a_spec = pl.BlockSpec((tk, tm), lambda i, j, k: (k, i)) if ta else pl.BlockSpec((tm, tk), lambda i, j, k: (i, k))
    if not b3:
        b_spec = pl.BlockSpec((tn, tk), lambda i, j, k: (j, k)) if tb else pl.BlockSpec((tk, tn), lambda i, j, k: (k, j))
    elif tb:
        per = bcols // tk
        b_spec = pl.BlockSpec((None, tn, tk), lambda i, j, k: (k // per, j, k % per))
    else:
        per = bcols // tn
        b_spec = pl.BlockSpec((None, tk, tn), lambda i, j, k: (j // per, k, j % per))
    o_spec = pl.BlockSpec((tm, tn), lambda i, j, k: (i, j))
    if out_blocks is None:
        out_specs, out_shape = [o_spec] * n_out, [S((M, N), dt) for dt, _ in outs]
    else:
        per_o = N // out_blocks // tn
        ob_spec = pl.BlockSpec((None, tm, tn), lambda i, j, k: (j // per_o, i, j % per_o))
        out_specs, out_shape = [ob_spec] * n_out, [S((out_blocks, M, N // out_blocks), dt) for dt, _ in outs]
    in_specs = [a_spec, b_spec] + ([o_spec] if n_extra else []) + [_ANY] * len(deps)
    args = [a, b] + ([extra] if n_extra else []) + list(deps)
    return _pcall(body, name=name, grid=(M // tm, N // tn, nk), in_specs=in_specs, out_specs=out_specs,
                  out_shape=out_shape, scratch_shapes=[pltpu.VMEM((tm, tn), F32)] if nk > 1 else [],
                  compiler_params=_cparams(("parallel", "parallel", "arbitrary")))(*args)


def _shift_down(x, halo, s):
    if s == 0:
        return x
    r = pltpu.roll(x, s, 0)
    hr = pltpu.roll(halo, s, 0)
    row = lax.broadcasted_iota(jnp.int32, halo.shape, 0)
    top = jnp.where(row < s, hr, r[:SUBLANES])
    if x.shape[0] == SUBLANES:
        return top
    return jnp.concatenate([top, r[SUBLANES:]], axis=0)


def _shift_up(x, nxt, s):
    if s == 0:
        return x
    n = x.shape[0]
    r = pltpu.roll(x, n - s, 0)
    nr = pltpu.roll(nxt, SUBLANES - s, 0)
    row = lax.broadcasted_iota(jnp.int32, nxt.shape, 0)
    bot = jnp.where(row >= SUBLANES - s, nr, r[n - SUBLANES:])
    if n == SUBLANES:
        return bot
    return jnp.concatenate([r[:n - SUBLANES], bot], axis=0)


def _conv_pre(x, halo, w, b):
    acc = b + w[CONV_WIDTH - 1:CONV_WIDTH, :] * x
    for k in range(CONV_WIDTH - 1):
        acc = acc + w[k:k + 1, :] * _shift_down(x, halo, CONV_WIDTH - 1 - k)
    return acc


def _silu_grad(p):
    s = jax.nn.sigmoid(p)
    return s * (1.0 + p * (1.0 - s))


def _conv_bwd_tile(i, n_tiles, x, hprev, xnext, d, dnext, w, b, silu):
    hprev = jnp.where(i == 0, 0.0, hprev)
    if silu:
        d = d * _silu_grad(_conv_pre(x, hprev, w, b))
        pre_next = _conv_pre(xnext, x[x.shape[0] - SUBLANES:], w, b)
        dnext = dnext * _silu_grad(pre_next)
    dnext = jnp.where(i == n_tiles - 1, 0.0, dnext)
    dx = w[CONV_WIDTH - 1:CONV_WIDTH, :] * d
    row8 = lax.broadcasted_iota(jnp.int32, (SUBLANES, x.shape[1]), 0)
    dw8 = jnp.where(row8 == CONV_WIDTH - 1, _colsum(d * x), 0.0)
    for k in range(CONV_WIDTH - 1):
        s = CONV_WIDTH - 1 - k
        dx = dx + w[k:k + 1, :] * _shift_up(d, dnext, s)
        dw8 = dw8 + jnp.where(row8 == k, _colsum(d * _shift_down(x, hprev, s)), 0.0)
    return dx, dw8, _colsum(d)


def _ssd_dims(xbc_act, n_heads):
    T, XBC = xbc_act.shape
    GN = XBC // 4
    DS = XBC - 2 * GN
    G = SSD_GROUPS
    N = GN // G
    P = DS // n_heads
    K = n_heads // G
    return T, XBC, DS, GN, G, N, P, K


def _ssd_common(dt, alog, Q):
    a = -jnp.exp(alog)
    adt = dt * a
    li = lax.broadcasted_iota(jnp.int32, (Q, Q), 0)
    si = lax.broadcasted_iota(jnp.int32, (Q, Q), 1)
    causal = li >= si
    ltri = causal.astype(F32)
    acs = jnp.dot(ltri, adt, precision=HI, preferred_element_type=F32)
    acs_row = lax.dot_general(adt, ltri, (((0,), (1,)), ((), ())), precision=HI,
                              preferred_element_type=F32)
    return a, adt, causal, ltri, acs, acs_row


def _expander(g, K, P, W):
    r = lax.broadcasted_iota(jnp.int32, (LANES, W), 0)
    c = lax.broadcasted_iota(jnp.int32, (LANES, W), 1)
    return (c // P + g * K == r).astype(F32)


def _dotb(a, b, dn=(((1,), (0,)), ((), ()))):
    return lax.dot_general(a.astype(BF16), b.astype(BF16), dn, preferred_element_type=F32)


_NT = (((1,), (1,)), ((), ()))
_TN = (((0,), (0,)), ((), ()))


def _ssd_fwd(xbc_act, dt, alog, dskip, n_heads, Q):
    T, XBC, DS, GN, G, N, P, K = _ssd_dims(xbc_act, n_heads)
    W = K * P
    nc = T // Q

    def body(xs_ref, b_ref, c_ref, dt_ref, alog_ref, d_ref, y_ref, hp_ref, h_scr):
        ci = pl.program_id(0)

        @pl.when(ci == 0)
        def _():
            h_scr[...] = jnp.zeros_like(h_scr)

        dtv = dt_ref[...]
        a, adt, causal, ltri, acs, acs_row = _ssd_common(dtv, alog_ref[...], Q)
        lane_head = lax.broadcasted_iota(jnp.int32, (Q, W), 1) // P
        for g in range(G):
            eg = _expander(g, K, P, W)
            dtb = jnp.dot(dtv, eg, precision=HI, preferred_element_type=F32)
            acsb = jnp.dot(acs, eg, precision=HI, preferred_element_type=F32)
            lastb = acsb[Q - 1:Q, :]
            db = jnp.dot(jnp.broadcast_to(d_ref[...], (SUBLANES, LANES)), eg, precision=HI,
                         preferred_element_type=F32)[0:1, :]
            xg = xs_ref[:, g * W:(g + 1) * W]
            bg = b_ref[:, g * N:(g + 1) * N]
            cg = c_ref[:, g * N:(g + 1) * N]
            xt = xg * dtb
            sc = _dotb(cg, bg, _NT)
            yd = jnp.zeros((Q, W), F32)
            for k in range(K):
                h = g * K + k
                seg = acs[:, h:h + 1] - acs_row[h:h + 1, :]
                lh = jnp.where(causal, jnp.exp(jnp.minimum(seg, 0.0)), 0.0)
                xk = jnp.where(lane_head == k, xt, 0.0)
                yd = yd + _dotb(sc * lh, xk)
            hp = h_scr[g]
            yoff = _dotb(cg, hp) * jnp.exp(acsb)
            y_ref[:, g * W:(g + 1) * W] = yd + yoff + xg * db
            e_end = jnp.exp(lastb - acsb)
            st = _dotb(bg, xt * e_end, _TN)
            hp_ref[0, g] = hp
            h_scr[g] = jnp.exp(lastb) * hp + st

    cb = DS // GN
    in_specs = [pl.BlockSpec((Q, DS), lambda c: (c, 0)),
                pl.BlockSpec((Q, GN), lambda c: (c, cb)),
                pl.BlockSpec((Q, GN), lambda c: (c, cb + 1)),
                pl.BlockSpec((Q, LANES), lambda c: (c, 0)),
                pl.BlockSpec((1, LANES), lambda c: (0, 0)),
                pl.BlockSpec((1, LANES), lambda c: (0, 0))]
    out_specs = [pl.BlockSpec((Q, DS), lambda c: (c, 0)),
                 pl.BlockSpec((1, G, N, W), lambda c: (c, 0, 0, 0))]
    return _pcall(body, name="ssd_fwd", grid=(nc,), in_specs=in_specs, out_specs=out_specs,
                  out_shape=[S((T, DS), F32), S((nc, G, N, W), F32)],
                  scratch_shapes=[pltpu.VMEM((G, N, W), F32)],
                  compiler_params=_cparams(("arbitrary",)))(xbc_act, xbc_act, xbc_act, dt, alog, dskip)


def _ssd_bwd(xbc_act, dt, alog, dskip, hprev, dy, n_heads, Q):
    T, XBC, DS, GN, G, N, P, K = _ssd_dims(xbc_act, n_heads)
    W = K * P
    nc = T // Q

    def body(xs_ref, b_ref, c_ref, dt_ref, alog_ref, d_ref, hp_ref, dy_ref,
             dxbc_ref, ddt_ref, dalog_ref, dd_ref, dh_scr):
        ci = pl.program_id(0)

        @pl.when(ci == 0)
        def _():
            dh_scr[...] = jnp.zeros_like(dh_scr)
            dalog_ref[...] = jnp.zeros_like(dalog_ref)
            dd_ref[...] = jnp.zeros_like(dd_ref)

        dtv = dt_ref[...]
        a, adt, causal, ltri, acs, acs_row = _ssd_common(dtv, alog_ref[...], Q)
        lane_head = lax.broadcasted_iota(jnp.int32, (Q, W), 1) // P
        lane128 = lax.broadcasted_iota(jnp.int32, (Q, LANES), 1)
        sub128 = lax.broadcasted_iota(jnp.int32, (LANES, Q), 0)
        rowq = lax.broadcasted_iota(jnp.int32, (Q, W), 0)
        dacs = jnp.zeros((Q, LANES), F32)
        dacs_row = jnp.zeros((LANES, Q), F32)
        ddt = jnp.zeros((Q, LANES), F32)
        dd_acc = jnp.zeros((1, LANES), F32)
        for g in range(G):
            eg = _expander(g, K, P, W)
            dtb = jnp.dot(dtv, eg, precision=HI, preferred_element_type=F32)
            acsb = jnp.dot(acs, eg, precision=HI, preferred_element_type=F32)
            lastb = acsb[Q - 1:Q, :]
            db = jnp.dot(jnp.broadcast_to(d_ref[...], (SUBLANES, LANES)), eg, precision=HI,
                         preferred_element_type=F32)[0:1, :]
            xg = xs_ref[:, g * W:(g + 1) * W]
            bg = b_ref[:, g * N:(g + 1) * N]
            cg = c_ref[:, g * N:(g + 1) * N]
            dyg = dy_ref[:, g * W:(g + 1) * W]
            hp = hp_ref[0, g]
            dhn = dh_scr[g]
            xt = xg * dtb
            sc = _dotb(cg, bg, _NT)
            eacs = jnp.exp(acsb)
            e_end = jnp.exp(lastb - acsb)
            elast = jnp.exp(lastb)

            wv = dyg * eacs
            dcg = _dotb(wv, hp, _NT)
            dhp = _dotb(cg, wv, _TN) + elast * dhn
            dacsb = dyg * (_dotb(cg, hp) * eacs)

            xe = xt * e_end
            dbg = _dotb(xe, dhn, _NT)
            v = _dotb(bg, dhn)
            dxt = v * e_end
            de = v * xe
            dacsb = dacsb - de
            dlastb = _colsum(de) + elast * jnp.sum(dhn * hp, axis=0, keepdims=True)

            dsc = jnp.zeros((Q, Q), F32)
            for k in range(K):
                h = g * K + k
                seg = acs[:, h:h + 1] - acs_row[h:h + 1, :]
                lh = jnp.where(causal, jnp.exp(jnp.minimum(seg, 0.0)), 0.0)
                mh = sc * lh
                dyk = jnp.where(lane_head == k, dyg, 0.0)
                dxt = dxt + jnp.where(lane_head == k, _dotb(mh, dyg, _TN), 0.0)
                dm = _dotb(dyk, xt, _NT)
                dsc = dsc + dm * lh
                gm = dm * mh
                dacs = dacs + jnp.where(lane128 == h, jnp.sum(gm, axis=1, keepdims=True), 0.0)
                dacs_row = dacs_row - jnp.where(sub128 == h, jnp.sum(gm, axis=0, keepdims=True), 0.0)
            dcg = dcg + _dotb(dsc, bg)
            dbg = dbg + _dotb(dsc, cg, _TN)

            dacsb = dacsb + jnp.where(rowq == Q - 1, dlastb, 0.0)
            dacs = dacs + lax.dot_general(dacsb, eg, _NT, precision=HI, preferred_element_type=F32)
            ddt = ddt + lax.dot_general(dxt * xg, eg, _NT, precision=HI, preferred_element_type=F32)
            dd_acc = dd_acc + lax.dot_general(jnp.broadcast_to(_colsum(dyg * xg), (SUBLANES, W)), eg, _NT,
                                              precision=HI, preferred_element_type=F32)[0:1, :]
            dxbc_ref[:, g * W:(g + 1) * W] = dxt * dtb + dyg * db
            dxbc_ref[:, DS + g * N:DS + (g + 1) * N] = dbg
            dxbc_ref[:, DS + GN + g * N:DS + GN + (g + 1) * N] = dcg
            dh_scr[g] = dhp

        eye = (lax.broadcasted_iota(jnp.int32, (LANES, LANES), 0) ==
               lax.broadcasted_iota(jnp.int32, (LANES, LANES), 1)).astype(F32)
        dacs = dacs + lax.dot_general(dacs_row, eye, _TN, precision=HI, preferred_element_type=F32)
        dadt = lax.dot_general(ltri, dacs, _TN, precision=HI, preferred_element_type=F32)
        ddt_ref[...] = ddt + dadt * a
        dalog_ref[...] += _colsum(dadt * dtv) * a
        dd_ref[...] += dd_acc

    cb = DS // GN
    rv = lambda c: nc - 1 - c
    in_specs = [pl.BlockSpec((Q, DS), lambda c: (rv(c), 0)),
                pl.BlockSpec((Q, GN), lambda c: (rv(c), cb)),
                pl.BlockSpec((Q, GN), lambda c: (rv(c), cb + 1)),
                pl.BlockSpec((Q, LANES), lambda c: (rv(c), 0)),
                pl.BlockSpec((1, LANES), lambda c: (0, 0)),
                pl.BlockSpec((1, LANES), lambda c: (0, 0)),
                pl.BlockSpec((1, G, N, W), lambda c: (rv(c), 0, 0, 0)),
                pl.BlockSpec((Q, DS), lambda c: (rv(c), 0))]
    out_specs = [pl.BlockSpec((Q, XBC), lambda c: (rv(c), 0)),
                 pl.BlockSpec((Q, LANES), lambda c: (rv(c), 0)),
                 pl.BlockSpec((1, LANES), lambda c: (0, 0)),
                 pl.BlockSpec((1, LANES), lambda c: (0, 0))]
    return _pcall(body, name="ssd_bwd", grid=(nc,), in_specs=in_specs, out_specs=out_specs,
                  out_shape=[S((T, XBC), F32), S((T, LANES), F32), S((1, LANES), F32), S((1, LANES), F32)],
                  scratch_shapes=[pltpu.VMEM((G, N, W), F32)],
                  compiler_params=_cparams(("arbitrary",)))(
                      xbc_act, xbc_act, xbc_act, dt, alog, dskip, hprev, dy)


def _blockdiag(x, w_ref, dn=(((1,), (0,)), ((), ()))):
    H, B, _ = w_ref.shape
    return jnp.concatenate([_dotb(x[:, h * B:(h + 1) * B], w_ref[h], dn) for h in range(H)], axis=1)


def _lru_elem(xl, r_pre, i_pre, lam):
    r = jax.nn.sigmoid(r_pre)
    i = jax.nn.sigmoid(i_pre)
    log_a = -LRU_C * r * jax.nn.softplus(-lam)
    a = jnp.exp(log_a)
    u = jnp.sqrt(1.0 - jnp.exp(2.0 * log_a)) * (i * xl)
    return a, u


def _lru_gates_fwd(xl, w_a, b_a, w_x, b_x, lam, tr):
    T, DL = xl.shape

    def body(xl_ref, wa_ref, ba_ref, wx_ref, bx_ref, lam_ref, a_ref, u_ref):
        x = xl_ref[...]
        r_pre = _blockdiag(x, wa_ref) + ba_ref[...]
        i_pre = _blockdiag(x, wx_ref) + bx_ref[...]
        a, u = _lru_elem(x, r_pre, i_pre, lam_ref[...])
        a_ref[...] = a
        u_ref[...] = u

    w3 = pl.BlockSpec(w_a.shape, lambda i: (0, 0, 0))
    vec = pl.BlockSpec((1, DL), lambda i: (0, 0))
    return _pcall(body, name="lru_gates_fwd", grid=(T // tr,),
                  in_specs=[_rt(tr, DL), w3, vec, w3, vec, vec],
                  out_specs=[_rt(tr, DL), _rt(tr, DL)], out_shape=[S((T, DL), F32), S((T, DL), F32)],
                  compiler_params=_cparams(("parallel",)))(xl, w_a, b_a, w_x, b_x, lam)


def _lru_gates_bwd(xl, w_a, b_a, w_x, b_x, lam, da, du, tr):
    T, DL = xl.shape
    H, B, _ = w_a.shape

    def body(xl_ref, wa_ref, ba_ref, wx_ref, bx_ref, lam_ref, da_ref, du_ref,
             dxl_ref, dwa_ref, dba_ref, dwx_ref, dbx_ref, dlam_ref):
        @pl.when(pl.program_id(0) == 0)
        def _():
            for r in (dwa_ref, dba_ref, dwx_ref, dbx_ref, dlam_ref):
                r[...] = jnp.zeros_like(r)

        x = xl_ref[...]
        r_pre = _blockdiag(x, wa_ref) + ba_ref[...]
        i_pre = _blockdiag(x, wx_ref) + bx_ref[...]
        _, vjp = jax.vjp(_lru_elem, x, r_pre, i_pre, lam_ref[...])
        dx, dr, di, dlam = vjp((da_ref[...], du_ref[...]))
        dxl_ref[...] = dx + _blockdiag(dr, wa_ref, _NT) + _blockdiag(di, wx_ref, _NT)
        for h in range(H):
            xh = x[:, h * B:(h + 1) * B]
            dwa_ref[h] += _dotb(xh, dr[:, h * B:(h + 1) * B], _TN)
            dwx_ref[h] += _dotb(xh, di[:, h * B:(h + 1) * B], _TN)
        dba_ref[...] += _colsum(dr)
        dbx_ref[...] += _colsum(di)
        dlam_ref[...] += dlam

    w3 = pl.BlockSpec(w_a.shape, lambda i: (0, 0, 0))
    vec = pl.BlockSpec((1, DL), lambda i: (0, 0))
    return _pcall(body, name="lru_gates_bwd", grid=(T // tr,),
                  in_specs=[_rt(tr, DL), w3, vec, w3, vec, vec, _rt(tr, DL), _rt(tr, DL)],
                  out_specs=[_rt(tr, DL), w3, vec, w3, vec, vec],
                  out_shape=[S((T, DL), F32), S(w_a.shape, F32), S((1, DL), F32), S(w_a.shape, F32),
                             S((1, DL), F32), S((1, DL), F32)],
                  compiler_params=_cparams(("arbitrary",)))(xl, w_a, b_a, w_x, b_x, lam, da, du)


def _scan_tile(a, u, up):
    n = a.shape[0]
    row = lax.broadcasted_iota(jnp.int32, a.shape, 0)
    d = 1
    while d < n:
        if up:
            keep = row < n - d
            a_s = jnp.where(keep, pltpu.roll(a, n - d, 0), 1.0)
            u_s = jnp.where(keep, pltpu.roll(u, n - d, 0), 0.0)
        else:
            keep = row >= d
            a_s = jnp.where(keep, pltpu.roll(a, d, 0), 1.0)
            u_s = jnp.where(keep, pltpu.roll(u, d, 0), 0.0)
        u = a * u_s + u
        a = a * a_s
        d *= 2
    return a, u


def _lru_scan_fwd(a, u, tr):
    T, DL = a.shape

    def body(a_ref, u_ref, h_ref, carry):
        @pl.when(pl.program_id(0) == 0)
        def _():
            carry[...] = jnp.zeros_like(carry)

        A, U = _scan_tile(a_ref[...], u_ref[...], up=False)
        h = U + A * carry[0:1, :]
        h_ref[...] = h
        carry[...] = jnp.broadcast_to(h[tr - 1:tr, :], carry.shape)

    return _pcall(body, name="lru_scan_fwd", grid=(T // tr,), in_specs=[_rt(tr, DL), _rt(tr, DL)],
                  out_specs=_rt(tr, DL), out_shape=S((T, DL), F32),
                  scratch_shapes=[pltpu.VMEM((SUBLANES, DL), F32)],
                  compiler_params=_cparams(("arbitrary",)))(a, u)


def _lru_scan_bwd(a, h, dh, tr):
    T, DL = a.shape
    n = T // tr

    def body(a_ref, an_ref, h_ref, hp_ref, dh_ref, du_ref, da_ref, carry):
        i = pl.program_id(0)
        ti = n - 1 - i

        @pl.when(i == 0)
        def _():
            carry[...] = jnp.zeros_like(carry)

        a_next = _shift_up(a_ref[...], jnp.where(ti == n - 1, 0.0, an_ref[...]), 1)
        A, U = _scan_tile(a_next, dh_ref[...], up=True)
        g = U + A * carry[0:1, :]
        du_ref[...] = g
        h_prev = _shift_down(h_ref[...], jnp.where(ti == 0, 0.0, hp_ref[...]), 1)
        da_ref[...] = g * h_prev
        carry[...] = jnp.broadcast_to(g[0:1, :], carry.shape)

    return _pcall(body, name="lru_scan_bwd", grid=(n,),
                  in_specs=[_rt(tr, DL, 0, n), _halo_next(tr, DL, n, 0, n), _rt(tr, DL, 0, n),
                            _halo_prev(tr, DL, 0, n), _rt(tr, DL, 0, n)],
                  out_specs=[_rt(tr, DL, 0, n), _rt(tr, DL, 0, n)],
                  out_shape=[S((T, DL), F32), S((T, DL), F32)],
                  scratch_shapes=[pltpu.VMEM((SUBLANES, DL), F32)],
                  compiler_params=_cparams(("arbitrary",)))(a, a, h, h, dh)


def _adamw(w, g, m, v):
    m = ADAM_B1 * m + (1.0 - ADAM_B1) * g
    v = ADAM_B2 * v + (1.0 - ADAM_B2) * (g * g)
    m_hat = m / (1.0 - ADAM_B1 ** ADAM_STEP)
    v_hat = v / (1.0 - ADAM_B2 ** ADAM_STEP)
    delta = -ADAM_LR * (m_hat / (jnp.sqrt(v_hat) + ADAM_EPS) + ADAM_WD * w)
    return delta, m, v


def _adamw_big(name, w, m, v, part, recv, chip_idx):
    _, R, C = w.shape
    tr = _pick(R, (256, 128, 64, 32, 16))

    def body(ci_ref, w_ref, m_ref, v_ref, p_ref, r0, r1, r2, g_ref, d_ref, nm_ref, nv_ref):
        g = ((p_ref[...].astype(F32) + r0[...].astype(F32)) + r1[...].astype(F32)) + r2[...].astype(F32)
        d, nm, nv = _adamw(w_ref[...], g, m_ref[...], v_ref[...])
        g_ref[...] = g
        d_ref[...] = d
        nm_ref[...] = nm
        nv_ref[...] = nv

    r_spec = lambda s: pl.BlockSpec((None, tr, C), lambda i, ci: (s, i, 0))
    t2 = r_spec(0)
    gs = pltpu.PrefetchScalarGridSpec(
        num_scalar_prefetch=1, grid=(R // tr,),
        in_specs=[t2, t2, t2, pl.BlockSpec((None, tr, C), lambda i, ci: (ci[0], i, 0)),
                  r_spec(0), r_spec(1), r_spec(2)],
        out_specs=[t2, t2, t2, t2])
    return _pcall(body, name=name, grid_spec=gs, out_shape=[S((1, R, C), F32)] * 4,
                  compiler_params=_cparams(("parallel",)))(chip_idx, w, m, v, part, recv, recv, recv)


def _adamw_small(ws, gs, ms, vs):
    n = len(ws)

    def body(*refs):
        for k in range(n):
            d, nm, nv = _adamw(refs[k][...], refs[n + k][...], refs[2 * n + k][...], refs[3 * n + k][...])
            refs[4 * n + k][...] = d
            refs[5 * n + k][...] = nm
            refs[6 * n + k][...] = nv

    res = _pcall(body, name="adamw_small", out_shape=[S(w.shape, F32) for w in ws] * 3,
                 compiler_params=_cparams())(*ws, *gs, *ms, *vs)
    return res[:n], res[n:2 * n], res[2 * n:]


def _sum8(name, parts):
    _, R, C = parts.shape

    def body(p_ref, o_ref):
        acc = p_ref[0]
        for k in range(1, N_DEV):
            acc = acc + p_ref[k]
        o_ref[...] = acc

    return _pcall(body, name=name, out_shape=S((R, C), F32), compiler_params=_cparams())(parts)


def _pair_sum(name, full, recv, c_idx):
    _, R, C = full.shape
    tr = _pick(R, (256, 128, 64, 32, 16))

    def body(c_ref, f_ref, r_ref, o_ref):
        o_ref[...] = (f_ref[...].astype(F32) + r_ref[...].astype(F32)).astype(o_ref.dtype)

    gs = pltpu.PrefetchScalarGridSpec(
        num_scalar_prefetch=1, grid=(4, R // tr),
        in_specs=[pl.BlockSpec((None, tr, C), lambda j, i, c: (2 * j + c[0], i, 0)),
                  pl.BlockSpec((None, tr, C), lambda j, i, c: (j, i, 0))],
        out_specs=pl.BlockSpec((None, tr, C), lambda j, i, c: (j, i, 0)))
    return _pcall(body, name=name, grid_spec=gs, out_shape=S((4, R, C), BF16),
                  compiler_params=_cparams(("parallel", "parallel")))(c_idx, full, recv)


def _cast_bf16(name, w):
    _, R, C = w.shape
    tr = _pick(R, (256, 128, 64, 32, 16))

    def body(w_ref, o_ref):
        o_ref[...] = w_ref[...].astype(BF16)

    sp = pl.BlockSpec((tr, C), lambda i: (i, 0))
    return _pcall(body, name=name, grid=(R // tr,), in_specs=[pl.BlockSpec((None, tr, C), lambda i: (0, i, 0))],
                  out_specs=sp, out_shape=S((R, C), BF16),
                  compiler_params=_cparams(("parallel",)))(w)


_ANY = pl.BlockSpec(memory_space=pl.ANY)


def _position():
    return lax.axis_index("x"), lax.axis_index("y"), lax.axis_index("c")


def _allgather(name, shards):
    n = len(shards)

    def body(*refs):
        ins, outs = refs[:n], refs[n:2 * n]
        send, recv, lsem = refs[2 * n:]
        x, y, c = _position()
        me, sib = (x, y, c), (x, y, 1 - c)
        chips = [(1 - x, y), (x, 1 - y), (1 - x, 1 - y)]

        def copy(a, k, block, to, src=None):
            bx, by, bc = block
            dst = outs[a].at[4 * bx + 2 * by + bc]
            return pltpu.make_async_remote_copy(
                src_ref=dst if src is None else src, dst_ref=dst, send_sem=send.at[a, k], recv_sem=recv.at[a, k],
                device_id=to, device_id_type=MESH)

        mine = [pltpu.make_async_copy(ins[a], outs[a].at[4 * x + 2 * y + c], lsem.at[a]) for a in range(n)]
        for cp in mine:
            cp.start()
        first = []
        for a in range(n):
            first.append(copy(a, 0, me, sib, src=ins[a]))
            first += [copy(a, 1 + j, me, (*chip, c), src=ins[a]) for j, chip in enumerate(chips)]
        for cp in first:
            cp.start()
        passed = []
        for j, chip in enumerate(chips):
            for a in range(n):
                copy(a, 1 + j, (*chip, c), me).wait_recv()
                cp = copy(a, 4 + j, (*chip, c), sib)
                cp.start()
                passed.append(cp)
        for a in range(n):
            copy(a, 0, sib, me).wait_recv()
        for j, chip in enumerate(chips):
            for a in range(n):
                copy(a, 4 + j, (*chip, 1 - c), me).wait_recv()
        for cp in first + passed:
            cp.wait_send()
        for cp in mine:
            cp.wait()

    return _pcall(body, name=name, in_specs=[_ANY] * n, out_specs=[_ANY] * n,
                  out_shape=[S((N_DEV,) + s.shape, s.dtype) for s in shards],
                  scratch_shapes=[pltpu.SemaphoreType.DMA((n, 7)), pltpu.SemaphoreType.DMA((n, 7)),
                                  pltpu.SemaphoreType.DMA((n,))])(*shards)


def _rs_sibling(name, fulls):
    n = len(fulls)

    def body(*refs):
        ins, outs = refs[:n], refs[n:2 * n]
        send, recv = refs[2 * n:]
        x, y, c = _position()
        copies = []
        for a in range(n):
            for j in range(4):
                copies.append(pltpu.make_async_remote_copy(
                    src_ref=ins[a].at[2 * j + (1 - c)], dst_ref=outs[a].at[j], send_sem=send.at[a, j],
                    recv_sem=recv.at[a, j], device_id=(x, y, 1 - c), device_id_type=MESH))
        for cp in copies:
            cp.start()
        for cp in copies:
            cp.wait()

    return _pcall(body, name=name, in_specs=[_ANY] * n, out_specs=[_ANY] * n,
                  out_shape=[S((4,) + f.shape[1:], f.dtype) for f in fulls],
                  scratch_shapes=[pltpu.SemaphoreType.DMA((n, 4)), pltpu.SemaphoreType.DMA((n, 4))])(*fulls)


_HBM = pl.BlockSpec(memory_space=pltpu.HBM)
_SEM = pl.BlockSpec(memory_space=pltpu.SEMAPHORE)
_EFFECT = pltpu.SideEffectType.DATAFLOW_SIDE_EFFECTING


def _remote_copies(copies_fn, srcs, lands, send, recv):
    x, y, c = _position()
    return [pltpu.make_async_remote_copy(src_ref=s, dst_ref=d, send_sem=send[i], recv_sem=recv[i], device_id=to,
                                         device_id_type=MESH)
            for i, (s, d, to) in enumerate(copies_fn(x, y, c, srcs, lands))]


def _split_start(name, srcs, land_shapes, copies_fn, nc):
    n, nl = len(srcs), len(land_shapes)

    def body(*refs):
        src_refs, land_refs = refs[:n], refs[n:n + nl]
        outs = refs[n + nl:]
        for cp in _remote_copies(copies_fn, src_refs, land_refs, outs[:nc], outs[nc:2 * nc]):
            cp.start()
        outs[-1][...] = jnp.zeros_like(outs[-1])

    hbm = lambda a: pltpu.with_memory_space_constraint(a, pltpu.HBM)
    res = _pcall(
        body, name=name, in_specs=[_HBM] * (n + nl),
        out_specs=[_SEM] * (2 * nc) + [_HBM] * (n + nl) + [pl.BlockSpec(memory_space=pltpu.VMEM)],
        out_shape=[pltpu.SemaphoreType.DMA(())] * (2 * nc) + [pltpu.HBM(s.shape, s.dtype) for s in srcs]
        + [pltpu.HBM(sh, dt) for sh, dt in land_shapes] + [S((SUBLANES, LANES), F32)],
        input_output_aliases={i: 2 * nc + i for i in range(n + nl)},
        compiler_params=pltpu.CompilerParams(has_side_effects=_EFFECT),
    )(*[hbm(s) for s in srcs], *[hbm(lax.empty(sh, dt)) for sh, dt in land_shapes])
    return res[:2 * nc], res[2 * nc:2 * nc + n], res[2 * nc + n:2 * nc + n + nl], res[-1]


def _split_wait(name, sems, srcs, lands, after, copies_fn, nc):
    n, nl = len(srcs), len(lands)

    def body(*refs):
        src_refs, land_refs = refs[:n], refs[n:n + nl]
        sem_refs = refs[n + nl:n + nl + 2 * nc]
        for cp in _remote_copies(copies_fn, src_refs, land_refs, sem_refs[:nc], sem_refs[nc:]):
            cp.wait_send()
            cp.wait_recv()

    res = _pcall(
        body, name=name, in_specs=[_HBM] * (n + nl) + [_SEM] * (2 * nc) + [_ANY],
        out_specs=[_HBM] * (n + nl), out_shape=[pltpu.HBM(a.shape, a.dtype) for a in list(srcs) + list(lands)],
        input_output_aliases={i: i for i in range(n + nl)},
        compiler_params=pltpu.CompilerParams(has_side_effects=_EFFECT),
    )(*srcs, *lands, *sems, after)
    return res[:n], res[n:]


def _other_chips(x, y):
    return [(1 - x, y), (x, 1 - y), (1 - x, 1 - y)]


def _ag_copies(x, y, c, srcs, lands):
    out = []
    for s, land in zip(srcs, lands):
        dst = land.at[4 * x + 2 * y + c]
        out.append((s, dst, (x, y, 1 - c)))
        out += [(s, dst, (px, py, c)) for px, py in _other_chips(x, y)]
    return out


def _rs_copies(x, y, c, srcs, lands):
    return [(s.at[2 * px + py], land.at[j], (px, py, c))
            for s, land in zip(srcs, lands) for j, (px, py) in enumerate(_other_chips(x, y))]


def _ag_finish(name, shards, lands):
    n = len(shards)

    def body(*refs):
        ins, outs = refs[:n], refs[2 * n:3 * n]
        send, recv, lsem = refs[3 * n:]
        x, y, c = _position()
        mine = [pltpu.make_async_copy(ins[a], outs[a].at[4 * x + 2 * y + c], lsem.at[a]) for a in range(n)]
        for cp in mine:
            cp.start()

        def swap(a, j, px, py, pc):
            blk = outs[a].at[4 * px + 2 * py + pc]
            return pltpu.make_async_remote_copy(src_ref=blk, dst_ref=blk, send_sem=send.at[a, j], recv_sem=recv.at[a, j],
                                                device_id=(x, y, 1 - c), device_id_type=MESH)

        chips = _other_chips(x, y)
        sends = [swap(a, j, px, py, c) for a in range(n) for j, (px, py) in enumerate(chips)]
        for cp in sends:
            cp.start()
        for a in range(n):
            for j, (px, py) in enumerate(chips):
                swap(a, j, px, py, 1 - c).wait_recv()
        for cp in sends:
            cp.wait_send()
        for cp in mine:
            cp.wait()

    return _pcall(body, name=name, in_specs=[_ANY] * (2 * n), out_specs=[_ANY] * n,
                  out_shape=[S(l.shape, l.dtype) for l in lands], input_output_aliases={n + a: a for a in range(n)},
                  scratch_shapes=[pltpu.SemaphoreType.DMA((n, 3)), pltpu.SemaphoreType.DMA((n, 3)),
                                  pltpu.SemaphoreType.DMA((n,))])(*shards, *lands)


def _pad_lanes(v):
    return jnp.pad(v, ((0, 0), (0, LANES - v.shape[1])))


def _flat_rows(pieces):
    flat = jnp.concatenate([p.reshape(-1) for p in pieces])
    rows = -(-flat.shape[0] // (SMALL_W * SUBLANES)) * SUBLANES
    return jnp.pad(flat, (0, rows * SMALL_W - flat.shape[0])).reshape(rows, SMALL_W)


def _unflat(buf, shapes):
    flat = buf.reshape(-1)
    out, off = [], 0
    for sh in shapes:
        n = 1
        for d in sh:
            n *= d
        out.append(flat[off:off + n].reshape(sh))
        off += n
    return out


def kernel(x, pre_mix_norm, w_in, ssd_conv_w, ssd_conv_b, ssd_dt_bias, ssd_a_log, ssd_d, ssd_norm, lru_conv_w, lru_conv_b, lru_w_a, lru_b_a, lru_w_x, lru_b_x, lru_lambda, lru_norm, w_out, post_mix_norm, pre_mlp_norm, w_mlp_in, w_mlp_out, post_mlp_norm, loss_target, m_pre_mix_norm, m_w_in, m_ssd_conv_w, m_ssd_conv_b, m_ssd_dt_bias, m_ssd_a_log, m_ssd_d, m_ssd_norm, m_lru_conv_w, m_lru_conv_b, m_lru_w_a, m_lru_b_a, m_lru_w_x, m_lru_b_x, m_lru_lambda, m_lru_norm, m_w_out, m_post_mix_norm, m_pre_mlp_norm, m_w_mlp_in, m_w_mlp_out, m_post_mlp_norm, v_pre_mix_norm, v_w_in, v_ssd_conv_w, v_ssd_conv_b, v_ssd_dt_bias, v_ssd_a_log, v_ssd_d, v_ssd_norm, v_lru_conv_w, v_lru_conv_b, v_lru_w_a, v_lru_b_a, v_lru_w_x, v_lru_b_x, v_lru_lambda, v_lru_norm, v_w_out, v_post_mix_norm, v_pre_mlp_norm, v_w_mlp_in, v_w_mlp_out, v_post_mlp_norm):
    names = ['pre_mix_norm', 'w_in', 'ssd_conv_w', 'ssd_conv_b', 'ssd_dt_bias', 'ssd_a_log', 'ssd_d', 'ssd_norm',
             'lru_conv_w', 'lru_conv_b', 'lru_w_a', 'lru_b_a', 'lru_w_x', 'lru_b_x', 'lru_lambda', 'lru_norm',
             'w_out', 'post_mix_norm', 'pre_mlp_norm', 'w_mlp_in', 'w_mlp_out', 'post_mlp_norm']
    loc = locals()
    W = {n: loc[n] for n in names}
    Mo = {n: loc["m_" + n] for n in names}
    Vo = {n: loc["v_" + n] for n in names}
    big = ['w_in', 'w_out', 'w_mlp_in', 'w_mlp_out']

    px, py, pc = _position()
    dev = 4 * px + 2 * py + pc
    c_idx = jnp.reshape(pc, (1,)).astype(jnp.int32)
    chip_idx = jnp.reshape(2 * px + py, (1,)).astype(jnp.int32)

    _, T, D = x.shape
    x2 = x.reshape(T, D)
    tgt = loss_target.reshape(T, D)
    n_heads = ssd_dt_bias.shape[1]
    XBC = ssd_conv_b.shape[1]
    GN = XBC // 4
    DS = XBC - 2 * GN
    DL = lru_norm.shape[1]
    DFF = w_mlp_in.shape[2] * N_DEV
    DIN = w_in.shape[2] * N_DEV
    NP = XBC + DS + 2 * DL + LANES
    assert DS % GN == 0 and XBC % DS == 0 and DS == DL and n_heads <= LANES
    cb_z, cb_gate, cb_xl, cb_dt = XBC // DS, XBC // DS + 1, XBC // DS + 2, (XBC + DS + 2 * DL) // LANES
    tr = min(256, T // 2)
    trw = min(128, T // 2)
    nt, ntw = T // tr, T // trw
    Q = min(128, T // 2)

    sh = {n: _cast_bf16("cast_" + n, W[n]) for n in big}
    g_in, g_cs, g_cl = _allgather("allgather_w_in", [sh['w_in'], ssd_conv_w[0], lru_conv_w[0]])
    later = big[1:]
    ag_sems, ag_srcs, ag_lands, ag_token = _split_start(
        "allgather_later_start", [sh[n] for n in later], [((N_DEV,) + sh[n].shape, BF16) for n in later],
        _ag_copies, 4 * len(later))
    conv_s = jnp.transpose(g_cs, (1, 0, 2)).reshape(CONV_WIDTH, XBC)
    conv_l = jnp.transpose(g_cl, (1, 0, 2)).reshape(CONV_WIDTH, DL)
    wb = DIN // N_DEV
    o_z, o_xbc, o_dt, o_gate, o_xl = 0, DS, DS + XBC, DS + XBC + n_heads, DS + XBC + n_heads + DL
    segs = [(o_xbc, o_xbc + XBC, 0), (o_z, o_z + DS, XBC), (o_gate, o_gate + DL, XBC + DS),
            (o_xl, o_xl + DL, XBC + DS + DL), (o_dt, o_dt + n_heads, NP - LANES)]

    def ref_cols(lo, hi):
        out = []
        while lo < hi:
            k = lo // wb
            e = min(hi, (k + 1) * wb)
            out.append(g_in[k, :, lo - k * wb:e - k * wb])
            lo = e
        return out

    def my_cols(g, lo, hi):
        out = []
        for a, b, m in sorted(segs):
            s, e = max(lo, a), min(hi, b)
            if s < e:
                out.append(g[:, m + s - a:m + e - a])
        return out

    wp = jnp.concatenate([p for a, b, _ in segs for p in ref_cols(a, b)]
                         + [jnp.zeros((D, LANES - n_heads), BF16)], axis=1)
    dt_bias = _pad_lanes(ssd_dt_bias)
    a_log = _pad_lanes(ssd_a_log)
    d_skip = _pad_lanes(ssd_d)
    wa_b, wx_b = lru_w_a[0].astype(BF16), lru_w_x[0].astype(BF16)
    b_a, b_x = lru_b_a.reshape(1, DL), lru_b_x.reshape(1, DL)

    def f_norm_in(i, xv, g):
        return (_rms(xv, g),), ()
    (h,) = _rows_call("norm_in", f_norm_in, nt, [x2, pre_mix_norm], [_rt(tr, D), _full(pre_mix_norm)],
                      [((T, D), BF16, _rt(tr, D))], [], deps=[ag_token])

    (proj,) = _mm("proj", h, wp)

    def f_ssd_pre(i, xbc, halo, dtr, w, b, dtb):
        pre = _conv_pre(xbc, jnp.where(i == 0, 0.0, halo), w, b)
        return (pre * jax.nn.sigmoid(pre), jax.nn.softplus(dtr + dtb)), ()
    xbc_act, dt = _rows_call(
        "ssd_pre", f_ssd_pre, ntw, [proj, proj, proj, conv_s, ssd_conv_b, dt_bias],
        [_rt(trw, XBC), _halo_prev(trw, XBC), _rt(trw, LANES, cb_dt), _full(conv_s), _full(ssd_conv_b), _full(dt_bias)],
        [((T, XBC), F32, _rt(trw, XBC)), ((T, LANES), F32, _rt(trw, LANES))], [])

    y_ssd, h_prev = _ssd_fwd(xbc_act, dt, a_log, d_skip, n_heads, Q)

    gw = DS // SSD_GROUPS

    def ssd_post(y, z, g):
        yz = y * jax.nn.silu(z)
        parts = []
        for k in range(SSD_GROUPS):
            yk = yz[:, k * gw:(k + 1) * gw]
            parts.append(yk * lax.rsqrt(jnp.mean(yk * yk, axis=-1, keepdims=True) + EPS))
        return jnp.concatenate(parts, axis=-1) * g

    def f_ssd_post(i, y, z, g):
        return (ssd_post(y, z, g),), ()
    (mixcat,) = _rows_call("ssd_post", f_ssd_post, nt, [y_ssd, proj, ssd_norm],
                           [_rt(tr, DS), _rt(tr, DS, cb_z), _full(ssd_norm)], [((T, DS + DL), BF16, _rt(tr, DS))], [])

    def f_lru_pre(i, xv, halo, w, b):
        return (_conv_pre(xv, jnp.where(i == 0, 0.0, halo), w, b),), ()
    (xl,) = _rows_call("lru_pre", f_lru_pre, nt, [proj, proj, conv_l, lru_conv_b],
                       [_rt(tr, DL, cb_xl), _halo_prev(tr, DL, cb_xl), _full(conv_l), _full(lru_conv_b)],
                       [((T, DL), F32, _rt(tr, DL))], [])

    a_lru, u_lru = _lru_gates_fwd(xl, wa_b, b_a, wx_b, b_x, lru_lambda, tr)
    h_lru = _lru_scan_fwd(a_lru, u_lru, tr)

    def lru_post(hv, gate, g):
        return _rms(hv * jax.nn.gelu(gate), g)

    def f_lru_post(i, hv, gate, g):
        return (lru_post(hv, gate, g),), ()
    cb_l = DS // DL
    (mixcat,) = _rows_call("lru_post", f_lru_post, nt, [h_lru, proj, lru_norm],
                           [_rt(tr, DL), _rt(tr, DL, cb_gate), _full(lru_norm)],
                           [((T, DS + DL), BF16, _rt(tr, DL, cb_l))], [], into=mixcat)

    ag_srcs, ag_lands = _split_wait("allgather_later_wait", ag_sems, ag_srcs, ag_lands, mixcat, _ag_copies,
                                    4 * len(later))
    g_out, g_mi, g_mo = _ag_finish("allgather_later_finish", ag_srcs, ag_lands)
    w_out_f = g_out.reshape(DS + DL, D)
    w_mi_f = jnp.transpose(g_mi, (1, 0, 2)).reshape(D, DFF)
    w_mo_f = g_mo.reshape(DFF, D)
    (mix,) = _mm("mix", mixcat, w_out_f)

    def f_post_mix(i, xv, mx, gpm, gpl):
        x1 = xv + _rms(mx, gpm)
        return (x1, _rms(x1, gpl)), ()
    x1, hn = _rows_call("post_mix", f_post_mix, nt, [x2, mix, post_mix_norm, pre_mlp_norm],
                        [_rt(tr, D), _rt(tr, D), _full(post_mix_norm), _full(pre_mlp_norm)],
                        [((T, D), F32, _rt(tr, D)), ((T, D), BF16, _rt(tr, D))], [])

    hm, act = _mm("mlp_in", hn, w_mi_f,
                  outs=((F32, None), (BF16, lambda r, e: jnp.square(jnp.maximum(r, 0.0)))))
    (hm2,) = _mm("mlp_out", act, w_mo_f)

    def f_final(i, x1v, hm2v, g, tg):
        def fwd(hv, gv):
            return x1v + _rms(hv, gv)
        x2v, vjp = jax.vjp(fwd, hm2v, g)
        err = x2v - tg
        dx2 = err * (1.0 / D)
        dh, dg = vjp(dx2)
        loss = jnp.full((1, LANES), 0.5 / D, F32) * jnp.sum(err * err)
        return (dx2, dh), (dg, loss)
    dx1a, dhm2, g_post_mlp, loss_part = _rows_call(
        "loss_head", f_final, nt, [x1, hm2, post_mlp_norm, tgt],
        [_rt(tr, D), _rt(tr, D), _full(post_mlp_norm), _rt(tr, D)],
        [((T, D), F32, _rt(tr, D)), ((T, D), BF16, _rt(tr, D))], [(1, D), (1, LANES)])

    def rs_begin(n, full):
        (from_sib,) = _rs_sibling("rs_sibling_" + n, [full])
        pair = _pair_sum("pair_sum_" + n, full, from_sib, c_idx)
        sems, srcs, lands, token = _split_start("rs_start_" + n, [pair], [((3,) + pair.shape[1:], BF16)],
                                                _rs_copies, 3)
        return (sems, srcs, lands), token

    def rs_end(n, state, after):
        (pair,), (recv,) = _split_wait("rs_wait_" + n, *state, after, _rs_copies, 3)
        return pair, recv

    (gw_mo,) = _mm("dw_mlp_out", act, dhm2, ta=True, outs=((BF16, None),))
    rs_mo, tok = rs_begin('w_mlp_out', gw_mo.reshape(N_DEV, DFF // N_DEV, D))
    (dhm,) = _mm("d_mlp_act", dhm2, w_mo_f, tb=True, extra=hm,
                 outs=((BF16, lambda r, e: r * (2.0 * jnp.maximum(e, 0.0))),), deps=[tok])
    (gw_mi,) = _mm("dw_mlp_in", hn, dhm, ta=True, outs=((BF16, None),), out_blocks=N_DEV)
    rs_mi, tok = rs_begin('w_mlp_in', gw_mi)
    (dhn,) = _mm("d_mlp_in", dhm, w_mi_f, tb=True, deps=[tok])

    def f_post_mix_bwd(i, x1v, mx, gpm, gpl, dhnv, dxa):
        _, vjp1 = jax.vjp(_rms, x1v, gpl)
        dx1, dgpl = vjp1(dhnv)
        dx1 = dx1 + dxa
        _, vjp2 = jax.vjp(_rms, mx, gpm)
        dmx, dgpm = vjp2(dx1)
        return (dx1, dmx), (dgpl, dgpm)
    dx1, dmix, g_pre_mlp, g_post_mix = _rows_call(
        "post_mix_bwd", f_post_mix_bwd, nt, [x1, mix, post_mix_norm, pre_mlp_norm, dhn, dx1a],
        [_rt(tr, D), _rt(tr, D), _full(post_mix_norm), _full(pre_mlp_norm), _rt(tr, D), _rt(tr, D)],
        [((T, D), F32, _rt(tr, D)), ((T, D), BF16, _rt(tr, D))], [(1, D), (1, D)])

    (gw_out,) = _mm("dw_out", mixcat, dmix, ta=True, outs=((BF16, None),))
    rs_out, tok = rs_begin('w_out', gw_out.reshape(N_DEV, -1, D))
    (dmixcat,) = _mm("d_mix", dmix, w_out_f, tb=True, deps=[tok])

    def f_lru_post_bwd(i, hv, gate, g, dy):
        _, vjp = jax.vjp(lru_post, hv, gate, g)
        dh_, dgate, dg = vjp(dy)
        return (dgate, dh_), (dg,)
    dproj, dh_lru, g_lru_norm = _rows_call(
        "lru_post_bwd", f_lru_post_bwd, nt, [h_lru, proj, lru_norm, dmixcat],
        [_rt(tr, DL), _rt(tr, DL, cb_gate), _full(lru_norm), _rt(tr, DL, cb_l)],
        [((T, NP), BF16, _rt(tr, DL, cb_gate)), ((T, DL), F32, _rt(tr, DL))], [(1, DL)])

    du_lru, da_lru = _lru_scan_bwd(a_lru, h_lru, dh_lru, tr)
    dxl, g_wa, g_ba, g_wx, g_bx, g_lam = _lru_gates_bwd(xl, wa_b, b_a, wx_b, b_x, lru_lambda, da_lru, du_lru, tr)

    def f_lru_pre_bwd(i, xv, hp, xn, d, dn, w, b):
        dx, dw8, db = _conv_bwd_tile(i, nt, xv, hp, xn, d, dn, w, b, silu=False)
        return (dx,), (dw8, db)
    dproj, g_convl8, g_convl_b = _rows_call(
        "lru_pre_bwd", f_lru_pre_bwd, nt, [proj, proj, proj, dxl, dxl, conv_l, lru_conv_b],
        [_rt(tr, DL, cb_xl), _halo_prev(tr, DL, cb_xl), _halo_next(tr, DL, nt, cb_xl), _rt(tr, DL),
         _halo_next(tr, DL, nt), _full(conv_l), _full(lru_conv_b)],
        [((T, NP), BF16, _rt(tr, DL, cb_xl))], [(SUBLANES, DL), (1, DL)], into=dproj)

    def f_ssd_post_bwd(i, y, z, g, dy):
        _, vjp = jax.vjp(ssd_post, y, z, g)
        dy_, dz, dg = vjp(dy)
        return (dz, dy_), (dg,)
    dproj, dy_ssd, g_ssd_norm = _rows_call(
        "ssd_post_bwd", f_ssd_post_bwd, nt, [y_ssd, proj, ssd_norm, dmixcat],
        [_rt(tr, DS), _rt(tr, DS, cb_z), _full(ssd_norm), _rt(tr, DS, 0)],
        [((T, NP), BF16, _rt(tr, DS, cb_z)), ((T, DS), F32, _rt(tr, DS))], [(1, DS)], into=dproj)

    dxbc_act, ddt, g_alog, g_dskip = _ssd_bwd(xbc_act, dt, a_log, d_skip, h_prev, dy_ssd, n_heads, Q)

    def f_ssd_pre_bwd(i, xv, hp, xn, d, dn, dtr, ddtv, w, b, dtb):
        dx, dw8, db = _conv_bwd_tile(i, ntw, xv, hp, xn, d, dn, w, b, silu=True)
        ddtr = ddtv * jax.nn.sigmoid(dtr + dtb)
        return (dx, ddtr), (dw8, db, _colsum(ddtr))
    dproj, ddt_raw, g_convs8, g_convs_b, g_dtb = _rows_call(
        "ssd_pre_bwd", f_ssd_pre_bwd, ntw,
        [proj, proj, proj, dxbc_act, dxbc_act, proj, ddt, conv_s, ssd_conv_b, dt_bias],
        [_rt(trw, XBC), _halo_prev(trw, XBC), _halo_next(trw, XBC, ntw), _rt(trw, XBC), _halo_next(trw, XBC, ntw),
         _rt(trw, LANES, cb_dt), _rt(trw, LANES), _full(conv_s), _full(ssd_conv_b), _full(dt_bias)],
        [((T, NP), BF16, _rt(trw, XBC)), ((T, LANES), BF16, _rt(trw, LANES))],
        [(SUBLANES, XBC), (1, XBC), (1, LANES)], into=dproj)

    def f_place(i, v):
        return (v,), ()
    (dproj,) = _rows_call("place_ddt", f_place, 1, [ddt_raw], [_rt(T, LANES)],
                          [((T, NP), BF16, _rt(T, LANES, cb_dt))], [], into=dproj)
    small = {
        'ssd_conv_w': g_convs8[:CONV_WIDTH], 'ssd_conv_b': g_convs_b,
        'ssd_dt_bias': g_dtb[:, :n_heads], 'ssd_a_log': g_alog[:, :n_heads], 'ssd_d': g_dskip[:, :n_heads],
        'ssd_norm': g_ssd_norm, 'lru_conv_w': g_convl8[:CONV_WIDTH], 'lru_conv_b': g_convl_b,
        'lru_w_a': g_wa, 'lru_b_a': g_ba, 'lru_w_x': g_wx, 'lru_b_x': g_bx, 'lru_lambda': g_lam,
        'lru_norm': g_lru_norm, 'post_mix_norm': g_post_mix, 'pre_mlp_norm': g_pre_mlp,
        'post_mlp_norm': g_post_mlp, 'loss': loss_part[:, :1],
    }
    wide = ['lru_w_a', 'lru_w_x']
    narrow = [n for n in small if n not in wide]
    lb = lru_w_a.shape[-1]
    s_srcs = [_flat_rows([small[n] for n in narrow]), g_wa.reshape(-1, lb), g_wx.reshape(-1, lb)]
    s_sems, s_srcs, s_lands, tok = _split_start(
        "small_grads_start", s_srcs, [((N_DEV,) + a.shape, F32) for a in s_srcs], _ag_copies, 4 * len(s_srcs))

    (gwp,) = _mm("dw_proj", h, dproj, ta=True, outs=((BF16, None),), deps=[tok])
    rs_in, tok = rs_begin(
        'w_in', jnp.stack([jnp.concatenate(my_cols(gwp, k * wb, (k + 1) * wb), axis=1) for k in range(N_DEV)]))
    (dh,) = _mm("d_proj", dproj, wp, tb=True, deps=[tok])

    def f_norm_in_bwd(i, xv, g, dhv, dxa):
        _, vjp = jax.vjp(_rms, xv, g)
        dx, dg = vjp(dhv)
        return (dx + dxa,), (dg,)
    grad_x, g_pre_mix = _rows_call(
        "norm_in_bwd", f_norm_in_bwd, nt, [x2, pre_mix_norm, dh, dx1],
        [_rt(tr, D), _full(pre_mix_norm), _rt(tr, D), _rt(tr, D)], [((T, D), F32, _rt(tr, D))], [(1, D)])

    (g_pm8,) = _allgather("allgather_pre_mix_grad", [g_pre_mix])
    s_srcs, s_lands = _split_wait("small_grads_wait", s_sems, s_srcs, s_lands, g_pm8, _ag_copies, 4 * len(s_srcs))
    g_narrow, g_wa8, g_wx8 = _ag_finish("small_grads_finish", s_srcs, s_lands)
    summed = dict(zip(narrow, _unflat(_sum8("sum_small_grads", g_narrow), [small[n].shape for n in narrow])))
    summed['pre_mix_norm'] = _sum8("sum_pre_mix_grad", g_pm8)
    summed['lru_w_a'] = _sum8("sum_lru_w_a_grads", g_wa8)
    summed['lru_w_x'] = _sum8("sum_lru_w_x_grads", g_wx8)
    loss = summed.pop('loss').reshape(())
    for n, full_w in (('ssd_conv_w', XBC), ('lru_conv_w', DL)):
        wdt = full_w // N_DEV
        summed[n] = lax.dynamic_slice_in_dim(summed[n], dev * wdt, wdt, axis=1)
    small_params = [n for n in names if n not in big]
    as2d = lambda a: a.reshape(-1, a.shape[-1])
    res = _adamw_small([as2d(W[n]) for n in small_params],
                       [summed[n].reshape(as2d(W[n]).shape) for n in small_params],
                       [as2d(Mo[n]) for n in small_params], [as2d(Vo[n]) for n in small_params])
    grads = {n: summed[n].reshape(W[n].shape) for n in small_params}
    delta, new_m, new_v = ({n: r.reshape(W[n].shape) for n, r in zip(small_params, rs)} for rs in res)

    for n, state in (('w_mlp_out', rs_mo), ('w_mlp_in', rs_mi), ('w_out', rs_out), ('w_in', rs_in)):
        p, r = rs_end(n, state, g_pm8)
        grads[n], delta[n], new_m[n], new_v[n] = _adamw_big("adamw_" + n, W[n], Mo[n], Vo[n], p, r, chip_idx)

    return (loss, grad_x.reshape(x.shape), *[grads[n] for n in names], *[delta[n] for n in names],
            *[new_m[n] for n in names], *[new_v[n] for n in names])
```

```python
import functools

import jax
import jax.numpy as jnp
from jax import lax
from jax.experimental import pallas as pl
from jax.experimental.pallas import tpu as pltpu

F32, BF16 = jnp.float32, jnp.bfloat16
S = jax.ShapeDtypeStruct
MESH = pl.DeviceIdType.MESH

SSD_GROUPS = 8
LRU_C = 8.0
EPS = 1e-6
CONV_WIDTH = 4
ADAM_LR, ADAM_B1, ADAM_B2, ADAM_EPS, ADAM_WD, ADAM_STEP = 0.001, 0.9, 0.999, 1e-08, 0.01, 10

LANES = 128
SUBLANES = 8
VMEM_LIMIT = 56 * 1024 * 1024
N_DEV = 8
SMALL_W = 512
HI = lax.Precision.HIGHEST


def _pcall(body, **kw):
    return pl.pallas_call(body, **kw)


def _cparams(sem=None, **kw):
    return pltpu.CompilerParams(dimension_semantics=sem, vmem_limit_bytes=VMEM_LIMIT, **kw)


def _pick(n, cands):
    for c in cands:
        if c <= n and n % c == 0:
            return c
    return n


def _rt(tr, w, cb=0, n=None):
    if n is None:
        return pl.BlockSpec((tr, w), lambda i: (i, cb))
    return pl.BlockSpec((tr, w), lambda i: (n - 1 - i, cb))


def _halo_prev(tr, w, cb=0, n=None):
    k = tr // SUBLANES
    if n is None:
        return pl.BlockSpec((SUBLANES, w), lambda i: (jnp.maximum(i * k - 1, 0), cb))
    return pl.BlockSpec((SUBLANES, w), lambda i: (jnp.maximum((n - 1 - i) * k - 1, 0), cb))


def _halo_next(tr, w, nt, cb=0, n=None):
    k = tr // SUBLANES
    last = nt * k - 1
    if n is None:
        return pl.BlockSpec((SUBLANES, w), lambda i: (jnp.minimum((i + 1) * k, last), cb))
    return pl.BlockSpec((SUBLANES, w), lambda i: (jnp.minimum((n - i) * k, last), cb))


def _full(a):
    nd = a.ndim
    return pl.BlockSpec(a.shape, lambda i: (0,) * nd)


def _rows_call(name, fn, n_tiles, arrays, in_specs, out_tiled, out_acc, into=None, deps=()):
    n_in, n_t = len(arrays), len(out_tiled)
    n_skip = len(deps) + (0 if into is None else 1)

    def body(*refs):
        i = pl.program_id(0)
        touts, aouts = fn(i, *[r[...] for r in refs[:n_in]])
        refs = refs[n_skip:]
        for r, v in zip(refs[n_in:n_in + n_t], touts):
            r[...] = v.astype(r.dtype)
        accs = refs[n_in + n_t:]
        if accs:
            @pl.when(i == 0)
            def _():
                for r in accs:
                    r[...] = jnp.zeros_like(r)
            for r, v in zip(accs, aouts):
                r[...] += v

    out_shape = [S(sh, dt) for sh, dt, _ in out_tiled] + [S(sh, F32) for sh in out_acc]
    out_specs = [sp for _, _, sp in out_tiled]
    for sh in out_acc:
        out_specs.append(pl.BlockSpec(sh, lambda i, nd=len(sh): (0,) * nd))
    in_specs = list(in_specs) + [_ANY] * len(deps)
    if into is None:
        return _pcall(body, name=name, grid=(n_tiles,), in_specs=in_specs, out_specs=out_specs,
                      out_shape=out_shape, compiler_params=_cparams(("arbitrary",)))(*arrays, *deps)
    return _pcall(body, name=name, grid=(n_tiles,), in_specs=in_specs + [_ANY], out_specs=out_specs,
                  out_shape=out_shape, input_output_aliases={n_in + len(deps): 0},
                  compiler_params=_cparams(("arbitrary",)))(*arrays, *deps, into)


def _rms(x, g):
    return x * lax.rsqrt(jnp.mean(x * x, axis=-1, keepdims=True) + EPS) * g


def _colsum(v):
    return jnp.sum(v, axis=0, keepdims=True)


_TILES = (1152, 1024, 896, 768, 640, 512, 384, 256, 128)
_K_TILES = (4096, 3456, 3072, 2688, 2048, 1536, 1344, 1152, 1024, 896, 768, 640, 512, 384, 256, 128)


def _mm(name, a, b, *, ta=False, tb=False, outs=((F32, None),), extra=None, out_blocks=None, tm=None, tn=None, tk=None,
        deps=()):
    M, K = (a.shape[1], a.shape[0]) if ta else a.shape
    b3 = b.ndim == 3
    if b3:
        nb_b, brows, bcols = b.shape
        N = brows if tb else nb_b * bcols
    else:
        N = b.shape[0] if tb else b.shape[1]
    n_lim = N if out_blocks is None else N // out_blocks
    if b3 and not tb:
        n_lim = min(n_lim, bcols)
    tm = tm or _pick(M, _TILES[1:])
    tn = tn or _pick(n_lim, _TILES)
    tk = tk or _pick(bcols if (b3 and tb) else K, _K_TILES)
    nk = K // tk
    assert M % tm == 0 and N % tn == 0 and K % tk == 0
    dn = (((0 if ta else 1,), (1 if tb else 0,)), ((), ()))
    n_extra = 0 if extra is None else 1
    n_out = len(outs)

    def body(*refs):
        a_ref, b_ref = refs[0], refs[1]
        e_ref = refs[2] if n_extra else None
        o_refs = refs[2 + n_extra + len(deps):2 + n_extra + len(deps) + n_out]

        def finish(r):
            e = e_ref[...] if n_extra else None
            for o, (_, f) in zip(o_refs, outs):
                o[...] = (r if f is None else f(r, e)).astype(o.dtype)

        part = lax.dot_general(a_ref[...], b_ref[...], dn, preferred_element_type=F32)
        if nk == 1:
            finish(part)
            return
        acc = refs[-1]
        k = pl.program_id(2)

        @pl.when(k == 0)
        def _():
            acc[...] = part

        @pl.when(jnp.logical_and(k > 0, k < nk - 1))
        def _():
            acc[...] += part

        @pl.when(k == nk - 1)
        def _():
            finish(acc[...] + part)

    a_spec = pl.BlockSpec((tk, tm), lambda i, j, k: (k, i)) if ta else pl.BlockSpec((tm, tk), lambda i, j, k: (i, k))
    if not b3:
        b_spec = pl.BlockSpec((tn, tk), lambda i, j, k: (j, k)) if tb else pl.BlockSpec((tk, tn), lambda i, j, k: (k, j))
    elif tb:
        per = bcols // tk
        b_spec = pl.BlockSpec((None, tn, tk), lambda i, j, k: (k // per, j, k % per))
    else:
        per = bcols // tn
        b_spec = pl.BlockSpec((None, tk, tn), lambda i, j, k: (j // per, k, j % per))
    o_spec = pl.BlockSpec((tm, tn), lambda i, j, k: (i, j))
    if out_blocks is None:
        out_specs, out_shape = [o_spec] * n_out, [S((M, N), dt) for dt, _ in outs]
    else:
        per_o = N // out_blocks // tn
        ob_spec = pl.BlockSpec((None, tm, tn), lambda i, j, k: (j // per_o, i, j % per_o))
        out_specs, out_shape = [ob_spec] * n_out, [S((out_blocks, M, N // out_blocks), dt) for dt, _ in outs]
    in_specs = [a_spec, b_spec] + ([o_spec] if n_extra else []) + [_ANY] * len(deps)
    args = [a, b] + ([extra] if n_extra else []) + list(deps)
    return _pcall(body, name=name, grid=(M // tm, N // tn, nk), in_specs=in_specs, out_specs=out_specs,
                  out_shape=out_shape, scratch_shapes=[pltpu.VMEM((tm, tn), F32)] if nk > 1 else [],
                  compiler_params=_cparams(("parallel", "parallel", "arbitrary")))(*args)


def _shift_down(x, halo, s):
    if s == 0:
        return x
    r = pltpu.roll(x, s, 0)
    hr = pltpu.roll(halo, s, 0)
    row = lax.broadcasted_iota(jnp.int32, halo.shape, 0)
    top = jnp.where(row < s, hr, r[:SUBLANES])
    if x.shape[0] == SUBLANES:
        return top
    return jnp.concatenate([top, r[SUBLANES:]], axis=0)


def _shift_up(x, nxt, s):
    if s == 0:
        return x
    n = x.shape[0]
    r = pltpu.roll(x, n - s, 0)
    nr = pltpu.roll(nxt, SUBLANES - s, 0)
    row = lax.broadcasted_iota(jnp.int32, nxt.shape, 0)
    bot = jnp.where(row >= SUBLANES - s, nr, r[n - SUBLANES:])
    if n == SUBLANES:
        return bot
    return jnp.concatenate([r[:n - SUBLANES], bot], axis=0)


def _conv_pre(x, halo, w, b):
    acc = b + w[CONV_WIDTH - 1:CONV_WIDTH, :] * x
    for k in range(CONV_WIDTH - 1):
        acc = acc + w[k:k + 1, :] * _shift_down(x, halo, CONV_WIDTH - 1 - k)
    return acc


def _silu_grad(p):
    s = jax.nn.sigmoid(p)
    return s * (1.0 + p * (1.0 - s))


def _conv_bwd_tile(i, n_tiles, x, hprev, xnext, d, dnext, w, b, silu):
    hprev = jnp.where(i == 0, 0.0, hprev)
    if silu:
        d = d * _silu_grad(_conv_pre(x, hprev, w, b))
        pre_next = _conv_pre(xnext, x[x.shape[0] - SUBLANES:], w, b)
        dnext = dnext * _silu_grad(pre_next)
    dnext = jnp.where(i == n_tiles - 1, 0.0, dnext)
    dx = w[CONV_WIDTH - 1:CONV_WIDTH, :] * d
    row8 = lax.broadcasted_iota(jnp.int32, (SUBLANES, x.shape[1]), 0)
    dw8 = jnp.where(row8 == CONV_WIDTH - 1, _colsum(d * x), 0.0)
    for k in range(CONV_WIDTH - 1):
        s = CONV_WIDTH - 1 - k
        dx = dx + w[k:k + 1, :] * _shift_up(d, dnext, s)
        dw8 = dw8 + jnp.where(row8 == k, _colsum(d * _shift_down(x, hprev, s)), 0.0)
    return dx, dw8, _colsum(d)


def _ssd_dims(xbc_act, n_heads):
    T, XBC = xbc_act.shape
    GN = XBC // 4
    DS = XBC - 2 * GN
    G = SSD_GROUPS
    N = GN // G
    P = DS // n_heads
    K = n_heads // G
    return T, XBC, DS, GN, G, N, P, K


def _ssd_common(dt, alog, Q):
    a = -jnp.exp(alog)
    adt = dt * a
    li = lax.broadcasted_iota(jnp.int32, (Q, Q), 0)
    si = lax.broadcasted_iota(jnp.int32, (Q, Q), 1)
    causal = li >= si
    ltri = causal.astype(F32)
    acs = jnp.dot(ltri, adt, precision=HI, preferred_element_type=F32)
    acs_row = lax.dot_general(adt, ltri, (((0,), (1,)), ((), ())), precision=HI,
                              preferred_element_type=F32)
    return a, adt, causal, ltri, acs, acs_row


def _expander(g, K, P, W):
    r = lax.broadcasted_iota(jnp.int32, (LANES, W), 0)
    c = lax.broadcasted_iota(jnp.int32, (LANES, W), 1)
    return (c // P + g * K == r).astype(F32)


def _dotb(a, b, dn=(((1,), (0,)), ((), ()))):
    return lax.dot_general(a.astype(BF16), b.astype(BF16), dn, preferred_element_type=F32)


def _dot_split(a, sel, terms, dn=(((1,), (0,)), ((), ()))):
    selb = sel.astype(BF16)
    out = None
    for _ in range(terms):
        piece = a.astype(BF16)
        part = lax.dot_general(piece, selb, dn, preferred_element_type=F32)
        out = part if out is None else out + part
        a = a - piece.astype(F32)
    return out


_NT = (((1,), (1,)), ((), ()))
_TN = (((0,), (0,)), ((), ()))


def _ssd_fwd(xbc_act, dt, alog, dskip, n_heads, Q):
    T, XBC, DS, GN, G, N, P, K = _ssd_dims(xbc_act, n_heads)
    W = K * P
    nc = T // Q

    def body(xs_ref, b_ref, c_ref, dt_ref, alog_ref, d_ref, y_ref, hp_ref, h_scr):
        ci = pl.program_id(0)

        @pl.when(ci == 0)
        def _():
            h_scr[...] = jnp.zeros_like(h_scr)

        dtv = dt_ref[...]
        a, adt, causal, ltri, acs, acs_row = _ssd_common(dtv, alog_ref[...], Q)
        lane_head = lax.broadcasted_iota(jnp.int32, (Q, W), 1) // P
        for g in range(G):
            eg = _expander(g, K, P, W)
            dtb = _dot_split(dtv, eg, 3)
            acsb = _dot_split(acs, eg, 3)
            lastb = acsb[Q - 1:Q, :]
            db = _dot_split(jnp.broadcast_to(d_ref[...], (SUBLANES, LANES)), eg, 3)[0:1, :]
            xg = xs_ref[:, g * W:(g + 1) * W]
            bg = b_ref[:, g * N:(g + 1) * N]
            cg = c_ref[:, g * N:(g + 1) * N]
            xt = xg * dtb
            sc = _dotb(cg, bg, _NT)
            yd = jnp.zeros((Q, W), F32)
            for k in range(K):
                h = g * K + k
                seg = acs[:, h:h + 1] - acs_row[h:h + 1, :]
                lh = jnp.where(causal, jnp.exp(jnp.minimum(seg, 0.0)), 0.0)
                xk = jnp.where(lane_head == k, xt, 0.0)
                yd = yd + _dotb(sc * lh, xk)
            hp = h_scr[g]
            yoff = _dotb(cg, hp) * jnp.exp(acsb)
            y_ref[:, g * W:(g + 1) * W] = yd + yoff + xg * db
            e_end = jnp.exp(lastb - acsb)
            st = _dotb(bg, xt * e_end, _TN)
            hp_ref[0, g] = hp
            h_scr[g] = jnp.exp(lastb) * hp + st

    cb = DS // GN
    in_specs = [pl.BlockSpec((Q, DS), lambda c: (c, 0)),
                pl.BlockSpec((Q, GN), lambda c: (c, cb)),
                pl.BlockSpec((Q, GN), lambda c: (c, cb + 1)),
                pl.BlockSpec((Q, LANES), lambda c: (c, 0)),
                pl.BlockSpec((1, LANES), lambda c: (0, 0)),
                pl.BlockSpec((1, LANES), lambda c: (0, 0))]
    out_specs = [pl.BlockSpec((Q, DS), lambda c: (c, 0)),
                 pl.BlockSpec((1, G, N, W), lambda c: (c, 0, 0, 0))]
    return _pcall(body, name="ssd_fwd", grid=(nc,), in_specs=in_specs, out_specs=out_specs,
                  out_shape=[S((T, DS), F32), S((nc, G, N, W), F32)],
                  scratch_shapes=[pltpu.VMEM((G, N, W), F32)],
                  compiler_params=_cparams(("arbitrary",)))(xbc_act, xbc_act, xbc_act, dt, alog, dskip)


def _ssd_bwd(xbc_act, dt, alog, dskip, hprev, dy, n_heads, Q):
    T, XBC, DS, GN, G, N, P, K = _ssd_dims(xbc_act, n_heads)
    W = K * P
    nc = T // Q

    def body(xs_ref, b_ref, c_ref, dt_ref, alog_ref, d_ref, hp_ref, dy_ref,
             dxbc_ref, ddt_ref, dalog_ref, dd_ref, dh_scr):
        ci = pl.program_id(0)

        @pl.when(ci == 0)
        def _():
            dh_scr[...] = jnp.zeros_like(dh_scr)
            dalog_ref[...] = jnp.zeros_like(dalog_ref)
            dd_ref[...] = jnp.zeros_like(dd_ref)

        dtv = dt_ref[...]
        a, adt, causal, ltri, acs, acs_row = _ssd_common(dtv, alog_ref[...], Q)
        lane_head = lax.broadcasted_iota(jnp.int32, (Q, W), 1) // P
        lane128 = lax.broadcasted_iota(jnp.int32, (Q, LANES), 1)
        sub128 = lax.broadcasted_iota(jnp.int32, (LANES, Q), 0)
        rowq = lax.broadcasted_iota(jnp.int32, (Q, W), 0)
        dacs = jnp.zeros((Q, LANES), F32)
        dacs_row = jnp.zeros((LANES, Q), F32)
        ddt = jnp.zeros((Q, LANES), F32)
        dd_acc = jnp.zeros((1, LANES), F32)
        for g in range(G):
            eg = _expander(g, K, P, W)
            dtb = _dot_split(dtv, eg, 3)
            acsb = _dot_split(acs, eg, 3)
            lastb = acsb[Q - 1:Q, :]
            db = _dot_split(jnp.broadcast_to(d_ref[...], (SUBLANES, LANES)), eg, 3)[0:1, :]
            xg = xs_ref[:, g * W:(g + 1) * W]
            bg = b_ref[:, g * N:(g + 1) * N]
            cg = c_ref[:, g * N:(g + 1) * N]
            dyg = dy_ref[:, g * W:(g + 1) * W]
            hp = hp_ref[0, g]
            dhn = dh_scr[g]
            xt = xg * dtb
            sc = _dotb(cg, bg, _NT)
            eacs = jnp.exp(acsb)
            e_end = jnp.exp(lastb - acsb)
            elast = jnp.exp(lastb)

            wv = dyg * eacs
            dcg = _dotb(wv, hp, _NT)
            dhp = _dotb(cg, wv, _TN) + elast * dhn
            dacsb = dyg * (_dotb(cg, hp) * eacs)

            xe = xt * e_end
            dbg = _dotb(xe, dhn, _NT)
            v = _dotb(bg, dhn)
            dxt = v * e_end
            de = v * xe
            dacsb = dacsb - de
            dlastb = _colsum(de) + elast * jnp.sum(dhn * hp, axis=0, keepdims=True)

            dsc = jnp.zeros((Q, Q), F32)
            for k in range(K):
                h = g * K + k
                seg = acs[:, h:h + 1] - acs_row[h:h + 1, :]
                lh = jnp.where(causal, jnp.exp(jnp.minimum(seg, 0.0)), 0.0)
                mh = sc * lh
                dyk = jnp.where(lane_head == k, dyg, 0.0)
                dxt = dxt + jnp.where(lane_head == k, _dotb(mh, dyg, _TN), 0.0)
                dm = _dotb(dyk, xt, _NT)
                dsc = dsc + dm * lh
                gm = dm * mh
                dacs = dacs + jnp.where(lane128 == h, jnp.sum(gm, axis=1, keepdims=True), 0.0)
                dacs_row = dacs_row - jnp.where(sub128 == h, jnp.sum(gm, axis=0, keepdims=True), 0.0)
            dcg = dcg + _dotb(dsc, bg)
            dbg = dbg + _dotb(dsc, cg, _TN)

            dacsb = dacsb + jnp.where(rowq == Q - 1, dlastb, 0.0)
            dacs = dacs + _dot_split(dacsb, eg, 2, _NT)
            ddt = ddt + _dot_split(dxt * xg, eg, 2, _NT)
            dd_acc = dd_acc + _dot_split(jnp.broadcast_to(_colsum(dyg * xg), (SUBLANES, W)), eg, 2, _NT)[0:1, :]
            dxbc_ref[:, g * W:(g + 1) * W] = dxt * dtb + dyg * db
            dxbc_ref[:, DS + g * N:DS + (g + 1) * N] = dbg
            dxbc_ref[:, DS + GN + g * N:DS + GN + (g + 1) * N] = dcg
            dh_scr[g] = dhp

        eye = (lax.broadcasted_iota(jnp.int32, (LANES, LANES), 0) ==
               lax.broadcasted_iota(jnp.int32, (LANES, LANES), 1)).astype(F32)
        dacs = dacs + lax.dot_general(dacs_row, eye, _TN, precision=HI, preferred_element_type=F32)
        dadt = lax.dot_general(ltri, dacs, _TN, precision=HI, preferred_element_type=F32)
        ddt_ref[...] = ddt + dadt * a
        dalog_ref[...] += _colsum(dadt * dtv) * a
        dd_ref[...] += dd_acc

    cb = DS // GN
    rv = lambda c: nc - 1 - c
    in_specs = [pl.BlockSpec((Q, DS), lambda c: (rv(c), 0)),
                pl.BlockSpec((Q, GN), lambda c: (rv(c), cb)),
                pl.BlockSpec((Q, GN), lambda c: (rv(c), cb + 1)),
                pl.BlockSpec((Q, LANES), lambda c: (rv(c), 0)),
                pl.BlockSpec((1, LANES), lambda c: (0, 0)),
                pl.BlockSpec((1, LANES), lambda c: (0, 0)),
                pl.BlockSpec((1, G, N, W), lambda c: (rv(c), 0, 0, 0)),
                pl.BlockSpec((Q, DS), lambda c: (rv(c), 0))]
    out_specs = [pl.BlockSpec((Q, XBC), lambda c: (rv(c), 0)),
                 pl.BlockSpec((Q, LANES), lambda c: (rv(c), 0)),
                 pl.BlockSpec((1, LANES), lambda c: (0, 0)),
                 pl.BlockSpec((1, LANES), lambda c: (0, 0))]
    return _pcall(body, name="ssd_bwd", grid=(nc,), in_specs=in_specs, out_specs=out_specs,
                  out_shape=[S((T, XBC), F32), S((T, LANES), F32), S((1, LANES), F32), S((1, LANES), F32)],
                  scratch_shapes=[pltpu.VMEM((G, N, W), F32)],
                  compiler_params=_cparams(("arbitrary",)))(
                      xbc_act, xbc_act, xbc_act, dt, alog, dskip, hprev, dy)


def _blockdiag(x, w_ref, dn=(((1,), (0,)), ((), ()))):
    H, B, _ = w_ref.shape
    return jnp.concatenate([_dotb(x[:, h * B:(h + 1) * B], w_ref[h], dn) for h in range(H)], axis=1)


def _lru_elem(xl, r_pre, i_pre, lam):
    r = jax.nn.sigmoid(r_pre)
    i = jax.nn.sigmoid(i_pre)
    log_a = -LRU_C * r * jax.nn.softplus(-lam)
    a = jnp.exp(log_a)
    u = jnp.sqrt(1.0 - jnp.exp(2.0 * log_a)) * (i * xl)
    return a, u


def _lru_gates_fwd(xl, w_a, b_a, w_x, b_x, lam, tr):
    T, DL = xl.shape

    def body(xl_ref, wa_ref, ba_ref, wx_ref, bx_ref, lam_ref, a_ref, u_ref):
        x = xl_ref[...]
        r_pre = _blockdiag(x, wa_ref) + ba_ref[...]
        i_pre = _blockdiag(x, wx_ref) + bx_ref[...]
        a, u = _lru_elem(x, r_pre, i_pre, lam_ref[...])
        a_ref[...] = a
        u_ref[...] = u

    w3 = pl.BlockSpec(w_a.shape, lambda i: (0, 0, 0))
    vec = pl.BlockSpec((1, DL), lambda i: (0, 0))
    return _pcall(body, name="lru_gates_fwd", grid=(T // tr,),
                  in_specs=[_rt(tr, DL), w3, vec, w3, vec, vec],
                  out_specs=[_rt(tr, DL), _rt(tr, DL)], out_shape=[S((T, DL), F32), S((T, DL), F32)],
                  compiler_params=_cparams(("parallel",)))(xl, w_a, b_a, w_x, b_x, lam)


def _lru_gates_bwd(xl, w_a, b_a, w_x, b_x, lam, da, du, tr):
    T, DL = xl.shape
    H, B, _ = w_a.shape

    def body(xl_ref, wa_ref, ba_ref, wx_ref, bx_ref, lam_ref, da_ref, du_ref,
             dxl_ref, dwa_ref, dba_ref, dwx_ref, dbx_ref, dlam_ref):
        @pl.when(pl.program_id(0) == 0)
        def _():
            for r in (dwa_ref, dba_ref, dwx_ref, dbx_ref, dlam_ref):
                r[...] = jnp.zeros_like(r)

        x = xl_ref[...]
        r_pre = _blockdiag(x, wa_ref) + ba_ref[...]
        i_pre = _blockdiag(x, wx_ref) + bx_ref[...]
        _, vjp = jax.vjp(_lru_elem, x, r_pre, i_pre, lam_ref[...])
        dx, dr, di, dlam = vjp((da_ref[...], du_ref[...]))
        dxl_ref[...] = dx + _blockdiag(dr, wa_ref, _NT) + _blockdiag(di, wx_ref, _NT)
        for h in range(H):
            xh = x[:, h * B:(h + 1) * B]
            dwa_ref[h] += _dotb(xh, dr[:, h * B:(h + 1) * B], _TN)
            dwx_ref[h] += _dotb(xh, di[:, h * B:(h + 1) * B], _TN)
        dba_ref[...] += _colsum(dr)
        dbx_ref[...] += _colsum(di)
        dlam_ref[...] += dlam

    w3 = pl.BlockSpec(w_a.shape, lambda i: (0, 0, 0))
    vec = pl.BlockSpec((1, DL), lambda i: (0, 0))
    return _pcall(body, name="lru_gates_bwd", grid=(T // tr,),
                  in_specs=[_rt(tr, DL), w3, vec, w3, vec, vec, _rt(tr, DL), _rt(tr, DL)],
                  out_specs=[_rt(tr, DL), w3, vec, w3, vec, vec],
                  out_shape=[S((T, DL), F32), S(w_a.shape, F32), S((1, DL), F32), S(w_a.shape, F32),
                             S((1, DL), F32), S((1, DL), F32)],
                  compiler_params=_cparams(("arbitrary",)))(xl, w_a, b_a, w_x, b_x, lam, da, du)


def _scan_tile(a, u, up):
    n = a.shape[0]
    row = lax.broadcasted_iota(jnp.int32, a.shape, 0)
    d = 1
    while d < n:
        if up:
            keep = row < n - d
            a_s = jnp.where(keep, pltpu.roll(a, n - d, 0), 1.0)
            u_s = jnp.where(keep, pltpu.roll(u, n - d, 0), 0.0)
        else:
            keep = row >= d
            a_s = jnp.where(keep, pltpu.roll(a, d, 0), 1.0)
            u_s = jnp.where(keep, pltpu.roll(u, d, 0), 0.0)
        u = a * u_s + u
        a = a * a_s
        d *= 2
    return a, u


def _lru_scan_fwd(a, u, tr):
    T, DL = a.shape

    def body(a_ref, u_ref, h_ref, carry):
        @pl.when(pl.program_id(0) == 0)
        def _():
            carry[...] = jnp.zeros_like(carry)

        A, U = _scan_tile(a_ref[...], u_ref[...], up=False)
        h = U + A * carry[0:1, :]
        h_ref[...] = h
        carry[...] = jnp.broadcast_to(h[tr - 1:tr, :], carry.shape)

    return _pcall(body, name="lru_scan_fwd", grid=(T // tr,), in_specs=[_rt(tr, DL), _rt(tr, DL)],
                  out_specs=_rt(tr, DL), out_shape=S((T, DL), F32),
                  scratch_shapes=[pltpu.VMEM((SUBLANES, DL), F32)],
                  compiler_params=_cparams(("arbitrary",)))(a, u)


def _lru_scan_bwd(a, h, dh, tr):
    T, DL = a.shape
    n = T // tr

    def body(a_ref, an_ref, h_ref, hp_ref, dh_ref, du_ref, da_ref, carry):
        i = pl.program_id(0)
        ti = n - 1 - i

        @pl.when(i == 0)
        def _():
            carry[...] = jnp.zeros_like(carry)

        a_next = _shift_up(a_ref[...], jnp.where(ti == n - 1, 0.0, an_ref[...]), 1)
        A, U = _scan_tile(a_next, dh_ref[...], up=True)
        g = U + A * carry[0:1, :]
        du_ref[...] = g
        h_prev = _shift_down(h_ref[...], jnp.where(ti == 0, 0.0, hp_ref[...]), 1)
        da_ref[...] = g * h_prev
        carry[...] = jnp.broadcast_to(g[0:1, :], carry.shape)

    return _pcall(body, name="lru_scan_bwd", grid=(n,),
                  in_specs=[_rt(tr, DL, 0, n), _halo_next(tr, DL, n, 0, n), _rt(tr, DL, 0, n),
                            _halo_prev(tr, DL, 0, n), _rt(tr, DL, 0, n)],
                  out_specs=[_rt(tr, DL, 0, n), _rt(tr, DL, 0, n)],
                  out_shape=[S((T, DL), F32), S((T, DL), F32)],
                  scratch_shapes=[pltpu.VMEM((SUBLANES, DL), F32)],
                  compiler_params=_cparams(("arbitrary",)))(a, a, h, h, dh)


def _adamw(w, g, m, v):
    m = ADAM_B1 * m + (1.0 - ADAM_B1) * g
    v = ADAM_B2 * v + (1.0 - ADAM_B2) * (g * g)
    m_hat = m / (1.0 - ADAM_B1 ** ADAM_STEP)
    v_hat = v / (1.0 - ADAM_B2 ** ADAM_STEP)
    delta = -ADAM_LR * (m_hat / (jnp.sqrt(v_hat) + ADAM_EPS) + ADAM_WD * w)
    return delta, m, v


def _adamw_big(name, w, m, v, part, recv, chip_idx):
    _, R, C = w.shape
    tr = _pick(R, (256, 128, 64, 32, 16))

    def body(ci_ref, w_ref, m_ref, v_ref, p_ref, r0, r1, r2, g_ref, d_ref, nm_ref, nv_ref):
        g = ((p_ref[...].astype(F32) + r0[...].astype(F32)) + r1[...].astype(F32)) + r2[...].astype(F32)
        d, nm, nv = _adamw(w_ref[...], g, m_ref[...], v_ref[...])
        g_ref[...] = g
        d_ref[...] = d
        nm_ref[...] = nm
        nv_ref[...] = nv

    r_spec = lambda s: pl.BlockSpec((None, tr, C), lambda i, ci: (s, i, 0))
    t2 = r_spec(0)
    gs = pltpu.PrefetchScalarGridSpec(
        num_scalar_prefetch=1, grid=(R // tr,),
        in_specs=[t2, t2, t2, pl.BlockSpec((None, tr, C), lambda i, ci: (ci[0], i, 0)),
                  r_spec(0), r_spec(1), r_spec(2)],
        out_specs=[t2, t2, t2, t2])
    return _pcall(body, name=name, grid_spec=gs, out_shape=[S((1, R, C), F32)] * 4,
                  compiler_params=_cparams(("parallel",)))(chip_idx, w, m, v, part, recv, recv, recv)


def _adamw_small(ws, gs, ms, vs):
    n = len(ws)

    def body(*refs):
        for k in range(n):
            d, nm, nv = _adamw(refs[k][...], refs[n + k][...], refs[2 * n + k][...], refs[3 * n + k][...])
            refs[4 * n + k][...] = d
            refs[5 * n + k][...] = nm
            refs[6 * n + k][...] = nv

    res = _pcall(body, name="adamw_small", out_shape=[S(w.shape, F32) for w in ws] * 3,
                 compiler_params=_cparams())(*ws, *gs, *ms, *vs)
    return res[:n], res[n:2 * n], res[2 * n:]


def _sum8(name, parts):
    _, R, C = parts.shape

    def body(p_ref, o_ref):
        acc = p_ref[0]
        for k in range(1, N_DEV):
            acc = acc + p_ref[k]
        o_ref[...] = acc

    return _pcall(body, name=name, out_shape=S((R, C), F32), compiler_params=_cparams())(parts)


def _pair_sum(name, full, recv, c_idx):
    _, R, C = full.shape
    tr = _pick(R, (256, 128, 64, 32, 16))

    def body(c_ref, f_ref, r_ref, o_ref):
        o_ref[...] = (f_ref[...].astype(F32) + r_ref[...].astype(F32)).astype(o_ref.dtype)

    gs = pltpu.PrefetchScalarGridSpec(
        num_scalar_prefetch=1, grid=(4, R // tr),
        in_specs=[pl.BlockSpec((None, tr, C), lambda j, i, c: (2 * j + c[0], i, 0)),
                  pl.BlockSpec((None, tr, C), lambda j, i, c: (j, i, 0))],
        out_specs=pl.BlockSpec((None, tr, C), lambda j, i, c: (j, i, 0)))
    return _pcall(body, name=name, grid_spec=gs, out_shape=S((4, R, C), BF16),
                  compiler_params=_cparams(("parallel", "parallel")))(c_idx, full, recv)


def _cast_bf16(name, w, dev_idx):
    _, R, C = w.shape
    tr = _pick(R, (256, 128, 64, 32, 16))

    def body(d_ref, w_ref, o_ref):
        o_ref[...] = w_ref[...].astype(BF16)

    gs = pltpu.PrefetchScalarGridSpec(
        num_scalar_prefetch=1, grid=(R // tr,),
        in_specs=[pl.BlockSpec((None, tr, C), lambda i, d: (0, i, 0))],
        out_specs=pl.BlockSpec((None, tr, C), lambda i, d: (d[0], i, 0)))
    return _pcall(body, name=name, grid_spec=gs, out_shape=S((N_DEV, R, C), BF16),
                  compiler_params=_cparams(("parallel",)))(dev_idx, w)


def _own_block(v, dev):
    return lax.dynamic_update_slice(lax.empty((N_DEV,) + v.shape, v.dtype), v[None], (dev,) + (0,) * v.ndim)


_ANY = pl.BlockSpec(memory_space=pl.ANY)


def _position():
    return lax.axis_index("x"), lax.axis_index("y"), lax.axis_index("c")


def _allgather(name, bufs):
    n = len(bufs)

    def body(*refs):
        outs = refs[n:2 * n]
        send, recv = refs[2 * n:]
        x, y, c = _position()
        me, sib = (x, y, c), (x, y, 1 - c)
        chips = [(1 - x, y), (x, 1 - y), (1 - x, 1 - y)]

        def copy(a, k, block, to):
            bx, by, bc = block
            blk = outs[a].at[4 * bx + 2 * by + bc]
            return pltpu.make_async_remote_copy(
                src_ref=blk, dst_ref=blk, send_sem=send.at[a, k], recv_sem=recv.at[a, k],
                device_id=to, device_id_type=MESH)

        first = []
        for a in range(n):
            first.append(copy(a, 0, me, sib))
            first += [copy(a, 1 + j, me, (*chip, c)) for j, chip in enumerate(chips)]
        for cp in first:
            cp.start()
        passed = []
        for j, chip in enumerate(chips):
            for a in range(n):
                copy(a, 1 + j, (*chip, c), me).wait_recv()
                cp = copy(a, 4 + j, (*chip, c), sib)
                cp.start()
                passed.append(cp)
        for a in range(n):
            copy(a, 0, sib, me).wait_recv()
        for j, chip in enumerate(chips):
            for a in range(n):
                copy(a, 4 + j, (*chip, 1 - c), me).wait_recv()
        for cp in first + passed:
            cp.wait_send()

    return _pcall(body, name=name, in_specs=[_ANY] * n, out_specs=[_ANY] * n,
                  out_shape=[S(b.shape, b.dtype) for b in bufs], input_output_aliases={a: a for a in range(n)},
                  scratch_shapes=[pltpu.SemaphoreType.DMA((n, 7)), pltpu.SemaphoreType.DMA((n, 7))])(*bufs)


def _rs_sibling(name, fulls):
    n = len(fulls)

    def body(*refs):
        ins, outs = refs[:n], refs[n:2 * n]
        send, recv = refs[2 * n:]
        x, y, c = _position()
        copies = []
        for a in range(n):
            for j in range(4):
                copies.append(pltpu.make_async_remote_copy(
                    src_ref=ins[a].at[2 * j + (1 - c)], dst_ref=outs[a].at[j], send_sem=send.at[a, j],
                    recv_sem=recv.at[a, j], device_id=(x, y, 1 - c), device_id_type=MESH))
        for cp in copies:
            cp.start()
        for cp in copies:
            cp.wait()

    return _pcall(body, name=name, in_specs=[_ANY] * n, out_specs=[_ANY] * n,
                  out_shape=[S((4,) + f.shape[1:], f.dtype) for f in fulls],
                  scratch_shapes=[pltpu.SemaphoreType.DMA((n, 4)), pltpu.SemaphoreType.DMA((n, 4))])(*fulls)


_HBM = pl.BlockSpec(memory_space=pltpu.HBM)
_SEM = pl.BlockSpec(memory_space=pltpu.SEMAPHORE)
_EFFECT = pltpu.SideEffectType.DATAFLOW_SIDE_EFFECTING


def _remote_copies(copies_fn, srcs, lands, send, recv):
    x, y, c = _position()
    return [pltpu.make_async_remote_copy(src_ref=s, dst_ref=d, send_sem=send[i], recv_sem=recv[i], device_id=to,
                                         device_id_type=MESH)
            for i, (s, d, to) in enumerate(copies_fn(x, y, c, srcs, lands))]


def _split_start(name, srcs, lands, copies_fn, nc):
    n, nl = len(srcs), len(lands)

    def body(*refs):
        src_refs, land_refs = refs[:n], refs[n:n + nl]
        outs = refs[n + nl:]
        for cp in _remote_copies(copies_fn, src_refs, land_refs, outs[:nc], outs[nc:2 * nc]):
            cp.start()
        outs[-1][...] = jnp.zeros_like(outs[-1])

    hbm = lambda a: pltpu.with_memory_space_constraint(a, pltpu.HBM)
    res = _pcall(
        body, name=name, in_specs=[_HBM] * (n + nl),
        out_specs=[_SEM] * (2 * nc) + [_HBM] * (n + nl) + [pl.BlockSpec(memory_space=pltpu.VMEM)],
        out_shape=[pltpu.SemaphoreType.DMA(())] * (2 * nc) + [pltpu.HBM(s.shape, s.dtype) for s in srcs]
        + [pltpu.HBM(l.shape, l.dtype) for l in lands] + [S((SUBLANES, LANES), F32)],
        input_output_aliases={i: 2 * nc + i for i in range(n + nl)},
        compiler_params=pltpu.CompilerParams(has_side_effects=_EFFECT),
    )(*[hbm(s) for s in srcs], *[hbm(l) for l in lands])
    return res[:2 * nc], res[2 * nc:2 * nc + n], res[2 * nc + n:2 * nc + n + nl], res[-1]


def _split_wait(name, sems, srcs, lands, after, copies_fn, nc):
    n, nl = len(srcs), len(lands)

    def body(*refs):
        src_refs, land_refs = refs[:n], refs[n:n + nl]
        sem_refs = refs[n + nl:n + nl + 2 * nc]
        for cp in _remote_copies(copies_fn, src_refs, land_refs, sem_refs[:nc], sem_refs[nc:]):
            cp.wait_send()
            cp.wait_recv()

    res = _pcall(
        body, name=name, in_specs=[_HBM] * (n + nl) + [_SEM] * (2 * nc) + [_ANY],
        out_specs=[_HBM] * (n + nl), out_shape=[pltpu.HBM(a.shape, a.dtype) for a in list(srcs) + list(lands)],
        input_output_aliases={i: i for i in range(n + nl)},
        compiler_params=pltpu.CompilerParams(has_side_effects=_EFFECT),
    )(*srcs, *lands, *sems, after)
    return res[:n], res[n:]


def _other_chips(x, y):
    return [(1 - x, y), (x, 1 - y), (1 - x, 1 - y)]


def _ag_copies(x, y, c, srcs, lands):
    out = []
    for land in lands:
        blk = land.at[4 * x + 2 * y + c]
        out.append((blk, blk, (x, y, 1 - c)))
        out += [(blk, blk, (px, py, c)) for px, py in _other_chips(x, y)]
    return out


def _rs_copies(x, y, c, srcs, lands):
    return [(s.at[2 * px + py], land.at[j], (px, py, c))
            for s, land in zip(srcs, lands) for j, (px, py) in enumerate(_other_chips(x, y))]


def _ag_finish(name, lands):
    n = len(lands)

    def body(*refs):
        outs = refs[n:2 * n]
        send, recv = refs[2 * n:]
        x, y, c = _position()

        def swap(a, j, px, py, pc):
            blk = outs[a].at[4 * px + 2 * py + pc]
            return pltpu.make_async_remote_copy(src_ref=blk, dst_ref=blk, send_sem=send.at[a, j], recv_sem=recv.at[a, j],
                                                device_id=(x, y, 1 - c), device_id_type=MESH)

        chips = _other_chips(x, y)
        sends = [swap(a, j, px, py, c) for a in range(n) for j, (px, py) in enumerate(chips)]
        for cp in sends:
            cp.start()
        for a in range(n):
            for j, (px, py) in enumerate(chips):
                swap(a, j, px, py, 1 - c).wait_recv()
        for cp in sends:
            cp.wait_send()

    return _pcall(body, name=name, in_specs=[_ANY] * n, out_specs=[_ANY] * n,
                  out_shape=[S(l.shape, l.dtype) for l in lands], input_output_aliases={a: a for a in range(n)},
                  scratch_shapes=[pltpu.SemaphoreType.DMA((n, 3)), pltpu.SemaphoreType.DMA((n, 3))])(*lands)


def _pad_lanes(v):
    return jnp.pad(v, ((0, 0), (0, LANES - v.shape[1])))


def _flat_rows(pieces):
    flat = jnp.concatenate([p.reshape(-1) for p in pieces])
    rows = -(-flat.shape[0] // (SMALL_W * SUBLANES)) * SUBLANES
    return jnp.pad(flat, (0, rows * SMALL_W - flat.shape[0])).reshape(rows, SMALL_W)


def _unflat(buf, shapes):
    flat = buf.reshape(-1)
    out, off = [], 0
    for sh in shapes:
        n = 1
        for d in sh:
            n *= d
        out.append(flat[off:off + n].reshape(sh))
        off += n
    return out


def kernel(x, pre_mix_norm, w_in, ssd_conv_w, ssd_conv_b, ssd_dt_bias, ssd_a_log, ssd_d, ssd_norm, lru_conv_w, lru_conv_b, lru_w_a, lru_b_a, lru_w_x, lru_b_x, lru_lambda, lru_norm, w_out, post_mix_norm, pre_mlp_norm, w_mlp_in, w_mlp_out, post_mlp_norm, loss_target, m_pre_mix_norm, m_w_in, m_ssd_conv_w, m_ssd_conv_b, m_ssd_dt_bias, m_ssd_a_log, m_ssd_d, m_ssd_norm, m_lru_conv_w, m_lru_conv_b, m_lru_w_a, m_lru_b_a, m_lru_w_x, m_lru_b_x, m_lru_lambda, m_lru_norm, m_w_out, m_post_mix_norm, m_pre_mlp_norm, m_w_mlp_in, m_w_mlp_out, m_post_mlp_norm, v_pre_mix_norm, v_w_in, v_ssd_conv_w, v_ssd_conv_b, v_ssd_dt_bias, v_ssd_a_log, v_ssd_d, v_ssd_norm, v_lru_conv_w, v_lru_conv_b, v_lru_w_a, v_lru_b_a, v_lru_w_x, v_lru_b_x, v_lru_lambda, v_lru_norm, v_w_out, v_post_mix_norm, v_pre_mlp_norm, v_w_mlp_in, v_w_mlp_out, v_post_mlp_norm):
    names = ['pre_mix_norm', 'w_in', 'ssd_conv_w', 'ssd_conv_b', 'ssd_dt_bias', 'ssd_a_log', 'ssd_d', 'ssd_norm',
             'lru_conv_w', 'lru_conv_b', 'lru_w_a', 'lru_b_a', 'lru_w_x', 'lru_b_x', 'lru_lambda', 'lru_norm',
             'w_out', 'post_mix_norm', 'pre_mlp_norm', 'w_mlp_in', 'w_mlp_out', 'post_mlp_norm']
    loc = locals()
    W = {n: loc[n] for n in names}
    Mo = {n: loc["m_" + n] for n in names}
    Vo = {n: loc["v_" + n] for n in names}
    big = ['w_in', 'w_out', 'w_mlp_in', 'w_mlp_out']

    px, py, pc = _position()
    dev = 4 * px + 2 * py + pc
    dev_idx = jnp.reshape(dev, (1,)).astype(jnp.int32)
    c_idx = jnp.reshape(pc, (1,)).astype(jnp.int32)
    chip_idx = jnp.reshape(2 * px + py, (1,)).astype(jnp.int32)

    _, T, D = x.shape
    x2 = x.reshape(T, D)
    tgt = loss_target.reshape(T, D)
    n_heads = ssd_dt_bias.shape[1]
    XBC = ssd_conv_b.shape[1]
    GN = XBC // 4
    DS = XBC - 2 * GN
    DL = lru_norm.shape[1]
    DFF = w_mlp_in.shape[2] * N_DEV
    DIN = w_in.shape[2] * N_DEV
    NP = XBC + DS + 2 * DL + LANES
    assert DS % GN == 0 and XBC % DS == 0 and DS == DL and n_heads <= LANES
    cb_z, cb_gate, cb_xl, cb_dt = XBC // DS, XBC // DS + 1, XBC // DS + 2, (XBC + DS + 2 * DL) // LANES
    tr = min(256, T // 2)
    trw = min(128, T // 2)
    nt, ntw = T // tr, T // trw
    Q = min(128, T // 2)

    sh = {n: _cast_bf16("cast_" + n, W[n], dev_idx) for n in big}
    g_in, g_cs, g_cl = _allgather(
        "allgather_w_in", [sh['w_in'], _own_block(ssd_conv_w[0], dev), _own_block(lru_conv_w[0], dev)])
    later = big[1:]
    ag_sems, ag_srcs, ag_lands, ag_token = _split_start(
        "allgather_later_start", [], [sh[n] for n in later], _ag_copies, 4 * len(later))
    conv_s = jnp.transpose(g_cs, (1, 0, 2)).reshape(CONV_WIDTH, XBC)
    conv_l = jnp.transpose(g_cl, (1, 0, 2)).reshape(CONV_WIDTH, DL)
    wb = DIN // N_DEV
    o_z, o_xbc, o_dt, o_gate, o_xl = 0, DS, DS + XBC, DS + XBC + n_heads, DS + XBC + n_heads + DL
    segs = [(o_xbc, o_xbc + XBC, 0), (o_z, o_z + DS, XBC), (o_gate, o_gate + DL, XBC + DS),
            (o_xl, o_xl + DL, XBC + DS + DL), (o_dt, o_dt + n_heads, NP - LANES)]

    def ref_cols(lo, hi):
        out = []
        while lo < hi:
            k = lo // wb
            e = min(hi, (k + 1) * wb)
            out.append(g_in[k, :, lo - k * wb:e - k * wb])
            lo = e
        return out

    def my_cols(g, lo, hi):
        out = []
        for a, b, m in sorted(segs):
            s, e = max(lo, a), min(hi, b)
            if s < e:
                out.append(g[:, m + s - a:m + e - a])
        return out

    wp = jnp.concatenate([p for a, b, _ in segs for p in ref_cols(a, b)]
                         + [jnp.zeros((D, LANES - n_heads), BF16)], axis=1)
    dt_bias = _pad_lanes(ssd_dt_bias)
    a_log = _pad_lanes(ssd_a_log)
    d_skip = _pad_lanes(ssd_d)
    wa_b, wx_b = lru_w_a[0].astype(BF16), lru_w_x[0].astype(BF16)
    b_a, b_x = lru_b_a.reshape(1, DL), lru_b_x.reshape(1, DL)

    def f_norm_in(i, xv, g):
        return (_rms(xv, g),), ()
    (h,) = _rows_call("norm_in", f_norm_in, nt, [x2, pre_mix_norm], [_rt(tr, D), _full(pre_mix_norm)],
                      [((T, D), BF16, _rt(tr, D))], [], deps=[ag_token])

    (proj,) = _mm("proj", h, wp)

    def f_ssd_pre(i, xbc, halo, dtr, w, b, dtb):
        pre = _conv_pre(xbc, jnp.where(i == 0, 0.0, halo), w, b)
        return (pre * jax.nn.sigmoid(pre), jax.nn.softplus(dtr + dtb)), ()
    xbc_act, dt = _rows_call(
        "ssd_pre", f_ssd_pre, ntw, [proj, proj, proj, conv_s, ssd_conv_b, dt_bias],
        [_rt(trw, XBC), _halo_prev(trw, XBC), _rt(trw, LANES, cb_dt), _full(conv_s), _full(ssd_conv_b), _full(dt_bias)],
        [((T, XBC), F32, _rt(trw, XBC)), ((T, LANES), F32, _rt(trw, LANES))], [])

    y_ssd, h_prev = _ssd_fwd(xbc_act, dt, a_log, d_skip, n_heads, Q)

    gw = DS // SSD_GROUPS

    def ssd_post(y, z, g):
        yz = y * jax.nn.silu(z)
        parts = []
        for k in range(SSD_GROUPS):
            yk = yz[:, k * gw:(k + 1) * gw]
            parts.append(yk * lax.rsqrt(jnp.mean(yk * yk, axis=-1, keepdims=True) + EPS))
        return jnp.concatenate(parts, axis=-1) * g

    def f_ssd_post(i, y, z, g):
        return (ssd_post(y, z, g),), ()
    (mixcat,) = _rows_call("ssd_post", f_ssd_post, nt, [y_ssd, proj, ssd_norm],
                           [_rt(tr, DS), _rt(tr, DS, cb_z), _full(ssd_norm)], [((T, DS + DL), BF16, _rt(tr, DS))], [])

    def f_lru_pre(i, xv, halo, w, b):
        return (_conv_pre(xv, jnp.where(i == 0, 0.0, halo), w, b),), ()
    (xl,) = _rows_call("lru_pre", f_lru_pre, nt, [proj, proj, conv_l, lru_conv_b],
                       [_rt(tr, DL, cb_xl), _halo_prev(tr, DL, cb_xl), _full(conv_l), _full(lru_conv_b)],
                       [((T, DL), F32, _rt(tr, DL))], [])

    a_lru, u_lru = _lru_gates_fwd(xl, wa_b, b_a, wx_b, b_x, lru_lambda, tr)
    h_lru = _lru_scan_fwd(a_lru, u_lru, tr)

    def lru_post(hv, gate, g):
        return _rms(hv * jax.nn.gelu(gate), g)

    def f_lru_post(i, hv, gate, g):
        return (lru_post(hv, gate, g),), ()
    cb_l = DS // DL
    (mixcat,) = _rows_call("lru_post", f_lru_post, nt, [h_lru, proj, lru_norm],
                           [_rt(tr, DL), _rt(tr, DL, cb_gate), _full(lru_norm)],
                           [((T, DS + DL), BF16, _rt(tr, DL, cb_l))], [], into=mixcat)

    ag_srcs, ag_lands = _split_wait("allgather_later_wait", ag_sems, ag_srcs, ag_lands, mixcat, _ag_copies,
                                    4 * len(later))
    g_out, g_mi, g_mo = _ag_finish("allgather_later_finish", ag_lands)
    w_out_f = g_out.reshape(DS + DL, D)
    w_mi_f = jnp.transpose(g_mi, (1, 0, 2)).reshape(D, DFF)
    w_mo_f = g_mo.reshape(DFF, D)
    (mix,) = _mm("mix", mixcat, w_out_f)

    def f_post_mix(i, xv, mx, gpm, gpl):
        x1 = xv + _rms(mx, gpm)
        return (x1, _rms(x1, gpl)), ()
    x1, hn = _rows_call("post_mix", f_post_mix, nt, [x2, mix, post_mix_norm, pre_mlp_norm],
                        [_rt(tr, D), _rt(tr, D), _full(post_mix_norm), _full(pre_mlp_norm)],
                        [((T, D), F32, _rt(tr, D)), ((T, D), BF16, _rt(tr, D))], [])

    hm, act = _mm("mlp_in", hn, w_mi_f,
                  outs=((F32, None), (BF16, lambda r, e: jnp.square(jnp.maximum(r, 0.0)))))
    (hm2,) = _mm("mlp_out", act, w_mo_f)

    def f_final(i, x1v, hm2v, g, tg):
        def fwd(hv, gv):
            return x1v + _rms(hv, gv)
        x2v, vjp = jax.vjp(fwd, hm2v, g)
        err = x2v - tg
        dx2 = err * (1.0 / D)
        dh, dg = vjp(dx2)
        loss = jnp.full((1, LANES), 0.5 / D, F32) * jnp.sum(err * err)
        return (dx2, dh), (dg, loss)
    dx1a, dhm2, g_post_mlp, loss_part = _rows_call(
        "loss_head", f_final, nt, [x1, hm2, post_mlp_norm, tgt],
        [_rt(tr, D), _rt(tr, D), _full(post_mlp_norm), _rt(tr, D)],
        [((T, D), F32, _rt(tr, D)), ((T, D), BF16, _rt(tr, D))], [(1, D), (1, LANES)])

    def rs_begin(n, full):
        (from_sib,) = _rs_sibling("rs_sibling_" + n, [full])
        pair = _pair_sum("pair_sum_" + n, full, from_sib, c_idx)
        sems, srcs, lands, token = _split_start("rs_start_" + n, [pair], [lax.empty((3,) + pair.shape[1:], BF16)],
                                                _rs_copies, 3)
        return (sems, srcs, lands), token

    def rs_end(n, state, after):
        (pair,), (recv,) = _split_wait("rs_wait_" + n, *state, after, _rs_copies, 3)
        return pair, recv

    (gw_mo,) = _mm("dw_mlp_out", act, dhm2, ta=True, outs=((BF16, None),))
    rs_mo, tok = rs_begin('w_mlp_out', gw_mo.reshape(N_DEV, DFF // N_DEV, D))
    (dhm,) = _mm("d_mlp_act", dhm2, w_mo_f, tb=True, extra=hm,
                 outs=((BF16, lambda r, e: r * (2.0 * jnp.maximum(e, 0.0))),), deps=[tok])
    (gw_mi,) = _mm("dw_mlp_in", hn, dhm, ta=True, outs=((BF16, None),), out_blocks=N_DEV)
    rs_mi, tok = rs_begin('w_mlp_in', gw_mi)
    (dhn,) = _mm("d_mlp_in", dhm, w_mi_f, tb=True, deps=[tok])

    def f_post_mix_bwd(i, x1v, mx, gpm, gpl, dhnv, dxa):
        _, vjp1 = jax.vjp(_rms, x1v, gpl)
        dx1, dgpl = vjp1(dhnv)
        dx1 = dx1 + dxa
        _, vjp2 = jax.vjp(_rms, mx, gpm)
        dmx, dgpm = vjp2(dx1)
        return (dx1, dmx), (dgpl, dgpm)
    dx1, dmix, g_pre_mlp, g_post_mix = _rows_call(
        "post_mix_bwd", f_post_mix_bwd, nt, [x1, mix, post_mix_norm, pre_mlp_norm, dhn, dx1a],
        [_rt(tr, D), _rt(tr, D), _full(post_mix_norm), _full(pre_mlp_norm), _rt(tr, D), _rt(tr, D)],
        [((T, D), F32, _rt(tr, D)), ((T, D), BF16, _rt(tr, D))], [(1, D), (1, D)])

    (gw_out,) = _mm("dw_out", mixcat, dmix, ta=True, outs=((BF16, None),))
    rs_out, tok = rs_begin('w_out', gw_out.reshape(N_DEV, -1, D))
    (dmixcat,) = _mm("d_mix", dmix, w_out_f, tb=True, deps=[tok])

    def f_lru_post_bwd(i, hv, gate, g, dy):
        _, vjp = jax.vjp(lru_post, hv, gate, g)
        dh_, dgate, dg = vjp(dy)
        return (dgate, dh_), (dg,)
    dproj, dh_lru, g_lru_norm = _rows_call(
        "lru_post_bwd", f_lru_post_bwd, nt, [h_lru, proj, lru_norm, dmixcat],
        [_rt(tr, DL), _rt(tr, DL, cb_gate), _full(lru_norm), _rt(tr, DL, cb_l)],
        [((T, NP), BF16, _rt(tr, DL, cb_gate)), ((T, DL), F32, _rt(tr, DL))], [(1, DL)])

    du_lru, da_lru = _lru_scan_bwd(a_lru, h_lru, dh_lru, tr)
    dxl, g_wa, g_ba, g_wx, g_bx, g_lam = _lru_gates_bwd(xl, wa_b, b_a, wx_b, b_x, lru_lambda, da_lru, du_lru, tr)

    def f_lru_pre_bwd(i, xv, hp, xn, d, dn, w, b):
        dx, dw8, db = _conv_bwd_tile(i, nt, xv, hp, xn, d, dn, w, b, silu=False)
        return (dx,), (dw8, db)
    dproj, g_convl8, g_convl_b = _rows_call(
        "lru_pre_bwd", f_lru_pre_bwd, nt, [proj, proj, proj, dxl, dxl, conv_l, lru_conv_b],
        [_rt(tr, DL, cb_xl), _halo_prev(tr, DL, cb_xl), _halo_next(tr, DL, nt, cb_xl), _rt(tr, DL),
         _halo_next(tr, DL, nt), _full(conv_l), _full(lru_conv_b)],
        [((T, NP), BF16, _rt(tr, DL, cb_xl))], [(SUBLANES, DL), (1, DL)], into=dproj)

    def f_ssd_post_bwd(i, y, z, g, dy):
        _, vjp = jax.vjp(ssd_post, y, z, g)
        dy_, dz, dg = vjp(dy)
        return (dz, dy_), (dg,)
    dproj, dy_ssd, g_ssd_norm = _rows_call(
        "ssd_post_bwd", f_ssd_post_bwd, nt, [y_ssd, proj, ssd_norm, dmixcat],
        [_rt(tr, DS), _rt(tr, DS, cb_z), _full(ssd_norm), _rt(tr, DS, 0)],
        [((T, NP), BF16, _rt(tr, DS, cb_z)), ((T, DS), F32, _rt(tr, DS))], [(1, DS)], into=dproj)

    dxbc_act, ddt, g_alog, g_dskip = _ssd_bwd(xbc_act, dt, a_log, d_skip, h_prev, dy_ssd, n_heads, Q)

    def f_ssd_pre_bwd(i, xv, hp, xn, d, dn, dtr, ddtv, w, b, dtb):
        dx, dw8, db = _conv_bwd_tile(i, ntw, xv, hp, xn, d, dn, w, b, silu=True)
        ddtr = ddtv * jax.nn.sigmoid(dtr + dtb)
        return (dx, ddtr), (dw8, db, _colsum(ddtr))
    dproj, ddt_raw, g_convs8, g_convs_b, g_dtb = _rows_call(
        "ssd_pre_bwd", f_ssd_pre_bwd, ntw,
        [proj, proj, proj, dxbc_act, dxbc_act, proj, ddt, conv_s, ssd_conv_b, dt_bias],
        [_rt(trw, XBC), _halo_prev(trw, XBC), _halo_next(trw, XBC, ntw), _rt(trw, XBC), _halo_next(trw, XBC, ntw),
         _rt(trw, LANES, cb_dt), _rt(trw, LANES), _full(conv_s), _full(ssd_conv_b), _full(dt_bias)],
        [((T, NP), BF16, _rt(trw, XBC)), ((T, LANES), BF16, _rt(trw, LANES))],
        [(SUBLANES, XBC), (1, XBC), (1, LANES)], into=dproj)

    def f_place(i, v):
        return (v,), ()
    (dproj,) = _rows_call("place_ddt", f_place, 1, [ddt_raw], [_rt(T, LANES)],
                          [((T, NP), BF16, _rt(T, LANES, cb_dt))], [], into=dproj)
    small = {
        'ssd_conv_w': g_convs8[:CONV_WIDTH], 'ssd_conv_b': g_convs_b,
        'ssd_dt_bias': g_dtb[:, :n_heads], 'ssd_a_log': g_alog[:, :n_heads], 'ssd_d': g_dskip[:, :n_heads],
        'ssd_norm': g_ssd_norm, 'lru_conv_w': g_convl8[:CONV_WIDTH], 'lru_conv_b': g_convl_b,
        'lru_w_a': g_wa, 'lru_b_a': g_ba, 'lru_w_x': g_wx, 'lru_b_x': g_bx, 'lru_lambda': g_lam,
        'lru_norm': g_lru_norm, 'post_mix_norm': g_post_mix, 'pre_mlp_norm': g_pre_mlp,
        'post_mlp_norm': g_post_mlp, 'loss': loss_part[:, :1],
    }
    wide = ['lru_w_a', 'lru_w_x']
    narrow = [n for n in small if n not in wide]
    lb = lru_w_a.shape[-1]
    s_srcs = [_flat_rows([small[n] for n in narrow]), g_wa.reshape(-1, lb), g_wx.reshape(-1, lb)]
    s_sems, s_srcs, s_lands, tok = _split_start(
        "small_grads_start", [], [_own_block(a, dev) for a in s_srcs], _ag_copies, 4 * len(s_srcs))

    (gwp,) = _mm("dw_proj", h, dproj, ta=True, outs=((BF16, None),), deps=[tok])
    rs_in, tok = rs_begin(
        'w_in', jnp.stack([jnp.concatenate(my_cols(gwp, k * wb, (k + 1) * wb), axis=1) for k in range(N_DEV)]))
    (dh,) = _mm("d_proj", dproj, wp, tb=True, deps=[tok])

    def f_norm_in_bwd(i, xv, g, dhv, dxa):
        _, vjp = jax.vjp(_rms, xv, g)
        dx, dg = vjp(dhv)
        return (dx + dxa,), (dg,)
    grad_x, g_pre_mix = _rows_call(
        "norm_in_bwd", f_norm_in_bwd, nt, [x2, pre_mix_norm, dh, dx1],
        [_rt(tr, D), _full(pre_mix_norm), _rt(tr, D), _rt(tr, D)], [((T, D), F32, _rt(tr, D))], [(1, D)])

    (g_pm8,) = _allgather("allgather_pre_mix_grad", [_own_block(g_pre_mix, dev)])
    _, s_lands = _split_wait("small_grads_wait", s_sems, s_srcs, s_lands, g_pm8, _ag_copies, 4 * len(s_lands))
    g_narrow, g_wa8, g_wx8 = _ag_finish("small_grads_finish", s_lands)
    summed = dict(zip(narrow, _unflat(_sum8("sum_small_grads", g_narrow), [small[n].shape for n in narrow])))
    summed['pre_mix_norm'] = _sum8("sum_pre_mix_grad", g_pm8)
    summed['lru_w_a'] = _sum8("sum_lru_w_a_grads", g_wa8)
    summed['lru_w_x'] = _sum8("sum_lru_w_x_grads", g_wx8)
    loss = summed.pop('loss').reshape(())
    for n, full_w in (('ssd_conv_w', XBC), ('lru_conv_w', DL)):
        wdt = full_w // N_DEV
        summed[n] = lax.dynamic_slice_in_dim(summed[n], dev * wdt, wdt, axis=1)
    small_params = [n for n in names if n not in big]
    as2d = lambda a: a.reshape(-1, a.shape[-1])
    res = _adamw_small([as2d(W[n]) for n in small_params],
                       [summed[n].reshape(as2d(W[n]).shape) for n in small_params],
                       [as2d(Mo[n]) for n in small_params], [as2d(Vo[n]) for n in small_params])
    grads = {n: summed[n].reshape(W[n].shape) for n in small_params}
    delta, new_m, new_v = ({n: r.reshape(W[n].shape) for n, r in zip(small_params, rs)} for rs in res)

    for n, state in (('w_mlp_out', rs_mo), ('w_mlp_in', rs_mi), ('w_out', rs_out), ('w_in', rs_in)):
        p, r = rs_end(n, state, g_pm8)
        grads[n], delta[n], new_m[n], new_v[n] = _adamw_big("adamw_" + n, W[n], Mo[n], Vo[n], p, r, chip_idx)

    return (loss, grad_x.reshape(x.shape), *[grads[n] for n in names], *[delta[n] for n in names],
            *[new_m[n] for n in names], *[new_v[n] for n in names])
```

```python
import functools

import jax
import jax.numpy as jnp
from jax import lax
from jax.experimental import pallas as pl
from jax.experimental.pallas import tpu as pltpu

F32, BF16 = jnp.float32, jnp.bfloat16
S = jax.ShapeDtypeStruct
MESH = pl.DeviceIdType.MESH

SSD_GROUPS = 8
LRU_C = 8.0
EPS = 1e-6
CONV_WIDTH = 4
ADAM_LR, ADAM_B1, ADAM_B2, ADAM_EPS, ADAM_WD, ADAM_STEP = 0.001, 0.9, 0.999, 1e-08, 0.01, 10

LANES = 128
SUBLANES = 8
VMEM_LIMIT = 56 * 1024 * 1024
N_DEV = 8
SMALL_W = 512
HI = lax.Precision.HIGHEST


def _pcall(body, **kw):
    return pl.pallas_call(body, **kw)


def _cparams(sem=None, **kw):
    return pltpu.CompilerParams(dimension_semantics=sem, vmem_limit_bytes=VMEM_LIMIT, **kw)


def _pick(n, cands):
    for c in cands:
        if c <= n and n % c == 0:
            return c
    return n


def _rt(tr, w, cb=0, n=None):
    if n is None:
        return pl.BlockSpec((tr, w), lambda i: (i, cb))
    return pl.BlockSpec((tr, w), lambda i: (n - 1 - i, cb))


def _halo_prev(tr, w, cb=0, n=None):
    k = tr // SUBLANES
    if n is None:
        return pl.BlockSpec((SUBLANES, w), lambda i: (jnp.maximum(i * k - 1, 0), cb))
    return pl.BlockSpec((SUBLANES, w), lambda i: (jnp.maximum((n - 1 - i) * k - 1, 0), cb))


def _halo_next(tr, w, nt, cb=0, n=None):
    k = tr // SUBLANES
    last = nt * k - 1
    if n is None:
        return pl.BlockSpec((SUBLANES, w), lambda i: (jnp.minimum((i + 1) * k, last), cb))
    return pl.BlockSpec((SUBLANES, w), lambda i: (jnp.minimum((n - i) * k, last), cb))


def _full(a):
    nd = a.ndim
    return pl.BlockSpec(a.shape, lambda i: (0,) * nd)


def _rows_call(name, fn, n_tiles, arrays, in_specs, out_tiled, out_acc, kinds, into=None, deps=(), cw=None):
    n_in, n_t = len(arrays), len(out_tiled)
    n_skip = len(deps) + (0 if into is None else 1)
    width = in_specs[kinds.index('t')].block_shape[1]
    cols = [(0, width)] if cw is None else [(c, cw) for c in range(0, width, cw)]

    def body(*refs):
        i = pl.program_id(0)
        ins = refs[:n_in]
        outs = refs[n_in + n_skip:n_in + n_skip + n_t]
        accs = refs[n_in + n_skip + n_t:]
        if accs:
            @pl.when(i == 0)
            def _():
                for r in accs:
                    r[...] = jnp.zeros_like(r)

        def lanes(ref, rows, c0, w):
            return ref[rows, c0:c0 + w] if ref.shape[-1] == width else ref[rows, :]

        for c0, w in cols:
            touts, aouts = fn(i == 0, i == n_tiles - 1, *[lanes(r, slice(None), c0, w) for r in ins])
            for r, v in zip(outs, touts):
                if r.shape[-1] == width:
                    r[:, c0:c0 + w] = v.astype(r.dtype)
                else:
                    r[...] = v.astype(r.dtype)
            for r, v in zip(accs, aouts):
                if r.shape[-1] == width:
                    r[:, c0:c0 + w] += v
                else:
                    r[...] += v

    out_shape = [S(sh, dt) for sh, dt, _ in out_tiled] + [S(sh, F32) for sh in out_acc]
    out_specs = [sp for _, _, sp in out_tiled]
    for sh in out_acc:
        out_specs.append(pl.BlockSpec(sh, lambda i, nd=len(sh): (0,) * nd))
    in_specs = list(in_specs) + [_ANY] * len(deps)
    if into is None:
        return _pcall(body, name=name, grid=(n_tiles,), in_specs=in_specs, out_specs=out_specs,
                      out_shape=out_shape, compiler_params=_cparams(("arbitrary",)))(*arrays, *deps)
    return _pcall(body, name=name, grid=(n_tiles,), in_specs=in_specs + [_ANY], out_specs=out_specs,
                  out_shape=out_shape, input_output_aliases={n_in + len(deps): 0},
                  compiler_params=_cparams(("arbitrary",)))(*arrays, *deps, into)


def _rms(x, g):
    return x * lax.rsqrt(jnp.mean(x * x, axis=-1, keepdims=True) + EPS) * g


def _colsum(v):
    return jnp.sum(v, axis=0, keepdims=True)


_TILES = (1152, 1024, 896, 768, 640, 512, 384, 256, 128)
_K_TILES = (4096, 3456, 3072, 2688, 2048, 1536, 1344, 1152, 1024, 896, 768, 640, 512, 384, 256, 128)


def _mm(name, a, b, *, ta=False, tb=False, outs=((F32, None),), extra=None, out_blocks=None, tm=None, tn=None, tk=None,
        deps=()):
    M, K = (a.shape[1], a.shape[0]) if ta else a.shape
    b3 = b.ndim == 3
    if b3:
        nb_b, brows, bcols = b.shape
        N = brows if tb else nb_b * bcols
    else:
        N = b.shape[0] if tb else b.shape[1]
    n_lim = N if out_blocks is None else N // out_blocks
    if b3 and not tb:
        n_lim = min(n_lim, bcols)
    tm = tm or _pick(M, _TILES[1:])
    tn = tn or _pick(n_lim, _TILES)
    tk = tk or _pick(bcols if (b3 and tb) else K, _K_TILES)
    nk = K // tk
    assert M % tm == 0 and N % tn == 0 and K % tk == 0
    dn = (((0 if ta else 1,), (1 if tb else 0,)), ((), ()))
    n_extra = 0 if extra is None else 1
    n_out = len(outs)

    def body(*refs):
        a_ref, b_ref = refs[0], refs[1]
        e_ref = refs[2] if n_extra else None
        o_refs = refs[2 + n_extra + len(deps):2 + n_extra + len(deps) + n_out]

        def finish(r):
            e = e_ref[...] if n_extra else None
            for o, (_, f) in zip(o_refs, outs):
                o[...] = (r if f is None else f(r, e)).astype(o.dtype)

        part = lax.dot_general(a_ref[...], b_ref[...], dn, preferred_element_type=F32)
        if nk == 1:
            finish(part)
            return
        acc = refs[-1]
        k = pl.program_id(2)

        @pl.when(k == 0)
        def _():
            acc[...] = part

        @pl.when(jnp.logical_and(k > 0, k < nk - 1))
        def _():
            acc[...] += part

        @pl.when(k == nk - 1)
        def _():
            finish(acc[...] + part)

    a_spec = pl.BlockSpec((tk, tm), lambda i, j, k: (k, i)) if ta else pl.BlockSpec((tm, tk), lambda i, j, k: (i, k))
    if not b3:
        b_spec = pl.BlockSpec((tn, tk), lambda i, j, k: (j, k)) if tb else pl.BlockSpec((tk, tn), lambda i, j, k: (k, j))
    elif tb:
        per = bcols // tk
        b_spec = pl.BlockSpec((None, tn, tk), lambda i, j, k: (k // per, j, k % per))
    else:
        per = bcols // tn
        b_spec = pl.BlockSpec((None, tk, tn), lambda i, j, k: (j // per, k, j % per))
    o_spec = pl.BlockSpec((tm, tn), lambda i, j, k: (i, j))
    if out_blocks is None:
        out_specs, out_shape = [o_spec] * n_out, [S((M, N), dt) for dt, _ in outs]
    else:
        per_o = N // out_blocks // tn
        ob_spec = pl.BlockSpec((None, tm, tn), lambda i, j, k: (j // per_o, i, j % per_o))
        out_specs, out_shape = [ob_spec] * n_out, [S((out_blocks, M, N // out_blocks), dt) for dt, _ in outs]
    in_specs = [a_spec, b_spec] + ([o_spec] if n_extra else []) + [_ANY] * len(deps)
    args = [a, b] + ([extra] if n_extra else []) + list(deps)
    return _pcall(body, name=name, grid=(M // tm, N // tn, nk), in_specs=in_specs, out_specs=out_specs,
                  out_shape=out_shape, scratch_shapes=[pltpu.VMEM((tm, tn), F32)] if nk > 1 else [],
                  compiler_params=_cparams(("parallel", "parallel", "arbitrary")))(*args)


def _shift_down(x, halo, s):
    if s == 0:
        return x
    r = pltpu.roll(x, s, 0)
    hr = pltpu.roll(halo, s, 0)
    row = lax.broadcasted_iota(jnp.int32, halo.shape, 0)
    top = jnp.where(row < s, hr, r[:SUBLANES])
    if x.shape[0] == SUBLANES:
        return top
    return jnp.concatenate([top, r[SUBLANES:]], axis=0)


def _shift_up(x, nxt, s):
    if s == 0:
        return x
    n = x.shape[0]
    r = pltpu.roll(x, n - s, 0)
    nr = pltpu.roll(nxt, SUBLANES - s, 0)
    row = lax.broadcasted_iota(jnp.int32, nxt.shape, 0)
    bot = jnp.where(row >= SUBLANES - s, nr, r[n - SUBLANES:])
    if n == SUBLANES:
        return bot
    return jnp.concatenate([r[:n - SUBLANES], bot], axis=0)


def _conv_pre(x, halo, w, b):
    acc = b + w[CONV_WIDTH - 1:CONV_WIDTH, :] * x
    for k in range(CONV_WIDTH - 1):
        acc = acc + w[k:k + 1, :] * _shift_down(x, halo, CONV_WIDTH - 1 - k)
    return acc


def _silu_grad(p):
    s = jax.nn.sigmoid(p)
    return s * (1.0 + p * (1.0 - s))


def _conv_bwd_tile(first, last, x, hprev, xnext, d, dnext, w, b, silu):
    hprev = jnp.where(first, 0.0, hprev)
    if silu:
        d = d * _silu_grad(_conv_pre(x, hprev, w, b))
        pre_next = _conv_pre(xnext, x[x.shape[0] - SUBLANES:], w, b)
        dnext = dnext * _silu_grad(pre_next)
    dnext = jnp.where(last, 0.0, dnext)
    dx = w[CONV_WIDTH - 1:CONV_WIDTH, :] * d
    row8 = lax.broadcasted_iota(jnp.int32, (SUBLANES, x.shape[1]), 0)
    dw8 = jnp.where(row8 == CONV_WIDTH - 1, _colsum(d * x), 0.0)
    for k in range(CONV_WIDTH - 1):
        s = CONV_WIDTH - 1 - k
        dx = dx + w[k:k + 1, :] * _shift_up(d, dnext, s)
        dw8 = dw8 + jnp.where(row8 == k, _colsum(d * _shift_down(x, hprev, s)), 0.0)
    return dx, dw8, _colsum(d)


def _ssd_dims(xbc_act, n_heads):
    T, XBC = xbc_act.shape
    GN = XBC // 4
    DS = XBC - 2 * GN
    G = SSD_GROUPS
    N = GN // G
    P = DS // n_heads
    K = n_heads // G
    return T, XBC, DS, GN, G, N, P, K


def _ssd_common(dt, alog, Q):
    a = -jnp.exp(alog)
    adt = dt * a
    li = lax.broadcasted_iota(jnp.int32, (Q, Q), 0)
    si = lax.broadcasted_iota(jnp.int32, (Q, Q), 1)
    causal = li >= si
    ltri = causal.astype(F32)
    acs = jnp.dot(ltri, adt, precision=HI, preferred_element_type=F32)
    acs_row = lax.dot_general(adt, ltri, (((0,), (1,)), ((), ())), precision=HI,
                              preferred_element_type=F32)
    return a, adt, causal, ltri, acs, acs_row


def _expander(g, K, P, W):
    r = lax.broadcasted_iota(jnp.int32, (LANES, W), 0)
    c = lax.broadcasted_iota(jnp.int32, (LANES, W), 1)
    return (c // P + g * K == r).astype(F32)


def _dotb(a, b, dn=(((1,), (0,)), ((), ()))):
    return lax.dot_general(a.astype(BF16), b.astype(BF16), dn, preferred_element_type=F32)


def _dot_split(a, sel, terms, dn=(((1,), (0,)), ((), ()))):
    selb = sel.astype(BF16)
    out = None
    for _ in range(terms):
        piece = a.astype(BF16)
        part = lax.dot_general(piece, selb, dn, preferred_element_type=F32)
        out = part if out is None else out + part
        a = a - piece.astype(F32)
    return out


_NT = (((1,), (1,)), ((), ()))
_TN = (((0,), (0,)), ((), ()))


def _ssd_fwd(xbc_act, dt, alog, dskip, n_heads, Q):
    T, XBC, DS, GN, G, N, P, K = _ssd_dims(xbc_act, n_heads)
    W = K * P
    nc = T // Q

    def body(xs_ref, b_ref, c_ref, dt_ref, alog_ref, d_ref, y_ref, hp_ref, h_scr):
        ci = pl.program_id(0)

        @pl.when(ci == 0)
        def _():
            h_scr[...] = jnp.zeros_like(h_scr)

        dtv = dt_ref[...]
        a, adt, causal, ltri, acs, acs_row = _ssd_common(dtv, alog_ref[...], Q)
        lane_head = lax.broadcasted_iota(jnp.int32, (Q, W), 1) // P
        for g in range(G):
            eg = _expander(g, K, P, W)
            dtb = _dot_split(dtv, eg, 3)
            acsb = _dot_split(acs, eg, 3)
            lastb = acsb[Q - 1:Q, :]
            db = _dot_split(jnp.broadcast_to(d_ref[...], (SUBLANES, LANES)), eg, 3)[0:1, :]
            xg = xs_ref[:, g * W:(g + 1) * W]
            bg = b_ref[:, g * N:(g + 1) * N]
            cg = c_ref[:, g * N:(g + 1) * N]
            xt = xg * dtb
            sc = _dotb(cg, bg, _NT)
            yd = jnp.zeros((Q, W), F32)
            for k in range(K):
                h = g * K + k
                seg = acs[:, h:h + 1] - acs_row[h:h + 1, :]
                lh = jnp.where(causal, jnp.exp(jnp.minimum(seg, 0.0)), 0.0)
                xk = jnp.where(lane_head == k, xt, 0.0)
                yd = yd + _dotb(sc * lh, xk)
            hp = h_scr[g]
            yoff = _dotb(cg, hp) * jnp.exp(acsb)
            y_ref[:, g * W:(g + 1) * W] = yd + yoff + xg * db
            e_end = jnp.exp(lastb - acsb)
            st = _dotb(bg, xt * e_end, _TN)
            hp_ref[0, g] = hp
            h_scr[g] = jnp.exp(lastb) * hp + st

    cb = DS // GN
    in_specs = [pl.BlockSpec((Q, DS), lambda c: (c, 0)),
                pl.BlockSpec((Q, GN), lambda c: (c, cb)),
                pl.BlockSpec((Q, GN), lambda c: (c, cb + 1)),
                pl.BlockSpec((Q, LANES), lambda c: (c, 0)),
                pl.BlockSpec((1, LANES), lambda c: (0, 0)),
                pl.BlockSpec((1, LANES), lambda c: (0, 0))]
    out_specs = [pl.BlockSpec((Q, DS), lambda c: (c, 0)),
                 pl.BlockSpec((1, G, N, W), lambda c: (c, 0, 0, 0))]
    return _pcall(body, name="ssd_fwd", grid=(nc,), in_specs=in_specs, out_specs=out_specs,
                  out_shape=[S((T, DS), F32), S((nc, G, N, W), F32)],
                  scratch_shapes=[pltpu.VMEM((G, N, W), F32)],
                  compiler_params=_cparams(("arbitrary",)))(xbc_act, xbc_act, xbc_act, dt, alog, dskip)


def _ssd_bwd(xbc_act, dt, alog, dskip, hprev, dy, n_heads, Q):
    T, XBC, DS, GN, G, N, P, K = _ssd_dims(xbc_act, n_heads)
    W = K * P
    nc = T // Q

    def body(xs_ref, b_ref, c_ref, dt_ref, alog_ref, d_ref, hp_ref, dy_ref,
             dxbc_ref, ddt_ref, dalog_ref, dd_ref, dh_scr):
        ci = pl.program_id(0)

        @pl.when(ci == 0)
        def _():
            dh_scr[...] = jnp.zeros_like(dh_scr)
            dalog_ref[...] = jnp.zeros_like(dalog_ref)
            dd_ref[...] = jnp.zeros_like(dd_ref)

        dtv = dt_ref[...]
        a, adt, causal, ltri, acs, acs_row = _ssd_common(dtv, alog_ref[...], Q)
        lane_head = lax.broadcasted_iota(jnp.int32, (Q, W), 1) // P
        lane128 = lax.broadcasted_iota(jnp.int32, (Q, LANES), 1)
        sub128 = lax.broadcasted_iota(jnp.int32, (LANES, Q), 0)
        rowq = lax.broadcasted_iota(jnp.int32, (Q, W), 0)
        dacs = jnp.zeros((Q, LANES), F32)
        dacs_row = jnp.zeros((LANES, Q), F32)
        ddt = jnp.zeros((Q, LANES), F32)
        dd_acc = jnp.zeros((1, LANES), F32)
        for g in range(G):
            eg = _expander(g, K, P, W)
            dtb = _dot_split(dtv, eg, 3)
            acsb = _dot_split(acs, eg, 3)
            lastb = acsb[Q - 1:Q, :]
            db = _dot_split(jnp.broadcast_to(d_ref[...], (SUBLANES, LANES)), eg, 3)[0:1, :]
            xg = xs_ref[:, g * W:(g + 1) * W]
            bg = b_ref[:, g * N:(g + 1) * N]
            cg = c_ref[:, g * N:(g + 1) * N]
            dyg = dy_ref[:, g * W:(g + 1) * W]
            hp = hp_ref[0, g]
            dhn = dh_scr[g]
            xt = xg * dtb
            sc = _dotb(cg, bg, _NT)
            eacs = jnp.exp(acsb)
            e_end = jnp.exp(lastb - acsb)
            elast = jnp.exp(lastb)

            wv = dyg * eacs
            dcg = _dotb(wv, hp, _NT)
            dhp = _dotb(cg, wv, _TN) + elast * dhn
            dacsb = dyg * (_dotb(cg, hp) * eacs)

            xe = xt * e_end
            dbg = _dotb(xe, dhn, _NT)
            v = _dotb(bg, dhn)
            dxt = v * e_end
            de = v * xe
            dacsb = dacsb - de
            dlastb = _colsum(de) + elast * jnp.sum(dhn * hp, axis=0, keepdims=True)

            dsc = jnp.zeros((Q, Q), F32)
            for k in range(K):
                h = g * K + k
                seg = acs[:, h:h + 1] - acs_row[h:h + 1, :]
                lh = jnp.where(causal, jnp.exp(jnp.minimum(seg, 0.0)), 0.0)
                mh = sc * lh
                dyk = jnp.where(lane_head == k, dyg, 0.0)
                dxt = dxt + jnp.where(lane_head == k, _dotb(mh, dyg, _TN), 0.0)
                dm = _dotb(dyk, xt, _NT)
                dsc = dsc + dm * lh
                gm = dm * mh
                dacs = dacs + jnp.where(lane128 == h, jnp.sum(gm, axis=1, keepdims=True), 0.0)
                dacs_row = dacs_row - jnp.where(sub128 == h, jnp.sum(gm, axis=0, keepdims=True), 0.0)
            dcg = dcg + _dotb(dsc, bg)
            dbg = dbg + _dotb(dsc, cg, _TN)

            dacsb = dacsb + jnp.where(rowq == Q - 1, dlastb, 0.0)
            dacs = dacs + _dot_split(dacsb, eg, 2, _NT)
            ddt = ddt + _dot_split(dxt * xg, eg, 2, _NT)
            dd_acc = dd_acc + _dot_split(jnp.broadcast_to(_colsum(dyg * xg), (SUBLANES, W)), eg, 2, _NT)[0:1, :]
            dxbc_ref[:, g * W:(g + 1) * W] = dxt * dtb + dyg * db
            dxbc_ref[:, DS + g * N:DS + (g + 1) * N] = dbg
            dxbc_ref[:, DS + GN + g * N:DS + GN + (g + 1) * N] = dcg
            dh_scr[g] = dhp

        eye = (lax.broadcasted_iota(jnp.int32, (LANES, LANES), 0) ==
               lax.broadcasted_iota(jnp.int32, (LANES, LANES), 1)).astype(F32)
        dacs = dacs + lax.dot_general(dacs_row, eye, _TN, precision=HI, preferred_element_type=F32)
        dadt = lax.dot_general(ltri, dacs, _TN, precision=HI, preferred_element_type=F32)
        ddt_ref[...] = ddt + dadt * a
        dalog_ref[...] += _colsum(dadt * dtv) * a
        dd_ref[...] += dd_acc

    cb = DS // GN
    rv = lambda c: nc - 1 - c
    in_specs = [pl.BlockSpec((Q, DS), lambda c: (rv(c), 0)),
                pl.BlockSpec((Q, GN), lambda c: (rv(c), cb)),
                pl.BlockSpec((Q, GN), lambda c: (rv(c), cb + 1)),
                pl.BlockSpec((Q, LANES), lambda c: (rv(c), 0)),
                pl.BlockSpec((1, LANES), lambda c: (0, 0)),
                pl.BlockSpec((1, LANES), lambda c: (0, 0)),
                pl.BlockSpec((1, G, N, W), lambda c: (rv(c), 0, 0, 0)),
                pl.BlockSpec((Q, DS), lambda c: (rv(c), 0))]
    out_specs = [pl.BlockSpec((Q, XBC), lambda c: (rv(c), 0)),
                 pl.BlockSpec((Q, LANES), lambda c: (rv(c), 0)),
                 pl.BlockSpec((1, LANES), lambda c: (0, 0)),
                 pl.BlockSpec((1, LANES), lambda c: (0, 0))]
    return _pcall(body, name="ssd_bwd", grid=(nc,), in_specs=in_specs, out_specs=out_specs,
                  out_shape=[S((T, XBC), F32), S((T, LANES), F32), S((1, LANES), F32), S((1, LANES), F32)],
                  scratch_shapes=[pltpu.VMEM((G, N, W), F32)],
                  compiler_params=_cparams(("arbitrary",)))(
                      xbc_act, xbc_act, xbc_act, dt, alog, dskip, hprev, dy)


def _blockdiag(x, w_ref, dn=(((1,), (0,)), ((), ()))):
    H, B, _ = w_ref.shape
    return jnp.concatenate([_dotb(x[:, h * B:(h + 1) * B], w_ref[h], dn) for h in range(H)], axis=1)


def _lru_elem(xl, r_pre, i_pre, lam):
    r = jax.nn.sigmoid(r_pre)
    i = jax.nn.sigmoid(i_pre)
    log_a = -LRU_C * r * jax.nn.softplus(-lam)
    a = jnp.exp(log_a)
    u = jnp.sqrt(1.0 - jnp.exp(2.0 * log_a)) * (i * xl)
    return a, u


def _lru_gates_fwd(xl, w_a, b_a, w_x, b_x, lam, tr):
    T, DL = xl.shape

    def body(xl_ref, wa_ref, ba_ref, wx_ref, bx_ref, lam_ref, a_ref, u_ref):
        x = xl_ref[...]
        r_pre = _blockdiag(x, wa_ref) + ba_ref[...]
        i_pre = _blockdiag(x, wx_ref) + bx_ref[...]
        a, u = _lru_elem(x, r_pre, i_pre, lam_ref[...])
        a_ref[...] = a
        u_ref[...] = u

    w3 = pl.BlockSpec(w_a.shape, lambda i: (0, 0, 0))
    vec = pl.BlockSpec((1, DL), lambda i: (0, 0))
    return _pcall(body, name="lru_gates_fwd", grid=(T // tr,),
                  in_specs=[_rt(tr, DL), w3, vec, w3, vec, vec],
                  out_specs=[_rt(tr, DL), _rt(tr, DL)], out_shape=[S((T, DL), F32), S((T, DL), F32)],
                  compiler_params=_cparams(("parallel",)))(xl, w_a, b_a, w_x, b_x, lam)


def _lru_gates_bwd(xl, w_a, b_a, w_x, b_x, lam, da, du, tr):
    T, DL = xl.shape
    H, B, _ = w_a.shape

    def body(xl_ref, wa_ref, ba_ref, wx_ref, bx_ref, lam_ref, da_ref, du_ref,
             dxl_ref, dwa_ref, dba_ref, dwx_ref, dbx_ref, dlam_ref):
        @pl.when(pl.program_id(0) == 0)
        def _():
            for r in (dwa_ref, dba_ref, dwx_ref, dbx_ref, dlam_ref):
                r[...] = jnp.zeros_like(r)

        x = xl_ref[...]
        r_pre = _blockdiag(x, wa_ref) + ba_ref[...]
        i_pre = _blockdiag(x, wx_ref) + bx_ref[...]
        _, vjp = jax.vjp(_lru_elem, x, r_pre, i_pre, lam_ref[...])
        dx, dr, di, dlam = vjp((da_ref[...], du_ref[...]))
        dxl_ref[...] = dx + _blockdiag(dr, wa_ref, _NT) + _blockdiag(di, wx_ref, _NT)
        for h in range(H):
            xh = x[:, h * B:(h + 1) * B]
            dwa_ref[h] += _dotb(xh, dr[:, h * B:(h + 1) * B], _TN)
            dwx_ref[h] += _dotb(xh, di[:, h * B:(h + 1) * B], _TN)
        dba_ref[...] += _colsum(dr)
        dbx_ref[...] += _colsum(di)
        dlam_ref[...] += dlam

    w3 = pl.BlockSpec(w_a.shape, lambda i: (0, 0, 0))
    vec = pl.BlockSpec((1, DL), lambda i: (0, 0))
    return _pcall(body, name="lru_gates_bwd", grid=(T // tr,),
                  in_specs=[_rt(tr, DL), w3, vec, w3, vec, vec, _rt(tr, DL), _rt(tr, DL)],
                  out_specs=[_rt(tr, DL), w3, vec, w3, vec, vec],
                  out_shape=[S((T, DL), F32), S(w_a.shape, F32), S((1, DL), F32), S(w_a.shape, F32),
                             S((1, DL), F32), S((1, DL), F32)],
                  compiler_params=_cparams(("arbitrary",)))(xl, w_a, b_a, w_x, b_x, lam, da, du)


def _groups(v):
    return v.reshape(v.shape[0] // SUBLANES, SUBLANES, v.shape[1])


def _rows_shifted(v, edge, up):
    sub = lax.broadcasted_iota(jnp.int32, v.shape, 1)
    if up:
        other = jnp.concatenate([v[1:], edge[None]], axis=0)
        return jnp.where(sub < SUBLANES - 1, pltpu.roll(v, SUBLANES - 1, 1), pltpu.roll(other, SUBLANES - 1, 1))
    other = jnp.concatenate([edge[None], v[:-1]], axis=0)
    return jnp.where(sub >= 1, pltpu.roll(v, 1, 1), pltpu.roll(other, 1, 1))


def _scan_tile(a, u, entering, emit, up):
    G = a.shape[0]
    sub = lax.broadcasted_iota(jnp.int32, a.shape, 1)
    d = 1
    while d < SUBLANES:
        if up:
            keep = sub < SUBLANES - d
            a_s = jnp.where(keep, pltpu.roll(a, SUBLANES - d, 1), 1.0)
            u_s = jnp.where(keep, pltpu.roll(u, SUBLANES - d, 1), 0.0)
        else:
            keep = sub >= d
            a_s = jnp.where(keep, pltpu.roll(a, d, 1), 1.0)
            u_s = jnp.where(keep, pltpu.roll(u, d, 1), 0.0)
        u = a * u_s + u
        a = a * a_s
        d *= 2
    for g in (reversed(range(G)) if up else range(G)):
        hg = u[g] + a[g] * entering
        emit(g, hg)
        entering = hg[0:1] if up else hg[SUBLANES - 1:SUBLANES]
    return entering


def _lru_scan_fwd(a, u, tr):
    T, DL = a.shape

    def body(a_ref, u_ref, h_ref, carry):
        @pl.when(pl.program_id(0) == 0)
        def _():
            carry[...] = jnp.zeros_like(carry)

        def emit(g, hg):
            h_ref[g * SUBLANES:(g + 1) * SUBLANES, :] = hg

        last = _scan_tile(_groups(a_ref[...]), _groups(u_ref[...]), carry[0:1, :], emit, up=False)
        carry[...] = jnp.broadcast_to(last, carry.shape)

    return _pcall(body, name="lru_scan_fwd", grid=(T // tr,), in_specs=[_rt(tr, DL), _rt(tr, DL)],
                  out_specs=_rt(tr, DL), out_shape=S((T, DL), F32),
                  scratch_shapes=[pltpu.VMEM((SUBLANES, DL), F32)],
                  compiler_params=_cparams(("arbitrary",)))(a, u)


def _lru_scan_bwd(a, h, dh, tr):
    T, DL = a.shape
    n = T // tr

    def body(a_ref, an_ref, h_ref, hp_ref, dh_ref, du_ref, da_ref, carry):
        i = pl.program_id(0)
        ti = n - 1 - i

        @pl.when(i == 0)
        def _():
            carry[...] = jnp.zeros_like(carry)

        a_next = _rows_shifted(_groups(a_ref[...]), jnp.where(ti == n - 1, 0.0, an_ref[...]), up=True)
        h_prev = _rows_shifted(_groups(h_ref[...]), jnp.where(ti == 0, 0.0, hp_ref[...]), up=False)

        def emit(g, gg):
            du_ref[g * SUBLANES:(g + 1) * SUBLANES, :] = gg
            da_ref[g * SUBLANES:(g + 1) * SUBLANES, :] = gg * h_prev[g]

        top = _scan_tile(a_next, _groups(dh_ref[...]), carry[0:1, :], emit, up=True)
        carry[...] = jnp.broadcast_to(top, carry.shape)

    return _pcall(body, name="lru_scan_bwd", grid=(n,),
                  in_specs=[_rt(tr, DL, 0, n), _halo_next(tr, DL, n, 0, n), _rt(tr, DL, 0, n),
                            _halo_prev(tr, DL, 0, n), _rt(tr, DL, 0, n)],
                  out_specs=[_rt(tr, DL, 0, n), _rt(tr, DL, 0, n)],
                  out_shape=[S((T, DL), F32), S((T, DL), F32)],
                  scratch_shapes=[pltpu.VMEM((SUBLANES, DL), F32)],
                  compiler_params=_cparams(("arbitrary",)))(a, a, h, h, dh)


def _adamw(w, g, m, v):
    m = ADAM_B1 * m + (1.0 - ADAM_B1) * g
    v = ADAM_B2 * v + (1.0 - ADAM_B2) * (g * g)
    m_hat = m / (1.0 - ADAM_B1 ** ADAM_STEP)
    v_hat = v / (1.0 - ADAM_B2 ** ADAM_STEP)
    delta = -ADAM_LR * (m_hat / (jnp.sqrt(v_hat) + ADAM_EPS) + ADAM_WD * w)
    return delta, m, v


def _adamw_big(name, w, m, v, part, recv, chip_idx):
    _, R, C = w.shape
    tr = _pick(R, (256, 128, 64, 32, 16))

    def body(ci_ref, w_ref, m_ref, v_ref, p_ref, r0, r1, r2, g_ref, d_ref, nm_ref, nv_ref):
        g = ((p_ref[...].astype(F32) + r0[...].astype(F32)) + r1[...].astype(F32)) + r2[...].astype(F32)
        d, nm, nv = _adamw(w_ref[...], g, m_ref[...], v_ref[...])
        g_ref[...] = g
        d_ref[...] = d
        nm_ref[...] = nm
        nv_ref[...] = nv

    r_spec = lambda s: pl.BlockSpec((None, tr, C), lambda i, ci: (s, i, 0))
    t2 = r_spec(0)
    gs = pltpu.PrefetchScalarGridSpec(
        num_scalar_prefetch=1, grid=(R // tr,),
        in_specs=[t2, t2, t2, pl.BlockSpec((None, tr, C), lambda i, ci: (ci[0], i, 0)),
                  r_spec(0), r_spec(1), r_spec(2)],
        out_specs=[t2, t2, t2, t2])
    return _pcall(body, name=name, grid_spec=gs, out_shape=[S((1, R, C), F32)] * 4,
                  compiler_params=_cparams(("parallel",)))(chip_idx, w, m, v, part, recv, recv, recv)


def _adamw_small(ws, gs, ms, vs):
    n = len(ws)

    def body(*refs):
        for k in range(n):
            d, nm, nv = _adamw(refs[k][...], refs[n + k][...], refs[2 * n + k][...], refs[3 * n + k][...])
            refs[4 * n + k][...] = d
            refs[5 * n + k][...] = nm
            refs[6 * n + k][...] = nv

    res = _pcall(body, name="adamw_small", out_shape=[S(w.shape, F32) for w in ws] * 3,
                 compiler_params=_cparams())(*ws, *gs, *ms, *vs)
    return res[:n], res[n:2 * n], res[2 * n:]


def _sum8(name, parts):
    _, R, C = parts.shape

    def body(p_ref, o_ref):
        acc = p_ref[0]
        for k in range(1, N_DEV):
            acc = acc + p_ref[k]
        o_ref[...] = acc

    return _pcall(body, name=name, out_shape=S((R, C), F32), compiler_params=_cparams())(parts)


def _pair_sum(name, full, recv, c_idx):
    _, R, C = full.shape
    tr = _pick(R, (256, 128, 64, 32, 16))

    def body(c_ref, f_ref, r_ref, o_ref):
        o_ref[...] = (f_ref[...].astype(F32) + r_ref[...].astype(F32)).astype(o_ref.dtype)

    gs = pltpu.PrefetchScalarGridSpec(
        num_scalar_prefetch=1, grid=(4, R // tr),
        in_specs=[pl.BlockSpec((None, tr, C), lambda j, i, c: (2 * j + c[0], i, 0)),
                  pl.BlockSpec((None, tr, C), lambda j, i, c: (j, i, 0))],
        out_specs=pl.BlockSpec((None, tr, C), lambda j, i, c: (j, i, 0)))
    return _pcall(body, name=name, grid_spec=gs, out_shape=S((4, R, C), BF16),
                  compiler_params=_cparams(("parallel", "parallel")))(c_idx, full, recv)


def _cast_bf16(name, w, dev_idx):
    _, R, C = w.shape
    tr = _pick(R, (256, 128, 64, 32, 16))

    def body(d_ref, w_ref, o_ref):
        o_ref[...] = w_ref[...].astype(BF16)

    gs = pltpu.PrefetchScalarGridSpec(
        num_scalar_prefetch=1, grid=(R // tr,),
        in_specs=[pl.BlockSpec((None, tr, C), lambda i, d: (0, i, 0))],
        out_specs=pl.BlockSpec((None, tr, C), lambda i, d: (d[0], i, 0)))
    return _pcall(body, name=name, grid_spec=gs, out_shape=S((N_DEV, R, C), BF16),
                  compiler_params=_cparams(("parallel",)))(dev_idx, w)


def _own_block(v, dev):
    return lax.dynamic_update_slice(lax.empty((N_DEV,) + v.shape, v.dtype), v[None], (dev,) + (0,) * v.ndim)


_ANY = pl.BlockSpec(memory_space=pl.ANY)


def _position():
    return lax.axis_index("x"), lax.axis_index("y"), lax.axis_index("c")


def _allgather(name, bufs):
    n = len(bufs)

    def body(*refs):
        outs = refs[n:2 * n]
        send, recv = refs[2 * n:]
        x, y, c = _position()
        me, sib = (x, y, c), (x, y, 1 - c)
        chips = [(1 - x, y), (x, 1 - y), (1 - x, 1 - y)]

        def copy(a, k, block, to):
            bx, by, bc = block
            blk = outs[a].at[4 * bx + 2 * by + bc]
            return pltpu.make_async_remote_copy(
                src_ref=blk, dst_ref=blk, send_sem=send.at[a, k], recv_sem=recv.at[a, k],
                device_id=to, device_id_type=MESH)

        first = []
        for a in range(n):
            first.append(copy(a, 0, me, sib))
            first += [copy(a, 1 + j, me, (*chip, c)) for j, chip in enumerate(chips)]
        for cp in first:
            cp.start()
        passed = []
        for j, chip in enumerate(chips):
            for a in range(n):
                copy(a, 1 + j, (*chip, c), me).wait_recv()
                cp = copy(a, 4 + j, (*chip, c), sib)
                cp.start()
                passed.append(cp)
        for a in range(n):
            copy(a, 0, sib, me).wait_recv()
        for j, chip in enumerate(chips):
            for a in range(n):
                copy(a, 4 + j, (*chip, 1 - c), me).wait_recv()
        for cp in first + passed:
            cp.wait_send()

    return _pcall(body, name=name, in_specs=[_ANY] * n, out_specs=[_ANY] * n,
                  out_shape=[S(b.shape, b.dtype) for b in bufs], input_output_aliases={a: a for a in range(n)},
                  scratch_shapes=[pltpu.SemaphoreType.DMA((n, 7)), pltpu.SemaphoreType.DMA((n, 7))])(*bufs)


def _rs_sibling(name, fulls):
    n = len(fulls)

    def body(*refs):
        ins, outs = refs[:n], refs[n:2 * n]
        send, recv = refs[2 * n:]
        x, y, c = _position()
        copies = []
        for a in range(n):
            for j in range(4):
                copies.append(pltpu.make_async_remote_copy(
                    src_ref=ins[a].at[2 * j + (1 - c)], dst_ref=outs[a].at[j], send_sem=send.at[a, j],
                    recv_sem=recv.at[a, j], device_id=(x, y, 1 - c), device_id_type=MESH))
        for cp in copies:
            cp.start()
        for cp in copies:
            cp.wait()

    return _pcall(body, name=name, in_specs=[_ANY] * n, out_specs=[_ANY] * n,
                  out_shape=[S((4,) + f.shape[1:], f.dtype) for f in fulls],
                  scratch_shapes=[pltpu.SemaphoreType.DMA((n, 4)), pltpu.SemaphoreType.DMA((n, 4))])(*fulls)


_HBM = pl.BlockSpec(memory_space=pltpu.HBM)
_SEM = pl.BlockSpec(memory_space=pltpu.SEMAPHORE)
_EFFECT = pltpu.SideEffectType.DATAFLOW_SIDE_EFFECTING


def _remote_copies(copies_fn, srcs, lands, send, recv):
    x, y, c = _position()
    return [pltpu.make_async_remote_copy(src_ref=s, dst_ref=d, send_sem=send[i], recv_sem=recv[i], device_id=to,
                                         device_id_type=MESH)
            for i, (s, d, to) in enumerate(copies_fn(x, y, c, srcs, lands))]


def _split_start(name, srcs, lands, copies_fn, nc):
    n, nl = len(srcs), len(lands)

    def body(*refs):
        src_refs, land_refs = refs[:n], refs[n:n + nl]
        outs = refs[n + nl:]
        for cp in _remote_copies(copies_fn, src_refs, land_refs, outs[:nc], outs[nc:2 * nc]):
            cp.start()
        outs[-1][...] = jnp.zeros_like(outs[-1])

    hbm = lambda a: pltpu.with_memory_space_constraint(a, pltpu.HBM)
    res = _pcall(
        body, name=name, in_specs=[_HBM] * (n + nl),
        out_specs=[_SEM] * (2 * nc) + [_HBM] * (n + nl) + [pl.BlockSpec(memory_space=pltpu.VMEM)],
        out_shape=[pltpu.SemaphoreType.DMA(())] * (2 * nc) + [pltpu.HBM(s.shape, s.dtype) for s in srcs]
        + [pltpu.HBM(l.shape, l.dtype) for l in lands] + [S((SUBLANES, LANES), F32)],
        input_output_aliases={i: 2 * nc + i for i in range(n + nl)},
        compiler_params=pltpu.CompilerParams(has_side_effects=_EFFECT),
    )(*[hbm(s) for s in srcs], *[hbm(l) for l in lands])
    return res[:2 * nc], res[2 * nc:2 * nc + n], res[2 * nc + n:2 * nc + n + nl], res[-1]


def _split_wait(name, sems, srcs, lands, after, copies_fn, nc):
    n, nl = len(srcs), len(lands)

    def body(*refs):
        src_refs, land_refs = refs[:n], refs[n:n + nl]
        sem_refs = refs[n + nl:n + nl + 2 * nc]
        for cp in _remote_copies(copies_fn, src_refs, land_refs, sem_refs[:nc], sem_refs[nc:]):
            cp.wait_send()
            cp.wait_recv()

    res = _pcall(
        body, name=name, in_specs=[_HBM] * (n + nl) + [_SEM] * (2 * nc) + [_ANY],
        out_specs=[_HBM] * (n + nl), out_shape=[pltpu.HBM(a.shape, a.dtype) for a in list(srcs) + list(lands)],
        input_output_aliases={i: i for i in range(n + nl)},
        compiler_params=pltpu.CompilerParams(has_side_effects=_EFFECT),
    )(*srcs, *lands, *sems, after)
    return res[:n], res[n:]


def _other_chips(x, y):
    return [(1 - x, y), (x, 1 - y), (1 - x, 1 - y)]


def _ag_copies(x, y, c, srcs, lands):
    out = []
    for land in lands:
        blk = land.at[4 * x + 2 * y + c]
        out.append((blk, blk, (x, y, 1 - c)))
        out += [(blk, blk, (px, py, c)) for px, py in _other_chips(x, y)]
    return out


def _rs_copies(x, y, c, srcs, lands):
    return [(s.at[2 * px + py], land.at[j], (px, py, c))
            for s, land in zip(srcs, lands) for j, (px, py) in enumerate(_other_chips(x, y))]


def _sib_copies(x, y, c, srcs, lands):
    return [(s.at[2 * j + (1 - c)], land.at[j], (x, y, 1 - c)) for s, land in zip(srcs, lands) for j in range(4)]


def _ag_finish(name, lands):
    n = len(lands)

    def body(*refs):
        outs = refs[n:2 * n]
        send, recv = refs[2 * n:]
        x, y, c = _position()

        def swap(a, j, px, py, pc):
            blk = outs[a].at[4 * px + 2 * py + pc]
            return pltpu.make_async_remote_copy(src_ref=blk, dst_ref=blk, send_sem=send.at[a, j], recv_sem=recv.at[a, j],
                                                device_id=(x, y, 1 - c), device_id_type=MESH)

        chips = _other_chips(x, y)
        sends = [swap(a, j, px, py, c) for a in range(n) for j, (px, py) in enumerate(chips)]
        for cp in sends:
            cp.start()
        for a in range(n):
            for j, (px, py) in enumerate(chips):
                swap(a, j, px, py, 1 - c).wait_recv()
        for cp in sends:
            cp.wait_send()

    return _pcall(body, name=name, in_specs=[_ANY] * n, out_specs=[_ANY] * n,
                  out_shape=[S(l.shape, l.dtype) for l in lands], input_output_aliases={a: a for a in range(n)},
                  scratch_shapes=[pltpu.SemaphoreType.DMA((n, 3)), pltpu.SemaphoreType.DMA((n, 3))])(*lands)


def _pad_lanes(v):
    return jnp.pad(v, ((0, 0), (0, LANES - v.shape[1])))


def _flat_rows(pieces):
    flat = jnp.concatenate([p.reshape(-1) for p in pieces])
    rows = -(-flat.shape[0] // (SMALL_W * SUBLANES)) * SUBLANES
    return jnp.pad(flat, (0, rows * SMALL_W - flat.shape[0])).reshape(rows, SMALL_W)


def _unflat(buf, shapes):
    flat = buf.reshape(-1)
    out, off = [], 0
    for sh in shapes:
        n = 1
        for d in sh:
            n *= d
        out.append(flat[off:off + n].reshape(sh))
        off += n
    return out


def kernel(x, pre_mix_norm, w_in, ssd_conv_w, ssd_conv_b, ssd_dt_bias, ssd_a_log, ssd_d, ssd_norm, lru_conv_w, lru_conv_b, lru_w_a, lru_b_a, lru_w_x, lru_b_x, lru_lambda, lru_norm, w_out, post_mix_norm, pre_mlp_norm, w_mlp_in, w_mlp_out, post_mlp_norm, loss_target, m_pre_mix_norm, m_w_in, m_ssd_conv_w, m_ssd_conv_b, m_ssd_dt_bias, m_ssd_a_log, m_ssd_d, m_ssd_norm, m_lru_conv_w, m_lru_conv_b, m_lru_w_a, m_lru_b_a, m_lru_w_x, m_lru_b_x, m_lru_lambda, m_lru_norm, m_w_out, m_post_mix_norm, m_pre_mlp_norm, m_w_mlp_in, m_w_mlp_out, m_post_mlp_norm, v_pre_mix_norm, v_w_in, v_ssd_conv_w, v_ssd_conv_b, v_ssd_dt_bias, v_ssd_a_log, v_ssd_d, v_ssd_norm, v_lru_conv_w, v_lru_conv_b, v_lru_w_a, v_lru_b_a, v_lru_w_x, v_lru_b_x, v_lru_lambda, v_lru_norm, v_w_out, v_post_mix_norm, v_pre_mlp_norm, v_w_mlp_in, v_w_mlp_out, v_post_mlp_norm):
    names = ['pre_mix_norm', 'w_in', 'ssd_conv_w', 'ssd_conv_b', 'ssd_dt_bias', 'ssd_a_log', 'ssd_d', 'ssd_norm',
             'lru_conv_w', 'lru_conv_b', 'lru_w_a', 'lru_b_a', 'lru_w_x', 'lru_b_x', 'lru_lambda', 'lru_norm',
             'w_out', 'post_mix_norm', 'pre_mlp_norm', 'w_mlp_in', 'w_mlp_out', 'post_mlp_norm']
    loc = locals()
    W = {n: loc[n] for n in names}
    Mo = {n: loc["m_" + n] for n in names}
    Vo = {n: loc["v_" + n] for n in names}
    big = ['w_in', 'w_out', 'w_mlp_in', 'w_mlp_out']

    px, py, pc = _position()
    dev = 4 * px + 2 * py + pc
    dev_idx = jnp.reshape(dev, (1,)).astype(jnp.int32)
    c_idx = jnp.reshape(pc, (1,)).astype(jnp.int32)
    chip_idx = jnp.reshape(2 * px + py, (1,)).astype(jnp.int32)

    _, T, D = x.shape
    x2 = x.reshape(T, D)
    tgt = loss_target.reshape(T, D)
    n_heads = ssd_dt_bias.shape[1]
    XBC = ssd_conv_b.shape[1]
    GN = XBC // 4
    DS = XBC - 2 * GN
    DL = lru_norm.shape[1]
    DFF = w_mlp_in.shape[2] * N_DEV
    DIN = w_in.shape[2] * N_DEV
    NP = XBC + DS + 2 * DL + LANES
    assert DS % GN == 0 and XBC % DS == 0 and DS == DL and n_heads <= LANES
    cb_z, cb_gate, cb_xl, cb_dt = XBC // DS, XBC // DS + 1, XBC // DS + 2, (XBC + DS + 2 * DL) // LANES
    tr = min(256, T // 2)
    nt = T // tr
    Q = min(128, T // 2)

    sh = {n: _cast_bf16("cast_" + n, W[n], dev_idx) for n in big}
    g_in, g_cs, g_cl = _allgather(
        "allgather_w_in", [sh['w_in'], _own_block(ssd_conv_w[0], dev), _own_block(lru_conv_w[0], dev)])
    later = big[1:]
    ag_sems, ag_srcs, ag_lands, ag_token = _split_start(
        "allgather_later_start", [], [sh[n] for n in later], _ag_copies, 4 * len(later))
    conv_s = jnp.transpose(g_cs, (1, 0, 2)).reshape(CONV_WIDTH, XBC)
    conv_l = jnp.transpose(g_cl, (1, 0, 2)).reshape(CONV_WIDTH, DL)
    wb = DIN // N_DEV
    o_z, o_xbc, o_dt, o_gate, o_xl = 0, DS, DS + XBC, DS + XBC + n_heads, DS + XBC + n_heads + DL
    segs = [(o_xbc, o_xbc + XBC, 0), (o_z, o_z + DS, XBC), (o_gate, o_gate + DL, XBC + DS),
            (o_xl, o_xl + DL, XBC + DS + DL), (o_dt, o_dt + n_heads, NP - LANES)]

    def ref_cols(lo, hi):
        out = []
        while lo < hi:
            k = lo // wb
            e = min(hi, (k + 1) * wb)
            out.append(g_in[k, :, lo - k * wb:e - k * wb])
            lo = e
        return out

    def my_cols(g, lo, hi):
        out = []
        for a, b, m in sorted(segs):
            s, e = max(lo, a), min(hi, b)
            if s < e:
                out.append(g[:, m + s - a:m + e - a])
        return out

    wp = jnp.concatenate([p for a, b, _ in segs for p in ref_cols(a, b)]
                         + [jnp.zeros((D, LANES - n_heads), BF16)], axis=1)
    dt_bias = _pad_lanes(ssd_dt_bias)
    a_log = _pad_lanes(ssd_a_log)
    d_skip = _pad_lanes(ssd_d)
    wa_b, wx_b = lru_w_a[0].astype(BF16), lru_w_x[0].astype(BF16)
    b_a, b_x = lru_b_a.reshape(1, DL), lru_b_x.reshape(1, DL)

    def f_norm_in(first, last, xv, g):
        return (_rms(xv, g),), ()
    (h,) = _rows_call("norm_in", f_norm_in, nt, [x2, pre_mix_norm], [_rt(tr, D), _full(pre_mix_norm)],
                      [((T, D), BF16, _rt(tr, D))], [], 'tf', deps=[ag_token])

    (proj,) = _mm("proj", h, wp)

    cwx = min(1024, XBC)

    def f_ssd_pre(first, last, xbc, halo, w, b):
        pre = _conv_pre(xbc, jnp.where(first, 0.0, halo), w, b)
        return (pre * jax.nn.sigmoid(pre),), ()
    (xbc_act,) = _rows_call(
        "ssd_pre", f_ssd_pre, nt, [proj, proj, conv_s, ssd_conv_b],
        [_rt(tr, XBC), _halo_prev(tr, XBC), _full(conv_s), _full(ssd_conv_b)],
        [((T, XBC), F32, _rt(tr, XBC))], [], ['t', 'p0', 'f', 'f'], cw=cwx)

    def f_ssd_dt(first, last, dtr, dtb):
        return (jax.nn.softplus(dtr + dtb),), ()
    (dt,) = _rows_call("ssd_dt", f_ssd_dt, nt, [proj, dt_bias], [_rt(tr, LANES, cb_dt), _full(dt_bias)],
                       [((T, LANES), F32, _rt(tr, LANES))], [], 'tf')

    y_ssd, h_prev = _ssd_fwd(xbc_act, dt, a_log, d_skip, n_heads, Q)

    gw = DS // SSD_GROUPS

    def ssd_post(y, z, g):
        yz = y * jax.nn.silu(z)
        parts = []
        for k in range(SSD_GROUPS):
            yk = yz[:, k * gw:(k + 1) * gw]
            parts.append(yk * lax.rsqrt(jnp.mean(yk * yk, axis=-1, keepdims=True) + EPS))
        return jnp.concatenate(parts, axis=-1) * g

    def f_ssd_post(first, last, y, z, g):
        return (ssd_post(y, z, g),), ()
    (mixcat,) = _rows_call("ssd_post", f_ssd_post, nt, [y_ssd, proj, ssd_norm],
                           [_rt(tr, DS), _rt(tr, DS, cb_z), _full(ssd_norm)], [((T, DS + DL), BF16, _rt(tr, DS))], [],
                           'ttf')

    def f_lru_pre(first, last, xv, halo, w, b):
        return (_conv_pre(xv, jnp.where(first, 0.0, halo), w, b),), ()
    (xl,) = _rows_call("lru_pre", f_lru_pre, nt, [proj, proj, conv_l, lru_conv_b],
                       [_rt(tr, DL, cb_xl), _halo_prev(tr, DL, cb_xl), _full(conv_l), _full(lru_conv_b)],
                       [((T, DL), F32, _rt(tr, DL))], [], ['t', 'p0', 'f', 'f'])

    a_lru, u_lru = _lru_gates_fwd(xl, wa_b, b_a, wx_b, b_x, lru_lambda, tr)
    h_lru = _lru_scan_fwd(a_lru, u_lru, tr)

    def lru_post(hv, gate, g):
        return _rms(hv * jax.nn.gelu(gate), g)

    def f_lru_post(first, last, hv, gate, g):
        return (lru_post(hv, gate, g),), ()
    cb_l = DS // DL
    (mixcat,) = _rows_call("lru_post", f_lru_post, nt, [h_lru, proj, lru_norm],
                           [_rt(tr, DL), _rt(tr, DL, cb_gate), _full(lru_norm)],
                           [((T, DS + DL), BF16, _rt(tr, DL, cb_l))], [], 'ttf', into=mixcat)

    ag_srcs, ag_lands = _split_wait("allgather_later_wait", ag_sems, ag_srcs, ag_lands, mixcat, _ag_copies,
                                    4 * len(later))
    g_out, g_mi, g_mo = _ag_finish("allgather_later_finish", ag_lands)
    w_out_f = g_out.reshape(DS + DL, D)
    w_mi_f = jnp.transpose(g_mi, (1, 0, 2)).reshape(D, DFF)
    w_mo_f = g_mo.reshape(DFF, D)
    (mix,) = _mm("mix", mixcat, w_out_f)

    def f_post_mix(first, last, xv, mx, gpm, gpl):
        x1 = xv + _rms(mx, gpm)
        return (x1, _rms(x1, gpl)), ()
    x1, hn = _rows_call("post_mix", f_post_mix, nt, [x2, mix, post_mix_norm, pre_mlp_norm],
                        [_rt(tr, D), _rt(tr, D), _full(post_mix_norm), _full(pre_mlp_norm)],
                        [((T, D), F32, _rt(tr, D)), ((T, D), BF16, _rt(tr, D))], [], 'ttff')

    hm, act = _mm("mlp_in", hn, w_mi_f,
                  outs=((F32, None), (BF16, lambda r, e: jnp.square(jnp.maximum(r, 0.0)))))
    (hm2,) = _mm("mlp_out", act, w_mo_f)

    def f_final(first, last, x1v, hm2v, g, tg):
        def fwd(hv, gv):
            return x1v + _rms(hv, gv)
        x2v, vjp = jax.vjp(fwd, hm2v, g)
        err = x2v - tg
        dx2 = err * (1.0 / D)
        dh, dg = vjp(dx2)
        loss = jnp.full((1, LANES), 0.5 / D, F32) * jnp.sum(err * err)
        return (dx2, dh), (dg, loss)
    dx1a, dhm2, g_post_mlp, loss_part = _rows_call(
        "loss_head", f_final, nt, [x1, hm2, post_mlp_norm, tgt],
        [_rt(tr, D), _rt(tr, D), _full(post_mlp_norm), _rt(tr, D)],
        [((T, D), F32, _rt(tr, D)), ((T, D), BF16, _rt(tr, D))], [(1, D), (1, LANES)], 'ttft')

    def rs_chips_start(n, full, from_sib):
        pair = _pair_sum("pair_sum_" + n, full, from_sib, c_idx)
        sems, srcs, lands, token = _split_start("rs_start_" + n, [pair], [lax.empty((3,) + pair.shape[1:], BF16)],
                                                _rs_copies, 3)
        return (sems, srcs, lands), token

    def rs_begin(n, full):
        (from_sib,) = _rs_sibling("rs_sibling_" + n, [full])
        return rs_chips_start(n, full, from_sib)

    def rs_sibling_start(n, full):
        sems, srcs, lands, token = _split_start("rs_sibling_start_" + n, [full],
                                                [lax.empty((4,) + full.shape[1:], BF16)], _sib_copies, 4)
        return (sems, srcs, lands), token

    def rs_begin_late(n, state, after):
        (full,), (from_sib,) = _split_wait("rs_sibling_wait_" + n, *state, after, _sib_copies, 4)
        return rs_chips_start(n, full, from_sib)

    def rs_end(n, state, after):
        (pair,), (recv,) = _split_wait("rs_wait_" + n, *state, after, _rs_copies, 3)
        return pair, recv

    (gw_mo,) = _mm("dw_mlp_out", act, dhm2, ta=True, outs=((BF16, None),))
    sib_mo, tok = rs_sibling_start('w_mlp_out', gw_mo.reshape(N_DEV, DFF // N_DEV, D))
    (dhm,) = _mm("d_mlp_act", dhm2, w_mo_f, tb=True, extra=hm,
                 outs=((BF16, lambda r, e: r * (2.0 * jnp.maximum(e, 0.0))),), deps=[tok])
    rs_mo, tok = rs_begin_late('w_mlp_out', sib_mo, dhm)
    (gw_mi,) = _mm("dw_mlp_in", hn, dhm, ta=True, outs=((BF16, None),), out_blocks=N_DEV, deps=[tok])
    sib_mi, tok = rs_sibling_start('w_mlp_in', gw_mi)
    (dhn,) = _mm("d_mlp_in", dhm, w_mi_f, tb=True, deps=[tok])
    rs_mi, tok_mi = rs_begin_late('w_mlp_in', sib_mi, dhn)

    def f_post_mix_bwd(first, last, x1v, mx, gpm, gpl, dhnv, dxa):
        _, vjp1 = jax.vjp(_rms, x1v, gpl)
        dx1, dgpl = vjp1(dhnv)
        dx1 = dx1 + dxa
        _, vjp2 = jax.vjp(_rms, mx, gpm)
        dmx, dgpm = vjp2(dx1)
        return (dx1, dmx), (dgpl, dgpm)
    dx1, dmix, g_pre_mlp, g_post_mix = _rows_call(
        "post_mix_bwd", f_post_mix_bwd, nt, [x1, mix, post_mix_norm, pre_mlp_norm, dhn, dx1a],
        [_rt(tr, D), _rt(tr, D), _full(post_mix_norm), _full(pre_mlp_norm), _rt(tr, D), _rt(tr, D)],
        [((T, D), F32, _rt(tr, D)), ((T, D), BF16, _rt(tr, D))], [(1, D), (1, D)], 'ttfftt', deps=[tok_mi])

    (gw_out,) = _mm("dw_out", mixcat, dmix, ta=True, outs=((BF16, None),))
    sib_out, tok = rs_sibling_start('w_out', gw_out.reshape(N_DEV, -1, D))
    (dmixcat,) = _mm("d_mix", dmix, w_out_f, tb=True, deps=[tok])
    rs_out, tok_out = rs_begin_late('w_out', sib_out, dmixcat)

    def f_lru_post_bwd(first, last, hv, gate, g, dy):
        _, vjp = jax.vjp(lru_post, hv, gate, g)
        dh_, dgate, dg = vjp(dy)
        return (dgate, dh_), (dg,)
    dproj, dh_lru, g_lru_norm = _rows_call(
        "lru_post_bwd", f_lru_post_bwd, nt, [h_lru, proj, lru_norm, dmixcat],
        [_rt(tr, DL), _rt(tr, DL, cb_gate), _full(lru_norm), _rt(tr, DL, cb_l)],
        [((T, NP), BF16, _rt(tr, DL, cb_gate)), ((T, DL), F32, _rt(tr, DL))], [(1, DL)], 'ttft', deps=[tok_out])

    du_lru, da_lru = _lru_scan_bwd(a_lru, h_lru, dh_lru, tr)
    dxl, g_wa, g_ba, g_wx, g_bx, g_lam = _lru_gates_bwd(xl, wa_b, b_a, wx_b, b_x, lru_lambda, da_lru, du_lru, tr)

    conv_bwd_kinds = ['t', 'p0', 'n0', 't', 'n3', 'f', 'f']

    def f_lru_pre_bwd(first, last, xv, hp, xn, d, dn, w, b):
        dx, dw8, db = _conv_bwd_tile(first, last, xv, hp, xn, d, dn, w, b, silu=False)
        return (dx,), (dw8, db)
    dproj, g_convl8, g_convl_b = _rows_call(
        "lru_pre_bwd", f_lru_pre_bwd, nt, [proj, proj, proj, dxl, dxl, conv_l, lru_conv_b],
        [_rt(tr, DL, cb_xl), _halo_prev(tr, DL, cb_xl), _halo_next(tr, DL, nt, cb_xl), _rt(tr, DL),
         _halo_next(tr, DL, nt), _full(conv_l), _full(lru_conv_b)],
        [((T, NP), BF16, _rt(tr, DL, cb_xl))], [(SUBLANES, DL), (1, DL)], conv_bwd_kinds, into=dproj)

    def f_ssd_post_bwd(first, last, y, z, g, dy):
        _, vjp = jax.vjp(ssd_post, y, z, g)
        dy_, dz, dg = vjp(dy)
        return (dz, dy_), (dg,)
    dproj, dy_ssd, g_ssd_norm = _rows_call(
        "ssd_post_bwd", f_ssd_post_bwd, nt, [y_ssd, proj, ssd_norm, dmixcat],
        [_rt(tr, DS), _rt(tr, DS, cb_z), _full(ssd_norm), _rt(tr, DS, 0)],
        [((T, NP), BF16, _rt(tr, DS, cb_z)), ((T, DS), F32, _rt(tr, DS))], [(1, DS)], 'ttft', into=dproj)

    dxbc_act, ddt, g_alog, g_dskip = _ssd_bwd(xbc_act, dt, a_log, d_skip, h_prev, dy_ssd, n_heads, Q)

    def f_ssd_pre_bwd(first, last, xv, hp, xn, d, dn, w, b):
        dx, dw8, db = _conv_bwd_tile(first, last, xv, hp, xn, d, dn, w, b, silu=True)
        return (dx,), (dw8, db)
    dproj, g_convs8, g_convs_b = _rows_call(
        "ssd_pre_bwd", f_ssd_pre_bwd, nt, [proj, proj, proj, dxbc_act, dxbc_act, conv_s, ssd_conv_b],
        [_rt(tr, XBC), _halo_prev(tr, XBC), _halo_next(tr, XBC, nt), _rt(tr, XBC), _halo_next(tr, XBC, nt),
         _full(conv_s), _full(ssd_conv_b)],
        [((T, NP), BF16, _rt(tr, XBC))], [(SUBLANES, XBC), (1, XBC)], conv_bwd_kinds, into=dproj, cw=cwx)

    def f_ssd_dt_bwd(first, last, ddtv, dtr, dtb):
        ddtr = ddtv * jax.nn.sigmoid(dtr + dtb)
        return (ddtr,), (_colsum(ddtr),)
    dproj, g_dtb = _rows_call(
        "ssd_dt_bwd", f_ssd_dt_bwd, nt, [ddt, proj, dt_bias],
        [_rt(tr, LANES), _rt(tr, LANES, cb_dt), _full(dt_bias)],
        [((T, NP), BF16, _rt(tr, LANES, cb_dt))], [(1, LANES)], 'ttf', into=dproj)
    small = {
        'ssd_conv_w': g_convs8[:CONV_WIDTH], 'ssd_conv_b': g_convs_b,
        'ssd_dt_bias': g_dtb[:, :n_heads], 'ssd_a_log': g_alog[:, :n_heads], 'ssd_d': g_dskip[:, :n_heads],
        'ssd_norm': g_ssd_norm, 'lru_conv_w': g_convl8[:CONV_WIDTH], 'lru_conv_b': g_convl_b,
        'lru_w_a': g_wa, 'lru_b_a': g_ba, 'lru_w_x': g_wx, 'lru_b_x': g_bx, 'lru_lambda': g_lam,
        'lru_norm': g_lru_norm, 'post_mix_norm': g_post_mix, 'pre_mlp_norm': g_pre_mlp,
        'post_mlp_norm': g_post_mlp, 'loss': loss_part[:, :1],
    }
    wide = ['lru_w_a', 'lru_w_x']
    narrow = [n for n in small if n not in wide]
    lb = lru_w_a.shape[-1]
    s_srcs = [_flat_rows([small[n] for n in narrow]), g_wa.reshape(-1, lb), g_wx.reshape(-1, lb)]
    s_sems, s_srcs, s_lands, tok = _split_start(
        "small_grads_start", [], [_own_block(a, dev) for a in s_srcs], _ag_copies, 4 * len(s_srcs))

    (gwp,) = _mm("dw_proj", h, dproj, ta=True, outs=((BF16, None),), deps=[tok])
    rs_in, tok = rs_begin(
        'w_in', jnp.stack([jnp.concatenate(my_cols(gwp, k * wb, (k + 1) * wb), axis=1) for k in range(N_DEV)]))
    (dh,) = _mm("d_proj", dproj, wp, tb=True, deps=[tok])

    def f_norm_in_bwd(first, last, xv, g, dhv, dxa):
        _, vjp = jax.vjp(_rms, xv, g)
        dx, dg = vjp(dhv)
        return (dx + dxa,), (dg,)
    grad_x, g_pre_mix = _rows_call(
        "norm_in_bwd", f_norm_in_bwd, nt, [x2, pre_mix_norm, dh, dx1],
        [_rt(tr, D), _full(pre_mix_norm), _rt(tr, D), _rt(tr, D)], [((T, D), F32, _rt(tr, D))], [(1, D)], 'tftt')

    (g_pm8,) = _allgather("allgather_pre_mix_grad", [_own_block(g_pre_mix, dev)])
    _, s_lands = _split_wait("small_grads_wait", s_sems, s_srcs, s_lands, g_pm8, _ag_copies, 4 * len(s_lands))
    g_narrow, g_wa8, g_wx8 = _ag_finish("small_grads_finish", s_lands)
    summed = dict(zip(narrow, _unflat(_sum8("sum_small_grads", g_narrow), [small[n].shape for n in narrow])))
    summed['pre_mix_norm'] = _sum8("sum_pre_mix_grad", g_pm8)
    summed['lru_w_a'] = _sum8("sum_lru_w_a_grads", g_wa8)
    summed['lru_w_x'] = _sum8("sum_lru_w_x_grads", g_wx8)
    loss = summed.pop('loss').reshape(())
    for n, full_w in (('ssd_conv_w', XBC), ('lru_conv_w', DL)):
        wdt = full_w // N_DEV
        summed[n] = lax.dynamic_slice_in_dim(summed[n], dev * wdt, wdt, axis=1)
    small_params = [n for n in names if n not in big]
    as2d = lambda a: a.reshape(-1, a.shape[-1])
    res = _adamw_small([as2d(W[n]) for n in small_params],
                       [summed[n].reshape(as2d(W[n]).shape) for n in small_params],
                       [as2d(Mo[n]) for n in small_params], [as2d(Vo[n]) for n in small_params])
    grads = {n: summed[n].reshape(W[n].shape) for n in small_params}
    delta, new_m, new_v = ({n: r.reshape(W[n].shape) for n, r in zip(small_params, rs)} for rs in res)

    for n, state in (('w_mlp_out', rs_mo), ('w_mlp_in', rs_mi), ('w_out', rs_out), ('w_in', rs_in)):
        p, r = rs_end(n, state, g_pm8)
        grads[n], delta[n], new_m[n], new_v[n] = _adamw_big("adamw_" + n, W[n], Mo[n], Vo[n], p, r, chip_idx)

    return (loss, grad_x.reshape(x.shape), *[grads[n] for n in names], *[delta[n] for n in names],
            *[new_m[n] for n in names], *[new_v[n] for n in names])
```

```python
import functools

import jax
import jax.numpy as jnp
from jax import lax
from jax.experimental import pallas as pl
from jax.experimental.pallas import tpu as pltpu

F32, BF16 = jnp.float32, jnp.bfloat16
S = jax.ShapeDtypeStruct
MESH = pl.DeviceIdType.MESH

SSD_GROUPS = 8
LRU_C = 8.0
EPS = 1e-6
CONV_WIDTH = 4
ADAM_LR, ADAM_B1, ADAM_B2, ADAM_EPS, ADAM_WD, ADAM_STEP = 0.001, 0.9, 0.999, 1e-08, 0.01, 10

LANES = 128
SUBLANES = 8
VMEM_LIMIT = 56 * 1024 * 1024
N_DEV = 8
SMALL_W = 512
HI = lax.Precision.HIGHEST


def _pcall(body, **kw):
    return pl.pallas_call(body, **kw)


def _cparams(sem=None, **kw):
    return pltpu.CompilerParams(dimension_semantics=sem, vmem_limit_bytes=VMEM_LIMIT, **kw)


def _pick(n, cands):
    for c in cands:
        if c <= n and n % c == 0:
            return c
    return n


def _rt(tr, w, cb=0, n=None):
    if n is None:
        return pl.BlockSpec((tr, w), lambda i: (i, cb))
    return pl.BlockSpec((tr, w), lambda i: (n - 1 - i, cb))


PACKED_ROWS = 16


def _halo_prev(tr, w, cb=0, n=None, rows=SUBLANES):
    k = tr // rows
    if n is None:
        return pl.BlockSpec((rows, w), lambda i: (jnp.maximum(i * k - 1, 0), cb))
    return pl.BlockSpec((rows, w), lambda i: (jnp.maximum((n - 1 - i) * k - 1, 0), cb))


def _halo_next(tr, w, nt, cb=0, n=None, rows=SUBLANES):
    k = tr // rows
    last = nt * k - 1
    if n is None:
        return pl.BlockSpec((rows, w), lambda i: (jnp.minimum((i + 1) * k, last), cb))
    return pl.BlockSpec((rows, w), lambda i: (jnp.minimum((n - i) * k, last), cb))


def _full(a):
    nd = a.ndim
    return pl.BlockSpec(a.shape, lambda i: (0,) * nd)


def _rows_call(name, fn, n_tiles, arrays, in_specs, out_tiled, out_acc, kinds, into=None, deps=(), cw=None):
    n_in, n_t = len(arrays), len(out_tiled)
    n_skip = len(deps) + (0 if into is None else 1)
    width = in_specs[kinds.index('t')].block_shape[1]
    cols = [(0, width)] if cw is None else [(c, cw) for c in range(0, width, cw)]

    def body(*refs):
        i = pl.program_id(0)
        ins = refs[:n_in]
        outs = refs[n_in + n_skip:n_in + n_skip + n_t]
        accs = refs[n_in + n_skip + n_t:]
        if accs:
            @pl.when(i == 0)
            def _():
                for r in accs:
                    r[...] = jnp.zeros_like(r)

        def lanes(ref, rows, c0, w):
            return ref[rows, c0:c0 + w] if ref.shape[-1] == width else ref[rows, :]

        def load(k, c0, w):
            v = lanes(ins[k], slice(None), c0, w).astype(F32)
            if kinds[k][0] in 'pn' and v.shape[0] == PACKED_ROWS:
                v = v[SUBLANES:] if kinds[k][0] == 'p' else v[:SUBLANES]
            return v

        for c0, w in cols:
            touts, aouts = fn(i == 0, i == n_tiles - 1, *[load(k, c0, w) for k in range(n_in)])
            for r, v in zip(outs, touts):
                if r.shape[-1] == width:
                    r[:, c0:c0 + w] = v.astype(r.dtype)
                else:
                    r[...] = v.astype(r.dtype)
            for r, v in zip(accs, aouts):
                if r.shape[-1] == width:
                    r[:, c0:c0 + w] += v
                else:
                    r[...] += v

    out_shape = [S(sh, dt) for sh, dt, _ in out_tiled] + [S(sh, F32) for sh in out_acc]
    out_specs = [sp for _, _, sp in out_tiled]
    for sh in out_acc:
        out_specs.append(pl.BlockSpec(sh, lambda i, nd=len(sh): (0,) * nd))
    in_specs = list(in_specs) + [_ANY] * len(deps)
    if into is None:
        return _pcall(body, name=name, grid=(n_tiles,), in_specs=in_specs, out_specs=out_specs,
                      out_shape=out_shape, compiler_params=_cparams(("arbitrary",)))(*arrays, *deps)
    return _pcall(body, name=name, grid=(n_tiles,), in_specs=in_specs + [_ANY], out_specs=out_specs,
                  out_shape=out_shape, input_output_aliases={n_in + len(deps): 0},
                  compiler_params=_cparams(("arbitrary",)))(*arrays, *deps, into)


def _rms(x, g):
    return x * lax.rsqrt(jnp.mean(x * x, axis=-1, keepdims=True) + EPS) * g


def _colsum(v):
    return jnp.sum(v, axis=0, keepdims=True)


_TILES = (1152, 1024, 896, 768, 640, 512, 384, 256, 128)
_K_TILES = (4096, 3456, 3072, 2688, 2048, 1536, 1344, 1152, 1024, 896, 768, 640, 512, 384, 256, 128)


def _mm(name, a, b, *, ta=False, tb=False, outs=((F32, None),), extra=None, out_blocks=None, tm=None, tn=None, tk=None,
        deps=()):
    M, K = (a.shape[1], a.shape[0]) if ta else a.shape
    b3 = b.ndim == 3
    if b3:
        nb_b, brows, bcols = b.shape
        N = brows if tb else nb_b * bcols
    else:
        N = b.shape[0] if tb else b.shape[1]
    n_lim = N if out_blocks is None else N // out_blocks
    if b3 and not tb:
        n_lim = min(n_lim, bcols)
    tm = tm or _pick(M, _TILES[1:])
    tn = tn or _pick(n_lim, _TILES)
    tk = tk or _pick(bcols if (b3 and tb) else K, _K_TILES)
    nk = K // tk
    assert M % tm == 0 and N % tn == 0 and K % tk == 0
    dn = (((0 if ta else 1,), (1 if tb else 0,)), ((), ()))
    n_extra = 0 if extra is None else 1
    n_out = len(outs)

    def body(*refs):
        a_ref, b_ref = refs[0], refs[1]
        e_ref = refs[2] if n_extra else None
        o_refs = refs[2 + n_extra + len(deps):2 + n_extra + len(deps) + n_out]

        def finish(r):
            e = e_ref[...] if n_extra else None
            for o, (_, f) in zip(o_refs, outs):
                o[...] = (r if f is None else f(r, e)).astype(o.dtype)

        part = lax.dot_general(a_ref[...], b_ref[...], dn, preferred_element_type=F32)
        if nk == 1:
            finish(part)
            return
        acc = refs[-1]
        k = pl.program_id(2)

        @pl.when(k == 0)
        def _():
            acc[...] = part

        @pl.when(jnp.logical_and(k > 0, k < nk - 1))
        def _():
            acc[...] += part

        @pl.when(k == nk - 1)
        def _():
            finish(acc[...] + part)

    a_spec = pl.BlockSpec((tk, tm), lambda i, j, k: (k, i)) if ta else pl.BlockSpec((tm, tk), lambda i, j, k: (i, k))
    if not b3:
        b_spec = pl.BlockSpec((tn, tk), lambda i, j, k: (j, k)) if tb else pl.BlockSpec((tk, tn), lambda i, j, k: (k, j))
    elif tb:
        per = bcols // tk
        b_spec = pl.BlockSpec((None, tn, tk), lambda i, j, k: (k // per, j, k % per))
    else:
        per = bcols // tn
        b_spec = pl.BlockSpec((None, tk, tn), lambda i, j, k: (j // per, k, j % per))
    o_spec = pl.BlockSpec((tm, tn), lambda i, j, k: (i, j))
    if out_blocks is None:
        out_specs, out_shape = [o_spec] * n_out, [S((M, N), dt) for dt, _ in outs]
    else:
        per_o = N // out_blocks // tn
        ob_spec = pl.BlockSpec((None, tm, tn), lambda i, j, k: (j // per_o, i, j % per_o))
        out_specs, out_shape = [ob_spec] * n_out, [S((out_blocks, M, N // out_blocks), dt) for dt, _ in outs]
    in_specs = [a_spec, b_spec] + ([o_spec] if n_extra else []) + [_ANY] * len(deps)
    args = [a, b] + ([extra] if n_extra else []) + list(deps)
    return _pcall(body, name=name, grid=(M // tm, N // tn, nk), in_specs=in_specs, out_specs=out_specs,
                  out_shape=out_shape, scratch_shapes=[pltpu.VMEM((tm, tn), F32)] if nk > 1 else [],
                  compiler_params=_cparams(("parallel", "parallel", "arbitrary")))(*args)


def _shift_down(x, halo, s):
    if s == 0:
        return x
    r = pltpu.roll(x, s, 0)
    hr = pltpu.roll(halo, s, 0)
    row = lax.broadcasted_iota(jnp.int32, halo.shape, 0)
    top = jnp.where(row < s, hr, r[:SUBLANES])
    if x.shape[0] == SUBLANES:
        return top
    return jnp.concatenate([top, r[SUBLANES:]], axis=0)


def _shift_up(x, nxt, s):
    if s == 0:
        return x
    n = x.shape[0]
    r = pltpu.roll(x, n - s, 0)
    nr = pltpu.roll(nxt, SUBLANES - s, 0)
    row = lax.broadcasted_iota(jnp.int32, nxt.shape, 0)
    bot = jnp.where(row >= SUBLANES - s, nr, r[n - SUBLANES:])
    if n == SUBLANES:
        return bot
    return jnp.concatenate([r[:n - SUBLANES], bot], axis=0)


def _conv_pre(x, halo, w, b):
    acc = b + w[CONV_WIDTH - 1:CONV_WIDTH, :] * x
    for k in range(CONV_WIDTH - 1):
        acc = acc + w[k:k + 1, :] * _shift_down(x, halo, CONV_WIDTH - 1 - k)
    return acc


def _silu_grad(p):
    s = jax.nn.sigmoid(p)
    return s * (1.0 + p * (1.0 - s))


def _conv_bwd_tile(first, last, x, hprev, xnext, d, dnext, w, b, silu):
    hprev = jnp.where(first, 0.0, hprev)
    if silu:
        d = d * _silu_grad(_conv_pre(x, hprev, w, b))
        pre_next = _conv_pre(xnext, x[x.shape[0] - SUBLANES:], w, b)
        dnext = dnext * _silu_grad(pre_next)
    dnext = jnp.where(last, 0.0, dnext)
    dx = w[CONV_WIDTH - 1:CONV_WIDTH, :] * d
    row8 = lax.broadcasted_iota(jnp.int32, (SUBLANES, x.shape[1]), 0)
    dw8 = jnp.where(row8 == CONV_WIDTH - 1, _colsum(d * x), 0.0)
    for k in range(CONV_WIDTH - 1):
        s = CONV_WIDTH - 1 - k
        dx = dx + w[k:k + 1, :] * _shift_up(d, dnext, s)
        dw8 = dw8 + jnp.where(row8 == k, _colsum(d * _shift_down(x, hprev, s)), 0.0)
    return dx, dw8, _colsum(d)


def _ssd_dims(xbc_act, n_heads):
    T, XBC = xbc_act.shape
    GN = XBC // 4
    DS = XBC - 2 * GN
    G = SSD_GROUPS
    N = GN // G
    P = DS // n_heads
    K = n_heads // G
    return T, XBC, DS, GN, G, N, P, K


def _ssd_common(dt, alog, Q):
    a = -jnp.exp(alog)
    adt = dt * a
    li = lax.broadcasted_iota(jnp.int32, (Q, Q), 0)
    si = lax.broadcasted_iota(jnp.int32, (Q, Q), 1)
    causal = li >= si
    ltri = causal.astype(F32)
    acs = jnp.dot(ltri, adt, precision=HI, preferred_element_type=F32)
    acs_row = lax.dot_general(adt, ltri, (((0,), (1,)), ((), ())), precision=HI,
                              preferred_element_type=F32)
    return a, adt, causal, ltri, acs, acs_row


def _expander(g, K, P, W):
    r = lax.broadcasted_iota(jnp.int32, (LANES, W), 0)
    c = lax.broadcasted_iota(jnp.int32, (LANES, W), 1)
    return (c // P + g * K == r).astype(F32)


def _dotb(a, b, dn=(((1,), (0,)), ((), ()))):
    return lax.dot_general(a.astype(BF16), b.astype(BF16), dn, preferred_element_type=F32)


def _dot_split(a, sel, terms, dn=(((1,), (0,)), ((), ()))):
    selb = sel.astype(BF16)
    out = None
    for _ in range(terms):
        piece = a.astype(BF16)
        part = lax.dot_general(piece, selb, dn, preferred_element_type=F32)
        out = part if out is None else out + part
        a = a - piece.astype(F32)
    return out


_NT = (((1,), (1,)), ((), ()))
_TN = (((0,), (0,)), ((), ()))


def _ssd_fwd(xbc_act, dt, alog, dskip, n_heads, Q):
    T, XBC, DS, GN, G, N, P, K = _ssd_dims(xbc_act, n_heads)
    W = K * P
    nc = T // Q

    def body(xs_ref, b_ref, c_ref, dt_ref, alog_ref, d_ref, y_ref, hp_ref, h_scr):
        ci = pl.program_id(0)

        @pl.when(ci == 0)
        def _():
            h_scr[...] = jnp.zeros_like(h_scr)

        dtv = dt_ref[...]
        a, adt, causal, ltri, acs, acs_row = _ssd_common(dtv, alog_ref[...], Q)
        lane_head = lax.broadcasted_iota(jnp.int32, (Q, W), 1) // P
        for g in range(G):
            eg = _expander(g, K, P, W)
            dtb = _dot_split(dtv, eg, 3)
            acsb = _dot_split(acs, eg, 3)
            lastb = acsb[Q - 1:Q, :]
            db = _dot_split(jnp.broadcast_to(d_ref[...], (SUBLANES, LANES)), eg, 3)[0:1, :]
            xg = xs_ref[:, g * W:(g + 1) * W]
            bg = b_ref[:, g * N:(g + 1) * N]
            cg = c_ref[:, g * N:(g + 1) * N]
            xt = xg * dtb
            sc = _dotb(cg, bg, _NT)
            yd = jnp.zeros((Q, W), F32)
            for k in range(K):
                h = g * K + k
                seg = acs[:, h:h + 1] - acs_row[h:h + 1, :]
                lh = jnp.where(causal, jnp.exp(jnp.minimum(seg, 0.0)), 0.0)
                xk = jnp.where(lane_head == k, xt, 0.0)
                yd = yd + _dotb(sc * lh, xk)
            hp = h_scr[g]
            yoff = _dotb(cg, hp) * jnp.exp(acsb)
            y_ref[:, g * W:(g + 1) * W] = yd + yoff + xg * db
            e_end = jnp.exp(lastb - acsb)
            st = _dotb(bg, xt * e_end, _TN)
            hp_ref[0, g] = hp
            h_scr[g] = jnp.exp(lastb) * hp + st

    cb = DS // GN
    in_specs = [pl.BlockSpec((Q, DS), lambda c: (c, 0)),
                pl.BlockSpec((Q, GN), lambda c: (c, cb)),
                pl.BlockSpec((Q, GN), lambda c: (c, cb + 1)),
                pl.BlockSpec((Q, LANES), lambda c: (c, 0)),
                pl.BlockSpec((1, LANES), lambda c: (0, 0)),
                pl.BlockSpec((1, LANES), lambda c: (0, 0))]
    out_specs = [pl.BlockSpec((Q, DS), lambda c: (c, 0)),
                 pl.BlockSpec((1, G, N, W), lambda c: (c, 0, 0, 0))]
    return _pcall(body, name="ssd_fwd", grid=(nc,), in_specs=in_specs, out_specs=out_specs,
                  out_shape=[S((T, DS), F32), S((nc, G, N, W), F32)],
                  scratch_shapes=[pltpu.VMEM((G, N, W), F32)],
                  compiler_params=_cparams(("arbitrary",)))(xbc_act, xbc_act, xbc_act, dt, alog, dskip)


def _ssd_bwd(xbc_act, dt, alog, dskip, hprev, dy, n_heads, Q):
    T, XBC, DS, GN, G, N, P, K = _ssd_dims(xbc_act, n_heads)
    W = K * P
    nc = T // Q

    def body(xs_ref, b_ref, c_ref, dt_ref, alog_ref, d_ref, hp_ref, dy_ref,
             dxbc_ref, ddt_ref, dalog_ref, dd_ref, dh_scr):
        ci = pl.program_id(0)

        @pl.when(ci == 0)
        def _():
            dh_scr[...] = jnp.zeros_like(dh_scr)
            dalog_ref[...] = jnp.zeros_like(dalog_ref)
            dd_ref[...] = jnp.zeros_like(dd_ref)

        dtv = dt_ref[...]
        a, adt, causal, ltri, acs, acs_row = _ssd_common(dtv, alog_ref[...], Q)
        lane_head = lax.broadcasted_iota(jnp.int32, (Q, W), 1) // P
        lane128 = lax.broadcasted_iota(jnp.int32, (Q, LANES), 1)
        sub128 = lax.broadcasted_iota(jnp.int32, (LANES, Q), 0)
        rowq = lax.broadcasted_iota(jnp.int32, (Q, W), 0)
        dacs = jnp.zeros((Q, LANES), F32)
        dacs_row = jnp.zeros((LANES, Q), F32)
        ddt = jnp.zeros((Q, LANES), F32)
        dd_acc = jnp.zeros((1, LANES), F32)
        for g in range(G):
            eg = _expander(g, K, P, W)
            dtb = _dot_split(dtv, eg, 3)
            acsb = _dot_split(acs, eg, 3)
            lastb = acsb[Q - 1:Q, :]
            db = _dot_split(jnp.broadcast_to(d_ref[...], (SUBLANES, LANES)), eg, 3)[0:1, :]
            xg = xs_ref[:, g * W:(g + 1) * W]
            bg = b_ref[:, g * N:(g + 1) * N]
            cg = c_ref[:, g * N:(g + 1) * N]
            dyg = dy_ref[:, g * W:(g + 1) * W]
            hp = hp_ref[0, g]
            dhn = dh_scr[g]
            xt = xg * dtb
            sc = _dotb(cg, bg, _NT)
            eacs = jnp.exp(acsb)
            e_end = jnp.exp(lastb - acsb)
            elast = jnp.exp(lastb)

            wv = dyg * eacs
            dcg = _dotb(wv, hp, _NT)
            dhp = _dotb(cg, wv, _TN) + elast * dhn
            dacsb = dyg * (_dotb(cg, hp) * eacs)

            xe = xt * e_end
            dbg = _dotb(xe, dhn, _NT)
            v = _dotb(bg, dhn)
            dxt = v * e_end
            de = v * xe
            dacsb = dacsb - de
            dlastb = _colsum(de) + elast * jnp.sum(dhn * hp, axis=0, keepdims=True)

            dsc = jnp.zeros((Q, Q), F32)
            for k in range(K):
                h = g * K + k
                seg = acs[:, h:h + 1] - acs_row[h:h + 1, :]
                lh = jnp.where(causal, jnp.exp(jnp.minimum(seg, 0.0)), 0.0)
                mh = sc * lh
                dyk = jnp.where(lane_head == k, dyg, 0.0)
                dxt = dxt + jnp.where(lane_head == k, _dotb(mh, dyg, _TN), 0.0)
                dm = _dotb(dyk, xt, _NT)
                dsc = dsc + dm * lh
                gm = dm * mh
                dacs = dacs + jnp.where(lane128 == h, jnp.sum(gm, axis=1, keepdims=True), 0.0)
                dacs_row = dacs_row - jnp.where(sub128 == h, jnp.sum(gm, axis=0, keepdims=True), 0.0)
            dcg = dcg + _dotb(dsc, bg)
            dbg = dbg + _dotb(dsc, cg, _TN)

            dacsb = dacsb + jnp.where(rowq == Q - 1, dlastb, 0.0)
            dacs = dacs + _dot_split(dacsb, eg, 2, _NT)
            ddt = ddt + _dot_split(dxt * xg, eg, 2, _NT)
            dd_acc = dd_acc + _dot_split(jnp.broadcast_to(_colsum(dyg * xg), (SUBLANES, W)), eg, 2, _NT)[0:1, :]
            dxbc_ref[:, g * W:(g + 1) * W] = dxt * dtb + dyg * db
            dxbc_ref[:, DS + g * N:DS + (g + 1) * N] = dbg
            dxbc_ref[:, DS + GN + g * N:DS + GN + (g + 1) * N] = dcg
            dh_scr[g] = dhp

        eye = (lax.broadcasted_iota(jnp.int32, (LANES, LANES), 0) ==
               lax.broadcasted_iota(jnp.int32, (LANES, LANES), 1)).astype(F32)
        dacs = dacs + lax.dot_general(dacs_row, eye, _TN, precision=HI, preferred_element_type=F32)
        dadt = lax.dot_general(ltri, dacs, _TN, precision=HI, preferred_element_type=F32)
        ddt_ref[...] = ddt + dadt * a
        dalog_ref[...] += _colsum(dadt * dtv) * a
        dd_ref[...] += dd_acc

    cb = DS // GN
    rv = lambda c: nc - 1 - c
    in_specs = [pl.BlockSpec((Q, DS), lambda c: (rv(c), 0)),
                pl.BlockSpec((Q, GN), lambda c: (rv(c), cb)),
                pl.BlockSpec((Q, GN), lambda c: (rv(c), cb + 1)),
                pl.BlockSpec((Q, LANES), lambda c: (rv(c), 0)),
                pl.BlockSpec((1, LANES), lambda c: (0, 0)),
                pl.BlockSpec((1, LANES), lambda c: (0, 0)),
                pl.BlockSpec((1, G, N, W), lambda c: (rv(c), 0, 0, 0)),
                pl.BlockSpec((Q, DS), lambda c: (rv(c), 0))]
    out_specs = [pl.BlockSpec((Q, XBC), lambda c: (rv(c), 0)),
                 pl.BlockSpec((Q, LANES), lambda c: (rv(c), 0)),
                 pl.BlockSpec((1, LANES), lambda c: (0, 0)),
                 pl.BlockSpec((1, LANES), lambda c: (0, 0))]
    return _pcall(body, name="ssd_bwd", grid=(nc,), in_specs=in_specs, out_specs=out_specs,
                  out_shape=[S((T, XBC), F32), S((T, LANES), F32), S((1, LANES), F32), S((1, LANES), F32)],
                  scratch_shapes=[pltpu.VMEM((G, N, W), F32)],
                  compiler_params=_cparams(("arbitrary",)))(
                      xbc_act, xbc_act, xbc_act, dt, alog, dskip, hprev, dy)


def _blockdiag(x, w_ref, dn=(((1,), (0,)), ((), ()))):
    H, B, _ = w_ref.shape
    return jnp.concatenate([_dotb(x[:, h * B:(h + 1) * B], w_ref[h], dn) for h in range(H)], axis=1)


def _lru_elem(xl, r_pre, i_pre, lam):
    r = jax.nn.sigmoid(r_pre)
    i = jax.nn.sigmoid(i_pre)
    log_a = -LRU_C * r * jax.nn.softplus(-lam)
    a = jnp.exp(log_a)
    u = jnp.sqrt(1.0 - jnp.exp(2.0 * log_a)) * (i * xl)
    return a, u


def _lru_gates_fwd(xl, w_a, b_a, w_x, b_x, lam, tr):
    T, DL = xl.shape

    def body(xl_ref, wa_ref, ba_ref, wx_ref, bx_ref, lam_ref, a_ref, u_ref):
        x = xl_ref[...]
        r_pre = _blockdiag(x, wa_ref) + ba_ref[...]
        i_pre = _blockdiag(x, wx_ref) + bx_ref[...]
        a, u = _lru_elem(x, r_pre, i_pre, lam_ref[...])
        a_ref[...] = a
        u_ref[...] = u

    w3 = pl.BlockSpec(w_a.shape, lambda i: (0, 0, 0))
    vec = pl.BlockSpec((1, DL), lambda i: (0, 0))
    return _pcall(body, name="lru_gates_fwd", grid=(T // tr,),
                  in_specs=[_rt(tr, DL), w3, vec, w3, vec, vec],
                  out_specs=[_rt(tr, DL), _rt(tr, DL)], out_shape=[S((T, DL), F32), S((T, DL), F32)],
                  compiler_params=_cparams(("parallel",)))(xl, w_a, b_a, w_x, b_x, lam)


def _lru_gates_bwd(xl, w_a, b_a, w_x, b_x, lam, da, du, tr):
    T, DL = xl.shape
    H, B, _ = w_a.shape

    def body(xl_ref, wa_ref, ba_ref, wx_ref, bx_ref, lam_ref, da_ref, du_ref,
             dxl_ref, dwa_ref, dba_ref, dwx_ref, dbx_ref, dlam_ref):
        @pl.when(pl.program_id(0) == 0)
        def _():
            for r in (dwa_ref, dba_ref, dwx_ref, dbx_ref, dlam_ref):
                r[...] = jnp.zeros_like(r)

        x = xl_ref[...]
        r_pre = _blockdiag(x, wa_ref) + ba_ref[...]
        i_pre = _blockdiag(x, wx_ref) + bx_ref[...]
        _, vjp = jax.vjp(_lru_elem, x, r_pre, i_pre, lam_ref[...])
        dx, dr, di, dlam = vjp((da_ref[...], du_ref[...]))
        dxl_ref[...] = dx + _blockdiag(dr, wa_ref, _NT) + _blockdiag(di, wx_ref, _NT)
        for h in range(H):
            xh = x[:, h * B:(h + 1) * B]
            dwa_ref[h] += _dotb(xh, dr[:, h * B:(h + 1) * B], _TN)
            dwx_ref[h] += _dotb(xh, di[:, h * B:(h + 1) * B], _TN)
        dba_ref[...] += _colsum(dr)
        dbx_ref[...] += _colsum(di)
        dlam_ref[...] += dlam

    w3 = pl.BlockSpec(w_a.shape, lambda i: (0, 0, 0))
    vec = pl.BlockSpec((1, DL), lambda i: (0, 0))
    return _pcall(body, name="lru_gates_bwd", grid=(T // tr,),
                  in_specs=[_rt(tr, DL), w3, vec, w3, vec, vec, _rt(tr, DL), _rt(tr, DL)],
                  out_specs=[_rt(tr, DL), w3, vec, w3, vec, vec],
                  out_shape=[S((T, DL), F32), S(w_a.shape, F32), S((1, DL), F32), S(w_a.shape, F32),
                             S((1, DL), F32), S((1, DL), F32)],
                  compiler_params=_cparams(("arbitrary",)))(xl, w_a, b_a, w_x, b_x, lam, da, du)


def _groups(v):
    return v.reshape(v.shape[0] // SUBLANES, SUBLANES, v.shape[1])


def _rows_shifted(v, edge, up):
    sub = lax.broadcasted_iota(jnp.int32, v.shape, 1)
    if up:
        other = jnp.concatenate([v[1:], edge[None]], axis=0)
        return jnp.where(sub < SUBLANES - 1, pltpu.roll(v, SUBLANES - 1, 1), pltpu.roll(other, SUBLANES - 1, 1))
    other = jnp.concatenate([edge[None], v[:-1]], axis=0)
    return jnp.where(sub >= 1, pltpu.roll(v, 1, 1), pltpu.roll(other, 1, 1))


def _scan_tile(a, u, entering, emit, up):
    G = a.shape[0]
    sub = lax.broadcasted_iota(jnp.int32, a.shape, 1)
    d = 1
    while d < SUBLANES:
        if up:
            keep = sub < SUBLANES - d
            a_s = jnp.where(keep, pltpu.roll(a, SUBLANES - d, 1), 1.0)
            u_s = jnp.where(keep, pltpu.roll(u, SUBLANES - d, 1), 0.0)
        else:
            keep = sub >= d
            a_s = jnp.where(keep, pltpu.roll(a, d, 1), 1.0)
            u_s = jnp.where(keep, pltpu.roll(u, d, 1), 0.0)
        u = a * u_s + u
        a = a * a_s
        d *= 2
    for g in (reversed(range(G)) if up else range(G)):
        hg = u[g] + a[g] * entering
        emit(g, hg)
        entering = hg[0:1] if up else hg[SUBLANES - 1:SUBLANES]
    return entering


def _lru_scan_fwd(a, u, tr):
    T, DL = a.shape

    def body(a_ref, u_ref, h_ref, carry):
        @pl.when(pl.program_id(0) == 0)
        def _():
            carry[...] = jnp.zeros_like(carry)

        def emit(g, hg):
            h_ref[g * SUBLANES:(g + 1) * SUBLANES, :] = hg

        last = _scan_tile(_groups(a_ref[...]), _groups(u_ref[...]), carry[0:1, :], emit, up=False)
        carry[...] = jnp.broadcast_to(last, carry.shape)

    return _pcall(body, name="lru_scan_fwd", grid=(T // tr,), in_specs=[_rt(tr, DL), _rt(tr, DL)],
                  out_specs=_rt(tr, DL), out_shape=S((T, DL), F32),
                  scratch_shapes=[pltpu.VMEM((SUBLANES, DL), F32)],
                  compiler_params=_cparams(("arbitrary",)))(a, u)


def _lru_scan_bwd(a, h, dh, tr):
    T, DL = a.shape
    n = T // tr

    def body(a_ref, an_ref, h_ref, hp_ref, dh_ref, du_ref, da_ref, carry):
        i = pl.program_id(0)
        ti = n - 1 - i

        @pl.when(i == 0)
        def _():
            carry[...] = jnp.zeros_like(carry)

        a_next = _rows_shifted(_groups(a_ref[...]), jnp.where(ti == n - 1, 0.0, an_ref[...]), up=True)
        h_prev = _rows_shifted(_groups(h_ref[...]), jnp.where(ti == 0, 0.0, hp_ref[...]), up=False)

        def emit(g, gg):
            du_ref[g * SUBLANES:(g + 1) * SUBLANES, :] = gg
            da_ref[g * SUBLANES:(g + 1) * SUBLANES, :] = gg * h_prev[g]

        top = _scan_tile(a_next, _groups(dh_ref[...]), carry[0:1, :], emit, up=True)
        carry[...] = jnp.broadcast_to(top, carry.shape)

    return _pcall(body, name="lru_scan_bwd", grid=(n,),
                  in_specs=[_rt(tr, DL, 0, n), _halo_next(tr, DL, n, 0, n), _rt(tr, DL, 0, n),
                            _halo_prev(tr, DL, 0, n), _rt(tr, DL, 0, n)],
                  out_specs=[_rt(tr, DL, 0, n), _rt(tr, DL, 0, n)],
                  out_shape=[S((T, DL), F32), S((T, DL), F32)],
                  scratch_shapes=[pltpu.VMEM((SUBLANES, DL), F32)],
                  compiler_params=_cparams(("arbitrary",)))(a, a, h, h, dh)


def _adamw(w, g, m, v):
    m = ADAM_B1 * m + (1.0 - ADAM_B1) * g
    v = ADAM_B2 * v + (1.0 - ADAM_B2) * (g * g)
    m_hat = m / (1.0 - ADAM_B1 ** ADAM_STEP)
    v_hat = v / (1.0 - ADAM_B2 ** ADAM_STEP)
    delta = -ADAM_LR * (m_hat / (jnp.sqrt(v_hat) + ADAM_EPS) + ADAM_WD * w)
    return delta, m, v


def _adamw_big(name, w, m, v, part, recv, chip_idx):
    _, R, C = w.shape
    tr = _pick(R, (256, 128, 64, 32, 16))

    def body(ci_ref, w_ref, m_ref, v_ref, p_ref, r0, r1, r2, g_ref, d_ref, nm_ref, nv_ref):
        g = ((p_ref[...].astype(F32) + r0[...].astype(F32)) + r1[...].astype(F32)) + r2[...].astype(F32)
        d, nm, nv = _adamw(w_ref[...], g, m_ref[...], v_ref[...])
        g_ref[...] = g
        d_ref[...] = d
        nm_ref[...] = nm
        nv_ref[...] = nv

    r_spec = lambda s: pl.BlockSpec((None, tr, C), lambda i, ci: (s, i, 0))
    t2 = r_spec(0)
    gs = pltpu.PrefetchScalarGridSpec(
        num_scalar_prefetch=1, grid=(R // tr,),
        in_specs=[t2, t2, t2, pl.BlockSpec((None, tr, C), lambda i, ci: (ci[0], i, 0)),
                  r_spec(0), r_spec(1), r_spec(2)],
        out_specs=[t2, t2, t2, t2])
    return _pcall(body, name=name, grid_spec=gs, out_shape=[S((1, R, C), F32)] * 4,
                  compiler_params=_cparams(("parallel",)))(chip_idx, w, m, v, part, recv, recv, recv)


def _adamw_small(ws, gs, ms, vs):
    n = len(ws)

    def body(*refs):
        for k in range(n):
            d, nm, nv = _adamw(refs[k][...], refs[n + k][...], refs[2 * n + k][...], refs[3 * n + k][...])
            refs[4 * n + k][...] = d
            refs[5 * n + k][...] = nm
            refs[6 * n + k][...] = nv

    res = _pcall(body, name="adamw_small", out_shape=[S(w.shape, F32) for w in ws] * 3,
                 compiler_params=_cparams())(*ws, *gs, *ms, *vs)
    return res[:n], res[n:2 * n], res[2 * n:]


def _sum8(name, parts):
    _, R, C = parts.shape

    def body(p_ref, o_ref):
        acc = p_ref[0]
        for k in range(1, N_DEV):
            acc = acc + p_ref[k]
        o_ref[...] = acc

    return _pcall(body, name=name, out_shape=S((R, C), F32), compiler_params=_cparams())(parts)


def _pair_sum(name, full, recv, c_idx):
    _, R, C = full.shape
    tr = _pick(R, (256, 128, 64, 32, 16))

    def body(c_ref, f_ref, r_ref, o_ref):
        o_ref[...] = (f_ref[...].astype(F32) + r_ref[...].astype(F32)).astype(o_ref.dtype)

    gs = pltpu.PrefetchScalarGridSpec(
        num_scalar_prefetch=1, grid=(4, R // tr),
        in_specs=[pl.BlockSpec((None, tr, C), lambda j, i, c: (2 * j + c[0], i, 0)),
                  pl.BlockSpec((None, tr, C), lambda j, i, c: (j, i, 0))],
        out_specs=pl.BlockSpec((None, tr, C), lambda j, i, c: (j, i, 0)))
    return _pcall(body, name=name, grid_spec=gs, out_shape=S((4, R, C), BF16),
                  compiler_params=_cparams(("parallel", "parallel")))(c_idx, full, recv)


def _cast_bf16(name, w, dev_idx):
    _, R, C = w.shape
    tr = _pick(R, (256, 128, 64, 32, 16))

    def body(d_ref, w_ref, o_ref):
        o_ref[...] = w_ref[...].astype(BF16)

    gs = pltpu.PrefetchScalarGridSpec(
        num_scalar_prefetch=1, grid=(R // tr,),
        in_specs=[pl.BlockSpec((None, tr, C), lambda i, d: (0, i, 0))],
        out_specs=pl.BlockSpec((None, tr, C), lambda i, d: (d[0], i, 0)))
    return _pcall(body, name=name, grid_spec=gs, out_shape=S((N_DEV, R, C), BF16),
                  compiler_params=_cparams(("parallel",)))(dev_idx, w)


def _own_block(v, dev):
    return lax.dynamic_update_slice(lax.empty((N_DEV,) + v.shape, v.dtype), v[None], (dev,) + (0,) * v.ndim)


_ANY = pl.BlockSpec(memory_space=pl.ANY)


def _position():
    return lax.axis_index("x"), lax.axis_index("y"), lax.axis_index("c")


def _allgather(name, bufs):
    n = len(bufs)

    def body(*refs):
        outs = refs[n:2 * n]
        send, recv = refs[2 * n:]
        x, y, c = _position()
        me, sib = (x, y, c), (x, y, 1 - c)
        chips = [(1 - x, y), (x, 1 - y), (1 - x, 1 - y)]

        def copy(a, k, block, to):
            bx, by, bc = block
            blk = outs[a].at[4 * bx + 2 * by + bc]
            return pltpu.make_async_remote_copy(
                src_ref=blk, dst_ref=blk, send_sem=send.at[a, k], recv_sem=recv.at[a, k],
                device_id=to, device_id_type=MESH)

        first = []
        for a in range(n):
            first.append(copy(a, 0, me, sib))
            first += [copy(a, 1 + j, me, (*chip, c)) for j, chip in enumerate(chips)]
        for cp in first:
            cp.start()
        passed = []
        for j, chip in enumerate(chips):
            for a in range(n):
                copy(a, 1 + j, (*chip, c), me).wait_recv()
                cp = copy(a, 4 + j, (*chip, c), sib)
                cp.start()
                passed.append(cp)
        for a in range(n):
            copy(a, 0, sib, me).wait_recv()
        for j, chip in enumerate(chips):
            for a in range(n):
                copy(a, 4 + j, (*chip, 1 - c), me).wait_recv()
        for cp in first + passed:
            cp.wait_send()

    return _pcall(body, name=name, in_specs=[_ANY] * n, out_specs=[_ANY] * n,
                  out_shape=[S(b.shape, b.dtype) for b in bufs], input_output_aliases={a: a for a in range(n)},
                  scratch_shapes=[pltpu.SemaphoreType.DMA((n, 7)), pltpu.SemaphoreType.DMA((n, 7))])(*bufs)


def _rs_sibling(name, fulls):
    n = len(fulls)

    def body(*refs):
        ins, outs = refs[:n], refs[n:2 * n]
        send, recv = refs[2 * n:]
        x, y, c = _position()
        copies = []
        for a in range(n):
            for j in range(4):
                copies.append(pltpu.make_async_remote_copy(
                    src_ref=ins[a].at[2 * j + (1 - c)], dst_ref=outs[a].at[j], send_sem=send.at[a, j],
                    recv_sem=recv.at[a, j], device_id=(x, y, 1 - c), device_id_type=MESH))
        for cp in copies:
            cp.start()
        for cp in copies:
            cp.wait()

    return _pcall(body, name=name, in_specs=[_ANY] * n, out_specs=[_ANY] * n,
                  out_shape=[S((4,) + f.shape[1:], f.dtype) for f in fulls],
                  scratch_shapes=[pltpu.SemaphoreType.DMA((n, 4)), pltpu.SemaphoreType.DMA((n, 4))])(*fulls)


_HBM = pl.BlockSpec(memory_space=pltpu.HBM)
_SEM = pl.BlockSpec(memory_space=pltpu.SEMAPHORE)
_EFFECT = pltpu.SideEffectType.DATAFLOW_SIDE_EFFECTING


def _remote_copies(copies_fn, srcs, lands, send, recv):
    x, y, c = _position()
    return [pltpu.make_async_remote_copy(src_ref=s, dst_ref=d, send_sem=send[i], recv_sem=recv[i], device_id=to,
                                         device_id_type=MESH)
            for i, (s, d, to) in enumerate(copies_fn(x, y, c, srcs, lands))]


def _split_start(name, srcs, lands, copies_fn, nc):
    n, nl = len(srcs), len(lands)

    def body(*refs):
        src_refs, land_refs = refs[:n], refs[n:n + nl]
        outs = refs[n + nl:]
        for cp in _remote_copies(copies_fn, src_refs, land_refs, outs[:nc], outs[nc:2 * nc]):
            cp.start()
        outs[-1][...] = jnp.zeros_like(outs[-1])

    hbm = lambda a: pltpu.with_memory_space_constraint(a, pltpu.HBM)
    res = _pcall(
        body, name=name, in_specs=[_HBM] * (n + nl),
        out_specs=[_SEM] * (2 * nc) + [_HBM] * (n + nl) + [pl.BlockSpec(memory_space=pltpu.VMEM)],
        out_shape=[pltpu.SemaphoreType.DMA(())] * (2 * nc) + [pltpu.HBM(s.shape, s.dtype) for s in srcs]
        + [pltpu.HBM(l.shape, l.dtype) for l in lands] + [S((SUBLANES, LANES), F32)],
        input_output_aliases={i: 2 * nc + i for i in range(n + nl)},
        compiler_params=pltpu.CompilerParams(has_side_effects=_EFFECT),
    )(*[hbm(s) for s in srcs], *[hbm(l) for l in lands])
    return res[:2 * nc], res[2 * nc:2 * nc + n], res[2 * nc + n:2 * nc + n + nl], res[-1]


def _split_wait(name, sems, srcs, lands, after, copies_fn, nc):
    n, nl = len(srcs), len(lands)

    def body(*refs):
        src_refs, land_refs = refs[:n], refs[n:n + nl]
        sem_refs = refs[n + nl:n + nl + 2 * nc]
        for cp in _remote_copies(copies_fn, src_refs, land_refs, sem_refs[:nc], sem_refs[nc:]):
            cp.wait_send()
            cp.wait_recv()

    res = _pcall(
        body, name=name, in_specs=[_HBM] * (n + nl) + [_SEM] * (2 * nc) + [_ANY],
        out_specs=[_HBM] * (n + nl), out_shape=[pltpu.HBM(a.shape, a.dtype) for a in list(srcs) + list(lands)],
        input_output_aliases={i: i for i in range(n + nl)},
        compiler_params=pltpu.CompilerParams(has_side_effects=_EFFECT),
    )(*srcs, *lands, *sems, after)
    return res[:n], res[n:]


def _other_chips(x, y):
    return [(1 - x, y), (x, 1 - y), (1 - x, 1 - y)]


def _ag_copies(x, y, c, srcs, lands):
    out = []
    for land in lands:
        blk = land.at[4 * x + 2 * y + c]
        out.append((blk, blk, (x, y, 1 - c)))
        out += [(blk, blk, (px, py, c)) for px, py in _other_chips(x, y)]
    return out


def _rs_copies(x, y, c, srcs, lands):
    return [(s.at[2 * px + py], land.at[j], (px, py, c))
            for s, land in zip(srcs, lands) for j, (px, py) in enumerate(_other_chips(x, y))]


def _sib_copies(x, y, c, srcs, lands):
    return [(s.at[2 * j + (1 - c)], land.at[j], (x, y, 1 - c)) for s, land in zip(srcs, lands) for j in range(4)]


def _ag_finish(name, lands):
    n = len(lands)

    def body(*refs):
        outs = refs[n:2 * n]
        send, recv = refs[2 * n:]
        x, y, c = _position()

        def swap(a, j, px, py, pc):
            blk = outs[a].at[4 * px + 2 * py + pc]
            return pltpu.make_async_remote_copy(src_ref=blk, dst_ref=blk, send_sem=send.at[a, j], recv_sem=recv.at[a, j],
                                                device_id=(x, y, 1 - c), device_id_type=MESH)

        chips = _other_chips(x, y)
        sends = [swap(a, j, px, py, c) for a in range(n) for j, (px, py) in enumerate(chips)]
        for cp in sends:
            cp.start()
        for a in range(n):
            for j, (px, py) in enumerate(chips):
                swap(a, j, px, py, 1 - c).wait_recv()
        for cp in sends:
            cp.wait_send()

    return _pcall(body, name=name, in_specs=[_ANY] * n, out_specs=[_ANY] * n,
                  out_shape=[S(l.shape, l.dtype) for l in lands], input_output_aliases={a: a for a in range(n)},
                  scratch_shapes=[pltpu.SemaphoreType.DMA((n, 3)), pltpu.SemaphoreType.DMA((n, 3))])(*lands)


def _pad_lanes(v):
    return jnp.pad(v, ((0, 0), (0, LANES - v.shape[1])))


def _flat_rows(pieces):
    flat = jnp.concatenate([p.reshape(-1) for p in pieces])
    rows = -(-flat.shape[0] // (SMALL_W * SUBLANES)) * SUBLANES
    return jnp.pad(flat, (0, rows * SMALL_W - flat.shape[0])).reshape(rows, SMALL_W)


def _unflat(buf, shapes):
    flat = buf.reshape(-1)
    out, off = [], 0
    for sh in shapes:
        n = 1
        for d in sh:
            n *= d
        out.append(flat[off:off + n].reshape(sh))
        off += n
    return out


def kernel(x, pre_mix_norm, w_in, ssd_conv_w, ssd_conv_b, ssd_dt_bias, ssd_a_log, ssd_d, ssd_norm, lru_conv_w, lru_conv_b, lru_w_a, lru_b_a, lru_w_x, lru_b_x, lru_lambda, lru_norm, w_out, post_mix_norm, pre_mlp_norm, w_mlp_in, w_mlp_out, post_mlp_norm, loss_target, m_pre_mix_norm, m_w_in, m_ssd_conv_w, m_ssd_conv_b, m_ssd_dt_bias, m_ssd_a_log, m_ssd_d, m_ssd_norm, m_lru_conv_w, m_lru_conv_b, m_lru_w_a, m_lru_b_a, m_lru_w_x, m_lru_b_x, m_lru_lambda, m_lru_norm, m_w_out, m_post_mix_norm, m_pre_mlp_norm, m_w_mlp_in, m_w_mlp_out, m_post_mlp_norm, v_pre_mix_norm, v_w_in, v_ssd_conv_w, v_ssd_conv_b, v_ssd_dt_bias, v_ssd_a_log, v_ssd_d, v_ssd_norm, v_lru_conv_w, v_lru_conv_b, v_lru_w_a, v_lru_b_a, v_lru_w_x, v_lru_b_x, v_lru_lambda, v_lru_norm, v_w_out, v_post_mix_norm, v_pre_mlp_norm, v_w_mlp_in, v_w_mlp_out, v_post_mlp_norm):
    names = ['pre_mix_norm', 'w_in', 'ssd_conv_w', 'ssd_conv_b', 'ssd_dt_bias', 'ssd_a_log', 'ssd_d', 'ssd_norm',
             'lru_conv_w', 'lru_conv_b', 'lru_w_a', 'lru_b_a', 'lru_w_x', 'lru_b_x', 'lru_lambda', 'lru_norm',
             'w_out', 'post_mix_norm', 'pre_mlp_norm', 'w_mlp_in', 'w_mlp_out', 'post_mlp_norm']
    loc = locals()
    W = {n: loc[n] for n in names}
    Mo = {n: loc["m_" + n] for n in names}
    Vo = {n: loc["v_" + n] for n in names}
    big = ['w_in', 'w_out', 'w_mlp_in', 'w_mlp_out']

    px, py, pc = _position()
    dev = 4 * px + 2 * py + pc
    dev_idx = jnp.reshape(dev, (1,)).astype(jnp.int32)
    c_idx = jnp.reshape(pc, (1,)).astype(jnp.int32)
    chip_idx = jnp.reshape(2 * px + py, (1,)).astype(jnp.int32)

    _, T, D = x.shape
    x2 = x.reshape(T, D)
    tgt = loss_target.reshape(T, D)
    n_heads = ssd_dt_bias.shape[1]
    XBC = ssd_conv_b.shape[1]
    GN = XBC // 4
    DS = XBC - 2 * GN
    DL = lru_norm.shape[1]
    DFF = w_mlp_in.shape[2] * N_DEV
    DIN = w_in.shape[2] * N_DEV
    NP = XBC + DS + 2 * DL + LANES
    assert DS % GN == 0 and XBC % DS == 0 and DS == DL and n_heads <= LANES
    cb_z, cb_gate, cb_xl, cb_dt = XBC // DS, XBC // DS + 1, XBC // DS + 2, (XBC + DS + 2 * DL) // LANES
    tr = min(256, T // 2)
    nt = T // tr
    Q = min(128, T // 2)

    sh = {n: _cast_bf16("cast_" + n, W[n], dev_idx) for n in big}
    g_in, g_cs, g_cl = _allgather(
        "allgather_w_in", [sh['w_in'], _own_block(ssd_conv_w[0], dev), _own_block(lru_conv_w[0], dev)])
    later = big[1:]
    ag_sems, ag_srcs, ag_lands, ag_token = _split_start(
        "allgather_later_start", [], [sh[n] for n in later], _ag_copies, 4 * len(later))
    conv_s = jnp.transpose(g_cs, (1, 0, 2)).reshape(CONV_WIDTH, XBC)
    conv_l = jnp.transpose(g_cl, (1, 0, 2)).reshape(CONV_WIDTH, DL)
    wb = DIN // N_DEV
    o_z, o_xbc, o_dt, o_gate, o_xl = 0, DS, DS + XBC, DS + XBC + n_heads, DS + XBC + n_heads + DL
    segs = [(o_xbc, o_xbc + XBC, 0), (o_z, o_z + DS, XBC), (o_gate, o_gate + DL, XBC + DS),
            (o_xl, o_xl + DL, XBC + DS + DL), (o_dt, o_dt + n_heads, NP - LANES)]

    def ref_cols(lo, hi):
        out = []
        while lo < hi:
            k = lo // wb
            e = min(hi, (k + 1) * wb)
            out.append(g_in[k, :, lo - k * wb:e - k * wb])
            lo = e
        return out

    def my_cols(g, lo, hi):
        out = []
        for a, b, m in sorted(segs):
            s, e = max(lo, a), min(hi, b)
            if s < e:
                out.append(g[:, m + s - a:m + e - a])
        return out

    wp = jnp.concatenate([p for a, b, _ in segs for p in ref_cols(a, b)]
                         + [jnp.zeros((D, LANES - n_heads), BF16)], axis=1)
    dt_bias = _pad_lanes(ssd_dt_bias)
    a_log = _pad_lanes(ssd_a_log)
    d_skip = _pad_lanes(ssd_d)
    wa_b, wx_b = lru_w_a[0].astype(BF16), lru_w_x[0].astype(BF16)
    b_a, b_x = lru_b_a.reshape(1, DL), lru_b_x.reshape(1, DL)

    def f_norm_in(first, last, xv, g):
        return (_rms(xv, g),), ()
    (h,) = _rows_call("norm_in", f_norm_in, nt, [x2, pre_mix_norm], [_rt(tr, D), _full(pre_mix_norm)],
                      [((T, D), BF16, _rt(tr, D))], [], 'tf', deps=[ag_token])

    (proj,) = _mm("proj", h, wp, outs=((BF16, None),))
    (dt_raw,) = _mm("proj_dt", h, wp[:, NP - LANES:])

    cwx = min(1024, XBC)

    def f_ssd_pre(first, last, xbc, halo, w, b):
        pre = _conv_pre(xbc, jnp.where(first, 0.0, halo), w, b)
        return (pre * jax.nn.sigmoid(pre),), ()
    (xbc_act,) = _rows_call(
        "ssd_pre", f_ssd_pre, nt, [proj, proj, conv_s, ssd_conv_b],
        [_rt(tr, XBC), _halo_prev(tr, XBC, rows=PACKED_ROWS), _full(conv_s), _full(ssd_conv_b)],
        [((T, XBC), F32, _rt(tr, XBC))], [], ['t', 'p0', 'f', 'f'], cw=cwx)

    def f_ssd_dt(first, last, dtr, dtb):
        return (jax.nn.softplus(dtr + dtb),), ()
    (dt,) = _rows_call("ssd_dt", f_ssd_dt, nt, [dt_raw, dt_bias], [_rt(tr, LANES), _full(dt_bias)],
                       [((T, LANES), F32, _rt(tr, LANES))], [], 'tf')

    y_ssd, h_prev = _ssd_fwd(xbc_act, dt, a_log, d_skip, n_heads, Q)

    gw = DS // SSD_GROUPS

    def ssd_post(y, z, g):
        yz = y * jax.nn.silu(z)
        parts = []
        for k in range(SSD_GROUPS):
            yk = yz[:, k * gw:(k + 1) * gw]
            parts.append(yk * lax.rsqrt(jnp.mean(yk * yk, axis=-1, keepdims=True) + EPS))
        return jnp.concatenate(parts, axis=-1) * g

    def f_ssd_post(first, last, y, z, g):
        return (ssd_post(y, z, g),), ()
    (mixcat,) = _rows_call("ssd_post", f_ssd_post, nt, [y_ssd, proj, ssd_norm],
                           [_rt(tr, DS), _rt(tr, DS, cb_z), _full(ssd_norm)], [((T, DS + DL), BF16, _rt(tr, DS))], [],
                           'ttf')

    def f_lru_pre(first, last, xv, halo, w, b):
        return (_conv_pre(xv, jnp.where(first, 0.0, halo), w, b),), ()
    (xl,) = _rows_call("lru_pre", f_lru_pre, nt, [proj, proj, conv_l, lru_conv_b],
                       [_rt(tr, DL, cb_xl), _halo_prev(tr, DL, cb_xl, rows=PACKED_ROWS), _full(conv_l),
                        _full(lru_conv_b)],
                       [((T, DL), F32, _rt(tr, DL))], [], ['t', 'p0', 'f', 'f'])

    a_lru, u_lru = _lru_gates_fwd(xl, wa_b, b_a, wx_b, b_x, lru_lambda, tr)
    h_lru = _lru_scan_fwd(a_lru, u_lru, tr)

    def lru_post(hv, gate, g):
        return _rms(hv * jax.nn.gelu(gate), g)

    def f_lru_post(first, last, hv, gate, g):
        return (lru_post(hv, gate, g),), ()
    cb_l = DS // DL
    (mixcat,) = _rows_call("lru_post", f_lru_post, nt, [h_lru, proj, lru_norm],
                           [_rt(tr, DL), _rt(tr, DL, cb_gate), _full(lru_norm)],
                           [((T, DS + DL), BF16, _rt(tr, DL, cb_l))], [], 'ttf', into=mixcat)

    ag_srcs, ag_lands = _split_wait("allgather_later_wait", ag_sems, ag_srcs, ag_lands, mixcat, _ag_copies,
                                    4 * len(later))
    g_out, g_mi, g_mo = _ag_finish("allgather_later_finish", ag_lands)
    w_out_f = g_out.reshape(DS + DL, D)
    w_mi_f = jnp.transpose(g_mi, (1, 0, 2)).reshape(D, DFF)
    w_mo_f = g_mo.reshape(DFF, D)
    (mix,) = _mm("mix", mixcat, w_out_f, outs=((BF16, None),))

    def f_post_mix(first, last, xv, mx, gpm, gpl):
        x1 = xv + _rms(mx, gpm)
        return (x1, _rms(x1, gpl)), ()
    x1, hn = _rows_call("post_mix", f_post_mix, nt, [x2, mix, post_mix_norm, pre_mlp_norm],
                        [_rt(tr, D), _rt(tr, D), _full(post_mix_norm), _full(pre_mlp_norm)],
                        [((T, D), F32, _rt(tr, D)), ((T, D), BF16, _rt(tr, D))], [], 'ttff')

    hm, act = _mm("mlp_in", hn, w_mi_f,
                  outs=((BF16, None), (BF16, lambda r, e: jnp.square(jnp.maximum(r, 0.0)))))
    (hm2,) = _mm("mlp_out", act, w_mo_f, outs=((BF16, None),))

    def f_final(first, last, x1v, hm2v, g, tg):
        def fwd(hv, gv):
            return x1v + _rms(hv, gv)
        x2v, vjp = jax.vjp(fwd, hm2v, g)
        err = x2v - tg
        dx2 = err * (1.0 / D)
        dh, dg = vjp(dx2)
        loss = jnp.full((1, LANES), 0.5 / D, F32) * jnp.sum(err * err)
        return (dx2, dh), (dg, loss)
    dx1a, dhm2, g_post_mlp, loss_part = _rows_call(
        "loss_head", f_final, nt, [x1, hm2, post_mlp_norm, tgt],
        [_rt(tr, D), _rt(tr, D), _full(post_mlp_norm), _rt(tr, D)],
        [((T, D), F32, _rt(tr, D)), ((T, D), BF16, _rt(tr, D))], [(1, D), (1, LANES)], 'ttft')

    def rs_chips_start(n, full, from_sib):
        pair = _pair_sum("pair_sum_" + n, full, from_sib, c_idx)
        sems, srcs, lands, token = _split_start("rs_start_" + n, [pair], [lax.empty((3,) + pair.shape[1:], BF16)],
                                                _rs_copies, 3)
        return (sems, srcs, lands), token

    def rs_begin(n, full):
        (from_sib,) = _rs_sibling("rs_sibling_" + n, [full])
        return rs_chips_start(n, full, from_sib)

    def rs_sibling_start(n, full):
        sems, srcs, lands, token = _split_start("rs_sibling_start_" + n, [full],
                                                [lax.empty((4,) + full.shape[1:], BF16)], _sib_copies, 4)
        return (sems, srcs, lands), token

    def rs_begin_late(n, state, after):
        (full,), (from_sib,) = _split_wait("rs_sibling_wait_" + n, *state, after, _sib_copies, 4)
        return rs_chips_start(n, full, from_sib)

    def rs_end(n, state, after):
        (pair,), (recv,) = _split_wait("rs_wait_" + n, *state, after, _rs_copies, 3)
        return pair, recv

    (gw_mo,) = _mm("dw_mlp_out", act, dhm2, ta=True, outs=((BF16, None),))
    sib_mo, tok = rs_sibling_start('w_mlp_out', gw_mo.reshape(N_DEV, DFF // N_DEV, D))
    (dhm,) = _mm("d_mlp_act", dhm2, w_mo_f, tb=True, extra=hm,
                 outs=((BF16, lambda r, e: r * (2.0 * jnp.maximum(e.astype(F32), 0.0))),), deps=[tok])
    rs_mo, tok = rs_begin_late('w_mlp_out', sib_mo, dhm)
    (gw_mi,) = _mm("dw_mlp_in", hn, dhm, ta=True, outs=((BF16, None),), out_blocks=N_DEV, deps=[tok])
    sib_mi, tok = rs_sibling_start('w_mlp_in', gw_mi)
    (dhn,) = _mm("d_mlp_in", dhm, w_mi_f, tb=True, outs=((BF16, None),), deps=[tok])
    rs_mi, tok_mi = rs_begin_late('w_mlp_in', sib_mi, dhn)

    def f_post_mix_bwd(first, last, x1v, mx, gpm, gpl, dhnv, dxa):
        _, vjp1 = jax.vjp(_rms, x1v, gpl)
        dx1, dgpl = vjp1(dhnv)
        dx1 = dx1 + dxa
        _, vjp2 = jax.vjp(_rms, mx, gpm)
        dmx, dgpm = vjp2(dx1)
        return (dx1, dmx), (dgpl, dgpm)
    dx1, dmix, g_pre_mlp, g_post_mix = _rows_call(
        "post_mix_bwd", f_post_mix_bwd, nt, [x1, mix, post_mix_norm, pre_mlp_norm, dhn, dx1a],
        [_rt(tr, D), _rt(tr, D), _full(post_mix_norm), _full(pre_mlp_norm), _rt(tr, D), _rt(tr, D)],
        [((T, D), F32, _rt(tr, D)), ((T, D), BF16, _rt(tr, D))], [(1, D), (1, D)], 'ttfftt', deps=[tok_mi])

    (gw_out,) = _mm("dw_out", mixcat, dmix, ta=True, outs=((BF16, None),))
    sib_out, tok = rs_sibling_start('w_out', gw_out.reshape(N_DEV, -1, D))
    (dmixcat,) = _mm("d_mix", dmix, w_out_f, tb=True, outs=((BF16, None),), deps=[tok])
    rs_out, tok_out = rs_begin_late('w_out', sib_out, dmixcat)

    def f_lru_post_bwd(first, last, hv, gate, g, dy):
        _, vjp = jax.vjp(lru_post, hv, gate, g)
        dh_, dgate, dg = vjp(dy)
        return (dgate, dh_), (dg,)
    dproj, dh_lru, g_lru_norm = _rows_call(
        "lru_post_bwd", f_lru_post_bwd, nt, [h_lru, proj, lru_norm, dmixcat],
        [_rt(tr, DL), _rt(tr, DL, cb_gate), _full(lru_norm), _rt(tr, DL, cb_l)],
        [((T, NP), BF16, _rt(tr, DL, cb_gate)), ((T, DL), F32, _rt(tr, DL))], [(1, DL)], 'ttft', deps=[tok_out])

    du_lru, da_lru = _lru_scan_bwd(a_lru, h_lru, dh_lru, tr)
    dxl, g_wa, g_ba, g_wx, g_bx, g_lam = _lru_gates_bwd(xl, wa_b, b_a, wx_b, b_x, lru_lambda, da_lru, du_lru, tr)

    conv_bwd_kinds = ['t', 'p0', 'n0', 't', 'n3', 'f', 'f']

    def f_lru_pre_bwd(first, last, xv, hp, xn, d, dn, w, b):
        dx, dw8, db = _conv_bwd_tile(first, last, xv, hp, xn, d, dn, w, b, silu=False)
        return (dx,), (dw8, db)
    dproj, g_convl8, g_convl_b = _rows_call(
        "lru_pre_bwd", f_lru_pre_bwd, nt, [proj, proj, proj, dxl, dxl, conv_l, lru_conv_b],
        [_rt(tr, DL, cb_xl), _halo_prev(tr, DL, cb_xl, rows=PACKED_ROWS),
         _halo_next(tr, DL, nt, cb_xl, rows=PACKED_ROWS), _rt(tr, DL),
         _halo_next(tr, DL, nt), _full(conv_l), _full(lru_conv_b)],
        [((T, NP), BF16, _rt(tr, DL, cb_xl))], [(SUBLANES, DL), (1, DL)], conv_bwd_kinds, into=dproj)

    def f_ssd_post_bwd(first, last, y, z, g, dy):
        _, vjp = jax.vjp(ssd_post, y, z, g)
        dy_, dz, dg = vjp(dy)
        return (dz, dy_), (dg,)
    dproj, dy_ssd, g_ssd_norm = _rows_call(
        "ssd_post_bwd", f_ssd_post_bwd, nt, [y_ssd, proj, ssd_norm, dmixcat],
        [_rt(tr, DS), _rt(tr, DS, cb_z), _full(ssd_norm), _rt(tr, DS, 0)],
        [((T, NP), BF16, _rt(tr, DS, cb_z)), ((T, DS), F32, _rt(tr, DS))], [(1, DS)], 'ttft', into=dproj)

    dxbc_act, ddt, g_alog, g_dskip = _ssd_bwd(xbc_act, dt, a_log, d_skip, h_prev, dy_ssd, n_heads, Q)

    def f_ssd_pre_bwd(first, last, xv, hp, xn, d, dn, w, b):
        dx, dw8, db = _conv_bwd_tile(first, last, xv, hp, xn, d, dn, w, b, silu=True)
        return (dx,), (dw8, db)
    dproj, g_convs8, g_convs_b = _rows_call(
        "ssd_pre_bwd", f_ssd_pre_bwd, nt, [proj, proj, proj, dxbc_act, dxbc_act, conv_s, ssd_conv_b],
        [_rt(tr, XBC), _halo_prev(tr, XBC, rows=PACKED_ROWS), _halo_next(tr, XBC, nt, rows=PACKED_ROWS),
         _rt(tr, XBC), _halo_next(tr, XBC, nt),
         _full(conv_s), _full(ssd_conv_b)],
        [((T, NP), BF16, _rt(tr, XBC))], [(SUBLANES, XBC), (1, XBC)], conv_bwd_kinds, into=dproj, cw=cwx)

    def f_ssd_dt_bwd(first, last, ddtv, dtr, dtb):
        ddtr = ddtv * jax.nn.sigmoid(dtr + dtb)
        return (ddtr,), (_colsum(ddtr),)
    dproj, g_dtb = _rows_call(
        "ssd_dt_bwd", f_ssd_dt_bwd, nt, [ddt, dt_raw, dt_bias],
        [_rt(tr, LANES), _rt(tr, LANES), _full(dt_bias)],
        [((T, NP), BF16, _rt(tr, LANES, cb_dt))], [(1, LANES)], 'ttf', into=dproj)
    small = {
        'ssd_conv_w': g_convs8[:CONV_WIDTH], 'ssd_conv_b': g_convs_b,
        'ssd_dt_bias': g_dtb[:, :n_heads], 'ssd_a_log': g_alog[:, :n_heads], 'ssd_d': g_dskip[:, :n_heads],
        'ssd_norm': g_ssd_norm, 'lru_conv_w': g_convl8[:CONV_WIDTH], 'lru_conv_b': g_convl_b,
        'lru_w_a': g_wa, 'lru_b_a': g_ba, 'lru_w_x': g_wx, 'lru_b_x': g_bx, 'lru_lambda': g_lam,
        'lru_norm': g_lru_norm, 'post_mix_norm': g_post_mix, 'pre_mlp_norm': g_pre_mlp,
        'post_mlp_norm': g_post_mlp, 'loss': loss_part[:, :1],
    }
    wide = ['lru_w_a', 'lru_w_x']
    narrow = [n for n in small if n not in wide]
    lb = lru_w_a.shape[-1]
    s_srcs = [_flat_rows([small[n] for n in narrow]), g_wa.reshape(-1, lb), g_wx.reshape(-1, lb)]
    s_sems, s_srcs, s_lands, tok = _split_start(
        "small_grads_start", [], [_own_block(a, dev) for a in s_srcs], _ag_copies, 4 * len(s_srcs))

    (gwp,) = _mm("dw_proj", h, dproj, ta=True, outs=((BF16, None),), deps=[tok])
    rs_in, tok = rs_begin(
        'w_in', jnp.stack([jnp.concatenate(my_cols(gwp, k * wb, (k + 1) * wb), axis=1) for k in range(N_DEV)]))
    (dh,) = _mm("d_proj", dproj, wp, tb=True, outs=((BF16, None),), deps=[tok])

    def f_norm_in_bwd(first, last, xv, g, dhv, dxa):
        _, vjp = jax.vjp(_rms, xv, g)
        dx, dg = vjp(dhv)
        return (dx + dxa,), (dg,)
    grad_x, g_pre_mix = _rows_call(
        "norm_in_bwd", f_norm_in_bwd, nt, [x2, pre_mix_norm, dh, dx1],
        [_rt(tr, D), _full(pre_mix_norm), _rt(tr, D), _rt(tr, D)], [((T, D), F32, _rt(tr, D))], [(1, D)], 'tftt')

    (g_pm8,) = _allgather("allgather_pre_mix_grad", [_own_block(g_pre_mix, dev)])
    _, s_lands = _split_wait("small_grads_wait", s_sems, s_srcs, s_lands, g_pm8, _ag_copies, 4 * len(s_lands))
    g_narrow, g_wa8, g_wx8 = _ag_finish("small_grads_finish", s_lands)
    summed = dict(zip(narrow, _unflat(_sum8("sum_small_grads", g_narrow), [small[n].shape for n in narrow])))
    summed['pre_mix_norm'] = _sum8("sum_pre_mix_grad", g_pm8)
    summed['lru_w_a'] = _sum8("sum_lru_w_a_grads", g_wa8)
    summed['lru_w_x'] = _sum8("sum_lru_w_x_grads", g_wx8)
    loss = summed.pop('loss').reshape(())
    for n, full_w in (('ssd_conv_w', XBC), ('lru_conv_w', DL)):
        wdt = full_w // N_DEV
        summed[n] = lax.dynamic_slice_in_dim(summed[n], dev * wdt, wdt, axis=1)
    small_params = [n for n in names if n not in big]
    as2d = lambda a: a.reshape(-1, a.shape[-1])
    res = _adamw_small([as2d(W[n]) for n in small_params],
                       [summed[n].reshape(as2d(W[n]).shape) for n in small_params],
                       [as2d(Mo[n]) for n in small_params], [as2d(Vo[n]) for n in small_params])
    grads = {n: summed[n].reshape(W[n].shape) for n in small_params}
    delta, new_m, new_v = ({n: r.reshape(W[n].shape) for n, r in zip(small_params, rs)} for rs in res)

    for n, state in (('w_mlp_out', rs_mo), ('w_mlp_in', rs_mi), ('w_out', rs_out), ('w_in', rs_in)):
        p, r = rs_end(n, state, g_pm8)
        grads[n], delta[n], new_m[n], new_v[n] = _adamw_big("adamw_" + n, W[n], Mo[n], Vo[n], p, r, chip_idx)

    return (loss, grad_x.reshape(x.shape), *[grads[n] for n in names], *[delta[n] for n in names],
            *[new_m[n] for n in names], *[new_v[n] for n in names])
```

```python
import functools

import jax
import jax.numpy as jnp
from jax import lax
from jax.experimental import pallas as pl
from jax.experimental.pallas import tpu as pltpu

F32, BF16 = jnp.float32, jnp.bfloat16
S = jax.ShapeDtypeStruct
MESH = pl.DeviceIdType.MESH

SSD_GROUPS = 8
LRU_C = 8.0
EPS = 1e-6
CONV_WIDTH = 4
ADAM_LR, ADAM_B1, ADAM_B2, ADAM_EPS, ADAM_WD, ADAM_STEP = 0.001, 0.9, 0.999, 1e-08, 0.01, 10

LANES = 128
SUBLANES = 8
VMEM_LIMIT = 56 * 1024 * 1024
N_DEV = 8
SMALL_W = 512
HI = lax.Precision.HIGHEST


def _pcall(body, **kw):
    return pl.pallas_call(body, **kw)


def _cparams(sem=None, **kw):
    return pltpu.CompilerParams(dimension_semantics=sem, vmem_limit_bytes=VMEM_LIMIT, **kw)


def _pick(n, cands):
    for c in cands:
        if c <= n and n % c == 0:
            return c
    return n


def _rt(tr, w, cb=0, n=None):
    if n is None:
        return pl.BlockSpec((tr, w), lambda i: (i, cb))
    return pl.BlockSpec((tr, w), lambda i: (n - 1 - i, cb))


PACKED_ROWS = 16


def _halo_prev(tr, w, cb=0, n=None, rows=SUBLANES):
    k = tr // rows
    if n is None:
        return pl.BlockSpec((rows, w), lambda i: (jnp.maximum(i * k - 1, 0), cb))
    return pl.BlockSpec((rows, w), lambda i: (jnp.maximum((n - 1 - i) * k - 1, 0), cb))


def _halo_next(tr, w, nt, cb=0, n=None, rows=SUBLANES):
    k = tr // rows
    last = nt * k - 1
    if n is None:
        return pl.BlockSpec((rows, w), lambda i: (jnp.minimum((i + 1) * k, last), cb))
    return pl.BlockSpec((rows, w), lambda i: (jnp.minimum((n - i) * k, last), cb))


def _full(a):
    nd = a.ndim
    return pl.BlockSpec(a.shape, lambda i: (0,) * nd)


def _rows_call(name, fn, n_tiles, arrays, in_specs, out_tiled, out_acc, kinds, into=None, deps=(), cw=None):
    n_in, n_t = len(arrays), len(out_tiled)
    n_skip = len(deps) + (0 if into is None else 1)
    width = in_specs[kinds.index('t')].block_shape[1]
    cols = [(0, width)] if cw is None else [(c, cw) for c in range(0, width, cw)]

    def body(*refs):
        i = pl.program_id(0)
        ins = refs[:n_in]
        outs = refs[n_in + n_skip:n_in + n_skip + n_t]
        accs = refs[n_in + n_skip + n_t:]
        if accs:
            @pl.when(i == 0)
            def _():
                for r in accs:
                    r[...] = jnp.zeros_like(r)

        def lanes(ref, rows, c0, w):
            return ref[rows, c0:c0 + w] if ref.shape[-1] == width else ref[rows, :]

        def load(k, c0, w):
            v = lanes(ins[k], slice(None), c0, w).astype(F32)
            if kinds[k][0] in 'pn' and v.shape[0] == PACKED_ROWS:
                v = v[SUBLANES:] if kinds[k][0] == 'p' else v[:SUBLANES]
            return v

        for c0, w in cols:
            touts, aouts = fn(i == 0, i == n_tiles - 1, *[load(k, c0, w) for k in range(n_in)])
            for r, v in zip(outs, touts):
                if r.shape[-1] == width:
                    r[:, c0:c0 + w] = v.astype(r.dtype)
                else:
                    r[...] = v.astype(r.dtype)
            for r, v in zip(accs, aouts):
                if r.shape[-1] == width:
                    r[:, c0:c0 + w] += v
                else:
                    r[...] += v

    out_shape = [S(sh, dt) for sh, dt, _ in out_tiled] + [S(sh, F32) for sh in out_acc]
    out_specs = [sp for _, _, sp in out_tiled]
    for sh in out_acc:
        out_specs.append(pl.BlockSpec(sh, lambda i, nd=len(sh): (0,) * nd))
    in_specs = list(in_specs) + [_ANY] * len(deps)
    if into is None:
        return _pcall(body, name=name, grid=(n_tiles,), in_specs=in_specs, out_specs=out_specs,
                      out_shape=out_shape, compiler_params=_cparams(("arbitrary",)))(*arrays, *deps)
    return _pcall(body, name=name, grid=(n_tiles,), in_specs=in_specs + [_ANY], out_specs=out_specs,
                  out_shape=out_shape, input_output_aliases={n_in + len(deps): 0},
                  compiler_params=_cparams(("arbitrary",)))(*arrays, *deps, into)


def _rms(x, g):
    return x * lax.rsqrt(jnp.mean(x * x, axis=-1, keepdims=True) + EPS) * g


def _colsum(v):
    return jnp.sum(v, axis=0, keepdims=True)


_TILES = (1152, 1024, 896, 768, 640, 512, 384, 256, 128)
_K_TILES = (4096, 3456, 3072, 2688, 2048, 1536, 1344, 1152, 1024, 896, 768, 640, 512, 384, 256, 128)


def _mm(name, a, b, *, ta=False, tb=False, outs=((F32, None),), extra=None, out_blocks=None, tm=None, tn=None, tk=None,
        deps=(), a_cols=None):
    M, K = (a.shape[1], a.shape[0]) if ta else a.shape
    if a_cols is not None:
        assert not ta
        K = a_cols[1]
    b3 = b.ndim == 3
    if b3:
        nb_b, brows, bcols = b.shape
        N = brows if tb else nb_b * bcols
    else:
        N = b.shape[0] if tb else b.shape[1]
    n_lim = N if out_blocks is None else N // out_blocks
    if b3 and not tb:
        n_lim = min(n_lim, bcols)
    tm = tm or _pick(M, _TILES[1:])
    tn = tn or _pick(n_lim, _TILES)
    tk = tk or _pick(bcols if (b3 and tb) else K, _K_TILES)
    nk = K // tk
    assert M % tm == 0 and N % tn == 0 and K % tk == 0
    dn = (((0 if ta else 1,), (1 if tb else 0,)), ((), ()))
    n_extra = 0 if extra is None else 1
    n_out = len(outs)

    def body(*refs):
        a_ref, b_ref = refs[0], refs[1]
        e_ref = refs[2] if n_extra else None
        o_refs = refs[2 + n_extra + len(deps):2 + n_extra + len(deps) + n_out]

        def finish(r):
            e = e_ref[...] if n_extra else None
            for o, (_, f) in zip(o_refs, outs):
                o[...] = (r if f is None else f(r, e)).astype(o.dtype)

        part = lax.dot_general(a_ref[...], b_ref[...], dn, preferred_element_type=F32)
        if nk == 1:
            finish(part)
            return
        acc = refs[-1]
        k = pl.program_id(2)

        @pl.when(k == 0)
        def _():
            acc[...] = part

        @pl.when(jnp.logical_and(k > 0, k < nk - 1))
        def _():
            acc[...] += part

        @pl.when(k == nk - 1)
        def _():
            finish(acc[...] + part)

    k0 = 0 if a_cols is None else a_cols[0] // tk
    a_spec = (pl.BlockSpec((tk, tm), lambda i, j, k: (k, i)) if ta
              else pl.BlockSpec((tm, tk), lambda i, j, k: (i, k + k0)))
    if not b3:
        b_spec = pl.BlockSpec((tn, tk), lambda i, j, k: (j, k)) if tb else pl.BlockSpec((tk, tn), lambda i, j, k: (k, j))
    elif tb:
        per = bcols // tk
        b_spec = pl.BlockSpec((None, tn, tk), lambda i, j, k: (k // per, j, k % per))
    else:
        per = bcols // tn
        b_spec = pl.BlockSpec((None, tk, tn), lambda i, j, k: (j // per, k, j % per))
    o_spec = pl.BlockSpec((tm, tn), lambda i, j, k: (i, j))
    if out_blocks is None:
        out_specs, out_shape = [o_spec] * n_out, [S((M, N), dt) for dt, _ in outs]
    else:
        per_o = N // out_blocks // tn
        ob_spec = pl.BlockSpec((None, tm, tn), lambda i, j, k: (j // per_o, i, j % per_o))
        out_specs, out_shape = [ob_spec] * n_out, [S((out_blocks, M, N // out_blocks), dt) for dt, _ in outs]
    in_specs = [a_spec, b_spec] + ([o_spec] if n_extra else []) + [_ANY] * len(deps)
    args = [a, b] + ([extra] if n_extra else []) + list(deps)
    return _pcall(body, name=name, grid=(M // tm, N // tn, nk), in_specs=in_specs, out_specs=out_specs,
                  out_shape=out_shape, scratch_shapes=[pltpu.VMEM((tm, tn), F32)] if nk > 1 else [],
                  compiler_params=_cparams(("parallel", "parallel", "arbitrary")))(*args)


def _shift_down(x, halo, s):
    if s == 0:
        return x
    r = pltpu.roll(x, s, 0)
    hr = pltpu.roll(halo, s, 0)
    row = lax.broadcasted_iota(jnp.int32, halo.shape, 0)
    top = jnp.where(row < s, hr, r[:SUBLANES])
    if x.shape[0] == SUBLANES:
        return top
    return jnp.concatenate([top, r[SUBLANES:]], axis=0)


def _shift_up(x, nxt, s):
    if s == 0:
        return x
    n = x.shape[0]
    r = pltpu.roll(x, n - s, 0)
    nr = pltpu.roll(nxt, SUBLANES - s, 0)
    row = lax.broadcasted_iota(jnp.int32, nxt.shape, 0)
    bot = jnp.where(row >= SUBLANES - s, nr, r[n - SUBLANES:])
    if n == SUBLANES:
        return bot
    return jnp.concatenate([r[:n - SUBLANES], bot], axis=0)


def _conv_pre(x, halo, w, b):
    acc = b + w[CONV_WIDTH - 1:CONV_WIDTH, :] * x
    for k in range(CONV_WIDTH - 1):
        acc = acc + w[k:k + 1, :] * _shift_down(x, halo, CONV_WIDTH - 1 - k)
    return acc


def _silu_grad(p):
    s = jax.nn.sigmoid(p)
    return s * (1.0 + p * (1.0 - s))


def _conv_bwd_tile(first, last, x, hprev, xnext, d, dnext, w, b, silu):
    hprev = jnp.where(first, 0.0, hprev)
    if silu:
        d = d * _silu_grad(_conv_pre(x, hprev, w, b))
        pre_next = _conv_pre(xnext, x[x.shape[0] - SUBLANES:], w, b)
        dnext = dnext * _silu_grad(pre_next)
    dnext = jnp.where(last, 0.0, dnext)
    dx = w[CONV_WIDTH - 1:CONV_WIDTH, :] * d
    row8 = lax.broadcasted_iota(jnp.int32, (SUBLANES, x.shape[1]), 0)
    dw8 = jnp.where(row8 == CONV_WIDTH - 1, _colsum(d * x), 0.0)
    for k in range(CONV_WIDTH - 1):
        s = CONV_WIDTH - 1 - k
        dx = dx + w[k:k + 1, :] * _shift_up(d, dnext, s)
        dw8 = dw8 + jnp.where(row8 == k, _colsum(d * _shift_down(x, hprev, s)), 0.0)
    return dx, dw8, _colsum(d)


def _ssd_dims(xbc_act, n_heads):
    T, XBC = xbc_act.shape
    GN = XBC // 4
    DS = XBC - 2 * GN
    G = SSD_GROUPS
    N = GN // G
    P = DS // n_heads
    K = n_heads // G
    return T, XBC, DS, GN, G, N, P, K


def _ssd_common(dt, alog, Q):
    a = -jnp.exp(alog)
    adt = dt * a
    li = lax.broadcasted_iota(jnp.int32, (Q, Q), 0)
    si = lax.broadcasted_iota(jnp.int32, (Q, Q), 1)
    causal = li >= si
    ltri = causal.astype(F32)
    acs = jnp.dot(ltri, adt, precision=HI, preferred_element_type=F32)
    acs_row = lax.dot_general(adt, ltri, (((0,), (1,)), ((), ())), precision=HI,
                              preferred_element_type=F32)
    return a, adt, causal, ltri, acs, acs_row


def _expander(g, K, P, W):
    r = lax.broadcasted_iota(jnp.int32, (LANES, W), 0)
    c = lax.broadcasted_iota(jnp.int32, (LANES, W), 1)
    return (c // P + g * K == r).astype(F32)


def _dotb(a, b, dn=(((1,), (0,)), ((), ()))):
    return lax.dot_general(a.astype(BF16), b.astype(BF16), dn, preferred_element_type=F32)


def _dot_split(a, sel, terms, dn=(((1,), (0,)), ((), ()))):
    selb = sel.astype(BF16)
    out = None
    for _ in range(terms):
        piece = a.astype(BF16)
        part = lax.dot_general(piece, selb, dn, preferred_element_type=F32)
        out = part if out is None else out + part
        a = a - piece.astype(F32)
    return out


_NT = (((1,), (1,)), ((), ()))
_TN = (((0,), (0,)), ((), ()))


def _ssd_fwd(xbc_act, dt, alog, dskip, n_heads, Q):
    T, XBC, DS, GN, G, N, P, K = _ssd_dims(xbc_act, n_heads)
    W = K * P
    nc = T // Q

    def body(xs_ref, b_ref, c_ref, dt_ref, alog_ref, d_ref, y_ref, hp_ref, h_scr):
        ci = pl.program_id(0)

        @pl.when(ci == 0)
        def _():
            h_scr[...] = jnp.zeros_like(h_scr)

        dtv = dt_ref[...]
        a, adt, causal, ltri, acs, acs_row = _ssd_common(dtv, alog_ref[...], Q)
        lane_head = lax.broadcasted_iota(jnp.int32, (Q, W), 1) // P
        for g in range(G):
            eg = _expander(g, K, P, W)
            dtb = _dot_split(dtv, eg, 3)
            acsb = _dot_split(acs, eg, 3)
            lastb = acsb[Q - 1:Q, :]
            db = _dot_split(jnp.broadcast_to(d_ref[...], (SUBLANES, LANES)), eg, 3)[0:1, :]
            xg = xs_ref[:, g * W:(g + 1) * W]
            bg = b_ref[:, g * N:(g + 1) * N]
            cg = c_ref[:, g * N:(g + 1) * N]
            xt = xg * dtb
            sc = _dotb(cg, bg, _NT)
            yd = jnp.zeros((Q, W), F32)
            for k in range(K):
                h = g * K + k
                seg = acs[:, h:h + 1] - acs_row[h:h + 1, :]
                lh = jnp.where(causal, jnp.exp(jnp.minimum(seg, 0.0)), 0.0)
                xk = jnp.where(lane_head == k, xt, 0.0)
                yd = yd + _dotb(sc * lh, xk)
            hp = h_scr[g]
            yoff = _dotb(cg, hp) * jnp.exp(acsb)
            y_ref[:, g * W:(g + 1) * W] = yd + yoff + xg * db
            e_end = jnp.exp(lastb - acsb)
            st = _dotb(bg, xt * e_end, _TN)
            hp_ref[0, g] = hp
            h_scr[g] = jnp.exp(lastb) * hp + st

    cb = DS // GN
    in_specs = [pl.BlockSpec((Q, DS), lambda c: (c, 0)),
                pl.BlockSpec((Q, GN), lambda c: (c, cb)),
                pl.BlockSpec((Q, GN), lambda c: (c, cb + 1)),
                pl.BlockSpec((Q, LANES), lambda c: (c, 0)),
                pl.BlockSpec((1, LANES), lambda c: (0, 0)),
                pl.BlockSpec((1, LANES), lambda c: (0, 0))]
    out_specs = [pl.BlockSpec((Q, DS), lambda c: (c, 0)),
                 pl.BlockSpec((1, G, N, W), lambda c: (c, 0, 0, 0))]
    return _pcall(body, name="ssd_fwd", grid=(nc,), in_specs=in_specs, out_specs=out_specs,
                  out_shape=[S((T, DS), F32), S((nc, G, N, W), F32)],
                  scratch_shapes=[pltpu.VMEM((G, N, W), F32)],
                  compiler_params=_cparams(("arbitrary",)))(xbc_act, xbc_act, xbc_act, dt, alog, dskip)


def _ssd_bwd(xbc_act, dt, alog, dskip, hprev, dy, n_heads, Q):
    T, XBC, DS, GN, G, N, P, K = _ssd_dims(xbc_act, n_heads)
    W = K * P
    nc = T // Q

    def body(xs_ref, b_ref, c_ref, dt_ref, alog_ref, d_ref, hp_ref, dy_ref,
             dxbc_ref, ddt_ref, dalog_ref, dd_ref, dh_scr):
        ci = pl.program_id(0)

        @pl.when(ci == 0)
        def _():
            dh_scr[...] = jnp.zeros_like(dh_scr)
            dalog_ref[...] = jnp.zeros_like(dalog_ref)
            dd_ref[...] = jnp.zeros_like(dd_ref)

        dtv = dt_ref[...]
        a, adt, causal, ltri, acs, acs_row = _ssd_common(dtv, alog_ref[...], Q)
        lane_head = lax.broadcasted_iota(jnp.int32, (Q, W), 1) // P
        lane128 = lax.broadcasted_iota(jnp.int32, (Q, LANES), 1)
        sub128 = lax.broadcasted_iota(jnp.int32, (LANES, Q), 0)
        rowq = lax.broadcasted_iota(jnp.int32, (Q, W), 0)
        dacs = jnp.zeros((Q, LANES), F32)
        dacs_row = jnp.zeros((LANES, Q), F32)
        ddt = jnp.zeros((Q, LANES), F32)
        dd_acc = jnp.zeros((1, LANES), F32)
        for g in range(G):
            eg = _expander(g, K, P, W)
            dtb = _dot_split(dtv, eg, 3)
            acsb = _dot_split(acs, eg, 3)
            lastb = acsb[Q - 1:Q, :]
            db = _dot_split(jnp.broadcast_to(d_ref[...], (SUBLANES, LANES)), eg, 3)[0:1, :]
            xg = xs_ref[:, g * W:(g + 1) * W]
            bg = b_ref[:, g * N:(g + 1) * N]
            cg = c_ref[:, g * N:(g + 1) * N]
            dyg = dy_ref[:, g * W:(g + 1) * W]
            hp = hp_ref[0, g]
            dhn = dh_scr[g]
            xt = xg * dtb
            sc = _dotb(cg, bg, _NT)
            eacs = jnp.exp(acsb)
            e_end = jnp.exp(lastb - acsb)
            elast = jnp.exp(lastb)

            wv = dyg * eacs
            dcg = _dotb(wv, hp, _NT)
            dhp = _dotb(cg, wv, _TN) + elast * dhn
            dacsb = dyg * (_dotb(cg, hp) * eacs)

            xe = xt * e_end
            dbg = _dotb(xe, dhn, _NT)
            v = _dotb(bg, dhn)
            dxt = v * e_end
            de = v * xe
            dacsb = dacsb - de
            dlastb = _colsum(de) + elast * jnp.sum(dhn * hp, axis=0, keepdims=True)

            dsc = jnp.zeros((Q, Q), F32)
            for k in range(K):
                h = g * K + k
                seg = acs[:, h:h + 1] - acs_row[h:h + 1, :]
                lh = jnp.where(causal, jnp.exp(jnp.minimum(seg, 0.0)), 0.0)
                mh = sc * lh
                dyk = jnp.where(lane_head == k, dyg, 0.0)
                dxt = dxt + jnp.where(lane_head == k, _dotb(mh, dyg, _TN), 0.0)
                dm = _dotb(dyk, xt, _NT)
                dsc = dsc + dm * lh
                gm = dm * mh
                dacs = dacs + jnp.where(lane128 == h, jnp.sum(gm, axis=1, keepdims=True), 0.0)
                dacs_row = dacs_row - jnp.where(sub128 == h, jnp.sum(gm, axis=0, keepdims=True), 0.0)
            dcg = dcg + _dotb(dsc, bg)
            dbg = dbg + _dotb(dsc, cg, _TN)

            dacsb = dacsb + jnp.where(rowq == Q - 1, dlastb, 0.0)
            dacs = dacs + _dot_split(dacsb, eg, 2, _NT)
            ddt = ddt + _dot_split(dxt * xg, eg, 2, _NT)
            dd_acc = dd_acc + _dot_split(jnp.broadcast_to(_colsum(dyg * xg), (SUBLANES, W)), eg, 2, _NT)[0:1, :]
            dxbc_ref[:, g * W:(g + 1) * W] = dxt * dtb + dyg * db
            dxbc_ref[:, DS + g * N:DS + (g + 1) * N] = dbg
            dxbc_ref[:, DS + GN + g * N:DS + GN + (g + 1) * N] = dcg
            dh_scr[g] = dhp

        eye = (lax.broadcasted_iota(jnp.int32, (LANES, LANES), 0) ==
               lax.broadcasted_iota(jnp.int32, (LANES, LANES), 1)).astype(F32)
        dacs = dacs + lax.dot_general(dacs_row, eye, _TN, precision=HI, preferred_element_type=F32)
        dadt = lax.dot_general(ltri, dacs, _TN, precision=HI, preferred_element_type=F32)
        ddt_ref[...] = ddt + dadt * a
        dalog_ref[...] += _colsum(dadt * dtv) * a
        dd_ref[...] += dd_acc

    cb = DS // GN
    rv = lambda c: nc - 1 - c
    in_specs = [pl.BlockSpec((Q, DS), lambda c: (rv(c), 0)),
                pl.BlockSpec((Q, GN), lambda c: (rv(c), cb)),
                pl.BlockSpec((Q, GN), lambda c: (rv(c), cb + 1)),
                pl.BlockSpec((Q, LANES), lambda c: (rv(c), 0)),
                pl.BlockSpec((1, LANES), lambda c: (0, 0)),
                pl.BlockSpec((1, LANES), lambda c: (0, 0)),
                pl.BlockSpec((1, G, N, W), lambda c: (rv(c), 0, 0, 0)),
                pl.BlockSpec((Q, DS), lambda c: (rv(c), 0))]
    out_specs = [pl.BlockSpec((Q, XBC), lambda c: (rv(c), 0)),
                 pl.BlockSpec((Q, LANES), lambda c: (rv(c), 0)),
                 pl.BlockSpec((1, LANES), lambda c: (0, 0)),
                 pl.BlockSpec((1, LANES), lambda c: (0, 0))]
    return _pcall(body, name="ssd_bwd", grid=(nc,), in_specs=in_specs, out_specs=out_specs,
                  out_shape=[S((T, XBC), F32), S((T, LANES), F32), S((1, LANES), F32), S((1, LANES), F32)],
                  scratch_shapes=[pltpu.VMEM((G, N, W), F32)],
                  compiler_params=_cparams(("arbitrary",)))(
                      xbc_act, xbc_act, xbc_act, dt, alog, dskip, hprev, dy)


def _blockdiag(x, w_ref, dn=(((1,), (0,)), ((), ()))):
    H, B, _ = w_ref.shape
    return jnp.concatenate([_dotb(x[:, h * B:(h + 1) * B], w_ref[h], dn) for h in range(H)], axis=1)


def _lru_elem(xl, r_pre, i_pre, lam):
    r = jax.nn.sigmoid(r_pre)
    i = jax.nn.sigmoid(i_pre)
    log_a = -LRU_C * r * jax.nn.softplus(-lam)
    a = jnp.exp(log_a)
    u = jnp.sqrt(1.0 - jnp.exp(2.0 * log_a)) * (i * xl)
    return a, u


def _lru_gates_fwd(xl, w_a, b_a, w_x, b_x, lam, tr):
    T, DL = xl.shape

    def body(xl_ref, wa_ref, ba_ref, wx_ref, bx_ref, lam_ref, a_ref, u_ref):
        x = xl_ref[...]
        r_pre = _blockdiag(x, wa_ref) + ba_ref[...]
        i_pre = _blockdiag(x, wx_ref) + bx_ref[...]
        a, u = _lru_elem(x, r_pre, i_pre, lam_ref[...])
        a_ref[...] = a
        u_ref[...] = u

    w3 = pl.BlockSpec(w_a.shape, lambda i: (0, 0, 0))
    vec = pl.BlockSpec((1, DL), lambda i: (0, 0))
    return _pcall(body, name="lru_gates_fwd", grid=(T // tr,),
                  in_specs=[_rt(tr, DL), w3, vec, w3, vec, vec],
                  out_specs=[_rt(tr, DL), _rt(tr, DL)], out_shape=[S((T, DL), F32), S((T, DL), F32)],
                  compiler_params=_cparams(("parallel",)))(xl, w_a, b_a, w_x, b_x, lam)


def _lru_gates_bwd(xl, w_a, b_a, w_x, b_x, lam, da, du, tr):
    T, DL = xl.shape
    H, B, _ = w_a.shape

    def body(xl_ref, wa_ref, ba_ref, wx_ref, bx_ref, lam_ref, da_ref, du_ref,
             dxl_ref, dwa_ref, dba_ref, dwx_ref, dbx_ref, dlam_ref):
        @pl.when(pl.program_id(0) == 0)
        def _():
            for r in (dwa_ref, dba_ref, dwx_ref, dbx_ref, dlam_ref):
                r[...] = jnp.zeros_like(r)

        x = xl_ref[...]
        r_pre = _blockdiag(x, wa_ref) + ba_ref[...]
        i_pre = _blockdiag(x, wx_ref) + bx_ref[...]
        _, vjp = jax.vjp(_lru_elem, x, r_pre, i_pre, lam_ref[...])
        dx, dr, di, dlam = vjp((da_ref[...], du_ref[...]))
        dxl_ref[...] = dx + _blockdiag(dr, wa_ref, _NT) + _blockdiag(di, wx_ref, _NT)
        for h in range(H):
            xh = x[:, h * B:(h + 1) * B]
            dwa_ref[h] += _dotb(xh, dr[:, h * B:(h + 1) * B], _TN)
            dwx_ref[h] += _dotb(xh, di[:, h * B:(h + 1) * B], _TN)
        dba_ref[...] += _colsum(dr)
        dbx_ref[...] += _colsum(di)
        dlam_ref[...] += dlam

    w3 = pl.BlockSpec(w_a.shape, lambda i: (0, 0, 0))
    vec = pl.BlockSpec((1, DL), lambda i: (0, 0))
    return _pcall(body, name="lru_gates_bwd", grid=(T // tr,),
                  in_specs=[_rt(tr, DL), w3, vec, w3, vec, vec, _rt(tr, DL), _rt(tr, DL)],
                  out_specs=[_rt(tr, DL), w3, vec, w3, vec, vec],
                  out_shape=[S((T, DL), F32), S(w_a.shape, F32), S((1, DL), F32), S(w_a.shape, F32),
                             S((1, DL), F32), S((1, DL), F32)],
                  compiler_params=_cparams(("arbitrary",)))(xl, w_a, b_a, w_x, b_x, lam, da, du)


def _groups(v):
    return v.reshape(v.shape[0] // SUBLANES, SUBLANES, v.shape[1])


def _rows_shifted(v, edge, up):
    sub = lax.broadcasted_iota(jnp.int32, v.shape, 1)
    if up:
        other = jnp.concatenate([v[1:], edge[None]], axis=0)
        return jnp.where(sub < SUBLANES - 1, pltpu.roll(v, SUBLANES - 1, 1), pltpu.roll(other, SUBLANES - 1, 1))
    other = jnp.concatenate([edge[None], v[:-1]], axis=0)
    return jnp.where(sub >= 1, pltpu.roll(v, 1, 1), pltpu.roll(other, 1, 1))


def _scan_tile(a, u, entering, emit, up):
    G = a.shape[0]
    sub = lax.broadcasted_iota(jnp.int32, a.shape, 1)
    d = 1
    while d < SUBLANES:
        if up:
            keep = sub < SUBLANES - d
            a_s = jnp.where(keep, pltpu.roll(a, SUBLANES - d, 1), 1.0)
            u_s = jnp.where(keep, pltpu.roll(u, SUBLANES - d, 1), 0.0)
        else:
            keep = sub >= d
            a_s = jnp.where(keep, pltpu.roll(a, d, 1), 1.0)
            u_s = jnp.where(keep, pltpu.roll(u, d, 1), 0.0)
        u = a * u_s + u
        a = a * a_s
        d *= 2
    for g in (reversed(range(G)) if up else range(G)):
        hg = u[g] + a[g] * entering
        emit(g, hg)
        entering = hg[0:1] if up else hg[SUBLANES - 1:SUBLANES]
    return entering


def _lru_scan_fwd(a, u, tr):
    T, DL = a.shape

    def body(a_ref, u_ref, h_ref, carry):
        @pl.when(pl.program_id(0) == 0)
        def _():
            carry[...] = jnp.zeros_like(carry)

        def emit(g, hg):
            h_ref[g * SUBLANES:(g + 1) * SUBLANES, :] = hg

        last = _scan_tile(_groups(a_ref[...]), _groups(u_ref[...]), carry[0:1, :], emit, up=False)
        carry[...] = jnp.broadcast_to(last, carry.shape)

    return _pcall(body, name="lru_scan_fwd", grid=(T // tr,), in_specs=[_rt(tr, DL), _rt(tr, DL)],
                  out_specs=_rt(tr, DL), out_shape=S((T, DL), F32),
                  scratch_shapes=[pltpu.VMEM((SUBLANES, DL), F32)],
                  compiler_params=_cparams(("arbitrary",)))(a, u)


def _lru_scan_bwd(a, h, dh, tr):
    T, DL = a.shape
    n = T // tr

    def body(a_ref, an_ref, h_ref, hp_ref, dh_ref, du_ref, da_ref, carry):
        i = pl.program_id(0)
        ti = n - 1 - i

        @pl.when(i == 0)
        def _():
            carry[...] = jnp.zeros_like(carry)

        a_next = _rows_shifted(_groups(a_ref[...]), jnp.where(ti == n - 1, 0.0, an_ref[...]), up=True)
        h_prev = _rows_shifted(_groups(h_ref[...]), jnp.where(ti == 0, 0.0, hp_ref[...]), up=False)

        def emit(g, gg):
            du_ref[g * SUBLANES:(g + 1) * SUBLANES, :] = gg
            da_ref[g * SUBLANES:(g + 1) * SUBLANES, :] = gg * h_prev[g]

        top = _scan_tile(a_next, _groups(dh_ref[...]), carry[0:1, :], emit, up=True)
        carry[...] = jnp.broadcast_to(top, carry.shape)

    return _pcall(body, name="lru_scan_bwd", grid=(n,),
                  in_specs=[_rt(tr, DL, 0, n), _halo_next(tr, DL, n, 0, n), _rt(tr, DL, 0, n),
                            _halo_prev(tr, DL, 0, n), _rt(tr, DL, 0, n)],
                  out_specs=[_rt(tr, DL, 0, n), _rt(tr, DL, 0, n)],
                  out_shape=[S((T, DL), F32), S((T, DL), F32)],
                  scratch_shapes=[pltpu.VMEM((SUBLANES, DL), F32)],
                  compiler_params=_cparams(("arbitrary",)))(a, a, h, h, dh)


def _adamw(w, g, m, v):
    m = ADAM_B1 * m + (1.0 - ADAM_B1) * g
    v = ADAM_B2 * v + (1.0 - ADAM_B2) * (g * g)
    m_hat = m / (1.0 - ADAM_B1 ** ADAM_STEP)
    v_hat = v / (1.0 - ADAM_B2 ** ADAM_STEP)
    delta = -ADAM_LR * (m_hat / (jnp.sqrt(v_hat) + ADAM_EPS) + ADAM_WD * w)
    return delta, m, v


def _adamw_big(name, w, m, v, part, recv, chip_idx):
    _, R, C = w.shape
    tr = _pick(R, (256, 128, 64, 32, 16))

    def body(ci_ref, w_ref, m_ref, v_ref, p_ref, r0, r1, r2, g_ref, d_ref, nm_ref, nv_ref):
        g = ((p_ref[...].astype(F32) + r0[...].astype(F32)) + r1[...].astype(F32)) + r2[...].astype(F32)
        d, nm, nv = _adamw(w_ref[...], g, m_ref[...], v_ref[...])
        g_ref[...] = g
        d_ref[...] = d
        nm_ref[...] = nm
        nv_ref[...] = nv

    r_spec = lambda s: pl.BlockSpec((None, tr, C), lambda i, ci: (s, i, 0))
    t2 = r_spec(0)
    gs = pltpu.PrefetchScalarGridSpec(
        num_scalar_prefetch=1, grid=(R // tr,),
        in_specs=[t2, t2, t2, pl.BlockSpec((None, tr, C), lambda i, ci: (ci[0], i, 0)),
                  r_spec(0), r_spec(1), r_spec(2)],
        out_specs=[t2, t2, t2, t2])
    return _pcall(body, name=name, grid_spec=gs, out_shape=[S((1, R, C), F32)] * 4,
                  compiler_params=_cparams(("parallel",)))(chip_idx, w, m, v, part, recv, recv, recv)


def _adamw_small(ws, gs, ms, vs):
    n = len(ws)

    def body(*refs):
        for k in range(n):
            d, nm, nv = _adamw(refs[k][...], refs[n + k][...], refs[2 * n + k][...], refs[3 * n + k][...])
            refs[4 * n + k][...] = d
            refs[5 * n + k][...] = nm
            refs[6 * n + k][...] = nv

    res = _pcall(body, name="adamw_small", out_shape=[S(w.shape, F32) for w in ws] * 3,
                 compiler_params=_cparams())(*ws, *gs, *ms, *vs)
    return res[:n], res[n:2 * n], res[2 * n:]


def _sum8(name, parts):
    _, R, C = parts.shape

    def body(p_ref, o_ref):
        acc = p_ref[0]
        for k in range(1, N_DEV):
            acc = acc + p_ref[k]
        o_ref[...] = acc

    return _pcall(body, name=name, out_shape=S((R, C), F32), compiler_params=_cparams())(parts)


def _pair_sum(name, full, recv, c_idx):
    _, R, C = full.shape
    tr = _pick(R, (256, 128, 64, 32, 16))

    def body(c_ref, f_ref, r_ref, o_ref):
        o_ref[...] = (f_ref[...].astype(F32) + r_ref[...].astype(F32)).astype(o_ref.dtype)

    gs = pltpu.PrefetchScalarGridSpec(
        num_scalar_prefetch=1, grid=(4, R // tr),
        in_specs=[pl.BlockSpec((None, tr, C), lambda j, i, c: (2 * j + c[0], i, 0)),
                  pl.BlockSpec((None, tr, C), lambda j, i, c: (j, i, 0))],
        out_specs=pl.BlockSpec((None, tr, C), lambda j, i, c: (j, i, 0)))
    return _pcall(body, name=name, grid_spec=gs, out_shape=S((4, R, C), BF16),
                  compiler_params=_cparams(("parallel", "parallel")))(c_idx, full, recv)


def _cast_bf16(name, w, dev_idx, row0=0, rows=None):
    C = w.shape[2]
    R = w.shape[1] if rows is None else rows
    tr = _pick(R, (256, 128, 64, 32, 16))
    b0 = row0 // tr

    def body(d_ref, w_ref, o_ref):
        o_ref[...] = w_ref[...].astype(BF16)

    gs = pltpu.PrefetchScalarGridSpec(
        num_scalar_prefetch=1, grid=(R // tr,),
        in_specs=[pl.BlockSpec((None, tr, C), lambda i, d: (0, i + b0, 0))],
        out_specs=pl.BlockSpec((None, tr, C), lambda i, d: (d[0], i, 0)))
    return _pcall(body, name=name, grid_spec=gs, out_shape=S((N_DEV, R, C), BF16),
                  compiler_params=_cparams(("parallel",)))(dev_idx, w)


def _own_block(v, dev):
    return lax.dynamic_update_slice(lax.empty((N_DEV,) + v.shape, v.dtype), v[None], (dev,) + (0,) * v.ndim)


_ANY = pl.BlockSpec(memory_space=pl.ANY)


def _position():
    return lax.axis_index("x"), lax.axis_index("y"), lax.axis_index("c")


def _allgather(name, bufs):
    n = len(bufs)

    def body(*refs):
        outs = refs[n:2 * n]
        send, recv = refs[2 * n:]
        x, y, c = _position()
        me, sib = (x, y, c), (x, y, 1 - c)
        chips = [(1 - x, y), (x, 1 - y), (1 - x, 1 - y)]

        def copy(a, k, block, to):
            bx, by, bc = block
            blk = outs[a].at[4 * bx + 2 * by + bc]
            return pltpu.make_async_remote_copy(
                src_ref=blk, dst_ref=blk, send_sem=send.at[a, k], recv_sem=recv.at[a, k],
                device_id=to, device_id_type=MESH)

        first = []
        for a in range(n):
            first.append(copy(a, 0, me, sib))
            first += [copy(a, 1 + j, me, (*chip, c)) for j, chip in enumerate(chips)]
        for cp in first:
            cp.start()
        passed = []
        for j, chip in enumerate(chips):
            for a in range(n):
                copy(a, 1 + j, (*chip, c), me).wait_recv()
                cp = copy(a, 4 + j, (*chip, c), sib)
                cp.start()
                passed.append(cp)
        for a in range(n):
            copy(a, 0, sib, me).wait_recv()
        for j, chip in enumerate(chips):
            for a in range(n):
                copy(a, 4 + j, (*chip, 1 - c), me).wait_recv()
        for cp in first + passed:
            cp.wait_send()

    return _pcall(body, name=name, in_specs=[_ANY] * n, out_specs=[_ANY] * n,
                  out_shape=[S(b.shape, b.dtype) for b in bufs], input_output_aliases={a: a for a in range(n)},
                  scratch_shapes=[pltpu.SemaphoreType.DMA((n, 7)), pltpu.SemaphoreType.DMA((n, 7))])(*bufs)


def _rs_sibling(name, fulls):
    n = len(fulls)

    def body(*refs):
        ins, outs = refs[:n], refs[n:2 * n]
        send, recv = refs[2 * n:]
        x, y, c = _position()
        copies = []
        for a in range(n):
            for j in range(4):
                copies.append(pltpu.make_async_remote_copy(
                    src_ref=ins[a].at[2 * j + (1 - c)], dst_ref=outs[a].at[j], send_sem=send.at[a, j],
                    recv_sem=recv.at[a, j], device_id=(x, y, 1 - c), device_id_type=MESH))
        for cp in copies:
            cp.start()
        for cp in copies:
            cp.wait()

    return _pcall(body, name=name, in_specs=[_ANY] * n, out_specs=[_ANY] * n,
                  out_shape=[S((4,) + f.shape[1:], f.dtype) for f in fulls],
                  scratch_shapes=[pltpu.SemaphoreType.DMA((n, 4)), pltpu.SemaphoreType.DMA((n, 4))])(*fulls)


_HBM = pl.BlockSpec(memory_space=pltpu.HBM)
_SEM = pl.BlockSpec(memory_space=pltpu.SEMAPHORE)
_EFFECT = pltpu.SideEffectType.DATAFLOW_SIDE_EFFECTING


def _remote_copies(copies_fn, srcs, lands, send, recv):
    x, y, c = _position()
    return [pltpu.make_async_remote_copy(src_ref=s, dst_ref=d, send_sem=send[i], recv_sem=recv[i], device_id=to,
                                         device_id_type=MESH)
            for i, (s, d, to) in enumerate(copies_fn(x, y, c, srcs, lands))]


def _split_start(name, srcs, lands, copies_fn, nc, after=()):
    n, nl, na = len(srcs), len(lands), len(after)

    def body(*refs):
        src_refs, land_refs = refs[:n], refs[n:n + nl]
        outs = refs[n + nl + na:]
        for cp in _remote_copies(copies_fn, src_refs, land_refs, outs[:nc], outs[nc:2 * nc]):
            cp.start()
        outs[-1][...] = jnp.zeros_like(outs[-1])

    hbm = lambda a: pltpu.with_memory_space_constraint(a, pltpu.HBM)
    res = _pcall(
        body, name=name, in_specs=[_HBM] * (n + nl) + [_ANY] * na,
        out_specs=[_SEM] * (2 * nc) + [_HBM] * (n + nl) + [pl.BlockSpec(memory_space=pltpu.VMEM)],
        out_shape=[pltpu.SemaphoreType.DMA(())] * (2 * nc) + [pltpu.HBM(s.shape, s.dtype) for s in srcs]
        + [pltpu.HBM(l.shape, l.dtype) for l in lands] + [S((SUBLANES, LANES), F32)],
        input_output_aliases={i: 2 * nc + i for i in range(n + nl)},
        compiler_params=pltpu.CompilerParams(has_side_effects=_EFFECT),
    )(*[hbm(s) for s in srcs], *[hbm(l) for l in lands], *after)
    return res[:2 * nc], res[2 * nc:2 * nc + n], res[2 * nc + n:2 * nc + n + nl], res[-1]


def _split_wait(name, sems, srcs, lands, after, copies_fn, nc):
    n, nl = len(srcs), len(lands)

    def body(*refs):
        src_refs, land_refs = refs[:n], refs[n:n + nl]
        sem_refs = refs[n + nl:n + nl + 2 * nc]
        for cp in _remote_copies(copies_fn, src_refs, land_refs, sem_refs[:nc], sem_refs[nc:]):
            cp.wait_send()
            cp.wait_recv()

    res = _pcall(
        body, name=name, in_specs=[_HBM] * (n + nl) + [_SEM] * (2 * nc) + [_ANY],
        out_specs=[_HBM] * (n + nl), out_shape=[pltpu.HBM(a.shape, a.dtype) for a in list(srcs) + list(lands)],
        input_output_aliases={i: i for i in range(n + nl)},
        compiler_params=pltpu.CompilerParams(has_side_effects=_EFFECT),
    )(*srcs, *lands, *sems, after)
    return res[:n], res[n:]


def _other_chips(x, y):
    return [(1 - x, y), (x, 1 - y), (1 - x, 1 - y)]


def _ag_copies(x, y, c, srcs, lands):
    out = []
    for land in lands:
        blk = land.at[4 * x + 2 * y + c]
        out.append((blk, blk, (x, y, 1 - c)))
        out += [(blk, blk, (px, py, c)) for px, py in _other_chips(x, y)]
    return out


def _rs_copies(x, y, c, srcs, lands):
    return [(s.at[2 * px + py], land.at[j], (px, py, c))
            for s, land in zip(srcs, lands) for j, (px, py) in enumerate(_other_chips(x, y))]


def _sib_copies(x, y, c, srcs, lands):
    return [(s.at[2 * j + (1 - c)], land.at[j], (x, y, 1 - c)) for s, land in zip(srcs, lands) for j in range(4)]


def _ag_finish(name, lands):
    n = len(lands)

    def body(*refs):
        outs = refs[n:2 * n]
        send, recv = refs[2 * n:]
        x, y, c = _position()

        def swap(a, j, px, py, pc):
            blk = outs[a].at[4 * px + 2 * py + pc]
            return pltpu.make_async_remote_copy(src_ref=blk, dst_ref=blk, send_sem=send.at[a, j], recv_sem=recv.at[a, j],
                                                device_id=(x, y, 1 - c), device_id_type=MESH)

        chips = _other_chips(x, y)
        sends = [swap(a, j, px, py, c) for a in range(n) for j, (px, py) in enumerate(chips)]
        for cp in sends:
            cp.start()
        for a in range(n):
            for j, (px, py) in enumerate(chips):
                swap(a, j, px, py, 1 - c).wait_recv()
        for cp in sends:
            cp.wait_send()

    return _pcall(body, name=name, in_specs=[_ANY] * n, out_specs=[_ANY] * n,
                  out_shape=[S(l.shape, l.dtype) for l in lands], input_output_aliases={a: a for a in range(n)},
                  scratch_shapes=[pltpu.SemaphoreType.DMA((n, 3)), pltpu.SemaphoreType.DMA((n, 3))])(*lands)


def _pad_lanes(v):
    return jnp.pad(v, ((0, 0), (0, LANES - v.shape[1])))


def _flat_rows(pieces):
    flat = jnp.concatenate([p.reshape(-1) for p in pieces])
    rows = -(-flat.shape[0] // (SMALL_W * SUBLANES)) * SUBLANES
    return jnp.pad(flat, (0, rows * SMALL_W - flat.shape[0])).reshape(rows, SMALL_W)


def _unflat(buf, shapes):
    flat = buf.reshape(-1)
    out, off = [], 0
    for sh in shapes:
        n = 1
        for d in sh:
            n *= d
        out.append(flat[off:off + n].reshape(sh))
        off += n
    return out


def kernel(x, pre_mix_norm, w_in, ssd_conv_w, ssd_conv_b, ssd_dt_bias, ssd_a_log, ssd_d, ssd_norm, lru_conv_w, lru_conv_b, lru_w_a, lru_b_a, lru_w_x, lru_b_x, lru_lambda, lru_norm, w_out, post_mix_norm, pre_mlp_norm, w_mlp_in, w_mlp_out, post_mlp_norm, loss_target, m_pre_mix_norm, m_w_in, m_ssd_conv_w, m_ssd_conv_b, m_ssd_dt_bias, m_ssd_a_log, m_ssd_d, m_ssd_norm, m_lru_conv_w, m_lru_conv_b, m_lru_w_a, m_lru_b_a, m_lru_w_x, m_lru_b_x, m_lru_lambda, m_lru_norm, m_w_out, m_post_mix_norm, m_pre_mlp_norm, m_w_mlp_in, m_w_mlp_out, m_post_mlp_norm, v_pre_mix_norm, v_w_in, v_ssd_conv_w, v_ssd_conv_b, v_ssd_dt_bias, v_ssd_a_log, v_ssd_d, v_ssd_norm, v_lru_conv_w, v_lru_conv_b, v_lru_w_a, v_lru_b_a, v_lru_w_x, v_lru_b_x, v_lru_lambda, v_lru_norm, v_w_out, v_post_mix_norm, v_pre_mlp_norm, v_w_mlp_in, v_w_mlp_out, v_post_mlp_norm):
    names = ['pre_mix_norm', 'w_in', 'ssd_conv_w', 'ssd_conv_b', 'ssd_dt_bias', 'ssd_a_log', 'ssd_d', 'ssd_norm',
             'lru_conv_w', 'lru_conv_b', 'lru_w_a', 'lru_b_a', 'lru_w_x', 'lru_b_x', 'lru_lambda', 'lru_norm',
             'w_out', 'post_mix_norm', 'pre_mlp_norm', 'w_mlp_in', 'w_mlp_out', 'post_mlp_norm']
    loc = locals()
    W = {n: loc[n] for n in names}
    Mo = {n: loc["m_" + n] for n in names}
    Vo = {n: loc["v_" + n] for n in names}
    big = ['w_in', 'w_out', 'w_mlp_in', 'w_mlp_out']

    px, py, pc = _position()
    dev = 4 * px + 2 * py + pc
    dev_idx = jnp.reshape(dev, (1,)).astype(jnp.int32)
    c_idx = jnp.reshape(pc, (1,)).astype(jnp.int32)
    chip_idx = jnp.reshape(2 * px + py, (1,)).astype(jnp.int32)

    _, T, D = x.shape
    x2 = x.reshape(T, D)
    tgt = loss_target.reshape(T, D)
    n_heads = ssd_dt_bias.shape[1]
    XBC = ssd_conv_b.shape[1]
    GN = XBC // 4
    DS = XBC - 2 * GN
    DL = lru_norm.shape[1]
    DFF = w_mlp_in.shape[2] * N_DEV
    DIN = w_in.shape[2] * N_DEV
    NP = XBC + DS + 2 * DL + LANES
    assert DS % GN == 0 and XBC % DS == 0 and DS == DL and n_heads <= LANES
    cb_z, cb_gate, cb_xl, cb_dt = XBC // DS, XBC // DS + 1, XBC // DS + 2, (XBC + DS + 2 * DL) // LANES
    tr = min(256, T // 2)
    nt = T // tr
    Q = min(256, T // 2)

    Dh = D // 2
    sh_a = _cast_bf16("cast_w_in_a", W['w_in'], dev_idx, 0, Dh)
    sh_b = _cast_bf16("cast_w_in_b", W['w_in'], dev_idx, Dh, Dh)
    later = big[1:]
    sh = {n: _cast_bf16("cast_" + n, W[n], dev_idx) for n in later}
    g_in_a, g_cs, g_cl = _allgather(
        "allgather_w_in_a", [sh_a, _own_block(ssd_conv_w[0], dev), _own_block(lru_conv_w[0], dev)])
    b_sems, _, b_lands, b_token = _split_start("allgather_w_in_b_start", [], [sh_b], _ag_copies, 4, after=[g_in_a])
    ag_sems, ag_srcs, ag_lands, ag_token = _split_start(
        "allgather_later_start", [], [sh[n] for n in later], _ag_copies, 4 * len(later), after=[b_token])
    conv_s = jnp.transpose(g_cs, (1, 0, 2)).reshape(CONV_WIDTH, XBC)
    conv_l = jnp.transpose(g_cl, (1, 0, 2)).reshape(CONV_WIDTH, DL)
    wb = DIN // N_DEV
    o_z, o_xbc, o_dt, o_gate, o_xl = 0, DS, DS + XBC, DS + XBC + n_heads, DS + XBC + n_heads + DL
    segs = [(o_xbc, o_xbc + XBC, 0), (o_z, o_z + DS, XBC), (o_gate, o_gate + DL, XBC + DS),
            (o_xl, o_xl + DL, XBC + DS + DL), (o_dt, o_dt + n_heads, NP - LANES)]

    def ref_cols(g, lo, hi):
        out = []
        while lo < hi:
            k = lo // wb
            e = min(hi, (k + 1) * wb)
            out.append(g[k, :, lo - k * wb:e - k * wb])
            lo = e
        return out

    def laid_out(g):
        return jnp.concatenate([p for a, b, _ in segs for p in ref_cols(g, a, b)]
                               + [jnp.zeros((g.shape[1], LANES - n_heads), BF16)], axis=1)

    def my_cols(g, lo, hi):
        out = []
        for a, b, m in sorted(segs):
            s, e = max(lo, a), min(hi, b)
            if s < e:
                out.append(g[:, m + s - a:m + e - a])
        return out

    wp_a = laid_out(g_in_a)
    dt_bias = _pad_lanes(ssd_dt_bias)
    a_log = _pad_lanes(ssd_a_log)
    d_skip = _pad_lanes(ssd_d)
    wa_b, wx_b = lru_w_a[0].astype(BF16), lru_w_x[0].astype(BF16)
    b_a, b_x = lru_b_a.reshape(1, DL), lru_b_x.reshape(1, DL)

    def f_norm_in(first, last, xv, g):
        return (_rms(xv, g),), ()
    (h,) = _rows_call("norm_in", f_norm_in, nt, [x2, pre_mix_norm], [_rt(tr, D), _full(pre_mix_norm)],
                      [((T, D), BF16, _rt(tr, D))], [], 'tf', deps=[ag_token])

    (proj_a,) = _mm("proj_a", h, wp_a, a_cols=(0, Dh))
    (dt_a,) = _mm("proj_dt_a", h, wp_a[:, NP - LANES:], a_cols=(0, Dh))
    _, b_lands = _split_wait("allgather_w_in_b_wait", b_sems, [], b_lands, proj_a, _ag_copies, 4)
    (g_in_b,) = _ag_finish("allgather_w_in_b_finish", b_lands)
    wp_b = laid_out(g_in_b)
    add = lambda r, e: r + e
    (proj,) = _mm("proj_b", h, wp_b, a_cols=(Dh, Dh), extra=proj_a, outs=((BF16, add),))
    (dt_raw,) = _mm("proj_dt_b", h, wp_b[:, NP - LANES:], a_cols=(Dh, Dh), extra=dt_a, outs=((F32, add),))

    cwx = min(1024, XBC)

    def f_ssd_pre(first, last, xbc, halo, w, b):
        pre = _conv_pre(xbc, jnp.where(first, 0.0, halo), w, b)
        return (pre * jax.nn.sigmoid(pre),), ()
    (xbc_act,) = _rows_call(
        "ssd_pre", f_ssd_pre, nt, [proj, proj, conv_s, ssd_conv_b],
        [_rt(tr, XBC), _halo_prev(tr, XBC, rows=PACKED_ROWS), _full(conv_s), _full(ssd_conv_b)],
        [((T, XBC), F32, _rt(tr, XBC))], [], ['t', 'p0', 'f', 'f'], cw=cwx)

    def f_ssd_dt(first, last, dtr, dtb):
        return (jax.nn.softplus(dtr + dtb),), ()
    (dt,) = _rows_call("ssd_dt", f_ssd_dt, nt, [dt_raw, dt_bias], [_rt(tr, LANES), _full(dt_bias)],
                       [((T, LANES), F32, _rt(tr, LANES))], [], 'tf')

    y_ssd, h_prev = _ssd_fwd(xbc_act, dt, a_log, d_skip, n_heads, Q)

    gw = DS // SSD_GROUPS

    def ssd_post(y, z, g):
        yz = y * jax.nn.silu(z)
        parts = []
        for k in range(SSD_GROUPS):
            yk = yz[:, k * gw:(k + 1) * gw]
            parts.append(yk * lax.rsqrt(jnp.mean(yk * yk, axis=-1, keepdims=True) + EPS))
        return jnp.concatenate(parts, axis=-1) * g

    def f_ssd_post(first, last, y, z, g):
        return (ssd_post(y, z, g),), ()
    (mixcat,) = _rows_call("ssd_post", f_ssd_post, nt, [y_ssd, proj, ssd_norm],
                           [_rt(tr, DS), _rt(tr, DS, cb_z), _full(ssd_norm)], [((T, DS + DL), BF16, _rt(tr, DS))], [],
                           'ttf')

    def f_lru_pre(first, last, xv, halo, w, b):
        return (_conv_pre(xv, jnp.where(first, 0.0, halo), w, b),), ()
    (xl,) = _rows_call("lru_pre", f_lru_pre, nt, [proj, proj, conv_l, lru_conv_b],
                       [_rt(tr, DL, cb_xl), _halo_prev(tr, DL, cb_xl, rows=PACKED_ROWS), _full(conv_l),
                        _full(lru_conv_b)],
                       [((T, DL), F32, _rt(tr, DL))], [], ['t', 'p0', 'f', 'f'])

    a_lru, u_lru = _lru_gates_fwd(xl, wa_b, b_a, wx_b, b_x, lru_lambda, tr)
    h_lru = _lru_scan_fwd(a_lru, u_lru, tr)

    def lru_post(hv, gate, g):
        return _rms(hv * jax.nn.gelu(gate), g)

    def f_lru_post(first, last, hv, gate, g):
        return (lru_post(hv, gate, g),), ()
    cb_l = DS // DL
    (mixcat,) = _rows_call("lru_post", f_lru_post, nt, [h_lru, proj, lru_norm],
                           [_rt(tr, DL), _rt(tr, DL, cb_gate), _full(lru_norm)],
                           [((T, DS + DL), BF16, _rt(tr, DL, cb_l))], [], 'ttf', into=mixcat)

    ag_srcs, ag_lands = _split_wait("allgather_later_wait", ag_sems, ag_srcs, ag_lands, mixcat, _ag_copies,
                                    4 * len(later))
    g_out, g_mi, g_mo = _ag_finish("allgather_later_finish", ag_lands)
    w_out_f = g_out.reshape(DS + DL, D)
    w_mi_f = jnp.transpose(g_mi, (1, 0, 2)).reshape(D, DFF)
    w_mo_f = g_mo.reshape(DFF, D)
    (mix,) = _mm("mix", mixcat, w_out_f, outs=((BF16, None),))

    def f_post_mix(first, last, xv, mx, gpm, gpl):
        x1 = xv + _rms(mx, gpm)
        return (x1, _rms(x1, gpl)), ()
    x1, hn = _rows_call("post_mix", f_post_mix, nt, [x2, mix, post_mix_norm, pre_mlp_norm],
                        [_rt(tr, D), _rt(tr, D), _full(post_mix_norm), _full(pre_mlp_norm)],
                        [((T, D), F32, _rt(tr, D)), ((T, D), BF16, _rt(tr, D))], [], 'ttff')

    hm, act = _mm("mlp_in", hn, w_mi_f,
                  outs=((BF16, None), (BF16, lambda r, e: jnp.square(jnp.maximum(r, 0.0)))))
    (hm2,) = _mm("mlp_out", act, w_mo_f, outs=((BF16, None),))

    def f_final(first, last, x1v, hm2v, g, tg):
        def fwd(hv, gv):
            return x1v + _rms(hv, gv)
        x2v, vjp = jax.vjp(fwd, hm2v, g)
        err = x2v - tg
        dx2 = err * (1.0 / D)
        dh, dg = vjp(dx2)
        loss = jnp.full((1, LANES), 0.5 / D, F32) * jnp.sum(err * err)
        return (dx2, dh), (dg, loss)
    dx1a, dhm2, g_post_mlp, loss_part = _rows_call(
        "loss_head", f_final, nt, [x1, hm2, post_mlp_norm, tgt],
        [_rt(tr, D), _rt(tr, D), _full(post_mlp_norm), _rt(tr, D)],
        [((T, D), F32, _rt(tr, D)), ((T, D), BF16, _rt(tr, D))], [(1, D), (1, LANES)], 'ttft')

    def rs_chips_start(n, full, from_sib):
        pair = _pair_sum("pair_sum_" + n, full, from_sib, c_idx)
        sems, srcs, lands, token = _split_start("rs_start_" + n, [pair], [lax.empty((3,) + pair.shape[1:], BF16)],
                                                _rs_copies, 3)
        return (sems, srcs, lands), token

    def rs_begin(n, full):
        (from_sib,) = _rs_sibling("rs_sibling_" + n, [full])
        return rs_chips_start(n, full, from_sib)

    def rs_sibling_start(n, full):
        sems, srcs, lands, token = _split_start("rs_sibling_start_" + n, [full],
                                                [lax.empty((4,) + full.shape[1:], BF16)], _sib_copies, 4)
        return (sems, srcs, lands), token

    def rs_begin_late(n, state, after):
        (full,), (from_sib,) = _split_wait("rs_sibling_wait_" + n, *state, after, _sib_copies, 4)
        return rs_chips_start(n, full, from_sib)

    def rs_end(n, state, after):
        (pair,), (recv,) = _split_wait("rs_wait_" + n, *state, after, _rs_copies, 3)
        return pair, recv

    (gw_mo,) = _mm("dw_mlp_out", act, dhm2, ta=True, outs=((BF16, None),))
    sib_mo, tok = rs_sibling_start('w_mlp_out', gw_mo.reshape(N_DEV, DFF // N_DEV, D))
    (dhm,) = _mm("d_mlp_act", dhm2, w_mo_f, tb=True, extra=hm,
                 outs=((BF16, lambda r, e: r * (2.0 * jnp.maximum(e.astype(F32), 0.0))),), deps=[tok])
    rs_mo, tok = rs_begin_late('w_mlp_out', sib_mo, dhm)
    (gw_mi,) = _mm("dw_mlp_in", hn, dhm, ta=True, outs=((BF16, None),), out_blocks=N_DEV, deps=[tok])
    sib_mi, tok = rs_sibling_start('w_mlp_in', gw_mi)
    (dhn,) = _mm("d_mlp_in", dhm, w_mi_f, tb=True, outs=((BF16, None),), deps=[tok])
    rs_mi, tok_mi = rs_begin_late('w_mlp_in', sib_mi, dhn)

    def f_post_mix_bwd(first, last, x1v, mx, gpm, gpl, dhnv, dxa):
        _, vjp1 = jax.vjp(_rms, x1v, gpl)
        dx1, dgpl = vjp1(dhnv)
        dx1 = dx1 + dxa
        _, vjp2 = jax.vjp(_rms, mx, gpm)
        dmx, dgpm = vjp2(dx1)
        return (dx1, dmx), (dgpl, dgpm)
    dx1, dmix, g_pre_mlp, g_post_mix = _rows_call(
        "post_mix_bwd", f_post_mix_bwd, nt, [x1, mix, post_mix_norm, pre_mlp_norm, dhn, dx1a],
        [_rt(tr, D), _rt(tr, D), _full(post_mix_norm), _full(pre_mlp_norm), _rt(tr, D), _rt(tr, D)],
        [((T, D), F32, _rt(tr, D)), ((T, D), BF16, _rt(tr, D))], [(1, D), (1, D)], 'ttfftt', deps=[tok_mi])

    (gw_out,) = _mm("dw_out", mixcat, dmix, ta=True, outs=((BF16, None),))
    sib_out, tok = rs_sibling_start('w_out', gw_out.reshape(N_DEV, -1, D))
    (dmixcat,) = _mm("d_mix", dmix, w_out_f, tb=True, outs=((BF16, None),), deps=[tok])
    rs_out, tok_out = rs_begin_late('w_out', sib_out, dmixcat)

    def f_lru_post_bwd(first, last, hv, gate, g, dy):
        _, vjp = jax.vjp(lru_post, hv, gate, g)
        dh_, dgate, dg = vjp(dy)
        return (dgate, dh_), (dg,)
    dproj, dh_lru, g_lru_norm = _rows_call(
        "lru_post_bwd", f_lru_post_bwd, nt, [h_lru, proj, lru_norm, dmixcat],
        [_rt(tr, DL), _rt(tr, DL, cb_gate), _full(lru_norm), _rt(tr, DL, cb_l)],
        [((T, NP), BF16, _rt(tr, DL, cb_gate)), ((T, DL), F32, _rt(tr, DL))], [(1, DL)], 'ttft', deps=[tok_out])

    du_lru, da_lru = _lru_scan_bwd(a_lru, h_lru, dh_lru, tr)
    dxl, g_wa, g_ba, g_wx, g_bx, g_lam = _lru_gates_bwd(xl, wa_b, b_a, wx_b, b_x, lru_lambda, da_lru, du_lru, tr)

    conv_bwd_kinds = ['t', 'p0', 'n0', 't', 'n3', 'f', 'f']

    def f_lru_pre_bwd(first, last, xv, hp, xn, d, dn, w, b):
        dx, dw8, db = _conv_bwd_tile(first, last, xv, hp, xn, d, dn, w, b, silu=False)
        return (dx,), (dw8, db)
    dproj, g_convl8, g_convl_b = _rows_call(
        "lru_pre_bwd", f_lru_pre_bwd, nt, [proj, proj, proj, dxl, dxl, conv_l, lru_conv_b],
        [_rt(tr, DL, cb_xl), _halo_prev(tr, DL, cb_xl, rows=PACKED_ROWS),
         _halo_next(tr, DL, nt, cb_xl, rows=PACKED_ROWS), _rt(tr, DL),
         _halo_next(tr, DL, nt), _full(conv_l), _full(lru_conv_b)],
        [((T, NP), BF16, _rt(tr, DL, cb_xl))], [(SUBLANES, DL), (1, DL)], conv_bwd_kinds, into=dproj)

    def f_ssd_post_bwd(first, last, y, z, g, dy):
        _, vjp = jax.vjp(ssd_post, y, z, g)
        dy_, dz, dg = vjp(dy)
        return (dz, dy_), (dg,)
    dproj, dy_ssd, g_ssd_norm = _rows_call(
        "ssd_post_bwd", f_ssd_post_bwd, nt, [y_ssd, proj, ssd_norm, dmixcat],
        [_rt(tr, DS), _rt(tr, DS, cb_z), _full(ssd_norm), _rt(tr, DS, 0)],
        [((T, NP), BF16, _rt(tr, DS, cb_z)), ((T, DS), F32, _rt(tr, DS))], [(1, DS)], 'ttft', into=dproj)

    dxbc_act, ddt, g_alog, g_dskip = _ssd_bwd(xbc_act, dt, a_log, d_skip, h_prev, dy_ssd, n_heads, Q)

    def f_ssd_pre_bwd(first, last, xv, hp, xn, d, dn, w, b):
        dx, dw8, db = _conv_bwd_tile(first, last, xv, hp, xn, d, dn, w, b, silu=True)
        return (dx,), (dw8, db)
    dproj, g_convs8, g_convs_b = _rows_call(
        "ssd_pre_bwd", f_ssd_pre_bwd, nt, [proj, proj, proj, dxbc_act, dxbc_act, conv_s, ssd_conv_b],
        [_rt(tr, XBC), _halo_prev(tr, XBC, rows=PACKED_ROWS), _halo_next(tr, XBC, nt, rows=PACKED_ROWS),
         _rt(tr, XBC), _halo_next(tr, XBC, nt),
         _full(conv_s), _full(ssd_conv_b)],
        [((T, NP), BF16, _rt(tr, XBC))], [(SUBLANES, XBC), (1, XBC)], conv_bwd_kinds, into=dproj, cw=cwx)

    def f_ssd_dt_bwd(first, last, ddtv, dtr, dtb):
        ddtr = ddtv * jax.nn.sigmoid(dtr + dtb)
        return (ddtr,), (_colsum(ddtr),)
    dproj, g_dtb = _rows_call(
        "ssd_dt_bwd", f_ssd_dt_bwd, nt, [ddt, dt_raw, dt_bias],
        [_rt(tr, LANES), _rt(tr, LANES), _full(dt_bias)],
        [((T, NP), BF16, _rt(tr, LANES, cb_dt))], [(1, LANES)], 'ttf', into=dproj)
    small = {
        'ssd_conv_w': g_convs8[:CONV_WIDTH], 'ssd_conv_b': g_convs_b,
        'ssd_dt_bias': g_dtb[:, :n_heads], 'ssd_a_log': g_alog[:, :n_heads], 'ssd_d': g_dskip[:, :n_heads],
        'ssd_norm': g_ssd_norm, 'lru_conv_w': g_convl8[:CONV_WIDTH], 'lru_conv_b': g_convl_b,
        'lru_w_a': g_wa, 'lru_b_a': g_ba, 'lru_w_x': g_wx, 'lru_b_x': g_bx, 'lru_lambda': g_lam,
        'lru_norm': g_lru_norm, 'post_mix_norm': g_post_mix, 'pre_mlp_norm': g_pre_mlp,
        'post_mlp_norm': g_post_mlp, 'loss': loss_part[:, :1],
    }
    wide = ['lru_w_a', 'lru_w_x']
    narrow = [n for n in small if n not in wide]
    lb = lru_w_a.shape[-1]
    s_srcs = [_flat_rows([small[n] for n in narrow]), g_wa.reshape(-1, lb), g_wx.reshape(-1, lb)]
    s_sems, s_srcs, s_lands, tok = _split_start(
        "small_grads_start", [], [_own_block(a, dev) for a in s_srcs], _ag_copies, 4 * len(s_srcs))

    (gwp,) = _mm("dw_proj", h, dproj, ta=True, outs=((BF16, None),), deps=[tok])
    rs_in, tok = rs_begin(
        'w_in', jnp.stack([jnp.concatenate(my_cols(gwp, k * wb, (k + 1) * wb), axis=1) for k in range(N_DEV)]))
    (dh_a,) = _mm("d_proj_a", dproj, wp_a, tb=True, outs=((BF16, None),), deps=[tok])
    (dh_b,) = _mm("d_proj_b", dproj, wp_b, tb=True, outs=((BF16, None),))

    def f_norm_in_bwd(first, last, xv, g, dha, dhb, dxa):
        _, vjp = jax.vjp(_rms, xv, g)
        dx, dg = vjp(jnp.concatenate([dha, dhb], axis=1))
        return (dx + dxa,), (dg,)
    grad_x, g_pre_mix = _rows_call(
        "norm_in_bwd", f_norm_in_bwd, nt, [x2, pre_mix_norm, dh_a, dh_b, dx1],
        [_rt(tr, D), _full(pre_mix_norm), _rt(tr, Dh), _rt(tr, Dh), _rt(tr, D)], [((T, D), F32, _rt(tr, D))],
        [(1, D)], 'tfttt')

    (g_pm8,) = _allgather("allgather_pre_mix_grad", [_own_block(g_pre_mix, dev)])
    _, s_lands = _split_wait("small_grads_wait", s_sems, s_srcs, s_lands, g_pm8, _ag_copies, 4 * len(s_lands))
    g_narrow, g_wa8, g_wx8 = _ag_finish("small_grads_finish", s_lands)
    summed = dict(zip(narrow, _unflat(_sum8("sum_small_grads", g_narrow), [small[n].shape for n in narrow])))
    summed['pre_mix_norm'] = _sum8("sum_pre_mix_grad", g_pm8)
    summed['lru_w_a'] = _sum8("sum_lru_w_a_grads", g_wa8)
    summed['lru_w_x'] = _sum8("sum_lru_w_x_grads", g_wx8)
    loss = summed.pop('loss').reshape(())
    for n, full_w in (('ssd_conv_w', XBC), ('lru_conv_w', DL)):
        wdt = full_w // N_DEV
        summed[n] = lax.dynamic_slice_in_dim(summed[n], dev * wdt, wdt, axis=1)
    small_params = [n for n in names if n not in big]
    as2d = lambda a: a.reshape(-1, a.shape[-1])
    res = _adamw_small([as2d(W[n]) for n in small_params],
                       [summed[n].reshape(as2d(W[n]).shape) for n in small_params],
                       [as2d(Mo[n]) for n in small_params], [as2d(Vo[n]) for n in small_params])
    grads = {n: summed[n].reshape(W[n].shape) for n in small_params}
    delta, new_m, new_v = ({n: r.reshape(W[n].shape) for n, r in zip(small_params, rs)} for rs in res)

    for n, state in (('w_mlp_out', rs_mo), ('w_mlp_in', rs_mi), ('w_out', rs_out), ('w_in', rs_in)):
        p, r = rs_end(n, state, g_pm8)
        grads[n], delta[n], new_m[n], new_v[n] = _adamw_big("adamw_" + n, W[n], Mo[n], Vo[n], p, r, chip_idx)

    return (loss, grad_x.reshape(x.shape), *[grads[n] for n in names], *[delta[n] for n in names],
            *[new_m[n] for n in names], *[new_v[n] for n in names])
```

```python
import functools

import jax
import jax.numpy as jnp
from jax import lax
from jax.experimental import pallas as pl
from jax.experimental.pallas import tpu as pltpu

F32, BF16 = jnp.float32, jnp.bfloat16
S = jax.ShapeDtypeStruct
MESH = pl.DeviceIdType.MESH

SSD_GROUPS = 8
LRU_C = 8.0
EPS = 1e-6
CONV_WIDTH = 4
ADAM_LR, ADAM_B1, ADAM_B2, ADAM_EPS, ADAM_WD, ADAM_STEP = 0.001, 0.9, 0.999, 1e-08, 0.01, 10

LANES = 128
SUBLANES = 8
VMEM_LIMIT = 56 * 1024 * 1024
N_DEV = 8
SMALL_W = 512
HI = lax.Precision.HIGHEST


def _pcall(body, **kw):
    return pl.pallas_call(body, **kw)


def _cparams(sem=None, **kw):
    return pltpu.CompilerParams(dimension_semantics=sem, vmem_limit_bytes=VMEM_LIMIT, **kw)


def _pick(n, cands):
    for c in cands:
        if c <= n and n % c == 0:
            return c
    return n


def _rt(tr, w, cb=0, n=None):
    if n is None:
        return pl.BlockSpec((tr, w), lambda i: (i, cb))
    return pl.BlockSpec((tr, w), lambda i: (n - 1 - i, cb))


PACKED_ROWS = 16


def _halo_prev(tr, w, cb=0, n=None, rows=SUBLANES):
    k = tr // rows
    if n is None:
        return pl.BlockSpec((rows, w), lambda i: (jnp.maximum(i * k - 1, 0), cb))
    return pl.BlockSpec((rows, w), lambda i: (jnp.maximum((n - 1 - i) * k - 1, 0), cb))


def _halo_next(tr, w, nt, cb=0, n=None, rows=SUBLANES):
    k = tr // rows
    last = nt * k - 1
    if n is None:
        return pl.BlockSpec((rows, w), lambda i: (jnp.minimum((i + 1) * k, last), cb))
    return pl.BlockSpec((rows, w), lambda i: (jnp.minimum((n - i) * k, last), cb))


def _full(a):
    nd = a.ndim
    return pl.BlockSpec(a.shape, lambda i: (0,) * nd)


def _rows_call(name, fn, n_tiles, arrays, in_specs, out_tiled, out_acc, kinds, into=None, deps=(), cw=None):
    n_in, n_t = len(arrays), len(out_tiled)
    n_skip = len(deps) + (0 if into is None else 1)
    width = in_specs[kinds.index('t')].block_shape[1]
    cols = [(0, width)] if cw is None else [(c, cw) for c in range(0, width, cw)]

    def body(*refs):
        i = pl.program_id(0)
        ins = refs[:n_in]
        outs = refs[n_in + n_skip:n_in + n_skip + n_t]
        accs = refs[n_in + n_skip + n_t:]
        if accs:
            @pl.when(i == 0)
            def _():
                for r in accs:
                    r[...] = jnp.zeros_like(r)

        def lanes(ref, rows, c0, w):
            return ref[rows, c0:c0 + w] if ref.shape[-1] == width else ref[rows, :]

        def load(k, c0, w):
            v = lanes(ins[k], slice(None), c0, w).astype(F32)
            if kinds[k][0] in 'pn' and v.shape[0] == PACKED_ROWS:
                v = v[SUBLANES:] if kinds[k][0] == 'p' else v[:SUBLANES]
            return v

        for c0, w in cols:
            touts, aouts = fn(i == 0, i == n_tiles - 1, *[load(k, c0, w) for k in range(n_in)])
            for r, v in zip(outs, touts):
                if r.shape[-1] == width:
                    r[:, c0:c0 + w] = v.astype(r.dtype)
                else:
                    r[...] = v.astype(r.dtype)
            for r, v in zip(accs, aouts):
                if r.shape[-1] == width:
                    r[:, c0:c0 + w] += v
                else:
                    r[...] += v

    out_shape = [S(sh, dt) for sh, dt, _ in out_tiled] + [S(sh, F32) for sh in out_acc]
    out_specs = [sp for _, _, sp in out_tiled]
    for sh in out_acc:
        out_specs.append(pl.BlockSpec(sh, lambda i, nd=len(sh): (0,) * nd))
    in_specs = list(in_specs) + [_ANY] * len(deps)
    if into is None:
        return _pcall(body, name=name, grid=(n_tiles,), in_specs=in_specs, out_specs=out_specs,
                      out_shape=out_shape, compiler_params=_cparams(("arbitrary",)))(*arrays, *deps)
    return _pcall(body, name=name, grid=(n_tiles,), in_specs=in_specs + [_ANY], out_specs=out_specs,
                  out_shape=out_shape, input_output_aliases={n_in + len(deps): 0},
                  compiler_params=_cparams(("arbitrary",)))(*arrays, *deps, into)


def _rms(x, g):
    return x * lax.rsqrt(jnp.mean(x * x, axis=-1, keepdims=True) + EPS) * g


def _colsum(v):
    return jnp.sum(v, axis=0, keepdims=True)


_TILES = (1152, 1024, 896, 768, 640, 512, 384, 256, 128)
_K_TILES = (4096, 3456, 3072, 2688, 2048, 1536, 1344, 1152, 1024, 896, 768, 640, 512, 384, 256, 128)


def _mm(name, a, b, *, ta=False, tb=False, outs=((F32, None),), extra=None, out_blocks=None, tm=None, tn=None, tk=None,
        deps=(), a_cols=None):
    M, K = (a.shape[1], a.shape[0]) if ta else a.shape
    if a_cols is not None:
        assert not ta
        K = a_cols[1]
    b3 = b.ndim == 3
    if b3:
        nb_b, brows, bcols = b.shape
        N = brows if tb else nb_b * bcols
    else:
        N = b.shape[0] if tb else b.shape[1]
    n_lim = N if out_blocks is None else N // out_blocks
    if b3 and not tb:
        n_lim = min(n_lim, bcols)
    tm = tm or _pick(M, _TILES[1:])
    tn = tn or _pick(n_lim, _TILES)
    tk = tk or _pick(bcols if (b3 and tb) else K, _K_TILES)
    nk = K // tk
    assert M % tm == 0 and N % tn == 0 and K % tk == 0
    dn = (((0 if ta else 1,), (1 if tb else 0,)), ((), ()))
    n_extra = 0 if extra is None else 1
    n_out = len(outs)

    def body(*refs):
        a_ref, b_ref = refs[0], refs[1]
        e_ref = refs[2] if n_extra else None
        o_refs = refs[2 + n_extra + len(deps):2 + n_extra + len(deps) + n_out]

        def finish(r):
            e = e_ref[...] if n_extra else None
            for o, (_, f) in zip(o_refs, outs):
                o[...] = (r if f is None else f(r, e)).astype(o.dtype)

        part = lax.dot_general(a_ref[...], b_ref[...], dn, preferred_element_type=F32)
        if nk == 1:
            finish(part)
            return
        acc = refs[-1]
        k = pl.program_id(2)

        @pl.when(k == 0)
        def _():
            acc[...] = part

        @pl.when(jnp.logical_and(k > 0, k < nk - 1))
        def _():
            acc[...] += part

        @pl.when(k == nk - 1)
        def _():
            finish(acc[...] + part)

    k0 = 0 if a_cols is None else a_cols[0] // tk
    a_spec = (pl.BlockSpec((tk, tm), lambda i, j, k: (k, i)) if ta
              else pl.BlockSpec((tm, tk), lambda i, j, k: (i, k + k0)))
    if not b3:
        b_spec = pl.BlockSpec((tn, tk), lambda i, j, k: (j, k)) if tb else pl.BlockSpec((tk, tn), lambda i, j, k: (k, j))
    elif tb:
        per = bcols // tk
        b_spec = pl.BlockSpec((None, tn, tk), lambda i, j, k: (k // per, j, k % per))
    else:
        per = bcols // tn
        b_spec = pl.BlockSpec((None, tk, tn), lambda i, j, k: (j // per, k, j % per))
    o_spec = pl.BlockSpec((tm, tn), lambda i, j, k: (i, j))
    if out_blocks is None:
        out_specs, out_shape = [o_spec] * n_out, [S((M, N), dt) for dt, _ in outs]
    else:
        per_o = N // out_blocks // tn
        ob_spec = pl.BlockSpec((None, tm, tn), lambda i, j, k: (j // per_o, i, j % per_o))
        out_specs, out_shape = [ob_spec] * n_out, [S((out_blocks, M, N // out_blocks), dt) for dt, _ in outs]
    in_specs = [a_spec, b_spec] + ([o_spec] if n_extra else []) + [_ANY] * len(deps)
    args = [a, b] + ([extra] if n_extra else []) + list(deps)
    return _pcall(body, name=name, grid=(M // tm, N // tn, nk), in_specs=in_specs, out_specs=out_specs,
                  out_shape=out_shape, scratch_shapes=[pltpu.VMEM((tm, tn), F32)] if nk > 1 else [],
                  compiler_params=_cparams(("parallel", "parallel", "arbitrary")))(*args)


def _shift_down(x, halo, s):
    if s == 0:
        return x
    r = pltpu.roll(x, s, 0)
    hr = pltpu.roll(halo, s, 0)
    row = lax.broadcasted_iota(jnp.int32, halo.shape, 0)
    top = jnp.where(row < s, hr, r[:SUBLANES])
    if x.shape[0] == SUBLANES:
        return top
    return jnp.concatenate([top, r[SUBLANES:]], axis=0)


def _shift_up(x, nxt, s):
    if s == 0:
        return x
    n = x.shape[0]
    r = pltpu.roll(x, n - s, 0)
    nr = pltpu.roll(nxt, SUBLANES - s, 0)
    row = lax.broadcasted_iota(jnp.int32, nxt.shape, 0)
    bot = jnp.where(row >= SUBLANES - s, nr, r[n - SUBLANES:])
    if n == SUBLANES:
        return bot
    return jnp.concatenate([r[:n - SUBLANES], bot], axis=0)


def _conv_pre(x, halo, w, b):
    acc = b + w[CONV_WIDTH - 1:CONV_WIDTH, :] * x
    for k in range(CONV_WIDTH - 1):
        acc = acc + w[k:k + 1, :] * _shift_down(x, halo, CONV_WIDTH - 1 - k)
    return acc


def _silu_grad(p):
    s = jax.nn.sigmoid(p)
    return s * (1.0 + p * (1.0 - s))


def _conv_bwd_tile(first, last, x, hprev, xnext, d, dnext, w, b, silu):
    hprev = jnp.where(first, 0.0, hprev)
    if silu:
        d = d * _silu_grad(_conv_pre(x, hprev, w, b))
        pre_next = _conv_pre(xnext, x[x.shape[0] - SUBLANES:], w, b)
        dnext = dnext * _silu_grad(pre_next)
    dnext = jnp.where(last, 0.0, dnext)
    dx = w[CONV_WIDTH - 1:CONV_WIDTH, :] * d
    row8 = lax.broadcasted_iota(jnp.int32, (SUBLANES, x.shape[1]), 0)
    dw8 = jnp.where(row8 == CONV_WIDTH - 1, _colsum(d * x), 0.0)
    for k in range(CONV_WIDTH - 1):
        s = CONV_WIDTH - 1 - k
        dx = dx + w[k:k + 1, :] * _shift_up(d, dnext, s)
        dw8 = dw8 + jnp.where(row8 == k, _colsum(d * _shift_down(x, hprev, s)), 0.0)
    return dx, dw8, _colsum(d)


def _ssd_dims(xbc_act, n_heads):
    T, XBC = xbc_act.shape
    GN = XBC // 4
    DS = XBC - 2 * GN
    G = SSD_GROUPS
    N = GN // G
    P = DS // n_heads
    K = n_heads // G
    return T, XBC, DS, GN, G, N, P, K


def _ssd_common(dt, alog, Q):
    a = -jnp.exp(alog)
    adt = dt * a
    li = lax.broadcasted_iota(jnp.int32, (Q, Q), 0)
    si = lax.broadcasted_iota(jnp.int32, (Q, Q), 1)
    causal = li >= si
    ltri = causal.astype(F32)
    acs = jnp.dot(ltri, adt, precision=HI, preferred_element_type=F32)
    acs_row = lax.dot_general(adt, ltri, (((0,), (1,)), ((), ())), precision=HI,
                              preferred_element_type=F32)
    return a, adt, causal, ltri, acs, acs_row


def _expander(g, K, P, W):
    r = lax.broadcasted_iota(jnp.int32, (LANES, W), 0)
    c = lax.broadcasted_iota(jnp.int32, (LANES, W), 1)
    return (c // P + g * K == r).astype(F32)


def _dotb(a, b, dn=(((1,), (0,)), ((), ()))):
    return lax.dot_general(a.astype(BF16), b.astype(BF16), dn, preferred_element_type=F32)


def _dot_split(a, sel, terms, dn=(((1,), (0,)), ((), ()))):
    selb = sel.astype(BF16)
    out = None
    for _ in range(terms):
        piece = a.astype(BF16)
        part = lax.dot_general(piece, selb, dn, preferred_element_type=F32)
        out = part if out is None else out + part
        a = a - piece.astype(F32)
    return out


_NT = (((1,), (1,)), ((), ()))
_TN = (((0,), (0,)), ((), ()))


def _ssd_fwd(xbc_act, dt, alog, dskip, n_heads, Q):
    T, XBC, DS, GN, G, N, P, K = _ssd_dims(xbc_act, n_heads)
    W = K * P
    nc = T // Q

    def body(xs_ref, b_ref, c_ref, dt_ref, alog_ref, d_ref, y_ref, hp_ref, h_scr):
        ci = pl.program_id(0)

        @pl.when(ci == 0)
        def _():
            h_scr[...] = jnp.zeros_like(h_scr)

        dtv = dt_ref[...]
        a, adt, causal, ltri, acs, acs_row = _ssd_common(dtv, alog_ref[...], Q)
        lane_head = lax.broadcasted_iota(jnp.int32, (Q, W), 1) // P
        for g in range(G):
            eg = _expander(g, K, P, W)
            dtb = _dot_split(dtv, eg, 3)
            acsb = _dot_split(acs, eg, 3)
            lastb = acsb[Q - 1:Q, :]
            db = _dot_split(jnp.broadcast_to(d_ref[...], (SUBLANES, LANES)), eg, 3)[0:1, :]
            xg = xs_ref[:, g * W:(g + 1) * W]
            bg = b_ref[:, g * N:(g + 1) * N]
            cg = c_ref[:, g * N:(g + 1) * N]
            xt = xg * dtb
            sc = _dotb(cg, bg, _NT)
            yd = jnp.zeros((Q, W), F32)
            for k in range(K):
                h = g * K + k
                seg = acs[:, h:h + 1] - acs_row[h:h + 1, :]
                lh = jnp.where(causal, jnp.exp(jnp.minimum(seg, 0.0)), 0.0)
                xk = jnp.where(lane_head == k, xt, 0.0)
                yd = yd + _dotb(sc * lh, xk)
            hp = h_scr[g]
            yoff = _dotb(cg, hp) * jnp.exp(acsb)
            y_ref[:, g * W:(g + 1) * W] = yd + yoff + xg * db
            e_end = jnp.exp(lastb - acsb)
            st = _dotb(bg, xt * e_end, _TN)
            hp_ref[0, g] = hp
            h_scr[g] = jnp.exp(lastb) * hp + st

    cb = DS // GN
    in_specs = [pl.BlockSpec((Q, DS), lambda c: (c, 0)),
                pl.BlockSpec((Q, GN), lambda c: (c, cb)),
                pl.BlockSpec((Q, GN), lambda c: (c, cb + 1)),
                pl.BlockSpec((Q, LANES), lambda c: (c, 0)),
                pl.BlockSpec((1, LANES), lambda c: (0, 0)),
                pl.BlockSpec((1, LANES), lambda c: (0, 0))]
    out_specs = [pl.BlockSpec((Q, DS), lambda c: (c, 0)),
                 pl.BlockSpec((1, G, N, W), lambda c: (c, 0, 0, 0))]
    return _pcall(body, name="ssd_fwd", grid=(nc,), in_specs=in_specs, out_specs=out_specs,
                  out_shape=[S((T, DS), F32), S((nc, G, N, W), F32)],
                  scratch_shapes=[pltpu.VMEM((G, N, W), F32)],
                  compiler_params=_cparams(("arbitrary",)))(xbc_act, xbc_act, xbc_act, dt, alog, dskip)


def _ssd_bwd(xbc_act, dt, alog, dskip, hprev, dy, n_heads, Q):
    T, XBC, DS, GN, G, N, P, K = _ssd_dims(xbc_act, n_heads)
    W = K * P
    nc = T // Q

    def body(xs_ref, b_ref, c_ref, dt_ref, alog_ref, d_ref, hp_ref, dy_ref,
             dxbc_ref, ddt_ref, dalog_ref, dd_ref, dh_scr):
        ci = pl.program_id(0)

        @pl.when(ci == 0)
        def _():
            dh_scr[...] = jnp.zeros_like(dh_scr)
            dalog_ref[...] = jnp.zeros_like(dalog_ref)
            dd_ref[...] = jnp.zeros_like(dd_ref)

        dtv = dt_ref[...]
        a, adt, causal, ltri, acs, acs_row = _ssd_common(dtv, alog_ref[...], Q)
        lane_head = lax.broadcasted_iota(jnp.int32, (Q, W), 1) // P
        lane128 = lax.broadcasted_iota(jnp.int32, (Q, LANES), 1)
        sub128 = lax.broadcasted_iota(jnp.int32, (LANES, Q), 0)
        rowq = lax.broadcasted_iota(jnp.int32, (Q, W), 0)
        dacs = jnp.zeros((Q, LANES), F32)
        dacs_row = jnp.zeros((LANES, Q), F32)
        ddt = jnp.zeros((Q, LANES), F32)
        dd_acc = jnp.zeros((1, LANES), F32)
        for g in range(G):
            eg = _expander(g, K, P, W)
            dtb = _dot_split(dtv, eg, 3)
            acsb = _dot_split(acs, eg, 3)
            lastb = acsb[Q - 1:Q, :]
            db = _dot_split(jnp.broadcast_to(d_ref[...], (SUBLANES, LANES)), eg, 3)[0:1, :]
            xg = xs_ref[:, g * W:(g + 1) * W]
            bg = b_ref[:, g * N:(g + 1) * N]
            cg = c_ref[:, g * N:(g + 1) * N]
            dyg = dy_ref[:, g * W:(g + 1) * W]
            hp = hp_ref[0, g]
            dhn = dh_scr[g]
            xt = xg * dtb
            sc = _dotb(cg, bg, _NT)
            eacs = jnp.exp(acsb)
            e_end = jnp.exp(lastb - acsb)
            elast = jnp.exp(lastb)

            wv = dyg * eacs
            dcg = _dotb(wv, hp, _NT)
            dhp = _dotb(cg, wv, _TN) + elast * dhn
            dacsb = dyg * (_dotb(cg, hp) * eacs)

            xe = xt * e_end
            dbg = _dotb(xe, dhn, _NT)
            v = _dotb(bg, dhn)
            dxt = v * e_end
            de = v * xe
            dacsb = dacsb - de
            dlastb = _colsum(de) + elast * jnp.sum(dhn * hp, axis=0, keepdims=True)

            dsc = jnp.zeros((Q, Q), F32)
            for k in range(K):
                h = g * K + k
                seg = acs[:, h:h + 1] - acs_row[h:h + 1, :]
                lh = jnp.where(causal, jnp.exp(jnp.minimum(seg, 0.0)), 0.0)
                mh = sc * lh
                dyk = jnp.where(lane_head == k, dyg, 0.0)
                dxt = dxt + jnp.where(lane_head == k, _dotb(mh, dyg, _TN), 0.0)
                dm = _dotb(dyk, xt, _NT)
                dsc = dsc + dm * lh
                gm = dm * mh
                dacs = dacs + jnp.where(lane128 == h, jnp.sum(gm, axis=1, keepdims=True), 0.0)
                dacs_row = dacs_row - jnp.where(sub128 == h, jnp.sum(gm, axis=0, keepdims=True), 0.0)
            dcg = dcg + _dotb(dsc, bg)
            dbg = dbg + _dotb(dsc, cg, _TN)

            dacsb = dacsb + jnp.where(rowq == Q - 1, dlastb, 0.0)
            dacs = dacs + _dot_split(dacsb, eg, 2, _NT)
            ddt = ddt + _dot_split(dxt * xg, eg, 2, _NT)
            dd_acc = dd_acc + _dot_split(jnp.broadcast_to(_colsum(dyg * xg), (SUBLANES, W)), eg, 2, _NT)[0:1, :]
            dxbc_ref[:, g * W:(g + 1) * W] = dxt * dtb + dyg * db
            dxbc_ref[:, DS + g * N:DS + (g + 1) * N] = dbg
            dxbc_ref[:, DS + GN + g * N:DS + GN + (g + 1) * N] = dcg
            dh_scr[g] = dhp

        eye = (lax.broadcasted_iota(jnp.int32, (LANES, LANES), 0) ==
               lax.broadcasted_iota(jnp.int32, (LANES, LANES), 1)).astype(F32)
        dacs = dacs + lax.dot_general(dacs_row, eye, _TN, precision=HI, preferred_element_type=F32)
        dadt = lax.dot_general(ltri, dacs, _TN, precision=HI, preferred_element_type=F32)
        ddt_ref[...] = ddt + dadt * a
        dalog_ref[...] += _colsum(dadt * dtv) * a
        dd_ref[...] += dd_acc

    cb = DS // GN
    rv = lambda c: nc - 1 - c
    in_specs = [pl.BlockSpec((Q, DS), lambda c: (rv(c), 0)),
                pl.BlockSpec((Q, GN), lambda c: (rv(c), cb)),
                pl.BlockSpec((Q, GN), lambda c: (rv(c), cb + 1)),
                pl.BlockSpec((Q, LANES), lambda c: (rv(c), 0)),
                pl.BlockSpec((1, LANES), lambda c: (0, 0)),
                pl.BlockSpec((1, LANES), lambda c: (0, 0)),
                pl.BlockSpec((1, G, N, W), lambda c: (rv(c), 0, 0, 0)),
                pl.BlockSpec((Q, DS), lambda c: (rv(c), 0))]
    out_specs = [pl.BlockSpec((Q, XBC), lambda c: (rv(c), 0)),
                 pl.BlockSpec((Q, LANES), lambda c: (rv(c), 0)),
                 pl.BlockSpec((1, LANES), lambda c: (0, 0)),
                 pl.BlockSpec((1, LANES), lambda c: (0, 0))]
    return _pcall(body, name="ssd_bwd", grid=(nc,), in_specs=in_specs, out_specs=out_specs,
                  out_shape=[S((T, XBC), F32), S((T, LANES), F32), S((1, LANES), F32), S((1, LANES), F32)],
                  scratch_shapes=[pltpu.VMEM((G, N, W), F32)],
                  compiler_params=_cparams(("arbitrary",)))(
                      xbc_act, xbc_act, xbc_act, dt, alog, dskip, hprev, dy)


def _blockdiag(x, w_ref, dn=(((1,), (0,)), ((), ()))):
    H, B, _ = w_ref.shape
    return jnp.concatenate([_dotb(x[:, h * B:(h + 1) * B], w_ref[h], dn) for h in range(H)], axis=1)


def _lru_elem(xl, r_pre, i_pre, lam):
    r = jax.nn.sigmoid(r_pre)
    i = jax.nn.sigmoid(i_pre)
    log_a = -LRU_C * r * jax.nn.softplus(-lam)
    a = jnp.exp(log_a)
    u = jnp.sqrt(1.0 - jnp.exp(2.0 * log_a)) * (i * xl)
    return a, u


def _lru_gates_fwd(xl, w_a, b_a, w_x, b_x, lam, tr):
    T, DL = xl.shape

    def body(xl_ref, wa_ref, ba_ref, wx_ref, bx_ref, lam_ref, a_ref, u_ref):
        x = xl_ref[...]
        r_pre = _blockdiag(x, wa_ref) + ba_ref[...]
        i_pre = _blockdiag(x, wx_ref) + bx_ref[...]
        a, u = _lru_elem(x, r_pre, i_pre, lam_ref[...])
        a_ref[...] = a
        u_ref[...] = u

    w3 = pl.BlockSpec(w_a.shape, lambda i: (0, 0, 0))
    vec = pl.BlockSpec((1, DL), lambda i: (0, 0))
    return _pcall(body, name="lru_gates_fwd", grid=(T // tr,),
                  in_specs=[_rt(tr, DL), w3, vec, w3, vec, vec],
                  out_specs=[_rt(tr, DL), _rt(tr, DL)], out_shape=[S((T, DL), F32), S((T, DL), F32)],
                  compiler_params=_cparams(("parallel",)))(xl, w_a, b_a, w_x, b_x, lam)


def _lru_gates_bwd(xl, w_a, b_a, w_x, b_x, lam, da, du, tr):
    T, DL = xl.shape
    H, B, _ = w_a.shape

    def body(xl_ref, wa_ref, ba_ref, wx_ref, bx_ref, lam_ref, da_ref, du_ref,
             dxl_ref, dwa_ref, dba_ref, dwx_ref, dbx_ref, dlam_ref):
        @pl.when(pl.program_id(0) == 0)
        def _():
            for r in (dwa_ref, dba_ref, dwx_ref, dbx_ref, dlam_ref):
                r[...] = jnp.zeros_like(r)

        x = xl_ref[...]
        r_pre = _blockdiag(x, wa_ref) + ba_ref[...]
        i_pre = _blockdiag(x, wx_ref) + bx_ref[...]
        _, vjp = jax.vjp(_lru_elem, x, r_pre, i_pre, lam_ref[...])
        dx, dr, di, dlam = vjp((da_ref[...], du_ref[...]))
        dxl_ref[...] = dx + _blockdiag(dr, wa_ref, _NT) + _blockdiag(di, wx_ref, _NT)
        for h in range(H):
            xh = x[:, h * B:(h + 1) * B]
            dwa_ref[h] += _dotb(xh, dr[:, h * B:(h + 1) * B], _TN)
            dwx_ref[h] += _dotb(xh, di[:, h * B:(h + 1) * B], _TN)
        dba_ref[...] += _colsum(dr)
        dbx_ref[...] += _colsum(di)
        dlam_ref[...] += dlam

    w3 = pl.BlockSpec(w_a.shape, lambda i: (0, 0, 0))
    vec = pl.BlockSpec((1, DL), lambda i: (0, 0))
    return _pcall(body, name="lru_gates_bwd", grid=(T // tr,),
                  in_specs=[_rt(tr, DL), w3, vec, w3, vec, vec, _rt(tr, DL), _rt(tr, DL)],
                  out_specs=[_rt(tr, DL), w3, vec, w3, vec, vec],
                  out_shape=[S((T, DL), F32), S(w_a.shape, F32), S((1, DL), F32), S(w_a.shape, F32),
                             S((1, DL), F32), S((1, DL), F32)],
                  compiler_params=_cparams(("arbitrary",)))(xl, w_a, b_a, w_x, b_x, lam, da, du)


def _groups(v):
    return v.reshape(v.shape[0] // SUBLANES, SUBLANES, v.shape[1])


def _rows_shifted(v, edge, up):
    sub = lax.broadcasted_iota(jnp.int32, v.shape, 1)
    if up:
        other = jnp.concatenate([v[1:], edge[None]], axis=0)
        return jnp.where(sub < SUBLANES - 1, pltpu.roll(v, SUBLANES - 1, 1), pltpu.roll(other, SUBLANES - 1, 1))
    other = jnp.concatenate([edge[None], v[:-1]], axis=0)
    return jnp.where(sub >= 1, pltpu.roll(v, 1, 1), pltpu.roll(other, 1, 1))


def _scan_tile(a, u, entering, emit, up):
    G = a.shape[0]
    sub = lax.broadcasted_iota(jnp.int32, a.shape, 1)
    d = 1
    while d < SUBLANES:
        if up:
            keep = sub < SUBLANES - d
            a_s = jnp.where(keep, pltpu.roll(a, SUBLANES - d, 1), 1.0)
            u_s = jnp.where(keep, pltpu.roll(u, SUBLANES - d, 1), 0.0)
        else:
            keep = sub >= d
            a_s = jnp.where(keep, pltpu.roll(a, d, 1), 1.0)
            u_s = jnp.where(keep, pltpu.roll(u, d, 1), 0.0)
        u = a * u_s + u
        a = a * a_s
        d *= 2
    for g in (reversed(range(G)) if up else range(G)):
        hg = u[g] + a[g] * entering
        emit(g, hg)
        entering = hg[0:1] if up else hg[SUBLANES - 1:SUBLANES]
    return entering


def _lru_scan_fwd(a, u, tr):
    T, DL = a.shape

    def body(a_ref, u_ref, h_ref, carry):
        @pl.when(pl.program_id(0) == 0)
        def _():
            carry[...] = jnp.zeros_like(carry)

        def emit(g, hg):
            h_ref[g * SUBLANES:(g + 1) * SUBLANES, :] = hg

        last = _scan_tile(_groups(a_ref[...]), _groups(u_ref[...]), carry[0:1, :], emit, up=False)
        carry[...] = jnp.broadcast_to(last, carry.shape)

    return _pcall(body, name="lru_scan_fwd", grid=(T // tr,), in_specs=[_rt(tr, DL), _rt(tr, DL)],
                  out_specs=_rt(tr, DL), out_shape=S((T, DL), F32),
                  scratch_shapes=[pltpu.VMEM((SUBLANES, DL), F32)],
                  compiler_params=_cparams(("arbitrary",)))(a, u)


def _lru_scan_bwd(a, h, dh, tr):
    T, DL = a.shape
    n = T // tr

    def body(a_ref, an_ref, h_ref, hp_ref, dh_ref, du_ref, da_ref, carry):
        i = pl.program_id(0)
        ti = n - 1 - i

        @pl.when(i == 0)
        def _():
            carry[...] = jnp.zeros_like(carry)

        a_next = _rows_shifted(_groups(a_ref[...]), jnp.where(ti == n - 1, 0.0, an_ref[...]), up=True)
        h_prev = _rows_shifted(_groups(h_ref[...]), jnp.where(ti == 0, 0.0, hp_ref[...]), up=False)

        def emit(g, gg):
            du_ref[g * SUBLANES:(g + 1) * SUBLANES, :] = gg
            da_ref[g * SUBLANES:(g + 1) * SUBLANES, :] = gg * h_prev[g]

        top = _scan_tile(a_next, _groups(dh_ref[...]), carry[0:1, :], emit, up=True)
        carry[...] = jnp.broadcast_to(top, carry.shape)

    return _pcall(body, name="lru_scan_bwd", grid=(n,),
                  in_specs=[_rt(tr, DL, 0, n), _halo_next(tr, DL, n, 0, n), _rt(tr, DL, 0, n),
                            _halo_prev(tr, DL, 0, n), _rt(tr, DL, 0, n)],
                  out_specs=[_rt(tr, DL, 0, n), _rt(tr, DL, 0, n)],
                  out_shape=[S((T, DL), F32), S((T, DL), F32)],
                  scratch_shapes=[pltpu.VMEM((SUBLANES, DL), F32)],
                  compiler_params=_cparams(("arbitrary",)))(a, a, h, h, dh)


def _adamw(w, g, m, v):
    m = ADAM_B1 * m + (1.0 - ADAM_B1) * g
    v = ADAM_B2 * v + (1.0 - ADAM_B2) * (g * g)
    m_hat = m / (1.0 - ADAM_B1 ** ADAM_STEP)
    v_hat = v / (1.0 - ADAM_B2 ** ADAM_STEP)
    delta = -ADAM_LR * (m_hat / (jnp.sqrt(v_hat) + ADAM_EPS) + ADAM_WD * w)
    return delta, m, v


def _adamw_big(name, w, m, v, part, recv, chip_idx):
    _, R, C = w.shape
    tr = _pick(R, (256, 128, 64, 32, 16))

    def body(ci_ref, w_ref, m_ref, v_ref, p_ref, r0, r1, r2, g_ref, d_ref, nm_ref, nv_ref):
        g = ((p_ref[...].astype(F32) + r0[...].astype(F32)) + r1[...].astype(F32)) + r2[...].astype(F32)
        d, nm, nv = _adamw(w_ref[...], g, m_ref[...], v_ref[...])
        g_ref[...] = g
        d_ref[...] = d
        nm_ref[...] = nm
        nv_ref[...] = nv

    r_spec = lambda s: pl.BlockSpec((None, tr, C), lambda i, ci: (s, i, 0))
    t2 = r_spec(0)
    gs = pltpu.PrefetchScalarGridSpec(
        num_scalar_prefetch=1, grid=(R // tr,),
        in_specs=[t2, t2, t2, pl.BlockSpec((None, tr, C), lambda i, ci: (ci[0], i, 0)),
                  r_spec(0), r_spec(1), r_spec(2)],
        out_specs=[t2, t2, t2, t2])
    return _pcall(body, name=name, grid_spec=gs, out_shape=[S((1, R, C), F32)] * 4,
                  compiler_params=_cparams(("parallel",)))(chip_idx, w, m, v, part, recv, recv, recv)


def _adamw_small(ws, gs, ms, vs):
    n = len(ws)

    def body(*refs):
        for k in range(n):
            d, nm, nv = _adamw(refs[k][...], refs[n + k][...], refs[2 * n + k][...], refs[3 * n + k][...])
            refs[4 * n + k][...] = d
            refs[5 * n + k][...] = nm
            refs[6 * n + k][...] = nv

    res = _pcall(body, name="adamw_small", out_shape=[S(w.shape, F32) for w in ws] * 3,
                 compiler_params=_cparams())(*ws, *gs, *ms, *vs)
    return res[:n], res[n:2 * n], res[2 * n:]


def _sum8(name, parts):
    _, R, C = parts.shape

    def body(p_ref, o_ref):
        acc = p_ref[0]
        for k in range(1, N_DEV):
            acc = acc + p_ref[k]
        o_ref[...] = acc

    return _pcall(body, name=name, out_shape=S((R, C), F32), compiler_params=_cparams())(parts)


def _pair_sum(name, full, recv, c_idx):
    _, R, C = full.shape
    tr = _pick(R, (256, 128, 64, 32, 16))

    def body(c_ref, f_ref, r_ref, o_ref):
        o_ref[...] = (f_ref[...].astype(F32) + r_ref[...].astype(F32)).astype(o_ref.dtype)

    gs = pltpu.PrefetchScalarGridSpec(
        num_scalar_prefetch=1, grid=(4, R // tr),
        in_specs=[pl.BlockSpec((None, tr, C), lambda j, i, c: (2 * j + c[0], i, 0)),
                  pl.BlockSpec((None, tr, C), lambda j, i, c: (j, i, 0))],
        out_specs=pl.BlockSpec((None, tr, C), lambda j, i, c: (j, i, 0)))
    return _pcall(body, name=name, grid_spec=gs, out_shape=S((4, R, C), BF16),
                  compiler_params=_cparams(("parallel", "parallel")))(c_idx, full, recv)


def _cast_bf16(name, w, dev_idx, row0=0, rows=None):
    C = w.shape[2]
    R = w.shape[1] if rows is None else rows
    tr = _pick(R, (256, 128, 64, 32, 16))
    b0 = row0 // tr

    def body(d_ref, w_ref, o_ref):
        o_ref[...] = w_ref[...].astype(BF16)

    gs = pltpu.PrefetchScalarGridSpec(
        num_scalar_prefetch=1, grid=(R // tr,),
        in_specs=[pl.BlockSpec((None, tr, C), lambda i, d: (0, i + b0, 0))],
        out_specs=pl.BlockSpec((None, tr, C), lambda i, d: (d[0], i, 0)))
    return _pcall(body, name=name, grid_spec=gs, out_shape=S((N_DEV, R, C), BF16),
                  compiler_params=_cparams(("parallel",)))(dev_idx, w)


def _own_block(v, dev):
    return lax.dynamic_update_slice(lax.empty((N_DEV,) + v.shape, v.dtype), v[None], (dev,) + (0,) * v.ndim)


_ANY = pl.BlockSpec(memory_space=pl.ANY)


def _position():
    return lax.axis_index("x"), lax.axis_index("y"), lax.axis_index("c")


def _allgather(name, bufs, deps=()):
    n = len(bufs)
    nd = len(deps)

    def body(*refs):
        outs = refs[n + nd:2 * n + nd]
        send, recv = refs[2 * n + nd:]
        x, y, c = _position()
        me, sib = (x, y, c), (x, y, 1 - c)
        chips = [(1 - x, y), (x, 1 - y), (1 - x, 1 - y)]

        def copy(a, k, block, to):
            bx, by, bc = block
            blk = outs[a].at[4 * bx + 2 * by + bc]
            return pltpu.make_async_remote_copy(
                src_ref=blk, dst_ref=blk, send_sem=send.at[a, k], recv_sem=recv.at[a, k],
                device_id=to, device_id_type=MESH)

        first = []
        for a in range(n):
            first.append(copy(a, 0, me, sib))
            first += [copy(a, 1 + j, me, (*chip, c)) for j, chip in enumerate(chips)]
        for cp in first:
            cp.start()
        passed = []
        for j, chip in enumerate(chips):
            for a in range(n):
                copy(a, 1 + j, (*chip, c), me).wait_recv()
                cp = copy(a, 4 + j, (*chip, c), sib)
                cp.start()
                passed.append(cp)
        for a in range(n):
            copy(a, 0, sib, me).wait_recv()
        for j, chip in enumerate(chips):
            for a in range(n):
                copy(a, 4 + j, (*chip, 1 - c), me).wait_recv()
        for cp in first + passed:
            cp.wait_send()

    return _pcall(body, name=name, in_specs=[_ANY] * (n + nd), out_specs=[_ANY] * n,
                  out_shape=[S(b.shape, b.dtype) for b in bufs], input_output_aliases={a: a for a in range(n)},
                  scratch_shapes=[pltpu.SemaphoreType.DMA((n, 7)), pltpu.SemaphoreType.DMA((n, 7))])(*bufs, *deps)


def _rs_sibling(name, fulls):
    n = len(fulls)

    def body(*refs):
        ins, outs = refs[:n], refs[n:2 * n]
        send, recv = refs[2 * n:]
        x, y, c = _position()
        copies = []
        for a in range(n):
            for j in range(4):
                copies.append(pltpu.make_async_remote_copy(
                    src_ref=ins[a].at[2 * j + (1 - c)], dst_ref=outs[a].at[j], send_sem=send.at[a, j],
                    recv_sem=recv.at[a, j], device_id=(x, y, 1 - c), device_id_type=MESH))
        for cp in copies:
            cp.start()
        for cp in copies:
            cp.wait()

    return _pcall(body, name=name, in_specs=[_ANY] * n, out_specs=[_ANY] * n,
                  out_shape=[S((4,) + f.shape[1:], f.dtype) for f in fulls],
                  scratch_shapes=[pltpu.SemaphoreType.DMA((n, 4)), pltpu.SemaphoreType.DMA((n, 4))])(*fulls)


_HBM = pl.BlockSpec(memory_space=pltpu.HBM)
_SEM = pl.BlockSpec(memory_space=pltpu.SEMAPHORE)
_EFFECT = pltpu.SideEffectType.DATAFLOW_SIDE_EFFECTING


def _remote_copies(copies_fn, srcs, lands, send, recv):
    x, y, c = _position()
    return [pltpu.make_async_remote_copy(src_ref=s, dst_ref=d, send_sem=send[i], recv_sem=recv[i], device_id=to,
                                         device_id_type=MESH)
            for i, (s, d, to) in enumerate(copies_fn(x, y, c, srcs, lands))]


def _split_start(name, srcs, lands, copies_fn, nc, after=()):
    n, nl, na = len(srcs), len(lands), len(after)

    def body(*refs):
        src_refs, land_refs = refs[:n], refs[n:n + nl]
        outs = refs[n + nl + na:]
        for cp in _remote_copies(copies_fn, src_refs, land_refs, outs[:nc], outs[nc:2 * nc]):
            cp.start()
        outs[-1][...] = jnp.zeros_like(outs[-1])

    hbm = lambda a: pltpu.with_memory_space_constraint(a, pltpu.HBM)
    res = _pcall(
        body, name=name, in_specs=[_HBM] * (n + nl) + [_ANY] * na,
        out_specs=[_SEM] * (2 * nc) + [_HBM] * (n + nl) + [pl.BlockSpec(memory_space=pltpu.VMEM)],
        out_shape=[pltpu.SemaphoreType.DMA(())] * (2 * nc) + [pltpu.HBM(s.shape, s.dtype) for s in srcs]
        + [pltpu.HBM(l.shape, l.dtype) for l in lands] + [S((SUBLANES, LANES), F32)],
        input_output_aliases={i: 2 * nc + i for i in range(n + nl)},
        compiler_params=pltpu.CompilerParams(has_side_effects=_EFFECT),
    )(*[hbm(s) for s in srcs], *[hbm(l) for l in lands], *after)
    return res[:2 * nc], res[2 * nc:2 * nc + n], res[2 * nc + n:2 * nc + n + nl], res[-1]


def _split_wait(name, sems, srcs, lands, after, copies_fn, nc):
    n, nl = len(srcs), len(lands)

    def body(*refs):
        src_refs, land_refs = refs[:n], refs[n:n + nl]
        sem_refs = refs[n + nl:n + nl + 2 * nc]
        for cp in _remote_copies(copies_fn, src_refs, land_refs, sem_refs[:nc], sem_refs[nc:]):
            cp.wait_send()
            cp.wait_recv()

    res = _pcall(
        body, name=name, in_specs=[_HBM] * (n + nl) + [_SEM] * (2 * nc) + [_ANY],
        out_specs=[_HBM] * (n + nl), out_shape=[pltpu.HBM(a.shape, a.dtype) for a in list(srcs) + list(lands)],
        input_output_aliases={i: i for i in range(n + nl)},
        compiler_params=pltpu.CompilerParams(has_side_effects=_EFFECT),
    )(*srcs, *lands, *sems, after)
    return res[:n], res[n:]


def _other_chips(x, y):
    return [(1 - x, y), (x, 1 - y), (1 - x, 1 - y)]


def _ag_copies(x, y, c, srcs, lands):
    out = []
    for land in lands:
        blk = land.at[4 * x + 2 * y + c]
        out.append((blk, blk, (x, y, 1 - c)))
        out += [(blk, blk, (px, py, c)) for px, py in _other_chips(x, y)]
    return out


def _rs_copies(x, y, c, srcs, lands):
    return [(s.at[2 * px + py], land.at[j], (px, py, c))
            for s, land in zip(srcs, lands) for j, (px, py) in enumerate(_other_chips(x, y))]


def _sib_copies(x, y, c, srcs, lands):
    return [(s.at[2 * j + (1 - c)], land.at[j], (x, y, 1 - c)) for s, land in zip(srcs, lands) for j in range(4)]


def _ag_finish(name, lands):
    n = len(lands)

    def body(*refs):
        outs = refs[n:2 * n]
        send, recv = refs[2 * n:]
        x, y, c = _position()

        def swap(a, j, px, py, pc):
            blk = outs[a].at[4 * px + 2 * py + pc]
            return pltpu.make_async_remote_copy(src_ref=blk, dst_ref=blk, send_sem=send.at[a, j], recv_sem=recv.at[a, j],
                                                device_id=(x, y, 1 - c), device_id_type=MESH)

        chips = _other_chips(x, y)
        sends = [swap(a, j, px, py, c) for a in range(n) for j, (px, py) in enumerate(chips)]
        for cp in sends:
            cp.start()
        for a in range(n):
            for j, (px, py) in enumerate(chips):
                swap(a, j, px, py, 1 - c).wait_recv()
        for cp in sends:
            cp.wait_send()

    return _pcall(body, name=name, in_specs=[_ANY] * n, out_specs=[_ANY] * n,
                  out_shape=[S(l.shape, l.dtype) for l in lands], input_output_aliases={a: a for a in range(n)},
                  scratch_shapes=[pltpu.SemaphoreType.DMA((n, 3)), pltpu.SemaphoreType.DMA((n, 3))])(*lands)


def _pad_lanes(v):
    return jnp.pad(v, ((0, 0), (0, LANES - v.shape[1])))


def _flat_rows(pieces):
    flat = jnp.concatenate([p.reshape(-1) for p in pieces])
    rows = -(-flat.shape[0] // (SMALL_W * SUBLANES)) * SUBLANES
    return jnp.pad(flat, (0, rows * SMALL_W - flat.shape[0])).reshape(rows, SMALL_W)


def _unflat(buf, shapes):
    flat = buf.reshape(-1)
    out, off = [], 0
    for sh in shapes:
        n = 1
        for d in sh:
            n *= d
        out.append(flat[off:off + n].reshape(sh))
        off += n
    return out


def kernel(x, pre_mix_norm, w_in, ssd_conv_w, ssd_conv_b, ssd_dt_bias, ssd_a_log, ssd_d, ssd_norm, lru_conv_w, lru_conv_b, lru_w_a, lru_b_a, lru_w_x, lru_b_x, lru_lambda, lru_norm, w_out, post_mix_norm, pre_mlp_norm, w_mlp_in, w_mlp_out, post_mlp_norm, loss_target, m_pre_mix_norm, m_w_in, m_ssd_conv_w, m_ssd_conv_b, m_ssd_dt_bias, m_ssd_a_log, m_ssd_d, m_ssd_norm, m_lru_conv_w, m_lru_conv_b, m_lru_w_a, m_lru_b_a, m_lru_w_x, m_lru_b_x, m_lru_lambda, m_lru_norm, m_w_out, m_post_mix_norm, m_pre_mlp_norm, m_w_mlp_in, m_w_mlp_out, m_post_mlp_norm, v_pre_mix_norm, v_w_in, v_ssd_conv_w, v_ssd_conv_b, v_ssd_dt_bias, v_ssd_a_log, v_ssd_d, v_ssd_norm, v_lru_conv_w, v_lru_conv_b, v_lru_w_a, v_lru_b_a, v_lru_w_x, v_lru_b_x, v_lru_lambda, v_lru_norm, v_w_out, v_post_mix_norm, v_pre_mlp_norm, v_w_mlp_in, v_w_mlp_out, v_post_mlp_norm):
    names = ['pre_mix_norm', 'w_in', 'ssd_conv_w', 'ssd_conv_b', 'ssd_dt_bias', 'ssd_a_log', 'ssd_d', 'ssd_norm',
             'lru_conv_w', 'lru_conv_b', 'lru_w_a', 'lru_b_a', 'lru_w_x', 'lru_b_x', 'lru_lambda', 'lru_norm',
             'w_out', 'post_mix_norm', 'pre_mlp_norm', 'w_mlp_in', 'w_mlp_out', 'post_mlp_norm']
    loc = locals()
    W = {n: loc[n] for n in names}
    Mo = {n: loc["m_" + n] for n in names}
    Vo = {n: loc["v_" + n] for n in names}
    big = ['w_in', 'w_out', 'w_mlp_in', 'w_mlp_out']

    px, py, pc = _position()
    dev = 4 * px + 2 * py + pc
    dev_idx = jnp.reshape(dev, (1,)).astype(jnp.int32)
    c_idx = jnp.reshape(pc, (1,)).astype(jnp.int32)
    chip_idx = jnp.reshape(2 * px + py, (1,)).astype(jnp.int32)

    _, T, D = x.shape
    x2 = x.reshape(T, D)
    tgt = loss_target.reshape(T, D)
    n_heads = ssd_dt_bias.shape[1]
    XBC = ssd_conv_b.shape[1]
    GN = XBC // 4
    DS = XBC - 2 * GN
    DL = lru_norm.shape[1]
    DFF = w_mlp_in.shape[2] * N_DEV
    DIN = w_in.shape[2] * N_DEV
    NP = XBC + DS + 2 * DL + LANES
    assert DS % GN == 0 and XBC % DS == 0 and DS == DL and n_heads <= LANES
    cb_z, cb_gate, cb_xl, cb_dt = XBC // DS, XBC // DS + 1, XBC // DS + 2, (XBC + DS + 2 * DL) // LANES
    tr = min(256, T // 2)
    nt = T // tr
    Q = min(256, T // 2)

    Dh = D // 2
    sh_a = _cast_bf16("cast_w_in_a", W['w_in'], dev_idx, 0, Dh)
    sh_b = _cast_bf16("cast_w_in_b", W['w_in'], dev_idx, Dh, Dh)
    later = big[1:]
    sh = {n: _cast_bf16("cast_" + n, W[n], dev_idx) for n in later}
    g_in_a, g_cs, g_cl = _allgather(
        "allgather_w_in_a", [sh_a, _own_block(ssd_conv_w[0], dev), _own_block(lru_conv_w[0], dev)])
    b_sems, _, b_lands, b_token = _split_start("allgather_w_in_b_start", [], [sh_b], _ag_copies, 4, after=[g_in_a])
    ag_sems, ag_srcs, ag_lands, ag_token = _split_start(
        "allgather_later_start", [], [sh[n] for n in later], _ag_copies, 4 * len(later), after=[b_token])
    conv_s = jnp.transpose(g_cs, (1, 0, 2)).reshape(CONV_WIDTH, XBC)
    conv_l = jnp.transpose(g_cl, (1, 0, 2)).reshape(CONV_WIDTH, DL)
    wb = DIN // N_DEV
    o_z, o_xbc, o_dt, o_gate, o_xl = 0, DS, DS + XBC, DS + XBC + n_heads, DS + XBC + n_heads + DL
    segs = [(o_xbc, o_xbc + XBC, 0), (o_z, o_z + DS, XBC), (o_gate, o_gate + DL, XBC + DS),
            (o_xl, o_xl + DL, XBC + DS + DL), (o_dt, o_dt + n_heads, NP - LANES)]

    def ref_cols(g, lo, hi):
        out = []
        while lo < hi:
            k = lo // wb
            e = min(hi, (k + 1) * wb)
            out.append(g[k, :, lo - k * wb:e - k * wb])
            lo = e
        return out

    def laid_out(g):
        return jnp.concatenate([p for a, b, _ in segs for p in ref_cols(g, a, b)]
                               + [jnp.zeros((g.shape[1], LANES - n_heads), BF16)], axis=1)

    def my_cols(g, lo, hi):
        out = []
        for a, b, m in sorted(segs):
            s, e = max(lo, a), min(hi, b)
            if s < e:
                out.append(g[:, m + s - a:m + e - a])
        return out

    wp_a = laid_out(g_in_a)
    dt_bias = _pad_lanes(ssd_dt_bias)
    a_log = _pad_lanes(ssd_a_log)
    d_skip = _pad_lanes(ssd_d)
    wa_b, wx_b = lru_w_a[0].astype(BF16), lru_w_x[0].astype(BF16)
    b_a, b_x = lru_b_a.reshape(1, DL), lru_b_x.reshape(1, DL)

    def f_norm_in(first, last, xv, g):
        return (_rms(xv, g),), ()
    (h,) = _rows_call("norm_in", f_norm_in, nt, [x2, pre_mix_norm], [_rt(tr, D), _full(pre_mix_norm)],
                      [((T, D), BF16, _rt(tr, D))], [], 'tf', deps=[ag_token])

    (proj_a,) = _mm("proj_a", h, wp_a, a_cols=(0, Dh))
    (dt_a,) = _mm("proj_dt_a", h, wp_a[:, NP - LANES:], a_cols=(0, Dh))
    _, b_lands = _split_wait("allgather_w_in_b_wait", b_sems, [], b_lands, proj_a, _ag_copies, 4)
    (g_in_b,) = _ag_finish("allgather_w_in_b_finish", b_lands)
    wp_b = laid_out(g_in_b)
    add = lambda r, e: r + e
    (proj,) = _mm("proj_b", h, wp_b, a_cols=(Dh, Dh), extra=proj_a, outs=((BF16, add),))
    (dt_raw,) = _mm("proj_dt_b", h, wp_b[:, NP - LANES:], a_cols=(Dh, Dh), extra=dt_a, outs=((F32, add),))

    cwx = min(1024, XBC)

    def f_ssd_pre(first, last, xbc, halo, w, b):
        pre = _conv_pre(xbc, jnp.where(first, 0.0, halo), w, b)
        return (pre * jax.nn.sigmoid(pre),), ()
    (xbc_act,) = _rows_call(
        "ssd_pre", f_ssd_pre, nt, [proj, proj, conv_s, ssd_conv_b],
        [_rt(tr, XBC), _halo_prev(tr, XBC, rows=PACKED_ROWS), _full(conv_s), _full(ssd_conv_b)],
        [((T, XBC), F32, _rt(tr, XBC))], [], ['t', 'p0', 'f', 'f'], cw=cwx)

    def f_ssd_dt(first, last, dtr, dtb):
        return (jax.nn.softplus(dtr + dtb),), ()
    (dt,) = _rows_call("ssd_dt", f_ssd_dt, nt, [dt_raw, dt_bias], [_rt(tr, LANES), _full(dt_bias)],
                       [((T, LANES), F32, _rt(tr, LANES))], [], 'tf')

    y_ssd, h_prev = _ssd_fwd(xbc_act, dt, a_log, d_skip, n_heads, Q)

    gw = DS // SSD_GROUPS

    def ssd_post(y, z, g):
        yz = y * jax.nn.silu(z)
        parts = []
        for k in range(SSD_GROUPS):
            yk = yz[:, k * gw:(k + 1) * gw]
            parts.append(yk * lax.rsqrt(jnp.mean(yk * yk, axis=-1, keepdims=True) + EPS))
        return jnp.concatenate(parts, axis=-1) * g

    def f_ssd_post(first, last, y, z, g):
        return (ssd_post(y, z, g),), ()
    (mixcat,) = _rows_call("ssd_post", f_ssd_post, nt, [y_ssd, proj, ssd_norm],
                           [_rt(tr, DS), _rt(tr, DS, cb_z), _full(ssd_norm)], [((T, DS + DL), BF16, _rt(tr, DS))], [],
                           'ttf')

    def f_lru_pre(first, last, xv, halo, w, b):
        return (_conv_pre(xv, jnp.where(first, 0.0, halo), w, b),), ()
    (xl,) = _rows_call("lru_pre", f_lru_pre, nt, [proj, proj, conv_l, lru_conv_b],
                       [_rt(tr, DL, cb_xl), _halo_prev(tr, DL, cb_xl, rows=PACKED_ROWS), _full(conv_l),
                        _full(lru_conv_b)],
                       [((T, DL), F32, _rt(tr, DL))], [], ['t', 'p0', 'f', 'f'])

    a_lru, u_lru = _lru_gates_fwd(xl, wa_b, b_a, wx_b, b_x, lru_lambda, tr)
    h_lru = _lru_scan_fwd(a_lru, u_lru, tr)

    def lru_post(hv, gate, g):
        return _rms(hv * jax.nn.gelu(gate), g)

    def f_lru_post(first, last, hv, gate, g):
        return (lru_post(hv, gate, g),), ()
    cb_l = DS // DL
    (mixcat,) = _rows_call("lru_post", f_lru_post, nt, [h_lru, proj, lru_norm],
                           [_rt(tr, DL), _rt(tr, DL, cb_gate), _full(lru_norm)],
                           [((T, DS + DL), BF16, _rt(tr, DL, cb_l))], [], 'ttf', into=mixcat)

    ag_srcs, ag_lands = _split_wait("allgather_later_wait", ag_sems, ag_srcs, ag_lands, mixcat, _ag_copies,
                                    4 * len(later))
    g_out, g_mi, g_mo = _ag_finish("allgather_later_finish", ag_lands)
    w_out_f = g_out.reshape(DS + DL, D)
    w_mi_f = jnp.transpose(g_mi, (1, 0, 2)).reshape(D, DFF)
    w_mo_f = g_mo.reshape(DFF, D)
    (mix,) = _mm("mix", mixcat, w_out_f, outs=((BF16, None),))

    def f_post_mix(first, last, xv, mx, gpm, gpl):
        x1 = xv + _rms(mx, gpm)
        return (x1, _rms(x1, gpl)), ()
    x1, hn = _rows_call("post_mix", f_post_mix, nt, [x2, mix, post_mix_norm, pre_mlp_norm],
                        [_rt(tr, D), _rt(tr, D), _full(post_mix_norm), _full(pre_mlp_norm)],
                        [((T, D), F32, _rt(tr, D)), ((T, D), BF16, _rt(tr, D))], [], 'ttff')

    hm, act = _mm("mlp_in", hn, w_mi_f,
                  outs=((BF16, None), (BF16, lambda r, e: jnp.square(jnp.maximum(r, 0.0)))))
    (hm2,) = _mm("mlp_out", act, w_mo_f, outs=((BF16, None),))

    def f_final(first, last, x1v, hm2v, g, tg):
        def fwd(hv, gv):
            return x1v + _rms(hv, gv)
        x2v, vjp = jax.vjp(fwd, hm2v, g)
        err = x2v - tg
        dx2 = err * (1.0 / D)
        dh, dg = vjp(dx2)
        loss = jnp.full((1, LANES), 0.5 / D, F32) * jnp.sum(err * err)
        return (dx2, dh), (dg, loss)
    dx1a, dhm2, g_post_mlp, loss_part = _rows_call(
        "loss_head", f_final, nt, [x1, hm2, post_mlp_norm, tgt],
        [_rt(tr, D), _rt(tr, D), _full(post_mlp_norm), _rt(tr, D)],
        [((T, D), F32, _rt(tr, D)), ((T, D), BF16, _rt(tr, D))], [(1, D), (1, LANES)], 'ttft')

    def rs_chips_start(n, full, from_sib):
        pair = _pair_sum("pair_sum_" + n, full, from_sib, c_idx)
        sems, srcs, lands, token = _split_start("rs_start_" + n, [pair], [lax.empty((3,) + pair.shape[1:], BF16)],
                                                _rs_copies, 3)
        return (sems, srcs, lands), token

    def rs_begin(n, full):
        (from_sib,) = _rs_sibling("rs_sibling_" + n, [full])
        return rs_chips_start(n, full, from_sib)

    def rs_sibling_start(n, full):
        sems, srcs, lands, token = _split_start("rs_sibling_start_" + n, [full],
                                                [lax.empty((4,) + full.shape[1:], BF16)], _sib_copies, 4)
        return (sems, srcs, lands), token

    def rs_begin_late(n, state, after):
        (full,), (from_sib,) = _split_wait("rs_sibling_wait_" + n, *state, after, _sib_copies, 4)
        return rs_chips_start(n, full, from_sib)

    def rs_end(n, state, after):
        (pair,), (recv,) = _split_wait("rs_wait_" + n, *state, after, _rs_copies, 3)
        return pair, recv

    (gw_mo,) = _mm("dw_mlp_out", act, dhm2, ta=True, outs=((BF16, None),))
    sib_mo, tok = rs_sibling_start('w_mlp_out', gw_mo.reshape(N_DEV, DFF // N_DEV, D))
    (dhm,) = _mm("d_mlp_act", dhm2, w_mo_f, tb=True, extra=hm,
                 outs=((BF16, lambda r, e: r * (2.0 * jnp.maximum(e.astype(F32), 0.0))),), deps=[tok])
    rs_mo, tok = rs_begin_late('w_mlp_out', sib_mo, dhm)
    (gw_mi,) = _mm("dw_mlp_in", hn, dhm, ta=True, outs=((BF16, None),), out_blocks=N_DEV, deps=[tok])
    sib_mi, tok = rs_sibling_start('w_mlp_in', gw_mi)
    (dhn,) = _mm("d_mlp_in", dhm, w_mi_f, tb=True, outs=((BF16, None),), deps=[tok])
    rs_mi, tok_mi = rs_begin_late('w_mlp_in', sib_mi, dhn)

    def f_post_mix_bwd(first, last, x1v, mx, gpm, gpl, dhnv, dxa):
        _, vjp1 = jax.vjp(_rms, x1v, gpl)
        dx1, dgpl = vjp1(dhnv)
        dx1 = dx1 + dxa
        _, vjp2 = jax.vjp(_rms, mx, gpm)
        dmx, dgpm = vjp2(dx1)
        return (dx1, dmx), (dgpl, dgpm)
    dx1, dmix, g_pre_mlp, g_post_mix = _rows_call(
        "post_mix_bwd", f_post_mix_bwd, nt, [x1, mix, post_mix_norm, pre_mlp_norm, dhn, dx1a],
        [_rt(tr, D), _rt(tr, D), _full(post_mix_norm), _full(pre_mlp_norm), _rt(tr, D), _rt(tr, D)],
        [((T, D), F32, _rt(tr, D)), ((T, D), BF16, _rt(tr, D))], [(1, D), (1, D)], 'ttfftt', deps=[tok_mi])

    (gw_out,) = _mm("dw_out", mixcat, dmix, ta=True, outs=((BF16, None),))
    sib_out, tok = rs_sibling_start('w_out', gw_out.reshape(N_DEV, -1, D))
    (dmixcat,) = _mm("d_mix", dmix, w_out_f, tb=True, outs=((BF16, None),), deps=[tok])
    rs_out, tok_out = rs_begin_late('w_out', sib_out, dmixcat)

    def f_lru_post_bwd(first, last, hv, gate, g, dy):
        _, vjp = jax.vjp(lru_post, hv, gate, g)
        dh_, dgate, dg = vjp(dy)
        return (dgate, dh_), (dg,)
    dproj, dh_lru, g_lru_norm = _rows_call(
        "lru_post_bwd", f_lru_post_bwd, nt, [h_lru, proj, lru_norm, dmixcat],
        [_rt(tr, DL), _rt(tr, DL, cb_gate), _full(lru_norm), _rt(tr, DL, cb_l)],
        [((T, NP), BF16, _rt(tr, DL, cb_gate)), ((T, DL), F32, _rt(tr, DL))], [(1, DL)], 'ttft', deps=[tok_out])

    du_lru, da_lru = _lru_scan_bwd(a_lru, h_lru, dh_lru, tr)
    dxl, g_wa, g_ba, g_wx, g_bx, g_lam = _lru_gates_bwd(xl, wa_b, b_a, wx_b, b_x, lru_lambda, da_lru, du_lru, tr)

    conv_bwd_kinds = ['t', 'p0', 'n0', 't', 'n3', 'f', 'f']

    def f_lru_pre_bwd(first, last, xv, hp, xn, d, dn, w, b):
        dx, dw8, db = _conv_bwd_tile(first, last, xv, hp, xn, d, dn, w, b, silu=False)
        return (dx,), (dw8, db)
    dproj, g_convl8, g_convl_b = _rows_call(
        "lru_pre_bwd", f_lru_pre_bwd, nt, [proj, proj, proj, dxl, dxl, conv_l, lru_conv_b],
        [_rt(tr, DL, cb_xl), _halo_prev(tr, DL, cb_xl, rows=PACKED_ROWS),
         _halo_next(tr, DL, nt, cb_xl, rows=PACKED_ROWS), _rt(tr, DL),
         _halo_next(tr, DL, nt), _full(conv_l), _full(lru_conv_b)],
        [((T, NP), BF16, _rt(tr, DL, cb_xl))], [(SUBLANES, DL), (1, DL)], conv_bwd_kinds, into=dproj)

    def f_ssd_post_bwd(first, last, y, z, g, dy):
        _, vjp = jax.vjp(ssd_post, y, z, g)
        dy_, dz, dg = vjp(dy)
        return (dz, dy_), (dg,)
    dproj, dy_ssd, g_ssd_norm = _rows_call(
        "ssd_post_bwd", f_ssd_post_bwd, nt, [y_ssd, proj, ssd_norm, dmixcat],
        [_rt(tr, DS), _rt(tr, DS, cb_z), _full(ssd_norm), _rt(tr, DS, 0)],
        [((T, NP), BF16, _rt(tr, DS, cb_z)), ((T, DS), F32, _rt(tr, DS))], [(1, DS)], 'ttft', into=dproj)

    dxbc_act, ddt, g_alog, g_dskip = _ssd_bwd(xbc_act, dt, a_log, d_skip, h_prev, dy_ssd, n_heads, Q)

    def f_ssd_pre_bwd(first, last, xv, hp, xn, d, dn, w, b):
        dx, dw8, db = _conv_bwd_tile(first, last, xv, hp, xn, d, dn, w, b, silu=True)
        return (dx,), (dw8, db)
    dproj, g_convs8, g_convs_b = _rows_call(
        "ssd_pre_bwd", f_ssd_pre_bwd, nt, [proj, proj, proj, dxbc_act, dxbc_act, conv_s, ssd_conv_b],
        [_rt(tr, XBC), _halo_prev(tr, XBC, rows=PACKED_ROWS), _halo_next(tr, XBC, nt, rows=PACKED_ROWS),
         _rt(tr, XBC), _halo_next(tr, XBC, nt),
         _full(conv_s), _full(ssd_conv_b)],
        [((T, NP), BF16, _rt(tr, XBC))], [(SUBLANES, XBC), (1, XBC)], conv_bwd_kinds, into=dproj, cw=cwx)

    def f_ssd_dt_bwd(first, last, ddtv, dtr, dtb):
        ddtr = ddtv * jax.nn.sigmoid(dtr + dtb)
        return (ddtr,), (_colsum(ddtr),)
    dproj, g_dtb = _rows_call(
        "ssd_dt_bwd", f_ssd_dt_bwd, nt, [ddt, dt_raw, dt_bias],
        [_rt(tr, LANES), _rt(tr, LANES), _full(dt_bias)],
        [((T, NP), BF16, _rt(tr, LANES, cb_dt))], [(1, LANES)], 'ttf', into=dproj)
    small = {
        'ssd_conv_w': g_convs8[:CONV_WIDTH], 'ssd_conv_b': g_convs_b,
        'ssd_dt_bias': g_dtb[:, :n_heads], 'ssd_a_log': g_alog[:, :n_heads], 'ssd_d': g_dskip[:, :n_heads],
        'ssd_norm': g_ssd_norm, 'lru_conv_w': g_convl8[:CONV_WIDTH], 'lru_conv_b': g_convl_b,
        'lru_w_a': g_wa, 'lru_b_a': g_ba, 'lru_w_x': g_wx, 'lru_b_x': g_bx, 'lru_lambda': g_lam,
        'lru_norm': g_lru_norm, 'post_mix_norm': g_post_mix, 'pre_mlp_norm': g_pre_mlp,
        'post_mlp_norm': g_post_mlp, 'loss': loss_part[:, :1],
    }
    wide = ['lru_w_a', 'lru_w_x']
    narrow = [n for n in small if n not in wide]
    lb = lru_w_a.shape[-1]
    s_srcs = [_flat_rows([small[n] for n in narrow]), g_wa.reshape(-1, lb), g_wx.reshape(-1, lb)]
    s_sems, s_srcs, s_lands, tok = _split_start(
        "small_grads_start", [], [_own_block(a, dev) for a in s_srcs], _ag_copies, 4 * len(s_srcs))

    (gwp,) = _mm("dw_proj", h, dproj, ta=True, outs=((BF16, None),), deps=[tok])
    rs_in, tok = rs_begin(
        'w_in', jnp.stack([jnp.concatenate(my_cols(gwp, k * wb, (k + 1) * wb), axis=1) for k in range(N_DEV)]))
    (dh_a,) = _mm("d_proj_a", dproj, wp_a, tb=True, outs=((BF16, None),), deps=[tok])
    (dh_b,) = _mm("d_proj_b", dproj, wp_b, tb=True, outs=((BF16, None),), deps=[tok])

    def f_norm_in_bwd(first, last, xv, g, dha, dhb, dxa):
        _, vjp = jax.vjp(_rms, xv, g)
        dx, dg = vjp(jnp.concatenate([dha, dhb], axis=1))
        return (dx + dxa,), (dg,)
    grad_x, g_pre_mix = _rows_call(
        "norm_in_bwd", f_norm_in_bwd, nt, [x2, pre_mix_norm, dh_a, dh_b, dx1],
        [_rt(tr, D), _full(pre_mix_norm), _rt(tr, Dh), _rt(tr, Dh), _rt(tr, D)], [((T, D), F32, _rt(tr, D))],
        [(1, D)], 'tfttt')

    big_out = {}
    for n, state in (('w_mlp_out', rs_mo), ('w_mlp_in', rs_mi), ('w_out', rs_out)):
        p, r = rs_end(n, state, grad_x)
        big_out[n] = _adamw_big("adamw_" + n, W[n], Mo[n], Vo[n], p, r, chip_idx)

    (g_pm8,) = _allgather("allgather_pre_mix_grad", [_own_block(g_pre_mix, dev)], deps=[big_out['w_out'][0]])
    _, s_lands = _split_wait("small_grads_wait", s_sems, s_srcs, s_lands, g_pm8, _ag_copies, 4 * len(s_lands))
    g_narrow, g_wa8, g_wx8 = _ag_finish("small_grads_finish", s_lands)
    summed = dict(zip(narrow, _unflat(_sum8("sum_small_grads", g_narrow), [small[n].shape for n in narrow])))
    summed['pre_mix_norm'] = _sum8("sum_pre_mix_grad", g_pm8)
    summed['lru_w_a'] = _sum8("sum_lru_w_a_grads", g_wa8)
    summed['lru_w_x'] = _sum8("sum_lru_w_x_grads", g_wx8)
    loss = summed.pop('loss').reshape(())
    for n, full_w in (('ssd_conv_w', XBC), ('lru_conv_w', DL)):
        wdt = full_w // N_DEV
        summed[n] = lax.dynamic_slice_in_dim(summed[n], dev * wdt, wdt, axis=1)
    small_params = [n for n in names if n not in big]
    as2d = lambda a: a.reshape(-1, a.shape[-1])
    res = _adamw_small([as2d(W[n]) for n in small_params],
                       [summed[n].reshape(as2d(W[n]).shape) for n in small_params],
                       [as2d(Mo[n]) for n in small_params], [as2d(Vo[n]) for n in small_params])
    grads = {n: summed[n].reshape(W[n].shape) for n in small_params}
    delta, new_m, new_v = ({n: r.reshape(W[n].shape) for n, r in zip(small_params, rs)} for rs in res)

    p, r = rs_end('w_in', rs_in, g_pm8)
    big_out['w_in'] = _adamw_big("adamw_w_in", W['w_in'], Mo['w_in'], Vo['w_in'], p, r, chip_idx)
    for n in big:
        grads[n], delta[n], new_m[n], new_v[n] = big_out[n]

    return (loss, grad_x.reshape(x.shape), *[grads[n] for n in names], *[delta[n] for n in names],
            *[new_m[n] for n in names], *[new_v[n] for n in names])
```

```python
import functools

import jax
import jax.numpy as jnp
from jax import lax
from jax.experimental import pallas as pl
from jax.experimental.pallas import tpu as pltpu

F32, BF16 = jnp.float32, jnp.bfloat16
S = jax.ShapeDtypeStruct
MESH = pl.DeviceIdType.MESH

SSD_GROUPS = 8
LRU_C = 8.0
EPS = 1e-6
CONV_WIDTH = 4
ADAM_LR, ADAM_B1, ADAM_B2, ADAM_EPS, ADAM_WD, ADAM_STEP = 0.001, 0.9, 0.999, 1e-08, 0.01, 10

LANES = 128
SUBLANES = 8
VMEM_LIMIT = 56 * 1024 * 1024
N_DEV = 8
SMALL_W = 512
HI = lax.Precision.HIGHEST


def _pcall(body, **kw):
    return pl.pallas_call(body, **kw)


def _cparams(sem=None, **kw):
    return pltpu.CompilerParams(dimension_semantics=sem, vmem_limit_bytes=VMEM_LIMIT, **kw)


def _pick(n, cands):
    for c in cands:
        if c <= n and n % c == 0:
            return c
    return n


def _rt(tr, w, cb=0, n=None):
    if n is None:
        return pl.BlockSpec((tr, w), lambda i: (i, cb))
    return pl.BlockSpec((tr, w), lambda i: (n - 1 - i, cb))


PACKED_ROWS = 16


def _halo_prev(tr, w, cb=0, n=None, rows=SUBLANES):
    k = tr // rows
    if n is None:
        return pl.BlockSpec((rows, w), lambda i: (jnp.maximum(i * k - 1, 0), cb))
    return pl.BlockSpec((rows, w), lambda i: (jnp.maximum((n - 1 - i) * k - 1, 0), cb))


def _halo_next(tr, w, nt, cb=0, n=None, rows=SUBLANES):
    k = tr // rows
    last = nt * k - 1
    if n is None:
        return pl.BlockSpec((rows, w), lambda i: (jnp.minimum((i + 1) * k, last), cb))
    return pl.BlockSpec((rows, w), lambda i: (jnp.minimum((n - i) * k, last), cb))


def _full(a):
    nd = a.ndim
    return pl.BlockSpec(a.shape, lambda i: (0,) * nd)


def _rows_call(name, fn, n_tiles, arrays, in_specs, out_tiled, out_acc, kinds, into=None, deps=(), cw=None):
    n_in, n_t = len(arrays), len(out_tiled)
    n_skip = len(deps) + (0 if into is None else 1)
    width = in_specs[kinds.index('t')].block_shape[1]
    cols = [(0, width)] if cw is None else [(c, cw) for c in range(0, width, cw)]

    def body(*refs):
        i = pl.program_id(0)
        ins = refs[:n_in]
        outs = refs[n_in + n_skip:n_in + n_skip + n_t]
        accs = refs[n_in + n_skip + n_t:]
        if accs:
            @pl.when(i == 0)
            def _():
                for r in accs:
                    r[...] = jnp.zeros_like(r)

        def lanes(ref, rows, c0, w):
            return ref[rows, c0:c0 + w] if ref.shape[-1] == width else ref[rows, :]

        def load(k, c0, w):
            v = lanes(ins[k], slice(None), c0, w).astype(F32)
            if kinds[k][0] in 'pn' and v.shape[0] == PACKED_ROWS:
                v = v[SUBLANES:] if kinds[k][0] == 'p' else v[:SUBLANES]
            return v

        for c0, w in cols:
            touts, aouts = fn(i == 0, i == n_tiles - 1, *[load(k, c0, w) for k in range(n_in)])
            for r, v in zip(outs, touts):
                if r.shape[-1] == width:
                    r[:, c0:c0 + w] = v.astype(r.dtype)
                else:
                    r[...] = v.astype(r.dtype)
            for r, v in zip(accs, aouts):
                if r.shape[-1] == width:
                    r[:, c0:c0 + w] += v
                else:
                    r[...] += v

    out_shape = [S(sh, dt) for sh, dt, _ in out_tiled] + [S(sh, F32) for sh in out_acc]
    out_specs = [sp for _, _, sp in out_tiled]
    for sh in out_acc:
        out_specs.append(pl.BlockSpec(sh, lambda i, nd=len(sh): (0,) * nd))
    in_specs = list(in_specs) + [_ANY] * len(deps)
    if into is None:
        return _pcall(body, name=name, grid=(n_tiles,), in_specs=in_specs, out_specs=out_specs,
                      out_shape=out_shape, compiler_params=_cparams(("arbitrary",)))(*arrays, *deps)
    return _pcall(body, name=name, grid=(n_tiles,), in_specs=in_specs + [_ANY], out_specs=out_specs,
                  out_shape=out_shape, input_output_aliases={n_in + len(deps): 0},
                  compiler_params=_cparams(("arbitrary",)))(*arrays, *deps, into)


def _rms(x, g):
    return x * lax.rsqrt(jnp.mean(x * x, axis=-1, keepdims=True) + EPS) * g


def _colsum(v):
    return jnp.sum(v, axis=0, keepdims=True)


_TILES = (1152, 1024, 896, 768, 640, 512, 384, 256, 128)
_K_TILES = (4096, 3456, 3072, 2688, 2048, 1536, 1344, 1152, 1024, 896, 768, 640, 512, 384, 256, 128)


def _mm(name, a, b, *, ta=False, tb=False, outs=((F32, None),), extra=None, out_blocks=None, tm=None, tn=None, tk=None,
        deps=(), a_cols=None):
    M, K = (a.shape[1], a.shape[0]) if ta else a.shape
    if a_cols is not None:
        assert not ta
        K = a_cols[1]
    b3 = b.ndim == 3
    if b3:
        nb_b, brows, bcols = b.shape
        N = brows if tb else nb_b * bcols
    else:
        N = b.shape[0] if tb else b.shape[1]
    n_lim = N if out_blocks is None else N // out_blocks
    if b3 and not tb:
        n_lim = min(n_lim, bcols)
    tm = tm or _pick(M, _TILES[1:])
    tn = tn or _pick(n_lim, _TILES)
    tk = tk or _pick(bcols if (b3 and tb) else K, _K_TILES)
    nk = K // tk
    assert M % tm == 0 and N % tn == 0 and K % tk == 0
    dn = (((0 if ta else 1,), (1 if tb else 0,)), ((), ()))
    n_extra = 0 if extra is None else 1
    n_out = len(outs)

    def body(*refs):
        a_ref, b_ref = refs[0], refs[1]
        e_ref = refs[2] if n_extra else None
        o_refs = refs[2 + n_extra + len(deps):2 + n_extra + len(deps) + n_out]

        def finish(r):
            e = e_ref[...] if n_extra else None
            for o, (_, f) in zip(o_refs, outs):
                o[...] = (r if f is None else f(r, e)).astype(o.dtype)

        part = lax.dot_general(a_ref[...], b_ref[...], dn, preferred_element_type=F32)
        if nk == 1:
            finish(part)
            return
        acc = refs[-1]
        k = pl.program_id(2)

        @pl.when(k == 0)
        def _():
            acc[...] = part

        @pl.when(jnp.logical_and(k > 0, k < nk - 1))
        def _():
            acc[...] += part

        @pl.when(k == nk - 1)
        def _():
            finish(acc[...] + part)

    k0 = 0 if a_cols is None else a_cols[0] // tk
    a_spec = (pl.BlockSpec((tk, tm), lambda i, j, k: (k, i)) if ta
              else pl.BlockSpec((tm, tk), lambda i, j, k: (i, k + k0)))
    if not b3:
        b_spec = pl.BlockSpec((tn, tk), lambda i, j, k: (j, k)) if tb else pl.BlockSpec((tk, tn), lambda i, j, k: (k, j))
    elif tb:
        per = bcols // tk
        b_spec = pl.BlockSpec((None, tn, tk), lambda i, j, k: (k // per, j, k % per))
    else:
        per = bcols // tn
        b_spec = pl.BlockSpec((None, tk, tn), lambda i, j, k: (j // per, k, j % per))
    o_spec = pl.BlockSpec((tm, tn), lambda i, j, k: (i, j))
    if out_blocks is None:
        out_specs, out_shape = [o_spec] * n_out, [S((M, N), dt) for dt, _ in outs]
    else:
        per_o = N // out_blocks // tn
        ob_spec = pl.BlockSpec((None, tm, tn), lambda i, j, k: (j // per_o, i, j % per_o))
        out_specs, out_shape = [ob_spec] * n_out, [S((out_blocks, M, N // out_blocks), dt) for dt, _ in outs]
    in_specs = [a_spec, b_spec] + ([o_spec] if n_extra else []) + [_ANY] * len(deps)
    args = [a, b] + ([extra] if n_extra else []) + list(deps)
    return _pcall(body, name=name, grid=(M // tm, N // tn, nk), in_specs=in_specs, out_specs=out_specs,
                  out_shape=out_shape, scratch_shapes=[pltpu.VMEM((tm, tn), F32)] if nk > 1 else [],
                  compiler_params=_cparams(("parallel", "parallel", "arbitrary")))(*args)


def _shift_down(x, halo, s):
    if s == 0:
        return x
    r = pltpu.roll(x, s, 0)
    hr = pltpu.roll(halo, s, 0)
    row = lax.broadcasted_iota(jnp.int32, halo.shape, 0)
    top = jnp.where(row < s, hr, r[:SUBLANES])
    if x.shape[0] == SUBLANES:
        return top
    return jnp.concatenate([top, r[SUBLANES:]], axis=0)


def _shift_up(x, nxt, s):
    if s == 0:
        return x
    n = x.shape[0]
    r = pltpu.roll(x, n - s, 0)
    nr = pltpu.roll(nxt, SUBLANES - s, 0)
    row = lax.broadcasted_iota(jnp.int32, nxt.shape, 0)
    bot = jnp.where(row >= SUBLANES - s, nr, r[n - SUBLANES:])
    if n == SUBLANES:
        return bot
    return jnp.concatenate([r[:n - SUBLANES], bot], axis=0)


def _conv_pre(x, halo, w, b):
    acc = b + w[CONV_WIDTH - 1:CONV_WIDTH, :] * x
    for k in range(CONV_WIDTH - 1):
        acc = acc + w[k:k + 1, :] * _shift_down(x, halo, CONV_WIDTH - 1 - k)
    return acc


def _silu_grad(p):
    s = jax.nn.sigmoid(p)
    return s * (1.0 + p * (1.0 - s))


def _conv_bwd_tile(first, last, x, hprev, xnext, d, dnext, w, b, silu):
    hprev = jnp.where(first, 0.0, hprev)
    if silu:
        d = d * _silu_grad(_conv_pre(x, hprev, w, b))
        pre_next = _conv_pre(xnext, x[x.shape[0] - SUBLANES:], w, b)
        dnext = dnext * _silu_grad(pre_next)
    dnext = jnp.where(last, 0.0, dnext)
    dx = w[CONV_WIDTH - 1:CONV_WIDTH, :] * d
    row8 = lax.broadcasted_iota(jnp.int32, (SUBLANES, x.shape[1]), 0)
    dw8 = jnp.where(row8 == CONV_WIDTH - 1, _colsum(d * x), 0.0)
    for k in range(CONV_WIDTH - 1):
        s = CONV_WIDTH - 1 - k
        dx = dx + w[k:k + 1, :] * _shift_up(d, dnext, s)
        dw8 = dw8 + jnp.where(row8 == k, _colsum(d * _shift_down(x, hprev, s)), 0.0)
    return dx, dw8, _colsum(d)


def _ssd_dims(xbc_act, n_heads):
    T, XBC = xbc_act.shape
    GN = XBC // 4
    DS = XBC - 2 * GN
    G = SSD_GROUPS
    N = GN // G
    P = DS // n_heads
    K = n_heads // G
    return T, XBC, DS, GN, G, N, P, K


def _ssd_common(dt, alog, Q):
    a = -jnp.exp(alog)
    adt = dt * a
    li = lax.broadcasted_iota(jnp.int32, (Q, Q), 0)
    si = lax.broadcasted_iota(jnp.int32, (Q, Q), 1)
    causal = li >= si
    ltri = causal.astype(F32)
    acs = jnp.dot(ltri, adt, precision=HI, preferred_element_type=F32)
    acs_row = lax.dot_general(adt, ltri, (((0,), (1,)), ((), ())), precision=HI,
                              preferred_element_type=F32)
    return a, adt, causal, ltri, acs, acs_row


def _expander(g, K, P, W):
    r = lax.broadcasted_iota(jnp.int32, (LANES, W), 0)
    c = lax.broadcasted_iota(jnp.int32, (LANES, W), 1)
    return (c // P + g * K == r).astype(F32)


def _dotb(a, b, dn=(((1,), (0,)), ((), ()))):
    return lax.dot_general(a.astype(BF16), b.astype(BF16), dn, preferred_element_type=F32)


def _dot_split(a, sel, terms, dn=(((1,), (0,)), ((), ()))):
    selb = sel.astype(BF16)
    out = None
    for _ in range(terms):
        piece = a.astype(BF16)
        part = lax.dot_general(piece, selb, dn, preferred_element_type=F32)
        out = part if out is None else out + part
        a = a - piece.astype(F32)
    return out


_NT = (((1,), (1,)), ((), ()))
_TN = (((0,), (0,)), ((), ()))


def _ssd_fwd(xbc_act, dt, alog, dskip, n_heads, Q):
    T, XBC, DS, GN, G, N, P, K = _ssd_dims(xbc_act, n_heads)
    W = K * P
    nc = T // Q

    def body(xs_ref, b_ref, c_ref, dt_ref, alog_ref, d_ref, y_ref, hp_ref, h_scr):
        ci = pl.program_id(0)

        @pl.when(ci == 0)
        def _():
            h_scr[...] = jnp.zeros_like(h_scr)

        dtv = dt_ref[...]
        a, adt, causal, ltri, acs, acs_row = _ssd_common(dtv, alog_ref[...], Q)
        lane_head = lax.broadcasted_iota(jnp.int32, (Q, W), 1) // P
        for g in range(G):
            eg = _expander(g, K, P, W)
            dtb = _dot_split(dtv, eg, 3)
            acsb = _dot_split(acs, eg, 3)
            lastb = acsb[Q - 1:Q, :]
            db = _dot_split(jnp.broadcast_to(d_ref[...], (SUBLANES, LANES)), eg, 3)[0:1, :]
            xg = xs_ref[:, g * W:(g + 1) * W]
            bg = b_ref[:, g * N:(g + 1) * N]
            cg = c_ref[:, g * N:(g + 1) * N]
            xt = xg * dtb
            sc = _dotb(cg, bg, _NT)
            yd = jnp.zeros((Q, W), F32)
            for k in range(K):
                h = g * K + k
                seg = acs[:, h:h + 1] - acs_row[h:h + 1, :]
                lh = jnp.where(causal, jnp.exp(jnp.minimum(seg, 0.0)), 0.0)
                xk = jnp.where(lane_head == k, xt, 0.0)
                yd = yd + _dotb(sc * lh, xk)
            hp = h_scr[g]
            yoff = _dotb(cg, hp) * jnp.exp(acsb)
            y_ref[:, g * W:(g + 1) * W] = yd + yoff + xg * db
            e_end = jnp.exp(lastb - acsb)
            st = _dotb(bg, xt * e_end, _TN)
            hp_ref[0, g] = hp
            h_scr[g] = jnp.exp(lastb) * hp + st

    cb = DS // GN
    in_specs = [pl.BlockSpec((Q, DS), lambda c: (c, 0)),
                pl.BlockSpec((Q, GN), lambda c: (c, cb)),
                pl.BlockSpec((Q, GN), lambda c: (c, cb + 1)),
                pl.BlockSpec((Q, LANES), lambda c: (c, 0)),
                pl.BlockSpec((1, LANES), lambda c: (0, 0)),
                pl.BlockSpec((1, LANES), lambda c: (0, 0))]
    out_specs = [pl.BlockSpec((Q, DS), lambda c: (c, 0)),
                 pl.BlockSpec((1, G, N, W), lambda c: (c, 0, 0, 0))]
    return _pcall(body, name="ssd_fwd", grid=(nc,), in_specs=in_specs, out_specs=out_specs,
                  out_shape=[S((T, DS), F32), S((nc, G, N, W), F32)],
                  scratch_shapes=[pltpu.VMEM((G, N, W), F32)],
                  compiler_params=_cparams(("arbitrary",)))(xbc_act, xbc_act, xbc_act, dt, alog, dskip)


def _ssd_bwd(xbc_act, dt, alog, dskip, hprev, dy, n_heads, Q):
    T, XBC, DS, GN, G, N, P, K = _ssd_dims(xbc_act, n_heads)
    W = K * P
    nc = T // Q

    def body(xs_ref, b_ref, c_ref, dt_ref, alog_ref, d_ref, hp_ref, dy_ref,
             dxbc_ref, ddt_ref, dalog_ref, dd_ref, dh_scr):
        ci = pl.program_id(0)

        @pl.when(ci == 0)
        def _():
            dh_scr[...] = jnp.zeros_like(dh_scr)
            dalog_ref[...] = jnp.zeros_like(dalog_ref)
            dd_ref[...] = jnp.zeros_like(dd_ref)

        dtv = dt_ref[...]
        a, adt, causal, ltri, acs, acs_row = _ssd_common(dtv, alog_ref[...], Q)
        lane_head = lax.broadcasted_iota(jnp.int32, (Q, W), 1) // P
        lane128 = lax.broadcasted_iota(jnp.int32, (Q, LANES), 1)
        sub128 = lax.broadcasted_iota(jnp.int32, (LANES, Q), 0)
        rowq = lax.broadcasted_iota(jnp.int32, (Q, W), 0)
        dacs = jnp.zeros((Q, LANES), F32)
        dacs_row = jnp.zeros((LANES, Q), F32)
        ddt = jnp.zeros((Q, LANES), F32)
        dd_acc = jnp.zeros((1, LANES), F32)
        for g in range(G):
            eg = _expander(g, K, P, W)
            dtb = _dot_split(dtv, eg, 3)
            acsb = _dot_split(acs, eg, 3)
            lastb = acsb[Q - 1:Q, :]
            db = _dot_split(jnp.broadcast_to(d_ref[...], (SUBLANES, LANES)), eg, 3)[0:1, :]
            xg = xs_ref[:, g * W:(g + 1) * W]
            bg = b_ref[:, g * N:(g + 1) * N]
            cg = c_ref[:, g * N:(g + 1) * N]
            dyg = dy_ref[:, g * W:(g + 1) * W]
            hp = hp_ref[0, g]
            dhn = dh_scr[g]
            xt = xg * dtb
            sc = _dotb(cg, bg, _NT)
            eacs = jnp.exp(acsb)
            e_end = jnp.exp(lastb - acsb)
            elast = jnp.exp(lastb)

            wv = dyg * eacs
            dcg = _dotb(wv, hp, _NT)
            dhp = _dotb(cg, wv, _TN) + elast * dhn
            dacsb = dyg * (_dotb(cg, hp) * eacs)

            xe = xt * e_end
            dbg = _dotb(xe, dhn, _NT)
            v = _dotb(bg, dhn)
            dxt = v * e_end
            de = v * xe
            dacsb = dacsb - de
            dlastb = _colsum(de) + elast * jnp.sum(dhn * hp, axis=0, keepdims=True)

            dsc = jnp.zeros((Q, Q), F32)
            for k in range(K):
                h = g * K + k
                seg = acs[:, h:h + 1] - acs_row[h:h + 1, :]
                lh = jnp.where(causal, jnp.exp(jnp.minimum(seg, 0.0)), 0.0)
                mh = sc * lh
                dyk = jnp.where(lane_head == k, dyg, 0.0)
                dxt = dxt + jnp.where(lane_head == k, _dotb(mh, dyg, _TN), 0.0)
                dm = _dotb(dyk, xt, _NT)
                dsc = dsc + dm * lh
                gm = dm * mh
                dacs = dacs + jnp.where(lane128 == h, jnp.sum(gm, axis=1, keepdims=True), 0.0)
                dacs_row = dacs_row - jnp.where(sub128 == h, jnp.sum(gm, axis=0, keepdims=True), 0.0)
            dcg = dcg + _dotb(dsc, bg)
            dbg = dbg + _dotb(dsc, cg, _TN)

            dacsb = dacsb + jnp.where(rowq == Q - 1, dlastb, 0.0)
            dacs = dacs + _dot_split(dacsb, eg, 2, _NT)
            ddt = ddt + _dot_split(dxt * xg, eg, 2, _NT)
            dd_acc = dd_acc + _dot_split(jnp.broadcast_to(_colsum(dyg * xg), (SUBLANES, W)), eg, 2, _NT)[0:1, :]
            dxbc_ref[:, g * W:(g + 1) * W] = dxt * dtb + dyg * db
            dxbc_ref[:, DS + g * N:DS + (g + 1) * N] = dbg
            dxbc_ref[:, DS + GN + g * N:DS + GN + (g + 1) * N] = dcg
            dh_scr[g] = dhp

        eye = (lax.broadcasted_iota(jnp.int32, (LANES, LANES), 0) ==
               lax.broadcasted_iota(jnp.int32, (LANES, LANES), 1)).astype(F32)
        dacs = dacs + lax.dot_general(dacs_row, eye, _TN, precision=HI, preferred_element_type=F32)
        dadt = lax.dot_general(ltri, dacs, _TN, precision=HI, preferred_element_type=F32)
        ddt_ref[...] = ddt + dadt * a
        dalog_ref[...] += _colsum(dadt * dtv) * a
        dd_ref[...] += dd_acc

    cb = DS // GN
    rv = lambda c: nc - 1 - c
    in_specs = [pl.BlockSpec((Q, DS), lambda c: (rv(c), 0)),
                pl.BlockSpec((Q, GN), lambda c: (rv(c), cb)),
                pl.BlockSpec((Q, GN), lambda c: (rv(c), cb + 1)),
                pl.BlockSpec((Q, LANES), lambda c: (rv(c), 0)),
                pl.BlockSpec((1, LANES), lambda c: (0, 0)),
                pl.BlockSpec((1, LANES), lambda c: (0, 0)),
                pl.BlockSpec((1, G, N, W), lambda c: (rv(c), 0, 0, 0)),
                pl.BlockSpec((Q, DS), lambda c: (rv(c), 0))]
    out_specs = [pl.BlockSpec((Q, XBC), lambda c: (rv(c), 0)),
                 pl.BlockSpec((Q, LANES), lambda c: (rv(c), 0)),
                 pl.BlockSpec((1, LANES), lambda c: (0, 0)),
                 pl.BlockSpec((1, LANES), lambda c: (0, 0))]
    return _pcall(body, name="ssd_bwd", grid=(nc,), in_specs=in_specs, out_specs=out_specs,
                  out_shape=[S((T, XBC), F32), S((T, LANES), F32), S((1, LANES), F32), S((1, LANES), F32)],
                  scratch_shapes=[pltpu.VMEM((G, N, W), F32)],
                  compiler_params=_cparams(("arbitrary",)))(
                      xbc_act, xbc_act, xbc_act, dt, alog, dskip, hprev, dy)


def _blockdiag(x, w_ref, dn=(((1,), (0,)), ((), ()))):
    H, B, _ = w_ref.shape
    return jnp.concatenate([_dotb(x[:, h * B:(h + 1) * B], w_ref[h], dn) for h in range(H)], axis=1)


def _lru_elem(xl, r_pre, i_pre, lam):
    r = jax.nn.sigmoid(r_pre)
    i = jax.nn.sigmoid(i_pre)
    log_a = -LRU_C * r * jax.nn.softplus(-lam)
    a = jnp.exp(log_a)
    u = jnp.sqrt(1.0 - jnp.exp(2.0 * log_a)) * (i * xl)
    return a, u


def _lru_gates_fwd(xl, w_a, b_a, w_x, b_x, lam, tr):
    T, DL = xl.shape

    def body(xl_ref, wa_ref, ba_ref, wx_ref, bx_ref, lam_ref, a_ref, u_ref):
        x = xl_ref[...]
        r_pre = _blockdiag(x, wa_ref) + ba_ref[...]
        i_pre = _blockdiag(x, wx_ref) + bx_ref[...]
        a, u = _lru_elem(x, r_pre, i_pre, lam_ref[...])
        a_ref[...] = a
        u_ref[...] = u

    w3 = pl.BlockSpec(w_a.shape, lambda i: (0, 0, 0))
    vec = pl.BlockSpec((1, DL), lambda i: (0, 0))
    return _pcall(body, name="lru_gates_fwd", grid=(T // tr,),
                  in_specs=[_rt(tr, DL), w3, vec, w3, vec, vec],
                  out_specs=[_rt(tr, DL), _rt(tr, DL)], out_shape=[S((T, DL), F32), S((T, DL), F32)],
                  compiler_params=_cparams(("parallel",)))(xl, w_a, b_a, w_x, b_x, lam)


def _lru_gates_bwd(xl, w_a, b_a, w_x, b_x, lam, da, du, tr):
    T, DL = xl.shape
    H, B, _ = w_a.shape

    def body(xl_ref, wa_ref, ba_ref, wx_ref, bx_ref, lam_ref, da_ref, du_ref,
             dxl_ref, dwa_ref, dba_ref, dwx_ref, dbx_ref, dlam_ref):
        @pl.when(pl.program_id(0) == 0)
        def _():
            for r in (dwa_ref, dba_ref, dwx_ref, dbx_ref, dlam_ref):
                r[...] = jnp.zeros_like(r)

        x = xl_ref[...]
        r_pre = _blockdiag(x, wa_ref) + ba_ref[...]
        i_pre = _blockdiag(x, wx_ref) + bx_ref[...]
        _, vjp = jax.vjp(_lru_elem, x, r_pre, i_pre, lam_ref[...])
        dx, dr, di, dlam = vjp((da_ref[...], du_ref[...]))
        dxl_ref[...] = dx + _blockdiag(dr, wa_ref, _NT) + _blockdiag(di, wx_ref, _NT)
        for h in range(H):
            xh = x[:, h * B:(h + 1) * B]
            dwa_ref[h] += _dotb(xh, dr[:, h * B:(h + 1) * B], _TN)
            dwx_ref[h] += _dotb(xh, di[:, h * B:(h + 1) * B], _TN)
        dba_ref[...] += _colsum(dr)
        dbx_ref[...] += _colsum(di)
        dlam_ref[...] += dlam

    w3 = pl.BlockSpec(w_a.shape, lambda i: (0, 0, 0))
    vec = pl.BlockSpec((1, DL), lambda i: (0, 0))
    return _pcall(body, name="lru_gates_bwd", grid=(T // tr,),
                  in_specs=[_rt(tr, DL), w3, vec, w3, vec, vec, _rt(tr, DL), _rt(tr, DL)],
                  out_specs=[_rt(tr, DL), w3, vec, w3, vec, vec],
                  out_shape=[S((T, DL), F32), S(w_a.shape, F32), S((1, DL), F32), S(w_a.shape, F32),
                             S((1, DL), F32), S((1, DL), F32)],
                  compiler_params=_cparams(("arbitrary",)))(xl, w_a, b_a, w_x, b_x, lam, da, du)


def _groups(v):
    return v.reshape(v.shape[0] // SUBLANES, SUBLANES, v.shape[1])


def _rows_shifted(v, edge, up):
    sub = lax.broadcasted_iota(jnp.int32, v.shape, 1)
    if up:
        other = jnp.concatenate([v[1:], edge[None]], axis=0)
        return jnp.where(sub < SUBLANES - 1, pltpu.roll(v, SUBLANES - 1, 1), pltpu.roll(other, SUBLANES - 1, 1))
    other = jnp.concatenate([edge[None], v[:-1]], axis=0)
    return jnp.where(sub >= 1, pltpu.roll(v, 1, 1), pltpu.roll(other, 1, 1))


def _scan_tile(a, u, entering, emit, up):
    G = a.shape[0]
    sub = lax.broadcasted_iota(jnp.int32, a.shape, 1)
    d = 1
    while d < SUBLANES:
        if up:
            keep = sub < SUBLANES - d
            a_s = jnp.where(keep, pltpu.roll(a, SUBLANES - d, 1), 1.0)
            u_s = jnp.where(keep, pltpu.roll(u, SUBLANES - d, 1), 0.0)
        else:
            keep = sub >= d
            a_s = jnp.where(keep, pltpu.roll(a, d, 1), 1.0)
            u_s = jnp.where(keep, pltpu.roll(u, d, 1), 0.0)
        u = a * u_s + u
        a = a * a_s
        d *= 2
    for g in (reversed(range(G)) if up else range(G)):
        hg = u[g] + a[g] * entering
        emit(g, hg)
        entering = hg[0:1] if up else hg[SUBLANES - 1:SUBLANES]
    return entering


def _lru_scan_fwd(a, u, tr):
    T, DL = a.shape

    def body(a_ref, u_ref, h_ref, carry):
        @pl.when(pl.program_id(0) == 0)
        def _():
            carry[...] = jnp.zeros_like(carry)

        def emit(g, hg):
            h_ref[g * SUBLANES:(g + 1) * SUBLANES, :] = hg

        last = _scan_tile(_groups(a_ref[...]), _groups(u_ref[...]), carry[0:1, :], emit, up=False)
        carry[...] = jnp.broadcast_to(last, carry.shape)

    return _pcall(body, name="lru_scan_fwd", grid=(T // tr,), in_specs=[_rt(tr, DL), _rt(tr, DL)],
                  out_specs=_rt(tr, DL), out_shape=S((T, DL), F32),
                  scratch_shapes=[pltpu.VMEM((SUBLANES, DL), F32)],
                  compiler_params=_cparams(("arbitrary",)))(a, u)


def _lru_scan_bwd(a, h, dh, tr):
    T, DL = a.shape
    n = T // tr

    def body(a_ref, an_ref, h_ref, hp_ref, dh_ref, du_ref, da_ref, carry):
        i = pl.program_id(0)
        ti = n - 1 - i

        @pl.when(i == 0)
        def _():
            carry[...] = jnp.zeros_like(carry)

        a_next = _rows_shifted(_groups(a_ref[...]), jnp.where(ti == n - 1, 0.0, an_ref[...]), up=True)
        h_prev = _rows_shifted(_groups(h_ref[...]), jnp.where(ti == 0, 0.0, hp_ref[...]), up=False)

        def emit(g, gg):
            du_ref[g * SUBLANES:(g + 1) * SUBLANES, :] = gg
            da_ref[g * SUBLANES:(g + 1) * SUBLANES, :] = gg * h_prev[g]

        top = _scan_tile(a_next, _groups(dh_ref[...]), carry[0:1, :], emit, up=True)
        carry[...] = jnp.broadcast_to(top, carry.shape)

    return _pcall(body, name="lru_scan_bwd", grid=(n,),
                  in_specs=[_rt(tr, DL, 0, n), _halo_next(tr, DL, n, 0, n), _rt(tr, DL, 0, n),
                            _halo_prev(tr, DL, 0, n), _rt(tr, DL, 0, n)],
                  out_specs=[_rt(tr, DL, 0, n), _rt(tr, DL, 0, n)],
                  out_shape=[S((T, DL), F32), S((T, DL), F32)],
                  scratch_shapes=[pltpu.VMEM((SUBLANES, DL), F32)],
                  compiler_params=_cparams(("arbitrary",)))(a, a, h, h, dh)


def _adamw(w, g, m, v):
    m = ADAM_B1 * m + (1.0 - ADAM_B1) * g
    v = ADAM_B2 * v + (1.0 - ADAM_B2) * (g * g)
    m_hat = m / (1.0 - ADAM_B1 ** ADAM_STEP)
    v_hat = v / (1.0 - ADAM_B2 ** ADAM_STEP)
    delta = -ADAM_LR * (m_hat / (jnp.sqrt(v_hat) + ADAM_EPS) + ADAM_WD * w)
    return delta, m, v


def _adamw_big(name, w, m, v, own, recv, own_idx):
    _, R, C = w.shape
    n_recv = recv.shape[0]
    tr = _pick(R, (256, 128, 64, 32, 16))

    def body(idx_ref, w_ref, m_ref, v_ref, p_ref, *rest):
        g = p_ref[...].astype(F32)
        for r in rest[:n_recv]:
            g = g + r[...].astype(F32)
        g_ref, d_ref, nm_ref, nv_ref = rest[n_recv:]
        d, nm, nv = _adamw(w_ref[...], g, m_ref[...], v_ref[...])
        g_ref[...] = g
        d_ref[...] = d
        nm_ref[...] = nm
        nv_ref[...] = nv

    r_spec = lambda s: pl.BlockSpec((None, tr, C), lambda i, idx: (s, i, 0))
    t2 = r_spec(0)
    gs = pltpu.PrefetchScalarGridSpec(
        num_scalar_prefetch=1, grid=(R // tr,),
        in_specs=[t2, t2, t2, pl.BlockSpec((None, tr, C), lambda i, idx: (idx[0], i, 0))]
        + [r_spec(s) for s in range(n_recv)],
        out_specs=[t2, t2, t2, t2])
    return _pcall(body, name=name, grid_spec=gs, out_shape=[S((1, R, C), F32)] * 4,
                  compiler_params=_cparams(("parallel",)))(own_idx, w, m, v, own, *([recv] * n_recv))


def _adamw_small(ws, gs, ms, vs):
    n = len(ws)

    def body(*refs):
        for k in range(n):
            d, nm, nv = _adamw(refs[k][...], refs[n + k][...], refs[2 * n + k][...], refs[3 * n + k][...])
            refs[4 * n + k][...] = d
            refs[5 * n + k][...] = nm
            refs[6 * n + k][...] = nv

    res = _pcall(body, name="adamw_small", out_shape=[S(w.shape, F32) for w in ws] * 3,
                 compiler_params=_cparams())(*ws, *gs, *ms, *vs)
    return res[:n], res[n:2 * n], res[2 * n:]


def _sum8(name, parts):
    _, R, C = parts.shape

    def body(p_ref, o_ref):
        acc = p_ref[0]
        for k in range(1, N_DEV):
            acc = acc + p_ref[k]
        o_ref[...] = acc

    return _pcall(body, name=name, out_shape=S((R, C), F32), compiler_params=_cparams())(parts)


def _pair_sum(name, full, recv, c_idx):
    _, R, C = full.shape
    tr = _pick(R, (256, 128, 64, 32, 16))

    def body(c_ref, f_ref, r_ref, o_ref):
        o_ref[...] = (f_ref[...].astype(F32) + r_ref[...].astype(F32)).astype(o_ref.dtype)

    gs = pltpu.PrefetchScalarGridSpec(
        num_scalar_prefetch=1, grid=(4, R // tr),
        in_specs=[pl.BlockSpec((None, tr, C), lambda j, i, c: (2 * j + c[0], i, 0)),
                  pl.BlockSpec((None, tr, C), lambda j, i, c: (j, i, 0))],
        out_specs=pl.BlockSpec((None, tr, C), lambda j, i, c: (j, i, 0)))
    return _pcall(body, name=name, grid_spec=gs, out_shape=S((4, R, C), BF16),
                  compiler_params=_cparams(("parallel", "parallel")))(c_idx, full, recv)


def _cast_bf16(name, w, dev_idx, row0=0, rows=None):
    C = w.shape[2]
    R = w.shape[1] if rows is None else rows
    tr = _pick(R, (256, 128, 64, 32, 16))
    b0 = row0 // tr

    def body(d_ref, w_ref, o_ref):
        o_ref[...] = w_ref[...].astype(BF16)

    gs = pltpu.PrefetchScalarGridSpec(
        num_scalar_prefetch=1, grid=(R // tr,),
        in_specs=[pl.BlockSpec((None, tr, C), lambda i, d: (0, i + b0, 0))],
        out_specs=pl.BlockSpec((None, tr, C), lambda i, d: (d[0], i, 0)))
    return _pcall(body, name=name, grid_spec=gs, out_shape=S((N_DEV, R, C), BF16),
                  compiler_params=_cparams(("parallel",)))(dev_idx, w)


def _own_block(v, dev):
    return lax.dynamic_update_slice(lax.empty((N_DEV,) + v.shape, v.dtype), v[None], (dev,) + (0,) * v.ndim)


_ANY = pl.BlockSpec(memory_space=pl.ANY)


def _position():
    return lax.axis_index("x"), lax.axis_index("y"), lax.axis_index("c")


def _allgather(name, bufs, deps=()):
    n = len(bufs)
    nd = len(deps)

    def body(*refs):
        outs = refs[n + nd:2 * n + nd]
        send, recv = refs[2 * n + nd:]
        x, y, c = _position()
        me, sib = (x, y, c), (x, y, 1 - c)
        chips = [(1 - x, y), (x, 1 - y), (1 - x, 1 - y)]

        def copy(a, k, block, to):
            bx, by, bc = block
            blk = outs[a].at[4 * bx + 2 * by + bc]
            return pltpu.make_async_remote_copy(
                src_ref=blk, dst_ref=blk, send_sem=send.at[a, k], recv_sem=recv.at[a, k],
                device_id=to, device_id_type=MESH)

        first = []
        for a in range(n):
            first.append(copy(a, 0, me, sib))
            first += [copy(a, 1 + j, me, (*chip, c)) for j, chip in enumerate(chips)]
        for cp in first:
            cp.start()
        passed = []
        for j, chip in enumerate(chips):
            for a in range(n):
                copy(a, 1 + j, (*chip, c), me).wait_recv()
                cp = copy(a, 4 + j, (*chip, c), sib)
                cp.start()
                passed.append(cp)
        for a in range(n):
            copy(a, 0, sib, me).wait_recv()
        for j, chip in enumerate(chips):
            for a in range(n):
                copy(a, 4 + j, (*chip, 1 - c), me).wait_recv()
        for cp in first + passed:
            cp.wait_send()

    return _pcall(body, name=name, in_specs=[_ANY] * (n + nd), out_specs=[_ANY] * n,
                  out_shape=[S(b.shape, b.dtype) for b in bufs], input_output_aliases={a: a for a in range(n)},
                  scratch_shapes=[pltpu.SemaphoreType.DMA((n, 7)), pltpu.SemaphoreType.DMA((n, 7))])(*bufs, *deps)


def _rs_sibling(name, fulls):
    n = len(fulls)

    def body(*refs):
        ins, outs = refs[:n], refs[n:2 * n]
        send, recv = refs[2 * n:]
        x, y, c = _position()
        copies = []
        for a in range(n):
            for j in range(4):
                copies.append(pltpu.make_async_remote_copy(
                    src_ref=ins[a].at[2 * j + (1 - c)], dst_ref=outs[a].at[j], send_sem=send.at[a, j],
                    recv_sem=recv.at[a, j], device_id=(x, y, 1 - c), device_id_type=MESH))
        for cp in copies:
            cp.start()
        for cp in copies:
            cp.wait()

    return _pcall(body, name=name, in_specs=[_ANY] * n, out_specs=[_ANY] * n,
                  out_shape=[S((4,) + f.shape[1:], f.dtype) for f in fulls],
                  scratch_shapes=[pltpu.SemaphoreType.DMA((n, 4)), pltpu.SemaphoreType.DMA((n, 4))])(*fulls)


_HBM = pl.BlockSpec(memory_space=pltpu.HBM)
_SEM = pl.BlockSpec(memory_space=pltpu.SEMAPHORE)
_EFFECT = pltpu.SideEffectType.DATAFLOW_SIDE_EFFECTING


def _remote_copies(copies_fn, srcs, lands, send, recv):
    x, y, c = _position()
    return [pltpu.make_async_remote_copy(src_ref=s, dst_ref=d, send_sem=send[i], recv_sem=recv[i], device_id=to,
                                         device_id_type=MESH)
            for i, (s, d, to) in enumerate(copies_fn(x, y, c, srcs, lands))]


def _split_start(name, srcs, lands, copies_fn, nc, after=()):
    n, nl, na = len(srcs), len(lands), len(after)

    def body(*refs):
        src_refs, land_refs = refs[:n], refs[n:n + nl]
        outs = refs[n + nl + na:]
        for cp in _remote_copies(copies_fn, src_refs, land_refs, outs[:nc], outs[nc:2 * nc]):
            cp.start()
        outs[-1][...] = jnp.zeros_like(outs[-1])

    hbm = lambda a: pltpu.with_memory_space_constraint(a, pltpu.HBM)
    res = _pcall(
        body, name=name, in_specs=[_HBM] * (n + nl) + [_ANY] * na,
        out_specs=[_SEM] * (2 * nc) + [_HBM] * (n + nl) + [pl.BlockSpec(memory_space=pltpu.VMEM)],
        out_shape=[pltpu.SemaphoreType.DMA(())] * (2 * nc) + [pltpu.HBM(s.shape, s.dtype) for s in srcs]
        + [pltpu.HBM(l.shape, l.dtype) for l in lands] + [S((SUBLANES, LANES), F32)],
        input_output_aliases={i: 2 * nc + i for i in range(n + nl)},
        compiler_params=pltpu.CompilerParams(has_side_effects=_EFFECT),
    )(*[hbm(s) for s in srcs], *[hbm(l) for l in lands], *after)
    return res[:2 * nc], res[2 * nc:2 * nc + n], res[2 * nc + n:2 * nc + n + nl], res[-1]


def _split_wait(name, sems, srcs, lands, after, copies_fn, nc):
    n, nl = len(srcs), len(lands)

    def body(*refs):
        src_refs, land_refs = refs[:n], refs[n:n + nl]
        sem_refs = refs[n + nl:n + nl + 2 * nc]
        for cp in _remote_copies(copies_fn, src_refs, land_refs, sem_refs[:nc], sem_refs[nc:]):
            cp.wait_send()
            cp.wait_recv()

    res = _pcall(
        body, name=name, in_specs=[_HBM] * (n + nl) + [_SEM] * (2 * nc) + [_ANY],
        out_specs=[_HBM] * (n + nl), out_shape=[pltpu.HBM(a.shape, a.dtype) for a in list(srcs) + list(lands)],
        input_output_aliases={i: i for i in range(n + nl)},
        compiler_params=pltpu.CompilerParams(has_side_effects=_EFFECT),
    )(*srcs, *lands, *sems, after)
    return res[:n], res[n:]


def _other_chips(x, y):
    return [(1 - x, y), (x, 1 - y), (1 - x, 1 - y)]


def _ag_copies(x, y, c, srcs, lands):
    out = []
    for land in lands:
        blk = land.at[4 * x + 2 * y + c]
        out.append((blk, blk, (x, y, 1 - c)))
        out += [(blk, blk, (px, py, c)) for px, py in _other_chips(x, y)]
    return out


def _rs_copies(x, y, c, srcs, lands):
    return [(s.at[2 * px + py], land.at[j], (px, py, c))
            for s, land in zip(srcs, lands) for j, (px, py) in enumerate(_other_chips(x, y))]


def _direct_copies(x, y, c, srcs, lands):
    out = []
    for s, land in zip(srcs, lands):
        for r in range(1, N_DEV):
            px = 1 - x if r & 4 else x
            py = 1 - y if r & 2 else y
            pc = 1 - c if r & 1 else c
            out.append((s.at[4 * px + 2 * py + pc], land.at[r - 1], (px, py, pc)))
    return out


def _ag_finish(name, lands):
    n = len(lands)

    def body(*refs):
        outs = refs[n:2 * n]
        send, recv = refs[2 * n:]
        x, y, c = _position()

        def swap(a, j, px, py, pc):
            blk = outs[a].at[4 * px + 2 * py + pc]
            return pltpu.make_async_remote_copy(src_ref=blk, dst_ref=blk, send_sem=send.at[a, j], recv_sem=recv.at[a, j],
                                                device_id=(x, y, 1 - c), device_id_type=MESH)

        chips = _other_chips(x, y)
        sends = [swap(a, j, px, py, c) for a in range(n) for j, (px, py) in enumerate(chips)]
        for cp in sends:
            cp.start()
        for a in range(n):
            for j, (px, py) in enumerate(chips):
                swap(a, j, px, py, 1 - c).wait_recv()
        for cp in sends:
            cp.wait_send()

    return _pcall(body, name=name, in_specs=[_ANY] * n, out_specs=[_ANY] * n,
                  out_shape=[S(l.shape, l.dtype) for l in lands], input_output_aliases={a: a for a in range(n)},
                  scratch_shapes=[pltpu.SemaphoreType.DMA((n, 3)), pltpu.SemaphoreType.DMA((n, 3))])(*lands)


def _pad_lanes(v):
    return jnp.pad(v, ((0, 0), (0, LANES - v.shape[1])))


def _flat_rows(pieces):
    flat = jnp.concatenate([p.reshape(-1) for p in pieces])
    rows = -(-flat.shape[0] // (SMALL_W * SUBLANES)) * SUBLANES
    return jnp.pad(flat, (0, rows * SMALL_W - flat.shape[0])).reshape(rows, SMALL_W)


def _unflat(buf, shapes):
    flat = buf.reshape(-1)
    out, off = [], 0
    for sh in shapes:
        n = 1
        for d in sh:
            n *= d
        out.append(flat[off:off + n].reshape(sh))
        off += n
    return out


def kernel(x, pre_mix_norm, w_in, ssd_conv_w, ssd_conv_b, ssd_dt_bias, ssd_a_log, ssd_d, ssd_norm, lru_conv_w, lru_conv_b, lru_w_a, lru_b_a, lru_w_x, lru_b_x, lru_lambda, lru_norm, w_out, post_mix_norm, pre_mlp_norm, w_mlp_in, w_mlp_out, post_mlp_norm, loss_target, m_pre_mix_norm, m_w_in, m_ssd_conv_w, m_ssd_conv_b, m_ssd_dt_bias, m_ssd_a_log, m_ssd_d, m_ssd_norm, m_lru_conv_w, m_lru_conv_b, m_lru_w_a, m_lru_b_a, m_lru_w_x, m_lru_b_x, m_lru_lambda, m_lru_norm, m_w_out, m_post_mix_norm, m_pre_mlp_norm, m_w_mlp_in, m_w_mlp_out, m_post_mlp_norm, v_pre_mix_norm, v_w_in, v_ssd_conv_w, v_ssd_conv_b, v_ssd_dt_bias, v_ssd_a_log, v_ssd_d, v_ssd_norm, v_lru_conv_w, v_lru_conv_b, v_lru_w_a, v_lru_b_a, v_lru_w_x, v_lru_b_x, v_lru_lambda, v_lru_norm, v_w_out, v_post_mix_norm, v_pre_mlp_norm, v_w_mlp_in, v_w_mlp_out, v_post_mlp_norm):
    names = ['pre_mix_norm', 'w_in', 'ssd_conv_w', 'ssd_conv_b', 'ssd_dt_bias', 'ssd_a_log', 'ssd_d', 'ssd_norm',
             'lru_conv_w', 'lru_conv_b', 'lru_w_a', 'lru_b_a', 'lru_w_x', 'lru_b_x', 'lru_lambda', 'lru_norm',
             'w_out', 'post_mix_norm', 'pre_mlp_norm', 'w_mlp_in', 'w_mlp_out', 'post_mlp_norm']
    loc = locals()
    W = {n: loc[n] for n in names}
    Mo = {n: loc["m_" + n] for n in names}
    Vo = {n: loc["v_" + n] for n in names}
    big = ['w_in', 'w_out', 'w_mlp_in', 'w_mlp_out']

    px, py, pc = _position()
    dev = 4 * px + 2 * py + pc
    dev_idx = jnp.reshape(dev, (1,)).astype(jnp.int32)
    c_idx = jnp.reshape(pc, (1,)).astype(jnp.int32)
    chip_idx = jnp.reshape(2 * px + py, (1,)).astype(jnp.int32)

    _, T, D = x.shape
    x2 = x.reshape(T, D)
    tgt = loss_target.reshape(T, D)
    n_heads = ssd_dt_bias.shape[1]
    XBC = ssd_conv_b.shape[1]
    GN = XBC // 4
    DS = XBC - 2 * GN
    DL = lru_norm.shape[1]
    DFF = w_mlp_in.shape[2] * N_DEV
    DIN = w_in.shape[2] * N_DEV
    NP = XBC + DS + 2 * DL + LANES
    assert DS % GN == 0 and XBC % DS == 0 and DS == DL and n_heads <= LANES
    cb_z, cb_gate, cb_xl, cb_dt = XBC // DS, XBC // DS + 1, XBC // DS + 2, (XBC + DS + 2 * DL) // LANES
    tr = min(256, T // 2)
    nt = T // tr
    Q = min(256, T // 2)

    Dh = D // 2
    sh_a = _cast_bf16("cast_w_in_a", W['w_in'], dev_idx, 0, Dh)
    sh_b = _cast_bf16("cast_w_in_b", W['w_in'], dev_idx, Dh, Dh)
    later = big[1:]
    sh = {n: _cast_bf16("cast_" + n, W[n], dev_idx) for n in later}
    g_in_a, g_cs, g_cl = _allgather(
        "allgather_w_in_a", [sh_a, _own_block(ssd_conv_w[0], dev), _own_block(lru_conv_w[0], dev)])
    b_sems, _, b_lands, b_token = _split_start("allgather_w_in_b_start", [], [sh_b], _ag_copies, 4, after=[g_in_a])
    ag_sems, ag_srcs, ag_lands, ag_token = _split_start(
        "allgather_later_start", [], [sh[n] for n in later], _ag_copies, 4 * len(later), after=[b_token])
    conv_s = jnp.transpose(g_cs, (1, 0, 2)).reshape(CONV_WIDTH, XBC)
    conv_l = jnp.transpose(g_cl, (1, 0, 2)).reshape(CONV_WIDTH, DL)
    wb = DIN // N_DEV
    o_z, o_xbc, o_dt, o_gate, o_xl = 0, DS, DS + XBC, DS + XBC + n_heads, DS + XBC + n_heads + DL
    segs = [(o_xbc, o_xbc + XBC, 0), (o_z, o_z + DS, XBC), (o_gate, o_gate + DL, XBC + DS),
            (o_xl, o_xl + DL, XBC + DS + DL), (o_dt, o_dt + n_heads, NP - LANES)]

    def ref_cols(g, lo, hi):
        out = []
        while lo < hi:
            k = lo // wb
            e = min(hi, (k + 1) * wb)
            out.append(g[k, :, lo - k * wb:e - k * wb])
            lo = e
        return out

    def laid_out(g):
        return jnp.concatenate([p for a, b, _ in segs for p in ref_cols(g, a, b)]
                               + [jnp.zeros((g.shape[1], LANES - n_heads), BF16)], axis=1)

    def my_cols(g, lo, hi):
        out = []
        for a, b, m in sorted(segs):
            s, e = max(lo, a), min(hi, b)
            if s < e:
                out.append(g[:, m + s - a:m + e - a])
        return out

    wp_a = laid_out(g_in_a)
    dt_bias = _pad_lanes(ssd_dt_bias)
    a_log = _pad_lanes(ssd_a_log)
    d_skip = _pad_lanes(ssd_d)
    wa_b, wx_b = lru_w_a[0].astype(BF16), lru_w_x[0].astype(BF16)
    b_a, b_x = lru_b_a.reshape(1, DL), lru_b_x.reshape(1, DL)

    def f_norm_in(first, last, xv, g):
        return (_rms(xv, g),), ()
    (h,) = _rows_call("norm_in", f_norm_in, nt, [x2, pre_mix_norm], [_rt(tr, D), _full(pre_mix_norm)],
                      [((T, D), BF16, _rt(tr, D))], [], 'tf', deps=[ag_token])

    (proj_a,) = _mm("proj_a", h, wp_a, a_cols=(0, Dh), outs=((BF16, None),))
    (dt_a,) = _mm("proj_dt_a", h, wp_a[:, NP - LANES:], a_cols=(0, Dh))
    _, b_lands = _split_wait("allgather_w_in_b_wait", b_sems, [], b_lands, proj_a, _ag_copies, 4)
    (g_in_b,) = _ag_finish("allgather_w_in_b_finish", b_lands)
    wp_b = laid_out(g_in_b)
    add = lambda r, e: r + e.astype(F32)
    (proj,) = _mm("proj_b", h, wp_b, a_cols=(Dh, Dh), extra=proj_a, outs=((BF16, add),))
    (dt_raw,) = _mm("proj_dt_b", h, wp_b[:, NP - LANES:], a_cols=(Dh, Dh), extra=dt_a, outs=((F32, add),))

    cwx = min(1024, XBC)

    def f_ssd_pre(first, last, xbc, halo, w, b):
        pre = _conv_pre(xbc, jnp.where(first, 0.0, halo), w, b)
        return (pre * jax.nn.sigmoid(pre),), ()
    (xbc_act,) = _rows_call(
        "ssd_pre", f_ssd_pre, nt, [proj, proj, conv_s, ssd_conv_b],
        [_rt(tr, XBC), _halo_prev(tr, XBC, rows=PACKED_ROWS), _full(conv_s), _full(ssd_conv_b)],
        [((T, XBC), F32, _rt(tr, XBC))], [], ['t', 'p0', 'f', 'f'], cw=cwx)

    def f_ssd_dt(first, last, dtr, dtb):
        return (jax.nn.softplus(dtr + dtb),), ()
    (dt,) = _rows_call("ssd_dt", f_ssd_dt, nt, [dt_raw, dt_bias], [_rt(tr, LANES), _full(dt_bias)],
                       [((T, LANES), F32, _rt(tr, LANES))], [], 'tf')

    y_ssd, h_prev = _ssd_fwd(xbc_act, dt, a_log, d_skip, n_heads, Q)

    gw = DS // SSD_GROUPS

    def ssd_post(y, z, g):
        yz = y * jax.nn.silu(z)
        parts = []
        for k in range(SSD_GROUPS):
            yk = yz[:, k * gw:(k + 1) * gw]
            parts.append(yk * lax.rsqrt(jnp.mean(yk * yk, axis=-1, keepdims=True) + EPS))
        return jnp.concatenate(parts, axis=-1) * g

    def f_ssd_post(first, last, y, z, g):
        return (ssd_post(y, z, g),), ()
    (mixcat,) = _rows_call("ssd_post", f_ssd_post, nt, [y_ssd, proj, ssd_norm],
                           [_rt(tr, DS), _rt(tr, DS, cb_z), _full(ssd_norm)], [((T, DS + DL), BF16, _rt(tr, DS))], [],
                           'ttf')

    def f_lru_pre(first, last, xv, halo, w, b):
        return (_conv_pre(xv, jnp.where(first, 0.0, halo), w, b),), ()
    (xl,) = _rows_call("lru_pre", f_lru_pre, nt, [proj, proj, conv_l, lru_conv_b],
                       [_rt(tr, DL, cb_xl), _halo_prev(tr, DL, cb_xl, rows=PACKED_ROWS), _full(conv_l),
                        _full(lru_conv_b)],
                       [((T, DL), F32, _rt(tr, DL))], [], ['t', 'p0', 'f', 'f'])

    a_lru, u_lru = _lru_gates_fwd(xl, wa_b, b_a, wx_b, b_x, lru_lambda, tr)
    h_lru = _lru_scan_fwd(a_lru, u_lru, tr)

    def lru_post(hv, gate, g):
        return _rms(hv * jax.nn.gelu(gate), g)

    def f_lru_post(first, last, hv, gate, g):
        return (lru_post(hv, gate, g),), ()
    cb_l = DS // DL
    (mixcat,) = _rows_call("lru_post", f_lru_post, nt, [h_lru, proj, lru_norm],
                           [_rt(tr, DL), _rt(tr, DL, cb_gate), _full(lru_norm)],
                           [((T, DS + DL), BF16, _rt(tr, DL, cb_l))], [], 'ttf', into=mixcat)

    ag_srcs, ag_lands = _split_wait("allgather_later_wait", ag_sems, ag_srcs, ag_lands, mixcat, _ag_copies,
                                    4 * len(later))
    g_out, g_mi, g_mo = _ag_finish("allgather_later_finish", ag_lands)
    w_out_f = g_out.reshape(DS + DL, D)
    w_mi_f = jnp.transpose(g_mi, (1, 0, 2)).reshape(D, DFF)
    w_mo_f = g_mo.reshape(DFF, D)
    (mix,) = _mm("mix", mixcat, w_out_f, outs=((BF16, None),))

    def f_post_mix(first, last, xv, mx, gpm, gpl):
        x1 = xv + _rms(mx, gpm)
        return (x1, _rms(x1, gpl)), ()
    x1, hn = _rows_call("post_mix", f_post_mix, nt, [x2, mix, post_mix_norm, pre_mlp_norm],
                        [_rt(tr, D), _rt(tr, D), _full(post_mix_norm), _full(pre_mlp_norm)],
                        [((T, D), F32, _rt(tr, D)), ((T, D), BF16, _rt(tr, D))], [], 'ttff')

    hm, act = _mm("mlp_in", hn, w_mi_f,
                  outs=((BF16, None), (BF16, lambda r, e: jnp.square(jnp.maximum(r, 0.0)))))
    (hm2,) = _mm("mlp_out", act, w_mo_f, outs=((BF16, None),))

    def f_final(first, last, x1v, hm2v, g, tg):
        def fwd(hv, gv):
            return x1v + _rms(hv, gv)
        x2v, vjp = jax.vjp(fwd, hm2v, g)
        err = x2v - tg
        dx2 = err * (1.0 / D)
        dh, dg = vjp(dx2)
        loss = jnp.full((1, LANES), 0.5 / D, F32) * jnp.sum(err * err)
        return (dx2, dh), (dg, loss)
    dx1a, dhm2, g_post_mlp, loss_part = _rows_call(
        "loss_head", f_final, nt, [x1, hm2, post_mlp_norm, tgt],
        [_rt(tr, D), _rt(tr, D), _full(post_mlp_norm), _rt(tr, D)],
        [((T, D), F32, _rt(tr, D)), ((T, D), BF16, _rt(tr, D))], [(1, D), (1, LANES)], 'ttft')

    def rs_begin(n, full):
        (from_sib,) = _rs_sibling("rs_sibling_" + n, [full])
        pair = _pair_sum("pair_sum_" + n, full, from_sib, c_idx)
        sems, srcs, lands, token = _split_start("rs_start_" + n, [pair], [lax.empty((3,) + pair.shape[1:], BF16)],
                                                _rs_copies, 3)
        return (sems, srcs, lands), token

    def rs_end(n, state, after):
        (pair,), (recv,) = _split_wait("rs_wait_" + n, *state, after, _rs_copies, 3)
        return pair, recv, chip_idx

    def rs_direct_begin(n, full):
        sems, srcs, lands, token = _split_start("rs_start_" + n, [full],
                                                [lax.empty((N_DEV - 1,) + full.shape[1:], BF16)], _direct_copies,
                                                N_DEV - 1)
        return (sems, srcs, lands), token

    def rs_direct_end(n, state, after):
        (full,), (recv,) = _split_wait("rs_wait_" + n, *state, after, _direct_copies, N_DEV - 1)
        return full, recv, dev_idx

    (gw_mo,) = _mm("dw_mlp_out", act, dhm2, ta=True, outs=((BF16, None),))
    rs_mo, tok = rs_direct_begin('w_mlp_out', gw_mo.reshape(N_DEV, DFF // N_DEV, D))
    (dhm,) = _mm("d_mlp_act", dhm2, w_mo_f, tb=True, extra=hm,
                 outs=((BF16, lambda r, e: r * (2.0 * jnp.maximum(e.astype(F32), 0.0))),), deps=[tok])
    (gw_mi,) = _mm("dw_mlp_in", hn, dhm, ta=True, outs=((BF16, None),), out_blocks=N_DEV)
    rs_mi, tok = rs_direct_begin('w_mlp_in', gw_mi)
    (dhn,) = _mm("d_mlp_in", dhm, w_mi_f, tb=True, outs=((BF16, None),), deps=[tok])

    def f_post_mix_bwd(first, last, x1v, mx, gpm, gpl, dhnv, dxa):
        _, vjp1 = jax.vjp(_rms, x1v, gpl)
        dx1, dgpl = vjp1(dhnv)
        dx1 = dx1 + dxa
        _, vjp2 = jax.vjp(_rms, mx, gpm)
        dmx, dgpm = vjp2(dx1)
        return (dx1, dmx), (dgpl, dgpm)
    dx1, dmix, g_pre_mlp, g_post_mix = _rows_call(
        "post_mix_bwd", f_post_mix_bwd, nt, [x1, mix, post_mix_norm, pre_mlp_norm, dhn, dx1a],
        [_rt(tr, D), _rt(tr, D), _full(post_mix_norm), _full(pre_mlp_norm), _rt(tr, D), _rt(tr, D)],
        [((T, D), F32, _rt(tr, D)), ((T, D), BF16, _rt(tr, D))], [(1, D), (1, D)], 'ttfftt')

    (gw_out,) = _mm("dw_out", mixcat, dmix, ta=True, outs=((BF16, None),))
    rs_out, tok = rs_direct_begin('w_out', gw_out.reshape(N_DEV, -1, D))
    (dmixcat,) = _mm("d_mix", dmix, w_out_f, tb=True, outs=((BF16, None),), deps=[tok])

    def f_lru_post_bwd(first, last, hv, gate, g, dy):
        _, vjp = jax.vjp(lru_post, hv, gate, g)
        dh_, dgate, dg = vjp(dy)
        return (dgate, dh_), (dg,)
    dproj, dh_lru, g_lru_norm = _rows_call(
        "lru_post_bwd", f_lru_post_bwd, nt, [h_lru, proj, lru_norm, dmixcat],
        [_rt(tr, DL), _rt(tr, DL, cb_gate), _full(lru_norm), _rt(tr, DL, cb_l)],
        [((T, NP), BF16, _rt(tr, DL, cb_gate)), ((T, DL), F32, _rt(tr, DL))], [(1, DL)], 'ttft')

    du_lru, da_lru = _lru_scan_bwd(a_lru, h_lru, dh_lru, tr)
    dxl, g_wa, g_ba, g_wx, g_bx, g_lam = _lru_gates_bwd(xl, wa_b, b_a, wx_b, b_x, lru_lambda, da_lru, du_lru, tr)

    conv_bwd_kinds = ['t', 'p0', 'n0', 't', 'n3', 'f', 'f']

    def f_lru_pre_bwd(first, last, xv, hp, xn, d, dn, w, b):
        dx, dw8, db = _conv_bwd_tile(first, last, xv, hp, xn, d, dn, w, b, silu=False)
        return (dx,), (dw8, db)
    dproj, g_convl8, g_convl_b = _rows_call(
        "lru_pre_bwd", f_lru_pre_bwd, nt, [proj, proj, proj, dxl, dxl, conv_l, lru_conv_b],
        [_rt(tr, DL, cb_xl), _halo_prev(tr, DL, cb_xl, rows=PACKED_ROWS),
         _halo_next(tr, DL, nt, cb_xl, rows=PACKED_ROWS), _rt(tr, DL),
         _halo_next(tr, DL, nt), _full(conv_l), _full(lru_conv_b)],
        [((T, NP), BF16, _rt(tr, DL, cb_xl))], [(SUBLANES, DL), (1, DL)], conv_bwd_kinds, into=dproj)

    def f_ssd_post_bwd(first, last, y, z, g, dy):
        _, vjp = jax.vjp(ssd_post, y, z, g)
        dy_, dz, dg = vjp(dy)
        return (dz, dy_), (dg,)
    dproj, dy_ssd, g_ssd_norm = _rows_call(
        "ssd_post_bwd", f_ssd_post_bwd, nt, [y_ssd, proj, ssd_norm, dmixcat],
        [_rt(tr, DS), _rt(tr, DS, cb_z), _full(ssd_norm), _rt(tr, DS, 0)],
        [((T, NP), BF16, _rt(tr, DS, cb_z)), ((T, DS), F32, _rt(tr, DS))], [(1, DS)], 'ttft', into=dproj)

    dxbc_act, ddt, g_alog, g_dskip = _ssd_bwd(xbc_act, dt, a_log, d_skip, h_prev, dy_ssd, n_heads, Q)

    def f_ssd_pre_bwd(first, last, xv, hp, xn, d, dn, w, b):
        dx, dw8, db = _conv_bwd_tile(first, last, xv, hp, xn, d, dn, w, b, silu=True)
        return (dx,), (dw8, db)
    dproj, g_convs8, g_convs_b = _rows_call(
        "ssd_pre_bwd", f_ssd_pre_bwd, nt, [proj, proj, proj, dxbc_act, dxbc_act, conv_s, ssd_conv_b],
        [_rt(tr, XBC), _halo_prev(tr, XBC, rows=PACKED_ROWS), _halo_next(tr, XBC, nt, rows=PACKED_ROWS),
         _rt(tr, XBC), _halo_next(tr, XBC, nt),
         _full(conv_s), _full(ssd_conv_b)],
        [((T, NP), BF16, _rt(tr, XBC))], [(SUBLANES, XBC), (1, XBC)], conv_bwd_kinds, into=dproj, cw=cwx)

    def f_ssd_dt_bwd(first, last, ddtv, dtr, dtb):
        ddtr = ddtv * jax.nn.sigmoid(dtr + dtb)
        return (ddtr,), (_colsum(ddtr),)
    dproj, g_dtb = _rows_call(
        "ssd_dt_bwd", f_ssd_dt_bwd, nt, [ddt, dt_raw, dt_bias],
        [_rt(tr, LANES), _rt(tr, LANES), _full(dt_bias)],
        [((T, NP), BF16, _rt(tr, LANES, cb_dt))], [(1, LANES)], 'ttf', into=dproj)
    small = {
        'ssd_conv_w': g_convs8[:CONV_WIDTH], 'ssd_conv_b': g_convs_b,
        'ssd_dt_bias': g_dtb[:, :n_heads], 'ssd_a_log': g_alog[:, :n_heads], 'ssd_d': g_dskip[:, :n_heads],
        'ssd_norm': g_ssd_norm, 'lru_conv_w': g_convl8[:CONV_WIDTH], 'lru_conv_b': g_convl_b,
        'lru_w_a': g_wa, 'lru_b_a': g_ba, 'lru_w_x': g_wx, 'lru_b_x': g_bx, 'lru_lambda': g_lam,
        'lru_norm': g_lru_norm, 'post_mix_norm': g_post_mix, 'pre_mlp_norm': g_pre_mlp,
        'post_mlp_norm': g_post_mlp, 'loss': loss_part[:, :1],
    }
    wide = ['lru_w_a', 'lru_w_x']
    narrow = [n for n in small if n not in wide]
    lb = lru_w_a.shape[-1]
    s_srcs = [_flat_rows([small[n] for n in narrow]), g_wa.reshape(-1, lb), g_wx.reshape(-1, lb)]
    s_sems, s_srcs, s_lands, tok = _split_start(
        "small_grads_start", [], [_own_block(a, dev) for a in s_srcs], _ag_copies, 4 * len(s_srcs))

    (gwp,) = _mm("dw_proj", h, dproj, ta=True, outs=((BF16, None),), deps=[tok])
    rs_in, tok = rs_begin(
        'w_in', jnp.stack([jnp.concatenate(my_cols(gwp, k * wb, (k + 1) * wb), axis=1) for k in range(N_DEV)]))
    (dh_a,) = _mm("d_proj_a", dproj, wp_a, tb=True, outs=((BF16, None),), deps=[tok])
    (dh_b,) = _mm("d_proj_b", dproj, wp_b, tb=True, outs=((BF16, None),), deps=[tok])

    def f_norm_in_bwd(first, last, xv, g, dha, dhb, dxa):
        _, vjp = jax.vjp(_rms, xv, g)
        dx, dg = vjp(jnp.concatenate([dha, dhb], axis=1))
        return (dx + dxa,), (dg,)
    grad_x, g_pre_mix = _rows_call(
        "norm_in_bwd", f_norm_in_bwd, nt, [x2, pre_mix_norm, dh_a, dh_b, dx1],
        [_rt(tr, D), _full(pre_mix_norm), _rt(tr, Dh), _rt(tr, Dh), _rt(tr, D)], [((T, D), F32, _rt(tr, D))],
        [(1, D)], 'tfttt')

    big_out = {}
    for n, state in (('w_mlp_out', rs_mo), ('w_mlp_in', rs_mi), ('w_out', rs_out)):
        big_out[n] = _adamw_big("adamw_" + n, W[n], Mo[n], Vo[n], *rs_direct_end(n, state, grad_x))

    (g_pm8,) = _allgather("allgather_pre_mix_grad", [_own_block(g_pre_mix, dev)], deps=[big_out['w_out'][0]])
    _, s_lands = _split_wait("small_grads_wait", s_sems, s_srcs, s_lands, g_pm8, _ag_copies, 4 * len(s_lands))
    g_narrow, g_wa8, g_wx8 = _ag_finish("small_grads_finish", s_lands)
    summed = dict(zip(narrow, _unflat(_sum8("sum_small_grads", g_narrow), [small[n].shape for n in narrow])))
    summed['pre_mix_norm'] = _sum8("sum_pre_mix_grad", g_pm8)
    summed['lru_w_a'] = _sum8("sum_lru_w_a_grads", g_wa8)
    summed['lru_w_x'] = _sum8("sum_lru_w_x_grads", g_wx8)
    loss = summed.pop('loss').reshape(())
    for n, full_w in (('ssd_conv_w', XBC), ('lru_conv_w', DL)):
        wdt = full_w // N_DEV
        summed[n] = lax.dynamic_slice_in_dim(summed[n], dev * wdt, wdt, axis=1)
    small_params = [n for n in names if n not in big]
    as2d = lambda a: a.reshape(-1, a.shape[-1])
    res = _adamw_small([as2d(W[n]) for n in small_params],
                       [summed[n].reshape(as2d(W[n]).shape) for n in small_params],
                       [as2d(Mo[n]) for n in small_params], [as2d(Vo[n]) for n in small_params])
    grads = {n: summed[n].reshape(W[n].shape) for n in small_params}
    delta, new_m, new_v = ({n: r.reshape(W[n].shape) for n, r in zip(small_params, rs)} for rs in res)

    big_out['w_in'] = _adamw_big("adamw_w_in", W['w_in'], Mo['w_in'], Vo['w_in'], *rs_end('w_in', rs_in, g_pm8))
    for n in big:
        grads[n], delta[n], new_m[n], new_v[n] = big_out[n]

    return (loss, grad_x.reshape(x.shape), *[grads[n] for n in names], *[delta[n] for n in names],
            *[new_m[n] for n in names], *[new_v[n] for n in names])
```

```python
import functools

import jax
import jax.numpy as jnp
from jax import lax
from jax.experimental import pallas as pl
from jax.experimental.pallas import tpu as pltpu

F32, BF16 = jnp.float32, jnp.bfloat16
S = jax.ShapeDtypeStruct
MESH = pl.DeviceIdType.MESH

SSD_GROUPS = 8
LRU_C = 8.0
EPS = 1e-6
CONV_WIDTH = 4
ADAM_LR, ADAM_B1, ADAM_B2, ADAM_EPS, ADAM_WD, ADAM_STEP = 0.001, 0.9, 0.999, 1e-08, 0.01, 10

LANES = 128
SUBLANES = 8
VMEM_LIMIT = 56 * 1024 * 1024
N_DEV = 8
SMALL_W = 512
HI = lax.Precision.HIGHEST


def _pcall(body, **kw):
    return pl.pallas_call(body, **kw)


def _cparams(sem=None, **kw):
    return pltpu.CompilerParams(dimension_semantics=sem, vmem_limit_bytes=VMEM_LIMIT, **kw)


def _pick(n, cands):
    for c in cands:
        if c <= n and n % c == 0:
            return c
    return n


def _rt(tr, w, cb=0, n=None):
    if n is None:
        return pl.BlockSpec((tr, w), lambda i: (i, cb))
    return pl.BlockSpec((tr, w), lambda i: (n - 1 - i, cb))


PACKED_ROWS = 16


def _halo_prev(tr, w, cb=0, n=None, rows=SUBLANES):
    k = tr // rows
    if n is None:
        return pl.BlockSpec((rows, w), lambda i: (jnp.maximum(i * k - 1, 0), cb))
    return pl.BlockSpec((rows, w), lambda i: (jnp.maximum((n - 1 - i) * k - 1, 0), cb))


def _halo_next(tr, w, nt, cb=0, n=None, rows=SUBLANES):
    k = tr // rows
    last = nt * k - 1
    if n is None:
        return pl.BlockSpec((rows, w), lambda i: (jnp.minimum((i + 1) * k, last), cb))
    return pl.BlockSpec((rows, w), lambda i: (jnp.minimum((n - i) * k, last), cb))


def _full(a):
    nd = a.ndim
    return pl.BlockSpec(a.shape, lambda i: (0,) * nd)


def _rows_call(name, fn, n_tiles, arrays, in_specs, out_tiled, out_acc, kinds, into=None, deps=(), cw=None):
    n_in, n_t = len(arrays), len(out_tiled)
    n_skip = len(deps) + (0 if into is None else 1)
    width = in_specs[kinds.index('t')].block_shape[1]
    cols = [(0, width)] if cw is None else [(c, cw) for c in range(0, width, cw)]

    def body(*refs):
        i = pl.program_id(0)
        ins = refs[:n_in]
        outs = refs[n_in + n_skip:n_in + n_skip + n_t]
        accs = refs[n_in + n_skip + n_t:]
        if accs:
            @pl.when(i == 0)
            def _():
                for r in accs:
                    r[...] = jnp.zeros_like(r)

        def lanes(ref, rows, c0, w):
            return ref[rows, c0:c0 + w] if ref.shape[-1] == width else ref[rows, :]

        def load(k, c0, w):
            v = lanes(ins[k], slice(None), c0, w).astype(F32)
            if kinds[k][0] in 'pn' and v.shape[0] == PACKED_ROWS:
                v = v[SUBLANES:] if kinds[k][0] == 'p' else v[:SUBLANES]
            return v

        for c0, w in cols:
            touts, aouts = fn(i == 0, i == n_tiles - 1, *[load(k, c0, w) for k in range(n_in)])
            for r, v in zip(outs, touts):
                if r.shape[-1] == width:
                    r[:, c0:c0 + w] = v.astype(r.dtype)
                else:
                    r[...] = v.astype(r.dtype)
            for r, v in zip(accs, aouts):
                if r.shape[-1] == width:
                    r[:, c0:c0 + w] += v
                else:
                    r[...] += v

    out_shape = [S(sh, dt) for sh, dt, _ in out_tiled] + [S(sh, F32) for sh in out_acc]
    out_specs = [sp for _, _, sp in out_tiled]
    for sh in out_acc:
        out_specs.append(pl.BlockSpec(sh, lambda i, nd=len(sh): (0,) * nd))
    in_specs = list(in_specs) + [_ANY] * len(deps)
    if into is None:
        return _pcall(body, name=name, grid=(n_tiles,), in_specs=in_specs, out_specs=out_specs,
                      out_shape=out_shape, compiler_params=_cparams(("arbitrary",)))(*arrays, *deps)
    return _pcall(body, name=name, grid=(n_tiles,), in_specs=in_specs + [_ANY], out_specs=out_specs,
                  out_shape=out_shape, input_output_aliases={n_in + len(deps): 0},
                  compiler_params=_cparams(("arbitrary",)))(*arrays, *deps, into)


def _rms(x, g):
    return x * lax.rsqrt(jnp.mean(x * x, axis=-1, keepdims=True) + EPS) * g


def _colsum(v):
    return jnp.sum(v, axis=0, keepdims=True)


_TILES = (1152, 1024, 896, 768, 640, 512, 384, 256, 128)
_K_TILES = (4096, 3456, 3072, 2688, 2048, 1536, 1344, 1152, 1024, 896, 768, 640, 512, 384, 256, 128)


def _mm(name, a, b, *, ta=False, tb=False, outs=((F32, None),), extra=None, out_blocks=None, tm=None, tn=None, tk=None,
        deps=(), a_cols=None):
    M, K = (a.shape[1], a.shape[0]) if ta else a.shape
    if a_cols is not None:
        assert not ta
        K = a_cols[1]
    b3 = b.ndim == 3
    if b3:
        nb_b, brows, bcols = b.shape
        N = brows if tb else nb_b * bcols
    else:
        N = b.shape[0] if tb else b.shape[1]
    n_lim = N if out_blocks is None else N // out_blocks
    if b3 and not tb:
        n_lim = min(n_lim, bcols)
    tm = tm or _pick(M, _TILES[1:])
    tn = tn or _pick(n_lim, _TILES)
    tk = tk or _pick(bcols if (b3 and tb) else K, _K_TILES)
    nk = K // tk
    assert M % tm == 0 and N % tn == 0 and K % tk == 0
    dn = (((0 if ta else 1,), (1 if tb else 0,)), ((), ()))
    n_extra = 0 if extra is None else 1
    n_out = len(outs)

    def body(*refs):
        a_ref, b_ref = refs[0], refs[1]
        e_ref = refs[2] if n_extra else None
        o_refs = refs[2 + n_extra + len(deps):2 + n_extra + len(deps) + n_out]

        def finish(r):
            e = e_ref[...] if n_extra else None
            for o, (_, f) in zip(o_refs, outs):
                o[...] = (r if f is None else f(r, e)).astype(o.dtype)

        part = lax.dot_general(a_ref[...], b_ref[...], dn, preferred_element_type=F32)
        if nk == 1:
            finish(part)
            return
        acc = refs[-1]
        k = pl.program_id(2)

        @pl.when(k == 0)
        def _():
            acc[...] = part

        @pl.when(jnp.logical_and(k > 0, k < nk - 1))
        def _():
            acc[...] += part

        @pl.when(k == nk - 1)
        def _():
            finish(acc[...] + part)

    k0 = 0 if a_cols is None else a_cols[0] // tk
    a_spec = (pl.BlockSpec((tk, tm), lambda i, j, k: (k, i)) if ta
              else pl.BlockSpec((tm, tk), lambda i, j, k: (i, k + k0)))
    if not b3:
        b_spec = pl.BlockSpec((tn, tk), lambda i, j, k: (j, k)) if tb else pl.BlockSpec((tk, tn), lambda i, j, k: (k, j))
    elif tb:
        per = bcols // tk
        b_spec = pl.BlockSpec((None, tn, tk), lambda i, j, k: (k // per, j, k % per))
    else:
        per = bcols // tn
        b_spec = pl.BlockSpec((None, tk, tn), lambda i, j, k: (j // per, k, j % per))
    o_spec = pl.BlockSpec((tm, tn), lambda i, j, k: (i, j))
    if out_blocks is None:
        out_specs, out_shape = [o_spec] * n_out, [S((M, N), dt) for dt, _ in outs]
    else:
        per_o = N // out_blocks // tn
        ob_spec = pl.BlockSpec((None, tm, tn), lambda i, j, k: (j // per_o, i, j % per_o))
        out_specs, out_shape = [ob_spec] * n_out, [S((out_blocks, M, N // out_blocks), dt) for dt, _ in outs]
    in_specs = [a_spec, b_spec] + ([o_spec] if n_extra else []) + [_ANY] * len(deps)
    args = [a, b] + ([extra] if n_extra else []) + list(deps)
    return _pcall(body, name=name, grid=(M // tm, N // tn, nk), in_specs=in_specs, out_specs=out_specs,
                  out_shape=out_shape, scratch_shapes=[pltpu.VMEM((tm, tn), F32)] if nk > 1 else [],
                  compiler_params=_cparams(("parallel", "parallel", "arbitrary")))(*args)


def _shift_down(x, halo, s):
    if s == 0:
        return x
    r = pltpu.roll(x, s, 0)
    hr = pltpu.roll(halo, s, 0)
    row = lax.broadcasted_iota(jnp.int32, halo.shape, 0)
    top = jnp.where(row < s, hr, r[:SUBLANES])
    if x.shape[0] == SUBLANES:
        return top
    return jnp.concatenate([top, r[SUBLANES:]], axis=0)


def _shift_up(x, nxt, s):
    if s == 0:
        return x
    n = x.shape[0]
    r = pltpu.roll(x, n - s, 0)
    nr = pltpu.roll(nxt, SUBLANES - s, 0)
    row = lax.broadcasted_iota(jnp.int32, nxt.shape, 0)
    bot = jnp.where(row >= SUBLANES - s, nr, r[n - SUBLANES:])
    if n == SUBLANES:
        return bot
    return jnp.concatenate([r[:n - SUBLANES], bot], axis=0)


def _conv_pre(x, halo, w, b):
    acc = b + w[CONV_WIDTH - 1:CONV_WIDTH, :] * x
    for k in range(CONV_WIDTH - 1):
        acc = acc + w[k:k + 1, :] * _shift_down(x, halo, CONV_WIDTH - 1 - k)
    return acc


def _silu_grad(p):
    s = jax.nn.sigmoid(p)
    return s * (1.0 + p * (1.0 - s))


def _conv_bwd_tile(first, last, x, hprev, xnext, d, dnext, w, b, silu):
    hprev = jnp.where(first, 0.0, hprev)
    if silu:
        d = d * _silu_grad(_conv_pre(x, hprev, w, b))
        pre_next = _conv_pre(xnext, x[x.shape[0] - SUBLANES:], w, b)
        dnext = dnext * _silu_grad(pre_next)
    dnext = jnp.where(last, 0.0, dnext)
    dx = w[CONV_WIDTH - 1:CONV_WIDTH, :] * d
    row8 = lax.broadcasted_iota(jnp.int32, (SUBLANES, x.shape[1]), 0)
    dw8 = jnp.where(row8 == CONV_WIDTH - 1, _colsum(d * x), 0.0)
    for k in range(CONV_WIDTH - 1):
        s = CONV_WIDTH - 1 - k
        dx = dx + w[k:k + 1, :] * _shift_up(d, dnext, s)
        dw8 = dw8 + jnp.where(row8 == k, _colsum(d * _shift_down(x, hprev, s)), 0.0)
    return dx, dw8, _colsum(d)


def _ssd_dims(xbc_act, n_heads):
    T, XBC = xbc_act.shape
    GN = XBC // 4
    DS = XBC - 2 * GN
    G = SSD_GROUPS
    N = GN // G
    P = DS // n_heads
    K = n_heads // G
    return T, XBC, DS, GN, G, N, P, K


def _ssd_common(dt, alog, Q):
    a = -jnp.exp(alog)
    adt = dt * a
    li = lax.broadcasted_iota(jnp.int32, (Q, Q), 0)
    si = lax.broadcasted_iota(jnp.int32, (Q, Q), 1)
    causal = li >= si
    ltri = causal.astype(F32)
    acs = jnp.dot(ltri, adt, precision=HI, preferred_element_type=F32)
    acs_row = lax.dot_general(adt, ltri, (((0,), (1,)), ((), ())), precision=HI,
                              preferred_element_type=F32)
    return a, adt, causal, ltri, acs, acs_row


def _expander(g, K, P, W):
    r = lax.broadcasted_iota(jnp.int32, (LANES, W), 0)
    c = lax.broadcasted_iota(jnp.int32, (LANES, W), 1)
    return (c // P + g * K == r).astype(F32)


def _dotb(a, b, dn=(((1,), (0,)), ((), ()))):
    return lax.dot_general(a.astype(BF16), b.astype(BF16), dn, preferred_element_type=F32)


def _dot_split(a, sel, terms, dn=(((1,), (0,)), ((), ()))):
    selb = sel.astype(BF16)
    out = None
    for _ in range(terms):
        piece = a.astype(BF16)
        part = lax.dot_general(piece, selb, dn, preferred_element_type=F32)
        out = part if out is None else out + part
        a = a - piece.astype(F32)
    return out


_NT = (((1,), (1,)), ((), ()))
_TN = (((0,), (0,)), ((), ()))


def _ssd_fwd(xbc_act, dt, alog, dskip, n_heads, Q):
    T, XBC, DS, GN, G, N, P, K = _ssd_dims(xbc_act, n_heads)
    W = K * P
    nc = T // Q

    def body(xs_ref, b_ref, c_ref, dt_ref, alog_ref, d_ref, y_ref, hp_ref, h_scr):
        ci = pl.program_id(0)

        @pl.when(ci == 0)
        def _():
            h_scr[...] = jnp.zeros_like(h_scr)

        dtv = dt_ref[...]
        a, adt, causal, ltri, acs, acs_row = _ssd_common(dtv, alog_ref[...], Q)
        lane_head = lax.broadcasted_iota(jnp.int32, (Q, W), 1) // P
        for g in range(G):
            eg = _expander(g, K, P, W)
            dtb = _dot_split(dtv, eg, 3)
            acsb = _dot_split(acs, eg, 3)
            lastb = acsb[Q - 1:Q, :]
            db = _dot_split(jnp.broadcast_to(d_ref[...], (SUBLANES, LANES)), eg, 3)[0:1, :]
            xg = xs_ref[:, g * W:(g + 1) * W]
            bg = b_ref[:, g * N:(g + 1) * N]
            cg = c_ref[:, g * N:(g + 1) * N]
            xt = xg * dtb
            sc = _dotb(cg, bg, _NT)
            yd = jnp.zeros((Q, W), F32)
            for k in range(K):
                h = g * K + k
                seg = acs[:, h:h + 1] - acs_row[h:h + 1, :]
                lh = jnp.where(causal, jnp.exp(jnp.minimum(seg, 0.0)), 0.0)
                xk = jnp.where(lane_head == k, xt, 0.0)
                yd = yd + _dotb(sc * lh, xk)
            hp = h_scr[g]
            yoff = _dotb(cg, hp) * jnp.exp(acsb)
            y_ref[:, g * W:(g + 1) * W] = yd + yoff + xg * db
            e_end = jnp.exp(lastb - acsb)
            st = _dotb(bg, xt * e_end, _TN)
            hp_ref[0, g] = hp
            h_scr[g] = jnp.exp(lastb) * hp + st

    cb = DS // GN
    in_specs = [pl.BlockSpec((Q, DS), lambda c: (c, 0)),
                pl.BlockSpec((Q, GN), lambda c: (c, cb)),
                pl.BlockSpec((Q, GN), lambda c: (c, cb + 1)),
                pl.BlockSpec((Q, LANES), lambda c: (c, 0)),
                pl.BlockSpec((1, LANES), lambda c: (0, 0)),
                pl.BlockSpec((1, LANES), lambda c: (0, 0))]
    out_specs = [pl.BlockSpec((Q, DS), lambda c: (c, 0)),
                 pl.BlockSpec((1, G, N, W), lambda c: (c, 0, 0, 0))]
    return _pcall(body, name="ssd_fwd", grid=(nc,), in_specs=in_specs, out_specs=out_specs,
                  out_shape=[S((T, DS), F32), S((nc, G, N, W), F32)],
                  scratch_shapes=[pltpu.VMEM((G, N, W), F32)],
                  compiler_params=_cparams(("arbitrary",)))(xbc_act, xbc_act, xbc_act, dt, alog, dskip)


def _ssd_bwd(xbc_act, dt, alog, dskip, hprev, dy, n_heads, Q):
    T, XBC, DS, GN, G, N, P, K = _ssd_dims(xbc_act, n_heads)
    W = K * P
    nc = T // Q

    def body(xs_ref, b_ref, c_ref, dt_ref, alog_ref, d_ref, hp_ref, dy_ref,
             dxbc_ref, ddt_ref, dalog_ref, dd_ref, dh_scr):
        ci = pl.program_id(0)

        @pl.when(ci == 0)
        def _():
            dh_scr[...] = jnp.zeros_like(dh_scr)
            dalog_ref[...] = jnp.zeros_like(dalog_ref)
            dd_ref[...] = jnp.zeros_like(dd_ref)

        dtv = dt_ref[...]
        a, adt, causal, ltri, acs, acs_row = _ssd_common(dtv, alog_ref[...], Q)
        lane_head = lax.broadcasted_iota(jnp.int32, (Q, W), 1) // P
        lane128 = lax.broadcasted_iota(jnp.int32, (Q, LANES), 1)
        sub128 = lax.broadcasted_iota(jnp.int32, (LANES, Q), 0)
        rowq = lax.broadcasted_iota(jnp.int32, (Q, W), 0)
        dacs = jnp.zeros((Q, LANES), F32)
        dacs_row = jnp.zeros((LANES, Q), F32)
        ddt = jnp.zeros((Q, LANES), F32)
        dd_acc = jnp.zeros((1, LANES), F32)
        for g in range(G):
            eg = _expander(g, K, P, W)
            dtb = _dot_split(dtv, eg, 3)
            acsb = _dot_split(acs, eg, 3)
            lastb = acsb[Q - 1:Q, :]
            db = _dot_split(jnp.broadcast_to(d_ref[...], (SUBLANES, LANES)), eg, 3)[0:1, :]
            xg = xs_ref[:, g * W:(g + 1) * W]
            bg = b_ref[:, g * N:(g + 1) * N]
            cg = c_ref[:, g * N:(g + 1) * N]
            dyg = dy_ref[:, g * W:(g + 1) * W]
            hp = hp_ref[0, g]
            dhn = dh_scr[g]
            xt = xg * dtb
            sc = _dotb(cg, bg, _NT)
            eacs = jnp.exp(acsb)
            e_end = jnp.exp(lastb - acsb)
            elast = jnp.exp(lastb)

            wv = dyg * eacs
            dcg = _dotb(wv, hp, _NT)
            dhp = _dotb(cg, wv, _TN) + elast * dhn
            dacsb = dyg * (_dotb(cg, hp) * eacs)

            xe = xt * e_end
            dbg = _dotb(xe, dhn, _NT)
            v = _dotb(bg, dhn)
            dxt = v * e_end
            de = v * xe
            dacsb = dacsb - de
            dlastb = _colsum(de) + elast * jnp.sum(dhn * hp, axis=0, keepdims=True)

            dsc = jnp.zeros((Q, Q), F32)
            for k in range(K):
                h = g * K + k
                seg = acs[:, h:h + 1] - acs_row[h:h + 1, :]
                lh = jnp.where(causal, jnp.exp(jnp.minimum(seg, 0.0)), 0.0)
                mh = sc * lh
                dyk = jnp.where(lane_head == k, dyg, 0.0)
                dxt = dxt + jnp.where(lane_head == k, _dotb(mh, dyg, _TN), 0.0)
                dm = _dotb(dyk, xt, _NT)
                dsc = dsc + dm * lh
                gm = dm * mh
                dacs = dacs + jnp.where(lane128 == h, jnp.sum(gm, axis=1, keepdims=True), 0.0)
                dacs_row = dacs_row - jnp.where(sub128 == h, jnp.sum(gm, axis=0, keepdims=True), 0.0)
            dcg = dcg + _dotb(dsc, bg)
            dbg = dbg + _dotb(dsc, cg, _TN)

            dacsb = dacsb + jnp.where(rowq == Q - 1, dlastb, 0.0)
            dacs = dacs + _dot_split(dacsb, eg, 2, _NT)
            ddt = ddt + _dot_split(dxt * xg, eg, 2, _NT)
            dd_acc = dd_acc + _dot_split(jnp.broadcast_to(_colsum(dyg * xg), (SUBLANES, W)), eg, 2, _NT)[0:1, :]
            dxbc_ref[:, g * W:(g + 1) * W] = dxt * dtb + dyg * db
            dxbc_ref[:, DS + g * N:DS + (g + 1) * N] = dbg
            dxbc_ref[:, DS + GN + g * N:DS + GN + (g + 1) * N] = dcg
            dh_scr[g] = dhp

        eye = (lax.broadcasted_iota(jnp.int32, (LANES, LANES), 0) ==
               lax.broadcasted_iota(jnp.int32, (LANES, LANES), 1)).astype(F32)
        dacs = dacs + lax.dot_general(dacs_row, eye, _TN, precision=HI, preferred_element_type=F32)
        dadt = lax.dot_general(ltri, dacs, _TN, precision=HI, preferred_element_type=F32)
        ddt_ref[...] = ddt + dadt * a
        dalog_ref[...] += _colsum(dadt * dtv) * a
        dd_ref[...] += dd_acc

    cb = DS // GN
    rv = lambda c: nc - 1 - c
    in_specs = [pl.BlockSpec((Q, DS), lambda c: (rv(c), 0)),
                pl.BlockSpec((Q, GN), lambda c: (rv(c), cb)),
                pl.BlockSpec((Q, GN), lambda c: (rv(c), cb + 1)),
                pl.BlockSpec((Q, LANES), lambda c: (rv(c), 0)),
                pl.BlockSpec((1, LANES), lambda c: (0, 0)),
                pl.BlockSpec((1, LANES), lambda c: (0, 0)),
                pl.BlockSpec((1, G, N, W), lambda c: (rv(c), 0, 0, 0)),
                pl.BlockSpec((Q, DS), lambda c: (rv(c), 0))]
    out_specs = [pl.BlockSpec((Q, XBC), lambda c: (rv(c), 0)),
                 pl.BlockSpec((Q, LANES), lambda c: (rv(c), 0)),
                 pl.BlockSpec((1, LANES), lambda c: (0, 0)),
                 pl.BlockSpec((1, LANES), lambda c: (0, 0))]
    return _pcall(body, name="ssd_bwd", grid=(nc,), in_specs=in_specs, out_specs=out_specs,
                  out_shape=[S((T, XBC), F32), S((T, LANES), F32), S((1, LANES), F32), S((1, LANES), F32)],
                  scratch_shapes=[pltpu.VMEM((G, N, W), F32)],
                  compiler_params=_cparams(("arbitrary",)))(
                      xbc_act, xbc_act, xbc_act, dt, alog, dskip, hprev, dy)


def _blockdiag(x, w_ref, dn=(((1,), (0,)), ((), ()))):
    H, B, _ = w_ref.shape
    return jnp.concatenate([_dotb(x[:, h * B:(h + 1) * B], w_ref[h], dn) for h in range(H)], axis=1)


def _lru_elem(xl, r_pre, i_pre, lam):
    r = jax.nn.sigmoid(r_pre)
    i = jax.nn.sigmoid(i_pre)
    log_a = -LRU_C * r * jax.nn.softplus(-lam)
    a = jnp.exp(log_a)
    u = jnp.sqrt(1.0 - jnp.exp(2.0 * log_a)) * (i * xl)
    return a, u


def _lru_gates_fwd(xl, w_a, b_a, w_x, b_x, lam, tr):
    T, DL = xl.shape

    def body(xl_ref, wa_ref, ba_ref, wx_ref, bx_ref, lam_ref, a_ref, u_ref):
        x = xl_ref[...]
        r_pre = _blockdiag(x, wa_ref) + ba_ref[...]
        i_pre = _blockdiag(x, wx_ref) + bx_ref[...]
        a, u = _lru_elem(x, r_pre, i_pre, lam_ref[...])
        a_ref[...] = a
        u_ref[...] = u

    w3 = pl.BlockSpec(w_a.shape, lambda i: (0, 0, 0))
    vec = pl.BlockSpec((1, DL), lambda i: (0, 0))
    return _pcall(body, name="lru_gates_fwd", grid=(T // tr,),
                  in_specs=[_rt(tr, DL), w3, vec, w3, vec, vec],
                  out_specs=[_rt(tr, DL), _rt(tr, DL)], out_shape=[S((T, DL), F32), S((T, DL), F32)],
                  compiler_params=_cparams(("parallel",)))(xl, w_a, b_a, w_x, b_x, lam)


def _lru_gates_bwd(xl, w_a, b_a, w_x, b_x, lam, da, du, tr):
    T, DL = xl.shape
    H, B, _ = w_a.shape

    def body(xl_ref, wa_ref, ba_ref, wx_ref, bx_ref, lam_ref, da_ref, du_ref,
             dxl_ref, dwa_ref, dba_ref, dwx_ref, dbx_ref, dlam_ref):
        @pl.when(pl.program_id(0) == 0)
        def _():
            for r in (dwa_ref, dba_ref, dwx_ref, dbx_ref, dlam_ref):
                r[...] = jnp.zeros_like(r)

        x = xl_ref[...]
        r_pre = _blockdiag(x, wa_ref) + ba_ref[...]
        i_pre = _blockdiag(x, wx_ref) + bx_ref[...]
        _, vjp = jax.vjp(_lru_elem, x, r_pre, i_pre, lam_ref[...])
        dx, dr, di, dlam = vjp((da_ref[...], du_ref[...]))
        dxl_ref[...] = dx + _blockdiag(dr, wa_ref, _NT) + _blockdiag(di, wx_ref, _NT)
        for h in range(H):
            xh = x[:, h * B:(h + 1) * B]
            dwa_ref[h] += _dotb(xh, dr[:, h * B:(h + 1) * B], _TN)
            dwx_ref[h] += _dotb(xh, di[:, h * B:(h + 1) * B], _TN)
        dba_ref[...] += _colsum(dr)
        dbx_ref[...] += _colsum(di)
        dlam_ref[...] += dlam

    w3 = pl.BlockSpec(w_a.shape, lambda i: (0, 0, 0))
    vec = pl.BlockSpec((1, DL), lambda i: (0, 0))
    return _pcall(body, name="lru_gates_bwd", grid=(T // tr,),
                  in_specs=[_rt(tr, DL), w3, vec, w3, vec, vec, _rt(tr, DL), _rt(tr, DL)],
                  out_specs=[_rt(tr, DL), w3, vec, w3, vec, vec],
                  out_shape=[S((T, DL), F32), S(w_a.shape, F32), S((1, DL), F32), S(w_a.shape, F32),
                             S((1, DL), F32), S((1, DL), F32)],
                  compiler_params=_cparams(("arbitrary",)))(xl, w_a, b_a, w_x, b_x, lam, da, du)


def _groups(v):
    return v.reshape(v.shape[0] // SUBLANES, SUBLANES, v.shape[1])


def _rows_shifted(v, edge, up):
    sub = lax.broadcasted_iota(jnp.int32, v.shape, 1)
    if up:
        other = jnp.concatenate([v[1:], edge[None]], axis=0)
        return jnp.where(sub < SUBLANES - 1, pltpu.roll(v, SUBLANES - 1, 1), pltpu.roll(other, SUBLANES - 1, 1))
    other = jnp.concatenate([edge[None], v[:-1]], axis=0)
    return jnp.where(sub >= 1, pltpu.roll(v, 1, 1), pltpu.roll(other, 1, 1))


def _scan_tile(a, u, entering, emit, up):
    G = a.shape[0]
    sub = lax.broadcasted_iota(jnp.int32, a.shape, 1)
    d = 1
    while d < SUBLANES:
        if up:
            keep = sub < SUBLANES - d
            a_s = jnp.where(keep, pltpu.roll(a, SUBLANES - d, 1), 1.0)
            u_s = jnp.where(keep, pltpu.roll(u, SUBLANES - d, 1), 0.0)
        else:
            keep = sub >= d
            a_s = jnp.where(keep, pltpu.roll(a, d, 1), 1.0)
            u_s = jnp.where(keep, pltpu.roll(u, d, 1), 0.0)
        u = a * u_s + u
        a = a * a_s
        d *= 2
    for g in (reversed(range(G)) if up else range(G)):
        hg = u[g] + a[g] * entering
        emit(g, hg)
        entering = hg[0:1] if up else hg[SUBLANES - 1:SUBLANES]
    return entering


def _lru_scan_fwd(a, u, tr, deps=()):
    T, DL = a.shape

    def body(a_ref, u_ref, *rest):
        h_ref, carry = rest[len(deps):]

        @pl.when(pl.program_id(0) == 0)
        def _():
            carry[...] = jnp.zeros_like(carry)

        def emit(g, hg):
            h_ref[g * SUBLANES:(g + 1) * SUBLANES, :] = hg

        last = _scan_tile(_groups(a_ref[...]), _groups(u_ref[...]), carry[0:1, :], emit, up=False)
        carry[...] = jnp.broadcast_to(last, carry.shape)

    return _pcall(body, name="lru_scan_fwd", grid=(T // tr,),
                  in_specs=[_rt(tr, DL), _rt(tr, DL)] + [_ANY] * len(deps),
                  out_specs=_rt(tr, DL), out_shape=S((T, DL), F32),
                  scratch_shapes=[pltpu.VMEM((SUBLANES, DL), F32)],
                  compiler_params=_cparams(("arbitrary",)))(a, u, *deps)


def _lru_scan_bwd(a, h, dh, tr):
    T, DL = a.shape
    n = T // tr

    def body(a_ref, an_ref, h_ref, hp_ref, dh_ref, du_ref, da_ref, carry):
        i = pl.program_id(0)
        ti = n - 1 - i

        @pl.when(i == 0)
        def _():
            carry[...] = jnp.zeros_like(carry)

        a_next = _rows_shifted(_groups(a_ref[...]), jnp.where(ti == n - 1, 0.0, an_ref[...]), up=True)
        h_prev = _rows_shifted(_groups(h_ref[...]), jnp.where(ti == 0, 0.0, hp_ref[...]), up=False)

        def emit(g, gg):
            du_ref[g * SUBLANES:(g + 1) * SUBLANES, :] = gg
            da_ref[g * SUBLANES:(g + 1) * SUBLANES, :] = gg * h_prev[g]

        top = _scan_tile(a_next, _groups(dh_ref[...]), carry[0:1, :], emit, up=True)
        carry[...] = jnp.broadcast_to(top, carry.shape)

    return _pcall(body, name="lru_scan_bwd", grid=(n,),
                  in_specs=[_rt(tr, DL, 0, n), _halo_next(tr, DL, n, 0, n), _rt(tr, DL, 0, n),
                            _halo_prev(tr, DL, 0, n), _rt(tr, DL, 0, n)],
                  out_specs=[_rt(tr, DL, 0, n), _rt(tr, DL, 0, n)],
                  out_shape=[S((T, DL), F32), S((T, DL), F32)],
                  scratch_shapes=[pltpu.VMEM((SUBLANES, DL), F32)],
                  compiler_params=_cparams(("arbitrary",)))(a, a, h, h, dh)


def _adamw(w, g, m, v):
    m = ADAM_B1 * m + (1.0 - ADAM_B1) * g
    v = ADAM_B2 * v + (1.0 - ADAM_B2) * (g * g)
    m_hat = m / (1.0 - ADAM_B1 ** ADAM_STEP)
    v_hat = v / (1.0 - ADAM_B2 ** ADAM_STEP)
    delta = -ADAM_LR * (m_hat / (jnp.sqrt(v_hat) + ADAM_EPS) + ADAM_WD * w)
    return delta, m, v


def _adamw_big(name, w, m, v, own, recv, own_idx):
    _, R, C = w.shape
    n_recv = recv.shape[0]
    tr = _pick(R, (256, 128, 64, 32, 16))

    def body(idx_ref, w_ref, m_ref, v_ref, p_ref, *rest):
        g = p_ref[...].astype(F32)
        for r in rest[:n_recv]:
            g = g + r[...].astype(F32)
        g_ref, d_ref, nm_ref, nv_ref = rest[n_recv:]
        d, nm, nv = _adamw(w_ref[...], g, m_ref[...], v_ref[...])
        g_ref[...] = g
        d_ref[...] = d
        nm_ref[...] = nm
        nv_ref[...] = nv

    r_spec = lambda s: pl.BlockSpec((None, tr, C), lambda i, idx: (s, i, 0))
    t2 = r_spec(0)
    gs = pltpu.PrefetchScalarGridSpec(
        num_scalar_prefetch=1, grid=(R // tr,),
        in_specs=[t2, t2, t2, pl.BlockSpec((None, tr, C), lambda i, idx: (idx[0], i, 0))]
        + [r_spec(s) for s in range(n_recv)],
        out_specs=[t2, t2, t2, t2])
    return _pcall(body, name=name, grid_spec=gs, out_shape=[S((1, R, C), F32)] * 4,
                  compiler_params=_cparams(("parallel",)))(own_idx, w, m, v, own, *([recv] * n_recv))


def _adamw_small(ws, gs, ms, vs):
    n = len(ws)

    def body(*refs):
        for k in range(n):
            d, nm, nv = _adamw(refs[k][...], refs[n + k][...], refs[2 * n + k][...], refs[3 * n + k][...])
            refs[4 * n + k][...] = d
            refs[5 * n + k][...] = nm
            refs[6 * n + k][...] = nv

    res = _pcall(body, name="adamw_small", out_shape=[S(w.shape, F32) for w in ws] * 3,
                 compiler_params=_cparams())(*ws, *gs, *ms, *vs)
    return res[:n], res[n:2 * n], res[2 * n:]


def _sum8(name, parts):
    _, R, C = parts.shape

    def body(p_ref, o_ref):
        acc = p_ref[0]
        for k in range(1, N_DEV):
            acc = acc + p_ref[k]
        o_ref[...] = acc

    return _pcall(body, name=name, out_shape=S((R, C), F32), compiler_params=_cparams())(parts)


def _pair_sum(name, full, recv, c_idx):
    _, R, C = full.shape
    tr = _pick(R, (256, 128, 64, 32, 16))

    def body(c_ref, f_ref, r_ref, o_ref):
        o_ref[...] = (f_ref[...].astype(F32) + r_ref[...].astype(F32)).astype(o_ref.dtype)

    gs = pltpu.PrefetchScalarGridSpec(
        num_scalar_prefetch=1, grid=(4, R // tr),
        in_specs=[pl.BlockSpec((None, tr, C), lambda j, i, c: (2 * j + c[0], i, 0)),
                  pl.BlockSpec((None, tr, C), lambda j, i, c: (j, i, 0))],
        out_specs=pl.BlockSpec((None, tr, C), lambda j, i, c: (j, i, 0)))
    return _pcall(body, name=name, grid_spec=gs, out_shape=S((4, R, C), BF16),
                  compiler_params=_cparams(("parallel", "parallel")))(c_idx, full, recv)


def _cast_bf16(name, w, dev_idx, row0=0, rows=None, deps=()):
    C = w.shape[2]
    R = w.shape[1] if rows is None else rows
    tr = _pick(R, (256, 128, 64, 32, 16))
    b0 = row0 // tr

    def body(d_ref, w_ref, *rest):
        rest[-1][...] = w_ref[...].astype(BF16)

    gs = pltpu.PrefetchScalarGridSpec(
        num_scalar_prefetch=1, grid=(R // tr,),
        in_specs=[pl.BlockSpec((None, tr, C), lambda i, d: (0, i + b0, 0))] + [_ANY] * len(deps),
        out_specs=pl.BlockSpec((None, tr, C), lambda i, d: (d[0], i, 0)))
    return _pcall(body, name=name, grid_spec=gs, out_shape=S((N_DEV, R, C), BF16),
                  compiler_params=_cparams(("parallel",)))(dev_idx, w, *deps)


def _own_block(v, dev):
    return lax.dynamic_update_slice(lax.empty((N_DEV,) + v.shape, v.dtype), v[None], (dev,) + (0,) * v.ndim)


_ANY = pl.BlockSpec(memory_space=pl.ANY)


def _position():
    return lax.axis_index("x"), lax.axis_index("y"), lax.axis_index("c")


def _allgather(name, bufs, deps=()):
    n = len(bufs)
    nd = len(deps)

    def body(*refs):
        outs = refs[n + nd:2 * n + nd]
        send, recv = refs[2 * n + nd:]
        x, y, c = _position()
        me, sib = (x, y, c), (x, y, 1 - c)
        chips = [(1 - x, y), (x, 1 - y), (1 - x, 1 - y)]

        def copy(a, k, block, to):
            bx, by, bc = block
            blk = outs[a].at[4 * bx + 2 * by + bc]
            return pltpu.make_async_remote_copy(
                src_ref=blk, dst_ref=blk, send_sem=send.at[a, k], recv_sem=recv.at[a, k],
                device_id=to, device_id_type=MESH)

        first = []
        for a in range(n):
            first.append(copy(a, 0, me, sib))
            first += [copy(a, 1 + j, me, (*chip, c)) for j, chip in enumerate(chips)]
        for cp in first:
            cp.start()
        passed = []
        for j, chip in enumerate(chips):
            for a in range(n):
                copy(a, 1 + j, (*chip, c), me).wait_recv()
                cp = copy(a, 4 + j, (*chip, c), sib)
                cp.start()
                passed.append(cp)
        for a in range(n):
            copy(a, 0, sib, me).wait_recv()
        for j, chip in enumerate(chips):
            for a in range(n):
                copy(a, 4 + j, (*chip, 1 - c), me).wait_recv()
        for cp in first + passed:
            cp.wait_send()

    return _pcall(body, name=name, in_specs=[_ANY] * (n + nd), out_specs=[_ANY] * n,
                  out_shape=[S(b.shape, b.dtype) for b in bufs], input_output_aliases={a: a for a in range(n)},
                  scratch_shapes=[pltpu.SemaphoreType.DMA((n, 7)), pltpu.SemaphoreType.DMA((n, 7))])(*bufs, *deps)


def _rs_sibling(name, fulls):
    n = len(fulls)

    def body(*refs):
        ins, outs = refs[:n], refs[n:2 * n]
        send, recv = refs[2 * n:]
        x, y, c = _position()
        copies = []
        for a in range(n):
            for j in range(4):
                copies.append(pltpu.make_async_remote_copy(
                    src_ref=ins[a].at[2 * j + (1 - c)], dst_ref=outs[a].at[j], send_sem=send.at[a, j],
                    recv_sem=recv.at[a, j], device_id=(x, y, 1 - c), device_id_type=MESH))
        for cp in copies:
            cp.start()
        for cp in copies:
            cp.wait()

    return _pcall(body, name=name, in_specs=[_ANY] * n, out_specs=[_ANY] * n,
                  out_shape=[S((4,) + f.shape[1:], f.dtype) for f in fulls],
                  scratch_shapes=[pltpu.SemaphoreType.DMA((n, 4)), pltpu.SemaphoreType.DMA((n, 4))])(*fulls)


_HBM = pl.BlockSpec(memory_space=pltpu.HBM)
_SEM = pl.BlockSpec(memory_space=pltpu.SEMAPHORE)
_EFFECT = pltpu.SideEffectType.DATAFLOW_SIDE_EFFECTING


def _remote_copies(copies_fn, srcs, lands, send, recv):
    x, y, c = _position()
    return [pltpu.make_async_remote_copy(src_ref=s, dst_ref=d, send_sem=send[i], recv_sem=recv[i], device_id=to,
                                         device_id_type=MESH)
            for i, (s, d, to) in enumerate(copies_fn(x, y, c, srcs, lands))]


def _split_start(name, srcs, lands, copies_fn, nc, after=()):
    n, nl, na = len(srcs), len(lands), len(after)

    def body(*refs):
        src_refs, land_refs = refs[:n], refs[n:n + nl]
        outs = refs[n + nl + na:]
        for cp in _remote_copies(copies_fn, src_refs, land_refs, outs[:nc], outs[nc:2 * nc]):
            cp.start()
        outs[-1][...] = jnp.zeros_like(outs[-1])

    hbm = lambda a: pltpu.with_memory_space_constraint(a, pltpu.HBM)
    res = _pcall(
        body, name=name, in_specs=[_HBM] * (n + nl) + [_ANY] * na,
        out_specs=[_SEM] * (2 * nc) + [_HBM] * (n + nl) + [pl.BlockSpec(memory_space=pltpu.VMEM)],
        out_shape=[pltpu.SemaphoreType.DMA(())] * (2 * nc) + [pltpu.HBM(s.shape, s.dtype) for s in srcs]
        + [pltpu.HBM(l.shape, l.dtype) for l in lands] + [S((SUBLANES, LANES), F32)],
        input_output_aliases={i: 2 * nc + i for i in range(n + nl)},
        compiler_params=pltpu.CompilerParams(has_side_effects=_EFFECT),
    )(*[hbm(s) for s in srcs], *[hbm(l) for l in lands], *after)
    return res[:2 * nc], res[2 * nc:2 * nc + n], res[2 * nc + n:2 * nc + n + nl], res[-1]


def _split_wait(name, sems, srcs, lands, after, copies_fn, nc):
    n, nl = len(srcs), len(lands)

    def body(*refs):
        src_refs, land_refs = refs[:n], refs[n:n + nl]
        sem_refs = refs[n + nl:n + nl + 2 * nc]
        for cp in _remote_copies(copies_fn, src_refs, land_refs, sem_refs[:nc], sem_refs[nc:]):
            cp.wait_send()
            cp.wait_recv()

    res = _pcall(
        body, name=name, in_specs=[_HBM] * (n + nl) + [_SEM] * (2 * nc) + [_ANY],
        out_specs=[_HBM] * (n + nl), out_shape=[pltpu.HBM(a.shape, a.dtype) for a in list(srcs) + list(lands)],
        input_output_aliases={i: i for i in range(n + nl)},
        compiler_params=pltpu.CompilerParams(has_side_effects=_EFFECT),
    )(*srcs, *lands, *sems, after)
    return res[:n], res[n:]


def _other_chips(x, y):
    return [(1 - x, y), (x, 1 - y), (1 - x, 1 - y)]


def _ag_copies(x, y, c, srcs, lands):
    out = []
    for land in lands:
        blk = land.at[4 * x + 2 * y + c]
        out.append((blk, blk, (x, y, 1 - c)))
        out += [(blk, blk, (px, py, c)) for px, py in _other_chips(x, y)]
    return out


def _rs_copies(x, y, c, srcs, lands):
    return [(s.at[2 * px + py], land.at[j], (px, py, c))
            for s, land in zip(srcs, lands) for j, (px, py) in enumerate(_other_chips(x, y))]


def _fwd_copies(x, y, c, srcs, lands):
    out = []
    for land in lands:
        for px, py in _other_chips(x, y):
            blk = land.at[4 * px + 2 * py + c]
            out.append((blk, blk, (x, y, 1 - c)))
    return out


def _direct_copies(x, y, c, srcs, lands):
    out = []
    for s, land in zip(srcs, lands):
        for r in range(1, N_DEV):
            px = 1 - x if r & 4 else x
            py = 1 - y if r & 2 else y
            pc = 1 - c if r & 1 else c
            out.append((s.at[4 * px + 2 * py + pc], land.at[r - 1], (px, py, pc)))
    return out


def _ag_finish(name, lands):
    n = len(lands)

    def body(*refs):
        outs = refs[n:2 * n]
        send, recv = refs[2 * n:]
        x, y, c = _position()

        def swap(a, j, px, py, pc):
            blk = outs[a].at[4 * px + 2 * py + pc]
            return pltpu.make_async_remote_copy(src_ref=blk, dst_ref=blk, send_sem=send.at[a, j], recv_sem=recv.at[a, j],
                                                device_id=(x, y, 1 - c), device_id_type=MESH)

        chips = _other_chips(x, y)
        sends = [swap(a, j, px, py, c) for a in range(n) for j, (px, py) in enumerate(chips)]
        for cp in sends:
            cp.start()
        for a in range(n):
            for j, (px, py) in enumerate(chips):
                swap(a, j, px, py, 1 - c).wait_recv()
        for cp in sends:
            cp.wait_send()

    return _pcall(body, name=name, in_specs=[_ANY] * n, out_specs=[_ANY] * n,
                  out_shape=[S(l.shape, l.dtype) for l in lands], input_output_aliases={a: a for a in range(n)},
                  scratch_shapes=[pltpu.SemaphoreType.DMA((n, 3)), pltpu.SemaphoreType.DMA((n, 3))])(*lands)


def _pad_lanes(v):
    return jnp.pad(v, ((0, 0), (0, LANES - v.shape[1])))


def _flat_rows(pieces):
    flat = jnp.concatenate([p.reshape(-1) for p in pieces])
    rows = -(-flat.shape[0] // (SMALL_W * SUBLANES)) * SUBLANES
    return jnp.pad(flat, (0, rows * SMALL_W - flat.shape[0])).reshape(rows, SMALL_W)


def _unflat(buf, shapes):
    flat = buf.reshape(-1)
    out, off = [], 0
    for sh in shapes:
        n = 1
        for d in sh:
            n *= d
        out.append(flat[off:off + n].reshape(sh))
        off += n
    return out


def kernel(x, pre_mix_norm, w_in, ssd_conv_w, ssd_conv_b, ssd_dt_bias, ssd_a_log, ssd_d, ssd_norm, lru_conv_w, lru_conv_b, lru_w_a, lru_b_a, lru_w_x, lru_b_x, lru_lambda, lru_norm, w_out, post_mix_norm, pre_mlp_norm, w_mlp_in, w_mlp_out, post_mlp_norm, loss_target, m_pre_mix_norm, m_w_in, m_ssd_conv_w, m_ssd_conv_b, m_ssd_dt_bias, m_ssd_a_log, m_ssd_d, m_ssd_norm, m_lru_conv_w, m_lru_conv_b, m_lru_w_a, m_lru_b_a, m_lru_w_x, m_lru_b_x, m_lru_lambda, m_lru_norm, m_w_out, m_post_mix_norm, m_pre_mlp_norm, m_w_mlp_in, m_w_mlp_out, m_post_mlp_norm, v_pre_mix_norm, v_w_in, v_ssd_conv_w, v_ssd_conv_b, v_ssd_dt_bias, v_ssd_a_log, v_ssd_d, v_ssd_norm, v_lru_conv_w, v_lru_conv_b, v_lru_w_a, v_lru_b_a, v_lru_w_x, v_lru_b_x, v_lru_lambda, v_lru_norm, v_w_out, v_post_mix_norm, v_pre_mlp_norm, v_w_mlp_in, v_w_mlp_out, v_post_mlp_norm):
    names = ['pre_mix_norm', 'w_in', 'ssd_conv_w', 'ssd_conv_b', 'ssd_dt_bias', 'ssd_a_log', 'ssd_d', 'ssd_norm',
             'lru_conv_w', 'lru_conv_b', 'lru_w_a', 'lru_b_a', 'lru_w_x', 'lru_b_x', 'lru_lambda', 'lru_norm',
             'w_out', 'post_mix_norm', 'pre_mlp_norm', 'w_mlp_in', 'w_mlp_out', 'post_mlp_norm']
    loc = locals()
    W = {n: loc[n] for n in names}
    Mo = {n: loc["m_" + n] for n in names}
    Vo = {n: loc["v_" + n] for n in names}
    big = ['w_in', 'w_out', 'w_mlp_in', 'w_mlp_out']

    px, py, pc = _position()
    dev = 4 * px + 2 * py + pc
    dev_idx = jnp.reshape(dev, (1,)).astype(jnp.int32)
    c_idx = jnp.reshape(pc, (1,)).astype(jnp.int32)
    chip_idx = jnp.reshape(2 * px + py, (1,)).astype(jnp.int32)

    _, T, D = x.shape
    x2 = x.reshape(T, D)
    tgt = loss_target.reshape(T, D)
    n_heads = ssd_dt_bias.shape[1]
    XBC = ssd_conv_b.shape[1]
    GN = XBC // 4
    DS = XBC - 2 * GN
    DL = lru_norm.shape[1]
    DFF = w_mlp_in.shape[2] * N_DEV
    DIN = w_in.shape[2] * N_DEV
    NP = XBC + DS + 2 * DL + LANES
    assert DS % GN == 0 and XBC % DS == 0 and DS == DL and n_heads <= LANES
    cb_z, cb_gate, cb_xl, cb_dt = XBC // DS, XBC // DS + 1, XBC // DS + 2, (XBC + DS + 2 * DL) // LANES
    tr = min(256, T // 2)
    nt = T // tr
    Q = min(256, T // 2)

    Dh = D // 2
    sh_a = _cast_bf16("cast_w_in_a", W['w_in'], dev_idx, 0, Dh)
    sh_b = _cast_bf16("cast_w_in_b", W['w_in'], dev_idx, Dh, Dh)
    later = big[1:]
    a_bufs = [sh_a, _own_block(ssd_conv_w[0], dev), _own_block(lru_conv_w[0], dev)]
    a_sems, _, a_lands, a_token = _split_start("allgather_w_in_a_start", [], a_bufs, _ag_copies, 4 * len(a_bufs))
    sh = {n: _cast_bf16("cast_" + n, W[n], dev_idx, deps=[a_token]) for n in later}

    def f_norm_in(first, last, xv, g):
        return (_rms(xv, g),), ()
    (h,) = _rows_call("norm_in", f_norm_in, nt, [x2, pre_mix_norm], [_rt(tr, D), _full(pre_mix_norm)],
                      [((T, D), BF16, _rt(tr, D))], [], 'tf', deps=[a_token])
    _, a_lands = _split_wait("allgather_w_in_a_wait", a_sems, [], a_lands, h, _ag_copies, 4 * len(a_bufs))
    g_in_a, g_cs, g_cl = _ag_finish("allgather_w_in_a_finish", a_lands)
    b_sems, _, b_lands, b_token = _split_start("allgather_w_in_b_start", [], [sh_b], _ag_copies, 4, after=[g_in_a])
    ag_sems, ag_srcs, ag_lands, ag_token = _split_start(
        "allgather_later_start", [], [sh[n] for n in later], _ag_copies, 4 * len(later), after=[b_token])
    conv_s = jnp.transpose(g_cs, (1, 0, 2)).reshape(CONV_WIDTH, XBC)
    conv_l = jnp.transpose(g_cl, (1, 0, 2)).reshape(CONV_WIDTH, DL)
    wb = DIN // N_DEV
    o_z, o_xbc, o_dt, o_gate, o_xl = 0, DS, DS + XBC, DS + XBC + n_heads, DS + XBC + n_heads + DL
    segs = [(o_xbc, o_xbc + XBC, 0), (o_z, o_z + DS, XBC), (o_gate, o_gate + DL, XBC + DS),
            (o_xl, o_xl + DL, XBC + DS + DL), (o_dt, o_dt + n_heads, NP - LANES)]

    def ref_cols(g, lo, hi):
        out = []
        while lo < hi:
            k = lo // wb
            e = min(hi, (k + 1) * wb)
            out.append(g[k, :, lo - k * wb:e - k * wb])
            lo = e
        return out

    def laid_out(g):
        return jnp.concatenate([p for a, b, _ in segs for p in ref_cols(g, a, b)]
                               + [jnp.zeros((g.shape[1], LANES - n_heads), BF16)], axis=1)

    def my_cols(g, lo, hi):
        out = []
        for a, b, m in sorted(segs):
            s, e = max(lo, a), min(hi, b)
            if s < e:
                out.append(g[:, m + s - a:m + e - a])
        return out

    wp_a = laid_out(g_in_a)
    dt_bias = _pad_lanes(ssd_dt_bias)
    a_log = _pad_lanes(ssd_a_log)
    d_skip = _pad_lanes(ssd_d)
    wa_b, wx_b = lru_w_a[0].astype(BF16), lru_w_x[0].astype(BF16)
    b_a, b_x = lru_b_a.reshape(1, DL), lru_b_x.reshape(1, DL)

    (proj_a,) = _mm("proj_a", h, wp_a, a_cols=(0, Dh), outs=((BF16, None),), deps=[ag_token])
    (dt_a,) = _mm("proj_dt_a", h, wp_a[:, NP - LANES:], a_cols=(0, Dh))
    _, b_lands = _split_wait("allgather_w_in_b_wait", b_sems, [], b_lands, proj_a, _ag_copies, 4)
    (g_in_b,) = _ag_finish("allgather_w_in_b_finish", b_lands)
    wp_b = laid_out(g_in_b)
    add = lambda r, e: r + e.astype(F32)
    (proj,) = _mm("proj_b", h, wp_b, a_cols=(Dh, Dh), extra=proj_a, outs=((BF16, add),))
    (dt_raw,) = _mm("proj_dt_b", h, wp_b[:, NP - LANES:], a_cols=(Dh, Dh), extra=dt_a, outs=((F32, add),))

    cwx = min(1024, XBC)

    def f_ssd_pre(first, last, xbc, halo, w, b):
        pre = _conv_pre(xbc, jnp.where(first, 0.0, halo), w, b)
        return (pre * jax.nn.sigmoid(pre),), ()
    (xbc_act,) = _rows_call(
        "ssd_pre", f_ssd_pre, nt, [proj, proj, conv_s, ssd_conv_b],
        [_rt(tr, XBC), _halo_prev(tr, XBC, rows=PACKED_ROWS), _full(conv_s), _full(ssd_conv_b)],
        [((T, XBC), F32, _rt(tr, XBC))], [], ['t', 'p0', 'f', 'f'], cw=cwx)

    def f_ssd_dt(first, last, dtr, dtb):
        return (jax.nn.softplus(dtr + dtb),), ()
    (dt,) = _rows_call("ssd_dt", f_ssd_dt, nt, [dt_raw, dt_bias], [_rt(tr, LANES), _full(dt_bias)],
                       [((T, LANES), F32, _rt(tr, LANES))], [], 'tf')

    y_ssd, h_prev = _ssd_fwd(xbc_act, dt, a_log, d_skip, n_heads, Q)

    gw = DS // SSD_GROUPS

    def ssd_post(y, z, g):
        yz = y * jax.nn.silu(z)
        parts = []
        for k in range(SSD_GROUPS):
            yk = yz[:, k * gw:(k + 1) * gw]
            parts.append(yk * lax.rsqrt(jnp.mean(yk * yk, axis=-1, keepdims=True) + EPS))
        return jnp.concatenate(parts, axis=-1) * g

    def f_ssd_post(first, last, y, z, g):
        return (ssd_post(y, z, g),), ()
    (mixcat,) = _rows_call("ssd_post", f_ssd_post, nt, [y_ssd, proj, ssd_norm],
                           [_rt(tr, DS), _rt(tr, DS, cb_z), _full(ssd_norm)], [((T, DS + DL), BF16, _rt(tr, DS))], [],
                           'ttf')

    def f_lru_pre(first, last, xv, halo, w, b):
        return (_conv_pre(xv, jnp.where(first, 0.0, halo), w, b),), ()
    (xl,) = _rows_call("lru_pre", f_lru_pre, nt, [proj, proj, conv_l, lru_conv_b],
                       [_rt(tr, DL, cb_xl), _halo_prev(tr, DL, cb_xl, rows=PACKED_ROWS), _full(conv_l),
                        _full(lru_conv_b)],
                       [((T, DL), F32, _rt(tr, DL))], [], ['t', 'p0', 'f', 'f'])

    a_lru, u_lru = _lru_gates_fwd(xl, wa_b, b_a, wx_b, b_x, lru_lambda, tr)
    _, ag_lands = _split_wait("allgather_later_wait", ag_sems, ag_srcs, ag_lands, u_lru, _ag_copies, 4 * len(later))
    f_sems, _, ag_lands, f_token = _split_start("allgather_later_fwd_start", [], ag_lands, _fwd_copies, 3 * len(later))
    h_lru = _lru_scan_fwd(a_lru, u_lru, tr, deps=[f_token])

    def lru_post(hv, gate, g):
        return _rms(hv * jax.nn.gelu(gate), g)

    def f_lru_post(first, last, hv, gate, g):
        return (lru_post(hv, gate, g),), ()
    cb_l = DS // DL
    (mixcat,) = _rows_call("lru_post", f_lru_post, nt, [h_lru, proj, lru_norm],
                           [_rt(tr, DL), _rt(tr, DL, cb_gate), _full(lru_norm)],
                           [((T, DS + DL), BF16, _rt(tr, DL, cb_l))], [], 'ttf', into=mixcat)

    _, (g_out, g_mi, g_mo) = _split_wait("allgather_later_fwd_wait", f_sems, [], ag_lands, mixcat, _fwd_copies,
                                         3 * len(later))
    w_out_f = g_out.reshape(DS + DL, D)
    w_mi_f = jnp.transpose(g_mi, (1, 0, 2)).reshape(D, DFF)
    w_mo_f = g_mo.reshape(DFF, D)
    (mix,) = _mm("mix", mixcat, w_out_f, outs=((BF16, None),))

    def f_post_mix(first, last, xv, mx, gpm, gpl):
        x1 = xv + _rms(mx, gpm)
        return (x1, _rms(x1, gpl)), ()
    x1, hn = _rows_call("post_mix", f_post_mix, nt, [x2, mix, post_mix_norm, pre_mlp_norm],
                        [_rt(tr, D), _rt(tr, D), _full(post_mix_norm), _full(pre_mlp_norm)],
                        [((T, D), F32, _rt(tr, D)), ((T, D), BF16, _rt(tr, D))], [], 'ttff')

    hm, act = _mm("mlp_in", hn, w_mi_f,
                  outs=((BF16, None), (BF16, lambda r, e: jnp.square(jnp.maximum(r, 0.0)))))
    (hm2,) = _mm("mlp_out", act, w_mo_f, outs=((BF16, None),))

    def f_final(first, last, x1v, hm2v, g, tg):
        def fwd(hv, gv):
            return x1v + _rms(hv, gv)
        x2v, vjp = jax.vjp(fwd, hm2v, g)
        err = x2v - tg
        dx2 = err * (1.0 / D)
        dh, dg = vjp(dx2)
        loss = jnp.full((1, LANES), 0.5 / D, F32) * jnp.sum(err * err)
        return (dx2, dh), (dg, loss)
    dx1a, dhm2, g_post_mlp, loss_part = _rows_call(
        "loss_head", f_final, nt, [x1, hm2, post_mlp_norm, tgt],
        [_rt(tr, D), _rt(tr, D), _full(post_mlp_norm), _rt(tr, D)],
        [((T, D), F32, _rt(tr, D)), ((T, D), BF16, _rt(tr, D))], [(1, D), (1, LANES)], 'ttft')

    def rs_begin(n, full):
        (from_sib,) = _rs_sibling("rs_sibling_" + n, [full])
        pair = _pair_sum("pair_sum_" + n, full, from_sib, c_idx)
        sems, srcs, lands, token = _split_start("rs_start_" + n, [pair], [lax.empty((3,) + pair.shape[1:], BF16)],
                                                _rs_copies, 3)
        return (sems, srcs, lands), token

    def rs_end(n, state, after):
        (pair,), (recv,) = _split_wait("rs_wait_" + n, *state, after, _rs_copies, 3)
        return pair, recv, chip_idx

    def rs_direct_begin(n, full):
        sems, srcs, lands, token = _split_start("rs_start_" + n, [full],
                                                [lax.empty((N_DEV - 1,) + full.shape[1:], BF16)], _direct_copies,
                                                N_DEV - 1)
        return (sems, srcs, lands), token

    def rs_direct_end(n, state, after):
        (full,), (recv,) = _split_wait("rs_wait_" + n, *state, after, _direct_copies, N_DEV - 1)
        return full, recv, dev_idx

    (gw_mo,) = _mm("dw_mlp_out", act, dhm2, ta=True, outs=((BF16, None),))
    rs_mo, tok = rs_direct_begin('w_mlp_out', gw_mo.reshape(N_DEV, DFF // N_DEV, D))
    (dhm,) = _mm("d_mlp_act", dhm2, w_mo_f, tb=True, extra=hm,
                 outs=((BF16, lambda r, e: r * (2.0 * jnp.maximum(e.astype(F32), 0.0))),), deps=[tok])
    (gw_mi,) = _mm("dw_mlp_in", hn, dhm, ta=True, outs=((BF16, None),), out_blocks=N_DEV)
    rs_mi, tok = rs_direct_begin('w_mlp_in', gw_mi)
    (dhn,) = _mm("d_mlp_in", dhm, w_mi_f, tb=True, outs=((BF16, None),), deps=[tok])

    def f_post_mix_bwd(first, last, x1v, mx, gpm, gpl, dhnv, dxa):
        _, vjp1 = jax.vjp(_rms, x1v, gpl)
        dx1, dgpl = vjp1(dhnv)
        dx1 = dx1 + dxa
        _, vjp2 = jax.vjp(_rms, mx, gpm)
        dmx, dgpm = vjp2(dx1)
        return (dx1, dmx), (dgpl, dgpm)
    dx1, dmix, g_pre_mlp, g_post_mix = _rows_call(
        "post_mix_bwd", f_post_mix_bwd, nt, [x1, mix, post_mix_norm, pre_mlp_norm, dhn, dx1a],
        [_rt(tr, D), _rt(tr, D), _full(post_mix_norm), _full(pre_mlp_norm), _rt(tr, D), _rt(tr, D)],
        [((T, D), F32, _rt(tr, D)), ((T, D), BF16, _rt(tr, D))], [(1, D), (1, D)], 'ttfftt')

    (gw_out,) = _mm("dw_out", mixcat, dmix, ta=True, outs=((BF16, None),))
    rs_out, tok = rs_direct_begin('w_out', gw_out.reshape(N_DEV, -1, D))
    (dmixcat,) = _mm("d_mix", dmix, w_out_f, tb=True, outs=((BF16, None),), deps=[tok])

    def f_lru_post_bwd(first, last, hv, gate, g, dy):
        _, vjp = jax.vjp(lru_post, hv, gate, g)
        dh_, dgate, dg = vjp(dy)
        return (dgate, dh_), (dg,)
    dproj, dh_lru, g_lru_norm = _rows_call(
        "lru_post_bwd", f_lru_post_bwd, nt, [h_lru, proj, lru_norm, dmixcat],
        [_rt(tr, DL), _rt(tr, DL, cb_gate), _full(lru_norm), _rt(tr, DL, cb_l)],
        [((T, NP), BF16, _rt(tr, DL, cb_gate)), ((T, DL), F32, _rt(tr, DL))], [(1, DL)], 'ttft')

    du_lru, da_lru = _lru_scan_bwd(a_lru, h_lru, dh_lru, tr)
    dxl, g_wa, g_ba, g_wx, g_bx, g_lam = _lru_gates_bwd(xl, wa_b, b_a, wx_b, b_x, lru_lambda, da_lru, du_lru, tr)

    conv_bwd_kinds = ['t', 'p0', 'n0', 't', 'n3', 'f', 'f']

    def f_lru_pre_bwd(first, last, xv, hp, xn, d, dn, w, b):
        dx, dw8, db = _conv_bwd_tile(first, last, xv, hp, xn, d, dn, w, b, silu=False)
        return (dx,), (dw8, db)
    dproj, g_convl8, g_convl_b = _rows_call(
        "lru_pre_bwd", f_lru_pre_bwd, nt, [proj, proj, proj, dxl, dxl, conv_l, lru_conv_b],
        [_rt(tr, DL, cb_xl), _halo_prev(tr, DL, cb_xl, rows=PACKED_ROWS),
         _halo_next(tr, DL, nt, cb_xl, rows=PACKED_ROWS), _rt(tr, DL),
         _halo_next(tr, DL, nt), _full(conv_l), _full(lru_conv_b)],
        [((T, NP), BF16, _rt(tr, DL, cb_xl))], [(SUBLANES, DL), (1, DL)], conv_bwd_kinds, into=dproj)

    def f_ssd_post_bwd(first, last, y, z, g, dy):
        _, vjp = jax.vjp(ssd_post, y, z, g)
        dy_, dz, dg = vjp(dy)
        return (dz, dy_), (dg,)
    dproj, dy_ssd, g_ssd_norm = _rows_call(
        "ssd_post_bwd", f_ssd_post_bwd, nt, [y_ssd, proj, ssd_norm, dmixcat],
        [_rt(tr, DS), _rt(tr, DS, cb_z), _full(ssd_norm), _rt(tr, DS, 0)],
        [((T, NP), BF16, _rt(tr, DS, cb_z)), ((T, DS), F32, _rt(tr, DS))], [(1, DS)], 'ttft', into=dproj)

    dxbc_act, ddt, g_alog, g_dskip = _ssd_bwd(xbc_act, dt, a_log, d_skip, h_prev, dy_ssd, n_heads, Q)

    def f_ssd_pre_bwd(first, last, xv, hp, xn, d, dn, w, b):
        dx, dw8, db = _conv_bwd_tile(first, last, xv, hp, xn, d, dn, w, b, silu=True)
        return (dx,), (dw8, db)
    dproj, g_convs8, g_convs_b = _rows_call(
        "ssd_pre_bwd", f_ssd_pre_bwd, nt, [proj, proj, proj, dxbc_act, dxbc_act, conv_s, ssd_conv_b],
        [_rt(tr, XBC), _halo_prev(tr, XBC, rows=PACKED_ROWS), _halo_next(tr, XBC, nt, rows=PACKED_ROWS),
         _rt(tr, XBC), _halo_next(tr, XBC, nt),
         _full(conv_s), _full(ssd_conv_b)],
        [((T, NP), BF16, _rt(tr, XBC))], [(SUBLANES, XBC), (1, XBC)], conv_bwd_kinds, into=dproj, cw=cwx)

    def f_ssd_dt_bwd(first, last, ddtv, dtr, dtb):
        ddtr = ddtv * jax.nn.sigmoid(dtr + dtb)
        return (ddtr,), (_colsum(ddtr),)
    dproj, g_dtb = _rows_call(
        "ssd_dt_bwd", f_ssd_dt_bwd, nt, [ddt, dt_raw, dt_bias],
        [_rt(tr, LANES), _rt(tr, LANES), _full(dt_bias)],
        [((T, NP), BF16, _rt(tr, LANES, cb_dt))], [(1, LANES)], 'ttf', into=dproj)
    small = {
        'ssd_conv_w': g_convs8[:CONV_WIDTH], 'ssd_conv_b': g_convs_b,
        'ssd_dt_bias': g_dtb[:, :n_heads], 'ssd_a_log': g_alog[:, :n_heads], 'ssd_d': g_dskip[:, :n_heads],
        'ssd_norm': g_ssd_norm, 'lru_conv_w': g_convl8[:CONV_WIDTH], 'lru_conv_b': g_convl_b,
        'lru_w_a': g_wa, 'lru_b_a': g_ba, 'lru_w_x': g_wx, 'lru_b_x': g_bx, 'lru_lambda': g_lam,
        'lru_norm': g_lru_norm, 'post_mix_norm': g_post_mix, 'pre_mlp_norm': g_pre_mlp,
        'post_mlp_norm': g_post_mlp, 'loss': loss_part[:, :1],
    }
    wide = ['lru_w_a', 'lru_w_x']
    narrow = [n for n in small if n not in wide]
    lb = lru_w_a.shape[-1]
    s_srcs = [_flat_rows([small[n] for n in narrow]), g_wa.reshape(-1, lb), g_wx.reshape(-1, lb)]
    s_sems, s_srcs, s_lands, tok = _split_start(
        "small_grads_start", [], [_own_block(a, dev) for a in s_srcs], _ag_copies, 4 * len(s_srcs))

    (gwp,) = _mm("dw_proj", h, dproj, ta=True, outs=((BF16, None),), deps=[tok])
    rs_in, tok = rs_begin(
        'w_in', jnp.stack([jnp.concatenate(my_cols(gwp, k * wb, (k + 1) * wb), axis=1) for k in range(N_DEV)]))
    (dh_a,) = _mm("d_proj_a", dproj, wp_a, tb=True, outs=((BF16, None),), deps=[tok])
    (dh_b,) = _mm("d_proj_b", dproj, wp_b, tb=True, outs=((BF16, None),), deps=[tok])

    def f_norm_in_bwd(first, last, xv, g, dha, dhb, dxa):
        _, vjp = jax.vjp(_rms, xv, g)
        dx, dg = vjp(jnp.concatenate([dha, dhb], axis=1))
        return (dx + dxa,), (dg,)
    grad_x, g_pre_mix = _rows_call(
        "norm_in_bwd", f_norm_in_bwd, nt, [x2, pre_mix_norm, dh_a, dh_b, dx1],
        [_rt(tr, D), _full(pre_mix_norm), _rt(tr, Dh), _rt(tr, Dh), _rt(tr, D)], [((T, D), F32, _rt(tr, D))],
        [(1, D)], 'tfttt')

    big_out = {}
    for n, state in (('w_mlp_out', rs_mo), ('w_mlp_in', rs_mi), ('w_out', rs_out)):
        big_out[n] = _adamw_big("adamw_" + n, W[n], Mo[n], Vo[n], *rs_direct_end(n, state, grad_x))

    (g_pm8,) = _allgather("allgather_pre_mix_grad", [_own_block(g_pre_mix, dev)], deps=[big_out['w_out'][0]])
    _, s_lands = _split_wait("small_grads_wait", s_sems, s_srcs, s_lands, g_pm8, _ag_copies, 4 * len(s_lands))
    g_narrow, g_wa8, g_wx8 = _ag_finish("small_grads_finish", s_lands)
    summed = dict(zip(narrow, _unflat(_sum8("sum_small_grads", g_narrow), [small[n].shape for n in narrow])))
    summed['pre_mix_norm'] = _sum8("sum_pre_mix_grad", g_pm8)
    summed['lru_w_a'] = _sum8("sum_lru_w_a_grads", g_wa8)
    summed['lru_w_x'] = _sum8("sum_lru_w_x_grads", g_wx8)
    loss = summed.pop('loss').reshape(())
    for n, full_w in (('ssd_conv_w', XBC), ('lru_conv_w', DL)):
        wdt = full_w // N_DEV
        summed[n] = lax.dynamic_slice_in_dim(summed[n], dev * wdt, wdt, axis=1)
    small_params = [n for n in names if n not in big]
    as2d = lambda a: a.reshape(-1, a.shape[-1])
    res = _adamw_small([as2d(W[n]) for n in small_params],
                       [summed[n].reshape(as2d(W[n]).shape) for n in small_params],
                       [as2d(Mo[n]) for n in small_params], [as2d(Vo[n]) for n in small_params])
    grads = {n: summed[n].reshape(W[n].shape) for n in small_params}
    delta, new_m, new_v = ({n: r.reshape(W[n].shape) for n, r in zip(small_params, rs)} for rs in res)

    big_out['w_in'] = _adamw_big("adamw_w_in", W['w_in'], Mo['w_in'], Vo['w_in'], *rs_end('w_in', rs_in, g_pm8))
    for n in big:
        grads[n], delta[n], new_m[n], new_v[n] = big_out[n]

    return (loss, grad_x.reshape(x.shape), *[grads[n] for n in names], *[delta[n] for n in names],
            *[new_m[n] for n in names], *[new_v[n] for n in names])
```

```python
import functools

import jax
import jax.numpy as jnp
from jax import lax
from jax.experimental import pallas as pl
from jax.experimental.pallas import tpu as pltpu

F32, BF16 = jnp.float32, jnp.bfloat16
S = jax.ShapeDtypeStruct
MESH = pl.DeviceIdType.MESH

SSD_GROUPS = 8
LRU_C = 8.0
EPS = 1e-6
CONV_WIDTH = 4
ADAM_LR, ADAM_B1, ADAM_B2, ADAM_EPS, ADAM_WD, ADAM_STEP = 0.001, 0.9, 0.999, 1e-08, 0.01, 10

LANES = 128
SUBLANES = 8
VMEM_LIMIT = 56 * 1024 * 1024
N_DEV = 8
SMALL_W = 512
HI = lax.Precision.HIGHEST


def _pcall(body, **kw):
    return pl.pallas_call(body, **kw)


def _cparams(sem=None, **kw):
    return pltpu.CompilerParams(dimension_semantics=sem, vmem_limit_bytes=VMEM_LIMIT, **kw)


def _pick(n, cands):
    for c in cands:
        if c <= n and n % c == 0:
            return c
    return n


def _rt(tr, w, cb=0, n=None):
    if n is None:
        return pl.BlockSpec((tr, w), lambda i: (i, cb))
    return pl.BlockSpec((tr, w), lambda i: (n - 1 - i, cb))


PACKED_ROWS = 16


def _halo_prev(tr, w, cb=0, n=None, rows=SUBLANES):
    k = tr // rows
    if n is None:
        return pl.BlockSpec((rows, w), lambda i: (jnp.maximum(i * k - 1, 0), cb))
    return pl.BlockSpec((rows, w), lambda i: (jnp.maximum((n - 1 - i) * k - 1, 0), cb))


def _halo_next(tr, w, nt, cb=0, n=None, rows=SUBLANES):
    k = tr // rows
    last = nt * k - 1
    if n is None:
        return pl.BlockSpec((rows, w), lambda i: (jnp.minimum((i + 1) * k, last), cb))
    return pl.BlockSpec((rows, w), lambda i: (jnp.minimum((n - i) * k, last), cb))


def _full(a):
    nd = a.ndim
    return pl.BlockSpec(a.shape, lambda i: (0,) * nd)


def _rows_call(name, fn, n_tiles, arrays, in_specs, out_tiled, out_acc, kinds, into=None, deps=(), cw=None):
    n_in, n_t = len(arrays), len(out_tiled)
    n_skip = len(deps) + (0 if into is None else 1)
    width = in_specs[kinds.index('t')].block_shape[1]
    cols = [(0, width)] if cw is None else [(c, cw) for c in range(0, width, cw)]

    def body(*refs):
        i = pl.program_id(0)
        ins = refs[:n_in]
        outs = refs[n_in + n_skip:n_in + n_skip + n_t]
        accs = refs[n_in + n_skip + n_t:]
        if accs:
            @pl.when(i == 0)
            def _():
                for r in accs:
                    r[...] = jnp.zeros_like(r)

        def lanes(ref, rows, c0, w):
            return ref[rows, c0:c0 + w] if ref.shape[-1] == width else ref[rows, :]

        def load(k, c0, w):
            v = lanes(ins[k], slice(None), c0, w).astype(F32)
            if kinds[k][0] in 'pn' and v.shape[0] == PACKED_ROWS:
                v = v[SUBLANES:] if kinds[k][0] == 'p' else v[:SUBLANES]
            return v

        for c0, w in cols:
            touts, aouts = fn(i == 0, i == n_tiles - 1, *[load(k, c0, w) for k in range(n_in)])
            for r, v in zip(outs, touts):
                if r.shape[-1] == width:
                    r[:, c0:c0 + w] = v.astype(r.dtype)
                else:
                    r[...] = v.astype(r.dtype)
            for r, v in zip(accs, aouts):
                if r.shape[-1] == width:
                    r[:, c0:c0 + w] += v
                else:
                    r[...] += v

    out_shape = [S(sh, dt) for sh, dt, _ in out_tiled] + [S(sh, F32) for sh in out_acc]
    out_specs = [sp for _, _, sp in out_tiled]
    for sh in out_acc:
        out_specs.append(pl.BlockSpec(sh, lambda i, nd=len(sh): (0,) * nd))
    in_specs = list(in_specs) + [_ANY] * len(deps)
    if into is None:
        return _pcall(body, name=name, grid=(n_tiles,), in_specs=in_specs, out_specs=out_specs,
                      out_shape=out_shape, compiler_params=_cparams(("arbitrary",)))(*arrays, *deps)
    return _pcall(body, name=name, grid=(n_tiles,), in_specs=in_specs + [_ANY], out_specs=out_specs,
                  out_shape=out_shape, input_output_aliases={n_in + len(deps): 0},
                  compiler_params=_cparams(("arbitrary",)))(*arrays, *deps, into)


def _rms(x, g):
    return x * lax.rsqrt(jnp.mean(x * x, axis=-1, keepdims=True) + EPS) * g


def _rms_bwd(x, g, dy):
    r = lax.rsqrt(jnp.mean(x * x, axis=-1, keepdims=True) + EPS)
    xr = x * r
    gdy = dy * g
    dx = (gdy - xr * jnp.mean(gdy * xr, axis=-1, keepdims=True)) * r
    return dx, _colsum(dy * xr)


_GELU_C0, _GELU_C1 = 0.7978845608028654, 0.044715


def _gelu_and_grad(x):
    x2 = x * x
    t = jnp.tanh(_GELU_C0 * x * (1.0 + _GELU_C1 * x2))
    half = 0.5 * (1.0 + t)
    grad = half + (0.5 * _GELU_C0) * x * (1.0 - t * t) * (1.0 + (3.0 * _GELU_C1) * x2)
    return x * half, grad


def _colsum(v):
    return jnp.sum(v, axis=0, keepdims=True)


_TILES = (1152, 1024, 896, 768, 640, 512, 384, 256, 128)
_K_TILES = (4096, 3456, 3072, 2688, 2048, 1536, 1344, 1152, 1024, 896, 768, 640, 512, 384, 256, 128)


def _mm(name, a, b, *, ta=False, tb=False, outs=((F32, None),), extra=None, out_blocks=None, tm=None, tn=None, tk=None,
        deps=(), a_cols=None):
    M, K = (a.shape[1], a.shape[0]) if ta else a.shape
    if a_cols is not None:
        assert not ta
        K = a_cols[1]
    b3 = b.ndim == 3
    if b3:
        nb_b, brows, bcols = b.shape
        N = brows if tb else nb_b * bcols
    else:
        N = b.shape[0] if tb else b.shape[1]
    n_lim = N if out_blocks is None else N // out_blocks
    if b3 and not tb:
        n_lim = min(n_lim, bcols)
    tm = tm or _pick(M, _TILES[1:])
    tn = tn or _pick(n_lim, _TILES)
    tk = tk or _pick(bcols if (b3 and tb) else K, _K_TILES)
    nk = K // tk
    assert M % tm == 0 and N % tn == 0 and K % tk == 0
    dn = (((0 if ta else 1,), (1 if tb else 0,)), ((), ()))
    n_extra = 0 if extra is None else 1
    n_out = len(outs)

    def body(*refs):
        a_ref, b_ref = refs[0], refs[1]
        e_ref = refs[2] if n_extra else None
        o_refs = refs[2 + n_extra + len(deps):2 + n_extra + len(deps) + n_out]

        def finish(r):
            e = e_ref[...] if n_extra else None
            for o, (_, f) in zip(o_refs, outs):
                o[...] = (r if f is None else f(r, e)).astype(o.dtype)

        part = lax.dot_general(a_ref[...], b_ref[...], dn, preferred_element_type=F32)
        if nk == 1:
            finish(part)
            return
        acc = refs[-1]
        k = pl.program_id(2)

        @pl.when(k == 0)
        def _():
            acc[...] = part

        @pl.when(jnp.logical_and(k > 0, k < nk - 1))
        def _():
            acc[...] += part

        @pl.when(k == nk - 1)
        def _():
            finish(acc[...] + part)

    k0 = 0 if a_cols is None else a_cols[0] // tk
    a_spec = (pl.BlockSpec((tk, tm), lambda i, j, k: (k, i)) if ta
              else pl.BlockSpec((tm, tk), lambda i, j, k: (i, k + k0)))
    if not b3:
        b_spec = pl.BlockSpec((tn, tk), lambda i, j, k: (j, k)) if tb else pl.BlockSpec((tk, tn), lambda i, j, k: (k, j))
    elif tb:
        per = bcols // tk
        b_spec = pl.BlockSpec((None, tn, tk), lambda i, j, k: (k // per, j, k % per))
    else:
        per = bcols // tn
        b_spec = pl.BlockSpec((None, tk, tn), lambda i, j, k: (j // per, k, j % per))
    o_spec = pl.BlockSpec((tm, tn), lambda i, j, k: (i, j))
    if out_blocks is None:
        out_specs, out_shape = [o_spec] * n_out, [S((M, N), dt) for dt, _ in outs]
    else:
        per_o = N // out_blocks // tn
        ob_spec = pl.BlockSpec((None, tm, tn), lambda i, j, k: (j // per_o, i, j % per_o))
        out_specs, out_shape = [ob_spec] * n_out, [S((out_blocks, M, N // out_blocks), dt) for dt, _ in outs]
    in_specs = [a_spec, b_spec] + ([o_spec] if n_extra else []) + [_ANY] * len(deps)
    args = [a, b] + ([extra] if n_extra else []) + list(deps)
    return _pcall(body, name=name, grid=(M // tm, N // tn, nk), in_specs=in_specs, out_specs=out_specs,
                  out_shape=out_shape, scratch_shapes=[pltpu.VMEM((tm, tn), F32)] if nk > 1 else [],
                  compiler_params=_cparams(("parallel", "parallel", "arbitrary")))(*args)


def _shift_down(x, halo, s):
    if s == 0:
        return x
    r = pltpu.roll(x, s, 0)
    hr = pltpu.roll(halo, s, 0)
    row = lax.broadcasted_iota(jnp.int32, halo.shape, 0)
    top = jnp.where(row < s, hr, r[:SUBLANES])
    if x.shape[0] == SUBLANES:
        return top
    return jnp.concatenate([top, r[SUBLANES:]], axis=0)


def _shift_up(x, nxt, s):
    if s == 0:
        return x
    n = x.shape[0]
    r = pltpu.roll(x, n - s, 0)
    nr = pltpu.roll(nxt, SUBLANES - s, 0)
    row = lax.broadcasted_iota(jnp.int32, nxt.shape, 0)
    bot = jnp.where(row >= SUBLANES - s, nr, r[n - SUBLANES:])
    if n == SUBLANES:
        return bot
    return jnp.concatenate([r[:n - SUBLANES], bot], axis=0)


def _conv_pre(x, halo, w, b):
    acc = b + w[CONV_WIDTH - 1:CONV_WIDTH, :] * x
    for k in range(CONV_WIDTH - 1):
        acc = acc + w[k:k + 1, :] * _shift_down(x, halo, CONV_WIDTH - 1 - k)
    return acc


def _silu_grad(p):
    s = jax.nn.sigmoid(p)
    return s * (1.0 + p * (1.0 - s))


def _conv_bwd_tile(first, last, x, hprev, xnext, d, dnext, w, b, silu):
    hprev = jnp.where(first, 0.0, hprev)
    if silu:
        d = d * _silu_grad(_conv_pre(x, hprev, w, b))
        pre_next = _conv_pre(xnext, x[x.shape[0] - SUBLANES:], w, b)
        dnext = dnext * _silu_grad(pre_next)
    dnext = jnp.where(last, 0.0, dnext)
    dx = w[CONV_WIDTH - 1:CONV_WIDTH, :] * d
    row8 = lax.broadcasted_iota(jnp.int32, (SUBLANES, x.shape[1]), 0)
    dw8 = jnp.where(row8 == CONV_WIDTH - 1, _colsum(d * x), 0.0)
    for k in range(CONV_WIDTH - 1):
        s = CONV_WIDTH - 1 - k
        dx = dx + w[k:k + 1, :] * _shift_up(d, dnext, s)
        dw8 = dw8 + jnp.where(row8 == k, _colsum(d * _shift_down(x, hprev, s)), 0.0)
    return dx, dw8, _colsum(d)


def _ssd_dims(xbc_act, n_heads):
    T, XBC = xbc_act.shape
    GN = XBC // 4
    DS = XBC - 2 * GN
    G = SSD_GROUPS
    N = GN // G
    P = DS // n_heads
    K = n_heads // G
    return T, XBC, DS, GN, G, N, P, K


def _ssd_common(dt, alog, Q):
    a = -jnp.exp(alog)
    adt = dt * a
    li = lax.broadcasted_iota(jnp.int32, (Q, Q), 0)
    si = lax.broadcasted_iota(jnp.int32, (Q, Q), 1)
    causal = li >= si
    ltri = causal.astype(F32)
    acs = jnp.dot(ltri, adt, precision=HI, preferred_element_type=F32)
    acs_row = lax.dot_general(adt, ltri, (((0,), (1,)), ((), ())), precision=HI,
                              preferred_element_type=F32)
    return a, adt, causal, ltri, acs, acs_row


def _expander(g, K, P, W):
    r = lax.broadcasted_iota(jnp.int32, (LANES, W), 0)
    c = lax.broadcasted_iota(jnp.int32, (LANES, W), 1)
    return (c // P + g * K == r).astype(F32)


def _dotb(a, b, dn=(((1,), (0,)), ((), ()))):
    return lax.dot_general(a.astype(BF16), b.astype(BF16), dn, preferred_element_type=F32)


def _dot_split(a, sel, terms, dn=(((1,), (0,)), ((), ()))):
    selb = sel.astype(BF16)
    out = None
    for _ in range(terms):
        piece = a.astype(BF16)
        part = lax.dot_general(piece, selb, dn, preferred_element_type=F32)
        out = part if out is None else out + part
        a = a - piece.astype(F32)
    return out


_NT = (((1,), (1,)), ((), ()))
_TN = (((0,), (0,)), ((), ()))


def _ssd_fwd(xbc_act, dt, alog, dskip, n_heads, Q):
    T, XBC, DS, GN, G, N, P, K = _ssd_dims(xbc_act, n_heads)
    W = K * P
    nc = T // Q

    def body(xs_ref, b_ref, c_ref, dt_ref, alog_ref, d_ref, y_ref, hp_ref, h_scr):
        ci = pl.program_id(0)

        @pl.when(ci == 0)
        def _():
            h_scr[...] = jnp.zeros_like(h_scr)

        dtv = dt_ref[...]
        a, adt, causal, ltri, acs, acs_row = _ssd_common(dtv, alog_ref[...], Q)
        lane_head = lax.broadcasted_iota(jnp.int32, (Q, W), 1) // P
        for g in range(G):
            eg = _expander(g, K, P, W)
            dtb = _dot_split(dtv, eg, 3)
            acsb = _dot_split(acs, eg, 3)
            lastb = acsb[Q - 1:Q, :]
            db = _dot_split(jnp.broadcast_to(d_ref[...], (SUBLANES, LANES)), eg, 3)[0:1, :]
            xg = xs_ref[:, g * W:(g + 1) * W]
            bg = b_ref[:, g * N:(g + 1) * N]
            cg = c_ref[:, g * N:(g + 1) * N]
            xt = xg * dtb
            sc = _dotb(cg, bg, _NT)
            yd = jnp.zeros((Q, W), F32)
            for k in range(K):
                h = g * K + k
                seg = acs[:, h:h + 1] - acs_row[h:h + 1, :]
                lh = jnp.where(causal, jnp.exp(jnp.minimum(seg, 0.0)), 0.0)
                xk = jnp.where(lane_head == k, xt, 0.0)
                yd = yd + _dotb(sc * lh, xk)
            hp = h_scr[g]
            yoff = _dotb(cg, hp) * jnp.exp(acsb)
            y_ref[:, g * W:(g + 1) * W] = yd + yoff + xg * db
            e_end = jnp.exp(lastb - acsb)
            st = _dotb(bg, xt * e_end, _TN)
            hp_ref[0, g] = hp
            h_scr[g] = jnp.exp(lastb) * hp + st

    cb = DS // GN
    in_specs = [pl.BlockSpec((Q, DS), lambda c: (c, 0)),
                pl.BlockSpec((Q, GN), lambda c: (c, cb)),
                pl.BlockSpec((Q, GN), lambda c: (c, cb + 1)),
                pl.BlockSpec((Q, LANES), lambda c: (c, 0)),
                pl.BlockSpec((1, LANES), lambda c: (0, 0)),
                pl.BlockSpec((1, LANES), lambda c: (0, 0))]
    out_specs = [pl.BlockSpec((Q, DS), lambda c: (c, 0)),
                 pl.BlockSpec((1, G, N, W), lambda c: (c, 0, 0, 0))]
    return _pcall(body, name="ssd_fwd", grid=(nc,), in_specs=in_specs, out_specs=out_specs,
                  out_shape=[S((T, DS), F32), S((nc, G, N, W), F32)],
                  scratch_shapes=[pltpu.VMEM((G, N, W), F32)],
                  compiler_params=_cparams(("arbitrary",)))(xbc_act, xbc_act, xbc_act, dt, alog, dskip)


def _ssd_bwd(xbc_act, dt, alog, dskip, hprev, dy, n_heads, Q):
    T, XBC, DS, GN, G, N, P, K = _ssd_dims(xbc_act, n_heads)
    W = K * P
    nc = T // Q

    def body(xs_ref, b_ref, c_ref, dt_ref, alog_ref, d_ref, hp_ref, dy_ref,
             dxbc_ref, ddt_ref, dalog_ref, dd_ref, dh_scr):
        ci = pl.program_id(0)

        @pl.when(ci == 0)
        def _():
            dh_scr[...] = jnp.zeros_like(dh_scr)
            dalog_ref[...] = jnp.zeros_like(dalog_ref)
            dd_ref[...] = jnp.zeros_like(dd_ref)

        dtv = dt_ref[...]
        a, adt, causal, ltri, acs, acs_row = _ssd_common(dtv, alog_ref[...], Q)
        lane_head = lax.broadcasted_iota(jnp.int32, (Q, W), 1) // P
        lane128 = lax.broadcasted_iota(jnp.int32, (Q, LANES), 1)
        sub128 = lax.broadcasted_iota(jnp.int32, (LANES, Q), 0)
        rowq = lax.broadcasted_iota(jnp.int32, (Q, W), 0)
        dacs = jnp.zeros((Q, LANES), F32)
        dacs_row = jnp.zeros((LANES, Q), F32)
        ddt = jnp.zeros((Q, LANES), F32)
        dd_acc = jnp.zeros((1, LANES), F32)
        for g in range(G):
            eg = _expander(g, K, P, W)
            dtb = _dot_split(dtv, eg, 3)
            acsb = _dot_split(acs, eg, 3)
            lastb = acsb[Q - 1:Q, :]
            db = _dot_split(jnp.broadcast_to(d_ref[...], (SUBLANES, LANES)), eg, 3)[0:1, :]
            xg = xs_ref[:, g * W:(g + 1) * W]
            bg = b_ref[:, g * N:(g + 1) * N]
            cg = c_ref[:, g * N:(g + 1) * N]
            dyg = dy_ref[:, g * W:(g + 1) * W]
            hp = hp_ref[0, g]
            dhn = dh_scr[g]
            xt = xg * dtb
            sc = _dotb(cg, bg, _NT)
            eacs = jnp.exp(acsb)
            e_end = jnp.exp(lastb - acsb)
            elast = jnp.exp(lastb)

            wv = dyg * eacs
            dcg = _dotb(wv, hp, _NT)
            dhp = _dotb(cg, wv, _TN) + elast * dhn
            dacsb = dyg * (_dotb(cg, hp) * eacs)

            xe = xt * e_end
            dbg = _dotb(xe, dhn, _NT)
            v = _dotb(bg, dhn)
            dxt = v * e_end
            de = v * xe
            dacsb = dacsb - de
            dlastb = _colsum(de) + elast * jnp.sum(dhn * hp, axis=0, keepdims=True)

            dsc = jnp.zeros((Q, Q), F32)
            for k in range(K):
                h = g * K + k
                seg = acs[:, h:h + 1] - acs_row[h:h + 1, :]
                lh = jnp.where(causal, jnp.exp(jnp.minimum(seg, 0.0)), 0.0)
                mh = sc * lh
                dyk = jnp.where(lane_head == k, dyg, 0.0)
                dxt = dxt + jnp.where(lane_head == k, _dotb(mh, dyg, _TN), 0.0)
                dm = _dotb(dyk, xt, _NT)
                dsc = dsc + dm * lh
                gm = dm * mh
                dacs = dacs + jnp.where(lane128 == h, jnp.sum(gm, axis=1, keepdims=True), 0.0)
                dacs_row = dacs_row - jnp.where(sub128 == h, jnp.sum(gm, axis=0, keepdims=True), 0.0)
            dcg = dcg + _dotb(dsc, bg)
            dbg = dbg + _dotb(dsc, cg, _TN)

            dacsb = dacsb + jnp.where(rowq == Q - 1, dlastb, 0.0)
            dacs = dacs + _dot_split(dacsb, eg, 2, _NT)
            ddt = ddt + _dot_split(dxt * xg, eg, 2, _NT)
            dd_acc = dd_acc + _dot_split(jnp.broadcast_to(_colsum(dyg * xg), (SUBLANES, W)), eg, 2, _NT)[0:1, :]
            dxbc_ref[:, g * W:(g + 1) * W] = dxt * dtb + dyg * db
            dxbc_ref[:, DS + g * N:DS + (g + 1) * N] = dbg
            dxbc_ref[:, DS + GN + g * N:DS + GN + (g + 1) * N] = dcg
            dh_scr[g] = dhp

        eye = (lax.broadcasted_iota(jnp.int32, (LANES, LANES), 0) ==
               lax.broadcasted_iota(jnp.int32, (LANES, LANES), 1)).astype(F32)
        dacs = dacs + lax.dot_general(dacs_row, eye, _TN, precision=HI, preferred_element_type=F32)
        dadt = lax.dot_general(ltri, dacs, _TN, precision=HI, preferred_element_type=F32)
        ddt_ref[...] = ddt + dadt * a
        dalog_ref[...] += _colsum(dadt * dtv) * a
        dd_ref[...] += dd_acc

    cb = DS // GN
    rv = lambda c: nc - 1 - c
    in_specs = [pl.BlockSpec((Q, DS), lambda c: (rv(c), 0)),
                pl.BlockSpec((Q, GN), lambda c: (rv(c), cb)),
                pl.BlockSpec((Q, GN), lambda c: (rv(c), cb + 1)),
                pl.BlockSpec((Q, LANES), lambda c: (rv(c), 0)),
                pl.BlockSpec((1, LANES), lambda c: (0, 0)),
                pl.BlockSpec((1, LANES), lambda c: (0, 0)),
                pl.BlockSpec((1, G, N, W), lambda c: (rv(c), 0, 0, 0)),
                pl.BlockSpec((Q, DS), lambda c: (rv(c), 0))]
    out_specs = [pl.BlockSpec((Q, XBC), lambda c: (rv(c), 0)),
                 pl.BlockSpec((Q, LANES), lambda c: (rv(c), 0)),
                 pl.BlockSpec((1, LANES), lambda c: (0, 0)),
                 pl.BlockSpec((1, LANES), lambda c: (0, 0))]
    return _pcall(body, name="ssd_bwd", grid=(nc,), in_specs=in_specs, out_specs=out_specs,
                  out_shape=[S((T, XBC), F32), S((T, LANES), F32), S((1, LANES), F32), S((1, LANES), F32)],
                  scratch_shapes=[pltpu.VMEM((G, N, W), F32)],
                  compiler_params=_cparams(("arbitrary",)))(
                      xbc_act, xbc_act, xbc_act, dt, alog, dskip, hprev, dy)


def _blockdiag(x, w_ref, dn=(((1,), (0,)), ((), ()))):
    H, B, _ = w_ref.shape
    return jnp.concatenate([_dotb(x[:, h * B:(h + 1) * B], w_ref[h], dn) for h in range(H)], axis=1)


def _lru_elem(xl, r_pre, i_pre, lam):
    r = jax.nn.sigmoid(r_pre)
    i = jax.nn.sigmoid(i_pre)
    log_a = -LRU_C * r * jax.nn.softplus(-lam)
    a = jnp.exp(log_a)
    u = jnp.sqrt(1.0 - jnp.exp(2.0 * log_a)) * (i * xl)
    return a, u


def _lru_elem_bwd(xl, r_pre, i_pre, lam, da, du):
    r = jax.nn.sigmoid(r_pre)
    i = jax.nn.sigmoid(i_pre)
    sp = jax.nn.softplus(-lam)
    a = jnp.exp(-LRU_C * r * sp)
    s = jnp.sqrt(1.0 - a * a)
    d_ix = du * s
    dlog_a = (da - du * (i * xl) * a / s) * a
    dr_pre = dlog_a * (-LRU_C) * sp * r * (1.0 - r)
    dlam = _colsum(dlog_a * r) * (LRU_C * jax.nn.sigmoid(-lam))
    di_pre = d_ix * xl * i * (1.0 - i)
    return d_ix * i, dr_pre, di_pre, dlam


def _lru_gates_fwd(xl, w_a, b_a, w_x, b_x, lam, tr):
    T, DL = xl.shape

    def body(xl_ref, wa_ref, ba_ref, wx_ref, bx_ref, lam_ref, a_ref, u_ref):
        x = xl_ref[...]
        r_pre = _blockdiag(x, wa_ref) + ba_ref[...]
        i_pre = _blockdiag(x, wx_ref) + bx_ref[...]
        a, u = _lru_elem(x, r_pre, i_pre, lam_ref[...])
        a_ref[...] = a
        u_ref[...] = u

    w3 = pl.BlockSpec(w_a.shape, lambda i: (0, 0, 0))
    vec = pl.BlockSpec((1, DL), lambda i: (0, 0))
    return _pcall(body, name="lru_gates_fwd", grid=(T // tr,),
                  in_specs=[_rt(tr, DL), w3, vec, w3, vec, vec],
                  out_specs=[_rt(tr, DL), _rt(tr, DL)], out_shape=[S((T, DL), F32), S((T, DL), F32)],
                  compiler_params=_cparams(("parallel",)))(xl, w_a, b_a, w_x, b_x, lam)


def _lru_gates_bwd(xl, w_a, b_a, w_x, b_x, lam, da, du, tr):
    T, DL = xl.shape
    H, B, _ = w_a.shape

    def body(xl_ref, wa_ref, ba_ref, wx_ref, bx_ref, lam_ref, da_ref, du_ref,
             dxl_ref, dwa_ref, dba_ref, dwx_ref, dbx_ref, dlam_ref):
        @pl.when(pl.program_id(0) == 0)
        def _():
            for r in (dwa_ref, dba_ref, dwx_ref, dbx_ref, dlam_ref):
                r[...] = jnp.zeros_like(r)

        x = xl_ref[...]
        r_pre = _blockdiag(x, wa_ref) + ba_ref[...]
        i_pre = _blockdiag(x, wx_ref) + bx_ref[...]
        dx, dr, di, dlam = _lru_elem_bwd(x, r_pre, i_pre, lam_ref[...], da_ref[...], du_ref[...])
        dxl_ref[...] = dx + _blockdiag(dr, wa_ref, _NT) + _blockdiag(di, wx_ref, _NT)
        for h in range(H):
            xh = x[:, h * B:(h + 1) * B]
            dwa_ref[h] += _dotb(xh, dr[:, h * B:(h + 1) * B], _TN)
            dwx_ref[h] += _dotb(xh, di[:, h * B:(h + 1) * B], _TN)
        dba_ref[...] += _colsum(dr)
        dbx_ref[...] += _colsum(di)
        dlam_ref[...] += dlam

    w3 = pl.BlockSpec(w_a.shape, lambda i: (0, 0, 0))
    vec = pl.BlockSpec((1, DL), lambda i: (0, 0))
    return _pcall(body, name="lru_gates_bwd", grid=(T // tr,),
                  in_specs=[_rt(tr, DL), w3, vec, w3, vec, vec, _rt(tr, DL), _rt(tr, DL)],
                  out_specs=[_rt(tr, DL), w3, vec, w3, vec, vec],
                  out_shape=[S((T, DL), F32), S(w_a.shape, F32), S((1, DL), F32), S(w_a.shape, F32),
                             S((1, DL), F32), S((1, DL), F32)],
                  compiler_params=_cparams(("arbitrary",)))(xl, w_a, b_a, w_x, b_x, lam, da, du)


def _groups(v):
    return v.reshape(v.shape[0] // SUBLANES, SUBLANES, v.shape[1])


def _rows_shifted(v, edge, up):
    sub = lax.broadcasted_iota(jnp.int32, v.shape, 1)
    if up:
        other = jnp.concatenate([v[1:], edge[None]], axis=0)
        return jnp.where(sub < SUBLANES - 1, pltpu.roll(v, SUBLANES - 1, 1), pltpu.roll(other, SUBLANES - 1, 1))
    other = jnp.concatenate([edge[None], v[:-1]], axis=0)
    return jnp.where(sub >= 1, pltpu.roll(v, 1, 1), pltpu.roll(other, 1, 1))


def _scan_tile(a, u, entering, emit, up):
    G = a.shape[0]
    sub = lax.broadcasted_iota(jnp.int32, a.shape, 1)
    d = 1
    while d < SUBLANES:
        if up:
            keep = sub < SUBLANES - d
            a_s = jnp.where(keep, pltpu.roll(a, SUBLANES - d, 1), 1.0)
            u_s = jnp.where(keep, pltpu.roll(u, SUBLANES - d, 1), 0.0)
        else:
            keep = sub >= d
            a_s = jnp.where(keep, pltpu.roll(a, d, 1), 1.0)
            u_s = jnp.where(keep, pltpu.roll(u, d, 1), 0.0)
        u = a * u_s + u
        a = a * a_s
        d *= 2
    for g in (reversed(range(G)) if up else range(G)):
        hg = u[g] + a[g] * entering
        emit(g, hg)
        entering = hg[0:1] if up else hg[SUBLANES - 1:SUBLANES]
    return entering


def _lru_scan_fwd(a, u, tr, deps=()):
    T, DL = a.shape

    def body(a_ref, u_ref, *rest):
        h_ref, carry = rest[len(deps):]

        @pl.when(pl.program_id(0) == 0)
        def _():
            carry[...] = jnp.zeros_like(carry)

        def emit(g, hg):
            h_ref[g * SUBLANES:(g + 1) * SUBLANES, :] = hg

        last = _scan_tile(_groups(a_ref[...]), _groups(u_ref[...]), carry[0:1, :], emit, up=False)
        carry[...] = jnp.broadcast_to(last, carry.shape)

    return _pcall(body, name="lru_scan_fwd", grid=(T // tr,),
                  in_specs=[_rt(tr, DL), _rt(tr, DL)] + [_ANY] * len(deps),
                  out_specs=_rt(tr, DL), out_shape=S((T, DL), F32),
                  scratch_shapes=[pltpu.VMEM((SUBLANES, DL), F32)],
                  compiler_params=_cparams(("arbitrary",)))(a, u, *deps)


def _lru_scan_bwd(a, h, dh, tr):
    T, DL = a.shape
    n = T // tr

    def body(a_ref, an_ref, h_ref, hp_ref, dh_ref, du_ref, da_ref, carry):
        i = pl.program_id(0)
        ti = n - 1 - i

        @pl.when(i == 0)
        def _():
            carry[...] = jnp.zeros_like(carry)

        a_next = _rows_shifted(_groups(a_ref[...]), jnp.where(ti == n - 1, 0.0, an_ref[...]), up=True)
        h_prev = _rows_shifted(_groups(h_ref[...]), jnp.where(ti == 0, 0.0, hp_ref[...]), up=False)

        def emit(g, gg):
            du_ref[g * SUBLANES:(g + 1) * SUBLANES, :] = gg
            da_ref[g * SUBLANES:(g + 1) * SUBLANES, :] = gg * h_prev[g]

        top = _scan_tile(a_next, _groups(dh_ref[...]), carry[0:1, :], emit, up=True)
        carry[...] = jnp.broadcast_to(top, carry.shape)

    return _pcall(body, name="lru_scan_bwd", grid=(n,),
                  in_specs=[_rt(tr, DL, 0, n), _halo_next(tr, DL, n, 0, n), _rt(tr, DL, 0, n),
                            _halo_prev(tr, DL, 0, n), _rt(tr, DL, 0, n)],
                  out_specs=[_rt(tr, DL, 0, n), _rt(tr, DL, 0, n)],
                  out_shape=[S((T, DL), F32), S((T, DL), F32)],
                  scratch_shapes=[pltpu.VMEM((SUBLANES, DL), F32)],
                  compiler_params=_cparams(("arbitrary",)))(a, a, h, h, dh)


def _adamw(w, g, m, v):
    m = ADAM_B1 * m + (1.0 - ADAM_B1) * g
    v = ADAM_B2 * v + (1.0 - ADAM_B2) * (g * g)
    m_hat = m / (1.0 - ADAM_B1 ** ADAM_STEP)
    v_hat = v / (1.0 - ADAM_B2 ** ADAM_STEP)
    delta = -ADAM_LR * (m_hat / (jnp.sqrt(v_hat) + ADAM_EPS) + ADAM_WD * w)
    return delta, m, v


def _adamw_big(name, w, m, v, own, recv, own_idx):
    _, R, C = w.shape
    n_recv = recv.shape[0]
    tr = _pick(R, (256, 128, 64, 32, 16))

    def body(idx_ref, w_ref, m_ref, v_ref, p_ref, *rest):
        g = p_ref[...].astype(F32)
        for r in rest[:n_recv]:
            g = g + r[...].astype(F32)
        g_ref, d_ref, nm_ref, nv_ref = rest[n_recv:]
        d, nm, nv = _adamw(w_ref[...], g, m_ref[...], v_ref[...])
        g_ref[...] = g
        d_ref[...] = d
        nm_ref[...] = nm
        nv_ref[...] = nv

    r_spec = lambda s: pl.BlockSpec((None, tr, C), lambda i, idx: (s, i, 0))
    t2 = r_spec(0)
    gs = pltpu.PrefetchScalarGridSpec(
        num_scalar_prefetch=1, grid=(R // tr,),
        in_specs=[t2, t2, t2, pl.BlockSpec((None, tr, C), lambda i, idx: (idx[0], i, 0))]
        + [r_spec(s) for s in range(n_recv)],
        out_specs=[t2, t2, t2, t2])
    return _pcall(body, name=name, grid_spec=gs, out_shape=[S((1, R, C), F32)] * 4,
                  compiler_params=_cparams(("parallel",)))(own_idx, w, m, v, own, *([recv] * n_recv))


def _adamw_small(ws, gs, ms, vs):
    n = len(ws)

    def body(*refs):
        for k in range(n):
            d, nm, nv = _adamw(refs[k][...], refs[n + k][...], refs[2 * n + k][...], refs[3 * n + k][...])
            refs[4 * n + k][...] = d
            refs[5 * n + k][...] = nm
            refs[6 * n + k][...] = nv

    res = _pcall(body, name="adamw_small", out_shape=[S(w.shape, F32) for w in ws] * 3,
                 compiler_params=_cparams())(*ws, *gs, *ms, *vs)
    return res[:n], res[n:2 * n], res[2 * n:]


def _sum8(name, parts):
    _, R, C = parts.shape

    def body(p_ref, o_ref):
        acc = p_ref[0]
        for k in range(1, N_DEV):
            acc = acc + p_ref[k]
        o_ref[...] = acc

    return _pcall(body, name=name, out_shape=S((R, C), F32), compiler_params=_cparams())(parts)


def _pair_sum(name, full, recv, c_idx):
    _, R, C = full.shape
    tr = _pick(R, (256, 128, 64, 32, 16))

    def body(c_ref, f_ref, r_ref, o_ref):
        o_ref[...] = (f_ref[...].astype(F32) + r_ref[...].astype(F32)).astype(o_ref.dtype)

    gs = pltpu.PrefetchScalarGridSpec(
        num_scalar_prefetch=1, grid=(4, R // tr),
        in_specs=[pl.BlockSpec((None, tr, C), lambda j, i, c: (2 * j + c[0], i, 0)),
                  pl.BlockSpec((None, tr, C), lambda j, i, c: (j, i, 0))],
        out_specs=pl.BlockSpec((None, tr, C), lambda j, i, c: (j, i, 0)))
    return _pcall(body, name=name, grid_spec=gs, out_shape=S((4, R, C), BF16),
                  compiler_params=_cparams(("parallel", "parallel")))(c_idx, full, recv)


def _cast_bf16(name, w, dev_idx, row0=0, rows=None, deps=()):
    C = w.shape[2]
    R = w.shape[1] if rows is None else rows
    tr = _pick(R, (256, 128, 64, 32, 16))
    b0 = row0 // tr

    def body(d_ref, w_ref, *rest):
        rest[-1][...] = w_ref[...].astype(BF16)

    gs = pltpu.PrefetchScalarGridSpec(
        num_scalar_prefetch=1, grid=(R // tr,),
        in_specs=[pl.BlockSpec((None, tr, C), lambda i, d: (0, i + b0, 0))] + [_ANY] * len(deps),
        out_specs=pl.BlockSpec((None, tr, C), lambda i, d: (d[0], i, 0)))
    return _pcall(body, name=name, grid_spec=gs, out_shape=S((N_DEV, R, C), BF16),
                  compiler_params=_cparams(("parallel",)))(dev_idx, w, *deps)


def _own_block(v, dev):
    return lax.dynamic_update_slice(lax.empty((N_DEV,) + v.shape, v.dtype), v[None], (dev,) + (0,) * v.ndim)


_ANY = pl.BlockSpec(memory_space=pl.ANY)


def _position():
    return lax.axis_index("x"), lax.axis_index("y"), lax.axis_index("c")


def _allgather(name, bufs, deps=()):
    n = len(bufs)
    nd = len(deps)

    def body(*refs):
        outs = refs[n + nd:2 * n + nd]
        send, recv = refs[2 * n + nd:]
        x, y, c = _position()
        me, sib = (x, y, c), (x, y, 1 - c)
        chips = [(1 - x, y), (x, 1 - y), (1 - x, 1 - y)]

        def copy(a, k, block, to):
            bx, by, bc = block
            blk = outs[a].at[4 * bx + 2 * by + bc]
            return pltpu.make_async_remote_copy(
                src_ref=blk, dst_ref=blk, send_sem=send.at[a, k], recv_sem=recv.at[a, k],
                device_id=to, device_id_type=MESH)

        first = []
        for a in range(n):
            first.append(copy(a, 0, me, sib))
            first += [copy(a, 1 + j, me, (*chip, c)) for j, chip in enumerate(chips)]
        for cp in first:
            cp.start()
        passed = []
        for j, chip in enumerate(chips):
            for a in range(n):
                copy(a, 1 + j, (*chip, c), me).wait_recv()
                cp = copy(a, 4 + j, (*chip, c), sib)
                cp.start()
                passed.append(cp)
        for a in range(n):
            copy(a, 0, sib, me).wait_recv()
        for j, chip in enumerate(chips):
            for a in range(n):
                copy(a, 4 + j, (*chip, 1 - c), me).wait_recv()
        for cp in first + passed:
            cp.wait_send()

    return _pcall(body, name=name, in_specs=[_ANY] * (n + nd), out_specs=[_ANY] * n,
                  out_shape=[S(b.shape, b.dtype) for b in bufs], input_output_aliases={a: a for a in range(n)},
                  scratch_shapes=[pltpu.SemaphoreType.DMA((n, 7)), pltpu.SemaphoreType.DMA((n, 7))])(*bufs, *deps)


def _rs_sibling(name, fulls):
    n = len(fulls)

    def body(*refs):
        ins, outs = refs[:n], refs[n:2 * n]
        send, recv = refs[2 * n:]
        x, y, c = _position()
        copies = []
        for a in range(n):
            for j in range(4):
                copies.append(pltpu.make_async_remote_copy(
                    src_ref=ins[a].at[2 * j + (1 - c)], dst_ref=outs[a].at[j], send_sem=send.at[a, j],
                    recv_sem=recv.at[a, j], device_id=(x, y, 1 - c), device_id_type=MESH))
        for cp in copies:
            cp.start()
        for cp in copies:
            cp.wait()

    return _pcall(body, name=name, in_specs=[_ANY] * n, out_specs=[_ANY] * n,
                  out_shape=[S((4,) + f.shape[1:], f.dtype) for f in fulls],
                  scratch_shapes=[pltpu.SemaphoreType.DMA((n, 4)), pltpu.SemaphoreType.DMA((n, 4))])(*fulls)


_HBM = pl.BlockSpec(memory_space=pltpu.HBM)
_SEM = pl.BlockSpec(memory_space=pltpu.SEMAPHORE)
_EFFECT = pltpu.SideEffectType.DATAFLOW_SIDE_EFFECTING


def _remote_copies(copies_fn, srcs, lands, send, recv):
    x, y, c = _position()
    return [pltpu.make_async_remote_copy(src_ref=s, dst_ref=d, send_sem=send[i], recv_sem=recv[i], device_id=to,
                                         device_id_type=MESH)
            for i, (s, d, to) in enumerate(copies_fn(x, y, c, srcs, lands))]


def _split_start(name, srcs, lands, copies_fn, nc, after=()):
    n, nl, na = len(srcs), len(lands), len(after)

    def body(*refs):
        src_refs, land_refs = refs[:n], refs[n:n + nl]
        outs = refs[n + nl + na:]
        for cp in _remote_copies(copies_fn, src_refs, land_refs, outs[:nc], outs[nc:2 * nc]):
            cp.start()
        outs[-1][...] = jnp.zeros_like(outs[-1])

    hbm = lambda a: pltpu.with_memory_space_constraint(a, pltpu.HBM)
    res = _pcall(
        body, name=name, in_specs=[_HBM] * (n + nl) + [_ANY] * na,
        out_specs=[_SEM] * (2 * nc) + [_HBM] * (n + nl) + [pl.BlockSpec(memory_space=pltpu.VMEM)],
        out_shape=[pltpu.SemaphoreType.DMA(())] * (2 * nc) + [pltpu.HBM(s.shape, s.dtype) for s in srcs]
        + [pltpu.HBM(l.shape, l.dtype) for l in lands] + [S((SUBLANES, LANES), F32)],
        input_output_aliases={i: 2 * nc + i for i in range(n + nl)},
        compiler_params=pltpu.CompilerParams(has_side_effects=_EFFECT),
    )(*[hbm(s) for s in srcs], *[hbm(l) for l in lands], *after)
    return res[:2 * nc], res[2 * nc:2 * nc + n], res[2 * nc + n:2 * nc + n + nl], res[-1]


def _split_wait(name, sems, srcs, lands, after, copies_fn, nc):
    n, nl = len(srcs), len(lands)

    def body(*refs):
        src_refs, land_refs = refs[:n], refs[n:n + nl]
        sem_refs = refs[n + nl:n + nl + 2 * nc]
        for cp in _remote_copies(copies_fn, src_refs, land_refs, sem_refs[:nc], sem_refs[nc:]):
            cp.wait_send()
            cp.wait_recv()

    res = _pcall(
        body, name=name, in_specs=[_HBM] * (n + nl) + [_SEM] * (2 * nc) + [_ANY],
        out_specs=[_HBM] * (n + nl), out_shape=[pltpu.HBM(a.shape, a.dtype) for a in list(srcs) + list(lands)],
        input_output_aliases={i: i for i in range(n + nl)},
        compiler_params=pltpu.CompilerParams(has_side_effects=_EFFECT),
    )(*srcs, *lands, *sems, after)
    return res[:n], res[n:]


def _other_chips(x, y):
    return [(1 - x, y), (x, 1 - y), (1 - x, 1 - y)]


def _ag_copies(x, y, c, srcs, lands):
    out = []
    for land in lands:
        blk = land.at[4 * x + 2 * y + c]
        out.append((blk, blk, (x, y, 1 - c)))
        out += [(blk, blk, (px, py, c)) for px, py in _other_chips(x, y)]
    return out


def _rs_copies(x, y, c, srcs, lands):
    return [(s.at[2 * px + py], land.at[j], (px, py, c))
            for s, land in zip(srcs, lands) for j, (px, py) in enumerate(_other_chips(x, y))]


def _fwd_copies(x, y, c, srcs, lands):
    out = []
    for land in lands:
        for px, py in _other_chips(x, y):
            blk = land.at[4 * px + 2 * py + c]
            out.append((blk, blk, (x, y, 1 - c)))
    return out


def _direct_copies(x, y, c, srcs, lands):
    out = []
    for s, land in zip(srcs, lands):
        for r in range(1, N_DEV):
            px = 1 - x if r & 4 else x
            py = 1 - y if r & 2 else y
            pc = 1 - c if r & 1 else c
            out.append((s.at[4 * px + 2 * py + pc], land.at[r - 1], (px, py, pc)))
    return out


def _ag_finish(name, lands):
    n = len(lands)

    def body(*refs):
        outs = refs[n:2 * n]
        send, recv = refs[2 * n:]
        x, y, c = _position()

        def swap(a, j, px, py, pc):
            blk = outs[a].at[4 * px + 2 * py + pc]
            return pltpu.make_async_remote_copy(src_ref=blk, dst_ref=blk, send_sem=send.at[a, j], recv_sem=recv.at[a, j],
                                                device_id=(x, y, 1 - c), device_id_type=MESH)

        chips = _other_chips(x, y)
        sends = [swap(a, j, px, py, c) for a in range(n) for j, (px, py) in enumerate(chips)]
        for cp in sends:
            cp.start()
        for a in range(n):
            for j, (px, py) in enumerate(chips):
                swap(a, j, px, py, 1 - c).wait_recv()
        for cp in sends:
            cp.wait_send()

    return _pcall(body, name=name, in_specs=[_ANY] * n, out_specs=[_ANY] * n,
                  out_shape=[S(l.shape, l.dtype) for l in lands], input_output_aliases={a: a for a in range(n)},
                  scratch_shapes=[pltpu.SemaphoreType.DMA((n, 3)), pltpu.SemaphoreType.DMA((n, 3))])(*lands)


def _pad_lanes(v):
    return jnp.pad(v, ((0, 0), (0, LANES - v.shape[1])))


def _flat_rows(pieces):
    flat = jnp.concatenate([p.reshape(-1) for p in pieces])
    rows = -(-flat.shape[0] // (SMALL_W * SUBLANES)) * SUBLANES
    return jnp.pad(flat, (0, rows * SMALL_W - flat.shape[0])).reshape(rows, SMALL_W)


def _unflat(buf, shapes):
    flat = buf.reshape(-1)
    out, off = [], 0
    for sh in shapes:
        n = 1
        for d in sh:
            n *= d
        out.append(flat[off:off + n].reshape(sh))
        off += n
    return out


def kernel(x, pre_mix_norm, w_in, ssd_conv_w, ssd_conv_b, ssd_dt_bias, ssd_a_log, ssd_d, ssd_norm, lru_conv_w, lru_conv_b, lru_w_a, lru_b_a, lru_w_x, lru_b_x, lru_lambda, lru_norm, w_out, post_mix_norm, pre_mlp_norm, w_mlp_in, w_mlp_out, post_mlp_norm, loss_target, m_pre_mix_norm, m_w_in, m_ssd_conv_w, m_ssd_conv_b, m_ssd_dt_bias, m_ssd_a_log, m_ssd_d, m_ssd_norm, m_lru_conv_w, m_lru_conv_b, m_lru_w_a, m_lru_b_a, m_lru_w_x, m_lru_b_x, m_lru_lambda, m_lru_norm, m_w_out, m_post_mix_norm, m_pre_mlp_norm, m_w_mlp_in, m_w_mlp_out, m_post_mlp_norm, v_pre_mix_norm, v_w_in, v_ssd_conv_w, v_ssd_conv_b, v_ssd_dt_bias, v_ssd_a_log, v_ssd_d, v_ssd_norm, v_lru_conv_w, v_lru_conv_b, v_lru_w_a, v_lru_b_a, v_lru_w_x, v_lru_b_x, v_lru_lambda, v_lru_norm, v_w_out, v_post_mix_norm, v_pre_mlp_norm, v_w_mlp_in, v_w_mlp_out, v_post_mlp_norm):
    names = ['pre_mix_norm', 'w_in', 'ssd_conv_w', 'ssd_conv_b', 'ssd_dt_bias', 'ssd_a_log', 'ssd_d', 'ssd_norm',
             'lru_conv_w', 'lru_conv_b', 'lru_w_a', 'lru_b_a', 'lru_w_x', 'lru_b_x', 'lru_lambda', 'lru_norm',
             'w_out', 'post_mix_norm', 'pre_mlp_norm', 'w_mlp_in', 'w_mlp_out', 'post_mlp_norm']
    loc = locals()
    W = {n: loc[n] for n in names}
    Mo = {n: loc["m_" + n] for n in names}
    Vo = {n: loc["v_" + n] for n in names}
    big = ['w_in', 'w_out', 'w_mlp_in', 'w_mlp_out']

    px, py, pc = _position()
    dev = 4 * px + 2 * py + pc
    dev_idx = jnp.reshape(dev, (1,)).astype(jnp.int32)
    c_idx = jnp.reshape(pc, (1,)).astype(jnp.int32)
    chip_idx = jnp.reshape(2 * px + py, (1,)).astype(jnp.int32)

    _, T, D = x.shape
    x2 = x.reshape(T, D)
    tgt = loss_target.reshape(T, D)
    n_heads = ssd_dt_bias.shape[1]
    XBC = ssd_conv_b.shape[1]
    GN = XBC // 4
    DS = XBC - 2 * GN
    DL = lru_norm.shape[1]
    DFF = w_mlp_in.shape[2] * N_DEV
    DIN = w_in.shape[2] * N_DEV
    NP = XBC + DS + 2 * DL + LANES
    assert DS % GN == 0 and XBC % DS == 0 and DS == DL and n_heads <= LANES
    cb_z, cb_gate, cb_xl, cb_dt = XBC // DS, XBC // DS + 1, XBC // DS + 2, (XBC + DS + 2 * DL) // LANES
    tr = min(256, T // 2)
    nt = T // tr
    Q = min(256, T // 2)

    Dh = D // 2
    sh_a = _cast_bf16("cast_w_in_a", W['w_in'], dev_idx, 0, Dh)
    sh_b = _cast_bf16("cast_w_in_b", W['w_in'], dev_idx, Dh, Dh)
    later = big[1:]
    a_bufs = [sh_a, _own_block(ssd_conv_w[0], dev), _own_block(lru_conv_w[0], dev)]
    a_sems, _, a_lands, a_token = _split_start("allgather_w_in_a_start", [], a_bufs, _ag_copies, 4 * len(a_bufs))
    sh = {n: _cast_bf16("cast_" + n, W[n], dev_idx, deps=[a_token]) for n in later}

    def f_norm_in(first, last, xv, g):
        return (_rms(xv, g),), ()
    (h,) = _rows_call("norm_in", f_norm_in, nt, [x2, pre_mix_norm], [_rt(tr, D), _full(pre_mix_norm)],
                      [((T, D), BF16, _rt(tr, D))], [], 'tf', deps=[a_token])
    _, a_lands = _split_wait("allgather_w_in_a_wait", a_sems, [], a_lands, h, _ag_copies, 4 * len(a_bufs))
    g_in_a, g_cs, g_cl = _ag_finish("allgather_w_in_a_finish", a_lands)
    b_sems, _, b_lands, b_token = _split_start("allgather_w_in_b_start", [], [sh_b], _ag_copies, 4, after=[g_in_a])
    ag_sems, ag_srcs, ag_lands, ag_token = _split_start(
        "allgather_later_start", [], [sh[n] for n in later], _ag_copies, 4 * len(later), after=[b_token])
    conv_s = jnp.transpose(g_cs, (1, 0, 2)).reshape(CONV_WIDTH, XBC)
    conv_l = jnp.transpose(g_cl, (1, 0, 2)).reshape(CONV_WIDTH, DL)
    wb = DIN // N_DEV
    o_z, o_xbc, o_dt, o_gate, o_xl = 0, DS, DS + XBC, DS + XBC + n_heads, DS + XBC + n_heads + DL
    segs = [(o_xbc, o_xbc + XBC, 0), (o_z, o_z + DS, XBC), (o_gate, o_gate + DL, XBC + DS),
            (o_xl, o_xl + DL, XBC + DS + DL), (o_dt, o_dt + n_heads, NP - LANES)]

    def ref_cols(g, lo, hi):
        out = []
        while lo < hi:
            k = lo // wb
            e = min(hi, (k + 1) * wb)
            out.append(g[k, :, lo - k * wb:e - k * wb])
            lo = e
        return out

    def laid_out(g):
        return jnp.concatenate([p for a, b, _ in segs for p in ref_cols(g, a, b)]
                               + [jnp.zeros((g.shape[1], LANES - n_heads), BF16)], axis=1)

    def my_cols(g, lo, hi):
        out = []
        for a, b, m in sorted(segs):
            s, e = max(lo, a), min(hi, b)
            if s < e:
                out.append(g[:, m + s - a:m + e - a])
        return out

    wp_a = laid_out(g_in_a)
    dt_bias = _pad_lanes(ssd_dt_bias)
    a_log = _pad_lanes(ssd_a_log)
    d_skip = _pad_lanes(ssd_d)
    wa_b, wx_b = lru_w_a[0].astype(BF16), lru_w_x[0].astype(BF16)
    b_a, b_x = lru_b_a.reshape(1, DL), lru_b_x.reshape(1, DL)

    (proj_a,) = _mm("proj_a", h, wp_a, a_cols=(0, Dh), outs=((BF16, None),), deps=[ag_token])
    (dt_a,) = _mm("proj_dt_a", h, wp_a[:, NP - LANES:], a_cols=(0, Dh))
    _, b_lands = _split_wait("allgather_w_in_b_wait", b_sems, [], b_lands, proj_a, _ag_copies, 4)
    (g_in_b,) = _ag_finish("allgather_w_in_b_finish", b_lands)
    wp_b = laid_out(g_in_b)
    add = lambda r, e: r + e.astype(F32)
    (proj,) = _mm("proj_b", h, wp_b, a_cols=(Dh, Dh), extra=proj_a, outs=((BF16, add),))
    (dt_raw,) = _mm("proj_dt_b", h, wp_b[:, NP - LANES:], a_cols=(Dh, Dh), extra=dt_a, outs=((F32, add),))

    cwx = min(1024, XBC)

    def f_ssd_pre(first, last, xbc, halo, w, b):
        pre = _conv_pre(xbc, jnp.where(first, 0.0, halo), w, b)
        return (pre * jax.nn.sigmoid(pre),), ()
    (xbc_act,) = _rows_call(
        "ssd_pre", f_ssd_pre, nt, [proj, proj, conv_s, ssd_conv_b],
        [_rt(tr, XBC), _halo_prev(tr, XBC, rows=PACKED_ROWS), _full(conv_s), _full(ssd_conv_b)],
        [((T, XBC), F32, _rt(tr, XBC))], [], ['t', 'p0', 'f', 'f'], cw=cwx)

    def f_ssd_dt(first, last, dtr, dtb):
        return (jax.nn.softplus(dtr + dtb),), ()
    (dt,) = _rows_call("ssd_dt", f_ssd_dt, nt, [dt_raw, dt_bias], [_rt(tr, LANES), _full(dt_bias)],
                       [((T, LANES), F32, _rt(tr, LANES))], [], 'tf')

    y_ssd, h_prev = _ssd_fwd(xbc_act, dt, a_log, d_skip, n_heads, Q)

    gw = DS // SSD_GROUPS

    def ssd_post(y, z, g):
        yz = y * jax.nn.silu(z)
        parts = []
        for k in range(SSD_GROUPS):
            yk = yz[:, k * gw:(k + 1) * gw]
            parts.append(yk * lax.rsqrt(jnp.mean(yk * yk, axis=-1, keepdims=True) + EPS))
        return jnp.concatenate(parts, axis=-1) * g

    def f_ssd_post(first, last, y, z, g):
        return (ssd_post(y, z, g),), ()
    (mixcat,) = _rows_call("ssd_post", f_ssd_post, nt, [y_ssd, proj, ssd_norm],
                           [_rt(tr, DS), _rt(tr, DS, cb_z), _full(ssd_norm)], [((T, DS + DL), BF16, _rt(tr, DS))], [],
                           'ttf')

    def f_lru_pre(first, last, xv, halo, w, b):
        return (_conv_pre(xv, jnp.where(first, 0.0, halo), w, b),), ()
    (xl,) = _rows_call("lru_pre", f_lru_pre, nt, [proj, proj, conv_l, lru_conv_b],
                       [_rt(tr, DL, cb_xl), _halo_prev(tr, DL, cb_xl, rows=PACKED_ROWS), _full(conv_l),
                        _full(lru_conv_b)],
                       [((T, DL), F32, _rt(tr, DL))], [], ['t', 'p0', 'f', 'f'])

    a_lru, u_lru = _lru_gates_fwd(xl, wa_b, b_a, wx_b, b_x, lru_lambda, tr)
    _, ag_lands = _split_wait("allgather_later_wait", ag_sems, ag_srcs, ag_lands, u_lru, _ag_copies, 4 * len(later))
    f_sems, _, ag_lands, f_token = _split_start("allgather_later_fwd_start", [], ag_lands, _fwd_copies, 3 * len(later))
    h_lru = _lru_scan_fwd(a_lru, u_lru, tr, deps=[f_token])

    def lru_post(hv, gate, g):
        return _rms(hv * jax.nn.gelu(gate), g)

    def f_lru_post(first, last, hv, gate, g):
        return (lru_post(hv, gate, g),), ()
    cb_l = DS // DL
    (mixcat,) = _rows_call("lru_post", f_lru_post, nt, [h_lru, proj, lru_norm],
                           [_rt(tr, DL), _rt(tr, DL, cb_gate), _full(lru_norm)],
                           [((T, DS + DL), BF16, _rt(tr, DL, cb_l))], [], 'ttf', into=mixcat)

    _, (g_out, g_mi, g_mo) = _split_wait("allgather_later_fwd_wait", f_sems, [], ag_lands, mixcat, _fwd_copies,
                                         3 * len(later))
    w_out_f = g_out.reshape(DS + DL, D)
    w_mi_f = jnp.transpose(g_mi, (1, 0, 2)).reshape(D, DFF)
    w_mo_f = g_mo.reshape(DFF, D)
    (mix,) = _mm("mix", mixcat, w_out_f, outs=((BF16, None),))

    def f_post_mix(first, last, xv, mx, gpm, gpl):
        x1 = xv + _rms(mx, gpm)
        return (x1, _rms(x1, gpl)), ()
    x1, hn = _rows_call("post_mix", f_post_mix, nt, [x2, mix, post_mix_norm, pre_mlp_norm],
                        [_rt(tr, D), _rt(tr, D), _full(post_mix_norm), _full(pre_mlp_norm)],
                        [((T, D), F32, _rt(tr, D)), ((T, D), BF16, _rt(tr, D))], [], 'ttff')

    hm, act = _mm("mlp_in", hn, w_mi_f,
                  outs=((BF16, None), (BF16, lambda r, e: jnp.square(jnp.maximum(r, 0.0)))))
    (hm2,) = _mm("mlp_out", act, w_mo_f, outs=((BF16, None),))

    def f_final(first, last, x1v, hm2v, g, tg):
        err = x1v + _rms(hm2v, g) - tg
        dx2 = err * (1.0 / D)
        dh, dg = _rms_bwd(hm2v, g, dx2)
        loss = jnp.full((1, LANES), 0.5 / D, F32) * jnp.sum(err * err)
        return (dx2, dh), (dg, loss)
    dx1a, dhm2, g_post_mlp, loss_part = _rows_call(
        "loss_head", f_final, nt, [x1, hm2, post_mlp_norm, tgt],
        [_rt(tr, D), _rt(tr, D), _full(post_mlp_norm), _rt(tr, D)],
        [((T, D), F32, _rt(tr, D)), ((T, D), BF16, _rt(tr, D))], [(1, D), (1, LANES)], 'ttft')

    def rs_begin(n, full):
        (from_sib,) = _rs_sibling("rs_sibling_" + n, [full])
        pair = _pair_sum("pair_sum_" + n, full, from_sib, c_idx)
        sems, srcs, lands, token = _split_start("rs_start_" + n, [pair], [lax.empty((3,) + pair.shape[1:], BF16)],
                                                _rs_copies, 3)
        return (sems, srcs, lands), token

    def rs_end(n, state, after):
        (pair,), (recv,) = _split_wait("rs_wait_" + n, *state, after, _rs_copies, 3)
        return pair, recv, chip_idx

    def rs_direct_begin(n, full):
        sems, srcs, lands, token = _split_start("rs_start_" + n, [full],
                                                [lax.empty((N_DEV - 1,) + full.shape[1:], BF16)], _direct_copies,
                                                N_DEV - 1)
        return (sems, srcs, lands), token

    def rs_direct_end(n, state, after):
        (full,), (recv,) = _split_wait("rs_wait_" + n, *state, after, _direct_copies, N_DEV - 1)
        return full, recv, dev_idx

    (gw_mo,) = _mm("dw_mlp_out", act, dhm2, ta=True, outs=((BF16, None),))
    rs_mo, tok = rs_direct_begin('w_mlp_out', gw_mo.reshape(N_DEV, DFF // N_DEV, D))
    (dhm,) = _mm("d_mlp_act", dhm2, w_mo_f, tb=True, extra=hm,
                 outs=((BF16, lambda r, e: r * (2.0 * jnp.maximum(e.astype(F32), 0.0))),), deps=[tok])
    (gw_mi,) = _mm("dw_mlp_in", hn, dhm, ta=True, outs=((BF16, None),), out_blocks=N_DEV)
    rs_mi, tok = rs_direct_begin('w_mlp_in', gw_mi)
    (dhn,) = _mm("d_mlp_in", dhm, w_mi_f, tb=True, outs=((BF16, None),), deps=[tok])

    def f_post_mix_bwd(first, last, x1v, mx, gpm, gpl, dhnv, dxa):
        dx1, dgpl = _rms_bwd(x1v, gpl, dhnv)
        dx1 = dx1 + dxa
        dmx, dgpm = _rms_bwd(mx, gpm, dx1)
        return (dx1, dmx), (dgpl, dgpm)
    dx1, dmix, g_pre_mlp, g_post_mix = _rows_call(
        "post_mix_bwd", f_post_mix_bwd, nt, [x1, mix, post_mix_norm, pre_mlp_norm, dhn, dx1a],
        [_rt(tr, D), _rt(tr, D), _full(post_mix_norm), _full(pre_mlp_norm), _rt(tr, D), _rt(tr, D)],
        [((T, D), F32, _rt(tr, D)), ((T, D), BF16, _rt(tr, D))], [(1, D), (1, D)], 'ttfftt')

    (gw_out,) = _mm("dw_out", mixcat, dmix, ta=True, outs=((BF16, None),))
    rs_out, tok = rs_direct_begin('w_out', gw_out.reshape(N_DEV, -1, D))
    (dmixcat,) = _mm("d_mix", dmix, w_out_f, tb=True, outs=((BF16, None),), deps=[tok])

    def f_lru_post_bwd(first, last, hv, gate, g, dy):
        gl, dgl = _gelu_and_grad(gate)
        dv, dg = _rms_bwd(hv * gl, g, dy)
        return (dv * hv * dgl, dv * gl), (dg,)
    dproj, dh_lru, g_lru_norm = _rows_call(
        "lru_post_bwd", f_lru_post_bwd, nt, [h_lru, proj, lru_norm, dmixcat],
        [_rt(tr, DL), _rt(tr, DL, cb_gate), _full(lru_norm), _rt(tr, DL, cb_l)],
        [((T, NP), BF16, _rt(tr, DL, cb_gate)), ((T, DL), F32, _rt(tr, DL))], [(1, DL)], 'ttft')

    du_lru, da_lru = _lru_scan_bwd(a_lru, h_lru, dh_lru, tr)
    dxl, g_wa, g_ba, g_wx, g_bx, g_lam = _lru_gates_bwd(xl, wa_b, b_a, wx_b, b_x, lru_lambda, da_lru, du_lru, tr)

    conv_bwd_kinds = ['t', 'p0', 'n0', 't', 'n3', 'f', 'f']

    def f_lru_pre_bwd(first, last, xv, hp, xn, d, dn, w, b):
        dx, dw8, db = _conv_bwd_tile(first, last, xv, hp, xn, d, dn, w, b, silu=False)
        return (dx,), (dw8, db)
    dproj, g_convl8, g_convl_b = _rows_call(
        "lru_pre_bwd", f_lru_pre_bwd, nt, [proj, proj, proj, dxl, dxl, conv_l, lru_conv_b],
        [_rt(tr, DL, cb_xl), _halo_prev(tr, DL, cb_xl, rows=PACKED_ROWS),
         _halo_next(tr, DL, nt, cb_xl, rows=PACKED_ROWS), _rt(tr, DL),
         _halo_next(tr, DL, nt), _full(conv_l), _full(lru_conv_b)],
        [((T, NP), BF16, _rt(tr, DL, cb_xl))], [(SUBLANES, DL), (1, DL)], conv_bwd_kinds, into=dproj)

    def f_ssd_post_bwd(first, last, y, z, g, dy):
        s = jax.nn.sigmoid(z)
        sz = z * s
        yz = y * sz
        gdy = dy * g
        dyz, yn = [], []
        for k in range(SSD_GROUPS):
            cols = slice(k * gw, (k + 1) * gw)
            r = lax.rsqrt(jnp.mean(yz[:, cols] * yz[:, cols], axis=-1, keepdims=True) + EPS)
            xr = yz[:, cols] * r
            dyz.append((gdy[:, cols] - xr * jnp.mean(gdy[:, cols] * xr, axis=-1, keepdims=True)) * r)
            yn.append(xr)
        dyz = jnp.concatenate(dyz, axis=-1)
        dz = dyz * y * (s + sz * (1.0 - s))
        return (dz, dyz * sz), (_colsum(dy * jnp.concatenate(yn, axis=-1)),)
    dproj, dy_ssd, g_ssd_norm = _rows_call(
        "ssd_post_bwd", f_ssd_post_bwd, nt, [y_ssd, proj, ssd_norm, dmixcat],
        [_rt(tr, DS), _rt(tr, DS, cb_z), _full(ssd_norm), _rt(tr, DS, 0)],
        [((T, NP), BF16, _rt(tr, DS, cb_z)), ((T, DS), F32, _rt(tr, DS))], [(1, DS)], 'ttft', into=dproj)

    dxbc_act, ddt, g_alog, g_dskip = _ssd_bwd(xbc_act, dt, a_log, d_skip, h_prev, dy_ssd, n_heads, Q)

    def f_ssd_pre_bwd(first, last, xv, hp, xn, d, dn, w, b):
        dx, dw8, db = _conv_bwd_tile(first, last, xv, hp, xn, d, dn, w, b, silu=True)
        return (dx,), (dw8, db)
    dproj, g_convs8, g_convs_b = _rows_call(
        "ssd_pre_bwd", f_ssd_pre_bwd, nt, [proj, proj, proj, dxbc_act, dxbc_act, conv_s, ssd_conv_b],
        [_rt(tr, XBC), _halo_prev(tr, XBC, rows=PACKED_ROWS), _halo_next(tr, XBC, nt, rows=PACKED_ROWS),
         _rt(tr, XBC), _halo_next(tr, XBC, nt),
         _full(conv_s), _full(ssd_conv_b)],
        [((T, NP), BF16, _rt(tr, XBC))], [(SUBLANES, XBC), (1, XBC)], conv_bwd_kinds, into=dproj, cw=cwx)

    def f_ssd_dt_bwd(first, last, ddtv, dtr, dtb):
        ddtr = ddtv * jax.nn.sigmoid(dtr + dtb)
        return (ddtr,), (_colsum(ddtr),)
    dproj, g_dtb = _rows_call(
        "ssd_dt_bwd", f_ssd_dt_bwd, nt, [ddt, dt_raw, dt_bias],
        [_rt(tr, LANES), _rt(tr, LANES), _full(dt_bias)],
        [((T, NP), BF16, _rt(tr, LANES, cb_dt))], [(1, LANES)], 'ttf', into=dproj)
    small = {
        'ssd_conv_w': g_convs8[:CONV_WIDTH], 'ssd_conv_b': g_convs_b,
        'ssd_dt_bias': g_dtb[:, :n_heads], 'ssd_a_log': g_alog[:, :n_heads], 'ssd_d': g_dskip[:, :n_heads],
        'ssd_norm': g_ssd_norm, 'lru_conv_w': g_convl8[:CONV_WIDTH], 'lru_conv_b': g_convl_b,
        'lru_w_a': g_wa, 'lru_b_a': g_ba, 'lru_w_x': g_wx, 'lru_b_x': g_bx, 'lru_lambda': g_lam,
        'lru_norm': g_lru_norm, 'post_mix_norm': g_post_mix, 'pre_mlp_norm': g_pre_mlp,
        'post_mlp_norm': g_post_mlp, 'loss': loss_part[:, :1],
    }
    wide = ['lru_w_a', 'lru_w_x']
    narrow = [n for n in small if n not in wide]
    lb = lru_w_a.shape[-1]
    s_srcs = [_flat_rows([small[n] for n in narrow]), g_wa.reshape(-1, lb), g_wx.reshape(-1, lb)]
    s_sems, s_srcs, s_lands, tok = _split_start(
        "small_grads_start", [], [_own_block(a, dev) for a in s_srcs], _ag_copies, 4 * len(s_srcs))

    (gwp,) = _mm("dw_proj", h, dproj, ta=True, outs=((BF16, None),), deps=[tok])
    rs_in, tok = rs_begin(
        'w_in', jnp.stack([jnp.concatenate(my_cols(gwp, k * wb, (k + 1) * wb), axis=1) for k in range(N_DEV)]))
    (dh_a,) = _mm("d_proj_a", dproj, wp_a, tb=True, outs=((BF16, None),), deps=[tok])
    (dh_b,) = _mm("d_proj_b", dproj, wp_b, tb=True, outs=((BF16, None),), deps=[tok])

    def f_norm_in_bwd(first, last, xv, g, dha, dhb, dxa):
        dx, dg = _rms_bwd(xv, g, jnp.concatenate([dha, dhb], axis=1))
        return (dx + dxa,), (dg,)
    grad_x, g_pre_mix = _rows_call(
        "norm_in_bwd", f_norm_in_bwd, nt, [x2, pre_mix_norm, dh_a, dh_b, dx1],
        [_rt(tr, D), _full(pre_mix_norm), _rt(tr, Dh), _rt(tr, Dh), _rt(tr, D)], [((T, D), F32, _rt(tr, D))],
        [(1, D)], 'tfttt')

    big_out = {}
    for n, state in (('w_mlp_out', rs_mo), ('w_mlp_in', rs_mi), ('w_out', rs_out)):
        big_out[n] = _adamw_big("adamw_" + n, W[n], Mo[n], Vo[n], *rs_direct_end(n, state, grad_x))

    (g_pm8,) = _allgather("allgather_pre_mix_grad", [_own_block(g_pre_mix, dev)], deps=[big_out['w_out'][0]])
    _, s_lands = _split_wait("small_grads_wait", s_sems, s_srcs, s_lands, g_pm8, _ag_copies, 4 * len(s_lands))
    g_narrow, g_wa8, g_wx8 = _ag_finish("small_grads_finish", s_lands)
    summed = dict(zip(narrow, _unflat(_sum8("sum_small_grads", g_narrow), [small[n].shape for n in narrow])))
    summed['pre_mix_norm'] = _sum8("sum_pre_mix_grad", g_pm8)
    summed['lru_w_a'] = _sum8("sum_lru_w_a_grads", g_wa8)
    summed['lru_w_x'] = _sum8("sum_lru_w_x_grads", g_wx8)
    loss = summed.pop('loss').reshape(())
    for n, full_w in (('ssd_conv_w', XBC), ('lru_conv_w', DL)):
        wdt = full_w // N_DEV
        summed[n] = lax.dynamic_slice_in_dim(summed[n], dev * wdt, wdt, axis=1)
    small_params = [n for n in names if n not in big]
    as2d = lambda a: a.reshape(-1, a.shape[-1])
    res = _adamw_small([as2d(W[n]) for n in small_params],
                       [summed[n].reshape(as2d(W[n]).shape) for n in small_params],
                       [as2d(Mo[n]) for n in small_params], [as2d(Vo[n]) for n in small_params])
    grads = {n: summed[n].reshape(W[n].shape) for n in small_params}
    delta, new_m, new_v = ({n: r.reshape(W[n].shape) for n, r in zip(small_params, rs)} for rs in res)

    big_out['w_in'] = _adamw_big("adamw_w_in", W['w_in'], Mo['w_in'], Vo['w_in'], *rs_end('w_in', rs_in, g_pm8))
    for n in big:
        grads[n], delta[n], new_m[n], new_v[n] = big_out[n]

    return (loss, grad_x.reshape(x.shape), *[grads[n] for n in names], *[delta[n] for n in names],
            *[new_m[n] for n in names], *[new_v[n] for n in names])
```

```python
import functools

import jax
import jax.numpy as jnp
from jax import lax
from jax.experimental import pallas as pl
from jax.experimental.pallas import tpu as pltpu

F32, BF16 = jnp.float32, jnp.bfloat16
S = jax.ShapeDtypeStruct
MESH = pl.DeviceIdType.MESH

SSD_GROUPS = 8
LRU_C = 8.0
EPS = 1e-6
CONV_WIDTH = 4
ADAM_LR, ADAM_B1, ADAM_B2, ADAM_EPS, ADAM_WD, ADAM_STEP = 0.001, 0.9, 0.999, 1e-08, 0.01, 10

LANES = 128
SUBLANES = 8
VMEM_LIMIT = 56 * 1024 * 1024
N_DEV = 8
SMALL_W = 512
HI = lax.Precision.HIGHEST


def _pcall(body, **kw):
    return pl.pallas_call(body, **kw)


def _cparams(sem=None, **kw):
    return pltpu.CompilerParams(dimension_semantics=sem, vmem_limit_bytes=VMEM_LIMIT, **kw)


def _pick(n, cands):
    for c in cands:
        if c <= n and n % c == 0:
            return c
    return n


def _rt(tr, w, cb=0, n=None):
    if n is None:
        return pl.BlockSpec((tr, w), lambda i: (i, cb))
    return pl.BlockSpec((tr, w), lambda i: (n - 1 - i, cb))


PACKED_ROWS = 16


def _halo_prev(tr, w, cb=0, n=None, rows=SUBLANES):
    k = tr // rows
    if n is None:
        return pl.BlockSpec((rows, w), lambda i: (jnp.maximum(i * k - 1, 0), cb))
    return pl.BlockSpec((rows, w), lambda i: (jnp.maximum((n - 1 - i) * k - 1, 0), cb))


def _halo_next(tr, w, nt, cb=0, n=None, rows=SUBLANES):
    k = tr // rows
    last = nt * k - 1
    if n is None:
        return pl.BlockSpec((rows, w), lambda i: (jnp.minimum((i + 1) * k, last), cb))
    return pl.BlockSpec((rows, w), lambda i: (jnp.minimum((n - i) * k, last), cb))


def _full(a):
    nd = a.ndim
    return pl.BlockSpec(a.shape, lambda i: (0,) * nd)


def _rows_call(name, fn, n_tiles, arrays, in_specs, out_tiled, out_acc, kinds, into=None, deps=(), cw=None):
    n_in, n_t = len(arrays), len(out_tiled)
    n_skip = len(deps) + (0 if into is None else 1)
    width = in_specs[kinds.index('t')].block_shape[1]
    cols = [(0, width)] if cw is None else [(c, cw) for c in range(0, width, cw)]

    def body(*refs):
        i = pl.program_id(0)
        ins = refs[:n_in]
        outs = refs[n_in + n_skip:n_in + n_skip + n_t]
        accs = refs[n_in + n_skip + n_t:]
        if accs:
            @pl.when(i == 0)
            def _():
                for r in accs:
                    r[...] = jnp.zeros_like(r)

        def lanes(ref, rows, c0, w):
            return ref[rows, c0:c0 + w] if ref.shape[-1] == width else ref[rows, :]

        def load(k, c0, w):
            v = lanes(ins[k], slice(None), c0, w).astype(F32)
            if kinds[k][0] in 'pn' and v.shape[0] == PACKED_ROWS:
                v = v[SUBLANES:] if kinds[k][0] == 'p' else v[:SUBLANES]
            return v

        for c0, w in cols:
            touts, aouts = fn(i == 0, i == n_tiles - 1, *[load(k, c0, w) for k in range(n_in)])
            for r, v in zip(outs, touts):
                if r.shape[-1] == width:
                    r[:, c0:c0 + w] = v.astype(r.dtype)
                else:
                    r[...] = v.astype(r.dtype)
            for r, v in zip(accs, aouts):
                if r.shape[-1] == width:
                    r[:, c0:c0 + w] += v
                else:
                    r[...] += v

    out_shape = [S(sh, dt) for sh, dt, _ in out_tiled] + [S(sh, F32) for sh in out_acc]
    out_specs = [sp for _, _, sp in out_tiled]
    for sh in out_acc:
        out_specs.append(pl.BlockSpec(sh, lambda i, nd=len(sh): (0,) * nd))
    in_specs = list(in_specs) + [_ANY] * len(deps)
    if into is None:
        return _pcall(body, name=name, grid=(n_tiles,), in_specs=in_specs, out_specs=out_specs,
                      out_shape=out_shape, compiler_params=_cparams(("arbitrary",)))(*arrays, *deps)
    return _pcall(body, name=name, grid=(n_tiles,), in_specs=in_specs + [_ANY], out_specs=out_specs,
                  out_shape=out_shape, input_output_aliases={n_in + len(deps): 0},
                  compiler_params=_cparams(("arbitrary",)))(*arrays, *deps, into)


def _rms(x, g):
    return x * lax.rsqrt(jnp.mean(x * x, axis=-1, keepdims=True) + EPS) * g


def _rms_bwd(x, g, dy):
    r = lax.rsqrt(jnp.mean(x * x, axis=-1, keepdims=True) + EPS)
    xr = x * r
    gdy = dy * g
    dx = (gdy - xr * jnp.mean(gdy * xr, axis=-1, keepdims=True)) * r
    return dx, _colsum(dy * xr)


_GELU_C0, _GELU_C1 = 0.7978845608028654, 0.044715


def _gelu_and_grad(x):
    x2 = x * x
    t = jnp.tanh(_GELU_C0 * x * (1.0 + _GELU_C1 * x2))
    half = 0.5 * (1.0 + t)
    grad = half + (0.5 * _GELU_C0) * x * (1.0 - t * t) * (1.0 + (3.0 * _GELU_C1) * x2)
    return x * half, grad


def _colsum(v):
    return jnp.sum(v, axis=0, keepdims=True)


_TILES = (1152, 1024, 896, 768, 640, 512, 384, 256, 128)
_K_TILES = (4096, 3456, 3072, 2688, 2048, 1536, 1344, 1152, 1024, 896, 768, 640, 512, 384, 256, 128)


def _mm(name, a, b, *, ta=False, tb=False, outs=((F32, None),), extra=None, out_blocks=None, tm=None, tn=None, tk=None,
        deps=(), a_cols=None):
    M, K = (a.shape[1], a.shape[0]) if ta else a.shape
    if a_cols is not None:
        assert not ta
        K = a_cols[1]
    b3 = b.ndim == 3
    if b3:
        nb_b, brows, bcols = b.shape
        N = brows if tb else nb_b * bcols
    else:
        N = b.shape[0] if tb else b.shape[1]
    n_lim = N if out_blocks is None else N // out_blocks
    if b3 and not tb:
        n_lim = min(n_lim, bcols)
    tm = tm or _pick(M, _TILES[1:])
    tn = tn or _pick(n_lim, _TILES)
    tk = tk or _pick(bcols if (b3 and tb) else K, _K_TILES)
    nk = K // tk
    assert M % tm == 0 and N % tn == 0 and K % tk == 0
    dn = (((0 if ta else 1,), (1 if tb else 0,)), ((), ()))
    n_extra = 0 if extra is None else 1
    n_out = len(outs)

    def body(*refs):
        a_ref, b_ref = refs[0], refs[1]
        e_ref = refs[2] if n_extra else None
        o_refs = refs[2 + n_extra + len(deps):2 + n_extra + len(deps) + n_out]

        def finish(r):
            e = e_ref[...] if n_extra else None
            for o, (_, f) in zip(o_refs, outs):
                o[...] = (r if f is None else f(r, e)).astype(o.dtype)

        part = lax.dot_general(a_ref[...], b_ref[...], dn, preferred_element_type=F32)
        if nk == 1:
            finish(part)
            return
        acc = refs[-1]
        k = pl.program_id(2)

        @pl.when(k == 0)
        def _():
            acc[...] = part

        @pl.when(jnp.logical_and(k > 0, k < nk - 1))
        def _():
            acc[...] += part

        @pl.when(k == nk - 1)
        def _():
            finish(acc[...] + part)

    k0 = 0 if a_cols is None else a_cols[0] // tk
    a_spec = (pl.BlockSpec((tk, tm), lambda i, j, k: (k, i)) if ta
              else pl.BlockSpec((tm, tk), lambda i, j, k: (i, k + k0)))
    if not b3:
        b_spec = pl.BlockSpec((tn, tk), lambda i, j, k: (j, k)) if tb else pl.BlockSpec((tk, tn), lambda i, j, k: (k, j))
    elif tb:
        per = bcols // tk
        b_spec = pl.BlockSpec((None, tn, tk), lambda i, j, k: (k // per, j, k % per))
    else:
        per = bcols // tn
        b_spec = pl.BlockSpec((None, tk, tn), lambda i, j, k: (j // per, k, j % per))
    o_spec = pl.BlockSpec((tm, tn), lambda i, j, k: (i, j))
    if out_blocks is None:
        out_specs, out_shape = [o_spec] * n_out, [S((M, N), dt) for dt, _ in outs]
    else:
        per_o = N // out_blocks // tn
        ob_spec = pl.BlockSpec((None, tm, tn), lambda i, j, k: (j // per_o, i, j % per_o))
        out_specs, out_shape = [ob_spec] * n_out, [S((out_blocks, M, N // out_blocks), dt) for dt, _ in outs]
    in_specs = [a_spec, b_spec] + ([o_spec] if n_extra else []) + [_ANY] * len(deps)
    args = [a, b] + ([extra] if n_extra else []) + list(deps)
    return _pcall(body, name=name, grid=(M // tm, N // tn, nk), in_specs=in_specs, out_specs=out_specs,
                  out_shape=out_shape, scratch_shapes=[pltpu.VMEM((tm, tn), F32)] if nk > 1 else [],
                  compiler_params=_cparams(("parallel", "parallel", "arbitrary")))(*args)


def _shift_down(x, halo, s):
    if s == 0:
        return x
    r = pltpu.roll(x, s, 0)
    hr = pltpu.roll(halo, s, 0)
    row = lax.broadcasted_iota(jnp.int32, halo.shape, 0)
    top = jnp.where(row < s, hr, r[:SUBLANES])
    if x.shape[0] == SUBLANES:
        return top
    return jnp.concatenate([top, r[SUBLANES:]], axis=0)


def _shift_up(x, nxt, s):
    if s == 0:
        return x
    n = x.shape[0]
    r = pltpu.roll(x, n - s, 0)
    nr = pltpu.roll(nxt, SUBLANES - s, 0)
    row = lax.broadcasted_iota(jnp.int32, nxt.shape, 0)
    bot = jnp.where(row >= SUBLANES - s, nr, r[n - SUBLANES:])
    if n == SUBLANES:
        return bot
    return jnp.concatenate([r[:n - SUBLANES], bot], axis=0)


def _conv_pre(x, halo, w, b):
    acc = b + w[CONV_WIDTH - 1:CONV_WIDTH, :] * x
    for k in range(CONV_WIDTH - 1):
        acc = acc + w[k:k + 1, :] * _shift_down(x, halo, CONV_WIDTH - 1 - k)
    return acc


def _silu_grad(p):
    s = jax.nn.sigmoid(p)
    return s * (1.0 + p * (1.0 - s))


def _conv_bwd_tile(first, last, x, hprev, xnext, d, dnext, w, b, silu):
    hprev = jnp.where(first, 0.0, hprev)
    if silu:
        d = d * _silu_grad(_conv_pre(x, hprev, w, b))
        pre_next = _conv_pre(xnext, x[x.shape[0] - SUBLANES:], w, b)
        dnext = dnext * _silu_grad(pre_next)
    dnext = jnp.where(last, 0.0, dnext)
    dx = w[CONV_WIDTH - 1:CONV_WIDTH, :] * d
    row8 = lax.broadcasted_iota(jnp.int32, (SUBLANES, x.shape[1]), 0)
    dw8 = jnp.where(row8 == CONV_WIDTH - 1, _colsum(d * x), 0.0)
    for k in range(CONV_WIDTH - 1):
        s = CONV_WIDTH - 1 - k
        dx = dx + w[k:k + 1, :] * _shift_up(d, dnext, s)
        dw8 = dw8 + jnp.where(row8 == k, _colsum(d * _shift_down(x, hprev, s)), 0.0)
    return dx, dw8, _colsum(d)


def _ssd_dims(xbc_act, n_heads):
    T, XBC = xbc_act.shape
    GN = XBC // 4
    DS = XBC - 2 * GN
    G = SSD_GROUPS
    N = GN // G
    P = DS // n_heads
    K = n_heads // G
    return T, XBC, DS, GN, G, N, P, K


def _ssd_common(dt, alog, Q):
    a = -jnp.exp(alog)
    adt = dt * a
    li = lax.broadcasted_iota(jnp.int32, (Q, Q), 0)
    si = lax.broadcasted_iota(jnp.int32, (Q, Q), 1)
    causal = li >= si
    ltri = causal.astype(F32)
    acs = jnp.dot(ltri, adt, precision=HI, preferred_element_type=F32)
    acs_row = lax.dot_general(adt, ltri, (((0,), (1,)), ((), ())), precision=HI,
                              preferred_element_type=F32)
    return a, adt, causal, ltri, acs, acs_row


def _expander(g, K, P, W):
    r = lax.broadcasted_iota(jnp.int32, (LANES, W), 0)
    c = lax.broadcasted_iota(jnp.int32, (LANES, W), 1)
    return (c // P + g * K == r).astype(F32)


def _dotb(a, b, dn=(((1,), (0,)), ((), ()))):
    return lax.dot_general(a.astype(BF16), b.astype(BF16), dn, preferred_element_type=F32)


def _dot_split(a, sel, terms, dn=(((1,), (0,)), ((), ()))):
    selb = sel.astype(BF16)
    out = None
    for _ in range(terms):
        piece = a.astype(BF16)
        part = lax.dot_general(piece, selb, dn, preferred_element_type=F32)
        out = part if out is None else out + part
        a = a - piece.astype(F32)
    return out


_NT = (((1,), (1,)), ((), ()))
_TN = (((0,), (0,)), ((), ()))


def _ssd_fwd(xbc_act, dt, alog, dskip, n_heads, Q):
    T, XBC, DS, GN, G, N, P, K = _ssd_dims(xbc_act, n_heads)
    W = K * P
    nc = T // Q

    def body(xs_ref, b_ref, c_ref, dt_ref, alog_ref, d_ref, y_ref, hp_ref, h_scr):
        ci = pl.program_id(0)

        @pl.when(ci == 0)
        def _():
            h_scr[...] = jnp.zeros_like(h_scr)

        dtv = dt_ref[...]
        a, adt, causal, ltri, acs, acs_row = _ssd_common(dtv, alog_ref[...], Q)
        lane_head = lax.broadcasted_iota(jnp.int32, (Q, W), 1) // P
        for g in range(G):
            eg = _expander(g, K, P, W)
            dtb = _dot_split(dtv, eg, 3)
            acsb = _dot_split(acs, eg, 3)
            lastb = acsb[Q - 1:Q, :]
            db = _dot_split(jnp.broadcast_to(d_ref[...], (SUBLANES, LANES)), eg, 3)[0:1, :]
            xg = xs_ref[:, g * W:(g + 1) * W]
            bg = b_ref[:, g * N:(g + 1) * N]
            cg = c_ref[:, g * N:(g + 1) * N]
            xt = xg * dtb
            sc = _dotb(cg, bg, _NT)
            yd = jnp.zeros((Q, W), F32)
            for k in range(K):
                h = g * K + k
                seg = acs[:, h:h + 1] - acs_row[h:h + 1, :]
                lh = jnp.where(causal, jnp.exp(jnp.minimum(seg, 0.0)), 0.0)
                xk = jnp.where(lane_head == k, xt, 0.0)
                yd = yd + _dotb(sc * lh, xk)
            hp = h_scr[g]
            yoff = _dotb(cg, hp) * jnp.exp(acsb)
            y_ref[:, g * W:(g + 1) * W] = yd + yoff + xg * db
            e_end = jnp.exp(lastb - acsb)
            st = _dotb(bg, xt * e_end, _TN)
            hp_ref[0, g] = hp
            h_scr[g] = jnp.exp(lastb) * hp + st

    cb = DS // GN
    in_specs = [pl.BlockSpec((Q, DS), lambda c: (c, 0)),
                pl.BlockSpec((Q, GN), lambda c: (c, cb)),
                pl.BlockSpec((Q, GN), lambda c: (c, cb + 1)),
                pl.BlockSpec((Q, LANES), lambda c: (c, 0)),
                pl.BlockSpec((1, LANES), lambda c: (0, 0)),
                pl.BlockSpec((1, LANES), lambda c: (0, 0))]
    out_specs = [pl.BlockSpec((Q, DS), lambda c: (c, 0)),
                 pl.BlockSpec((1, G, N, W), lambda c: (c, 0, 0, 0))]
    return _pcall(body, name="ssd_fwd", grid=(nc,), in_specs=in_specs, out_specs=out_specs,
                  out_shape=[S((T, DS), F32), S((nc, G, N, W), F32)],
                  scratch_shapes=[pltpu.VMEM((G, N, W), F32)],
                  compiler_params=_cparams(("arbitrary",)))(xbc_act, xbc_act, xbc_act, dt, alog, dskip)


def _ssd_bwd(xbc_act, dt, alog, dskip, hprev, dy, n_heads, Q):
    T, XBC, DS, GN, G, N, P, K = _ssd_dims(xbc_act, n_heads)
    W = K * P
    nc = T // Q

    def body(xs_ref, b_ref, c_ref, dt_ref, alog_ref, d_ref, hp_ref, dy_ref,
             dxbc_ref, ddt_ref, dalog_ref, dd_ref, dh_scr):
        ci = pl.program_id(0)

        @pl.when(ci == 0)
        def _():
            dh_scr[...] = jnp.zeros_like(dh_scr)
            dalog_ref[...] = jnp.zeros_like(dalog_ref)
            dd_ref[...] = jnp.zeros_like(dd_ref)

        dtv = dt_ref[...]
        a, adt, causal, ltri, acs, acs_row = _ssd_common(dtv, alog_ref[...], Q)
        lane_head = lax.broadcasted_iota(jnp.int32, (Q, W), 1) // P
        lane128 = lax.broadcasted_iota(jnp.int32, (Q, LANES), 1)
        sub128 = lax.broadcasted_iota(jnp.int32, (LANES, Q), 0)
        rowq = lax.broadcasted_iota(jnp.int32, (Q, W), 0)
        dacs = jnp.zeros((Q, LANES), F32)
        dacs_row = jnp.zeros((LANES, Q), F32)
        ddt = jnp.zeros((Q, LANES), F32)
        dd_acc = jnp.zeros((1, LANES), F32)
        for g in range(G):
            eg = _expander(g, K, P, W)
            dtb = _dot_split(dtv, eg, 3)
            acsb = _dot_split(acs, eg, 3)
            lastb = acsb[Q - 1:Q, :]
            db = _dot_split(jnp.broadcast_to(d_ref[...], (SUBLANES, LANES)), eg, 3)[0:1, :]
            xg = xs_ref[:, g * W:(g + 1) * W]
            bg = b_ref[:, g * N:(g + 1) * N]
            cg = c_ref[:, g * N:(g + 1) * N]
            dyg = dy_ref[:, g * W:(g + 1) * W].astype(F32)
            hp = hp_ref[0, g]
            dhn = dh_scr[g]
            xt = xg * dtb
            sc = _dotb(cg, bg, _NT)
            eacs = jnp.exp(acsb)
            e_end = jnp.exp(lastb - acsb)
            elast = jnp.exp(lastb)

            wv = dyg * eacs
            dcg = _dotb(wv, hp, _NT)
            dhp = _dotb(cg, wv, _TN) + elast * dhn
            dacsb = dyg * (_dotb(cg, hp) * eacs)

            xe = xt * e_end
            dbg = _dotb(xe, dhn, _NT)
            v = _dotb(bg, dhn)
            dxt = v * e_end
            de = v * xe
            dacsb = dacsb - de
            dlastb = _colsum(de) + elast * jnp.sum(dhn * hp, axis=0, keepdims=True)

            dsc = jnp.zeros((Q, Q), F32)
            for k in range(K):
                h = g * K + k
                seg = acs[:, h:h + 1] - acs_row[h:h + 1, :]
                lh = jnp.where(causal, jnp.exp(jnp.minimum(seg, 0.0)), 0.0)
                mh = sc * lh
                dyk = jnp.where(lane_head == k, dyg, 0.0)
                dxt = dxt + jnp.where(lane_head == k, _dotb(mh, dyg, _TN), 0.0)
                dm = _dotb(dyk, xt, _NT)
                dsc = dsc + dm * lh
                gm = dm * mh
                dacs = dacs + jnp.where(lane128 == h, jnp.sum(gm, axis=1, keepdims=True), 0.0)
                dacs_row = dacs_row - jnp.where(sub128 == h, jnp.sum(gm, axis=0, keepdims=True), 0.0)
            dcg = dcg + _dotb(dsc, bg)
            dbg = dbg + _dotb(dsc, cg, _TN)

            dacsb = dacsb + jnp.where(rowq == Q - 1, dlastb, 0.0)
            dacs = dacs + _dot_split(dacsb, eg, 2, _NT)
            ddt = ddt + _dot_split(dxt * xg, eg, 2, _NT)
            dd_acc = dd_acc + _dot_split(jnp.broadcast_to(_colsum(dyg * xg), (SUBLANES, W)), eg, 2, _NT)[0:1, :]
            dxbc_ref[:, g * W:(g + 1) * W] = dxt * dtb + dyg * db
            dxbc_ref[:, DS + g * N:DS + (g + 1) * N] = dbg
            dxbc_ref[:, DS + GN + g * N:DS + GN + (g + 1) * N] = dcg
            dh_scr[g] = dhp

        eye = (lax.broadcasted_iota(jnp.int32, (LANES, LANES), 0) ==
               lax.broadcasted_iota(jnp.int32, (LANES, LANES), 1)).astype(F32)
        dacs = dacs + lax.dot_general(dacs_row, eye, _TN, precision=HI, preferred_element_type=F32)
        dadt = lax.dot_general(ltri, dacs, _TN, precision=HI, preferred_element_type=F32)
        ddt_ref[...] = ddt + dadt * a
        dalog_ref[...] += _colsum(dadt * dtv) * a
        dd_ref[...] += dd_acc

    cb = DS // GN
    rv = lambda c: nc - 1 - c
    in_specs = [pl.BlockSpec((Q, DS), lambda c: (rv(c), 0)),
                pl.BlockSpec((Q, GN), lambda c: (rv(c), cb)),
                pl.BlockSpec((Q, GN), lambda c: (rv(c), cb + 1)),
                pl.BlockSpec((Q, LANES), lambda c: (rv(c), 0)),
                pl.BlockSpec((1, LANES), lambda c: (0, 0)),
                pl.BlockSpec((1, LANES), lambda c: (0, 0)),
                pl.BlockSpec((1, G, N, W), lambda c: (rv(c), 0, 0, 0)),
                pl.BlockSpec((Q, DS), lambda c: (rv(c), 0))]
    out_specs = [pl.BlockSpec((Q, XBC), lambda c: (rv(c), 0)),
                 pl.BlockSpec((Q, LANES), lambda c: (rv(c), 0)),
                 pl.BlockSpec((1, LANES), lambda c: (0, 0)),
                 pl.BlockSpec((1, LANES), lambda c: (0, 0))]
    return _pcall(body, name="ssd_bwd", grid=(nc,), in_specs=in_specs, out_specs=out_specs,
                  out_shape=[S((T, XBC), F32), S((T, LANES), F32), S((1, LANES), F32), S((1, LANES), F32)],
                  scratch_shapes=[pltpu.VMEM((G, N, W), F32)],
                  compiler_params=_cparams(("arbitrary",)))(
                      xbc_act, xbc_act, xbc_act, dt, alog, dskip, hprev, dy)


def _blockdiag(x, w_ref, dn=(((1,), (0,)), ((), ()))):
    H, B, _ = w_ref.shape
    return jnp.concatenate([_dotb(x[:, h * B:(h + 1) * B], w_ref[h], dn) for h in range(H)], axis=1)


def _lru_elem(xl, r_pre, i_pre, lam):
    r = jax.nn.sigmoid(r_pre)
    i = jax.nn.sigmoid(i_pre)
    log_a = -LRU_C * r * jax.nn.softplus(-lam)
    a = jnp.exp(log_a)
    u = jnp.sqrt(1.0 - jnp.exp(2.0 * log_a)) * (i * xl)
    return a, u


def _lru_elem_bwd(xl, r_pre, i_pre, lam, da, du):
    r = jax.nn.sigmoid(r_pre)
    i = jax.nn.sigmoid(i_pre)
    sp = jax.nn.softplus(-lam)
    a = jnp.exp(-LRU_C * r * sp)
    s = jnp.sqrt(1.0 - a * a)
    d_ix = du * s
    dlog_a = (da - du * (i * xl) * a / s) * a
    dr_pre = dlog_a * (-LRU_C) * sp * r * (1.0 - r)
    dlam = _colsum(dlog_a * r) * (LRU_C * jax.nn.sigmoid(-lam))
    di_pre = d_ix * xl * i * (1.0 - i)
    return d_ix * i, dr_pre, di_pre, dlam


def _lru_gates_fwd(xl, w_a, b_a, w_x, b_x, lam, tr):
    T, DL = xl.shape

    def body(xl_ref, wa_ref, ba_ref, wx_ref, bx_ref, lam_ref, a_ref, u_ref):
        x = xl_ref[...]
        r_pre = _blockdiag(x, wa_ref) + ba_ref[...]
        i_pre = _blockdiag(x, wx_ref) + bx_ref[...]
        a, u = _lru_elem(x, r_pre, i_pre, lam_ref[...])
        a_ref[...] = a
        u_ref[...] = u

    w3 = pl.BlockSpec(w_a.shape, lambda i: (0, 0, 0))
    vec = pl.BlockSpec((1, DL), lambda i: (0, 0))
    return _pcall(body, name="lru_gates_fwd", grid=(T // tr,),
                  in_specs=[_rt(tr, DL), w3, vec, w3, vec, vec],
                  out_specs=[_rt(tr, DL), _rt(tr, DL)], out_shape=[S((T, DL), F32), S((T, DL), F32)],
                  compiler_params=_cparams(("parallel",)))(xl, w_a, b_a, w_x, b_x, lam)


def _lru_gates_bwd(xl, w_a, b_a, w_x, b_x, lam, da, du, tr):
    T, DL = xl.shape
    H, B, _ = w_a.shape

    def body(xl_ref, wa_ref, ba_ref, wx_ref, bx_ref, lam_ref, da_ref, du_ref,
             dxl_ref, dwa_ref, dba_ref, dwx_ref, dbx_ref, dlam_ref):
        @pl.when(pl.program_id(0) == 0)
        def _():
            for r in (dwa_ref, dba_ref, dwx_ref, dbx_ref, dlam_ref):
                r[...] = jnp.zeros_like(r)

        x = xl_ref[...]
        r_pre = _blockdiag(x, wa_ref) + ba_ref[...]
        i_pre = _blockdiag(x, wx_ref) + bx_ref[...]
        dx, dr, di, dlam = _lru_elem_bwd(x, r_pre, i_pre, lam_ref[...], da_ref[...], du_ref[...])
        dxl_ref[...] = (dx + _blockdiag(dr, wa_ref, _NT) + _blockdiag(di, wx_ref, _NT)).astype(dxl_ref.dtype)
        for h in range(H):
            xh = x[:, h * B:(h + 1) * B]
            dwa_ref[h] += _dotb(xh, dr[:, h * B:(h + 1) * B], _TN)
            dwx_ref[h] += _dotb(xh, di[:, h * B:(h + 1) * B], _TN)
        dba_ref[...] += _colsum(dr)
        dbx_ref[...] += _colsum(di)
        dlam_ref[...] += dlam

    w3 = pl.BlockSpec(w_a.shape, lambda i: (0, 0, 0))
    vec = pl.BlockSpec((1, DL), lambda i: (0, 0))
    return _pcall(body, name="lru_gates_bwd", grid=(T // tr,),
                  in_specs=[_rt(tr, DL), w3, vec, w3, vec, vec, _rt(tr, DL), _rt(tr, DL)],
                  out_specs=[_rt(tr, DL), w3, vec, w3, vec, vec],
                  out_shape=[S((T, DL), BF16), S(w_a.shape, F32), S((1, DL), F32), S(w_a.shape, F32),
                             S((1, DL), F32), S((1, DL), F32)],
                  compiler_params=_cparams(("arbitrary",)))(xl, w_a, b_a, w_x, b_x, lam, da, du)


def _groups(v):
    return v.reshape(v.shape[0] // SUBLANES, SUBLANES, v.shape[1])


def _rows_shifted(v, edge, up):
    sub = lax.broadcasted_iota(jnp.int32, v.shape, 1)
    if up:
        other = jnp.concatenate([v[1:], edge[None]], axis=0)
        return jnp.where(sub < SUBLANES - 1, pltpu.roll(v, SUBLANES - 1, 1), pltpu.roll(other, SUBLANES - 1, 1))
    other = jnp.concatenate([edge[None], v[:-1]], axis=0)
    return jnp.where(sub >= 1, pltpu.roll(v, 1, 1), pltpu.roll(other, 1, 1))


def _scan_tile(a, u, entering, emit, up):
    G = a.shape[0]
    sub = lax.broadcasted_iota(jnp.int32, a.shape, 1)
    d = 1
    while d < SUBLANES:
        if up:
            keep = sub < SUBLANES - d
            a_s = jnp.where(keep, pltpu.roll(a, SUBLANES - d, 1), 1.0)
            u_s = jnp.where(keep, pltpu.roll(u, SUBLANES - d, 1), 0.0)
        else:
            keep = sub >= d
            a_s = jnp.where(keep, pltpu.roll(a, d, 1), 1.0)
            u_s = jnp.where(keep, pltpu.roll(u, d, 1), 0.0)
        u = a * u_s + u
        a = a * a_s
        d *= 2
    for g in (reversed(range(G)) if up else range(G)):
        hg = u[g] + a[g] * entering
        emit(g, hg)
        entering = hg[0:1] if up else hg[SUBLANES - 1:SUBLANES]
    return entering


def _lru_scan_fwd(a, u, tr, deps=()):
    T, DL = a.shape

    def body(a_ref, u_ref, *rest):
        h_ref, carry = rest[len(deps):]

        @pl.when(pl.program_id(0) == 0)
        def _():
            carry[...] = jnp.zeros_like(carry)

        def emit(g, hg):
            h_ref[g * SUBLANES:(g + 1) * SUBLANES, :] = hg

        last = _scan_tile(_groups(a_ref[...]), _groups(u_ref[...]), carry[0:1, :], emit, up=False)
        carry[...] = jnp.broadcast_to(last, carry.shape)

    return _pcall(body, name="lru_scan_fwd", grid=(T // tr,),
                  in_specs=[_rt(tr, DL), _rt(tr, DL)] + [_ANY] * len(deps),
                  out_specs=_rt(tr, DL), out_shape=S((T, DL), F32),
                  scratch_shapes=[pltpu.VMEM((SUBLANES, DL), F32)],
                  compiler_params=_cparams(("arbitrary",)))(a, u, *deps)


def _lru_scan_bwd(a, h, dh, tr):
    T, DL = a.shape
    n = T // tr

    def body(a_ref, an_ref, h_ref, hp_ref, dh_ref, du_ref, da_ref, carry):
        i = pl.program_id(0)
        ti = n - 1 - i

        @pl.when(i == 0)
        def _():
            carry[...] = jnp.zeros_like(carry)

        a_next = _rows_shifted(_groups(a_ref[...]), jnp.where(ti == n - 1, 0.0, an_ref[...]), up=True)
        h_prev = _rows_shifted(_groups(h_ref[...]), jnp.where(ti == 0, 0.0, hp_ref[...]), up=False)

        def emit(g, gg):
            du_ref[g * SUBLANES:(g + 1) * SUBLANES, :] = gg
            da_ref[g * SUBLANES:(g + 1) * SUBLANES, :] = gg * h_prev[g]

        top = _scan_tile(a_next, _groups(dh_ref[...].astype(F32)), carry[0:1, :], emit, up=True)
        carry[...] = jnp.broadcast_to(top, carry.shape)

    return _pcall(body, name="lru_scan_bwd", grid=(n,),
                  in_specs=[_rt(tr, DL, 0, n), _halo_next(tr, DL, n, 0, n), _rt(tr, DL, 0, n),
                            _halo_prev(tr, DL, 0, n), _rt(tr, DL, 0, n)],
                  out_specs=[_rt(tr, DL, 0, n), _rt(tr, DL, 0, n)],
                  out_shape=[S((T, DL), F32), S((T, DL), F32)],
                  scratch_shapes=[pltpu.VMEM((SUBLANES, DL), F32)],
                  compiler_params=_cparams(("arbitrary",)))(a, a, h, h, dh)


def _adamw(w, g, m, v):
    m = ADAM_B1 * m + (1.0 - ADAM_B1) * g
    v = ADAM_B2 * v + (1.0 - ADAM_B2) * (g * g)
    m_hat = m / (1.0 - ADAM_B1 ** ADAM_STEP)
    v_hat = v / (1.0 - ADAM_B2 ** ADAM_STEP)
    delta = -ADAM_LR * (m_hat / (jnp.sqrt(v_hat) + ADAM_EPS) + ADAM_WD * w)
    return delta, m, v


def _adamw_big(name, w, m, v, own, recv, own_idx):
    _, R, C = w.shape
    n_recv = recv.shape[0]
    tr = _pick(R, (256, 128, 64, 32, 16))

    def body(idx_ref, w_ref, m_ref, v_ref, p_ref, *rest):
        g = p_ref[...].astype(F32)
        for r in rest[:n_recv]:
            g = g + r[...].astype(F32)
        g_ref, d_ref, nm_ref, nv_ref = rest[n_recv:]
        d, nm, nv = _adamw(w_ref[...], g, m_ref[...], v_ref[...])
        g_ref[...] = g
        d_ref[...] = d
        nm_ref[...] = nm
        nv_ref[...] = nv

    r_spec = lambda s: pl.BlockSpec((None, tr, C), lambda i, idx: (s, i, 0))
    t2 = r_spec(0)
    gs = pltpu.PrefetchScalarGridSpec(
        num_scalar_prefetch=1, grid=(R // tr,),
        in_specs=[t2, t2, t2, pl.BlockSpec((None, tr, C), lambda i, idx: (idx[0], i, 0))]
        + [r_spec(s) for s in range(n_recv)],
        out_specs=[t2, t2, t2, t2])
    return _pcall(body, name=name, grid_spec=gs, out_shape=[S((1, R, C), F32)] * 4,
                  compiler_params=_cparams(("parallel",)))(own_idx, w, m, v, own, *([recv] * n_recv))


def _adamw_small(ws, gs, ms, vs):
    n = len(ws)

    def body(*refs):
        for k in range(n):
            d, nm, nv = _adamw(refs[k][...], refs[n + k][...], refs[2 * n + k][...], refs[3 * n + k][...])
            refs[4 * n + k][...] = d
            refs[5 * n + k][...] = nm
            refs[6 * n + k][...] = nv

    res = _pcall(body, name="adamw_small", out_shape=[S(w.shape, F32) for w in ws] * 3,
                 compiler_params=_cparams())(*ws, *gs, *ms, *vs)
    return res[:n], res[n:2 * n], res[2 * n:]


def _sum8(name, parts):
    _, R, C = parts.shape

    def body(p_ref, o_ref):
        acc = p_ref[0]
        for k in range(1, N_DEV):
            acc = acc + p_ref[k]
        o_ref[...] = acc

    return _pcall(body, name=name, out_shape=S((R, C), F32), compiler_params=_cparams())(parts)


def _pair_sum(name, full, recv, c_idx):
    _, R, C = full.shape
    tr = _pick(R, (256, 128, 64, 32, 16))

    def body(c_ref, f_ref, r_ref, o_ref):
        o_ref[...] = (f_ref[...].astype(F32) + r_ref[...].astype(F32)).astype(o_ref.dtype)

    gs = pltpu.PrefetchScalarGridSpec(
        num_scalar_prefetch=1, grid=(4, R // tr),
        in_specs=[pl.BlockSpec((None, tr, C), lambda j, i, c: (2 * j + c[0], i, 0)),
                  pl.BlockSpec((None, tr, C), lambda j, i, c: (j, i, 0))],
        out_specs=pl.BlockSpec((None, tr, C), lambda j, i, c: (j, i, 0)))
    return _pcall(body, name=name, grid_spec=gs, out_shape=S((4, R, C), BF16),
                  compiler_params=_cparams(("parallel", "parallel")))(c_idx, full, recv)


def _cast_bf16(name, w, dev_idx, row0=0, rows=None, deps=()):
    C = w.shape[2]
    R = w.shape[1] if rows is None else rows
    tr = _pick(R, (256, 128, 64, 32, 16))
    b0 = row0 // tr

    def body(d_ref, w_ref, *rest):
        rest[-1][...] = w_ref[...].astype(BF16)

    gs = pltpu.PrefetchScalarGridSpec(
        num_scalar_prefetch=1, grid=(R // tr,),
        in_specs=[pl.BlockSpec((None, tr, C), lambda i, d: (0, i + b0, 0))] + [_ANY] * len(deps),
        out_specs=pl.BlockSpec((None, tr, C), lambda i, d: (d[0], i, 0)))
    return _pcall(body, name=name, grid_spec=gs, out_shape=S((N_DEV, R, C), BF16),
                  compiler_params=_cparams(("parallel",)))(dev_idx, w, *deps)


def _own_block(v, dev):
    return lax.dynamic_update_slice(lax.empty((N_DEV,) + v.shape, v.dtype), v[None], (dev,) + (0,) * v.ndim)


_ANY = pl.BlockSpec(memory_space=pl.ANY)


def _position():
    return lax.axis_index("x"), lax.axis_index("y"), lax.axis_index("c")


def _allgather(name, bufs, deps=()):
    n = len(bufs)
    nd = len(deps)

    def body(*refs):
        outs = refs[n + nd:2 * n + nd]
        send, recv = refs[2 * n + nd:]
        x, y, c = _position()
        me, sib = (x, y, c), (x, y, 1 - c)
        chips = [(1 - x, y), (x, 1 - y), (1 - x, 1 - y)]

        def copy(a, k, block, to):
            bx, by, bc = block
            blk = outs[a].at[4 * bx + 2 * by + bc]
            return pltpu.make_async_remote_copy(
                src_ref=blk, dst_ref=blk, send_sem=send.at[a, k], recv_sem=recv.at[a, k],
                device_id=to, device_id_type=MESH)

        first = []
        for a in range(n):
            first.append(copy(a, 0, me, sib))
            first += [copy(a, 1 + j, me, (*chip, c)) for j, chip in enumerate(chips)]
        for cp in first:
            cp.start()
        passed = []
        for j, chip in enumerate(chips):
            for a in range(n):
                copy(a, 1 + j, (*chip, c), me).wait_recv()
                cp = copy(a, 4 + j, (*chip, c), sib)
                cp.start()
                passed.append(cp)
        for a in range(n):
            copy(a, 0, sib, me).wait_recv()
        for j, chip in enumerate(chips):
            for a in range(n):
                copy(a, 4 + j, (*chip, 1 - c), me).wait_recv()
        for cp in first + passed:
            cp.wait_send()

    return _pcall(body, name=name, in_specs=[_ANY] * (n + nd), out_specs=[_ANY] * n,
                  out_shape=[S(b.shape, b.dtype) for b in bufs], input_output_aliases={a: a for a in range(n)},
                  scratch_shapes=[pltpu.SemaphoreType.DMA((n, 7)), pltpu.SemaphoreType.DMA((n, 7))])(*bufs, *deps)


def _rs_sibling(name, fulls):
    n = len(fulls)

    def body(*refs):
        ins, outs = refs[:n], refs[n:2 * n]
        send, recv = refs[2 * n:]
        x, y, c = _position()
        copies = []
        for a in range(n):
            for j in range(4):
                copies.append(pltpu.make_async_remote_copy(
                    src_ref=ins[a].at[2 * j + (1 - c)], dst_ref=outs[a].at[j], send_sem=send.at[a, j],
                    recv_sem=recv.at[a, j], device_id=(x, y, 1 - c), device_id_type=MESH))
        for cp in copies:
            cp.start()
        for cp in copies:
            cp.wait()

    return _pcall(body, name=name, in_specs=[_ANY] * n, out_specs=[_ANY] * n,
                  out_shape=[S((4,) + f.shape[1:], f.dtype) for f in fulls],
                  scratch_shapes=[pltpu.SemaphoreType.DMA((n, 4)), pltpu.SemaphoreType.DMA((n, 4))])(*fulls)


_HBM = pl.BlockSpec(memory_space=pltpu.HBM)
_SEM = pl.BlockSpec(memory_space=pltpu.SEMAPHORE)
_EFFECT = pltpu.SideEffectType.DATAFLOW_SIDE_EFFECTING


def _remote_copies(copies_fn, srcs, lands, send, recv):
    x, y, c = _position()
    return [pltpu.make_async_remote_copy(src_ref=s, dst_ref=d, send_sem=send[i], recv_sem=recv[i], device_id=to,
                                         device_id_type=MESH)
            for i, (s, d, to) in enumerate(copies_fn(x, y, c, srcs, lands))]


def _split_start(name, srcs, lands, copies_fn, nc, after=()):
    n, nl, na = len(srcs), len(lands), len(after)

    def body(*refs):
        src_refs, land_refs = refs[:n], refs[n:n + nl]
        outs = refs[n + nl + na:]
        for cp in _remote_copies(copies_fn, src_refs, land_refs, outs[:nc], outs[nc:2 * nc]):
            cp.start()
        outs[-1][...] = jnp.zeros_like(outs[-1])

    hbm = lambda a: pltpu.with_memory_space_constraint(a, pltpu.HBM)
    res = _pcall(
        body, name=name, in_specs=[_HBM] * (n + nl) + [_ANY] * na,
        out_specs=[_SEM] * (2 * nc) + [_HBM] * (n + nl) + [pl.BlockSpec(memory_space=pltpu.VMEM)],
        out_shape=[pltpu.SemaphoreType.DMA(())] * (2 * nc) + [pltpu.HBM(s.shape, s.dtype) for s in srcs]
        + [pltpu.HBM(l.shape, l.dtype) for l in lands] + [S((SUBLANES, LANES), F32)],
        input_output_aliases={i: 2 * nc + i for i in range(n + nl)},
        compiler_params=pltpu.CompilerParams(has_side_effects=_EFFECT),
    )(*[hbm(s) for s in srcs], *[hbm(l) for l in lands], *after)
    return res[:2 * nc], res[2 * nc:2 * nc + n], res[2 * nc + n:2 * nc + n + nl], res[-1]


def _split_wait(name, sems, srcs, lands, after, copies_fn, nc):
    n, nl = len(srcs), len(lands)

    def body(*refs):
        src_refs, land_refs = refs[:n], refs[n:n + nl]
        sem_refs = refs[n + nl:n + nl + 2 * nc]
        for cp in _remote_copies(copies_fn, src_refs, land_refs, sem_refs[:nc], sem_refs[nc:]):
            cp.wait_send()
            cp.wait_recv()

    res = _pcall(
        body, name=name, in_specs=[_HBM] * (n + nl) + [_SEM] * (2 * nc) + [_ANY],
        out_specs=[_HBM] * (n + nl), out_shape=[pltpu.HBM(a.shape, a.dtype) for a in list(srcs) + list(lands)],
        input_output_aliases={i: i for i in range(n + nl)},
        compiler_params=pltpu.CompilerParams(has_side_effects=_EFFECT),
    )(*srcs, *lands, *sems, after)
    return res[:n], res[n:]


def _other_chips(x, y):
    return [(1 - x, y), (x, 1 - y), (1 - x, 1 - y)]


def _ag_copies(x, y, c, srcs, lands):
    out = []
    for land in lands:
        blk = land.at[4 * x + 2 * y + c]
        out.append((blk, blk, (x, y, 1 - c)))
        out += [(blk, blk, (px, py, c)) for px, py in _other_chips(x, y)]
    return out


def _rs_copies(x, y, c, srcs, lands):
    return [(s.at[2 * px + py], land.at[j], (px, py, c))
            for s, land in zip(srcs, lands) for j, (px, py) in enumerate(_other_chips(x, y))]


def _fwd_copies(x, y, c, srcs, lands):
    out = []
    for land in lands:
        for px, py in _other_chips(x, y):
            blk = land.at[4 * px + 2 * py + c]
            out.append((blk, blk, (x, y, 1 - c)))
    return out


def _direct_copies(x, y, c, srcs, lands):
    out = []
    for s, land in zip(srcs, lands):
        for r in range(1, N_DEV):
            px = 1 - x if r & 4 else x
            py = 1 - y if r & 2 else y
            pc = 1 - c if r & 1 else c
            out.append((s.at[4 * px + 2 * py + pc], land.at[r - 1], (px, py, pc)))
    return out


def _ag_finish(name, lands):
    n = len(lands)

    def body(*refs):
        outs = refs[n:2 * n]
        send, recv = refs[2 * n:]
        x, y, c = _position()

        def swap(a, j, px, py, pc):
            blk = outs[a].at[4 * px + 2 * py + pc]
            return pltpu.make_async_remote_copy(src_ref=blk, dst_ref=blk, send_sem=send.at[a, j], recv_sem=recv.at[a, j],
                                                device_id=(x, y, 1 - c), device_id_type=MESH)

        chips = _other_chips(x, y)
        sends = [swap(a, j, px, py, c) for a in range(n) for j, (px, py) in enumerate(chips)]
        for cp in sends:
            cp.start()
        for a in range(n):
            for j, (px, py) in enumerate(chips):
                swap(a, j, px, py, 1 - c).wait_recv()
        for cp in sends:
            cp.wait_send()

    return _pcall(body, name=name, in_specs=[_ANY] * n, out_specs=[_ANY] * n,
                  out_shape=[S(l.shape, l.dtype) for l in lands], input_output_aliases={a: a for a in range(n)},
                  scratch_shapes=[pltpu.SemaphoreType.DMA((n, 3)), pltpu.SemaphoreType.DMA((n, 3))])(*lands)


def _pad_lanes(v):
    return jnp.pad(v, ((0, 0), (0, LANES - v.shape[1])))


def _flat_rows(pieces):
    flat = jnp.concatenate([p.reshape(-1) for p in pieces])
    rows = -(-flat.shape[0] // (SMALL_W * SUBLANES)) * SUBLANES
    return jnp.pad(flat, (0, rows * SMALL_W - flat.shape[0])).reshape(rows, SMALL_W)


def _unflat(buf, shapes):
    flat = buf.reshape(-1)
    out, off = [], 0
    for sh in shapes:
        n = 1
        for d in sh:
            n *= d
        out.append(flat[off:off + n].reshape(sh))
        off += n
    return out


def kernel(x, pre_mix_norm, w_in, ssd_conv_w, ssd_conv_b, ssd_dt_bias, ssd_a_log, ssd_d, ssd_norm, lru_conv_w, lru_conv_b, lru_w_a, lru_b_a, lru_w_x, lru_b_x, lru_lambda, lru_norm, w_out, post_mix_norm, pre_mlp_norm, w_mlp_in, w_mlp_out, post_mlp_norm, loss_target, m_pre_mix_norm, m_w_in, m_ssd_conv_w, m_ssd_conv_b, m_ssd_dt_bias, m_ssd_a_log, m_ssd_d, m_ssd_norm, m_lru_conv_w, m_lru_conv_b, m_lru_w_a, m_lru_b_a, m_lru_w_x, m_lru_b_x, m_lru_lambda, m_lru_norm, m_w_out, m_post_mix_norm, m_pre_mlp_norm, m_w_mlp_in, m_w_mlp_out, m_post_mlp_norm, v_pre_mix_norm, v_w_in, v_ssd_conv_w, v_ssd_conv_b, v_ssd_dt_bias, v_ssd_a_log, v_ssd_d, v_ssd_norm, v_lru_conv_w, v_lru_conv_b, v_lru_w_a, v_lru_b_a, v_lru_w_x, v_lru_b_x, v_lru_lambda, v_lru_norm, v_w_out, v_post_mix_norm, v_pre_mlp_norm, v_w_mlp_in, v_w_mlp_out, v_post_mlp_norm):
    names = ['pre_mix_norm', 'w_in', 'ssd_conv_w', 'ssd_conv_b', 'ssd_dt_bias', 'ssd_a_log', 'ssd_d', 'ssd_norm',
             'lru_conv_w', 'lru_conv_b', 'lru_w_a', 'lru_b_a', 'lru_w_x', 'lru_b_x', 'lru_lambda', 'lru_norm',
             'w_out', 'post_mix_norm', 'pre_mlp_norm', 'w_mlp_in', 'w_mlp_out', 'post_mlp_norm']
    loc = locals()
    W = {n: loc[n] for n in names}
    Mo = {n: loc["m_" + n] for n in names}
    Vo = {n: loc["v_" + n] for n in names}
    big = ['w_in', 'w_out', 'w_mlp_in', 'w_mlp_out']

    px, py, pc = _position()
    dev = 4 * px + 2 * py + pc
    dev_idx = jnp.reshape(dev, (1,)).astype(jnp.int32)
    c_idx = jnp.reshape(pc, (1,)).astype(jnp.int32)
    chip_idx = jnp.reshape(2 * px + py, (1,)).astype(jnp.int32)

    _, T, D = x.shape
    x2 = x.reshape(T, D)
    tgt = loss_target.reshape(T, D)
    n_heads = ssd_dt_bias.shape[1]
    XBC = ssd_conv_b.shape[1]
    GN = XBC // 4
    DS = XBC - 2 * GN
    DL = lru_norm.shape[1]
    DFF = w_mlp_in.shape[2] * N_DEV
    DIN = w_in.shape[2] * N_DEV
    NP = XBC + DS + 2 * DL + LANES
    assert DS % GN == 0 and XBC % DS == 0 and DS == DL and n_heads <= LANES
    cb_z, cb_gate, cb_xl, cb_dt = XBC // DS, XBC // DS + 1, XBC // DS + 2, (XBC + DS + 2 * DL) // LANES
    tr = min(256, T // 2)
    nt = T // tr
    Q = min(256, T // 2)

    Dh = D // 2
    sh_a = _cast_bf16("cast_w_in_a", W['w_in'], dev_idx, 0, Dh)
    sh_b = _cast_bf16("cast_w_in_b", W['w_in'], dev_idx, Dh, Dh)
    later = big[1:]
    a_bufs = [sh_a, _own_block(ssd_conv_w[0], dev), _own_block(lru_conv_w[0], dev)]
    a_sems, _, a_lands, a_token = _split_start("allgather_w_in_a_start", [], a_bufs, _ag_copies, 4 * len(a_bufs))
    sh = {n: _cast_bf16("cast_" + n, W[n], dev_idx, deps=[a_token]) for n in later}

    def f_norm_in(first, last, xv, g):
        return (_rms(xv, g),), ()
    (h,) = _rows_call("norm_in", f_norm_in, nt, [x2, pre_mix_norm], [_rt(tr, D), _full(pre_mix_norm)],
                      [((T, D), BF16, _rt(tr, D))], [], 'tf', deps=[a_token])
    _, a_lands = _split_wait("allgather_w_in_a_wait", a_sems, [], a_lands, h, _ag_copies, 4 * len(a_bufs))
    g_in_a, g_cs, g_cl = _ag_finish("allgather_w_in_a_finish", a_lands)
    b_sems, _, b_lands, b_token = _split_start("allgather_w_in_b_start", [], [sh_b], _ag_copies, 4, after=[g_in_a])
    ag_sems, ag_srcs, ag_lands, ag_token = _split_start(
        "allgather_later_start", [], [sh[n] for n in later], _ag_copies, 4 * len(later), after=[b_token])
    conv_s = jnp.transpose(g_cs, (1, 0, 2)).reshape(CONV_WIDTH, XBC)
    conv_l = jnp.transpose(g_cl, (1, 0, 2)).reshape(CONV_WIDTH, DL)
    wb = DIN // N_DEV
    o_z, o_xbc, o_dt, o_gate, o_xl = 0, DS, DS + XBC, DS + XBC + n_heads, DS + XBC + n_heads + DL
    segs = [(o_xbc, o_xbc + XBC, 0), (o_z, o_z + DS, XBC), (o_gate, o_gate + DL, XBC + DS),
            (o_xl, o_xl + DL, XBC + DS + DL), (o_dt, o_dt + n_heads, NP - LANES)]

    def ref_cols(g, lo, hi):
        out = []
        while lo < hi:
            k = lo // wb
            e = min(hi, (k + 1) * wb)
            out.append(g[k, :, lo - k * wb:e - k * wb])
            lo = e
        return out

    def laid_out(g):
        return jnp.concatenate([p for a, b, _ in segs for p in ref_cols(g, a, b)]
                               + [jnp.zeros((g.shape[1], LANES - n_heads), BF16)], axis=1)

    def my_cols(g, lo, hi):
        out = []
        for a, b, m in sorted(segs):
            s, e = max(lo, a), min(hi, b)
            if s < e:
                out.append(g[:, m + s - a:m + e - a])
        return out

    wp_a = laid_out(g_in_a)
    dt_bias = _pad_lanes(ssd_dt_bias)
    a_log = _pad_lanes(ssd_a_log)
    d_skip = _pad_lanes(ssd_d)
    wa_b, wx_b = lru_w_a[0].astype(BF16), lru_w_x[0].astype(BF16)
    b_a, b_x = lru_b_a.reshape(1, DL), lru_b_x.reshape(1, DL)

    (proj_a,) = _mm("proj_a", h, wp_a, a_cols=(0, Dh), outs=((BF16, None),), deps=[ag_token])
    (dt_a,) = _mm("proj_dt_a", h, wp_a[:, NP - LANES:], a_cols=(0, Dh))
    _, b_lands = _split_wait("allgather_w_in_b_wait", b_sems, [], b_lands, proj_a, _ag_copies, 4)
    (g_in_b,) = _ag_finish("allgather_w_in_b_finish", b_lands)
    wp_b = laid_out(g_in_b)
    add = lambda r, e: r + e.astype(F32)
    (proj,) = _mm("proj_b", h, wp_b, a_cols=(Dh, Dh), extra=proj_a, outs=((BF16, add),))
    (dt_raw,) = _mm("proj_dt_b", h, wp_b[:, NP - LANES:], a_cols=(Dh, Dh), extra=dt_a, outs=((F32, add),))

    cwx = min(1024, XBC)

    def f_ssd_pre(first, last, xbc, halo, w, b):
        pre = _conv_pre(xbc, jnp.where(first, 0.0, halo), w, b)
        return (pre * jax.nn.sigmoid(pre),), ()
    (xbc_act,) = _rows_call(
        "ssd_pre", f_ssd_pre, nt, [proj, proj, conv_s, ssd_conv_b],
        [_rt(tr, XBC), _halo_prev(tr, XBC, rows=PACKED_ROWS), _full(conv_s), _full(ssd_conv_b)],
        [((T, XBC), F32, _rt(tr, XBC))], [], ['t', 'p0', 'f', 'f'], cw=cwx)

    def f_ssd_dt(first, last, dtr, dtb):
        return (jax.nn.softplus(dtr + dtb),), ()
    (dt,) = _rows_call("ssd_dt", f_ssd_dt, nt, [dt_raw, dt_bias], [_rt(tr, LANES), _full(dt_bias)],
                       [((T, LANES), F32, _rt(tr, LANES))], [], 'tf')

    y_ssd, h_prev = _ssd_fwd(xbc_act, dt, a_log, d_skip, n_heads, Q)

    gw = DS // SSD_GROUPS

    def ssd_post(y, z, g):
        yz = y * jax.nn.silu(z)
        parts = []
        for k in range(SSD_GROUPS):
            yk = yz[:, k * gw:(k + 1) * gw]
            parts.append(yk * lax.rsqrt(jnp.mean(yk * yk, axis=-1, keepdims=True) + EPS))
        return jnp.concatenate(parts, axis=-1) * g

    def f_ssd_post(first, last, y, z, g):
        return (ssd_post(y, z, g),), ()
    (mixcat,) = _rows_call("ssd_post", f_ssd_post, nt, [y_ssd, proj, ssd_norm],
                           [_rt(tr, DS), _rt(tr, DS, cb_z), _full(ssd_norm)], [((T, DS + DL), BF16, _rt(tr, DS))], [],
                           'ttf')

    def f_lru_pre(first, last, xv, halo, w, b):
        return (_conv_pre(xv, jnp.where(first, 0.0, halo), w, b),), ()
    (xl,) = _rows_call("lru_pre", f_lru_pre, nt, [proj, proj, conv_l, lru_conv_b],
                       [_rt(tr, DL, cb_xl), _halo_prev(tr, DL, cb_xl, rows=PACKED_ROWS), _full(conv_l),
                        _full(lru_conv_b)],
                       [((T, DL), F32, _rt(tr, DL))], [], ['t', 'p0', 'f', 'f'])

    a_lru, u_lru = _lru_gates_fwd(xl, wa_b, b_a, wx_b, b_x, lru_lambda, tr)
    _, ag_lands = _split_wait("allgather_later_wait", ag_sems, ag_srcs, ag_lands, u_lru, _ag_copies, 4 * len(later))
    f_sems, _, ag_lands, f_token = _split_start("allgather_later_fwd_start", [], ag_lands, _fwd_copies, 3 * len(later))
    h_lru = _lru_scan_fwd(a_lru, u_lru, tr, deps=[f_token])

    def lru_post(hv, gate, g):
        return _rms(hv * jax.nn.gelu(gate), g)

    def f_lru_post(first, last, hv, gate, g):
        return (lru_post(hv, gate, g),), ()
    cb_l = DS // DL
    (mixcat,) = _rows_call("lru_post", f_lru_post, nt, [h_lru, proj, lru_norm],
                           [_rt(tr, DL), _rt(tr, DL, cb_gate), _full(lru_norm)],
                           [((T, DS + DL), BF16, _rt(tr, DL, cb_l))], [], 'ttf', into=mixcat)

    _, (g_out, g_mi, g_mo) = _split_wait("allgather_later_fwd_wait", f_sems, [], ag_lands, mixcat, _fwd_copies,
                                         3 * len(later))
    w_out_f = g_out.reshape(DS + DL, D)
    w_mi_f = jnp.transpose(g_mi, (1, 0, 2)).reshape(D, DFF)
    w_mo_f = g_mo.reshape(DFF, D)
    (mix,) = _mm("mix", mixcat, w_out_f, outs=((BF16, None),))

    def f_post_mix(first, last, xv, mx, gpm, gpl):
        x1 = xv + _rms(mx, gpm)
        return (x1, _rms(x1, gpl)), ()
    x1, hn = _rows_call("post_mix", f_post_mix, nt, [x2, mix, post_mix_norm, pre_mlp_norm],
                        [_rt(tr, D), _rt(tr, D), _full(post_mix_norm), _full(pre_mlp_norm)],
                        [((T, D), F32, _rt(tr, D)), ((T, D), BF16, _rt(tr, D))], [], 'ttff')

    hm, act = _mm("mlp_in", hn, w_mi_f,
                  outs=((BF16, None), (BF16, lambda r, e: jnp.square(jnp.maximum(r, 0.0)))))
    (hm2,) = _mm("mlp_out", act, w_mo_f, outs=((BF16, None),))

    def f_final(first, last, x1v, hm2v, g, tg):
        err = x1v + _rms(hm2v, g) - tg
        dx2 = err * (1.0 / D)
        dh, dg = _rms_bwd(hm2v, g, dx2)
        loss = jnp.full((1, LANES), 0.5 / D, F32) * jnp.sum(err * err)
        return (dx2, dh), (dg, loss)
    dx1a, dhm2, g_post_mlp, loss_part = _rows_call(
        "loss_head", f_final, nt, [x1, hm2, post_mlp_norm, tgt],
        [_rt(tr, D), _rt(tr, D), _full(post_mlp_norm), _rt(tr, D)],
        [((T, D), BF16, _rt(tr, D)), ((T, D), BF16, _rt(tr, D))], [(1, D), (1, LANES)], 'ttft')

    def rs_begin(n, full):
        (from_sib,) = _rs_sibling("rs_sibling_" + n, [full])
        pair = _pair_sum("pair_sum_" + n, full, from_sib, c_idx)
        sems, srcs, lands, token = _split_start("rs_start_" + n, [pair], [lax.empty((3,) + pair.shape[1:], BF16)],
                                                _rs_copies, 3)
        return (sems, srcs, lands), token

    def rs_end(n, state, after):
        (pair,), (recv,) = _split_wait("rs_wait_" + n, *state, after, _rs_copies, 3)
        return pair, recv, chip_idx

    def rs_direct_begin(n, full):
        sems, srcs, lands, token = _split_start("rs_start_" + n, [full],
                                                [lax.empty((N_DEV - 1,) + full.shape[1:], BF16)], _direct_copies,
                                                N_DEV - 1)
        return (sems, srcs, lands), token

    def rs_direct_end(n, state, after):
        (full,), (recv,) = _split_wait("rs_wait_" + n, *state, after, _direct_copies, N_DEV - 1)
        return full, recv, dev_idx

    (gw_mo,) = _mm("dw_mlp_out", act, dhm2, ta=True, outs=((BF16, None),))
    rs_mo, tok = rs_direct_begin('w_mlp_out', gw_mo.reshape(N_DEV, DFF // N_DEV, D))
    (dhm,) = _mm("d_mlp_act", dhm2, w_mo_f, tb=True, extra=hm,
                 outs=((BF16, lambda r, e: r * (2.0 * jnp.maximum(e.astype(F32), 0.0))),), deps=[tok])
    (gw_mi,) = _mm("dw_mlp_in", hn, dhm, ta=True, outs=((BF16, None),), out_blocks=N_DEV)
    rs_mi, tok = rs_direct_begin('w_mlp_in', gw_mi)
    (dhn,) = _mm("d_mlp_in", dhm, w_mi_f, tb=True, outs=((BF16, None),), deps=[tok])

    def f_post_mix_bwd(first, last, x1v, mx, gpm, gpl, dhnv, dxa):
        dx1, dgpl = _rms_bwd(x1v, gpl, dhnv)
        dx1 = dx1 + dxa
        dmx, dgpm = _rms_bwd(mx, gpm, dx1)
        return (dx1, dmx), (dgpl, dgpm)
    dx1, dmix, g_pre_mlp, g_post_mix = _rows_call(
        "post_mix_bwd", f_post_mix_bwd, nt, [x1, mix, post_mix_norm, pre_mlp_norm, dhn, dx1a],
        [_rt(tr, D), _rt(tr, D), _full(post_mix_norm), _full(pre_mlp_norm), _rt(tr, D), _rt(tr, D)],
        [((T, D), BF16, _rt(tr, D)), ((T, D), BF16, _rt(tr, D))], [(1, D), (1, D)], 'ttfftt')

    (gw_out,) = _mm("dw_out", mixcat, dmix, ta=True, outs=((BF16, None),))
    rs_out, tok = rs_direct_begin('w_out', gw_out.reshape(N_DEV, -1, D))
    (dmixcat,) = _mm("d_mix", dmix, w_out_f, tb=True, outs=((BF16, None),), deps=[tok])

    def f_lru_post_bwd(first, last, hv, gate, g, dy):
        gl, dgl = _gelu_and_grad(gate)
        dv, dg = _rms_bwd(hv * gl, g, dy)
        return (dv * hv * dgl, dv * gl), (dg,)
    dproj, dh_lru, g_lru_norm = _rows_call(
        "lru_post_bwd", f_lru_post_bwd, nt, [h_lru, proj, lru_norm, dmixcat],
        [_rt(tr, DL), _rt(tr, DL, cb_gate), _full(lru_norm), _rt(tr, DL, cb_l)],
        [((T, NP), BF16, _rt(tr, DL, cb_gate)), ((T, DL), BF16, _rt(tr, DL))], [(1, DL)], 'ttft')

    du_lru, da_lru = _lru_scan_bwd(a_lru, h_lru, dh_lru, tr)
    dxl, g_wa, g_ba, g_wx, g_bx, g_lam = _lru_gates_bwd(xl, wa_b, b_a, wx_b, b_x, lru_lambda, da_lru, du_lru, tr)

    conv_bwd_kinds = ['t', 'p0', 'n0', 't', 'n3', 'f', 'f']

    def f_lru_pre_bwd(first, last, xv, hp, xn, d, dn, w, b):
        dx, dw8, db = _conv_bwd_tile(first, last, xv, hp, xn, d, dn, w, b, silu=False)
        return (dx,), (dw8, db)
    dproj, g_convl8, g_convl_b = _rows_call(
        "lru_pre_bwd", f_lru_pre_bwd, nt, [proj, proj, proj, dxl, dxl, conv_l, lru_conv_b],
        [_rt(tr, DL, cb_xl), _halo_prev(tr, DL, cb_xl, rows=PACKED_ROWS),
         _halo_next(tr, DL, nt, cb_xl, rows=PACKED_ROWS), _rt(tr, DL),
         _halo_next(tr, DL, nt, rows=PACKED_ROWS), _full(conv_l), _full(lru_conv_b)],
        [((T, NP), BF16, _rt(tr, DL, cb_xl))], [(SUBLANES, DL), (1, DL)], conv_bwd_kinds, into=dproj)

    def f_ssd_post_bwd(first, last, y, z, g, dy):
        s = jax.nn.sigmoid(z)
        sz = z * s
        yz = y * sz
        gdy = dy * g
        dyz, yn = [], []
        for k in range(SSD_GROUPS):
            cols = slice(k * gw, (k + 1) * gw)
            r = lax.rsqrt(jnp.mean(yz[:, cols] * yz[:, cols], axis=-1, keepdims=True) + EPS)
            xr = yz[:, cols] * r
            dyz.append((gdy[:, cols] - xr * jnp.mean(gdy[:, cols] * xr, axis=-1, keepdims=True)) * r)
            yn.append(xr)
        dyz = jnp.concatenate(dyz, axis=-1)
        dz = dyz * y * (s + sz * (1.0 - s))
        return (dz, dyz * sz), (_colsum(dy * jnp.concatenate(yn, axis=-1)),)
    dproj, dy_ssd, g_ssd_norm = _rows_call(
        "ssd_post_bwd", f_ssd_post_bwd, nt, [y_ssd, proj, ssd_norm, dmixcat],
        [_rt(tr, DS), _rt(tr, DS, cb_z), _full(ssd_norm), _rt(tr, DS, 0)],
        [((T, NP), BF16, _rt(tr, DS, cb_z)), ((T, DS), BF16, _rt(tr, DS))], [(1, DS)], 'ttft', into=dproj)

    dxbc_act, ddt, g_alog, g_dskip = _ssd_bwd(xbc_act, dt, a_log, d_skip, h_prev, dy_ssd, n_heads, Q)

    def f_ssd_pre_bwd(first, last, xv, hp, xn, d, dn, w, b):
        dx, dw8, db = _conv_bwd_tile(first, last, xv, hp, xn, d, dn, w, b, silu=True)
        return (dx,), (dw8, db)
    dproj, g_convs8, g_convs_b = _rows_call(
        "ssd_pre_bwd", f_ssd_pre_bwd, nt, [proj, proj, proj, dxbc_act, dxbc_act, conv_s, ssd_conv_b],
        [_rt(tr, XBC), _halo_prev(tr, XBC, rows=PACKED_ROWS), _halo_next(tr, XBC, nt, rows=PACKED_ROWS),
         _rt(tr, XBC), _halo_next(tr, XBC, nt),
         _full(conv_s), _full(ssd_conv_b)],
        [((T, NP), BF16, _rt(tr, XBC))], [(SUBLANES, XBC), (1, XBC)], conv_bwd_kinds, into=dproj, cw=cwx)

    def f_ssd_dt_bwd(first, last, ddtv, dtr, dtb):
        ddtr = ddtv * jax.nn.sigmoid(dtr + dtb)
        return (ddtr,), (_colsum(ddtr),)
    dproj, g_dtb = _rows_call(
        "ssd_dt_bwd", f_ssd_dt_bwd, nt, [ddt, dt_raw, dt_bias],
        [_rt(tr, LANES), _rt(tr, LANES), _full(dt_bias)],
        [((T, NP), BF16, _rt(tr, LANES, cb_dt))], [(1, LANES)], 'ttf', into=dproj)
    small = {
        'ssd_conv_w': g_convs8[:CONV_WIDTH], 'ssd_conv_b': g_convs_b,
        'ssd_dt_bias': g_dtb[:, :n_heads], 'ssd_a_log': g_alog[:, :n_heads], 'ssd_d': g_dskip[:, :n_heads],
        'ssd_norm': g_ssd_norm, 'lru_conv_w': g_convl8[:CONV_WIDTH], 'lru_conv_b': g_convl_b,
        'lru_w_a': g_wa, 'lru_b_a': g_ba, 'lru_w_x': g_wx, 'lru_b_x': g_bx, 'lru_lambda': g_lam,
        'lru_norm': g_lru_norm, 'post_mix_norm': g_post_mix, 'pre_mlp_norm': g_pre_mlp,
        'post_mlp_norm': g_post_mlp, 'loss': loss_part[:, :1],
    }
    wide = ['lru_w_a', 'lru_w_x']
    narrow = [n for n in small if n not in wide]
    lb = lru_w_a.shape[-1]
    s_srcs = [_flat_rows([small[n] for n in narrow]), g_wa.reshape(-1, lb), g_wx.reshape(-1, lb)]
    s_sems, s_srcs, s_lands, tok = _split_start(
        "small_grads_start", [], [_own_block(a, dev) for a in s_srcs], _ag_copies, 4 * len(s_srcs))

    (gwp,) = _mm("dw_proj", h, dproj, ta=True, outs=((BF16, None),), deps=[tok])
    rs_in, tok = rs_begin(
        'w_in', jnp.stack([jnp.concatenate(my_cols(gwp, k * wb, (k + 1) * wb), axis=1) for k in range(N_DEV)]))
    (dh_a,) = _mm("d_proj_a", dproj, wp_a, tb=True, outs=((BF16, None),), deps=[tok])
    (dh_b,) = _mm("d_proj_b", dproj, wp_b, tb=True, outs=((BF16, None),), deps=[tok])

    def f_norm_in_bwd(first, last, xv, g, dha, dhb, dxa):
        dx, dg = _rms_bwd(xv, g, jnp.concatenate([dha, dhb], axis=1))
        return (dx + dxa,), (dg,)
    grad_x, g_pre_mix = _rows_call(
        "norm_in_bwd", f_norm_in_bwd, nt, [x2, pre_mix_norm, dh_a, dh_b, dx1],
        [_rt(tr, D), _full(pre_mix_norm), _rt(tr, Dh), _rt(tr, Dh), _rt(tr, D)], [((T, D), F32, _rt(tr, D))],
        [(1, D)], 'tfttt')

    big_out = {}
    for n, state in (('w_mlp_out', rs_mo), ('w_mlp_in', rs_mi), ('w_out', rs_out)):
        big_out[n] = _adamw_big("adamw_" + n, W[n], Mo[n], Vo[n], *rs_direct_end(n, state, grad_x))

    (g_pm8,) = _allgather("allgather_pre_mix_grad", [_own_block(g_pre_mix, dev)], deps=[big_out['w_out'][0]])
    _, s_lands = _split_wait("small_grads_wait", s_sems, s_srcs, s_lands, g_pm8, _ag_copies, 4 * len(s_lands))
    g_narrow, g_wa8, g_wx8 = _ag_finish("small_grads_finish", s_lands)
    summed = dict(zip(narrow, _unflat(_sum8("sum_small_grads", g_narrow), [small[n].shape for n in narrow])))
    summed['pre_mix_norm'] = _sum8("sum_pre_mix_grad", g_pm8)
    summed['lru_w_a'] = _sum8("sum_lru_w_a_grads", g_wa8)
    summed['lru_w_x'] = _sum8("sum_lru_w_x_grads", g_wx8)
    loss = summed.pop('loss').reshape(())
    for n, full_w in (('ssd_conv_w', XBC), ('lru_conv_w', DL)):
        wdt = full_w // N_DEV
        summed[n] = lax.dynamic_slice_in_dim(summed[n], dev * wdt, wdt, axis=1)
    small_params = [n for n in names if n not in big]
    as2d = lambda a: a.reshape(-1, a.shape[-1])
    res = _adamw_small([as2d(W[n]) for n in small_params],
                       [summed[n].reshape(as2d(W[n]).shape) for n in small_params],
                       [as2d(Mo[n]) for n in small_params], [as2d(Vo[n]) for n in small_params])
    grads = {n: summed[n].reshape(W[n].shape) for n in small_params}
    delta, new_m, new_v = ({n: r.reshape(W[n].shape) for n, r in zip(small_params, rs)} for rs in res)

    big_out['w_in'] = _adamw_big("adamw_w_in", W['w_in'], Mo['w_in'], Vo['w_in'], *rs_end('w_in', rs_in, g_pm8))
    for n in big:
        grads[n], delta[n], new_m[n], new_v[n] = big_out[n]

    return (loss, grad_x.reshape(x.shape), *[grads[n] for n in names], *[delta[n] for n in names],
            *[new_m[n] for n in names], *[new_v[n] for n in names])
```

```python
import functools

import jax
import jax.numpy as jnp
from jax import lax
from jax.experimental import pallas as pl
from jax.experimental.pallas import tpu as pltpu

F32, BF16 = jnp.float32, jnp.bfloat16
S = jax.ShapeDtypeStruct
MESH = pl.DeviceIdType.MESH

SSD_GROUPS = 8
LRU_C = 8.0
EPS = 1e-6
CONV_WIDTH = 4
ADAM_LR, ADAM_B1, ADAM_B2, ADAM_EPS, ADAM_WD, ADAM_STEP = 0.001, 0.9, 0.999, 1e-08, 0.01, 10

LANES = 128
SUBLANES = 8
VMEM_LIMIT = 56 * 1024 * 1024
N_DEV = 8
SMALL_W = 512
HI = lax.Precision.HIGHEST


def _pcall(body, **kw):
    return pl.pallas_call(body, **kw)


def _cparams(sem=None, **kw):
    return pltpu.CompilerParams(dimension_semantics=sem, vmem_limit_bytes=VMEM_LIMIT, **kw)


def _pick(n, cands):
    for c in cands:
        if c <= n and n % c == 0:
            return c
    return n


def _rt(tr, w, cb=0, n=None):
    if n is None:
        return pl.BlockSpec((tr, w), lambda i: (i, cb))
    return pl.BlockSpec((tr, w), lambda i: (n - 1 - i, cb))


PACKED_ROWS = 16


def _halo_prev(tr, w, cb=0, n=None, rows=SUBLANES):
    k = tr // rows
    if n is None:
        return pl.BlockSpec((rows, w), lambda i: (jnp.maximum(i * k - 1, 0), cb))
    return pl.BlockSpec((rows, w), lambda i: (jnp.maximum((n - 1 - i) * k - 1, 0), cb))


def _halo_next(tr, w, nt, cb=0, n=None, rows=SUBLANES):
    k = tr // rows
    last = nt * k - 1
    if n is None:
        return pl.BlockSpec((rows, w), lambda i: (jnp.minimum((i + 1) * k, last), cb))
    return pl.BlockSpec((rows, w), lambda i: (jnp.minimum((n - i) * k, last), cb))


def _full(a):
    nd = a.ndim
    return pl.BlockSpec(a.shape, lambda i: (0,) * nd)


def _rows_call(name, fn, n_tiles, arrays, in_specs, out_tiled, out_acc, kinds, into=None, deps=(), cw=None):
    n_in, n_t = len(arrays), len(out_tiled)
    n_skip = len(deps) + (0 if into is None else 1)
    width = in_specs[kinds.index('t')].block_shape[1]
    cols = [(0, width)] if cw is None else [(c, cw) for c in range(0, width, cw)]

    def body(*refs):
        i = pl.program_id(0)
        ins = refs[:n_in]
        outs = refs[n_in + n_skip:n_in + n_skip + n_t]
        accs = refs[n_in + n_skip + n_t:]
        if accs:
            @pl.when(i == 0)
            def _():
                for r in accs:
                    r[...] = jnp.zeros_like(r)

        def lanes(ref, rows, c0, w):
            return ref[rows, c0:c0 + w] if ref.shape[-1] == width else ref[rows, :]

        def load(k, c0, w):
            v = lanes(ins[k], slice(None), c0, w).astype(F32)
            if kinds[k][0] in 'pn' and v.shape[0] == PACKED_ROWS:
                v = v[SUBLANES:] if kinds[k][0] == 'p' else v[:SUBLANES]
            return v

        for c0, w in cols:
            touts, aouts = fn(i == 0, i == n_tiles - 1, *[load(k, c0, w) for k in range(n_in)])
            for r, v in zip(outs, touts):
                if r.shape[-1] == width:
                    r[:, c0:c0 + w] = v.astype(r.dtype)
                else:
                    r[...] = v.astype(r.dtype)
            for r, v in zip(accs, aouts):
                if r.shape[-1] == width:
                    r[:, c0:c0 + w] += v
                else:
                    r[...] += v

    out_shape = [S(sh, dt) for sh, dt, _ in out_tiled] + [S(sh, F32) for sh in out_acc]
    out_specs = [sp for _, _, sp in out_tiled]
    for sh in out_acc:
        out_specs.append(pl.BlockSpec(sh, lambda i, nd=len(sh): (0,) * nd))
    in_specs = list(in_specs) + [_ANY] * len(deps)
    if into is None:
        return _pcall(body, name=name, grid=(n_tiles,), in_specs=in_specs, out_specs=out_specs,
                      out_shape=out_shape, compiler_params=_cparams(("arbitrary",)))(*arrays, *deps)
    return _pcall(body, name=name, grid=(n_tiles,), in_specs=in_specs + [_ANY], out_specs=out_specs,
                  out_shape=out_shape, input_output_aliases={n_in + len(deps): 0},
                  compiler_params=_cparams(("arbitrary",)))(*arrays, *deps, into)


def _rms(x, g):
    return x * lax.rsqrt(jnp.mean(x * x, axis=-1, keepdims=True) + EPS) * g


def _rms_bwd(x, g, dy):
    r = lax.rsqrt(jnp.mean(x * x, axis=-1, keepdims=True) + EPS)
    xr = x * r
    gdy = dy * g
    dx = (gdy - xr * jnp.mean(gdy * xr, axis=-1, keepdims=True)) * r
    return dx, _colsum(dy * xr)


_GELU_C0, _GELU_C1 = 0.7978845608028654, 0.044715


def _gelu_and_grad(x):
    x2 = x * x
    t = jnp.tanh(_GELU_C0 * x * (1.0 + _GELU_C1 * x2))
    half = 0.5 * (1.0 + t)
    grad = half + (0.5 * _GELU_C0) * x * (1.0 - t * t) * (1.0 + (3.0 * _GELU_C1) * x2)
    return x * half, grad


def _colsum(v):
    return jnp.sum(v, axis=0, keepdims=True)


_TILES = (1152, 1024, 896, 768, 640, 512, 384, 256, 128)
_K_TILES = (4096, 3456, 3072, 2688, 2048, 1536, 1344, 1152, 1024, 896, 768, 640, 512, 384, 256, 128)


def _mm(name, a, b, *, ta=False, tb=False, outs=((F32, None),), extra=None, out_blocks=None, tm=None, tn=None, tk=None,
        deps=(), a_cols=None):
    M, K = (a.shape[1], a.shape[0]) if ta else a.shape
    if a_cols is not None:
        assert not ta
        K = a_cols[1]
    b3 = b.ndim == 3
    if b3:
        nb_b, brows, bcols = b.shape
        N = brows if tb else nb_b * bcols
    else:
        N = b.shape[0] if tb else b.shape[1]
    n_lim = N if out_blocks is None else N // out_blocks
    if b3 and not tb:
        n_lim = min(n_lim, bcols)
    tm = tm or _pick(M, _TILES[1:])
    tn = tn or _pick(n_lim, _TILES)
    tk = tk or _pick(bcols if (b3 and tb) else K, _K_TILES)
    nk = K // tk
    assert M % tm == 0 and N % tn == 0 and K % tk == 0
    dn = (((0 if ta else 1,), (1 if tb else 0,)), ((), ()))
    n_extra = 0 if extra is None else 1
    n_out = len(outs)

    def body(*refs):
        a_ref, b_ref = refs[0], refs[1]
        e_ref = refs[2] if n_extra else None
        o_refs = refs[2 + n_extra + len(deps):2 + n_extra + len(deps) + n_out]

        def finish(r):
            e = e_ref[...] if n_extra else None
            for o, (_, f) in zip(o_refs, outs):
                o[...] = (r if f is None else f(r, e)).astype(o.dtype)

        part = lax.dot_general(a_ref[...], b_ref[...], dn, preferred_element_type=F32)
        if nk == 1:
            finish(part)
            return
        acc = refs[-1]
        k = pl.program_id(2)

        @pl.when(k == 0)
        def _():
            acc[...] = part

        @pl.when(jnp.logical_and(k > 0, k < nk - 1))
        def _():
            acc[...] += part

        @pl.when(k == nk - 1)
        def _():
            finish(acc[...] + part)

    k0 = 0 if a_cols is None else a_cols[0] // tk
    a_spec = (pl.BlockSpec((tk, tm), lambda i, j, k: (k, i)) if ta
              else pl.BlockSpec((tm, tk), lambda i, j, k: (i, k + k0)))
    if not b3:
        b_spec = pl.BlockSpec((tn, tk), lambda i, j, k: (j, k)) if tb else pl.BlockSpec((tk, tn), lambda i, j, k: (k, j))
    elif tb:
        per = bcols // tk
        b_spec = pl.BlockSpec((None, tn, tk), lambda i, j, k: (k // per, j, k % per))
    else:
        per = bcols // tn
        b_spec = pl.BlockSpec((None, tk, tn), lambda i, j, k: (j // per, k, j % per))
    o_spec = pl.BlockSpec((tm, tn), lambda i, j, k: (i, j))
    if out_blocks is None:
        out_specs, out_shape = [o_spec] * n_out, [S((M, N), dt) for dt, _ in outs]
    else:
        per_o = N // out_blocks // tn
        ob_spec = pl.BlockSpec((None, tm, tn), lambda i, j, k: (j // per_o, i, j % per_o))
        out_specs, out_shape = [ob_spec] * n_out, [S((out_blocks, M, N // out_blocks), dt) for dt, _ in outs]
    in_specs = [a_spec, b_spec] + ([o_spec] if n_extra else []) + [_ANY] * len(deps)
    args = [a, b] + ([extra] if n_extra else []) + list(deps)
    return _pcall(body, name=name, grid=(M // tm, N // tn, nk), in_specs=in_specs, out_specs=out_specs,
                  out_shape=out_shape, scratch_shapes=[pltpu.VMEM((tm, tn), F32)] if nk > 1 else [],
                  compiler_params=_cparams(("parallel", "parallel", "arbitrary")))(*args)


def _shift_down(x, halo, s):
    if s == 0:
        return x
    r = pltpu.roll(x, s, 0)
    hr = pltpu.roll(halo, s, 0)
    row = lax.broadcasted_iota(jnp.int32, halo.shape, 0)
    top = jnp.where(row < s, hr, r[:SUBLANES])
    if x.shape[0] == SUBLANES:
        return top
    return jnp.concatenate([top, r[SUBLANES:]], axis=0)


def _shift_up(x, nxt, s):
    if s == 0:
        return x
    n = x.shape[0]
    r = pltpu.roll(x, n - s, 0)
    nr = pltpu.roll(nxt, SUBLANES - s, 0)
    row = lax.broadcasted_iota(jnp.int32, nxt.shape, 0)
    bot = jnp.where(row >= SUBLANES - s, nr, r[n - SUBLANES:])
    if n == SUBLANES:
        return bot
    return jnp.concatenate([r[:n - SUBLANES], bot], axis=0)


def _conv_taps(x, halo):
    return [_shift_down(x, halo, CONV_WIDTH - 1 - k) for k in range(CONV_WIDTH)]


def _conv_pre(x, halo, w, b, taps=None):
    taps = _conv_taps(x, halo) if taps is None else taps
    acc = b + w[0:1, :] * taps[0]
    for k in range(1, CONV_WIDTH):
        acc = acc + w[k:k + 1, :] * taps[k]
    return acc


def _silu_grad(p):
    s = jax.nn.sigmoid(p)
    return s * (1.0 + p * (1.0 - s))


def _conv_bwd_tile(first, last, x, hprev, xnext, d, dnext, w, b, silu):
    hprev = jnp.where(first, 0.0, hprev)
    taps = _conv_taps(x, hprev)
    if silu:
        d = d * _silu_grad(_conv_pre(x, hprev, w, b, taps))
        pre_next = _conv_pre(xnext, x[x.shape[0] - SUBLANES:], w, b)
        dnext = dnext * _silu_grad(pre_next)
    dnext = jnp.where(last, 0.0, dnext)
    dx = w[CONV_WIDTH - 1:CONV_WIDTH, :] * d
    row8 = lax.broadcasted_iota(jnp.int32, (SUBLANES, x.shape[1]), 0)
    dw8 = jnp.where(row8 == CONV_WIDTH - 1, _colsum(d * x), 0.0)
    for k in range(CONV_WIDTH - 1):
        dx = dx + w[k:k + 1, :] * _shift_up(d, dnext, CONV_WIDTH - 1 - k)
        dw8 = dw8 + jnp.where(row8 == k, _colsum(d * taps[k]), 0.0)
    return dx, dw8, _colsum(d)


def _ssd_dims(xbc_act, n_heads):
    T, XBC = xbc_act.shape
    GN = XBC // 4
    DS = XBC - 2 * GN
    G = SSD_GROUPS
    N = GN // G
    P = DS // n_heads
    K = n_heads // G
    return T, XBC, DS, GN, G, N, P, K


def _ssd_common(dt, alog, Q):
    a = -jnp.exp(alog)
    adt = dt * a
    li = lax.broadcasted_iota(jnp.int32, (Q, Q), 0)
    si = lax.broadcasted_iota(jnp.int32, (Q, Q), 1)
    causal = li >= si
    ltri = causal.astype(F32)
    acs = jnp.dot(ltri, adt, precision=HI, preferred_element_type=F32)
    acs_row = lax.dot_general(adt, ltri, (((0,), (1,)), ((), ())), precision=HI,
                              preferred_element_type=F32)
    return a, adt, causal, ltri, acs, acs_row


def _expander(g, K, P, W):
    r = lax.broadcasted_iota(jnp.int32, (LANES, W), 0)
    c = lax.broadcasted_iota(jnp.int32, (LANES, W), 1)
    return (c // P + g * K == r).astype(F32)


def _dotb(a, b, dn=(((1,), (0,)), ((), ()))):
    return lax.dot_general(a.astype(BF16), b.astype(BF16), dn, preferred_element_type=F32)


def _dot_split(a, sel, terms, dn=(((1,), (0,)), ((), ()))):
    selb = sel.astype(BF16)
    out = None
    for _ in range(terms):
        piece = a.astype(BF16)
        part = lax.dot_general(piece, selb, dn, preferred_element_type=F32)
        out = part if out is None else out + part
        a = a - piece.astype(F32)
    return out


_NT = (((1,), (1,)), ((), ()))
_TN = (((0,), (0,)), ((), ()))


def _ssd_fwd(xbc_act, dt, alog, dskip, n_heads, Q):
    T, XBC, DS, GN, G, N, P, K = _ssd_dims(xbc_act, n_heads)
    W = K * P
    nc = T // Q

    def body(xs_ref, b_ref, c_ref, dt_ref, alog_ref, d_ref, y_ref, hp_ref, h_scr):
        ci = pl.program_id(0)

        @pl.when(ci == 0)
        def _():
            h_scr[...] = jnp.zeros_like(h_scr)

        dtv = dt_ref[...]
        a, adt, causal, ltri, acs, acs_row = _ssd_common(dtv, alog_ref[...], Q)
        lane_head = lax.broadcasted_iota(jnp.int32, (Q, W), 1) // P
        for g in range(G):
            eg = _expander(g, K, P, W)
            dtb = _dot_split(dtv, eg, 3)
            acsb = _dot_split(acs, eg, 3)
            lastb = acsb[Q - 1:Q, :]
            db = _dot_split(jnp.broadcast_to(d_ref[...], (SUBLANES, LANES)), eg, 3)[0:1, :]
            xg = xs_ref[:, g * W:(g + 1) * W]
            bg = b_ref[:, g * N:(g + 1) * N]
            cg = c_ref[:, g * N:(g + 1) * N]
            xt = xg * dtb
            sc = _dotb(cg, bg, _NT)
            yd = jnp.zeros((Q, W), F32)
            for k in range(K):
                h = g * K + k
                seg = acs[:, h:h + 1] - acs_row[h:h + 1, :]
                lh = jnp.where(causal, jnp.exp(jnp.minimum(seg, 0.0)), 0.0)
                xk = jnp.where(lane_head == k, xt, 0.0)
                yd = yd + _dotb(sc * lh, xk)
            hp = h_scr[g]
            yoff = _dotb(cg, hp) * jnp.exp(acsb)
            y_ref[:, g * W:(g + 1) * W] = yd + yoff + xg * db
            e_end = jnp.exp(lastb - acsb)
            st = _dotb(bg, xt * e_end, _TN)
            hp_ref[0, g] = hp
            h_scr[g] = jnp.exp(lastb) * hp + st

    cb = DS // GN
    in_specs = [pl.BlockSpec((Q, DS), lambda c: (c, 0)),
                pl.BlockSpec((Q, GN), lambda c: (c, cb)),
                pl.BlockSpec((Q, GN), lambda c: (c, cb + 1)),
                pl.BlockSpec((Q, LANES), lambda c: (c, 0)),
                pl.BlockSpec((1, LANES), lambda c: (0, 0)),
                pl.BlockSpec((1, LANES), lambda c: (0, 0))]
    out_specs = [pl.BlockSpec((Q, DS), lambda c: (c, 0)),
                 pl.BlockSpec((1, G, N, W), lambda c: (c, 0, 0, 0))]
    return _pcall(body, name="ssd_fwd", grid=(nc,), in_specs=in_specs, out_specs=out_specs,
                  out_shape=[S((T, DS), F32), S((nc, G, N, W), F32)],
                  scratch_shapes=[pltpu.VMEM((G, N, W), F32)],
                  compiler_params=_cparams(("arbitrary",)))(xbc_act, xbc_act, xbc_act, dt, alog, dskip)


def _ssd_bwd(xbc_act, dt, alog, dskip, hprev, dy, n_heads, Q):
    T, XBC, DS, GN, G, N, P, K = _ssd_dims(xbc_act, n_heads)
    W = K * P
    nc = T // Q

    def body(xs_ref, b_ref, c_ref, dt_ref, alog_ref, d_ref, hp_ref, dy_ref,
             dxbc_ref, ddt_ref, dalog_ref, dd_ref, dh_scr):
        ci = pl.program_id(0)

        @pl.when(ci == 0)
        def _():
            dh_scr[...] = jnp.zeros_like(dh_scr)
            dalog_ref[...] = jnp.zeros_like(dalog_ref)
            dd_ref[...] = jnp.zeros_like(dd_ref)

        dtv = dt_ref[...]
        a, adt, causal, ltri, acs, acs_row = _ssd_common(dtv, alog_ref[...], Q)
        lane_head = lax.broadcasted_iota(jnp.int32, (Q, W), 1) // P
        lane128 = lax.broadcasted_iota(jnp.int32, (Q, LANES), 1)
        sub128 = lax.broadcasted_iota(jnp.int32, (LANES, Q), 0)
        rowq = lax.broadcasted_iota(jnp.int32, (Q, W), 0)
        dacs = jnp.zeros((Q, LANES), F32)
        dacs_row = jnp.zeros((LANES, Q), F32)
        ddt = jnp.zeros((Q, LANES), F32)
        dd_acc = jnp.zeros((1, LANES), F32)
        for g in range(G):
            eg = _expander(g, K, P, W)
            dtb = _dot_split(dtv, eg, 3)
            acsb = _dot_split(acs, eg, 3)
            lastb = acsb[Q - 1:Q, :]
            db = _dot_split(jnp.broadcast_to(d_ref[...], (SUBLANES, LANES)), eg, 3)[0:1, :]
            xg = xs_ref[:, g * W:(g + 1) * W]
            bg = b_ref[:, g * N:(g + 1) * N]
            cg = c_ref[:, g * N:(g + 1) * N]
            dyg = dy_ref[:, g * W:(g + 1) * W].astype(F32)
            hp = hp_ref[0, g]
            dhn = dh_scr[g]
            xt = xg * dtb
            sc = _dotb(cg, bg, _NT)
            eacs = jnp.exp(acsb)
            e_end = jnp.exp(lastb - acsb)
            elast = jnp.exp(lastb)

            wv = dyg * eacs
            dcg = _dotb(wv, hp, _NT)
            dhp = _dotb(cg, wv, _TN) + elast * dhn
            dacsb = dyg * (_dotb(cg, hp) * eacs)

            xe = xt * e_end
            dbg = _dotb(xe, dhn, _NT)
            v = _dotb(bg, dhn)
            dxt = v * e_end
            de = v * xe
            dacsb = dacsb - de
            dlastb = _colsum(de) + elast * jnp.sum(dhn * hp, axis=0, keepdims=True)

            dsc = jnp.zeros((Q, Q), F32)
            for k in range(K):
                h = g * K + k
                seg = acs[:, h:h + 1] - acs_row[h:h + 1, :]
                lh = jnp.where(causal, jnp.exp(jnp.minimum(seg, 0.0)), 0.0)
                mh = sc * lh
                dyk = jnp.where(lane_head == k, dyg, 0.0)
                dxt = dxt + jnp.where(lane_head == k, _dotb(mh, dyg, _TN), 0.0)
                dm = _dotb(dyk, xt, _NT)
                dsc = dsc + dm * lh
                gm = dm * mh
                dacs = dacs + jnp.where(lane128 == h, jnp.sum(gm, axis=1, keepdims=True), 0.0)
                dacs_row = dacs_row - jnp.where(sub128 == h, jnp.sum(gm, axis=0, keepdims=True), 0.0)
            dcg = dcg + _dotb(dsc, bg)
            dbg = dbg + _dotb(dsc, cg, _TN)

            dacsb = dacsb + jnp.where(rowq == Q - 1, dlastb, 0.0)
            dacs = dacs + _dot_split(dacsb, eg, 2, _NT)
            ddt = ddt + _dot_split(dxt * xg, eg, 2, _NT)
            dd_acc = dd_acc + _dot_split(jnp.broadcast_to(_colsum(dyg * xg), (SUBLANES, W)), eg, 2, _NT)[0:1, :]
            dxbc_ref[:, g * W:(g + 1) * W] = dxt * dtb + dyg * db
            dxbc_ref[:, DS + g * N:DS + (g + 1) * N] = dbg
            dxbc_ref[:, DS + GN + g * N:DS + GN + (g + 1) * N] = dcg
            dh_scr[g] = dhp

        eye = (lax.broadcasted_iota(jnp.int32, (LANES, LANES), 0) ==
               lax.broadcasted_iota(jnp.int32, (LANES, LANES), 1)).astype(F32)
        dacs = dacs + lax.dot_general(dacs_row, eye, _TN, precision=HI, preferred_element_type=F32)
        dadt = lax.dot_general(ltri, dacs, _TN, precision=HI, preferred_element_type=F32)
        ddt_ref[...] = ddt + dadt * a
        dalog_ref[...] += _colsum(dadt * dtv) * a
        dd_ref[...] += dd_acc

    cb = DS // GN
    rv = lambda c: nc - 1 - c
    in_specs = [pl.BlockSpec((Q, DS), lambda c: (rv(c), 0)),
                pl.BlockSpec((Q, GN), lambda c: (rv(c), cb)),
                pl.BlockSpec((Q, GN), lambda c: (rv(c), cb + 1)),
                pl.BlockSpec((Q, LANES), lambda c: (rv(c), 0)),
                pl.BlockSpec((1, LANES), lambda c: (0, 0)),
                pl.BlockSpec((1, LANES), lambda c: (0, 0)),
                pl.BlockSpec((1, G, N, W), lambda c: (rv(c), 0, 0, 0)),
                pl.BlockSpec((Q, DS), lambda c: (rv(c), 0))]
    out_specs = [pl.BlockSpec((Q, XBC), lambda c: (rv(c), 0)),
                 pl.BlockSpec((Q, LANES), lambda c: (rv(c), 0)),
                 pl.BlockSpec((1, LANES), lambda c: (0, 0)),
                 pl.BlockSpec((1, LANES), lambda c: (0, 0))]
    return _pcall(body, name="ssd_bwd", grid=(nc,), in_specs=in_specs, out_specs=out_specs,
                  out_shape=[S((T, XBC), F32), S((T, LANES), F32), S((1, LANES), F32), S((1, LANES), F32)],
                  scratch_shapes=[pltpu.VMEM((G, N, W), F32)],
                  compiler_params=_cparams(("arbitrary",)))(
                      xbc_act, xbc_act, xbc_act, dt, alog, dskip, hprev, dy)


def _blockdiag(x, w_ref, dn=(((1,), (0,)), ((), ()))):
    H, B, _ = w_ref.shape
    return jnp.concatenate([_dotb(x[:, h * B:(h + 1) * B], w_ref[h], dn) for h in range(H)], axis=1)


def _lru_elem(xl, r_pre, i_pre, lam):
    r = jax.nn.sigmoid(r_pre)
    i = jax.nn.sigmoid(i_pre)
    log_a = -LRU_C * r * jax.nn.softplus(-lam)
    a = jnp.exp(log_a)
    u = jnp.sqrt(1.0 - jnp.exp(2.0 * log_a)) * (i * xl)
    return a, u


def _lru_elem_bwd(xl, r_pre, i_pre, lam, da, du):
    r = jax.nn.sigmoid(r_pre)
    i = jax.nn.sigmoid(i_pre)
    sp = jax.nn.softplus(-lam)
    a = jnp.exp(-LRU_C * r * sp)
    s = jnp.sqrt(1.0 - a * a)
    d_ix = du * s
    dlog_a = (da - du * (i * xl) * a / s) * a
    dr_pre = dlog_a * (-LRU_C) * sp * r * (1.0 - r)
    dlam = _colsum(dlog_a * r) * (LRU_C * jax.nn.sigmoid(-lam))
    di_pre = d_ix * xl * i * (1.0 - i)
    return d_ix * i, dr_pre, di_pre, dlam


def _lru_gates_fwd(xl, w_a, b_a, w_x, b_x, lam, tr):
    T, DL = xl.shape

    def body(xl_ref, wa_ref, ba_ref, wx_ref, bx_ref, lam_ref, a_ref, u_ref):
        x = xl_ref[...]
        r_pre = _blockdiag(x, wa_ref) + ba_ref[...]
        i_pre = _blockdiag(x, wx_ref) + bx_ref[...]
        a, u = _lru_elem(x, r_pre, i_pre, lam_ref[...])
        a_ref[...] = a
        u_ref[...] = u

    w3 = pl.BlockSpec(w_a.shape, lambda i: (0, 0, 0))
    vec = pl.BlockSpec((1, DL), lambda i: (0, 0))
    return _pcall(body, name="lru_gates_fwd", grid=(T // tr,),
                  in_specs=[_rt(tr, DL), w3, vec, w3, vec, vec],
                  out_specs=[_rt(tr, DL), _rt(tr, DL)], out_shape=[S((T, DL), F32), S((T, DL), F32)],
                  compiler_params=_cparams(("parallel",)))(xl, w_a, b_a, w_x, b_x, lam)


def _lru_gates_bwd(xl, w_a, b_a, w_x, b_x, lam, da, du, tr):
    T, DL = xl.shape
    H, B, _ = w_a.shape

    def body(xl_ref, wa_ref, ba_ref, wx_ref, bx_ref, lam_ref, da_ref, du_ref,
             dxl_ref, dwa_ref, dba_ref, dwx_ref, dbx_ref, dlam_ref):
        @pl.when(pl.program_id(0) == 0)
        def _():
            for r in (dwa_ref, dba_ref, dwx_ref, dbx_ref, dlam_ref):
                r[...] = jnp.zeros_like(r)

        x = xl_ref[...]
        r_pre = _blockdiag(x, wa_ref) + ba_ref[...]
        i_pre = _blockdiag(x, wx_ref) + bx_ref[...]
        dx, dr, di, dlam = _lru_elem_bwd(x, r_pre, i_pre, lam_ref[...], da_ref[...], du_ref[...])
        dxl_ref[...] = (dx + _blockdiag(dr, wa_ref, _NT) + _blockdiag(di, wx_ref, _NT)).astype(dxl_ref.dtype)
        for h in range(H):
            xh = x[:, h * B:(h + 1) * B]
            dwa_ref[h] += _dotb(xh, dr[:, h * B:(h + 1) * B], _TN)
            dwx_ref[h] += _dotb(xh, di[:, h * B:(h + 1) * B], _TN)
        dba_ref[...] += _colsum(dr)
        dbx_ref[...] += _colsum(di)
        dlam_ref[...] += dlam

    w3 = pl.BlockSpec(w_a.shape, lambda i: (0, 0, 0))
    vec = pl.BlockSpec((1, DL), lambda i: (0, 0))
    return _pcall(body, name="lru_gates_bwd", grid=(T // tr,),
                  in_specs=[_rt(tr, DL), w3, vec, w3, vec, vec, _rt(tr, DL), _rt(tr, DL)],
                  out_specs=[_rt(tr, DL), w3, vec, w3, vec, vec],
                  out_shape=[S((T, DL), BF16), S(w_a.shape, F32), S((1, DL), F32), S(w_a.shape, F32),
                             S((1, DL), F32), S((1, DL), F32)],
                  compiler_params=_cparams(("arbitrary",)))(xl, w_a, b_a, w_x, b_x, lam, da, du)


def _groups(v):
    return v.reshape(v.shape[0] // SUBLANES, SUBLANES, v.shape[1])


def _rows_shifted(v, edge, up):
    sub = lax.broadcasted_iota(jnp.int32, v.shape, 1)
    if up:
        other = jnp.concatenate([v[1:], edge[None]], axis=0)
        return jnp.where(sub < SUBLANES - 1, pltpu.roll(v, SUBLANES - 1, 1), pltpu.roll(other, SUBLANES - 1, 1))
    other = jnp.concatenate([edge[None], v[:-1]], axis=0)
    return jnp.where(sub >= 1, pltpu.roll(v, 1, 1), pltpu.roll(other, 1, 1))


def _scan_tile(a, u, entering, emit, up):
    G = a.shape[0]
    sub = lax.broadcasted_iota(jnp.int32, a.shape, 1)
    d = 1
    while d < SUBLANES:
        if up:
            keep = sub < SUBLANES - d
            a_s = jnp.where(keep, pltpu.roll(a, SUBLANES - d, 1), 1.0)
            u_s = jnp.where(keep, pltpu.roll(u, SUBLANES - d, 1), 0.0)
        else:
            keep = sub >= d
            a_s = jnp.where(keep, pltpu.roll(a, d, 1), 1.0)
            u_s = jnp.where(keep, pltpu.roll(u, d, 1), 0.0)
        u = a * u_s + u
        a = a * a_s
        d *= 2
    for g in (reversed(range(G)) if up else range(G)):
        hg = u[g] + a[g] * entering
        emit(g, hg)
        entering = hg[0:1] if up else hg[SUBLANES - 1:SUBLANES]
    return entering


def _lru_scan_fwd(a, u, tr, deps=()):
    T, DL = a.shape

    def body(a_ref, u_ref, *rest):
        h_ref, carry = rest[len(deps):]

        @pl.when(pl.program_id(0) == 0)
        def _():
            carry[...] = jnp.zeros_like(carry)

        def emit(g, hg):
            h_ref[g * SUBLANES:(g + 1) * SUBLANES, :] = hg

        last = _scan_tile(_groups(a_ref[...]), _groups(u_ref[...]), carry[0:1, :], emit, up=False)
        carry[...] = jnp.broadcast_to(last, carry.shape)

    return _pcall(body, name="lru_scan_fwd", grid=(T // tr,),
                  in_specs=[_rt(tr, DL), _rt(tr, DL)] + [_ANY] * len(deps),
                  out_specs=_rt(tr, DL), out_shape=S((T, DL), F32),
                  scratch_shapes=[pltpu.VMEM((SUBLANES, DL), F32)],
                  compiler_params=_cparams(("arbitrary",)))(a, u, *deps)


def _lru_scan_bwd(a, h, dh, tr):
    T, DL = a.shape
    n = T // tr

    def body(a_ref, an_ref, h_ref, hp_ref, dh_ref, du_ref, da_ref, carry):
        i = pl.program_id(0)
        ti = n - 1 - i

        @pl.when(i == 0)
        def _():
            carry[...] = jnp.zeros_like(carry)

        a_next = _rows_shifted(_groups(a_ref[...]), jnp.where(ti == n - 1, 0.0, an_ref[...]), up=True)
        h_prev = _rows_shifted(_groups(h_ref[...]), jnp.where(ti == 0, 0.0, hp_ref[...]), up=False)

        def emit(g, gg):
            du_ref[g * SUBLANES:(g + 1) * SUBLANES, :] = gg
            da_ref[g * SUBLANES:(g + 1) * SUBLANES, :] = gg * h_prev[g]

        top = _scan_tile(a_next, _groups(dh_ref[...].astype(F32)), carry[0:1, :], emit, up=True)
        carry[...] = jnp.broadcast_to(top, carry.shape)

    return _pcall(body, name="lru_scan_bwd", grid=(n,),
                  in_specs=[_rt(tr, DL, 0, n), _halo_next(tr, DL, n, 0, n), _rt(tr, DL, 0, n),
                            _halo_prev(tr, DL, 0, n), _rt(tr, DL, 0, n)],
                  out_specs=[_rt(tr, DL, 0, n), _rt(tr, DL, 0, n)],
                  out_shape=[S((T, DL), F32), S((T, DL), F32)],
                  scratch_shapes=[pltpu.VMEM((SUBLANES, DL), F32)],
                  compiler_params=_cparams(("arbitrary",)))(a, a, h, h, dh)


def _adamw(w, g, m, v):
    m = ADAM_B1 * m + (1.0 - ADAM_B1) * g
    v = ADAM_B2 * v + (1.0 - ADAM_B2) * (g * g)
    m_hat = m / (1.0 - ADAM_B1 ** ADAM_STEP)
    v_hat = v / (1.0 - ADAM_B2 ** ADAM_STEP)
    delta = -ADAM_LR * (m_hat / (jnp.sqrt(v_hat) + ADAM_EPS) + ADAM_WD * w)
    return delta, m, v


def _adamw_big(name, w, m, v, own, recv, own_idx):
    _, R, C = w.shape
    n_recv = recv.shape[0]
    tr = _pick(R, (256, 128, 64, 32, 16))

    def body(idx_ref, w_ref, m_ref, v_ref, p_ref, *rest):
        g = p_ref[...].astype(F32)
        for r in rest[:n_recv]:
            g = g + r[...].astype(F32)
        g_ref, d_ref, nm_ref, nv_ref = rest[n_recv:]
        d, nm, nv = _adamw(w_ref[...], g, m_ref[...], v_ref[...])
        g_ref[...] = g
        d_ref[...] = d
        nm_ref[...] = nm
        nv_ref[...] = nv

    r_spec = lambda s: pl.BlockSpec((None, tr, C), lambda i, idx: (s, i, 0))
    t2 = r_spec(0)
    gs = pltpu.PrefetchScalarGridSpec(
        num_scalar_prefetch=1, grid=(R // tr,),
        in_specs=[t2, t2, t2, pl.BlockSpec((None, tr, C), lambda i, idx: (idx[0], i, 0))]
        + [r_spec(s) for s in range(n_recv)],
        out_specs=[t2, t2, t2, t2])
    return _pcall(body, name=name, grid_spec=gs, out_shape=[S((1, R, C), F32)] * 4,
                  compiler_params=_cparams(("parallel",)))(own_idx, w, m, v, own, *([recv] * n_recv))


def _adamw_small(ws, gs, ms, vs):
    n = len(ws)

    def body(*refs):
        for k in range(n):
            d, nm, nv = _adamw(refs[k][...], refs[n + k][...], refs[2 * n + k][...], refs[3 * n + k][...])
            refs[4 * n + k][...] = d
            refs[5 * n + k][...] = nm
            refs[6 * n + k][...] = nv

    res = _pcall(body, name="adamw_small", out_shape=[S(w.shape, F32) for w in ws] * 3,
                 compiler_params=_cparams())(*ws, *gs, *ms, *vs)
    return res[:n], res[n:2 * n], res[2 * n:]


def _sum8(name, parts):
    _, R, C = parts.shape

    def body(p_ref, o_ref):
        acc = p_ref[0]
        for k in range(1, N_DEV):
            acc = acc + p_ref[k]
        o_ref[...] = acc

    return _pcall(body, name=name, out_shape=S((R, C), F32), compiler_params=_cparams())(parts)


def _pair_sum(name, full, recv, c_idx):
    _, R, C = full.shape
    tr = _pick(R, (256, 128, 64, 32, 16))

    def body(c_ref, f_ref, r_ref, o_ref):
        o_ref[...] = (f_ref[...].astype(F32) + r_ref[...].astype(F32)).astype(o_ref.dtype)

    gs = pltpu.PrefetchScalarGridSpec(
        num_scalar_prefetch=1, grid=(4, R // tr),
        in_specs=[pl.BlockSpec((None, tr, C), lambda j, i, c: (2 * j + c[0], i, 0)),
                  pl.BlockSpec((None, tr, C), lambda j, i, c: (j, i, 0))],
        out_specs=pl.BlockSpec((None, tr, C), lambda j, i, c: (j, i, 0)))
    return _pcall(body, name=name, grid_spec=gs, out_shape=S((4, R, C), BF16),
                  compiler_params=_cparams(("parallel", "parallel")))(c_idx, full, recv)


def _cast_bf16(name, w, dev_idx, row0=0, rows=None, deps=()):
    C = w.shape[2]
    R = w.shape[1] if rows is None else rows
    tr = _pick(R, (256, 128, 64, 32, 16))
    b0 = row0 // tr

    def body(d_ref, w_ref, *rest):
        rest[-1][...] = w_ref[...].astype(BF16)

    gs = pltpu.PrefetchScalarGridSpec(
        num_scalar_prefetch=1, grid=(R // tr,),
        in_specs=[pl.BlockSpec((None, tr, C), lambda i, d: (0, i + b0, 0))] + [_ANY] * len(deps),
        out_specs=pl.BlockSpec((None, tr, C), lambda i, d: (d[0], i, 0)))
    return _pcall(body, name=name, grid_spec=gs, out_shape=S((N_DEV, R, C), BF16),
                  compiler_params=_cparams(("parallel",)))(dev_idx, w, *deps)


def _own_block(v, dev):
    return lax.dynamic_update_slice(lax.empty((N_DEV,) + v.shape, v.dtype), v[None], (dev,) + (0,) * v.ndim)


_ANY = pl.BlockSpec(memory_space=pl.ANY)


def _position():
    return lax.axis_index("x"), lax.axis_index("y"), lax.axis_index("c")


def _allgather(name, bufs, deps=()):
    n = len(bufs)
    nd = len(deps)

    def body(*refs):
        outs = refs[n + nd:2 * n + nd]
        send, recv = refs[2 * n + nd:]
        x, y, c = _position()
        me, sib = (x, y, c), (x, y, 1 - c)
        chips = [(1 - x, y), (x, 1 - y), (1 - x, 1 - y)]

        def copy(a, k, block, to):
            bx, by, bc = block
            blk = outs[a].at[4 * bx + 2 * by + bc]
            return pltpu.make_async_remote_copy(
                src_ref=blk, dst_ref=blk, send_sem=send.at[a, k], recv_sem=recv.at[a, k],
                device_id=to, device_id_type=MESH)

        first = []
        for a in range(n):
            first.append(copy(a, 0, me, sib))
            first += [copy(a, 1 + j, me, (*chip, c)) for j, chip in enumerate(chips)]
        for cp in first:
            cp.start()
        passed = []
        for j, chip in enumerate(chips):
            for a in range(n):
                copy(a, 1 + j, (*chip, c), me).wait_recv()
                cp = copy(a, 4 + j, (*chip, c), sib)
                cp.start()
                passed.append(cp)
        for a in range(n):
            copy(a, 0, sib, me).wait_recv()
        for j, chip in enumerate(chips):
            for a in range(n):
                copy(a, 4 + j, (*chip, 1 - c), me).wait_recv()
        for cp in first + passed:
            cp.wait_send()

    return _pcall(body, name=name, in_specs=[_ANY] * (n + nd), out_specs=[_ANY] * n,
                  out_shape=[S(b.shape, b.dtype) for b in bufs], input_output_aliases={a: a for a in range(n)},
                  scratch_shapes=[pltpu.SemaphoreType.DMA((n, 7)), pltpu.SemaphoreType.DMA((n, 7))])(*bufs, *deps)


def _rs_sibling(name, fulls):
    n = len(fulls)

    def body(*refs):
        ins, outs = refs[:n], refs[n:2 * n]
        send, recv = refs[2 * n:]
        x, y, c = _position()
        copies = []
        for a in range(n):
            for j in range(4):
                copies.append(pltpu.make_async_remote_copy(
                    src_ref=ins[a].at[2 * j + (1 - c)], dst_ref=outs[a].at[j], send_sem=send.at[a, j],
                    recv_sem=recv.at[a, j], device_id=(x, y, 1 - c), device_id_type=MESH))
        for cp in copies:
            cp.start()
        for cp in copies:
            cp.wait()

    return _pcall(body, name=name, in_specs=[_ANY] * n, out_specs=[_ANY] * n,
                  out_shape=[S((4,) + f.shape[1:], f.dtype) for f in fulls],
                  scratch_shapes=[pltpu.SemaphoreType.DMA((n, 4)), pltpu.SemaphoreType.DMA((n, 4))])(*fulls)


_HBM = pl.BlockSpec(memory_space=pltpu.HBM)
_SEM = pl.BlockSpec(memory_space=pltpu.SEMAPHORE)
_EFFECT = pltpu.SideEffectType.DATAFLOW_SIDE_EFFECTING


def _remote_copies(copies_fn, srcs, lands, send, recv):
    x, y, c = _position()
    return [pltpu.make_async_remote_copy(src_ref=s, dst_ref=d, send_sem=send[i], recv_sem=recv[i], device_id=to,
                                         device_id_type=MESH)
            for i, (s, d, to) in enumerate(copies_fn(x, y, c, srcs, lands))]


def _split_start(name, srcs, lands, copies_fn, nc, after=()):
    n, nl, na = len(srcs), len(lands), len(after)

    def body(*refs):
        src_refs, land_refs = refs[:n], refs[n:n + nl]
        outs = refs[n + nl + na:]
        for cp in _remote_copies(copies_fn, src_refs, land_refs, outs[:nc], outs[nc:2 * nc]):
            cp.start()
        outs[-1][...] = jnp.zeros_like(outs[-1])

    hbm = lambda a: pltpu.with_memory_space_constraint(a, pltpu.HBM)
    res = _pcall(
        body, name=name, in_specs=[_HBM] * (n + nl) + [_ANY] * na,
        out_specs=[_SEM] * (2 * nc) + [_HBM] * (n + nl) + [pl.BlockSpec(memory_space=pltpu.VMEM)],
        out_shape=[pltpu.SemaphoreType.DMA(())] * (2 * nc) + [pltpu.HBM(s.shape, s.dtype) for s in srcs]
        + [pltpu.HBM(l.shape, l.dtype) for l in lands] + [S((SUBLANES, LANES), F32)],
        input_output_aliases={i: 2 * nc + i for i in range(n + nl)},
        compiler_params=pltpu.CompilerParams(has_side_effects=_EFFECT),
    )(*[hbm(s) for s in srcs], *[hbm(l) for l in lands], *after)
    return res[:2 * nc], res[2 * nc:2 * nc + n], res[2 * nc + n:2 * nc + n + nl], res[-1]


def _split_wait(name, sems, srcs, lands, after, copies_fn, nc):
    n, nl = len(srcs), len(lands)
    after = list(after) if isinstance(after, (list, tuple)) else [after]

    def body(*refs):
        src_refs, land_refs = refs[:n], refs[n:n + nl]
        sem_refs = refs[n + nl:n + nl + 2 * nc]
        for cp in _remote_copies(copies_fn, src_refs, land_refs, sem_refs[:nc], sem_refs[nc:]):
            cp.wait_send()
            cp.wait_recv()

    res = _pcall(
        body, name=name, in_specs=[_HBM] * (n + nl) + [_SEM] * (2 * nc) + [_ANY] * len(after),
        out_specs=[_HBM] * (n + nl), out_shape=[pltpu.HBM(a.shape, a.dtype) for a in list(srcs) + list(lands)],
        input_output_aliases={i: i for i in range(n + nl)},
        compiler_params=pltpu.CompilerParams(has_side_effects=_EFFECT),
    )(*srcs, *lands, *sems, *after)
    return res[:n], res[n:]


def _other_chips(x, y):
    return [(1 - x, y), (x, 1 - y), (1 - x, 1 - y)]


def _ag_copies(x, y, c, srcs, lands):
    out = []
    for land in lands:
        blk = land.at[4 * x + 2 * y + c]
        out.append((blk, blk, (x, y, 1 - c)))
        out += [(blk, blk, (px, py, c)) for px, py in _other_chips(x, y)]
    return out


def _rs_copies(x, y, c, srcs, lands):
    return [(s.at[2 * px + py], land.at[j], (px, py, c))
            for s, land in zip(srcs, lands) for j, (px, py) in enumerate(_other_chips(x, y))]


def _fwd_copies(x, y, c, srcs, lands):
    out = []
    for land in lands:
        for px, py in _other_chips(x, y):
            blk = land.at[4 * px + 2 * py + c]
            out.append((blk, blk, (x, y, 1 - c)))
    return out


def _direct_copies(x, y, c, srcs, lands):
    out = []
    for s, land in zip(srcs, lands):
        for r in range(1, N_DEV):
            px = 1 - x if r & 4 else x
            py = 1 - y if r & 2 else y
            pc = 1 - c if r & 1 else c
            out.append((s.at[4 * px + 2 * py + pc], land.at[r - 1], (px, py, pc)))
    return out


def _ag_finish(name, lands):
    n = len(lands)

    def body(*refs):
        outs = refs[n:2 * n]
        send, recv = refs[2 * n:]
        x, y, c = _position()

        def swap(a, j, px, py, pc):
            blk = outs[a].at[4 * px + 2 * py + pc]
            return pltpu.make_async_remote_copy(src_ref=blk, dst_ref=blk, send_sem=send.at[a, j], recv_sem=recv.at[a, j],
                                                device_id=(x, y, 1 - c), device_id_type=MESH)

        chips = _other_chips(x, y)
        sends = [swap(a, j, px, py, c) for a in range(n) for j, (px, py) in enumerate(chips)]
        for cp in sends:
            cp.start()
        for a in range(n):
            for j, (px, py) in enumerate(chips):
                swap(a, j, px, py, 1 - c).wait_recv()
        for cp in sends:
            cp.wait_send()

    return _pcall(body, name=name, in_specs=[_ANY] * n, out_specs=[_ANY] * n,
                  out_shape=[S(l.shape, l.dtype) for l in lands], input_output_aliases={a: a for a in range(n)},
                  scratch_shapes=[pltpu.SemaphoreType.DMA((n, 3)), pltpu.SemaphoreType.DMA((n, 3))])(*lands)


def _pad_lanes(v):
    return jnp.pad(v, ((0, 0), (0, LANES - v.shape[1])))


def _flat_rows(pieces):
    flat = jnp.concatenate([p.reshape(-1) for p in pieces])
    rows = -(-flat.shape[0] // (SMALL_W * SUBLANES)) * SUBLANES
    return jnp.pad(flat, (0, rows * SMALL_W - flat.shape[0])).reshape(rows, SMALL_W)


def _unflat(buf, shapes):
    flat = buf.reshape(-1)
    out, off = [], 0
    for sh in shapes:
        n = 1
        for d in sh:
            n *= d
        out.append(flat[off:off + n].reshape(sh))
        off += n
    return out


def kernel(x, pre_mix_norm, w_in, ssd_conv_w, ssd_conv_b, ssd_dt_bias, ssd_a_log, ssd_d, ssd_norm, lru_conv_w, lru_conv_b, lru_w_a, lru_b_a, lru_w_x, lru_b_x, lru_lambda, lru_norm, w_out, post_mix_norm, pre_mlp_norm, w_mlp_in, w_mlp_out, post_mlp_norm, loss_target, m_pre_mix_norm, m_w_in, m_ssd_conv_w, m_ssd_conv_b, m_ssd_dt_bias, m_ssd_a_log, m_ssd_d, m_ssd_norm, m_lru_conv_w, m_lru_conv_b, m_lru_w_a, m_lru_b_a, m_lru_w_x, m_lru_b_x, m_lru_lambda, m_lru_norm, m_w_out, m_post_mix_norm, m_pre_mlp_norm, m_w_mlp_in, m_w_mlp_out, m_post_mlp_norm, v_pre_mix_norm, v_w_in, v_ssd_conv_w, v_ssd_conv_b, v_ssd_dt_bias, v_ssd_a_log, v_ssd_d, v_ssd_norm, v_lru_conv_w, v_lru_conv_b, v_lru_w_a, v_lru_b_a, v_lru_w_x, v_lru_b_x, v_lru_lambda, v_lru_norm, v_w_out, v_post_mix_norm, v_pre_mlp_norm, v_w_mlp_in, v_w_mlp_out, v_post_mlp_norm):
    names = ['pre_mix_norm', 'w_in', 'ssd_conv_w', 'ssd_conv_b', 'ssd_dt_bias', 'ssd_a_log', 'ssd_d', 'ssd_norm',
             'lru_conv_w', 'lru_conv_b', 'lru_w_a', 'lru_b_a', 'lru_w_x', 'lru_b_x', 'lru_lambda', 'lru_norm',
             'w_out', 'post_mix_norm', 'pre_mlp_norm', 'w_mlp_in', 'w_mlp_out', 'post_mlp_norm']
    loc = locals()
    W = {n: loc[n] for n in names}
    Mo = {n: loc["m_" + n] for n in names}
    Vo = {n: loc["v_" + n] for n in names}
    big = ['w_in', 'w_out', 'w_mlp_in', 'w_mlp_out']

    px, py, pc = _position()
    dev = 4 * px + 2 * py + pc
    dev_idx = jnp.reshape(dev, (1,)).astype(jnp.int32)
    c_idx = jnp.reshape(pc, (1,)).astype(jnp.int32)
    chip_idx = jnp.reshape(2 * px + py, (1,)).astype(jnp.int32)

    _, T, D = x.shape
    x2 = x.reshape(T, D)
    tgt = loss_target.reshape(T, D)
    n_heads = ssd_dt_bias.shape[1]
    XBC = ssd_conv_b.shape[1]
    GN = XBC // 4
    DS = XBC - 2 * GN
    DL = lru_norm.shape[1]
    DFF = w_mlp_in.shape[2] * N_DEV
    DIN = w_in.shape[2] * N_DEV
    NP = XBC + DS + 2 * DL + LANES
    assert DS % GN == 0 and XBC % DS == 0 and DS == DL and n_heads <= LANES
    cb_z, cb_gate, cb_xl, cb_dt = XBC // DS, XBC // DS + 1, XBC // DS + 2, (XBC + DS + 2 * DL) // LANES
    tr = min(256, T // 2)
    nt = T // tr
    Q = min(256, T // 2)

    Dh = D // 2
    sh_a = _cast_bf16("cast_w_in_a", W['w_in'], dev_idx, 0, Dh)
    sh_b = _cast_bf16("cast_w_in_b", W['w_in'], dev_idx, Dh, Dh)
    later = big[1:]
    a_bufs = [sh_a, _own_block(ssd_conv_w[0], dev), _own_block(lru_conv_w[0], dev)]
    a_sems, _, a_lands, a_token = _split_start("allgather_w_in_a_start", [], a_bufs, _ag_copies, 4 * len(a_bufs))
    sh = {n: _cast_bf16("cast_" + n, W[n], dev_idx, deps=[a_token]) for n in later}

    def f_norm_in(first, last, xv, g):
        return (_rms(xv, g),), ()
    (h,) = _rows_call("norm_in", f_norm_in, nt, [x2, pre_mix_norm], [_rt(tr, D), _full(pre_mix_norm)],
                      [((T, D), BF16, _rt(tr, D))], [], 'tf', deps=[a_token])
    Mo['w_in'] = Mo['w_in'] + a_token[0, 0]
    Vo['w_in'] = Vo['w_in'] + a_token[0, 0]
    _, a_lands = _split_wait("allgather_w_in_a_wait", a_sems, [], a_lands,
                             [h, Mo['w_in'], Vo['w_in']] + [sh[n] for n in later], _ag_copies, 4 * len(a_bufs))
    g_in_a, g_cs, g_cl = _ag_finish("allgather_w_in_a_finish", a_lands)
    b_sems, _, b_lands, b_token = _split_start("allgather_w_in_b_start", [], [sh_b], _ag_copies, 4, after=[g_in_a])
    ag_sems, ag_srcs, ag_lands, ag_token = _split_start(
        "allgather_later_start", [], [sh[n] for n in later], _ag_copies, 4 * len(later), after=[b_token])
    conv_s = jnp.transpose(g_cs, (1, 0, 2)).reshape(CONV_WIDTH, XBC)
    conv_l = jnp.transpose(g_cl, (1, 0, 2)).reshape(CONV_WIDTH, DL)
    wb = DIN // N_DEV
    o_z, o_xbc, o_dt, o_gate, o_xl = 0, DS, DS + XBC, DS + XBC + n_heads, DS + XBC + n_heads + DL
    segs = [(o_xbc, o_xbc + XBC, 0), (o_z, o_z + DS, XBC), (o_gate, o_gate + DL, XBC + DS),
            (o_xl, o_xl + DL, XBC + DS + DL), (o_dt, o_dt + n_heads, NP - LANES)]

    def ref_cols(g, lo, hi):
        out = []
        while lo < hi:
            k = lo // wb
            e = min(hi, (k + 1) * wb)
            out.append(g[k, :, lo - k * wb:e - k * wb])
            lo = e
        return out

    def laid_out(g):
        return jnp.concatenate([p for a, b, _ in segs for p in ref_cols(g, a, b)]
                               + [jnp.zeros((g.shape[1], LANES - n_heads), BF16)], axis=1)

    def my_cols(g, lo, hi):
        out = []
        for a, b, m in sorted(segs):
            s, e = max(lo, a), min(hi, b)
            if s < e:
                out.append(g[:, m + s - a:m + e - a])
        return out

    wp_a = laid_out(g_in_a)
    dt_bias = _pad_lanes(ssd_dt_bias)
    a_log = _pad_lanes(ssd_a_log)
    d_skip = _pad_lanes(ssd_d)
    wa_b, wx_b = lru_w_a[0].astype(BF16), lru_w_x[0].astype(BF16)
    b_a, b_x = lru_b_a.reshape(1, DL), lru_b_x.reshape(1, DL)

    (proj_a,) = _mm("proj_a", h, wp_a, a_cols=(0, Dh), outs=((BF16, None),), deps=[ag_token])
    (dt_a,) = _mm("proj_dt_a", h, wp_a[:, NP - LANES:], a_cols=(0, Dh))
    _, b_lands = _split_wait("allgather_w_in_b_wait", b_sems, [], b_lands, proj_a, _ag_copies, 4)
    (g_in_b,) = _ag_finish("allgather_w_in_b_finish", b_lands)
    wp_b = laid_out(g_in_b)
    add = lambda r, e: r + e.astype(F32)
    (proj,) = _mm("proj_b", h, wp_b, a_cols=(Dh, Dh), extra=proj_a, outs=((BF16, add),))
    (dt_raw,) = _mm("proj_dt_b", h, wp_b[:, NP - LANES:], a_cols=(Dh, Dh), extra=dt_a, outs=((F32, add),))

    cwx = min(1024, XBC)

    def f_ssd_pre(first, last, xbc, halo, w, b):
        pre = _conv_pre(xbc, jnp.where(first, 0.0, halo), w, b)
        return (pre * jax.nn.sigmoid(pre),), ()
    (xbc_act,) = _rows_call(
        "ssd_pre", f_ssd_pre, nt, [proj, proj, conv_s, ssd_conv_b],
        [_rt(tr, XBC), _halo_prev(tr, XBC, rows=PACKED_ROWS), _full(conv_s), _full(ssd_conv_b)],
        [((T, XBC), F32, _rt(tr, XBC))], [], ['t', 'p0', 'f', 'f'], cw=cwx)

    def f_ssd_dt(first, last, dtr, dtb):
        return (jax.nn.softplus(dtr + dtb),), ()
    (dt,) = _rows_call("ssd_dt", f_ssd_dt, nt, [dt_raw, dt_bias], [_rt(tr, LANES), _full(dt_bias)],
                       [((T, LANES), F32, _rt(tr, LANES))], [], 'tf')

    y_ssd, h_prev = _ssd_fwd(xbc_act, dt, a_log, d_skip, n_heads, Q)

    gw = DS // SSD_GROUPS

    def ssd_post(y, z, g):
        yz = y * jax.nn.silu(z)
        parts = []
        for k in range(SSD_GROUPS):
            yk = yz[:, k * gw:(k + 1) * gw]
            parts.append(yk * lax.rsqrt(jnp.mean(yk * yk, axis=-1, keepdims=True) + EPS))
        return jnp.concatenate(parts, axis=-1) * g

    def f_ssd_post(first, last, y, z, g):
        return (ssd_post(y, z, g),), ()
    (mixcat,) = _rows_call("ssd_post", f_ssd_post, nt, [y_ssd, proj, ssd_norm],
                           [_rt(tr, DS), _rt(tr, DS, cb_z), _full(ssd_norm)], [((T, DS + DL), BF16, _rt(tr, DS))], [],
                           'ttf')

    def f_lru_pre(first, last, xv, halo, w, b):
        return (_conv_pre(xv, jnp.where(first, 0.0, halo), w, b),), ()
    (xl,) = _rows_call("lru_pre", f_lru_pre, nt, [proj, proj, conv_l, lru_conv_b],
                       [_rt(tr, DL, cb_xl), _halo_prev(tr, DL, cb_xl, rows=PACKED_ROWS), _full(conv_l),
                        _full(lru_conv_b)],
                       [((T, DL), F32, _rt(tr, DL))], [], ['t', 'p0', 'f', 'f'])

    a_lru, u_lru = _lru_gates_fwd(xl, wa_b, b_a, wx_b, b_x, lru_lambda, tr)
    _, ag_lands = _split_wait("allgather_later_wait", ag_sems, ag_srcs, ag_lands, u_lru, _ag_copies, 4 * len(later))
    f_sems, _, ag_lands, f_token = _split_start("allgather_later_fwd_start", [], ag_lands, _fwd_copies, 3 * len(later))
    h_lru = _lru_scan_fwd(a_lru, u_lru, tr, deps=[f_token])

    def lru_post(hv, gate, g):
        return _rms(hv * jax.nn.gelu(gate), g)

    def f_lru_post(first, last, hv, gate, g):
        return (lru_post(hv, gate, g),), ()
    cb_l = DS // DL
    (mixcat,) = _rows_call("lru_post", f_lru_post, nt, [h_lru, proj, lru_norm],
                           [_rt(tr, DL), _rt(tr, DL, cb_gate), _full(lru_norm)],
                           [((T, DS + DL), BF16, _rt(tr, DL, cb_l))], [], 'ttf', into=mixcat)

    _, (g_out, g_mi, g_mo) = _split_wait("allgather_later_fwd_wait", f_sems, [], ag_lands, mixcat, _fwd_copies,
                                         3 * len(later))
    w_out_f = g_out.reshape(DS + DL, D)
    w_mi_f = jnp.transpose(g_mi, (1, 0, 2)).reshape(D, DFF)
    w_mo_f = g_mo.reshape(DFF, D)
    (mix,) = _mm("mix", mixcat, w_out_f, outs=((BF16, None),))

    def f_post_mix(first, last, xv, mx, gpm, gpl):
        x1 = xv + _rms(mx, gpm)
        return (x1, _rms(x1, gpl)), ()
    x1, hn = _rows_call("post_mix", f_post_mix, nt, [x2, mix, post_mix_norm, pre_mlp_norm],
                        [_rt(tr, D), _rt(tr, D), _full(post_mix_norm), _full(pre_mlp_norm)],
                        [((T, D), F32, _rt(tr, D)), ((T, D), BF16, _rt(tr, D))], [], 'ttff')

    hm, act = _mm("mlp_in", hn, w_mi_f,
                  outs=((BF16, None), (BF16, lambda r, e: jnp.square(jnp.maximum(r, 0.0)))))
    (hm2,) = _mm("mlp_out", act, w_mo_f, outs=((BF16, None),))

    def f_final(first, last, x1v, hm2v, g, tg):
        err = x1v + _rms(hm2v, g) - tg
        dx2 = err * (1.0 / D)
        dh, dg = _rms_bwd(hm2v, g, dx2)
        loss = jnp.full((1, LANES), 0.5 / D, F32) * jnp.sum(err * err)
        return (dx2, dh), (dg, loss)
    dx1a, dhm2, g_post_mlp, loss_part = _rows_call(
        "loss_head", f_final, nt, [x1, hm2, post_mlp_norm, tgt],
        [_rt(tr, D), _rt(tr, D), _full(post_mlp_norm), _rt(tr, D)],
        [((T, D), BF16, _rt(tr, D)), ((T, D), BF16, _rt(tr, D))], [(1, D), (1, LANES)], 'ttft')

    def rs_begin(n, full):
        (from_sib,) = _rs_sibling("rs_sibling_" + n, [full])
        pair = _pair_sum("pair_sum_" + n, full, from_sib, c_idx)
        sems, srcs, lands, token = _split_start("rs_start_" + n, [pair], [lax.empty((3,) + pair.shape[1:], BF16)],
                                                _rs_copies, 3)
        return (sems, srcs, lands), token

    def rs_end(n, state, after):
        (pair,), (recv,) = _split_wait("rs_wait_" + n, *state, after, _rs_copies, 3)
        return pair, recv, chip_idx

    def rs_direct_begin(n, full):
        sems, srcs, lands, token = _split_start("rs_start_" + n, [full],
                                                [lax.empty((N_DEV - 1,) + full.shape[1:], BF16)], _direct_copies,
                                                N_DEV - 1)
        return (sems, srcs, lands), token

    def rs_direct_end(n, state, after):
        (full,), (recv,) = _split_wait("rs_wait_" + n, *state, after, _direct_copies, N_DEV - 1)
        return full, recv, dev_idx

    (gw_mo,) = _mm("dw_mlp_out", act, dhm2, ta=True, outs=((BF16, None),))
    rs_mo, tok = rs_direct_begin('w_mlp_out', gw_mo.reshape(N_DEV, DFF // N_DEV, D))
    (dhm,) = _mm("d_mlp_act", dhm2, w_mo_f, tb=True, extra=hm,
                 outs=((BF16, lambda r, e: r * (2.0 * jnp.maximum(e.astype(F32), 0.0))),), deps=[tok])
    (gw_mi,) = _mm("dw_mlp_in", hn, dhm, ta=True, outs=((BF16, None),), out_blocks=N_DEV)
    rs_mi, tok = rs_direct_begin('w_mlp_in', gw_mi)
    (dhn,) = _mm("d_mlp_in", dhm, w_mi_f, tb=True, outs=((BF16, None),), deps=[tok])

    def f_post_mix_bwd(first, last, x1v, mx, gpm, gpl, dhnv, dxa):
        dx1, dgpl = _rms_bwd(x1v, gpl, dhnv)
        dx1 = dx1 + dxa
        dmx, dgpm = _rms_bwd(mx, gpm, dx1)
        return (dx1, dmx), (dgpl, dgpm)
    dx1, dmix, g_pre_mlp, g_post_mix = _rows_call(
        "post_mix_bwd", f_post_mix_bwd, nt, [x1, mix, post_mix_norm, pre_mlp_norm, dhn, dx1a],
        [_rt(tr, D), _rt(tr, D), _full(post_mix_norm), _full(pre_mlp_norm), _rt(tr, D), _rt(tr, D)],
        [((T, D), BF16, _rt(tr, D)), ((T, D), BF16, _rt(tr, D))], [(1, D), (1, D)], 'ttfftt')

    (gw_out,) = _mm("dw_out", mixcat, dmix, ta=True, outs=((BF16, None),))
    rs_out, tok = rs_direct_begin('w_out', gw_out.reshape(N_DEV, -1, D))
    (dmixcat,) = _mm("d_mix", dmix, w_out_f, tb=True, outs=((BF16, None),), deps=[tok])

    def f_lru_post_bwd(first, last, hv, gate, g, dy):
        gl, dgl = _gelu_and_grad(gate)
        dv, dg = _rms_bwd(hv * gl, g, dy)
        return (dv * hv * dgl, dv * gl), (dg,)
    dproj, dh_lru, g_lru_norm = _rows_call(
        "lru_post_bwd", f_lru_post_bwd, nt, [h_lru, proj, lru_norm, dmixcat],
        [_rt(tr, DL), _rt(tr, DL, cb_gate), _full(lru_norm), _rt(tr, DL, cb_l)],
        [((T, NP), BF16, _rt(tr, DL, cb_gate)), ((T, DL), BF16, _rt(tr, DL))], [(1, DL)], 'ttft')

    du_lru, da_lru = _lru_scan_bwd(a_lru, h_lru, dh_lru, tr)
    dxl, g_wa, g_ba, g_wx, g_bx, g_lam = _lru_gates_bwd(xl, wa_b, b_a, wx_b, b_x, lru_lambda, da_lru, du_lru, tr)

    conv_bwd_kinds = ['t', 'p0', 'n0', 't', 'n3', 'f', 'f']

    def f_lru_pre_bwd(first, last, xv, hp, xn, d, dn, w, b):
        dx, dw8, db = _conv_bwd_tile(first, last, xv, hp, xn, d, dn, w, b, silu=False)
        return (dx,), (dw8, db)
    dproj, g_convl8, g_convl_b = _rows_call(
        "lru_pre_bwd", f_lru_pre_bwd, nt, [proj, proj, proj, dxl, dxl, conv_l, lru_conv_b],
        [_rt(tr, DL, cb_xl), _halo_prev(tr, DL, cb_xl, rows=PACKED_ROWS),
         _halo_next(tr, DL, nt, cb_xl, rows=PACKED_ROWS), _rt(tr, DL),
         _halo_next(tr, DL, nt, rows=PACKED_ROWS), _full(conv_l), _full(lru_conv_b)],
        [((T, NP), BF16, _rt(tr, DL, cb_xl))], [(SUBLANES, DL), (1, DL)], conv_bwd_kinds, into=dproj)

    def f_ssd_post_bwd(first, last, y, z, g, dy):
        s = jax.nn.sigmoid(z)
        sz = z * s
        yz = y * sz
        gdy = dy * g
        dyz, yn = [], []
        for k in range(SSD_GROUPS):
            cols = slice(k * gw, (k + 1) * gw)
            r = lax.rsqrt(jnp.mean(yz[:, cols] * yz[:, cols], axis=-1, keepdims=True) + EPS)
            xr = yz[:, cols] * r
            dyz.append((gdy[:, cols] - xr * jnp.mean(gdy[:, cols] * xr, axis=-1, keepdims=True)) * r)
            yn.append(xr)
        dyz = jnp.concatenate(dyz, axis=-1)
        dz = dyz * y * (s + sz * (1.0 - s))
        return (dz, dyz * sz), (_colsum(dy * jnp.concatenate(yn, axis=-1)),)
    dproj, dy_ssd, g_ssd_norm = _rows_call(
        "ssd_post_bwd", f_ssd_post_bwd, nt, [y_ssd, proj, ssd_norm, dmixcat],
        [_rt(tr, DS), _rt(tr, DS, cb_z), _full(ssd_norm), _rt(tr, DS, 0)],
        [((T, NP), BF16, _rt(tr, DS, cb_z)), ((T, DS), BF16, _rt(tr, DS))], [(1, DS)], 'ttft', into=dproj)

    dxbc_act, ddt, g_alog, g_dskip = _ssd_bwd(xbc_act, dt, a_log, d_skip, h_prev, dy_ssd, n_heads, Q)

    def f_ssd_pre_bwd(first, last, xv, hp, xn, d, dn, w, b):
        dx, dw8, db = _conv_bwd_tile(first, last, xv, hp, xn, d, dn, w, b, silu=True)
        return (dx,), (dw8, db)
    dproj, g_convs8, g_convs_b = _rows_call(
        "ssd_pre_bwd", f_ssd_pre_bwd, nt, [proj, proj, proj, dxbc_act, dxbc_act, conv_s, ssd_conv_b],
        [_rt(tr, XBC), _halo_prev(tr, XBC, rows=PACKED_ROWS), _halo_next(tr, XBC, nt, rows=PACKED_ROWS),
         _rt(tr, XBC), _halo_next(tr, XBC, nt),
         _full(conv_s), _full(ssd_conv_b)],
        [((T, NP), BF16, _rt(tr, XBC))], [(SUBLANES, XBC), (1, XBC)], conv_bwd_kinds, into=dproj, cw=cwx)

    def f_ssd_dt_bwd(first, last, ddtv, dtr, dtb):
        ddtr = ddtv * jax.nn.sigmoid(dtr + dtb)
        return (ddtr,), (_colsum(ddtr),)
    dproj, g_dtb = _rows_call(
        "ssd_dt_bwd", f_ssd_dt_bwd, nt, [ddt, dt_raw, dt_bias],
        [_rt(tr, LANES), _rt(tr, LANES), _full(dt_bias)],
        [((T, NP), BF16, _rt(tr, LANES, cb_dt))], [(1, LANES)], 'ttf', into=dproj)
    small = {
        'ssd_conv_w': g_convs8[:CONV_WIDTH], 'ssd_conv_b': g_convs_b,
        'ssd_dt_bias': g_dtb[:, :n_heads], 'ssd_a_log': g_alog[:, :n_heads], 'ssd_d': g_dskip[:, :n_heads],
        'ssd_norm': g_ssd_norm, 'lru_conv_w': g_convl8[:CONV_WIDTH], 'lru_conv_b': g_convl_b,
        'lru_w_a': g_wa, 'lru_b_a': g_ba, 'lru_w_x': g_wx, 'lru_b_x': g_bx, 'lru_lambda': g_lam,
        'lru_norm': g_lru_norm, 'post_mix_norm': g_post_mix, 'pre_mlp_norm': g_pre_mlp,
        'post_mlp_norm': g_post_mlp, 'loss': loss_part[:, :1],
    }
    wide = ['lru_w_a', 'lru_w_x']
    narrow = [n for n in small if n not in wide]
    lb = lru_w_a.shape[-1]
    s_srcs = [_flat_rows([small[n] for n in narrow]), g_wa.reshape(-1, lb), g_wx.reshape(-1, lb)]
    s_sems, s_srcs, s_lands, tok = _split_start(
        "small_grads_start", [], [_own_block(a, dev) for a in s_srcs], _ag_copies, 4 * len(s_srcs))

    (gwp,) = _mm("dw_proj", h, dproj, ta=True, outs=((BF16, None),), deps=[tok])
    rs_in, tok = rs_begin(
        'w_in', jnp.stack([jnp.concatenate(my_cols(gwp, k * wb, (k + 1) * wb), axis=1) for k in range(N_DEV)]))
    (dh_a,) = _mm("d_proj_a", dproj, wp_a, tb=True, outs=((BF16, None),), deps=[tok])
    (dh_b,) = _mm("d_proj_b", dproj, wp_b, tb=True, outs=((BF16, None),), deps=[tok])

    def f_norm_in_bwd(first, last, xv, g, dha, dhb, dxa):
        dx, dg = _rms_bwd(xv, g, jnp.concatenate([dha, dhb], axis=1))
        return (dx + dxa,), (dg,)
    grad_x, g_pre_mix = _rows_call(
        "norm_in_bwd", f_norm_in_bwd, nt, [x2, pre_mix_norm, dh_a, dh_b, dx1],
        [_rt(tr, D), _full(pre_mix_norm), _rt(tr, Dh), _rt(tr, Dh), _rt(tr, D)], [((T, D), F32, _rt(tr, D))],
        [(1, D)], 'tfttt')

    big_out = {}
    for n, state in (('w_mlp_out', rs_mo), ('w_mlp_in', rs_mi), ('w_out', rs_out)):
        big_out[n] = _adamw_big("adamw_" + n, W[n], Mo[n], Vo[n], *rs_direct_end(n, state, grad_x))

    (g_pm8,) = _allgather("allgather_pre_mix_grad", [_own_block(g_pre_mix, dev)], deps=[big_out['w_out'][0]])
    _, s_lands = _split_wait("small_grads_wait", s_sems, s_srcs, s_lands, g_pm8, _ag_copies, 4 * len(s_lands))
    g_narrow, g_wa8, g_wx8 = _ag_finish("small_grads_finish", s_lands)
    summed = dict(zip(narrow, _unflat(_sum8("sum_small_grads", g_narrow), [small[n].shape for n in narrow])))
    summed['pre_mix_norm'] = _sum8("sum_pre_mix_grad", g_pm8)
    summed['lru_w_a'] = _sum8("sum_lru_w_a_grads", g_wa8)
    summed['lru_w_x'] = _sum8("sum_lru_w_x_grads", g_wx8)
    loss = summed.pop('loss').reshape(())
    for n, full_w in (('ssd_conv_w', XBC), ('lru_conv_w', DL)):
        wdt = full_w // N_DEV
        summed[n] = lax.dynamic_slice_in_dim(summed[n], dev * wdt, wdt, axis=1)
    small_params = [n for n in names if n not in big]
    as2d = lambda a: a.reshape(-1, a.shape[-1])
    res = _adamw_small([as2d(W[n]) for n in small_params],
                       [summed[n].reshape(as2d(W[n]).shape) for n in small_params],
                       [as2d(Mo[n]) for n in small_params], [as2d(Vo[n]) for n in small_params])
    grads = {n: summed[n].reshape(W[n].shape) for n in small_params}
    delta, new_m, new_v = ({n: r.reshape(W[n].shape) for n, r in zip(small_params, rs)} for rs in res)

    big_out['w_in'] = _adamw_big("adamw_w_in", W['w_in'], Mo['w_in'], Vo['w_in'], *rs_end('w_in', rs_in, g_pm8))
    for n in big:
        grads[n], delta[n], new_m[n], new_v[n] = big_out[n]

    return (loss, grad_x.reshape(x.shape), *[grads[n] for n in names], *[delta[n] for n in names],
            *[new_m[n] for n in names], *[new_v[n] for n in names])
```

```python
import functools

import jax
import jax.numpy as jnp
from jax import lax
from jax.experimental import pallas as pl
from jax.experimental.pallas import tpu as pltpu

F32, BF16 = jnp.float32, jnp.bfloat16
S = jax.ShapeDtypeStruct
MESH = pl.DeviceIdType.MESH

SSD_GROUPS = 8
LRU_C = 8.0
EPS = 1e-6
CONV_WIDTH = 4
ADAM_LR, ADAM_B1, ADAM_B2, ADAM_EPS, ADAM_WD, ADAM_STEP = 0.001, 0.9, 0.999, 1e-08, 0.01, 10

LANES = 128
SUBLANES = 8
VMEM_LIMIT = 56 * 1024 * 1024
N_DEV = 8
SMALL_W = 512
HI = lax.Precision.HIGHEST


def _pcall(body, **kw):
    return pl.pallas_call(body, **kw)


def _cparams(sem=None, **kw):
    return pltpu.CompilerParams(dimension_semantics=sem, vmem_limit_bytes=VMEM_LIMIT, **kw)


def _pick(n, cands):
    for c in cands:
        if c <= n and n % c == 0:
            return c
    return n


def _rt(tr, w, cb=0, n=None):
    if n is None:
        return pl.BlockSpec((tr, w), lambda i: (i, cb))
    return pl.BlockSpec((tr, w), lambda i: (n - 1 - i, cb))


PACKED_ROWS = 16


def _halo_prev(tr, w, cb=0, n=None, rows=SUBLANES):
    k = tr // rows
    if n is None:
        return pl.BlockSpec((rows, w), lambda i: (jnp.maximum(i * k - 1, 0), cb))
    return pl.BlockSpec((rows, w), lambda i: (jnp.maximum((n - 1 - i) * k - 1, 0), cb))


def _halo_next(tr, w, nt, cb=0, n=None, rows=SUBLANES):
    k = tr // rows
    last = nt * k - 1
    if n is None:
        return pl.BlockSpec((rows, w), lambda i: (jnp.minimum((i + 1) * k, last), cb))
    return pl.BlockSpec((rows, w), lambda i: (jnp.minimum((n - i) * k, last), cb))


def _full(a):
    nd = a.ndim
    return pl.BlockSpec(a.shape, lambda i: (0,) * nd)


def _rows_call(name, fn, n_tiles, arrays, in_specs, out_tiled, out_acc, kinds, into=None, deps=(), cw=None):
    n_in, n_t = len(arrays), len(out_tiled)
    n_skip = len(deps) + (0 if into is None else 1)
    width = in_specs[kinds.index('t')].block_shape[1]
    cols = [(0, width)] if cw is None else [(c, cw) for c in range(0, width, cw)]

    def body(*refs):
        i = pl.program_id(0)
        ins = refs[:n_in]
        outs = refs[n_in + n_skip:n_in + n_skip + n_t]
        accs = refs[n_in + n_skip + n_t:]
        if accs:
            @pl.when(i == 0)
            def _():
                for r in accs:
                    r[...] = jnp.zeros_like(r)

        def lanes(ref, rows, c0, w):
            return ref[rows, c0:c0 + w] if ref.shape[-1] == width else ref[rows, :]

        def load(k, c0, w):
            v = lanes(ins[k], slice(None), c0, w).astype(F32)
            if kinds[k][0] in 'pn' and v.shape[0] == PACKED_ROWS:
                v = v[SUBLANES:] if kinds[k][0] == 'p' else v[:SUBLANES]
            return v

        for c0, w in cols:
            touts, aouts = fn(i == 0, i == n_tiles - 1, *[load(k, c0, w) for k in range(n_in)])
            for r, v in zip(outs, touts):
                if r.shape[-1] == width:
                    r[:, c0:c0 + w] = v.astype(r.dtype)
                else:
                    r[...] = v.astype(r.dtype)
            for r, v in zip(accs, aouts):
                if r.shape[-1] == width:
                    r[:, c0:c0 + w] += v
                else:
                    r[...] += v

    out_shape = [S(sh, dt) for sh, dt, _ in out_tiled] + [S(sh, F32) for sh in out_acc]
    out_specs = [sp for _, _, sp in out_tiled]
    for sh in out_acc:
        out_specs.append(pl.BlockSpec(sh, lambda i, nd=len(sh): (0,) * nd))
    in_specs = list(in_specs) + [_ANY] * len(deps)
    if into is None:
        return _pcall(body, name=name, grid=(n_tiles,), in_specs=in_specs, out_specs=out_specs,
                      out_shape=out_shape, compiler_params=_cparams(("arbitrary",)))(*arrays, *deps)
    return _pcall(body, name=name, grid=(n_tiles,), in_specs=in_specs + [_ANY], out_specs=out_specs,
                  out_shape=out_shape, input_output_aliases={n_in + len(deps): 0},
                  compiler_params=_cparams(("arbitrary",)))(*arrays, *deps, into)


def _rms(x, g):
    return x * lax.rsqrt(jnp.mean(x * x, axis=-1, keepdims=True) + EPS) * g


def _rms_bwd(x, g, dy):
    r = lax.rsqrt(jnp.mean(x * x, axis=-1, keepdims=True) + EPS)
    xr = x * r
    gdy = dy * g
    dx = (gdy - xr * jnp.mean(gdy * xr, axis=-1, keepdims=True)) * r
    return dx, _colsum(dy * xr)


_GELU_C0, _GELU_C1 = 0.7978845608028654, 0.044715


def _gelu_and_grad(x):
    x2 = x * x
    t = jnp.tanh(_GELU_C0 * x * (1.0 + _GELU_C1 * x2))
    half = 0.5 * (1.0 + t)
    grad = half + (0.5 * _GELU_C0) * x * (1.0 - t * t) * (1.0 + (3.0 * _GELU_C1) * x2)
    return x * half, grad


def _colsum(v):
    return jnp.sum(v, axis=0, keepdims=True)


_TILES = (1152, 1024, 896, 768, 640, 512, 384, 256, 128)
_K_TILES = (4096, 3456, 3072, 2688, 2048, 1536, 1344, 1152, 1024, 896, 768, 640, 512, 384, 256, 128)


def _mm(name, a, b, *, ta=False, tb=False, outs=((F32, None),), extra=None, out_blocks=None, tm=None, tn=None, tk=None,
        deps=(), a_cols=None):
    M, K = (a.shape[1], a.shape[0]) if ta else a.shape
    if a_cols is not None:
        assert not ta
        K = a_cols[1]
    b3 = b.ndim == 3
    if b3:
        nb_b, brows, bcols = b.shape
        N = brows if tb else nb_b * bcols
    else:
        N = b.shape[0] if tb else b.shape[1]
    n_lim = N if out_blocks is None else N // out_blocks
    if b3 and not tb:
        n_lim = min(n_lim, bcols)
    tm = tm or _pick(M, _TILES[1:])
    tn = tn or _pick(n_lim, _TILES)
    tk = tk or _pick(bcols if (b3 and tb) else K, _K_TILES)
    nk = K // tk
    assert M % tm == 0 and N % tn == 0 and K % tk == 0
    dn = (((0 if ta else 1,), (1 if tb else 0,)), ((), ()))
    n_extra = 0 if extra is None else 1
    n_out = len(outs)

    def body(*refs):
        a_ref, b_ref = refs[0], refs[1]
        e_ref = refs[2] if n_extra else None
        o_refs = refs[2 + n_extra + len(deps):2 + n_extra + len(deps) + n_out]

        def finish(r):
            e = e_ref[...] if n_extra else None
            for o, (_, f) in zip(o_refs, outs):
                o[...] = (r if f is None else f(r, e)).astype(o.dtype)

        part = lax.dot_general(a_ref[...], b_ref[...], dn, preferred_element_type=F32)
        if nk == 1:
            finish(part)
            return
        acc = refs[-1]
        k = pl.program_id(2)

        @pl.when(k == 0)
        def _():
            acc[...] = part

        @pl.when(jnp.logical_and(k > 0, k < nk - 1))
        def _():
            acc[...] += part

        @pl.when(k == nk - 1)
        def _():
            finish(acc[...] + part)

    k0 = 0 if a_cols is None else a_cols[0] // tk
    a_spec = (pl.BlockSpec((tk, tm), lambda i, j, k: (k, i)) if ta
              else pl.BlockSpec((tm, tk), lambda i, j, k: (i, k + k0)))
    if not b3:
        b_spec = pl.BlockSpec((tn, tk), lambda i, j, k: (j, k)) if tb else pl.BlockSpec((tk, tn), lambda i, j, k: (k, j))
    elif tb:
        per = bcols // tk
        b_spec = pl.BlockSpec((None, tn, tk), lambda i, j, k: (k // per, j, k % per))
    else:
        per = bcols // tn
        b_spec = pl.BlockSpec((None, tk, tn), lambda i, j, k: (j // per, k, j % per))
    o_spec = pl.BlockSpec((tm, tn), lambda i, j, k: (i, j))
    if out_blocks is None:
        out_specs, out_shape = [o_spec] * n_out, [S((M, N), dt) for dt, _ in outs]
    else:
        per_o = N // out_blocks // tn
        ob_spec = pl.BlockSpec((None, tm, tn), lambda i, j, k: (j // per_o, i, j % per_o))
        out_specs, out_shape = [ob_spec] * n_out, [S((out_blocks, M, N // out_blocks), dt) for dt, _ in outs]
    in_specs = [a_spec, b_spec] + ([o_spec] if n_extra else []) + [_ANY] * len(deps)
    args = [a, b] + ([extra] if n_extra else []) + list(deps)
    return _pcall(body, name=name, grid=(M // tm, N // tn, nk), in_specs=in_specs, out_specs=out_specs,
                  out_shape=out_shape, scratch_shapes=[pltpu.VMEM((tm, tn), F32)] if nk > 1 else [],
                  compiler_params=_cparams(("parallel", "parallel", "arbitrary")))(*args)


def _shift_down(x, halo, s):
    if s == 0:
        return x
    r = pltpu.roll(x, s, 0)
    hr = pltpu.roll(halo, s, 0)
    row = lax.broadcasted_iota(jnp.int32, halo.shape, 0)
    top = jnp.where(row < s, hr, r[:SUBLANES])
    if x.shape[0] == SUBLANES:
        return top
    return jnp.concatenate([top, r[SUBLANES:]], axis=0)


def _shift_up(x, nxt, s):
    if s == 0:
        return x
    n = x.shape[0]
    r = pltpu.roll(x, n - s, 0)
    nr = pltpu.roll(nxt, SUBLANES - s, 0)
    row = lax.broadcasted_iota(jnp.int32, nxt.shape, 0)
    bot = jnp.where(row >= SUBLANES - s, nr, r[n - SUBLANES:])
    if n == SUBLANES:
        return bot
    return jnp.concatenate([r[:n - SUBLANES], bot], axis=0)


def _conv_taps(x, halo):
    return [_shift_down(x, halo, CONV_WIDTH - 1 - k) for k in range(CONV_WIDTH)]


def _conv_pre(x, halo, w, b, taps=None):
    taps = _conv_taps(x, halo) if taps is None else taps
    acc = b + w[0:1, :] * taps[0]
    for k in range(1, CONV_WIDTH):
        acc = acc + w[k:k + 1, :] * taps[k]
    return acc


def _silu_grad(p):
    s = jax.nn.sigmoid(p)
    return s * (1.0 + p * (1.0 - s))


def _conv_bwd_tile(first, last, x, hprev, xnext, d, dnext, w, b, silu):
    hprev = jnp.where(first, 0.0, hprev)
    taps = _conv_taps(x, hprev)
    if silu:
        d = d * _silu_grad(_conv_pre(x, hprev, w, b, taps))
        pre_next = _conv_pre(xnext, x[x.shape[0] - SUBLANES:], w, b)
        dnext = dnext * _silu_grad(pre_next)
    dnext = jnp.where(last, 0.0, dnext)
    dx = w[CONV_WIDTH - 1:CONV_WIDTH, :] * d
    row8 = lax.broadcasted_iota(jnp.int32, (SUBLANES, x.shape[1]), 0)
    dw8 = jnp.where(row8 == CONV_WIDTH - 1, _colsum(d * x), 0.0)
    for k in range(CONV_WIDTH - 1):
        dx = dx + w[k:k + 1, :] * _shift_up(d, dnext, CONV_WIDTH - 1 - k)
        dw8 = dw8 + jnp.where(row8 == k, _colsum(d * taps[k]), 0.0)
    return dx, dw8, _colsum(d)


def _ssd_dims(xbc_act, n_heads):
    T, XBC = xbc_act.shape
    GN = XBC // 4
    DS = XBC - 2 * GN
    G = SSD_GROUPS
    N = GN // G
    P = DS // n_heads
    K = n_heads // G
    return T, XBC, DS, GN, G, N, P, K


def _ssd_common(dt, alog, Q):
    a = -jnp.exp(alog)
    adt = dt * a
    li = lax.broadcasted_iota(jnp.int32, (Q, Q), 0)
    si = lax.broadcasted_iota(jnp.int32, (Q, Q), 1)
    causal = li >= si
    ltri = causal.astype(F32)
    acs = jnp.dot(ltri, adt, precision=HI, preferred_element_type=F32)
    acs_row = lax.dot_general(adt, ltri, (((0,), (1,)), ((), ())), precision=HI,
                              preferred_element_type=F32)
    return a, adt, causal, ltri, acs, acs_row


def _expander(g, K, P, W):
    r = lax.broadcasted_iota(jnp.int32, (LANES, W), 0)
    c = lax.broadcasted_iota(jnp.int32, (LANES, W), 1)
    return (c // P + g * K == r).astype(F32)


def _dotb(a, b, dn=(((1,), (0,)), ((), ()))):
    return lax.dot_general(a.astype(BF16), b.astype(BF16), dn, preferred_element_type=F32)


def _dot_split(a, sel, terms, dn=(((1,), (0,)), ((), ()))):
    selb = sel.astype(BF16)
    out = None
    for _ in range(terms):
        piece = a.astype(BF16)
        part = lax.dot_general(piece, selb, dn, preferred_element_type=F32)
        out = part if out is None else out + part
        a = a - piece.astype(F32)
    return out


_NT = (((1,), (1,)), ((), ()))
_TN = (((0,), (0,)), ((), ()))


def _ssd_fwd(xbc_act, dt, alog, dskip, n_heads, Q):
    T, XBC, DS, GN, G, N, P, K = _ssd_dims(xbc_act, n_heads)
    W = K * P
    nc = T // Q

    def body(xs_ref, b_ref, c_ref, dt_ref, alog_ref, d_ref, y_ref, hp_ref, h_scr):
        ci = pl.program_id(0)

        @pl.when(ci == 0)
        def _():
            h_scr[...] = jnp.zeros_like(h_scr)

        dtv = dt_ref[...]
        a, adt, causal, ltri, acs, acs_row = _ssd_common(dtv, alog_ref[...], Q)
        lane_head = lax.broadcasted_iota(jnp.int32, (Q, W), 1) // P
        for g in range(G):
            eg = _expander(g, K, P, W)
            dtb = _dot_split(dtv, eg, 3)
            acsb = _dot_split(acs, eg, 3)
            lastb = acsb[Q - 1:Q, :]
            db = _dot_split(jnp.broadcast_to(d_ref[...], (SUBLANES, LANES)), eg, 3)[0:1, :]
            xg = xs_ref[:, g * W:(g + 1) * W]
            bg = b_ref[:, g * N:(g + 1) * N]
            cg = c_ref[:, g * N:(g + 1) * N]
            xt = xg * dtb
            sc = _dotb(cg, bg, _NT)
            yd = jnp.zeros((Q, W), F32)
            for k in range(K):
                h = g * K + k
                seg = acs[:, h:h + 1] - acs_row[h:h + 1, :]
                lh = jnp.where(causal, jnp.exp(jnp.minimum(seg, 0.0)), 0.0)
                xk = jnp.where(lane_head == k, xt, 0.0)
                yd = yd + _dotb(sc * lh, xk)
            hp = h_scr[g]
            yoff = _dotb(cg, hp) * jnp.exp(acsb)
            y_ref[:, g * W:(g + 1) * W] = yd + yoff + xg * db
            e_end = jnp.exp(lastb - acsb)
            st = _dotb(bg, xt * e_end, _TN)
            hp_ref[0, g] = hp
            h_scr[g] = jnp.exp(lastb) * hp + st

    cb = DS // GN
    in_specs = [pl.BlockSpec((Q, DS), lambda c: (c, 0)),
                pl.BlockSpec((Q, GN), lambda c: (c, cb)),
                pl.BlockSpec((Q, GN), lambda c: (c, cb + 1)),
                pl.BlockSpec((Q, LANES), lambda c: (c, 0)),
                pl.BlockSpec((1, LANES), lambda c: (0, 0)),
                pl.BlockSpec((1, LANES), lambda c: (0, 0))]
    out_specs = [pl.BlockSpec((Q, DS), lambda c: (c, 0)),
                 pl.BlockSpec((1, G, N, W), lambda c: (c, 0, 0, 0))]
    return _pcall(body, name="ssd_fwd", grid=(nc,), in_specs=in_specs, out_specs=out_specs,
                  out_shape=[S((T, DS), F32), S((nc, G, N, W), F32)],
                  scratch_shapes=[pltpu.VMEM((G, N, W), F32)],
                  compiler_params=_cparams(("arbitrary",)))(xbc_act, xbc_act, xbc_act, dt, alog, dskip)


def _ssd_bwd(xbc_act, dt, alog, dskip, hprev, dy, n_heads, Q):
    T, XBC, DS, GN, G, N, P, K = _ssd_dims(xbc_act, n_heads)
    W = K * P
    nc = T // Q

    def body(xs_ref, b_ref, c_ref, dt_ref, alog_ref, d_ref, hp_ref, dy_ref,
             dxbc_ref, ddt_ref, dalog_ref, dd_ref, dh_scr):
        ci = pl.program_id(0)

        @pl.when(ci == 0)
        def _():
            dh_scr[...] = jnp.zeros_like(dh_scr)
            dalog_ref[...] = jnp.zeros_like(dalog_ref)
            dd_ref[...] = jnp.zeros_like(dd_ref)

        dtv = dt_ref[...]
        a, adt, causal, ltri, acs, acs_row = _ssd_common(dtv, alog_ref[...], Q)
        lane_head = lax.broadcasted_iota(jnp.int32, (Q, W), 1) // P
        lane128 = lax.broadcasted_iota(jnp.int32, (Q, LANES), 1)
        sub128 = lax.broadcasted_iota(jnp.int32, (LANES, Q), 0)
        rowq = lax.broadcasted_iota(jnp.int32, (Q, W), 0)
        dacs = jnp.zeros((Q, LANES), F32)
        dacs_row = jnp.zeros((LANES, Q), F32)
        ddt = jnp.zeros((Q, LANES), F32)
        dd_acc = jnp.zeros((1, LANES), F32)
        for g in range(G):
            eg = _expander(g, K, P, W)
            dtb = _dot_split(dtv, eg, 3)
            acsb = _dot_split(acs, eg, 3)
            lastb = acsb[Q - 1:Q, :]
            db = _dot_split(jnp.broadcast_to(d_ref[...], (SUBLANES, LANES)), eg, 3)[0:1, :]
            xg = xs_ref[:, g * W:(g + 1) * W]
            bg = b_ref[:, g * N:(g + 1) * N]
            cg = c_ref[:, g * N:(g + 1) * N]
            dyg = dy_ref[:, g * W:(g + 1) * W].astype(F32)
            hp = hp_ref[0, g]
            dhn = dh_scr[g]
            xt = xg * dtb
            sc = _dotb(cg, bg, _NT)
            eacs = jnp.exp(acsb)
            e_end = jnp.exp(lastb - acsb)
            elast = jnp.exp(lastb)

            wv = dyg * eacs
            dcg = _dotb(wv, hp, _NT)
            dhp = _dotb(cg, wv, _TN) + elast * dhn
            dacsb = dyg * (_dotb(cg, hp) * eacs)

            xe = xt * e_end
            dbg = _dotb(xe, dhn, _NT)
            v = _dotb(bg, dhn)
            dxt = v * e_end
            de = v * xe
            dacsb = dacsb - de
            dlastb = _colsum(de) + elast * jnp.sum(dhn * hp, axis=0, keepdims=True)

            dsc = jnp.zeros((Q, Q), F32)
            for k in range(K):
                h = g * K + k
                seg = acs[:, h:h + 1] - acs_row[h:h + 1, :]
                lh = jnp.where(causal, jnp.exp(jnp.minimum(seg, 0.0)), 0.0)
                mh = sc * lh
                dyk = jnp.where(lane_head == k, dyg, 0.0)
                dxt = dxt + jnp.where(lane_head == k, _dotb(mh, dyg, _TN), 0.0)
                dm = _dotb(dyk, xt, _NT)
                dsc = dsc + dm * lh
                gm = dm * mh
                dacs = dacs + jnp.where(lane128 == h, jnp.sum(gm, axis=1, keepdims=True), 0.0)
                dacs_row = dacs_row - jnp.where(sub128 == h, jnp.sum(gm, axis=0, keepdims=True), 0.0)
            dcg = dcg + _dotb(dsc, bg)
            dbg = dbg + _dotb(dsc, cg, _TN)

            dacsb = dacsb + jnp.where(rowq == Q - 1, dlastb, 0.0)
            dacs = dacs + _dot_split(dacsb, eg, 2, _NT)
            ddt = ddt + _dot_split(dxt * xg, eg, 2, _NT)
            dd_acc = dd_acc + _dot_split(jnp.broadcast_to(_colsum(dyg * xg), (SUBLANES, W)), eg, 2, _NT)[0:1, :]
            dxbc_ref[:, g * W:(g + 1) * W] = dxt * dtb + dyg * db
            dxbc_ref[:, DS + g * N:DS + (g + 1) * N] = dbg
            dxbc_ref[:, DS + GN + g * N:DS + GN + (g + 1) * N] = dcg
            dh_scr[g] = dhp

        eye = (lax.broadcasted_iota(jnp.int32, (LANES, LANES), 0) ==
               lax.broadcasted_iota(jnp.int32, (LANES, LANES), 1)).astype(F32)
        dacs = dacs + lax.dot_general(dacs_row, eye, _TN, precision=HI, preferred_element_type=F32)
        dadt = lax.dot_general(ltri, dacs, _TN, precision=HI, preferred_element_type=F32)
        ddt_ref[...] = ddt + dadt * a
        dalog_ref[...] += _colsum(dadt * dtv) * a
        dd_ref[...] += dd_acc

    cb = DS // GN
    rv = lambda c: nc - 1 - c
    in_specs = [pl.BlockSpec((Q, DS), lambda c: (rv(c), 0)),
                pl.BlockSpec((Q, GN), lambda c: (rv(c), cb)),
                pl.BlockSpec((Q, GN), lambda c: (rv(c), cb + 1)),
                pl.BlockSpec((Q, LANES), lambda c: (rv(c), 0)),
                pl.BlockSpec((1, LANES), lambda c: (0, 0)),
                pl.BlockSpec((1, LANES), lambda c: (0, 0)),
                pl.BlockSpec((1, G, N, W), lambda c: (rv(c), 0, 0, 0)),
                pl.BlockSpec((Q, DS), lambda c: (rv(c), 0))]
    out_specs = [pl.BlockSpec((Q, XBC), lambda c: (rv(c), 0)),
                 pl.BlockSpec((Q, LANES), lambda c: (rv(c), 0)),
                 pl.BlockSpec((1, LANES), lambda c: (0, 0)),
                 pl.BlockSpec((1, LANES), lambda c: (0, 0))]
    return _pcall(body, name="ssd_bwd", grid=(nc,), in_specs=in_specs, out_specs=out_specs,
                  out_shape=[S((T, XBC), F32), S((T, LANES), F32), S((1, LANES), F32), S((1, LANES), F32)],
                  scratch_shapes=[pltpu.VMEM((G, N, W), F32)],
                  compiler_params=_cparams(("arbitrary",)))(
                      xbc_act, xbc_act, xbc_act, dt, alog, dskip, hprev, dy)


def _blockdiag(x, w_ref, dn=(((1,), (0,)), ((), ()))):
    H, B, _ = w_ref.shape
    return jnp.concatenate([_dotb(x[:, h * B:(h + 1) * B], w_ref[h], dn) for h in range(H)], axis=1)


def _lru_elem(xl, r_pre, i_pre, lam):
    r = jax.nn.sigmoid(r_pre)
    i = jax.nn.sigmoid(i_pre)
    log_a = -LRU_C * r * jax.nn.softplus(-lam)
    a = jnp.exp(log_a)
    u = jnp.sqrt(1.0 - jnp.exp(2.0 * log_a)) * (i * xl)
    return a, u


def _lru_elem_bwd(xl, r_pre, i_pre, lam, da, du):
    r = jax.nn.sigmoid(r_pre)
    i = jax.nn.sigmoid(i_pre)
    sp = jax.nn.softplus(-lam)
    a = jnp.exp(-LRU_C * r * sp)
    s = jnp.sqrt(1.0 - a * a)
    d_ix = du * s
    dlog_a = (da - du * (i * xl) * a / s) * a
    dr_pre = dlog_a * (-LRU_C) * sp * r * (1.0 - r)
    dlam = _colsum(dlog_a * r) * (LRU_C * jax.nn.sigmoid(-lam))
    di_pre = d_ix * xl * i * (1.0 - i)
    return d_ix * i, dr_pre, di_pre, dlam


def _lru_gates_fwd(xl, w_a, b_a, w_x, b_x, lam, tr):
    T, DL = xl.shape

    def body(xl_ref, wa_ref, ba_ref, wx_ref, bx_ref, lam_ref, a_ref, u_ref):
        x = xl_ref[...]
        r_pre = _blockdiag(x, wa_ref) + ba_ref[...]
        i_pre = _blockdiag(x, wx_ref) + bx_ref[...]
        a, u = _lru_elem(x, r_pre, i_pre, lam_ref[...])
        a_ref[...] = a
        u_ref[...] = u

    w3 = pl.BlockSpec(w_a.shape, lambda i: (0, 0, 0))
    vec = pl.BlockSpec((1, DL), lambda i: (0, 0))
    return _pcall(body, name="lru_gates_fwd", grid=(T // tr,),
                  in_specs=[_rt(tr, DL), w3, vec, w3, vec, vec],
                  out_specs=[_rt(tr, DL), _rt(tr, DL)], out_shape=[S((T, DL), F32), S((T, DL), F32)],
                  compiler_params=_cparams(("parallel",)))(xl, w_a, b_a, w_x, b_x, lam)


def _lru_gates_bwd(xl, w_a, b_a, w_x, b_x, lam, da, du, tr):
    T, DL = xl.shape
    H, B, _ = w_a.shape

    def body(xl_ref, wa_ref, ba_ref, wx_ref, bx_ref, lam_ref, da_ref, du_ref,
             dxl_ref, dwa_ref, dba_ref, dwx_ref, dbx_ref, dlam_ref):
        @pl.when(pl.program_id(0) == 0)
        def _():
            for r in (dwa_ref, dba_ref, dwx_ref, dbx_ref, dlam_ref):
                r[...] = jnp.zeros_like(r)

        x = xl_ref[...]
        r_pre = _blockdiag(x, wa_ref) + ba_ref[...]
        i_pre = _blockdiag(x, wx_ref) + bx_ref[...]
        dx, dr, di, dlam = _lru_elem_bwd(x, r_pre, i_pre, lam_ref[...], da_ref[...], du_ref[...])
        dxl_ref[...] = (dx + _blockdiag(dr, wa_ref, _NT) + _blockdiag(di, wx_ref, _NT)).astype(dxl_ref.dtype)
        for h in range(H):
            xh = x[:, h * B:(h + 1) * B]
            dwa_ref[h] += _dotb(xh, dr[:, h * B:(h + 1) * B], _TN)
            dwx_ref[h] += _dotb(xh, di[:, h * B:(h + 1) * B], _TN)
        dba_ref[...] += _colsum(dr)
        dbx_ref[...] += _colsum(di)
        dlam_ref[...] += dlam

    w3 = pl.BlockSpec(w_a.shape, lambda i: (0, 0, 0))
    vec = pl.BlockSpec((1, DL), lambda i: (0, 0))
    return _pcall(body, name="lru_gates_bwd", grid=(T // tr,),
                  in_specs=[_rt(tr, DL), w3, vec, w3, vec, vec, _rt(tr, DL), _rt(tr, DL)],
                  out_specs=[_rt(tr, DL), w3, vec, w3, vec, vec],
                  out_shape=[S((T, DL), BF16), S(w_a.shape, F32), S((1, DL), F32), S(w_a.shape, F32),
                             S((1, DL), F32), S((1, DL), F32)],
                  compiler_params=_cparams(("arbitrary",)))(xl, w_a, b_a, w_x, b_x, lam, da, du)


def _groups(v):
    return v.reshape(v.shape[0] // SUBLANES, SUBLANES, v.shape[1])


def _rows_shifted(v, edge, up):
    sub = lax.broadcasted_iota(jnp.int32, v.shape, 1)
    if up:
        other = jnp.concatenate([v[1:], edge[None]], axis=0)
        return jnp.where(sub < SUBLANES - 1, pltpu.roll(v, SUBLANES - 1, 1), pltpu.roll(other, SUBLANES - 1, 1))
    other = jnp.concatenate([edge[None], v[:-1]], axis=0)
    return jnp.where(sub >= 1, pltpu.roll(v, 1, 1), pltpu.roll(other, 1, 1))


def _scan_tile(a, u, entering, emit, up):
    G = a.shape[0]
    sub = lax.broadcasted_iota(jnp.int32, a.shape, 1)
    d = 1
    while d < SUBLANES:
        if up:
            keep = sub < SUBLANES - d
            a_s = jnp.where(keep, pltpu.roll(a, SUBLANES - d, 1), 1.0)
            u_s = jnp.where(keep, pltpu.roll(u, SUBLANES - d, 1), 0.0)
        else:
            keep = sub >= d
            a_s = jnp.where(keep, pltpu.roll(a, d, 1), 1.0)
            u_s = jnp.where(keep, pltpu.roll(u, d, 1), 0.0)
        u = a * u_s + u
        a = a * a_s
        d *= 2
    for g in (reversed(range(G)) if up else range(G)):
        hg = u[g] + a[g] * entering
        emit(g, hg)
        entering = hg[0:1] if up else hg[SUBLANES - 1:SUBLANES]
    return entering


def _lru_scan_fwd(a, u, tr, deps=()):
    T, DL = a.shape

    def body(a_ref, u_ref, *rest):
        h_ref, carry = rest[len(deps):]

        @pl.when(pl.program_id(0) == 0)
        def _():
            carry[...] = jnp.zeros_like(carry)

        def emit(g, hg):
            h_ref[g * SUBLANES:(g + 1) * SUBLANES, :] = hg

        last = _scan_tile(_groups(a_ref[...]), _groups(u_ref[...]), carry[0:1, :], emit, up=False)
        carry[...] = jnp.broadcast_to(last, carry.shape)

    return _pcall(body, name="lru_scan_fwd", grid=(T // tr,),
                  in_specs=[_rt(tr, DL), _rt(tr, DL)] + [_ANY] * len(deps),
                  out_specs=_rt(tr, DL), out_shape=S((T, DL), F32),
                  scratch_shapes=[pltpu.VMEM((SUBLANES, DL), F32)],
                  compiler_params=_cparams(("arbitrary",)))(a, u, *deps)


def _lru_scan_bwd(a, h, dh, tr):
    T, DL = a.shape
    n = T // tr

    def body(a_ref, an_ref, h_ref, hp_ref, dh_ref, du_ref, da_ref, carry):
        i = pl.program_id(0)
        ti = n - 1 - i

        @pl.when(i == 0)
        def _():
            carry[...] = jnp.zeros_like(carry)

        a_next = _rows_shifted(_groups(a_ref[...]), jnp.where(ti == n - 1, 0.0, an_ref[...]), up=True)
        h_prev = _rows_shifted(_groups(h_ref[...]), jnp.where(ti == 0, 0.0, hp_ref[...]), up=False)

        def emit(g, gg):
            du_ref[g * SUBLANES:(g + 1) * SUBLANES, :] = gg
            da_ref[g * SUBLANES:(g + 1) * SUBLANES, :] = gg * h_prev[g]

        top = _scan_tile(a_next, _groups(dh_ref[...].astype(F32)), carry[0:1, :], emit, up=True)
        carry[...] = jnp.broadcast_to(top, carry.shape)

    return _pcall(body, name="lru_scan_bwd", grid=(n,),
                  in_specs=[_rt(tr, DL, 0, n), _halo_next(tr, DL, n, 0, n), _rt(tr, DL, 0, n),
                            _halo_prev(tr, DL, 0, n), _rt(tr, DL, 0, n)],
                  out_specs=[_rt(tr, DL, 0, n), _rt(tr, DL, 0, n)],
                  out_shape=[S((T, DL), F32), S((T, DL), F32)],
                  scratch_shapes=[pltpu.VMEM((SUBLANES, DL), F32)],
                  compiler_params=_cparams(("arbitrary",)))(a, a, h, h, dh)


def _adamw(w, g, m, v):
    m = ADAM_B1 * m + (1.0 - ADAM_B1) * g
    v = ADAM_B2 * v + (1.0 - ADAM_B2) * (g * g)
    m_hat = m / (1.0 - ADAM_B1 ** ADAM_STEP)
    v_hat = v / (1.0 - ADAM_B2 ** ADAM_STEP)
    delta = -ADAM_LR * (m_hat / (jnp.sqrt(v_hat) + ADAM_EPS) + ADAM_WD * w)
    return delta, m, v


def _adamw_big(name, w, m, v, own, recv, own_idx):
    _, R, C = w.shape
    n_recv = recv.shape[0]
    tr = _pick(R, (256, 128, 64, 32, 16))

    def body(idx_ref, w_ref, m_ref, v_ref, p_ref, *rest):
        g = p_ref[...].astype(F32)
        for r in rest[:n_recv]:
            g = g + r[...].astype(F32)
        g_ref, d_ref, nm_ref, nv_ref = rest[n_recv:]
        d, nm, nv = _adamw(w_ref[...], g, m_ref[...], v_ref[...])
        g_ref[...] = g
        d_ref[...] = d
        nm_ref[...] = nm
        nv_ref[...] = nv

    r_spec = lambda s: pl.BlockSpec((None, tr, C), lambda i, idx: (s, i, 0))
    t2 = r_spec(0)
    gs = pltpu.PrefetchScalarGridSpec(
        num_scalar_prefetch=1, grid=(R // tr,),
        in_specs=[t2, t2, t2, pl.BlockSpec((None, tr, C), lambda i, idx: (idx[0], i, 0))]
        + [r_spec(s) for s in range(n_recv)],
        out_specs=[t2, t2, t2, t2])
    return _pcall(body, name=name, grid_spec=gs, out_shape=[S((1, R, C), F32)] * 4,
                  compiler_params=_cparams(("parallel",)))(own_idx, w, m, v, own, *([recv] * n_recv))


def _adamw_small(ws, gs, ms, vs):
    n = len(ws)

    def body(*refs):
        for k in range(n):
            d, nm, nv = _adamw(refs[k][...], refs[n + k][...], refs[2 * n + k][...], refs[3 * n + k][...])
            refs[4 * n + k][...] = d
            refs[5 * n + k][...] = nm
            refs[6 * n + k][...] = nv

    res = _pcall(body, name="adamw_small", out_shape=[S(w.shape, F32) for w in ws] * 3,
                 compiler_params=_cparams())(*ws, *gs, *ms, *vs)
    return res[:n], res[n:2 * n], res[2 * n:]


def _sum8(name, parts):
    _, R, C = parts.shape

    def body(p_ref, o_ref):
        acc = p_ref[0]
        for k in range(1, N_DEV):
            acc = acc + p_ref[k]
        o_ref[...] = acc

    return _pcall(body, name=name, out_shape=S((R, C), F32), compiler_params=_cparams())(parts)


def _pair_sum(name, full, recv, c_idx):
    _, R, C = full.shape
    tr = _pick(R, (256, 128, 64, 32, 16))

    def body(c_ref, f_ref, r_ref, o_ref):
        o_ref[...] = (f_ref[...].astype(F32) + r_ref[...].astype(F32)).astype(o_ref.dtype)

    gs = pltpu.PrefetchScalarGridSpec(
        num_scalar_prefetch=1, grid=(4, R // tr),
        in_specs=[pl.BlockSpec((None, tr, C), lambda j, i, c: (2 * j + c[0], i, 0)),
                  pl.BlockSpec((None, tr, C), lambda j, i, c: (j, i, 0))],
        out_specs=pl.BlockSpec((None, tr, C), lambda j, i, c: (j, i, 0)))
    return _pcall(body, name=name, grid_spec=gs, out_shape=S((4, R, C), BF16),
                  compiler_params=_cparams(("parallel", "parallel")))(c_idx, full, recv)


def _cast_bf16(name, w, dev_idx, row0=0, rows=None, deps=()):
    C = w.shape[2]
    R = w.shape[1] if rows is None else rows
    tr = _pick(R, (256, 128, 64, 32, 16))
    b0 = row0 // tr

    def body(d_ref, w_ref, *rest):
        rest[-1][...] = w_ref[...].astype(BF16)

    gs = pltpu.PrefetchScalarGridSpec(
        num_scalar_prefetch=1, grid=(R // tr,),
        in_specs=[pl.BlockSpec((None, tr, C), lambda i, d: (0, i + b0, 0))] + [_ANY] * len(deps),
        out_specs=pl.BlockSpec((None, tr, C), lambda i, d: (d[0], i, 0)))
    return _pcall(body, name=name, grid_spec=gs, out_shape=S((N_DEV, R, C), BF16),
                  compiler_params=_cparams(("parallel",)))(dev_idx, w, *deps)


def _own_block(v, dev):
    return lax.dynamic_update_slice(lax.empty((N_DEV,) + v.shape, v.dtype), v[None], (dev,) + (0,) * v.ndim)


_ANY = pl.BlockSpec(memory_space=pl.ANY)


def _position():
    return lax.axis_index("x"), lax.axis_index("y"), lax.axis_index("c")


def _allgather(name, bufs, deps=()):
    n = len(bufs)
    nd = len(deps)

    def body(*refs):
        outs = refs[n + nd:2 * n + nd]
        send, recv = refs[2 * n + nd:]
        x, y, c = _position()
        me, sib = (x, y, c), (x, y, 1 - c)
        chips = [(1 - x, y), (x, 1 - y), (1 - x, 1 - y)]

        def copy(a, k, block, to):
            bx, by, bc = block
            blk = outs[a].at[4 * bx + 2 * by + bc]
            return pltpu.make_async_remote_copy(
                src_ref=blk, dst_ref=blk, send_sem=send.at[a, k], recv_sem=recv.at[a, k],
                device_id=to, device_id_type=MESH)

        first = []
        for a in range(n):
            first.append(copy(a, 0, me, sib))
            first += [copy(a, 1 + j, me, (*chip, c)) for j, chip in enumerate(chips)]
        for cp in first:
            cp.start()
        passed = []
        for j, chip in enumerate(chips):
            for a in range(n):
                copy(a, 1 + j, (*chip, c), me).wait_recv()
                cp = copy(a, 4 + j, (*chip, c), sib)
                cp.start()
                passed.append(cp)
        for a in range(n):
            copy(a, 0, sib, me).wait_recv()
        for j, chip in enumerate(chips):
            for a in range(n):
                copy(a, 4 + j, (*chip, 1 - c), me).wait_recv()
        for cp in first + passed:
            cp.wait_send()

    return _pcall(body, name=name, in_specs=[_ANY] * (n + nd), out_specs=[_ANY] * n,
                  out_shape=[S(b.shape, b.dtype) for b in bufs], input_output_aliases={a: a for a in range(n)},
                  scratch_shapes=[pltpu.SemaphoreType.DMA((n, 7)), pltpu.SemaphoreType.DMA((n, 7))])(*bufs, *deps)


def _rs_sibling(name, fulls):
    n = len(fulls)

    def body(*refs):
        ins, outs = refs[:n], refs[n:2 * n]
        send, recv = refs[2 * n:]
        x, y, c = _position()
        copies = []
        for a in range(n):
            for j in range(4):
                copies.append(pltpu.make_async_remote_copy(
                    src_ref=ins[a].at[2 * j + (1 - c)], dst_ref=outs[a].at[j], send_sem=send.at[a, j],
                    recv_sem=recv.at[a, j], device_id=(x, y, 1 - c), device_id_type=MESH))
        for cp in copies:
            cp.start()
        for cp in copies:
            cp.wait()

    return _pcall(body, name=name, in_specs=[_ANY] * n, out_specs=[_ANY] * n,
                  out_shape=[S((4,) + f.shape[1:], f.dtype) for f in fulls],
                  scratch_shapes=[pltpu.SemaphoreType.DMA((n, 4)), pltpu.SemaphoreType.DMA((n, 4))])(*fulls)


_HBM = pl.BlockSpec(memory_space=pltpu.HBM)
_SEM = pl.BlockSpec(memory_space=pltpu.SEMAPHORE)
_EFFECT = pltpu.SideEffectType.DATAFLOW_SIDE_EFFECTING


def _remote_copies(copies_fn, srcs, lands, send, recv):
    x, y, c = _position()
    return [pltpu.make_async_remote_copy(src_ref=s, dst_ref=d, send_sem=send[i], recv_sem=recv[i], device_id=to,
                                         device_id_type=MESH)
            for i, (s, d, to) in enumerate(copies_fn(x, y, c, srcs, lands))]


def _split_start(name, srcs, lands, copies_fn, nc, after=()):
    n, nl, na = len(srcs), len(lands), len(after)

    def body(*refs):
        src_refs, land_refs = refs[:n], refs[n:n + nl]
        outs = refs[n + nl + na:]
        for cp in _remote_copies(copies_fn, src_refs, land_refs, outs[:nc], outs[nc:2 * nc]):
            cp.start()
        outs[-1][...] = jnp.zeros_like(outs[-1])

    hbm = lambda a: pltpu.with_memory_space_constraint(a, pltpu.HBM)
    res = _pcall(
        body, name=name, in_specs=[_HBM] * (n + nl) + [_ANY] * na,
        out_specs=[_SEM] * (2 * nc) + [_HBM] * (n + nl) + [pl.BlockSpec(memory_space=pltpu.VMEM)],
        out_shape=[pltpu.SemaphoreType.DMA(())] * (2 * nc) + [pltpu.HBM(s.shape, s.dtype) for s in srcs]
        + [pltpu.HBM(l.shape, l.dtype) for l in lands] + [S((SUBLANES, LANES), F32)],
        input_output_aliases={i: 2 * nc + i for i in range(n + nl)},
        compiler_params=pltpu.CompilerParams(has_side_effects=_EFFECT),
    )(*[hbm(s) for s in srcs], *[hbm(l) for l in lands], *after)
    return res[:2 * nc], res[2 * nc:2 * nc + n], res[2 * nc + n:2 * nc + n + nl], res[-1]


def _split_wait(name, sems, srcs, lands, after, copies_fn, nc):
    n, nl = len(srcs), len(lands)
    after = list(after) if isinstance(after, (list, tuple)) else [after]

    def body(*refs):
        src_refs, land_refs = refs[:n], refs[n:n + nl]
        sem_refs = refs[n + nl:n + nl + 2 * nc]
        for cp in _remote_copies(copies_fn, src_refs, land_refs, sem_refs[:nc], sem_refs[nc:]):
            cp.wait_send()
            cp.wait_recv()

    res = _pcall(
        body, name=name, in_specs=[_HBM] * (n + nl) + [_SEM] * (2 * nc) + [_ANY] * len(after),
        out_specs=[_HBM] * (n + nl), out_shape=[pltpu.HBM(a.shape, a.dtype) for a in list(srcs) + list(lands)],
        input_output_aliases={i: i for i in range(n + nl)},
        compiler_params=pltpu.CompilerParams(has_side_effects=_EFFECT),
    )(*srcs, *lands, *sems, *after)
    return res[:n], res[n:]


def _other_chips(x, y):
    return [(1 - x, y), (x, 1 - y), (1 - x, 1 - y)]


def _ag_copies(x, y, c, srcs, lands):
    out = []
    for land in lands:
        blk = land.at[4 * x + 2 * y + c]
        out.append((blk, blk, (x, y, 1 - c)))
        out += [(blk, blk, (px, py, c)) for px, py in _other_chips(x, y)]
    return out


def _rs_copies(x, y, c, srcs, lands):
    return [(s.at[2 * px + py], land.at[j], (px, py, c))
            for s, land in zip(srcs, lands) for j, (px, py) in enumerate(_other_chips(x, y))]


def _fwd_copies(x, y, c, srcs, lands):
    out = []
    for land in lands:
        for px, py in _other_chips(x, y):
            blk = land.at[4 * px + 2 * py + c]
            out.append((blk, blk, (x, y, 1 - c)))
    return out


def _direct_copies(x, y, c, srcs, lands):
    out = []
    for s, land in zip(srcs, lands):
        for r in range(1, N_DEV):
            px = 1 - x if r & 4 else x
            py = 1 - y if r & 2 else y
            pc = 1 - c if r & 1 else c
            out.append((s.at[4 * px + 2 * py + pc], land.at[r - 1], (px, py, pc)))
    return out


def _ag_finish(name, lands):
    n = len(lands)

    def body(*refs):
        outs = refs[n:2 * n]
        send, recv = refs[2 * n:]
        x, y, c = _position()

        def swap(a, j, px, py, pc):
            blk = outs[a].at[4 * px + 2 * py + pc]
            return pltpu.make_async_remote_copy(src_ref=blk, dst_ref=blk, send_sem=send.at[a, j], recv_sem=recv.at[a, j],
                                                device_id=(x, y, 1 - c), device_id_type=MESH)

        chips = _other_chips(x, y)
        sends = [swap(a, j, px, py, c) for a in range(n) for j, (px, py) in enumerate(chips)]
        for cp in sends:
            cp.start()
        for a in range(n):
            for j, (px, py) in enumerate(chips):
                swap(a, j, px, py, 1 - c).wait_recv()
        for cp in sends:
            cp.wait_send()

    return _pcall(body, name=name, in_specs=[_ANY] * n, out_specs=[_ANY] * n,
                  out_shape=[S(l.shape, l.dtype) for l in lands], input_output_aliases={a: a for a in range(n)},
                  scratch_shapes=[pltpu.SemaphoreType.DMA((n, 3)), pltpu.SemaphoreType.DMA((n, 3))])(*lands)


def _pad_lanes(v):
    return jnp.pad(v, ((0, 0), (0, LANES - v.shape[1])))


def _flat_rows(pieces):
    flat = jnp.concatenate([p.reshape(-1) for p in pieces])
    rows = -(-flat.shape[0] // (SMALL_W * SUBLANES)) * SUBLANES
    return jnp.pad(flat, (0, rows * SMALL_W - flat.shape[0])).reshape(rows, SMALL_W)


def _unflat(buf, shapes):
    flat = buf.reshape(-1)
    out, off = [], 0
    for sh in shapes:
        n = 1
        for d in sh:
            n *= d
        out.append(flat[off:off + n].reshape(sh))
        off += n
    return out


def kernel(x, pre_mix_norm, w_in, ssd_conv_w, ssd_conv_b, ssd_dt_bias, ssd_a_log, ssd_d, ssd_norm, lru_conv_w, lru_conv_b, lru_w_a, lru_b_a, lru_w_x, lru_b_x, lru_lambda, lru_norm, w_out, post_mix_norm, pre_mlp_norm, w_mlp_in, w_mlp_out, post_mlp_norm, loss_target, m_pre_mix_norm, m_w_in, m_ssd_conv_w, m_ssd_conv_b, m_ssd_dt_bias, m_ssd_a_log, m_ssd_d, m_ssd_norm, m_lru_conv_w, m_lru_conv_b, m_lru_w_a, m_lru_b_a, m_lru_w_x, m_lru_b_x, m_lru_lambda, m_lru_norm, m_w_out, m_post_mix_norm, m_pre_mlp_norm, m_w_mlp_in, m_w_mlp_out, m_post_mlp_norm, v_pre_mix_norm, v_w_in, v_ssd_conv_w, v_ssd_conv_b, v_ssd_dt_bias, v_ssd_a_log, v_ssd_d, v_ssd_norm, v_lru_conv_w, v_lru_conv_b, v_lru_w_a, v_lru_b_a, v_lru_w_x, v_lru_b_x, v_lru_lambda, v_lru_norm, v_w_out, v_post_mix_norm, v_pre_mlp_norm, v_w_mlp_in, v_w_mlp_out, v_post_mlp_norm):
    names = ['pre_mix_norm', 'w_in', 'ssd_conv_w', 'ssd_conv_b', 'ssd_dt_bias', 'ssd_a_log', 'ssd_d', 'ssd_norm',
             'lru_conv_w', 'lru_conv_b', 'lru_w_a', 'lru_b_a', 'lru_w_x', 'lru_b_x', 'lru_lambda', 'lru_norm',
             'w_out', 'post_mix_norm', 'pre_mlp_norm', 'w_mlp_in', 'w_mlp_out', 'post_mlp_norm']
    loc = locals()
    W = {n: loc[n] for n in names}
    Mo = {n: loc["m_" + n] for n in names}
    Vo = {n: loc["v_" + n] for n in names}
    big = ['w_in', 'w_out', 'w_mlp_in', 'w_mlp_out']

    px, py, pc = _position()
    dev = 4 * px + 2 * py + pc
    dev_idx = jnp.reshape(dev, (1,)).astype(jnp.int32)
    c_idx = jnp.reshape(pc, (1,)).astype(jnp.int32)
    chip_idx = jnp.reshape(2 * px + py, (1,)).astype(jnp.int32)

    _, T, D = x.shape
    x2 = x.reshape(T, D)
    tgt = loss_target.reshape(T, D)
    n_heads = ssd_dt_bias.shape[1]
    XBC = ssd_conv_b.shape[1]
    GN = XBC // 4
    DS = XBC - 2 * GN
    DL = lru_norm.shape[1]
    DFF = w_mlp_in.shape[2] * N_DEV
    DIN = w_in.shape[2] * N_DEV
    NP = XBC + DS + 2 * DL + LANES
    assert DS % GN == 0 and XBC % DS == 0 and DS == DL and n_heads <= LANES
    cb_z, cb_gate, cb_xl, cb_dt = XBC // DS, XBC // DS + 1, XBC // DS + 2, (XBC + DS + 2 * DL) // LANES
    tr = min(256, T // 2)
    nt = T // tr
    Q = min(256, T // 2)

    Dh = D // 2
    sh_a = _cast_bf16("cast_w_in_a", W['w_in'], dev_idx, 0, Dh)
    sh_b = _cast_bf16("cast_w_in_b", W['w_in'], dev_idx, Dh, Dh)
    later = big[1:]
    a_bufs = [sh_a, _own_block(ssd_conv_w[0], dev), _own_block(lru_conv_w[0], dev)]
    a_sems, _, a_lands, a_token = _split_start("allgather_w_in_a_start", [], a_bufs, _ag_copies, 4 * len(a_bufs))
    sh = {n: _cast_bf16("cast_" + n, W[n], dev_idx, deps=[a_token]) for n in later}

    def f_norm_in(first, last, xv, g):
        return (_rms(xv, g),), ()
    (h,) = _rows_call("norm_in", f_norm_in, nt, [x2, pre_mix_norm], [_rt(tr, D), _full(pre_mix_norm)],
                      [((T, D), BF16, _rt(tr, D))], [], 'tf', deps=[a_token])
    Mo['w_in'] = Mo['w_in'] + a_token[0, 0]
    Vo['w_in'] = Vo['w_in'] + a_token[0, 0]
    _, a_lands = _split_wait("allgather_w_in_a_wait", a_sems, [], a_lands,
                             [h, Mo['w_in'], Vo['w_in']] + [sh[n] for n in later], _ag_copies, 4 * len(a_bufs))
    g_in_a, g_cs, g_cl = _ag_finish("allgather_w_in_a_finish", a_lands)
    b_sems, _, b_lands, b_token = _split_start("allgather_w_in_b_start", [], [sh_b], _ag_copies, 4, after=[g_in_a])
    ag_sems, ag_srcs, ag_lands, ag_token = _split_start(
        "allgather_later_start", [], [sh[n] for n in later], _ag_copies, 4 * len(later), after=[b_token])
    conv_s = jnp.transpose(g_cs, (1, 0, 2)).reshape(CONV_WIDTH, XBC)
    conv_l = jnp.transpose(g_cl, (1, 0, 2)).reshape(CONV_WIDTH, DL)
    wb = DIN // N_DEV
    o_z, o_xbc, o_dt, o_gate, o_xl = 0, DS, DS + XBC, DS + XBC + n_heads, DS + XBC + n_heads + DL
    segs = [(o_xbc, o_xbc + XBC, 0), (o_z, o_z + DS, XBC), (o_gate, o_gate + DL, XBC + DS),
            (o_xl, o_xl + DL, XBC + DS + DL), (o_dt, o_dt + n_heads, NP - LANES)]

    def ref_cols(g, lo, hi):
        out = []
        while lo < hi:
            k = lo // wb
            e = min(hi, (k + 1) * wb)
            out.append(g[k, :, lo - k * wb:e - k * wb])
            lo = e
        return out

    def laid_out(g):
        return jnp.concatenate([p for a, b, _ in segs for p in ref_cols(g, a, b)]
                               + [jnp.zeros((g.shape[1], LANES - n_heads), BF16)], axis=1)

    def my_cols(g, lo, hi):
        out = []
        for a, b, m in sorted(segs):
            s, e = max(lo, a), min(hi, b)
            if s < e:
                out.append(g[:, m + s - a:m + e - a])
        return out

    wp_a = laid_out(g_in_a)
    dt_bias = _pad_lanes(ssd_dt_bias)
    a_log = _pad_lanes(ssd_a_log)
    d_skip = _pad_lanes(ssd_d)
    wa_b, wx_b = lru_w_a[0].astype(BF16), lru_w_x[0].astype(BF16)
    b_a, b_x = lru_b_a.reshape(1, DL), lru_b_x.reshape(1, DL)

    (proj_a,) = _mm("proj_a", h, wp_a, a_cols=(0, Dh), outs=((BF16, None),), deps=[ag_token])
    (dt_a,) = _mm("proj_dt_a", h, wp_a[:, NP - LANES:], a_cols=(0, Dh))
    _, b_lands = _split_wait("allgather_w_in_b_wait", b_sems, [], b_lands, proj_a, _ag_copies, 4)
    (g_in_b,) = _ag_finish("allgather_w_in_b_finish", b_lands)
    wp_b = laid_out(g_in_b)
    add = lambda r, e: r + e.astype(F32)
    (proj,) = _mm("proj_b", h, wp_b, a_cols=(Dh, Dh), extra=proj_a, outs=((BF16, add),))
    (dt_raw,) = _mm("proj_dt_b", h, wp_b[:, NP - LANES:], a_cols=(Dh, Dh), extra=dt_a, outs=((F32, add),))

    cwx = min(1024, XBC)

    def f_ssd_pre(first, last, xbc, halo, w, b):
        pre = _conv_pre(xbc, jnp.where(first, 0.0, halo), w, b)
        return (pre * jax.nn.sigmoid(pre),), ()
    (xbc_act,) = _rows_call(
        "ssd_pre", f_ssd_pre, nt, [proj, proj, conv_s, ssd_conv_b],
        [_rt(tr, XBC), _halo_prev(tr, XBC, rows=PACKED_ROWS), _full(conv_s), _full(ssd_conv_b)],
        [((T, XBC), F32, _rt(tr, XBC))], [], ['t', 'p0', 'f', 'f'], cw=cwx)

    def f_ssd_dt(first, last, dtr, dtb):
        return (jax.nn.softplus(dtr + dtb),), ()
    (dt,) = _rows_call("ssd_dt", f_ssd_dt, nt, [dt_raw, dt_bias], [_rt(tr, LANES), _full(dt_bias)],
                       [((T, LANES), F32, _rt(tr, LANES))], [], 'tf')

    y_ssd, h_prev = _ssd_fwd(xbc_act, dt, a_log, d_skip, n_heads, Q)

    gw = DS // SSD_GROUPS

    def ssd_post(y, z, g):
        yz = y * jax.nn.silu(z)
        parts = []
        for k in range(SSD_GROUPS):
            yk = yz[:, k * gw:(k + 1) * gw]
            parts.append(yk * lax.rsqrt(jnp.mean(yk * yk, axis=-1, keepdims=True) + EPS))
        return jnp.concatenate(parts, axis=-1) * g

    def f_ssd_post(first, last, y, z, g):
        return (ssd_post(y, z, g),), ()
    (mixcat,) = _rows_call("ssd_post", f_ssd_post, nt, [y_ssd, proj, ssd_norm],
                           [_rt(tr, DS), _rt(tr, DS, cb_z), _full(ssd_norm)], [((T, DS + DL), BF16, _rt(tr, DS))], [],
                           'ttf')

    def f_lru_pre(first, last, xv, halo, w, b):
        return (_conv_pre(xv, jnp.where(first, 0.0, halo), w, b),), ()
    (xl,) = _rows_call("lru_pre", f_lru_pre, nt, [proj, proj, conv_l, lru_conv_b],
                       [_rt(tr, DL, cb_xl), _halo_prev(tr, DL, cb_xl, rows=PACKED_ROWS), _full(conv_l),
                        _full(lru_conv_b)],
                       [((T, DL), F32, _rt(tr, DL))], [], ['t', 'p0', 'f', 'f'])

    a_lru, u_lru = _lru_gates_fwd(xl, wa_b, b_a, wx_b, b_x, lru_lambda, tr)
    h_lru = _lru_scan_fwd(a_lru, u_lru, tr)
    _, ag_lands = _split_wait("allgather_later_wait", ag_sems, ag_srcs, ag_lands, h_lru, _ag_copies, 4 * len(later))
    f_sems, _, ag_lands, f_token = _split_start("allgather_later_fwd_start", [], ag_lands, _fwd_copies, 3 * len(later))

    def lru_post(hv, gate, g):
        return _rms(hv * jax.nn.gelu(gate), g)

    def f_lru_post(first, last, hv, gate, g):
        return (lru_post(hv, gate, g),), ()
    cb_l = DS // DL
    (mixcat,) = _rows_call("lru_post", f_lru_post, nt, [h_lru, proj, lru_norm],
                           [_rt(tr, DL), _rt(tr, DL, cb_gate), _full(lru_norm)],
                           [((T, DS + DL), BF16, _rt(tr, DL, cb_l))], [], 'ttf', into=mixcat, deps=[f_token])

    _, (g_out, g_mi, g_mo) = _split_wait("allgather_later_fwd_wait", f_sems, [], ag_lands, mixcat, _fwd_copies,
                                         3 * len(later))
    w_out_f = g_out.reshape(DS + DL, D)
    w_mi_f = jnp.transpose(g_mi, (1, 0, 2)).reshape(D, DFF)
    w_mo_f = g_mo.reshape(DFF, D)
    (mix,) = _mm("mix", mixcat, w_out_f, outs=((BF16, None),))

    def f_post_mix(first, last, xv, mx, gpm, gpl):
        x1 = xv + _rms(mx, gpm)
        return (x1, _rms(x1, gpl)), ()
    x1, hn = _rows_call("post_mix", f_post_mix, nt, [x2, mix, post_mix_norm, pre_mlp_norm],
                        [_rt(tr, D), _rt(tr, D), _full(post_mix_norm), _full(pre_mlp_norm)],
                        [((T, D), F32, _rt(tr, D)), ((T, D), BF16, _rt(tr, D))], [], 'ttff')

    hm, act = _mm("mlp_in", hn, w_mi_f,
                  outs=((BF16, None), (BF16, lambda r, e: jnp.square(jnp.maximum(r, 0.0)))))
    (hm2,) = _mm("mlp_out", act, w_mo_f, outs=((BF16, None),))

    def f_final(first, last, x1v, hm2v, g, tg):
        err = x1v + _rms(hm2v, g) - tg
        dx2 = err * (1.0 / D)
        dh, dg = _rms_bwd(hm2v, g, dx2)
        loss = jnp.full((1, LANES), 0.5 / D, F32) * jnp.sum(err * err)
        return (dx2, dh), (dg, loss)
    dx1a, dhm2, g_post_mlp, loss_part = _rows_call(
        "loss_head", f_final, nt, [x1, hm2, post_mlp_norm, tgt],
        [_rt(tr, D), _rt(tr, D), _full(post_mlp_norm), _rt(tr, D)],
        [((T, D), BF16, _rt(tr, D)), ((T, D), BF16, _rt(tr, D))], [(1, D), (1, LANES)], 'ttft')

    def rs_begin(n, full):
        (from_sib,) = _rs_sibling("rs_sibling_" + n, [full])
        pair = _pair_sum("pair_sum_" + n, full, from_sib, c_idx)
        sems, srcs, lands, token = _split_start("rs_start_" + n, [pair], [lax.empty((3,) + pair.shape[1:], BF16)],
                                                _rs_copies, 3)
        return (sems, srcs, lands), token

    def rs_end(n, state, after):
        (pair,), (recv,) = _split_wait("rs_wait_" + n, *state, after, _rs_copies, 3)
        return pair, recv, chip_idx

    def rs_direct_begin(n, full):
        sems, srcs, lands, token = _split_start("rs_start_" + n, [full],
                                                [lax.empty((N_DEV - 1,) + full.shape[1:], BF16)], _direct_copies,
                                                N_DEV - 1)
        return (sems, srcs, lands), token

    def rs_direct_end(n, state, after):
        (full,), (recv,) = _split_wait("rs_wait_" + n, *state, after, _direct_copies, N_DEV - 1)
        return full, recv, dev_idx

    (gw_mo,) = _mm("dw_mlp_out", act, dhm2, ta=True, outs=((BF16, None),))
    rs_mo, tok = rs_direct_begin('w_mlp_out', gw_mo.reshape(N_DEV, DFF // N_DEV, D))
    (dhm,) = _mm("d_mlp_act", dhm2, w_mo_f, tb=True, extra=hm,
                 outs=((BF16, lambda r, e: r * (2.0 * jnp.maximum(e.astype(F32), 0.0))),), deps=[tok])
    (gw_mi,) = _mm("dw_mlp_in", hn, dhm, ta=True, outs=((BF16, None),), out_blocks=N_DEV)
    rs_mi, tok = rs_direct_begin('w_mlp_in', gw_mi)
    (dhn,) = _mm("d_mlp_in", dhm, w_mi_f, tb=True, outs=((BF16, None),), deps=[tok])

    def f_post_mix_bwd(first, last, x1v, mx, gpm, gpl, dhnv, dxa):
        dx1, dgpl = _rms_bwd(x1v, gpl, dhnv)
        dx1 = dx1 + dxa
        dmx, dgpm = _rms_bwd(mx, gpm, dx1)
        return (dx1, dmx), (dgpl, dgpm)
    dx1, dmix, g_pre_mlp, g_post_mix = _rows_call(
        "post_mix_bwd", f_post_mix_bwd, nt, [x1, mix, post_mix_norm, pre_mlp_norm, dhn, dx1a],
        [_rt(tr, D), _rt(tr, D), _full(post_mix_norm), _full(pre_mlp_norm), _rt(tr, D), _rt(tr, D)],
        [((T, D), BF16, _rt(tr, D)), ((T, D), BF16, _rt(tr, D))], [(1, D), (1, D)], 'ttfftt')

    (gw_out,) = _mm("dw_out", mixcat, dmix, ta=True, outs=((BF16, None),))
    rs_out, tok = rs_direct_begin('w_out', gw_out.reshape(N_DEV, -1, D))
    (dmixcat,) = _mm("d_mix", dmix, w_out_f, tb=True, outs=((BF16, None),), deps=[tok])

    def f_lru_post_bwd(first, last, hv, gate, g, dy):
        gl, dgl = _gelu_and_grad(gate)
        dv, dg = _rms_bwd(hv * gl, g, dy)
        return (dv * hv * dgl, dv * gl), (dg,)
    dproj, dh_lru, g_lru_norm = _rows_call(
        "lru_post_bwd", f_lru_post_bwd, nt, [h_lru, proj, lru_norm, dmixcat],
        [_rt(tr, DL), _rt(tr, DL, cb_gate), _full(lru_norm), _rt(tr, DL, cb_l)],
        [((T, NP), BF16, _rt(tr, DL, cb_gate)), ((T, DL), BF16, _rt(tr, DL))], [(1, DL)], 'ttft')

    du_lru, da_lru = _lru_scan_bwd(a_lru, h_lru, dh_lru, tr)
    dxl, g_wa, g_ba, g_wx, g_bx, g_lam = _lru_gates_bwd(xl, wa_b, b_a, wx_b, b_x, lru_lambda, da_lru, du_lru, tr)

    conv_bwd_kinds = ['t', 'p0', 'n0', 't', 'n3', 'f', 'f']

    def f_lru_pre_bwd(first, last, xv, hp, xn, d, dn, w, b):
        dx, dw8, db = _conv_bwd_tile(first, last, xv, hp, xn, d, dn, w, b, silu=False)
        return (dx,), (dw8, db)
    dproj, g_convl8, g_convl_b = _rows_call(
        "lru_pre_bwd", f_lru_pre_bwd, nt, [proj, proj, proj, dxl, dxl, conv_l, lru_conv_b],
        [_rt(tr, DL, cb_xl), _halo_prev(tr, DL, cb_xl, rows=PACKED_ROWS),
         _halo_next(tr, DL, nt, cb_xl, rows=PACKED_ROWS), _rt(tr, DL),
         _halo_next(tr, DL, nt, rows=PACKED_ROWS), _full(conv_l), _full(lru_conv_b)],
        [((T, NP), BF16, _rt(tr, DL, cb_xl))], [(SUBLANES, DL), (1, DL)], conv_bwd_kinds, into=dproj)

    def f_ssd_post_bwd(first, last, y, z, g, dy):
        s = jax.nn.sigmoid(z)
        sz = z * s
        yz = y * sz
        gdy = dy * g
        dyz, yn = [], []
        for k in range(SSD_GROUPS):
            cols = slice(k * gw, (k + 1) * gw)
            r = lax.rsqrt(jnp.mean(yz[:, cols] * yz[:, cols], axis=-1, keepdims=True) + EPS)
            xr = yz[:, cols] * r
            dyz.append((gdy[:, cols] - xr * jnp.mean(gdy[:, cols] * xr, axis=-1, keepdims=True)) * r)
            yn.append(xr)
        dyz = jnp.concatenate(dyz, axis=-1)
        dz = dyz * y * (s + sz * (1.0 - s))
        return (dz, dyz * sz), (_colsum(dy * jnp.concatenate(yn, axis=-1)),)
    dproj, dy_ssd, g_ssd_norm = _rows_call(
        "ssd_post_bwd", f_ssd_post_bwd, nt, [y_ssd, proj, ssd_norm, dmixcat],
        [_rt(tr, DS), _rt(tr, DS, cb_z), _full(ssd_norm), _rt(tr, DS, 0)],
        [((T, NP), BF16, _rt(tr, DS, cb_z)), ((T, DS), BF16, _rt(tr, DS))], [(1, DS)], 'ttft', into=dproj)

    dxbc_act, ddt, g_alog, g_dskip = _ssd_bwd(xbc_act, dt, a_log, d_skip, h_prev, dy_ssd, n_heads, Q)

    def f_ssd_pre_bwd(first, last, xv, hp, xn, d, dn, w, b):
        dx, dw8, db = _conv_bwd_tile(first, last, xv, hp, xn, d, dn, w, b, silu=True)
        return (dx,), (dw8, db)
    dproj, g_convs8, g_convs_b = _rows_call(
        "ssd_pre_bwd", f_ssd_pre_bwd, nt, [proj, proj, proj, dxbc_act, dxbc_act, conv_s, ssd_conv_b],
        [_rt(tr, XBC), _halo_prev(tr, XBC, rows=PACKED_ROWS), _halo_next(tr, XBC, nt, rows=PACKED_ROWS),
         _rt(tr, XBC), _halo_next(tr, XBC, nt),
         _full(conv_s), _full(ssd_conv_b)],
        [((T, NP), BF16, _rt(tr, XBC))], [(SUBLANES, XBC), (1, XBC)], conv_bwd_kinds, into=dproj, cw=cwx)

    def f_ssd_dt_bwd(first, last, ddtv, dtr, dtb):
        ddtr = ddtv * jax.nn.sigmoid(dtr + dtb)
        return (ddtr,), (_colsum(ddtr),)
    dproj, g_dtb = _rows_call(
        "ssd_dt_bwd", f_ssd_dt_bwd, nt, [ddt, dt_raw, dt_bias],
        [_rt(tr, LANES), _rt(tr, LANES), _full(dt_bias)],
        [((T, NP), BF16, _rt(tr, LANES, cb_dt))], [(1, LANES)], 'ttf', into=dproj)
    small = {
        'ssd_conv_w': g_convs8[:CONV_WIDTH], 'ssd_conv_b': g_convs_b,
        'ssd_dt_bias': g_dtb[:, :n_heads], 'ssd_a_log': g_alog[:, :n_heads], 'ssd_d': g_dskip[:, :n_heads],
        'ssd_norm': g_ssd_norm, 'lru_conv_w': g_convl8[:CONV_WIDTH], 'lru_conv_b': g_convl_b,
        'lru_w_a': g_wa, 'lru_b_a': g_ba, 'lru_w_x': g_wx, 'lru_b_x': g_bx, 'lru_lambda': g_lam,
        'lru_norm': g_lru_norm, 'post_mix_norm': g_post_mix, 'pre_mlp_norm': g_pre_mlp,
        'post_mlp_norm': g_post_mlp, 'loss': loss_part[:, :1],
    }
    wide = ['lru_w_a', 'lru_w_x']
    narrow = [n for n in small if n not in wide]
    lb = lru_w_a.shape[-1]
    s_srcs = [_flat_rows([small[n] for n in narrow]), g_wa.reshape(-1, lb), g_wx.reshape(-1, lb)]
    s_sems, s_srcs, s_lands, tok = _split_start(
        "small_grads_start", [], [_own_block(a, dev) for a in s_srcs], _ag_copies, 4 * len(s_srcs))

    (gwp,) = _mm("dw_proj", h, dproj, ta=True, outs=((BF16, None),), deps=[tok])
    rs_in, tok = rs_begin(
        'w_in', jnp.stack([jnp.concatenate(my_cols(gwp, k * wb, (k + 1) * wb), axis=1) for k in range(N_DEV)]))
    (dh_a,) = _mm("d_proj_a", dproj, wp_a, tb=True, outs=((BF16, None),), deps=[tok])
    (dh_b,) = _mm("d_proj_b", dproj, wp_b, tb=True, outs=((BF16, None),), deps=[tok])

    def f_norm_in_bwd(first, last, xv, g, dha, dhb, dxa):
        dx, dg = _rms_bwd(xv, g, jnp.concatenate([dha, dhb], axis=1))
        return (dx + dxa,), (dg,)
    grad_x, g_pre_mix = _rows_call(
        "norm_in_bwd", f_norm_in_bwd, nt, [x2, pre_mix_norm, dh_a, dh_b, dx1],
        [_rt(tr, D), _full(pre_mix_norm), _rt(tr, Dh), _rt(tr, Dh), _rt(tr, D)], [((T, D), F32, _rt(tr, D))],
        [(1, D)], 'tfttt')

    big_out = {}
    for n, state in (('w_mlp_out', rs_mo), ('w_mlp_in', rs_mi), ('w_out', rs_out)):
        big_out[n] = _adamw_big("adamw_" + n, W[n], Mo[n], Vo[n], *rs_direct_end(n, state, grad_x))

    (g_pm8,) = _allgather("allgather_pre_mix_grad", [_own_block(g_pre_mix, dev)], deps=[big_out['w_out'][0]])
    _, s_lands = _split_wait("small_grads_wait", s_sems, s_srcs, s_lands, g_pm8, _ag_copies, 4 * len(s_lands))
    g_narrow, g_wa8, g_wx8 = _ag_finish("small_grads_finish", s_lands)
    summed = dict(zip(narrow, _unflat(_sum8("sum_small_grads", g_narrow), [small[n].shape for n in narrow])))
    summed['pre_mix_norm'] = _sum8("sum_pre_mix_grad", g_pm8)
    summed['lru_w_a'] = _sum8("sum_lru_w_a_grads", g_wa8)
    summed['lru_w_x'] = _sum8("sum_lru_w_x_grads", g_wx8)
    loss = summed.pop('loss').reshape(())
    for n, full_w in (('ssd_conv_w', XBC), ('lru_conv_w', DL)):
        wdt = full_w // N_DEV
        summed[n] = lax.dynamic_slice_in_dim(summed[n], dev * wdt, wdt, axis=1)
    small_params = [n for n in names if n not in big]
    as2d = lambda a: a.reshape(-1, a.shape[-1])
    res = _adamw_small([as2d(W[n]) for n in small_params],
                       [summed[n].reshape(as2d(W[n]).shape) for n in small_params],
                       [as2d(Mo[n]) for n in small_params], [as2d(Vo[n]) for n in small_params])
    grads = {n: summed[n].reshape(W[n].shape) for n in small_params}
    delta, new_m, new_v = ({n: r.reshape(W[n].shape) for n, r in zip(small_params, rs)} for rs in res)

    big_out['w_in'] = _adamw_big("adamw_w_in", W['w_in'], Mo['w_in'], Vo['w_in'], *rs_end('w_in', rs_in, g_pm8))
    for n in big:
        grads[n], delta[n], new_m[n], new_v[n] = big_out[n]

    return (loss, grad_x.reshape(x.shape), *[grads[n] for n in names], *[delta[n] for n in names],
            *[new_m[n] for n in names], *[new_v[n] for n in names])
```

```python
import functools

import jax
import jax.numpy as jnp
from jax import lax
from jax.experimental import pallas as pl
from jax.experimental.pallas import tpu as pltpu

F32, BF16 = jnp.float32, jnp.bfloat16
S = jax.ShapeDtypeStruct
MESH = pl.DeviceIdType.MESH

SSD_GROUPS = 8
LRU_C = 8.0
EPS = 1e-6
CONV_WIDTH = 4
ADAM_LR, ADAM_B1, ADAM_B2, ADAM_EPS, ADAM_WD, ADAM_STEP = 0.001, 0.9, 0.999, 1e-08, 0.01, 10

LANES = 128
SUBLANES = 8
VMEM_LIMIT = 56 * 1024 * 1024
N_DEV = 8
SMALL_W = 512
HI = lax.Precision.HIGHEST


def _pcall(body, **kw):
    return pl.pallas_call(body, **kw)


def _cparams(sem=None, **kw):
    return pltpu.CompilerParams(dimension_semantics=sem, vmem_limit_bytes=VMEM_LIMIT, **kw)


def _pick(n, cands):
    for c in cands:
        if c <= n and n % c == 0:
            return c
    return n


def _rt(tr, w, cb=0, n=None):
    if n is None:
        return pl.BlockSpec((tr, w), lambda i: (i, cb))
    return pl.BlockSpec((tr, w), lambda i: (n - 1 - i, cb))


PACKED_ROWS = 16


def _halo_prev(tr, w, cb=0, n=None, rows=SUBLANES):
    k = tr // rows
    if n is None:
        return pl.BlockSpec((rows, w), lambda i: (jnp.maximum(i * k - 1, 0), cb))
    return pl.BlockSpec((rows, w), lambda i: (jnp.maximum((n - 1 - i) * k - 1, 0), cb))


def _halo_next(tr, w, nt, cb=0, n=None, rows=SUBLANES):
    k = tr // rows
    last = nt * k - 1
    if n is None:
        return pl.BlockSpec((rows, w), lambda i: (jnp.minimum((i + 1) * k, last), cb))
    return pl.BlockSpec((rows, w), lambda i: (jnp.minimum((n - i) * k, last), cb))


def _full(a):
    nd = a.ndim
    return pl.BlockSpec(a.shape, lambda i: (0,) * nd)


def _rows_call(name, fn, n_tiles, arrays, in_specs, out_tiled, out_acc, kinds, into=None, deps=(), cw=None):
    n_in, n_t = len(arrays), len(out_tiled)
    n_skip = len(deps) + (0 if into is None else 1)
    width = in_specs[kinds.index('t')].block_shape[1]
    cols = [(0, width)] if cw is None else [(c, cw) for c in range(0, width, cw)]

    def body(*refs):
        i = pl.program_id(0)
        ins = refs[:n_in]
        outs = refs[n_in + n_skip:n_in + n_skip + n_t]
        accs = refs[n_in + n_skip + n_t:]
        if accs:
            @pl.when(i == 0)
            def _():
                for r in accs:
                    r[...] = jnp.zeros_like(r)

        def lanes(ref, rows, c0, w):
            return ref[rows, c0:c0 + w] if ref.shape[-1] == width else ref[rows, :]

        def load(k, c0, w):
            v = lanes(ins[k], slice(None), c0, w).astype(F32)
            if kinds[k][0] in 'pn' and v.shape[0] == PACKED_ROWS:
                v = v[SUBLANES:] if kinds[k][0] == 'p' else v[:SUBLANES]
            return v

        for c0, w in cols:
            touts, aouts = fn(i == 0, i == n_tiles - 1, *[load(k, c0, w) for k in range(n_in)])
            for r, v in zip(outs, touts):
                if r.shape[-1] == width:
                    r[:, c0:c0 + w] = v.astype(r.dtype)
                else:
                    r[...] = v.astype(r.dtype)
            for r, v in zip(accs, aouts):
                if r.shape[-1] == width:
                    r[:, c0:c0 + w] += v
                else:
                    r[...] += v

    out_shape = [S(sh, dt) for sh, dt, _ in out_tiled] + [S(sh, F32) for sh in out_acc]
    out_specs = [sp for _, _, sp in out_tiled]
    for sh in out_acc:
        out_specs.append(pl.BlockSpec(sh, lambda i, nd=len(sh): (0,) * nd))
    in_specs = list(in_specs) + [_ANY] * len(deps)
    if into is None:
        return _pcall(body, name=name, grid=(n_tiles,), in_specs=in_specs, out_specs=out_specs,
                      out_shape=out_shape, compiler_params=_cparams(("arbitrary",)))(*arrays, *deps)
    return _pcall(body, name=name, grid=(n_tiles,), in_specs=in_specs + [_ANY], out_specs=out_specs,
                  out_shape=out_shape, input_output_aliases={n_in + len(deps): 0},
                  compiler_params=_cparams(("arbitrary",)))(*arrays, *deps, into)


def _rms(x, g):
    return x * lax.rsqrt(jnp.mean(x * x, axis=-1, keepdims=True) + EPS) * g


def _rms_bwd(x, g, dy):
    r = lax.rsqrt(jnp.mean(x * x, axis=-1, keepdims=True) + EPS)
    xr = x * r
    gdy = dy * g
    dx = (gdy - xr * jnp.mean(gdy * xr, axis=-1, keepdims=True)) * r
    return dx, _colsum(dy * xr)


_GELU_C0, _GELU_C1 = 0.7978845608028654, 0.044715


def _gelu_and_grad(x):
    x2 = x * x
    t = jnp.tanh(_GELU_C0 * x * (1.0 + _GELU_C1 * x2))
    half = 0.5 * (1.0 + t)
    grad = half + (0.5 * _GELU_C0) * x * (1.0 - t * t) * (1.0 + (3.0 * _GELU_C1) * x2)
    return x * half, grad


def _colsum(v):
    return jnp.sum(v, axis=0, keepdims=True)


_TILES = (1152, 1024, 896, 768, 640, 512, 384, 256, 128)
_K_TILES = (4096, 3456, 3072, 2688, 2048, 1536, 1344, 1152, 1024, 896, 768, 640, 512, 384, 256, 128)


def _mm(name, a, b, *, ta=False, tb=False, outs=((F32, None),), extra=None, out_blocks=None, tm=None, tn=None, tk=None,
        deps=(), a_cols=None):
    M, K = (a.shape[1], a.shape[0]) if ta else a.shape
    if a_cols is not None:
        assert not ta
        K = a_cols[1]
    b3 = b.ndim == 3
    if b3:
        nb_b, brows, bcols = b.shape
        N = brows if tb else nb_b * bcols
    else:
        N = b.shape[0] if tb else b.shape[1]
    n_lim = N if out_blocks is None else N // out_blocks
    if b3 and not tb:
        n_lim = min(n_lim, bcols)
    tm = tm or _pick(M, _TILES[1:])
    tn = tn or _pick(n_lim, _TILES)
    tk = tk or _pick(bcols if (b3 and tb) else K, _K_TILES)
    nk = K // tk
    assert M % tm == 0 and N % tn == 0 and K % tk == 0
    dn = (((0 if ta else 1,), (1 if tb else 0,)), ((), ()))
    n_extra = 0 if extra is None else 1
    n_out = len(outs)

    def body(*refs):
        a_ref, b_ref = refs[0], refs[1]
        e_ref = refs[2] if n_extra else None
        o_refs = refs[2 + n_extra + len(deps):2 + n_extra + len(deps) + n_out]

        def finish(r):
            e = e_ref[...] if n_extra else None
            for o, (_, f) in zip(o_refs, outs):
                o[...] = (r if f is None else f(r, e)).astype(o.dtype)

        part = lax.dot_general(a_ref[...], b_ref[...], dn, preferred_element_type=F32)
        if nk == 1:
            finish(part)
            return
        acc = refs[-1]
        k = pl.program_id(2)

        @pl.when(k == 0)
        def _():
            acc[...] = part

        @pl.when(jnp.logical_and(k > 0, k < nk - 1))
        def _():
            acc[...] += part

        @pl.when(k == nk - 1)
        def _():
            finish(acc[...] + part)

    k0 = 0 if a_cols is None else a_cols[0] // tk
    a_spec = (pl.BlockSpec((tk, tm), lambda i, j, k: (k, i)) if ta
              else pl.BlockSpec((tm, tk), lambda i, j, k: (i, k + k0)))
    if not b3:
        b_spec = pl.BlockSpec((tn, tk), lambda i, j, k: (j, k)) if tb else pl.BlockSpec((tk, tn), lambda i, j, k: (k, j))
    elif tb:
        per = bcols // tk
        b_spec = pl.BlockSpec((None, tn, tk), lambda i, j, k: (k // per, j, k % per))
    else:
        per = bcols // tn
        b_spec = pl.BlockSpec((None, tk, tn), lambda i, j, k: (j // per, k, j % per))
    o_spec = pl.BlockSpec((tm, tn), lambda i, j, k: (i, j))
    if out_blocks is None:
        out_specs, out_shape = [o_spec] * n_out, [S((M, N), dt) for dt, _ in outs]
    else:
        per_o = N // out_blocks // tn
        ob_spec = pl.BlockSpec((None, tm, tn), lambda i, j, k: (j // per_o, i, j % per_o))
        out_specs, out_shape = [ob_spec] * n_out, [S((out_blocks, M, N // out_blocks), dt) for dt, _ in outs]
    in_specs = [a_spec, b_spec] + ([o_spec] if n_extra else []) + [_ANY] * len(deps)
    args = [a, b] + ([extra] if n_extra else []) + list(deps)
    return _pcall(body, name=name, grid=(M // tm, N // tn, nk), in_specs=in_specs, out_specs=out_specs,
                  out_shape=out_shape, scratch_shapes=[pltpu.VMEM((tm, tn), F32)] if nk > 1 else [],
                  compiler_params=_cparams(("parallel", "parallel", "arbitrary")))(*args)


def _shift_down(x, halo, s):
    if s == 0:
        return x
    r = pltpu.roll(x, s, 0)
    hr = pltpu.roll(halo, s, 0)
    row = lax.broadcasted_iota(jnp.int32, halo.shape, 0)
    top = jnp.where(row < s, hr, r[:SUBLANES])
    if x.shape[0] == SUBLANES:
        return top
    return jnp.concatenate([top, r[SUBLANES:]], axis=0)


def _shift_up(x, nxt, s):
    if s == 0:
        return x
    n = x.shape[0]
    r = pltpu.roll(x, n - s, 0)
    nr = pltpu.roll(nxt, SUBLANES - s, 0)
    row = lax.broadcasted_iota(jnp.int32, nxt.shape, 0)
    bot = jnp.where(row >= SUBLANES - s, nr, r[n - SUBLANES:])
    if n == SUBLANES:
        return bot
    return jnp.concatenate([r[:n - SUBLANES], bot], axis=0)


def _conv_taps(x, halo):
    return [_shift_down(x, halo, CONV_WIDTH - 1 - k) for k in range(CONV_WIDTH)]


def _conv_pre(x, halo, w, b, taps=None):
    taps = _conv_taps(x, halo) if taps is None else taps
    acc = b + w[0:1, :] * taps[0]
    for k in range(1, CONV_WIDTH):
        acc = acc + w[k:k + 1, :] * taps[k]
    return acc


def _silu_grad(p):
    s = jax.nn.sigmoid(p)
    return s * (1.0 + p * (1.0 - s))


def _conv_bwd_tile(first, last, x, hprev, xnext, d, dnext, w, b, silu):
    hprev = jnp.where(first, 0.0, hprev)
    taps = _conv_taps(x, hprev)
    if silu:
        d = d * _silu_grad(_conv_pre(x, hprev, w, b, taps))
        pre_next = _conv_pre(xnext, x[x.shape[0] - SUBLANES:], w, b)
        dnext = dnext * _silu_grad(pre_next)
    dnext = jnp.where(last, 0.0, dnext)
    dx = w[CONV_WIDTH - 1:CONV_WIDTH, :] * d
    row8 = lax.broadcasted_iota(jnp.int32, (SUBLANES, x.shape[1]), 0)
    dw8 = jnp.where(row8 == CONV_WIDTH - 1, _colsum(d * x), 0.0)
    for k in range(CONV_WIDTH - 1):
        dx = dx + w[k:k + 1, :] * _shift_up(d, dnext, CONV_WIDTH - 1 - k)
        dw8 = dw8 + jnp.where(row8 == k, _colsum(d * taps[k]), 0.0)
    return dx, dw8, _colsum(d)


def _ssd_dims(xbc_act, n_heads):
    T, XBC = xbc_act.shape
    GN = XBC // 4
    DS = XBC - 2 * GN
    G = SSD_GROUPS
    N = GN // G
    P = DS // n_heads
    K = n_heads // G
    return T, XBC, DS, GN, G, N, P, K


def _ssd_common(dt, alog, Q):
    a = -jnp.exp(alog)
    adt = dt * a
    li = lax.broadcasted_iota(jnp.int32, (Q, Q), 0)
    si = lax.broadcasted_iota(jnp.int32, (Q, Q), 1)
    causal = li >= si
    ltri = causal.astype(F32)
    acs = jnp.dot(ltri, adt, precision=HI, preferred_element_type=F32)
    acs_row = lax.dot_general(adt, ltri, (((0,), (1,)), ((), ())), precision=HI,
                              preferred_element_type=F32)
    return a, adt, causal, ltri, acs, acs_row


def _expander(g, K, P, W):
    r = lax.broadcasted_iota(jnp.int32, (LANES, W), 0)
    c = lax.broadcasted_iota(jnp.int32, (LANES, W), 1)
    return (c // P + g * K == r).astype(F32)


def _dotb(a, b, dn=(((1,), (0,)), ((), ()))):
    return lax.dot_general(a.astype(BF16), b.astype(BF16), dn, preferred_element_type=F32)


def _dot_split(a, sel, terms, dn=(((1,), (0,)), ((), ()))):
    selb = sel.astype(BF16)
    out = None
    for _ in range(terms):
        piece = a.astype(BF16)
        part = lax.dot_general(piece, selb, dn, preferred_element_type=F32)
        out = part if out is None else out + part
        a = a - piece.astype(F32)
    return out


_NT = (((1,), (1,)), ((), ()))
_TN = (((0,), (0,)), ((), ()))


def _ssd_fwd(xbc_act, dt, alog, dskip, n_heads, Q):
    T, XBC, DS, GN, G, N, P, K = _ssd_dims(xbc_act, n_heads)
    W = K * P
    nc = T // Q

    def body(xs_ref, b_ref, c_ref, dt_ref, alog_ref, d_ref, y_ref, hp_ref, h_scr):
        ci = pl.program_id(0)

        @pl.when(ci == 0)
        def _():
            h_scr[...] = jnp.zeros_like(h_scr)

        dtv = dt_ref[...]
        a, adt, causal, ltri, acs, acs_row = _ssd_common(dtv, alog_ref[...], Q)
        lane_head = lax.broadcasted_iota(jnp.int32, (Q, W), 1) // P
        for g in range(G):
            eg = _expander(g, K, P, W)
            dtb = _dot_split(dtv, eg, 3)
            acsb = _dot_split(acs, eg, 3)
            lastb = acsb[Q - 1:Q, :]
            db = _dot_split(jnp.broadcast_to(d_ref[...], (SUBLANES, LANES)), eg, 3)[0:1, :]
            xg = xs_ref[:, g * W:(g + 1) * W]
            bg = b_ref[:, g * N:(g + 1) * N]
            cg = c_ref[:, g * N:(g + 1) * N]
            xt = xg * dtb
            sc = _dotb(cg, bg, _NT)
            yd = jnp.zeros((Q, W), F32)
            for k in range(K):
                h = g * K + k
                seg = acs[:, h:h + 1] - acs_row[h:h + 1, :]
                lh = jnp.where(causal, jnp.exp(jnp.minimum(seg, 0.0)), 0.0)
                xk = jnp.where(lane_head == k, xt, 0.0)
                yd = yd + _dotb(sc * lh, xk)
            hp = h_scr[g]
            yoff = _dotb(cg, hp) * jnp.exp(acsb)
            y_ref[:, g * W:(g + 1) * W] = yd + yoff + xg * db
            e_end = jnp.exp(lastb - acsb)
            st = _dotb(bg, xt * e_end, _TN)
            hp_ref[0, g] = hp
            h_scr[g] = jnp.exp(lastb) * hp + st

    cb = DS // GN
    in_specs = [pl.BlockSpec((Q, DS), lambda c: (c, 0)),
                pl.BlockSpec((Q, GN), lambda c: (c, cb)),
                pl.BlockSpec((Q, GN), lambda c: (c, cb + 1)),
                pl.BlockSpec((Q, LANES), lambda c: (c, 0)),
                pl.BlockSpec((1, LANES), lambda c: (0, 0)),
                pl.BlockSpec((1, LANES), lambda c: (0, 0))]
    out_specs = [pl.BlockSpec((Q, DS), lambda c: (c, 0)),
                 pl.BlockSpec((1, G, N, W), lambda c: (c, 0, 0, 0))]
    return _pcall(body, name="ssd_fwd", grid=(nc,), in_specs=in_specs, out_specs=out_specs,
                  out_shape=[S((T, DS), F32), S((nc, G, N, W), F32)],
                  scratch_shapes=[pltpu.VMEM((G, N, W), F32)],
                  compiler_params=_cparams(("arbitrary",)))(xbc_act, xbc_act, xbc_act, dt, alog, dskip)


def _ssd_bwd(xbc_act, dt, alog, dskip, hprev, dy, n_heads, Q):
    T, XBC, DS, GN, G, N, P, K = _ssd_dims(xbc_act, n_heads)
    W = K * P
    nc = T // Q

    def body(xs_ref, b_ref, c_ref, dt_ref, alog_ref, d_ref, hp_ref, dy_ref,
             dxbc_ref, ddt_ref, dalog_ref, dd_ref, dh_scr):
        ci = pl.program_id(0)

        @pl.when(ci == 0)
        def _():
            dh_scr[...] = jnp.zeros_like(dh_scr)
            dalog_ref[...] = jnp.zeros_like(dalog_ref)
            dd_ref[...] = jnp.zeros_like(dd_ref)

        dtv = dt_ref[...]
        a, adt, causal, ltri, acs, acs_row = _ssd_common(dtv, alog_ref[...], Q)
        lane_head = lax.broadcasted_iota(jnp.int32, (Q, W), 1) // P
        lane128 = lax.broadcasted_iota(jnp.int32, (Q, LANES), 1)
        sub128 = lax.broadcasted_iota(jnp.int32, (LANES, Q), 0)
        rowq = lax.broadcasted_iota(jnp.int32, (Q, W), 0)
        dacs = jnp.zeros((Q, LANES), F32)
        dacs_row = jnp.zeros((LANES, Q), F32)
        ddt = jnp.zeros((Q, LANES), F32)
        dd_acc = jnp.zeros((1, LANES), F32)
        for g in range(G):
            eg = _expander(g, K, P, W)
            dtb = _dot_split(dtv, eg, 3)
            acsb = _dot_split(acs, eg, 3)
            lastb = acsb[Q - 1:Q, :]
            db = _dot_split(jnp.broadcast_to(d_ref[...], (SUBLANES, LANES)), eg, 3)[0:1, :]
            xg = xs_ref[:, g * W:(g + 1) * W]
            bg = b_ref[:, g * N:(g + 1) * N]
            cg = c_ref[:, g * N:(g + 1) * N]
            dyg = dy_ref[:, g * W:(g + 1) * W].astype(F32)
            hp = hp_ref[0, g]
            dhn = dh_scr[g]
            xt = xg * dtb
            sc = _dotb(cg, bg, _NT)
            eacs = jnp.exp(acsb)
            e_end = jnp.exp(lastb - acsb)
            elast = jnp.exp(lastb)

            wv = dyg * eacs
            dcg = _dotb(wv, hp, _NT)
            dhp = _dotb(cg, wv, _TN) + elast * dhn
            dacsb = dyg * (_dotb(cg, hp) * eacs)

            xe = xt * e_end
            dbg = _dotb(xe, dhn, _NT)
            v = _dotb(bg, dhn)
            dxt = v * e_end
            de = v * xe
            dacsb = dacsb - de
            dlastb = _colsum(de) + elast * jnp.sum(dhn * hp, axis=0, keepdims=True)

            dsc = jnp.zeros((Q, Q), F32)
            for k in range(K):
                h = g * K + k
                seg = acs[:, h:h + 1] - acs_row[h:h + 1, :]
                lh = jnp.where(causal, jnp.exp(jnp.minimum(seg, 0.0)), 0.0)
                mh = sc * lh
                dyk = jnp.where(lane_head == k, dyg, 0.0)
                dxt = dxt + jnp.where(lane_head == k, _dotb(mh, dyg, _TN), 0.0)
                dm = _dotb(dyk, xt, _NT)
                dsc = dsc + dm * lh
                gm = dm * mh
                dacs = dacs + jnp.where(lane128 == h, jnp.sum(gm, axis=1, keepdims=True), 0.0)
                dacs_row = dacs_row - jnp.where(sub128 == h, jnp.sum(gm, axis=0, keepdims=True), 0.0)
            dcg = dcg + _dotb(dsc, bg)
            dbg = dbg + _dotb(dsc, cg, _TN)

            dacsb = dacsb + jnp.where(rowq == Q - 1, dlastb, 0.0)
            dacs = dacs + _dot_split(dacsb, eg, 2, _NT)
            ddt = ddt + _dot_split(dxt * xg, eg, 2, _NT)
            dd_acc = dd_acc + _dot_split(jnp.broadcast_to(_colsum(dyg * xg), (SUBLANES, W)), eg, 2, _NT)[0:1, :]
            dxbc_ref[:, g * W:(g + 1) * W] = dxt * dtb + dyg * db
            dxbc_ref[:, DS + g * N:DS + (g + 1) * N] = dbg
            dxbc_ref[:, DS + GN + g * N:DS + GN + (g + 1) * N] = dcg
            dh_scr[g] = dhp

        eye = (lax.broadcasted_iota(jnp.int32, (LANES, LANES), 0) ==
               lax.broadcasted_iota(jnp.int32, (LANES, LANES), 1)).astype(F32)
        dacs = dacs + lax.dot_general(dacs_row, eye, _TN, precision=HI, preferred_element_type=F32)
        dadt = lax.dot_general(ltri, dacs, _TN, precision=HI, preferred_element_type=F32)
        ddt_ref[...] = ddt + dadt * a
        dalog_ref[...] += _colsum(dadt * dtv) * a
        dd_ref[...] += dd_acc

    cb = DS // GN
    rv = lambda c: nc - 1 - c
    in_specs = [pl.BlockSpec((Q, DS), lambda c: (rv(c), 0)),
                pl.BlockSpec((Q, GN), lambda c: (rv(c), cb)),
                pl.BlockSpec((Q, GN), lambda c: (rv(c), cb + 1)),
                pl.BlockSpec((Q, LANES), lambda c: (rv(c), 0)),
                pl.BlockSpec((1, LANES), lambda c: (0, 0)),
                pl.BlockSpec((1, LANES), lambda c: (0, 0)),
                pl.BlockSpec((1, G, N, W), lambda c: (rv(c), 0, 0, 0)),
                pl.BlockSpec((Q, DS), lambda c: (rv(c), 0))]
    out_specs = [pl.BlockSpec((Q, XBC), lambda c: (rv(c), 0)),
                 pl.BlockSpec((Q, LANES), lambda c: (rv(c), 0)),
                 pl.BlockSpec((1, LANES), lambda c: (0, 0)),
                 pl.BlockSpec((1, LANES), lambda c: (0, 0))]
    return _pcall(body, name="ssd_bwd", grid=(nc,), in_specs=in_specs, out_specs=out_specs,
                  out_shape=[S((T, XBC), F32), S((T, LANES), F32), S((1, LANES), F32), S((1, LANES), F32)],
                  scratch_shapes=[pltpu.VMEM((G, N, W), F32)],
                  compiler_params=_cparams(("arbitrary",)))(
                      xbc_act, xbc_act, xbc_act, dt, alog, dskip, hprev, dy)


def _blockdiag(x, w_ref, dn=(((1,), (0,)), ((), ()))):
    H, B, _ = w_ref.shape
    return jnp.concatenate([_dotb(x[:, h * B:(h + 1) * B], w_ref[h], dn) for h in range(H)], axis=1)


def _lru_elem(xl, r_pre, i_pre, lam):
    r = jax.nn.sigmoid(r_pre)
    i = jax.nn.sigmoid(i_pre)
    log_a = -LRU_C * r * jax.nn.softplus(-lam)
    a = jnp.exp(log_a)
    u = jnp.sqrt(1.0 - jnp.exp(2.0 * log_a)) * (i * xl)
    return a, u


def _lru_elem_bwd(xl, r_pre, i_pre, lam, da, du):
    r = jax.nn.sigmoid(r_pre)
    i = jax.nn.sigmoid(i_pre)
    sp = jax.nn.softplus(-lam)
    a = jnp.exp(-LRU_C * r * sp)
    s = jnp.sqrt(1.0 - a * a)
    d_ix = du * s
    dlog_a = (da - du * (i * xl) * a / s) * a
    dr_pre = dlog_a * (-LRU_C) * sp * r * (1.0 - r)
    dlam = _colsum(dlog_a * r) * (LRU_C * jax.nn.sigmoid(-lam))
    di_pre = d_ix * xl * i * (1.0 - i)
    return d_ix * i, dr_pre, di_pre, dlam


def _lru_gates_fwd(xl, w_a, b_a, w_x, b_x, lam, tr):
    T, DL = xl.shape

    def body(xl_ref, wa_ref, ba_ref, wx_ref, bx_ref, lam_ref, a_ref, u_ref):
        x = xl_ref[...]
        r_pre = _blockdiag(x, wa_ref) + ba_ref[...]
        i_pre = _blockdiag(x, wx_ref) + bx_ref[...]
        a, u = _lru_elem(x, r_pre, i_pre, lam_ref[...])
        a_ref[...] = a
        u_ref[...] = u

    w3 = pl.BlockSpec(w_a.shape, lambda i: (0, 0, 0))
    vec = pl.BlockSpec((1, DL), lambda i: (0, 0))
    return _pcall(body, name="lru_gates_fwd", grid=(T // tr,),
                  in_specs=[_rt(tr, DL), w3, vec, w3, vec, vec],
                  out_specs=[_rt(tr, DL), _rt(tr, DL)], out_shape=[S((T, DL), F32), S((T, DL), F32)],
                  compiler_params=_cparams(("parallel",)))(xl, w_a, b_a, w_x, b_x, lam)


def _lru_gates_bwd(xl, w_a, b_a, w_x, b_x, lam, da, du, tr):
    T, DL = xl.shape
    H, B, _ = w_a.shape

    def body(xl_ref, wa_ref, ba_ref, wx_ref, bx_ref, lam_ref, da_ref, du_ref,
             dxl_ref, dwa_ref, dba_ref, dwx_ref, dbx_ref, dlam_ref):
        @pl.when(pl.program_id(0) == 0)
        def _():
            for r in (dwa_ref, dba_ref, dwx_ref, dbx_ref, dlam_ref):
                r[...] = jnp.zeros_like(r)

        x = xl_ref[...]
        r_pre = _blockdiag(x, wa_ref) + ba_ref[...]
        i_pre = _blockdiag(x, wx_ref) + bx_ref[...]
        dx, dr, di, dlam = _lru_elem_bwd(x, r_pre, i_pre, lam_ref[...], da_ref[...], du_ref[...])
        dxl_ref[...] = (dx + _blockdiag(dr, wa_ref, _NT) + _blockdiag(di, wx_ref, _NT)).astype(dxl_ref.dtype)
        for h in range(H):
            xh = x[:, h * B:(h + 1) * B]
            dwa_ref[h] += _dotb(xh, dr[:, h * B:(h + 1) * B], _TN)
            dwx_ref[h] += _dotb(xh, di[:, h * B:(h + 1) * B], _TN)
        dba_ref[...] += _colsum(dr)
        dbx_ref[...] += _colsum(di)
        dlam_ref[...] += dlam

    w3 = pl.BlockSpec(w_a.shape, lambda i: (0, 0, 0))
    vec = pl.BlockSpec((1, DL), lambda i: (0, 0))
    return _pcall(body, name="lru_gates_bwd", grid=(T // tr,),
                  in_specs=[_rt(tr, DL), w3, vec, w3, vec, vec, _rt(tr, DL), _rt(tr, DL)],
                  out_specs=[_rt(tr, DL), w3, vec, w3, vec, vec],
                  out_shape=[S((T, DL), BF16), S(w_a.shape, F32), S((1, DL), F32), S(w_a.shape, F32),
                             S((1, DL), F32), S((1, DL), F32)],
                  compiler_params=_cparams(("arbitrary",)))(xl, w_a, b_a, w_x, b_x, lam, da, du)


def _groups(v):
    return v.reshape(v.shape[0] // SUBLANES, SUBLANES, v.shape[1])


def _rows_shifted(v, edge, up):
    sub = lax.broadcasted_iota(jnp.int32, v.shape, 1)
    if up:
        other = jnp.concatenate([v[1:], edge[None]], axis=0)
        return jnp.where(sub < SUBLANES - 1, pltpu.roll(v, SUBLANES - 1, 1), pltpu.roll(other, SUBLANES - 1, 1))
    other = jnp.concatenate([edge[None], v[:-1]], axis=0)
    return jnp.where(sub >= 1, pltpu.roll(v, 1, 1), pltpu.roll(other, 1, 1))


def _scan_tile(a, u, entering, emit, up):
    G = a.shape[0]
    sub = lax.broadcasted_iota(jnp.int32, a.shape, 1)
    d = 1
    while d < SUBLANES:
        if up:
            keep = sub < SUBLANES - d
            a_s = jnp.where(keep, pltpu.roll(a, SUBLANES - d, 1), 1.0)
            u_s = jnp.where(keep, pltpu.roll(u, SUBLANES - d, 1), 0.0)
        else:
            keep = sub >= d
            a_s = jnp.where(keep, pltpu.roll(a, d, 1), 1.0)
            u_s = jnp.where(keep, pltpu.roll(u, d, 1), 0.0)
        u = a * u_s + u
        a = a * a_s
        d *= 2
    for g in (reversed(range(G)) if up else range(G)):
        hg = u[g] + a[g] * entering
        emit(g, hg)
        entering = hg[0:1] if up else hg[SUBLANES - 1:SUBLANES]
    return entering


def _lru_scan_fwd(a, u, tr, deps=()):
    T, DL = a.shape

    def body(a_ref, u_ref, *rest):
        h_ref, carry = rest[len(deps):]

        @pl.when(pl.program_id(0) == 0)
        def _():
            carry[...] = jnp.zeros_like(carry)

        def emit(g, hg):
            h_ref[g * SUBLANES:(g + 1) * SUBLANES, :] = hg

        last = _scan_tile(_groups(a_ref[...]), _groups(u_ref[...]), carry[0:1, :], emit, up=False)
        carry[...] = jnp.broadcast_to(last, carry.shape)

    return _pcall(body, name="lru_scan_fwd", grid=(T // tr,),
                  in_specs=[_rt(tr, DL), _rt(tr, DL)] + [_ANY] * len(deps),
                  out_specs=_rt(tr, DL), out_shape=S((T, DL), F32),
                  scratch_shapes=[pltpu.VMEM((SUBLANES, DL), F32)],
                  compiler_params=_cparams(("arbitrary",)))(a, u, *deps)


def _lru_scan_bwd(a, h, dh, tr):
    T, DL = a.shape
    n = T // tr

    def body(a_ref, an_ref, h_ref, hp_ref, dh_ref, du_ref, da_ref, carry):
        i = pl.program_id(0)
        ti = n - 1 - i

        @pl.when(i == 0)
        def _():
            carry[...] = jnp.zeros_like(carry)

        a_next = _rows_shifted(_groups(a_ref[...]), jnp.where(ti == n - 1, 0.0, an_ref[...]), up=True)
        h_prev = _rows_shifted(_groups(h_ref[...]), jnp.where(ti == 0, 0.0, hp_ref[...]), up=False)

        def emit(g, gg):
            du_ref[g * SUBLANES:(g + 1) * SUBLANES, :] = gg
            da_ref[g * SUBLANES:(g + 1) * SUBLANES, :] = gg * h_prev[g]

        top = _scan_tile(a_next, _groups(dh_ref[...].astype(F32)), carry[0:1, :], emit, up=True)
        carry[...] = jnp.broadcast_to(top, carry.shape)

    return _pcall(body, name="lru_scan_bwd", grid=(n,),
                  in_specs=[_rt(tr, DL, 0, n), _halo_next(tr, DL, n, 0, n), _rt(tr, DL, 0, n),
                            _halo_prev(tr, DL, 0, n), _rt(tr, DL, 0, n)],
                  out_specs=[_rt(tr, DL, 0, n), _rt(tr, DL, 0, n)],
                  out_shape=[S((T, DL), F32), S((T, DL), F32)],
                  scratch_shapes=[pltpu.VMEM((SUBLANES, DL), F32)],
                  compiler_params=_cparams(("arbitrary",)))(a, a, h, h, dh)


def _adamw(w, g, m, v):
    m = ADAM_B1 * m + (1.0 - ADAM_B1) * g
    v = ADAM_B2 * v + (1.0 - ADAM_B2) * (g * g)
    m_hat = m / (1.0 - ADAM_B1 ** ADAM_STEP)
    v_hat = v / (1.0 - ADAM_B2 ** ADAM_STEP)
    delta = -ADAM_LR * (m_hat / (jnp.sqrt(v_hat) + ADAM_EPS) + ADAM_WD * w)
    return delta, m, v


def _adamw_big(name, w, m, v, own, recv, own_idx):
    _, R, C = w.shape
    n_recv = recv.shape[0]
    tr = _pick(R, (256, 128, 64, 32, 16))

    def body(idx_ref, w_ref, m_ref, v_ref, p_ref, *rest):
        g = p_ref[...].astype(F32)
        for r in rest[:n_recv]:
            g = g + r[...].astype(F32)
        g_ref, d_ref, nm_ref, nv_ref = rest[n_recv:]
        d, nm, nv = _adamw(w_ref[...], g, m_ref[...], v_ref[...])
        g_ref[...] = g
        d_ref[...] = d
        nm_ref[...] = nm
        nv_ref[...] = nv

    r_spec = lambda s: pl.BlockSpec((None, tr, C), lambda i, idx: (s, i, 0))
    t2 = r_spec(0)
    gs = pltpu.PrefetchScalarGridSpec(
        num_scalar_prefetch=1, grid=(R // tr,),
        in_specs=[t2, t2, t2, pl.BlockSpec((None, tr, C), lambda i, idx: (idx[0], i, 0))]
        + [r_spec(s) for s in range(n_recv)],
        out_specs=[t2, t2, t2, t2])
    return _pcall(body, name=name, grid_spec=gs, out_shape=[S((1, R, C), F32)] * 4,
                  compiler_params=_cparams(("parallel",)))(own_idx, w, m, v, own, *([recv] * n_recv))


def _adamw_small(ws, gs, ms, vs):
    n = len(ws)

    def body(*refs):
        for k in range(n):
            d, nm, nv = _adamw(refs[k][...], refs[n + k][...], refs[2 * n + k][...], refs[3 * n + k][...])
            refs[4 * n + k][...] = d
            refs[5 * n + k][...] = nm
            refs[6 * n + k][...] = nv

    res = _pcall(body, name="adamw_small", out_shape=[S(w.shape, F32) for w in ws] * 3,
                 compiler_params=_cparams())(*ws, *gs, *ms, *vs)
    return res[:n], res[n:2 * n], res[2 * n:]


def _sum8(name, parts):
    _, R, C = parts.shape

    def body(p_ref, o_ref):
        acc = p_ref[0]
        for k in range(1, N_DEV):
            acc = acc + p_ref[k]
        o_ref[...] = acc

    return _pcall(body, name=name, out_shape=S((R, C), F32), compiler_params=_cparams())(parts)


def _pair_sum(name, full, recv, c_idx):
    _, R, C = full.shape
    tr = _pick(R, (256, 128, 64, 32, 16))

    def body(c_ref, f_ref, r_ref, o_ref):
        o_ref[...] = (f_ref[...].astype(F32) + r_ref[...].astype(F32)).astype(o_ref.dtype)

    gs = pltpu.PrefetchScalarGridSpec(
        num_scalar_prefetch=1, grid=(4, R // tr),
        in_specs=[pl.BlockSpec((None, tr, C), lambda j, i, c: (2 * j + c[0], i, 0)),
                  pl.BlockSpec((None, tr, C), lambda j, i, c: (j, i, 0))],
        out_specs=pl.BlockSpec((None, tr, C), lambda j, i, c: (j, i, 0)))
    return _pcall(body, name=name, grid_spec=gs, out_shape=S((4, R, C), BF16),
                  compiler_params=_cparams(("parallel", "parallel")))(c_idx, full, recv)


def _cast_bf16(name, w, dev_idx, row0=0, rows=None, deps=()):
    C = w.shape[2]
    R = w.shape[1] if rows is None else rows
    tr = _pick(R, (256, 128, 64, 32, 16))
    b0 = row0 // tr

    def body(d_ref, w_ref, *rest):
        rest[-1][...] = w_ref[...].astype(BF16)

    gs = pltpu.PrefetchScalarGridSpec(
        num_scalar_prefetch=1, grid=(R // tr,),
        in_specs=[pl.BlockSpec((None, tr, C), lambda i, d: (0, i + b0, 0))] + [_ANY] * len(deps),
        out_specs=pl.BlockSpec((None, tr, C), lambda i, d: (d[0], i, 0)))
    return _pcall(body, name=name, grid_spec=gs, out_shape=S((N_DEV, R, C), BF16),
                  compiler_params=_cparams(("parallel",)))(dev_idx, w, *deps)


def _own_block(v, dev):
    return lax.dynamic_update_slice(lax.empty((N_DEV,) + v.shape, v.dtype), v[None], (dev,) + (0,) * v.ndim)


_ANY = pl.BlockSpec(memory_space=pl.ANY)


def _position():
    return lax.axis_index("x"), lax.axis_index("y"), lax.axis_index("c")


def _allgather(name, bufs, deps=()):
    n = len(bufs)
    nd = len(deps)

    def body(*refs):
        outs = refs[n + nd:2 * n + nd]
        send, recv = refs[2 * n + nd:]
        x, y, c = _position()
        me, sib = (x, y, c), (x, y, 1 - c)
        chips = [(1 - x, y), (x, 1 - y), (1 - x, 1 - y)]

        def copy(a, k, block, to):
            bx, by, bc = block
            blk = outs[a].at[4 * bx + 2 * by + bc]
            return pltpu.make_async_remote_copy(
                src_ref=blk, dst_ref=blk, send_sem=send.at[a, k], recv_sem=recv.at[a, k],
                device_id=to, device_id_type=MESH)

        first = []
        for a in range(n):
            first.append(copy(a, 0, me, sib))
            first += [copy(a, 1 + j, me, (*chip, c)) for j, chip in enumerate(chips)]
        for cp in first:
            cp.start()
        passed = []
        for j, chip in enumerate(chips):
            for a in range(n):
                copy(a, 1 + j, (*chip, c), me).wait_recv()
                cp = copy(a, 4 + j, (*chip, c), sib)
                cp.start()
                passed.append(cp)
        for a in range(n):
            copy(a, 0, sib, me).wait_recv()
        for j, chip in enumerate(chips):
            for a in range(n):
                copy(a, 4 + j, (*chip, 1 - c), me).wait_recv()
        for cp in first + passed:
            cp.wait_send()

    return _pcall(body, name=name, in_specs=[_ANY] * (n + nd), out_specs=[_ANY] * n,
                  out_shape=[S(b.shape, b.dtype) for b in bufs], input_output_aliases={a: a for a in range(n)},
                  scratch_shapes=[pltpu.SemaphoreType.DMA((n, 7)), pltpu.SemaphoreType.DMA((n, 7))])(*bufs, *deps)


def _rs_sibling(name, fulls):
    n = len(fulls)

    def body(*refs):
        ins, outs = refs[:n], refs[n:2 * n]
        send, recv = refs[2 * n:]
        x, y, c = _position()
        copies = []
        for a in range(n):
            for j in range(4):
                copies.append(pltpu.make_async_remote_copy(
                    src_ref=ins[a].at[2 * j + (1 - c)], dst_ref=outs[a].at[j], send_sem=send.at[a, j],
                    recv_sem=recv.at[a, j], device_id=(x, y, 1 - c), device_id_type=MESH))
        for cp in copies:
            cp.start()
        for cp in copies:
            cp.wait()

    return _pcall(body, name=name, in_specs=[_ANY] * n, out_specs=[_ANY] * n,
                  out_shape=[S((4,) + f.shape[1:], f.dtype) for f in fulls],
                  scratch_shapes=[pltpu.SemaphoreType.DMA((n, 4)), pltpu.SemaphoreType.DMA((n, 4))])(*fulls)


_HBM = pl.BlockSpec(memory_space=pltpu.HBM)
_SEM = pl.BlockSpec(memory_space=pltpu.SEMAPHORE)
_EFFECT = pltpu.SideEffectType.DATAFLOW_SIDE_EFFECTING


def _remote_copies(copies_fn, srcs, lands, send, recv):
    x, y, c = _position()
    return [pltpu.make_async_remote_copy(src_ref=s, dst_ref=d, send_sem=send[i], recv_sem=recv[i], device_id=to,
                                         device_id_type=MESH)
            for i, (s, d, to) in enumerate(copies_fn(x, y, c, srcs, lands))]


def _split_start(name, srcs, lands, copies_fn, nc, after=()):
    n, nl, na = len(srcs), len(lands), len(after)

    def body(*refs):
        src_refs, land_refs = refs[:n], refs[n:n + nl]
        outs = refs[n + nl + na:]
        for cp in _remote_copies(copies_fn, src_refs, land_refs, outs[:nc], outs[nc:2 * nc]):
            cp.start()
        outs[-1][...] = jnp.zeros_like(outs[-1])

    hbm = lambda a: pltpu.with_memory_space_constraint(a, pltpu.HBM)
    res = _pcall(
        body, name=name, in_specs=[_HBM] * (n + nl) + [_ANY] * na,
        out_specs=[_SEM] * (2 * nc) + [_HBM] * (n + nl) + [pl.BlockSpec(memory_space=pltpu.VMEM)],
        out_shape=[pltpu.SemaphoreType.DMA(())] * (2 * nc) + [pltpu.HBM(s.shape, s.dtype) for s in srcs]
        + [pltpu.HBM(l.shape, l.dtype) for l in lands] + [S((SUBLANES, LANES), F32)],
        input_output_aliases={i: 2 * nc + i for i in range(n + nl)},
        compiler_params=pltpu.CompilerParams(has_side_effects=_EFFECT),
    )(*[hbm(s) for s in srcs], *[hbm(l) for l in lands], *after)
    return res[:2 * nc], res[2 * nc:2 * nc + n], res[2 * nc + n:2 * nc + n + nl], res[-1]


def _split_wait(name, sems, srcs, lands, after, copies_fn, nc):
    n, nl = len(srcs), len(lands)
    after = list(after) if isinstance(after, (list, tuple)) else [after]

    def body(*refs):
        src_refs, land_refs = refs[:n], refs[n:n + nl]
        sem_refs = refs[n + nl:n + nl + 2 * nc]
        for cp in _remote_copies(copies_fn, src_refs, land_refs, sem_refs[:nc], sem_refs[nc:]):
            cp.wait_send()
            cp.wait_recv()

    res = _pcall(
        body, name=name, in_specs=[_HBM] * (n + nl) + [_SEM] * (2 * nc) + [_ANY] * len(after),
        out_specs=[_HBM] * (n + nl), out_shape=[pltpu.HBM(a.shape, a.dtype) for a in list(srcs) + list(lands)],
        input_output_aliases={i: i for i in range(n + nl)},
        compiler_params=pltpu.CompilerParams(has_side_effects=_EFFECT),
    )(*srcs, *lands, *sems, *after)
    return res[:n], res[n:]


def _other_chips(x, y):
    return [(1 - x, y), (x, 1 - y), (1 - x, 1 - y)]


def _ag_copies(x, y, c, srcs, lands):
    out = []
    for land in lands:
        blk = land.at[4 * x + 2 * y + c]
        out.append((blk, blk, (x, y, 1 - c)))
        out += [(blk, blk, (px, py, c)) for px, py in _other_chips(x, y)]
    return out


def _rs_copies(x, y, c, srcs, lands):
    return [(s.at[2 * px + py], land.at[j], (px, py, c))
            for s, land in zip(srcs, lands) for j, (px, py) in enumerate(_other_chips(x, y))]


def _fwd_copies(x, y, c, srcs, lands):
    out = []
    for land in lands:
        for px, py in _other_chips(x, y):
            blk = land.at[4 * px + 2 * py + c]
            out.append((blk, blk, (x, y, 1 - c)))
    return out


def _sib_copies(x, y, c, srcs, lands):
    return [(s.at[2 * j + (1 - c)], land.at[j], (x, y, 1 - c)) for s, land in zip(srcs, lands) for j in range(4)]


def _direct_copies(x, y, c, srcs, lands):
    out = []
    for s, land in zip(srcs, lands):
        for r in range(1, N_DEV):
            px = 1 - x if r & 4 else x
            py = 1 - y if r & 2 else y
            pc = 1 - c if r & 1 else c
            out.append((s.at[4 * px + 2 * py + pc], land.at[r - 1], (px, py, pc)))
    return out


def _ag_finish(name, lands):
    n = len(lands)

    def body(*refs):
        outs = refs[n:2 * n]
        send, recv = refs[2 * n:]
        x, y, c = _position()

        def swap(a, j, px, py, pc):
            blk = outs[a].at[4 * px + 2 * py + pc]
            return pltpu.make_async_remote_copy(src_ref=blk, dst_ref=blk, send_sem=send.at[a, j], recv_sem=recv.at[a, j],
                                                device_id=(x, y, 1 - c), device_id_type=MESH)

        chips = _other_chips(x, y)
        sends = [swap(a, j, px, py, c) for a in range(n) for j, (px, py) in enumerate(chips)]
        for cp in sends:
            cp.start()
        for a in range(n):
            for j, (px, py) in enumerate(chips):
                swap(a, j, px, py, 1 - c).wait_recv()
        for cp in sends:
            cp.wait_send()

    return _pcall(body, name=name, in_specs=[_ANY] * n, out_specs=[_ANY] * n,
                  out_shape=[S(l.shape, l.dtype) for l in lands], input_output_aliases={a: a for a in range(n)},
                  scratch_shapes=[pltpu.SemaphoreType.DMA((n, 3)), pltpu.SemaphoreType.DMA((n, 3))])(*lands)


def _pad_lanes(v):
    return jnp.pad(v, ((0, 0), (0, LANES - v.shape[1])))


def _flat_rows(pieces):
    flat = jnp.concatenate([p.reshape(-1) for p in pieces])
    rows = -(-flat.shape[0] // (SMALL_W * SUBLANES)) * SUBLANES
    return jnp.pad(flat, (0, rows * SMALL_W - flat.shape[0])).reshape(rows, SMALL_W)


def _unflat(buf, shapes):
    flat = buf.reshape(-1)
    out, off = [], 0
    for sh in shapes:
        n = 1
        for d in sh:
            n *= d
        out.append(flat[off:off + n].reshape(sh))
        off += n
    return out


def kernel(x, pre_mix_norm, w_in, ssd_conv_w, ssd_conv_b, ssd_dt_bias, ssd_a_log, ssd_d, ssd_norm, lru_conv_w, lru_conv_b, lru_w_a, lru_b_a, lru_w_x, lru_b_x, lru_lambda, lru_norm, w_out, post_mix_norm, pre_mlp_norm, w_mlp_in, w_mlp_out, post_mlp_norm, loss_target, m_pre_mix_norm, m_w_in, m_ssd_conv_w, m_ssd_conv_b, m_ssd_dt_bias, m_ssd_a_log, m_ssd_d, m_ssd_norm, m_lru_conv_w, m_lru_conv_b, m_lru_w_a, m_lru_b_a, m_lru_w_x, m_lru_b_x, m_lru_lambda, m_lru_norm, m_w_out, m_post_mix_norm, m_pre_mlp_norm, m_w_mlp_in, m_w_mlp_out, m_post_mlp_norm, v_pre_mix_norm, v_w_in, v_ssd_conv_w, v_ssd_conv_b, v_ssd_dt_bias, v_ssd_a_log, v_ssd_d, v_ssd_norm, v_lru_conv_w, v_lru_conv_b, v_lru_w_a, v_lru_b_a, v_lru_w_x, v_lru_b_x, v_lru_lambda, v_lru_norm, v_w_out, v_post_mix_norm, v_pre_mlp_norm, v_w_mlp_in, v_w_mlp_out, v_post_mlp_norm):
    names = ['pre_mix_norm', 'w_in', 'ssd_conv_w', 'ssd_conv_b', 'ssd_dt_bias', 'ssd_a_log', 'ssd_d', 'ssd_norm',
             'lru_conv_w', 'lru_conv_b', 'lru_w_a', 'lru_b_a', 'lru_w_x', 'lru_b_x', 'lru_lambda', 'lru_norm',
             'w_out', 'post_mix_norm', 'pre_mlp_norm', 'w_mlp_in', 'w_mlp_out', 'post_mlp_norm']
    loc = locals()
    W = {n: loc[n] for n in names}
    Mo = {n: loc["m_" + n] for n in names}
    Vo = {n: loc["v_" + n] for n in names}
    big = ['w_in', 'w_out', 'w_mlp_in', 'w_mlp_out']

    px, py, pc = _position()
    dev = 4 * px + 2 * py + pc
    dev_idx = jnp.reshape(dev, (1,)).astype(jnp.int32)
    c_idx = jnp.reshape(pc, (1,)).astype(jnp.int32)
    chip_idx = jnp.reshape(2 * px + py, (1,)).astype(jnp.int32)

    _, T, D = x.shape
    x2 = x.reshape(T, D)
    tgt = loss_target.reshape(T, D)
    n_heads = ssd_dt_bias.shape[1]
    XBC = ssd_conv_b.shape[1]
    GN = XBC // 4
    DS = XBC - 2 * GN
    DL = lru_norm.shape[1]
    DFF = w_mlp_in.shape[2] * N_DEV
    DIN = w_in.shape[2] * N_DEV
    NP = XBC + DS + 2 * DL + LANES
    assert DS % GN == 0 and XBC % DS == 0 and DS == DL and n_heads <= LANES
    cb_z, cb_gate, cb_xl, cb_dt = XBC // DS, XBC // DS + 1, XBC // DS + 2, (XBC + DS + 2 * DL) // LANES
    tr = min(256, T // 2)
    nt = T // tr
    Q = min(256, T // 2)

    Dh = D // 2
    sh_a = _cast_bf16("cast_w_in_a", W['w_in'], dev_idx, 0, Dh)
    sh_b = _cast_bf16("cast_w_in_b", W['w_in'], dev_idx, Dh, Dh)
    later = big[1:]
    a_bufs = [sh_a, _own_block(ssd_conv_w[0], dev), _own_block(lru_conv_w[0], dev)]
    a_sems, _, a_lands, a_token = _split_start("allgather_w_in_a_start", [], a_bufs, _ag_copies, 4 * len(a_bufs))
    sh = {n: _cast_bf16("cast_" + n, W[n], dev_idx, deps=[a_token]) for n in later}

    def f_norm_in(first, last, xv, g):
        return (_rms(xv, g),), ()
    (h,) = _rows_call("norm_in", f_norm_in, nt, [x2, pre_mix_norm], [_rt(tr, D), _full(pre_mix_norm)],
                      [((T, D), BF16, _rt(tr, D))], [], 'tf', deps=[a_token])
    Mo['w_in'] = Mo['w_in'] + a_token[0, 0]
    Vo['w_in'] = Vo['w_in'] + a_token[0, 0]
    _, a_lands = _split_wait("allgather_w_in_a_wait", a_sems, [], a_lands,
                             [h, Mo['w_in'], Vo['w_in']] + [sh[n] for n in later], _ag_copies, 4 * len(a_bufs))
    g_in_a, g_cs, g_cl = _ag_finish("allgather_w_in_a_finish", a_lands)
    b_sems, _, b_lands, b_token = _split_start("allgather_w_in_b_start", [], [sh_b], _ag_copies, 4, after=[g_in_a])
    ag_sems, ag_srcs, ag_lands, ag_token = _split_start(
        "allgather_later_start", [], [sh[n] for n in later], _ag_copies, 4 * len(later), after=[b_token])
    conv_s = jnp.transpose(g_cs, (1, 0, 2)).reshape(CONV_WIDTH, XBC)
    conv_l = jnp.transpose(g_cl, (1, 0, 2)).reshape(CONV_WIDTH, DL)
    wb = DIN // N_DEV
    o_z, o_xbc, o_dt, o_gate, o_xl = 0, DS, DS + XBC, DS + XBC + n_heads, DS + XBC + n_heads + DL
    segs = [(o_xbc, o_xbc + XBC, 0), (o_z, o_z + DS, XBC), (o_gate, o_gate + DL, XBC + DS),
            (o_xl, o_xl + DL, XBC + DS + DL), (o_dt, o_dt + n_heads, NP - LANES)]

    def ref_cols(g, lo, hi):
        out = []
        while lo < hi:
            k = lo // wb
            e = min(hi, (k + 1) * wb)
            out.append(g[k, :, lo - k * wb:e - k * wb])
            lo = e
        return out

    def laid_out(g):
        return jnp.concatenate([p for a, b, _ in segs for p in ref_cols(g, a, b)]
                               + [jnp.zeros((g.shape[1], LANES - n_heads), BF16)], axis=1)

    def my_cols(g, lo, hi):
        out = []
        for a, b, m in sorted(segs):
            s, e = max(lo, a), min(hi, b)
            if s < e:
                out.append(g[:, m + s - a:m + e - a])
        return out

    wp_a = laid_out(g_in_a)
    dt_bias = _pad_lanes(ssd_dt_bias)
    a_log = _pad_lanes(ssd_a_log)
    d_skip = _pad_lanes(ssd_d)
    wa_b, wx_b = lru_w_a[0].astype(BF16), lru_w_x[0].astype(BF16)
    b_a, b_x = lru_b_a.reshape(1, DL), lru_b_x.reshape(1, DL)

    (proj_a,) = _mm("proj_a", h, wp_a, a_cols=(0, Dh), outs=((BF16, None),), deps=[ag_token])
    (dt_a,) = _mm("proj_dt_a", h, wp_a[:, NP - LANES:], a_cols=(0, Dh))
    _, b_lands = _split_wait("allgather_w_in_b_wait", b_sems, [], b_lands, proj_a, _ag_copies, 4)
    (g_in_b,) = _ag_finish("allgather_w_in_b_finish", b_lands)
    wp_b = laid_out(g_in_b)
    add = lambda r, e: r + e.astype(F32)
    (proj,) = _mm("proj_b", h, wp_b, a_cols=(Dh, Dh), extra=proj_a, outs=((BF16, add),))
    (dt_raw,) = _mm("proj_dt_b", h, wp_b[:, NP - LANES:], a_cols=(Dh, Dh), extra=dt_a, outs=((F32, add),))

    cwx = min(1024, XBC)

    def f_ssd_pre(first, last, xbc, halo, w, b):
        pre = _conv_pre(xbc, jnp.where(first, 0.0, halo), w, b)
        return (pre * jax.nn.sigmoid(pre),), ()
    (xbc_act,) = _rows_call(
        "ssd_pre", f_ssd_pre, nt, [proj, proj, conv_s, ssd_conv_b],
        [_rt(tr, XBC), _halo_prev(tr, XBC, rows=PACKED_ROWS), _full(conv_s), _full(ssd_conv_b)],
        [((T, XBC), F32, _rt(tr, XBC))], [], ['t', 'p0', 'f', 'f'], cw=cwx)

    def f_ssd_dt(first, last, dtr, dtb):
        return (jax.nn.softplus(dtr + dtb),), ()
    (dt,) = _rows_call("ssd_dt", f_ssd_dt, nt, [dt_raw, dt_bias], [_rt(tr, LANES), _full(dt_bias)],
                       [((T, LANES), F32, _rt(tr, LANES))], [], 'tf')

    y_ssd, h_prev = _ssd_fwd(xbc_act, dt, a_log, d_skip, n_heads, Q)

    gw = DS // SSD_GROUPS

    def ssd_post(y, z, g):
        yz = y * jax.nn.silu(z)
        parts = []
        for k in range(SSD_GROUPS):
            yk = yz[:, k * gw:(k + 1) * gw]
            parts.append(yk * lax.rsqrt(jnp.mean(yk * yk, axis=-1, keepdims=True) + EPS))
        return jnp.concatenate(parts, axis=-1) * g

    def f_ssd_post(first, last, y, z, g):
        return (ssd_post(y, z, g),), ()
    (mixcat,) = _rows_call("ssd_post", f_ssd_post, nt, [y_ssd, proj, ssd_norm],
                           [_rt(tr, DS), _rt(tr, DS, cb_z), _full(ssd_norm)], [((T, DS + DL), BF16, _rt(tr, DS))], [],
                           'ttf')

    def f_lru_pre(first, last, xv, halo, w, b):
        return (_conv_pre(xv, jnp.where(first, 0.0, halo), w, b),), ()
    (xl,) = _rows_call("lru_pre", f_lru_pre, nt, [proj, proj, conv_l, lru_conv_b],
                       [_rt(tr, DL, cb_xl), _halo_prev(tr, DL, cb_xl, rows=PACKED_ROWS), _full(conv_l),
                        _full(lru_conv_b)],
                       [((T, DL), F32, _rt(tr, DL))], [], ['t', 'p0', 'f', 'f'])

    a_lru, u_lru = _lru_gates_fwd(xl, wa_b, b_a, wx_b, b_x, lru_lambda, tr)
    h_lru = _lru_scan_fwd(a_lru, u_lru, tr)
    _, ag_lands = _split_wait("allgather_later_wait", ag_sems, ag_srcs, ag_lands, h_lru, _ag_copies, 4 * len(later))
    f_sems, _, ag_lands, f_token = _split_start("allgather_later_fwd_start", [], ag_lands, _fwd_copies, 3 * len(later))

    def lru_post(hv, gate, g):
        return _rms(hv * jax.nn.gelu(gate), g)

    def f_lru_post(first, last, hv, gate, g):
        return (lru_post(hv, gate, g),), ()
    cb_l = DS // DL
    (mixcat,) = _rows_call("lru_post", f_lru_post, nt, [h_lru, proj, lru_norm],
                           [_rt(tr, DL), _rt(tr, DL, cb_gate), _full(lru_norm)],
                           [((T, DS + DL), BF16, _rt(tr, DL, cb_l))], [], 'ttf', into=mixcat, deps=[f_token])

    _, (g_out, g_mi, g_mo) = _split_wait("allgather_later_fwd_wait", f_sems, [], ag_lands, mixcat, _fwd_copies,
                                         3 * len(later))
    w_out_f = g_out.reshape(DS + DL, D)
    w_mi_f = jnp.transpose(g_mi, (1, 0, 2)).reshape(D, DFF)
    w_mo_f = g_mo.reshape(DFF, D)
    (mix,) = _mm("mix", mixcat, w_out_f, outs=((BF16, None),))

    def f_post_mix(first, last, xv, mx, gpm, gpl):
        x1 = xv + _rms(mx, gpm)
        return (x1, _rms(x1, gpl)), ()
    x1, hn = _rows_call("post_mix", f_post_mix, nt, [x2, mix, post_mix_norm, pre_mlp_norm],
                        [_rt(tr, D), _rt(tr, D), _full(post_mix_norm), _full(pre_mlp_norm)],
                        [((T, D), F32, _rt(tr, D)), ((T, D), BF16, _rt(tr, D))], [], 'ttff')

    hm, act = _mm("mlp_in", hn, w_mi_f,
                  outs=((BF16, None), (BF16, lambda r, e: jnp.square(jnp.maximum(r, 0.0)))))
    (hm2,) = _mm("mlp_out", act, w_mo_f, outs=((BF16, None),))

    def f_final(first, last, x1v, hm2v, g, tg):
        err = x1v + _rms(hm2v, g) - tg
        dx2 = err * (1.0 / D)
        dh, dg = _rms_bwd(hm2v, g, dx2)
        loss = jnp.full((1, LANES), 0.5 / D, F32) * jnp.sum(err * err)
        return (dx2, dh), (dg, loss)
    dx1a, dhm2, g_post_mlp, loss_part = _rows_call(
        "loss_head", f_final, nt, [x1, hm2, post_mlp_norm, tgt],
        [_rt(tr, D), _rt(tr, D), _full(post_mlp_norm), _rt(tr, D)],
        [((T, D), BF16, _rt(tr, D)), ((T, D), BF16, _rt(tr, D))], [(1, D), (1, LANES)], 'ttft')

    def rs_sibling_begin(n, full):
        sems, srcs, lands, token = _split_start("rs_sibling_start_" + n, [full],
                                                [lax.empty((4,) + full.shape[1:], BF16)], _sib_copies, 4)
        return (sems, srcs, lands), token

    def rs_begin(n, state, after):
        (full,), (from_sib,) = _split_wait("rs_sibling_wait_" + n, *state, after, _sib_copies, 4)
        pair = _pair_sum("pair_sum_" + n, full, from_sib, c_idx)
        sems, srcs, lands, token = _split_start("rs_start_" + n, [pair], [lax.empty((3,) + pair.shape[1:], BF16)],
                                                _rs_copies, 3)
        return (sems, srcs, lands), token

    def rs_end(n, state, after):
        (pair,), (recv,) = _split_wait("rs_wait_" + n, *state, after, _rs_copies, 3)
        return pair, recv, chip_idx

    def rs_direct_begin(n, full):
        sems, srcs, lands, token = _split_start("rs_start_" + n, [full],
                                                [lax.empty((N_DEV - 1,) + full.shape[1:], BF16)], _direct_copies,
                                                N_DEV - 1)
        return (sems, srcs, lands), token

    def rs_direct_end(n, state, after):
        (full,), (recv,) = _split_wait("rs_wait_" + n, *state, after, _direct_copies, N_DEV - 1)
        return full, recv, dev_idx

    (gw_mo,) = _mm("dw_mlp_out", act, dhm2, ta=True, outs=((BF16, None),))
    rs_mo, tok = rs_direct_begin('w_mlp_out', gw_mo.reshape(N_DEV, DFF // N_DEV, D))
    (dhm,) = _mm("d_mlp_act", dhm2, w_mo_f, tb=True, extra=hm,
                 outs=((BF16, lambda r, e: r * (2.0 * jnp.maximum(e.astype(F32), 0.0))),), deps=[tok])
    (gw_mi,) = _mm("dw_mlp_in", hn, dhm, ta=True, outs=((BF16, None),), out_blocks=N_DEV)
    rs_mi, tok = rs_direct_begin('w_mlp_in', gw_mi)
    (dhn,) = _mm("d_mlp_in", dhm, w_mi_f, tb=True, outs=((BF16, None),), deps=[tok])

    def f_post_mix_bwd(first, last, x1v, mx, gpm, gpl, dhnv, dxa):
        dx1, dgpl = _rms_bwd(x1v, gpl, dhnv)
        dx1 = dx1 + dxa
        dmx, dgpm = _rms_bwd(mx, gpm, dx1)
        return (dx1, dmx), (dgpl, dgpm)
    dx1, dmix, g_pre_mlp, g_post_mix = _rows_call(
        "post_mix_bwd", f_post_mix_bwd, nt, [x1, mix, post_mix_norm, pre_mlp_norm, dhn, dx1a],
        [_rt(tr, D), _rt(tr, D), _full(post_mix_norm), _full(pre_mlp_norm), _rt(tr, D), _rt(tr, D)],
        [((T, D), BF16, _rt(tr, D)), ((T, D), BF16, _rt(tr, D))], [(1, D), (1, D)], 'ttfftt')

    (gw_out,) = _mm("dw_out", mixcat, dmix, ta=True, outs=((BF16, None),))
    rs_out, tok = rs_direct_begin('w_out', gw_out.reshape(N_DEV, -1, D))
    (dmixcat,) = _mm("d_mix", dmix, w_out_f, tb=True, outs=((BF16, None),), deps=[tok])

    def f_lru_post_bwd(first, last, hv, gate, g, dy):
        gl, dgl = _gelu_and_grad(gate)
        dv, dg = _rms_bwd(hv * gl, g, dy)
        return (dv * hv * dgl, dv * gl), (dg,)
    dproj, dh_lru, g_lru_norm = _rows_call(
        "lru_post_bwd", f_lru_post_bwd, nt, [h_lru, proj, lru_norm, dmixcat],
        [_rt(tr, DL), _rt(tr, DL, cb_gate), _full(lru_norm), _rt(tr, DL, cb_l)],
        [((T, NP), BF16, _rt(tr, DL, cb_gate)), ((T, DL), BF16, _rt(tr, DL))], [(1, DL)], 'ttft')

    du_lru, da_lru = _lru_scan_bwd(a_lru, h_lru, dh_lru, tr)
    dxl, g_wa, g_ba, g_wx, g_bx, g_lam = _lru_gates_bwd(xl, wa_b, b_a, wx_b, b_x, lru_lambda, da_lru, du_lru, tr)

    conv_bwd_kinds = ['t', 'p0', 'n0', 't', 'n3', 'f', 'f']

    def f_lru_pre_bwd(first, last, xv, hp, xn, d, dn, w, b):
        dx, dw8, db = _conv_bwd_tile(first, last, xv, hp, xn, d, dn, w, b, silu=False)
        return (dx,), (dw8, db)
    dproj, g_convl8, g_convl_b = _rows_call(
        "lru_pre_bwd", f_lru_pre_bwd, nt, [proj, proj, proj, dxl, dxl, conv_l, lru_conv_b],
        [_rt(tr, DL, cb_xl), _halo_prev(tr, DL, cb_xl, rows=PACKED_ROWS),
         _halo_next(tr, DL, nt, cb_xl, rows=PACKED_ROWS), _rt(tr, DL),
         _halo_next(tr, DL, nt, rows=PACKED_ROWS), _full(conv_l), _full(lru_conv_b)],
        [((T, NP), BF16, _rt(tr, DL, cb_xl))], [(SUBLANES, DL), (1, DL)], conv_bwd_kinds, into=dproj)

    def f_ssd_post_bwd(first, last, y, z, g, dy):
        s = jax.nn.sigmoid(z)
        sz = z * s
        yz = y * sz
        gdy = dy * g
        dyz, yn = [], []
        for k in range(SSD_GROUPS):
            cols = slice(k * gw, (k + 1) * gw)
            r = lax.rsqrt(jnp.mean(yz[:, cols] * yz[:, cols], axis=-1, keepdims=True) + EPS)
            xr = yz[:, cols] * r
            dyz.append((gdy[:, cols] - xr * jnp.mean(gdy[:, cols] * xr, axis=-1, keepdims=True)) * r)
            yn.append(xr)
        dyz = jnp.concatenate(dyz, axis=-1)
        dz = dyz * y * (s + sz * (1.0 - s))
        return (dz, dyz * sz), (_colsum(dy * jnp.concatenate(yn, axis=-1)),)
    dproj, dy_ssd, g_ssd_norm = _rows_call(
        "ssd_post_bwd", f_ssd_post_bwd, nt, [y_ssd, proj, ssd_norm, dmixcat],
        [_rt(tr, DS), _rt(tr, DS, cb_z), _full(ssd_norm), _rt(tr, DS, 0)],
        [((T, NP), BF16, _rt(tr, DS, cb_z)), ((T, DS), BF16, _rt(tr, DS))], [(1, DS)], 'ttft', into=dproj)

    dxbc_act, ddt, g_alog, g_dskip = _ssd_bwd(xbc_act, dt, a_log, d_skip, h_prev, dy_ssd, n_heads, Q)

    def f_ssd_pre_bwd(first, last, xv, hp, xn, d, dn, w, b):
        dx, dw8, db = _conv_bwd_tile(first, last, xv, hp, xn, d, dn, w, b, silu=True)
        return (dx,), (dw8, db)
    dproj, g_convs8, g_convs_b = _rows_call(
        "ssd_pre_bwd", f_ssd_pre_bwd, nt, [proj, proj, proj, dxbc_act, dxbc_act, conv_s, ssd_conv_b],
        [_rt(tr, XBC), _halo_prev(tr, XBC, rows=PACKED_ROWS), _halo_next(tr, XBC, nt, rows=PACKED_ROWS),
         _rt(tr, XBC), _halo_next(tr, XBC, nt),
         _full(conv_s), _full(ssd_conv_b)],
        [((T, NP), BF16, _rt(tr, XBC))], [(SUBLANES, XBC), (1, XBC)], conv_bwd_kinds, into=dproj, cw=cwx)

    def f_ssd_dt_bwd(first, last, ddtv, dtr, dtb):
        ddtr = ddtv * jax.nn.sigmoid(dtr + dtb)
        return (ddtr,), (_colsum(ddtr),)
    dproj, g_dtb = _rows_call(
        "ssd_dt_bwd", f_ssd_dt_bwd, nt, [ddt, dt_raw, dt_bias],
        [_rt(tr, LANES), _rt(tr, LANES), _full(dt_bias)],
        [((T, NP), BF16, _rt(tr, LANES, cb_dt))], [(1, LANES)], 'ttf', into=dproj)
    small = {
        'ssd_conv_w': g_convs8[:CONV_WIDTH], 'ssd_conv_b': g_convs_b,
        'ssd_dt_bias': g_dtb[:, :n_heads], 'ssd_a_log': g_alog[:, :n_heads], 'ssd_d': g_dskip[:, :n_heads],
        'ssd_norm': g_ssd_norm, 'lru_conv_w': g_convl8[:CONV_WIDTH], 'lru_conv_b': g_convl_b,
        'lru_w_a': g_wa, 'lru_b_a': g_ba, 'lru_w_x': g_wx, 'lru_b_x': g_bx, 'lru_lambda': g_lam,
        'lru_norm': g_lru_norm, 'post_mix_norm': g_post_mix, 'pre_mlp_norm': g_pre_mlp,
        'post_mlp_norm': g_post_mlp, 'loss': loss_part[:, :1],
    }
    wide = ['lru_w_a', 'lru_w_x']
    narrow = [n for n in small if n not in wide]
    lb = lru_w_a.shape[-1]
    s_srcs = [_flat_rows([small[n] for n in narrow]), g_wa.reshape(-1, lb), g_wx.reshape(-1, lb)]
    s_sems, s_srcs, s_lands, tok = _split_start(
        "small_grads_start", [], [_own_block(a, dev) for a in s_srcs], _ag_copies, 4 * len(s_srcs))

    (gwp,) = _mm("dw_proj", h, dproj, ta=True, outs=((BF16, None),), deps=[tok])
    sib_in, tok = rs_sibling_begin(
        'w_in', jnp.stack([jnp.concatenate(my_cols(gwp, k * wb, (k + 1) * wb), axis=1) for k in range(N_DEV)]))
    (dh_a,) = _mm("d_proj_a", dproj, wp_a, tb=True, outs=((BF16, None),), deps=[tok])
    rs_in, tok = rs_begin('w_in', sib_in, dh_a)
    (dh_b,) = _mm("d_proj_b", dproj, wp_b, tb=True, outs=((BF16, None),), deps=[tok])

    def f_norm_in_bwd(first, last, xv, g, dha, dhb, dxa):
        dx, dg = _rms_bwd(xv, g, jnp.concatenate([dha, dhb], axis=1))
        return (dx + dxa,), (dg,)
    grad_x, g_pre_mix = _rows_call(
        "norm_in_bwd", f_norm_in_bwd, nt, [x2, pre_mix_norm, dh_a, dh_b, dx1],
        [_rt(tr, D), _full(pre_mix_norm), _rt(tr, Dh), _rt(tr, Dh), _rt(tr, D)], [((T, D), F32, _rt(tr, D))],
        [(1, D)], 'tfttt')

    big_out = {}
    for n, state in (('w_mlp_out', rs_mo), ('w_mlp_in', rs_mi), ('w_out', rs_out)):
        big_out[n] = _adamw_big("adamw_" + n, W[n], Mo[n], Vo[n], *rs_direct_end(n, state, grad_x))

    (g_pm8,) = _allgather("allgather_pre_mix_grad", [_own_block(g_pre_mix, dev)], deps=[big_out['w_out'][0]])
    _, s_lands = _split_wait("small_grads_wait", s_sems, s_srcs, s_lands, g_pm8, _ag_copies, 4 * len(s_lands))
    g_narrow, g_wa8, g_wx8 = _ag_finish("small_grads_finish", s_lands)
    summed = dict(zip(narrow, _unflat(_sum8("sum_small_grads", g_narrow), [small[n].shape for n in narrow])))
    summed['pre_mix_norm'] = _sum8("sum_pre_mix_grad", g_pm8)
    summed['lru_w_a'] = _sum8("sum_lru_w_a_grads", g_wa8)
    summed['lru_w_x'] = _sum8("sum_lru_w_x_grads", g_wx8)
    loss = summed.pop('loss').reshape(())
    for n, full_w in (('ssd_conv_w', XBC), ('lru_conv_w', DL)):
        wdt = full_w // N_DEV
        summed[n] = lax.dynamic_slice_in_dim(summed[n], dev * wdt, wdt, axis=1)
    small_params = [n for n in names if n not in big]
    as2d = lambda a: a.reshape(-1, a.shape[-1])
    res = _adamw_small([as2d(W[n]) for n in small_params],
                       [summed[n].reshape(as2d(W[n]).shape) for n in small_params],
                       [as2d(Mo[n]) for n in small_params], [as2d(Vo[n]) for n in small_params])
    grads = {n: summed[n].reshape(W[n].shape) for n in small_params}
    delta, new_m, new_v = ({n: r.reshape(W[n].shape) for n, r in zip(small_params, rs)} for rs in res)

    big_out['w_in'] = _adamw_big("adamw_w_in", W['w_in'], Mo['w_in'], Vo['w_in'], *rs_end('w_in', rs_in, g_pm8))
    for n in big:
        grads[n], delta[n], new_m[n], new_v[n] = big_out[n]

    return (loss, grad_x.reshape(x.shape), *[grads[n] for n in names], *[delta[n] for n in names],
            *[new_m[n] for n in names], *[new_v[n] for n in names])
```

```python
import functools

import jax
import jax.numpy as jnp
from jax import lax
from jax.experimental import pallas as pl
from jax.experimental.pallas import tpu as pltpu

F32, BF16 = jnp.float32, jnp.bfloat16
S = jax.ShapeDtypeStruct
MESH = pl.DeviceIdType.MESH

SSD_GROUPS = 8
LRU_C = 8.0
EPS = 1e-6
CONV_WIDTH = 4
ADAM_LR, ADAM_B1, ADAM_B2, ADAM_EPS, ADAM_WD, ADAM_STEP = 0.001, 0.9, 0.999, 1e-08, 0.01, 10

LANES = 128
SUBLANES = 8
VMEM_LIMIT = 56 * 1024 * 1024
N_DEV = 8
SMALL_W = 512
HI = lax.Precision.HIGHEST


def _pcall(body, **kw):
    return pl.pallas_call(body, **kw)


def _cparams(sem=None, **kw):
    return pltpu.CompilerParams(dimension_semantics=sem, vmem_limit_bytes=VMEM_LIMIT, **kw)


def _pick(n, cands):
    for c in cands:
        if c <= n and n % c == 0:
            return c
    return n


def _rt(tr, w, cb=0, n=None):
    if n is None:
        return pl.BlockSpec((tr, w), lambda i: (i, cb))
    return pl.BlockSpec((tr, w), lambda i: (n - 1 - i, cb))


PACKED_ROWS = 16


def _halo_prev(tr, w, cb=0, n=None, rows=SUBLANES):
    k = tr // rows
    if n is None:
        return pl.BlockSpec((rows, w), lambda i: (jnp.maximum(i * k - 1, 0), cb))
    return pl.BlockSpec((rows, w), lambda i: (jnp.maximum((n - 1 - i) * k - 1, 0), cb))


def _halo_next(tr, w, nt, cb=0, n=None, rows=SUBLANES):
    k = tr // rows
    last = nt * k - 1
    if n is None:
        return pl.BlockSpec((rows, w), lambda i: (jnp.minimum((i + 1) * k, last), cb))
    return pl.BlockSpec((rows, w), lambda i: (jnp.minimum((n - i) * k, last), cb))


def _full(a):
    nd = a.ndim
    return pl.BlockSpec(a.shape, lambda i: (0,) * nd)


def _rows_call(name, fn, n_tiles, arrays, in_specs, out_tiled, out_acc, kinds, into=None, deps=(), cw=None):
    n_in, n_t = len(arrays), len(out_tiled)
    n_skip = len(deps) + (0 if into is None else 1)
    width = in_specs[kinds.index('t')].block_shape[1]
    cols = [(0, width)] if cw is None else [(c, cw) for c in range(0, width, cw)]

    def body(*refs):
        i = pl.program_id(0)
        ins = refs[:n_in]
        outs = refs[n_in + n_skip:n_in + n_skip + n_t]
        accs = refs[n_in + n_skip + n_t:]
        if accs:
            @pl.when(i == 0)
            def _():
                for r in accs:
                    r[...] = jnp.zeros_like(r)

        def lanes(ref, rows, c0, w):
            return ref[rows, c0:c0 + w] if ref.shape[-1] == width else ref[rows, :]

        def load(k, c0, w):
            v = lanes(ins[k], slice(None), c0, w).astype(F32)
            if kinds[k][0] in 'pn' and v.shape[0] == PACKED_ROWS:
                v = v[SUBLANES:] if kinds[k][0] == 'p' else v[:SUBLANES]
            return v

        for c0, w in cols:
            touts, aouts = fn(i == 0, i == n_tiles - 1, *[load(k, c0, w) for k in range(n_in)])
            for r, v in zip(outs, touts):
                if r.shape[-1] == width:
                    r[:, c0:c0 + w] = v.astype(r.dtype)
                else:
                    r[...] = v.astype(r.dtype)
            for r, v in zip(accs, aouts):
                if r.shape[-1] == width:
                    r[:, c0:c0 + w] += v
                else:
                    r[...] += v

    out_shape = [S(sh, dt) for sh, dt, _ in out_tiled] + [S(sh, F32) for sh in out_acc]
    out_specs = [sp for _, _, sp in out_tiled]
    for sh in out_acc:
        out_specs.append(pl.BlockSpec(sh, lambda i, nd=len(sh): (0,) * nd))
    in_specs = list(in_specs) + [_ANY] * len(deps)
    if into is None:
        return _pcall(body, name=name, grid=(n_tiles,), in_specs=in_specs, out_specs=out_specs,
                      out_shape=out_shape, compiler_params=_cparams(("arbitrary",)))(*arrays, *deps)
    return _pcall(body, name=name, grid=(n_tiles,), in_specs=in_specs + [_ANY], out_specs=out_specs,
                  out_shape=out_shape, input_output_aliases={n_in + len(deps): 0},
                  compiler_params=_cparams(("arbitrary",)))(*arrays, *deps, into)


def _rms(x, g):
    return x * lax.rsqrt(jnp.mean(x * x, axis=-1, keepdims=True) + EPS) * g


def _rms_bwd(x, g, dy):
    r = lax.rsqrt(jnp.mean(x * x, axis=-1, keepdims=True) + EPS)
    xr = x * r
    gdy = dy * g
    dx = (gdy - xr * jnp.mean(gdy * xr, axis=-1, keepdims=True)) * r
    return dx, _colsum(dy * xr)


_GELU_C0, _GELU_C1 = 0.7978845608028654, 0.044715


def _gelu_and_grad(x):
    x2 = x * x
    t = jnp.tanh(_GELU_C0 * x * (1.0 + _GELU_C1 * x2))
    half = 0.5 * (1.0 + t)
    grad = half + (0.5 * _GELU_C0) * x * (1.0 - t * t) * (1.0 + (3.0 * _GELU_C1) * x2)
    return x * half, grad


def _colsum(v):
    return jnp.sum(v, axis=0, keepdims=True)


_TILES = (1152, 1024, 896, 768, 640, 512, 384, 256, 128)
_K_TILES = (4096, 3456, 3072, 2688, 2048, 1536, 1344, 1152, 1024, 896, 768, 640, 512, 384, 256, 128)


def _mm(name, a, b, *, ta=False, tb=False, outs=((F32, None),), extra=None, out_blocks=None, tm=None, tn=None, tk=None,
        deps=(), a_cols=None):
    M, K = (a.shape[1], a.shape[0]) if ta else a.shape
    if a_cols is not None:
        assert not ta
        K = a_cols[1]
    b3 = b.ndim == 3
    if b3:
        nb_b, brows, bcols = b.shape
        N = brows if tb else nb_b * bcols
    else:
        N = b.shape[0] if tb else b.shape[1]
    n_lim = N if out_blocks is None else N // out_blocks
    if b3 and not tb:
        n_lim = min(n_lim, bcols)
    tm = tm or _pick(M, _TILES[1:])
    tn = tn or _pick(n_lim, _TILES)
    tk = tk or _pick(bcols if (b3 and tb) else K, _K_TILES)
    nk = K // tk
    assert M % tm == 0 and N % tn == 0 and K % tk == 0
    dn = (((0 if ta else 1,), (1 if tb else 0,)), ((), ()))
    n_extra = 0 if extra is None else 1
    n_out = len(outs)

    def body(*refs):
        a_ref, b_ref = refs[0], refs[1]
        e_ref = refs[2] if n_extra else None
        o_refs = refs[2 + n_extra + len(deps):2 + n_extra + len(deps) + n_out]

        def finish(r):
            e = e_ref[...] if n_extra else None
            for o, (_, f) in zip(o_refs, outs):
                o[...] = (r if f is None else f(r, e)).astype(o.dtype)

        part = lax.dot_general(a_ref[...], b_ref[...], dn, preferred_element_type=F32)
        if nk == 1:
            finish(part)
            return
        acc = refs[-1]
        k = pl.program_id(2)

        @pl.when(k == 0)
        def _():
            acc[...] = part

        @pl.when(jnp.logical_and(k > 0, k < nk - 1))
        def _():
            acc[...] += part

        @pl.when(k == nk - 1)
        def _():
            finish(acc[...] + part)

    k0 = 0 if a_cols is None else a_cols[0] // tk
    a_spec = (pl.BlockSpec((tk, tm), lambda i, j, k: (k, i)) if ta
              else pl.BlockSpec((tm, tk), lambda i, j, k: (i, k + k0)))
    if not b3:
        b_spec = pl.BlockSpec((tn, tk), lambda i, j, k: (j, k)) if tb else pl.BlockSpec((tk, tn), lambda i, j, k: (k, j))
    elif tb:
        per = bcols // tk
        b_spec = pl.BlockSpec((None, tn, tk), lambda i, j, k: (k // per, j, k % per))
    else:
        per = bcols // tn
        b_spec = pl.BlockSpec((None, tk, tn), lambda i, j, k: (j // per, k, j % per))
    o_spec = pl.BlockSpec((tm, tn), lambda i, j, k: (i, j))
    if out_blocks is None:
        out_specs, out_shape = [o_spec] * n_out, [S((M, N), dt) for dt, _ in outs]
    else:
        per_o = N // out_blocks // tn
        ob_spec = pl.BlockSpec((None, tm, tn), lambda i, j, k: (j // per_o, i, j % per_o))
        out_specs, out_shape = [ob_spec] * n_out, [S((out_blocks, M, N // out_blocks), dt) for dt, _ in outs]
    in_specs = [a_spec, b_spec] + ([o_spec] if n_extra else []) + [_ANY] * len(deps)
    args = [a, b] + ([extra] if n_extra else []) + list(deps)
    return _pcall(body, name=name, grid=(M // tm, N // tn, nk), in_specs=in_specs, out_specs=out_specs,
                  out_shape=out_shape, scratch_shapes=[pltpu.VMEM((tm, tn), F32)] if nk > 1 else [],
                  compiler_params=_cparams(("parallel", "parallel", "arbitrary")))(*args)


def _shift_down(x, halo, s):
    if s == 0:
        return x
    r = pltpu.roll(x, s, 0)
    hr = pltpu.roll(halo, s, 0)
    row = lax.broadcasted_iota(jnp.int32, halo.shape, 0)
    top = jnp.where(row < s, hr, r[:SUBLANES])
    if x.shape[0] == SUBLANES:
        return top
    return jnp.concatenate([top, r[SUBLANES:]], axis=0)


def _shift_up(x, nxt, s):
    if s == 0:
        return x
    n = x.shape[0]
    r = pltpu.roll(x, n - s, 0)
    nr = pltpu.roll(nxt, SUBLANES - s, 0)
    row = lax.broadcasted_iota(jnp.int32, nxt.shape, 0)
    bot = jnp.where(row >= SUBLANES - s, nr, r[n - SUBLANES:])
    if n == SUBLANES:
        return bot
    return jnp.concatenate([r[:n - SUBLANES], bot], axis=0)


def _conv_taps(x, halo):
    return [_shift_down(x, halo, CONV_WIDTH - 1 - k) for k in range(CONV_WIDTH)]


def _conv_pre(x, halo, w, b, taps=None):
    taps = _conv_taps(x, halo) if taps is None else taps
    acc = b + w[0:1, :] * taps[0]
    for k in range(1, CONV_WIDTH):
        acc = acc + w[k:k + 1, :] * taps[k]
    return acc


def _silu_grad(p):
    s = jax.nn.sigmoid(p)
    return s * (1.0 + p * (1.0 - s))


def _conv_bwd_tile(first, last, x, hprev, xnext, d, dnext, w, b, silu):
    hprev = jnp.where(first, 0.0, hprev)
    taps = _conv_taps(x, hprev)
    if silu:
        d = d * _silu_grad(_conv_pre(x, hprev, w, b, taps))
        pre_next = _conv_pre(xnext, x[x.shape[0] - SUBLANES:], w, b)
        dnext = dnext * _silu_grad(pre_next)
    dnext = jnp.where(last, 0.0, dnext)
    dx = w[CONV_WIDTH - 1:CONV_WIDTH, :] * d
    row8 = lax.broadcasted_iota(jnp.int32, (SUBLANES, x.shape[1]), 0)
    dw8 = jnp.where(row8 == CONV_WIDTH - 1, _colsum(d * x), 0.0)
    for k in range(CONV_WIDTH - 1):
        dx = dx + w[k:k + 1, :] * _shift_up(d, dnext, CONV_WIDTH - 1 - k)
        dw8 = dw8 + jnp.where(row8 == k, _colsum(d * taps[k]), 0.0)
    return dx, dw8, _colsum(d)


def _ssd_dims(xbc_act, n_heads):
    T, XBC = xbc_act.shape
    GN = XBC // 4
    DS = XBC - 2 * GN
    G = SSD_GROUPS
    N = GN // G
    P = DS // n_heads
    K = n_heads // G
    return T, XBC, DS, GN, G, N, P, K


def _ssd_common(dt, alog, Q):
    a = -jnp.exp(alog)
    adt = dt * a
    li = lax.broadcasted_iota(jnp.int32, (Q, Q), 0)
    si = lax.broadcasted_iota(jnp.int32, (Q, Q), 1)
    causal = li >= si
    ltri = causal.astype(F32)
    acs = jnp.dot(ltri, adt, precision=HI, preferred_element_type=F32)
    acs_row = lax.dot_general(adt, ltri, (((0,), (1,)), ((), ())), precision=HI,
                              preferred_element_type=F32)
    return a, adt, causal, ltri, acs, acs_row


def _expander(g, K, P, W):
    r = lax.broadcasted_iota(jnp.int32, (LANES, W), 0)
    c = lax.broadcasted_iota(jnp.int32, (LANES, W), 1)
    return (c // P + g * K == r).astype(F32)


def _dotb(a, b, dn=(((1,), (0,)), ((), ()))):
    return lax.dot_general(a.astype(BF16), b.astype(BF16), dn, preferred_element_type=F32)


def _dot_split(a, sel, terms, dn=(((1,), (0,)), ((), ()))):
    selb = sel.astype(BF16)
    out = None
    for _ in range(terms):
        piece = a.astype(BF16)
        part = lax.dot_general(piece, selb, dn, preferred_element_type=F32)
        out = part if out is None else out + part
        a = a - piece.astype(F32)
    return out


_NT = (((1,), (1,)), ((), ()))
_TN = (((0,), (0,)), ((), ()))


def _ssd_fwd(xbc_act, dt, alog, dskip, n_heads, Q):
    T, XBC, DS, GN, G, N, P, K = _ssd_dims(xbc_act, n_heads)
    W = K * P
    nc = T // Q

    def body(xs_ref, b_ref, c_ref, dt_ref, alog_ref, d_ref, y_ref, hp_ref, h_scr):
        ci = pl.program_id(0)

        @pl.when(ci == 0)
        def _():
            h_scr[...] = jnp.zeros_like(h_scr)

        dtv = dt_ref[...]
        a, adt, causal, ltri, acs, acs_row = _ssd_common(dtv, alog_ref[...], Q)
        lane_head = lax.broadcasted_iota(jnp.int32, (Q, W), 1) // P
        for g in range(G):
            eg = _expander(g, K, P, W)
            dtb = _dot_split(dtv, eg, 3)
            acsb = _dot_split(acs, eg, 3)
            lastb = acsb[Q - 1:Q, :]
            db = _dot_split(jnp.broadcast_to(d_ref[...], (SUBLANES, LANES)), eg, 3)[0:1, :]
            xg = xs_ref[:, g * W:(g + 1) * W]
            bg = b_ref[:, g * N:(g + 1) * N].astype(BF16)
            cg = c_ref[:, g * N:(g + 1) * N].astype(BF16)
            xt = xg * dtb
            sc = _dotb(cg, bg, _NT)
            yd = jnp.zeros((Q, W), F32)
            for k in range(K):
                h = g * K + k
                seg = acs[:, h:h + 1] - acs_row[h:h + 1, :]
                lh = jnp.where(causal, jnp.exp(jnp.minimum(seg, 0.0)), 0.0)
                xk = jnp.where(lane_head == k, xt, 0.0)
                yd = yd + _dotb(sc * lh, xk)
            hp = h_scr[g]
            yoff = _dotb(cg, hp) * jnp.exp(acsb)
            y_ref[:, g * W:(g + 1) * W] = yd + yoff + xg * db
            e_end = jnp.exp(lastb - acsb)
            st = _dotb(bg, xt * e_end, _TN)
            hp_ref[0, g] = hp
            h_scr[g] = jnp.exp(lastb) * hp + st

    cb = DS // GN
    in_specs = [pl.BlockSpec((Q, DS), lambda c: (c, 0)),
                pl.BlockSpec((Q, GN), lambda c: (c, cb)),
                pl.BlockSpec((Q, GN), lambda c: (c, cb + 1)),
                pl.BlockSpec((Q, LANES), lambda c: (c, 0)),
                pl.BlockSpec((1, LANES), lambda c: (0, 0)),
                pl.BlockSpec((1, LANES), lambda c: (0, 0))]
    out_specs = [pl.BlockSpec((Q, DS), lambda c: (c, 0)),
                 pl.BlockSpec((1, G, N, W), lambda c: (c, 0, 0, 0))]
    return _pcall(body, name="ssd_fwd", grid=(nc,), in_specs=in_specs, out_specs=out_specs,
                  out_shape=[S((T, DS), F32), S((nc, G, N, W), F32)],
                  scratch_shapes=[pltpu.VMEM((G, N, W), F32)],
                  compiler_params=_cparams(("arbitrary",)))(xbc_act, xbc_act, xbc_act, dt, alog, dskip)


def _ssd_bwd(xbc_act, dt, alog, dskip, hprev, dy, n_heads, Q):
    T, XBC, DS, GN, G, N, P, K = _ssd_dims(xbc_act, n_heads)
    W = K * P
    nc = T // Q

    def body(xs_ref, b_ref, c_ref, dt_ref, alog_ref, d_ref, hp_ref, dy_ref,
             dxbc_ref, ddt_ref, dalog_ref, dd_ref, dh_scr):
        ci = pl.program_id(0)

        @pl.when(ci == 0)
        def _():
            dh_scr[...] = jnp.zeros_like(dh_scr)
            dalog_ref[...] = jnp.zeros_like(dalog_ref)
            dd_ref[...] = jnp.zeros_like(dd_ref)

        dtv = dt_ref[...]
        a, adt, causal, ltri, acs, acs_row = _ssd_common(dtv, alog_ref[...], Q)
        lane_head = lax.broadcasted_iota(jnp.int32, (Q, W), 1) // P
        lane128 = lax.broadcasted_iota(jnp.int32, (Q, LANES), 1)
        sub128 = lax.broadcasted_iota(jnp.int32, (LANES, Q), 0)
        rowq = lax.broadcasted_iota(jnp.int32, (Q, W), 0)
        dacs = jnp.zeros((Q, LANES), F32)
        dacs_row = jnp.zeros((LANES, Q), F32)
        ddt = jnp.zeros((Q, LANES), F32)
        dd_acc = jnp.zeros((1, LANES), F32)
        for g in range(G):
            eg = _expander(g, K, P, W)
            dtb = _dot_split(dtv, eg, 3)
            acsb = _dot_split(acs, eg, 3)
            lastb = acsb[Q - 1:Q, :]
            db = _dot_split(jnp.broadcast_to(d_ref[...], (SUBLANES, LANES)), eg, 3)[0:1, :]
            xg = xs_ref[:, g * W:(g + 1) * W]
            bg = b_ref[:, g * N:(g + 1) * N].astype(BF16)
            cg = c_ref[:, g * N:(g + 1) * N].astype(BF16)
            dyg = dy_ref[:, g * W:(g + 1) * W].astype(F32)
            hp = hp_ref[0, g]
            dhn = dh_scr[g]
            xt = xg * dtb
            sc = _dotb(cg, bg, _NT)
            eacs = jnp.exp(acsb)
            e_end = jnp.exp(lastb - acsb)
            elast = jnp.exp(lastb)

            wv = dyg * eacs
            dcg = _dotb(wv, hp, _NT)
            dhp = _dotb(cg, wv, _TN) + elast * dhn
            dacsb = dyg * (_dotb(cg, hp) * eacs)

            xe = xt * e_end
            dbg = _dotb(xe, dhn, _NT)
            v = _dotb(bg, dhn)
            dxt = v * e_end
            de = v * xe
            dacsb = dacsb - de
            dlastb = _colsum(de) + elast * jnp.sum(dhn * hp, axis=0, keepdims=True)

            dsc = jnp.zeros((Q, Q), F32)
            for k in range(K):
                h = g * K + k
                seg = acs[:, h:h + 1] - acs_row[h:h + 1, :]
                lh = jnp.where(causal, jnp.exp(jnp.minimum(seg, 0.0)), 0.0)
                mh = sc * lh
                dyk = jnp.where(lane_head == k, dyg, 0.0)
                dxt = dxt + jnp.where(lane_head == k, _dotb(mh, dyg, _TN), 0.0)
                dm = _dotb(dyk, xt, _NT)
                dsc = dsc + dm * lh
                gm = dm * mh
                dacs = dacs + jnp.where(lane128 == h, jnp.sum(gm, axis=1, keepdims=True), 0.0)
                dacs_row = dacs_row - jnp.where(sub128 == h, jnp.sum(gm, axis=0, keepdims=True), 0.0)
            dcg = dcg + _dotb(dsc, bg)
            dbg = dbg + _dotb(dsc, cg, _TN)

            dacsb = dacsb + jnp.where(rowq == Q - 1, dlastb, 0.0)
            dacs = dacs + _dot_split(dacsb, eg, 2, _NT)
            ddt = ddt + _dot_split(dxt * xg, eg, 2, _NT)
            dd_acc = dd_acc + _dot_split(jnp.broadcast_to(_colsum(dyg * xg), (SUBLANES, W)), eg, 2, _NT)[0:1, :]
            dxbc_ref[:, g * W:(g + 1) * W] = dxt * dtb + dyg * db
            dxbc_ref[:, DS + g * N:DS + (g + 1) * N] = dbg
            dxbc_ref[:, DS + GN + g * N:DS + GN + (g + 1) * N] = dcg
            dh_scr[g] = dhp

        eye = (lax.broadcasted_iota(jnp.int32, (LANES, LANES), 0) ==
               lax.broadcasted_iota(jnp.int32, (LANES, LANES), 1)).astype(F32)
        dacs = dacs + lax.dot_general(dacs_row, eye, _TN, precision=HI, preferred_element_type=F32)
        dadt = lax.dot_general(ltri, dacs, _TN, precision=HI, preferred_element_type=F32)
        ddt_ref[...] = ddt + dadt * a
        dalog_ref[...] += _colsum(dadt * dtv) * a
        dd_ref[...] += dd_acc

    cb = DS // GN
    rv = lambda c: nc - 1 - c
    in_specs = [pl.BlockSpec((Q, DS), lambda c: (rv(c), 0)),
                pl.BlockSpec((Q, GN), lambda c: (rv(c), cb)),
                pl.BlockSpec((Q, GN), lambda c: (rv(c), cb + 1)),
                pl.BlockSpec((Q, LANES), lambda c: (rv(c), 0)),
                pl.BlockSpec((1, LANES), lambda c: (0, 0)),
                pl.BlockSpec((1, LANES), lambda c: (0, 0)),
                pl.BlockSpec((1, G, N, W), lambda c: (rv(c), 0, 0, 0)),
                pl.BlockSpec((Q, DS), lambda c: (rv(c), 0))]
    out_specs = [pl.BlockSpec((Q, XBC), lambda c: (rv(c), 0)),
                 pl.BlockSpec((Q, LANES), lambda c: (rv(c), 0)),
                 pl.BlockSpec((1, LANES), lambda c: (0, 0)),
                 pl.BlockSpec((1, LANES), lambda c: (0, 0))]
    return _pcall(body, name="ssd_bwd", grid=(nc,), in_specs=in_specs, out_specs=out_specs,
                  out_shape=[S((T, XBC), F32), S((T, LANES), F32), S((1, LANES), F32), S((1, LANES), F32)],
                  scratch_shapes=[pltpu.VMEM((G, N, W), F32)],
                  compiler_params=_cparams(("arbitrary",)))(
                      xbc_act, xbc_act, xbc_act, dt, alog, dskip, hprev, dy)


def _blockdiag(x, w_ref, dn=(((1,), (0,)), ((), ()))):
    H, B, _ = w_ref.shape
    return jnp.concatenate([_dotb(x[:, h * B:(h + 1) * B], w_ref[h], dn) for h in range(H)], axis=1)


def _lru_elem(xl, r_pre, i_pre, lam):
    r = jax.nn.sigmoid(r_pre)
    i = jax.nn.sigmoid(i_pre)
    log_a = -LRU_C * r * jax.nn.softplus(-lam)
    a = jnp.exp(log_a)
    u = jnp.sqrt(1.0 - jnp.exp(2.0 * log_a)) * (i * xl)
    return a, u


def _lru_elem_bwd(xl, r_pre, i_pre, lam, da, du):
    r = jax.nn.sigmoid(r_pre)
    i = jax.nn.sigmoid(i_pre)
    sp = jax.nn.softplus(-lam)
    a = jnp.exp(-LRU_C * r * sp)
    s = jnp.sqrt(1.0 - a * a)
    d_ix = du * s
    dlog_a = (da - du * (i * xl) * a / s) * a
    dr_pre = dlog_a * (-LRU_C) * sp * r * (1.0 - r)
    dlam = _colsum(dlog_a * r) * (LRU_C * jax.nn.sigmoid(-lam))
    di_pre = d_ix * xl * i * (1.0 - i)
    return d_ix * i, dr_pre, di_pre, dlam


def _lru_gates_fwd(xl, w_a, b_a, w_x, b_x, lam, tr):
    T, DL = xl.shape

    def body(xl_ref, wa_ref, ba_ref, wx_ref, bx_ref, lam_ref, a_ref, u_ref):
        x = xl_ref[...]
        r_pre = _blockdiag(x, wa_ref) + ba_ref[...]
        i_pre = _blockdiag(x, wx_ref) + bx_ref[...]
        a, u = _lru_elem(x, r_pre, i_pre, lam_ref[...])
        a_ref[...] = a
        u_ref[...] = u

    w3 = pl.BlockSpec(w_a.shape, lambda i: (0, 0, 0))
    vec = pl.BlockSpec((1, DL), lambda i: (0, 0))
    return _pcall(body, name="lru_gates_fwd", grid=(T // tr,),
                  in_specs=[_rt(tr, DL), w3, vec, w3, vec, vec],
                  out_specs=[_rt(tr, DL), _rt(tr, DL)], out_shape=[S((T, DL), F32), S((T, DL), F32)],
                  compiler_params=_cparams(("parallel",)))(xl, w_a, b_a, w_x, b_x, lam)


def _lru_gates_bwd(xl, w_a, b_a, w_x, b_x, lam, da, du, tr):
    T, DL = xl.shape
    H, B, _ = w_a.shape

    def body(xl_ref, wa_ref, ba_ref, wx_ref, bx_ref, lam_ref, da_ref, du_ref,
             dxl_ref, dwa_ref, dba_ref, dwx_ref, dbx_ref, dlam_ref):
        @pl.when(pl.program_id(0) == 0)
        def _():
            for r in (dwa_ref, dba_ref, dwx_ref, dbx_ref, dlam_ref):
                r[...] = jnp.zeros_like(r)

        x = xl_ref[...]
        r_pre = _blockdiag(x, wa_ref) + ba_ref[...]
        i_pre = _blockdiag(x, wx_ref) + bx_ref[...]
        dx, dr, di, dlam = _lru_elem_bwd(x, r_pre, i_pre, lam_ref[...], da_ref[...], du_ref[...])
        dxl_ref[...] = (dx + _blockdiag(dr, wa_ref, _NT) + _blockdiag(di, wx_ref, _NT)).astype(dxl_ref.dtype)
        for h in range(H):
            xh = x[:, h * B:(h + 1) * B]
            dwa_ref[h] += _dotb(xh, dr[:, h * B:(h + 1) * B], _TN)
            dwx_ref[h] += _dotb(xh, di[:, h * B:(h + 1) * B], _TN)
        dba_ref[...] += _colsum(dr)
        dbx_ref[...] += _colsum(di)
        dlam_ref[...] += dlam

    w3 = pl.BlockSpec(w_a.shape, lambda i: (0, 0, 0))
    vec = pl.BlockSpec((1, DL), lambda i: (0, 0))
    return _pcall(body, name="lru_gates_bwd", grid=(T // tr,),
                  in_specs=[_rt(tr, DL), w3, vec, w3, vec, vec, _rt(tr, DL), _rt(tr, DL)],
                  out_specs=[_rt(tr, DL), w3, vec, w3, vec, vec],
                  out_shape=[S((T, DL), BF16), S(w_a.shape, F32), S((1, DL), F32), S(w_a.shape, F32),
                             S((1, DL), F32), S((1, DL), F32)],
                  compiler_params=_cparams(("arbitrary",)))(xl, w_a, b_a, w_x, b_x, lam, da, du)


def _groups(v):
    return v.reshape(v.shape[0] // SUBLANES, SUBLANES, v.shape[1])


def _rows_shifted(v, edge, up):
    sub = lax.broadcasted_iota(jnp.int32, v.shape, 1)
    if up:
        other = jnp.concatenate([v[1:], edge[None]], axis=0)
        return jnp.where(sub < SUBLANES - 1, pltpu.roll(v, SUBLANES - 1, 1), pltpu.roll(other, SUBLANES - 1, 1))
    other = jnp.concatenate([edge[None], v[:-1]], axis=0)
    return jnp.where(sub >= 1, pltpu.roll(v, 1, 1), pltpu.roll(other, 1, 1))


def _scan_tile(a, u, entering, emit, up):
    G = a.shape[0]
    sub = lax.broadcasted_iota(jnp.int32, a.shape, 1)
    d = 1
    while d < SUBLANES:
        if up:
            keep = sub < SUBLANES - d
            a_s = jnp.where(keep, pltpu.roll(a, SUBLANES - d, 1), 1.0)
            u_s = jnp.where(keep, pltpu.roll(u, SUBLANES - d, 1), 0.0)
        else:
            keep = sub >= d
            a_s = jnp.where(keep, pltpu.roll(a, d, 1), 1.0)
            u_s = jnp.where(keep, pltpu.roll(u, d, 1), 0.0)
        u = a * u_s + u
        a = a * a_s
        d *= 2
    for g in (reversed(range(G)) if up else range(G)):
        hg = u[g] + a[g] * entering
        emit(g, hg)
        entering = hg[0:1] if up else hg[SUBLANES - 1:SUBLANES]
    return entering


def _lru_scan_fwd(a, u, tr, deps=()):
    T, DL = a.shape

    def body(a_ref, u_ref, *rest):
        h_ref, carry = rest[len(deps):]

        @pl.when(pl.program_id(0) == 0)
        def _():
            carry[...] = jnp.zeros_like(carry)

        def emit(g, hg):
            h_ref[g * SUBLANES:(g + 1) * SUBLANES, :] = hg

        last = _scan_tile(_groups(a_ref[...]), _groups(u_ref[...]), carry[0:1, :], emit, up=False)
        carry[...] = jnp.broadcast_to(last, carry.shape)

    return _pcall(body, name="lru_scan_fwd", grid=(T // tr,),
                  in_specs=[_rt(tr, DL), _rt(tr, DL)] + [_ANY] * len(deps),
                  out_specs=_rt(tr, DL), out_shape=S((T, DL), F32),
                  scratch_shapes=[pltpu.VMEM((SUBLANES, DL), F32)],
                  compiler_params=_cparams(("arbitrary",)))(a, u, *deps)


def _lru_scan_bwd(a, h, dh, tr):
    T, DL = a.shape
    n = T // tr

    def body(a_ref, an_ref, h_ref, hp_ref, dh_ref, du_ref, da_ref, carry):
        i = pl.program_id(0)
        ti = n - 1 - i

        @pl.when(i == 0)
        def _():
            carry[...] = jnp.zeros_like(carry)

        a_next = _rows_shifted(_groups(a_ref[...]), jnp.where(ti == n - 1, 0.0, an_ref[...]), up=True)
        h_prev = _rows_shifted(_groups(h_ref[...]), jnp.where(ti == 0, 0.0, hp_ref[...]), up=False)

        def emit(g, gg):
            du_ref[g * SUBLANES:(g + 1) * SUBLANES, :] = gg
            da_ref[g * SUBLANES:(g + 1) * SUBLANES, :] = gg * h_prev[g]

        top = _scan_tile(a_next, _groups(dh_ref[...].astype(F32)), carry[0:1, :], emit, up=True)
        carry[...] = jnp.broadcast_to(top, carry.shape)

    return _pcall(body, name="lru_scan_bwd", grid=(n,),
                  in_specs=[_rt(tr, DL, 0, n), _halo_next(tr, DL, n, 0, n), _rt(tr, DL, 0, n),
                            _halo_prev(tr, DL, 0, n), _rt(tr, DL, 0, n)],
                  out_specs=[_rt(tr, DL, 0, n), _rt(tr, DL, 0, n)],
                  out_shape=[S((T, DL), F32), S((T, DL), F32)],
                  scratch_shapes=[pltpu.VMEM((SUBLANES, DL), F32)],
                  compiler_params=_cparams(("arbitrary",)))(a, a, h, h, dh)


def _adamw(w, g, m, v):
    m = ADAM_B1 * m + (1.0 - ADAM_B1) * g
    v = ADAM_B2 * v + (1.0 - ADAM_B2) * (g * g)
    m_hat = m / (1.0 - ADAM_B1 ** ADAM_STEP)
    v_hat = v / (1.0 - ADAM_B2 ** ADAM_STEP)
    delta = -ADAM_LR * (m_hat / (jnp.sqrt(v_hat) + ADAM_EPS) + ADAM_WD * w)
    return delta, m, v


def _adamw_big(name, w, m, v, own, recv, own_idx):
    _, R, C = w.shape
    n_recv = recv.shape[0]
    tr = _pick(R, (256, 128, 64, 32, 16))

    def body(idx_ref, w_ref, m_ref, v_ref, p_ref, *rest):
        g = p_ref[...].astype(F32)
        for r in rest[:n_recv]:
            g = g + r[...].astype(F32)
        g_ref, d_ref, nm_ref, nv_ref = rest[n_recv:]
        d, nm, nv = _adamw(w_ref[...], g, m_ref[...], v_ref[...])
        g_ref[...] = g
        d_ref[...] = d
        nm_ref[...] = nm
        nv_ref[...] = nv

    r_spec = lambda s: pl.BlockSpec((None, tr, C), lambda i, idx: (s, i, 0))
    t2 = r_spec(0)
    gs = pltpu.PrefetchScalarGridSpec(
        num_scalar_prefetch=1, grid=(R // tr,),
        in_specs=[t2, t2, t2, pl.BlockSpec((None, tr, C), lambda i, idx: (idx[0], i, 0))]
        + [r_spec(s) for s in range(n_recv)],
        out_specs=[t2, t2, t2, t2])
    return _pcall(body, name=name, grid_spec=gs, out_shape=[S((1, R, C), F32)] * 4,
                  compiler_params=_cparams(("parallel",)))(own_idx, w, m, v, own, *([recv] * n_recv))


def _adamw_small(ws, gs, ms, vs):
    n = len(ws)

    def body(*refs):
        for k in range(n):
            d, nm, nv = _adamw(refs[k][...], refs[n + k][...], refs[2 * n + k][...], refs[3 * n + k][...])
            refs[4 * n + k][...] = d
            refs[5 * n + k][...] = nm
            refs[6 * n + k][...] = nv

    res = _pcall(body, name="adamw_small", out_shape=[S(w.shape, F32) for w in ws] * 3,
                 compiler_params=_cparams())(*ws, *gs, *ms, *vs)
    return res[:n], res[n:2 * n], res[2 * n:]


def _sum8(name, parts):
    _, R, C = parts.shape

    def body(p_ref, o_ref):
        acc = p_ref[0]
        for k in range(1, N_DEV):
            acc = acc + p_ref[k]
        o_ref[...] = acc

    return _pcall(body, name=name, out_shape=S((R, C), F32), compiler_params=_cparams())(parts)


def _pair_sum(name, full, recv, c_idx):
    _, R, C = full.shape
    tr = _pick(R, (256, 128, 64, 32, 16))

    def body(c_ref, f_ref, r_ref, o_ref):
        o_ref[...] = (f_ref[...].astype(F32) + r_ref[...].astype(F32)).astype(o_ref.dtype)

    gs = pltpu.PrefetchScalarGridSpec(
        num_scalar_prefetch=1, grid=(4, R // tr),
        in_specs=[pl.BlockSpec((None, tr, C), lambda j, i, c: (2 * j + c[0], i, 0)),
                  pl.BlockSpec((None, tr, C), lambda j, i, c: (j, i, 0))],
        out_specs=pl.BlockSpec((None, tr, C), lambda j, i, c: (j, i, 0)))
    return _pcall(body, name=name, grid_spec=gs, out_shape=S((4, R, C), BF16),
                  compiler_params=_cparams(("parallel", "parallel")))(c_idx, full, recv)


def _cast_bf16(name, w, dev_idx, row0=0, rows=None, deps=()):
    C = w.shape[2]
    R = w.shape[1] if rows is None else rows
    tr = _pick(R, (256, 128, 64, 32, 16))
    b0 = row0 // tr

    def body(d_ref, w_ref, *rest):
        rest[-1][...] = w_ref[...].astype(BF16)

    gs = pltpu.PrefetchScalarGridSpec(
        num_scalar_prefetch=1, grid=(R // tr,),
        in_specs=[pl.BlockSpec((None, tr, C), lambda i, d: (0, i + b0, 0))] + [_ANY] * len(deps),
        out_specs=pl.BlockSpec((None, tr, C), lambda i, d: (d[0], i, 0)))
    return _pcall(body, name=name, grid_spec=gs, out_shape=S((N_DEV, R, C), BF16),
                  compiler_params=_cparams(("parallel",)))(dev_idx, w, *deps)


def _own_block(v, dev):
    return lax.dynamic_update_slice(lax.empty((N_DEV,) + v.shape, v.dtype), v[None], (dev,) + (0,) * v.ndim)


_ANY = pl.BlockSpec(memory_space=pl.ANY)


def _position():
    return lax.axis_index("x"), lax.axis_index("y"), lax.axis_index("c")


def _allgather(name, bufs, deps=()):
    n = len(bufs)
    nd = len(deps)

    def body(*refs):
        outs = refs[n + nd:2 * n + nd]
        send, recv = refs[2 * n + nd:]
        x, y, c = _position()
        me, sib = (x, y, c), (x, y, 1 - c)
        chips = [(1 - x, y), (x, 1 - y), (1 - x, 1 - y)]

        def copy(a, k, block, to):
            bx, by, bc = block
            blk = outs[a].at[4 * bx + 2 * by + bc]
            return pltpu.make_async_remote_copy(
                src_ref=blk, dst_ref=blk, send_sem=send.at[a, k], recv_sem=recv.at[a, k],
                device_id=to, device_id_type=MESH)

        first = []
        for a in range(n):
            first.append(copy(a, 0, me, sib))
            first += [copy(a, 1 + j, me, (*chip, c)) for j, chip in enumerate(chips)]
        for cp in first:
            cp.start()
        passed = []
        for j, chip in enumerate(chips):
            for a in range(n):
                copy(a, 1 + j, (*chip, c), me).wait_recv()
                cp = copy(a, 4 + j, (*chip, c), sib)
                cp.start()
                passed.append(cp)
        for a in range(n):
            copy(a, 0, sib, me).wait_recv()
        for j, chip in enumerate(chips):
            for a in range(n):
                copy(a, 4 + j, (*chip, 1 - c), me).wait_recv()
        for cp in first + passed:
            cp.wait_send()

    return _pcall(body, name=name, in_specs=[_ANY] * (n + nd), out_specs=[_ANY] * n,
                  out_shape=[S(b.shape, b.dtype) for b in bufs], input_output_aliases={a: a for a in range(n)},
                  scratch_shapes=[pltpu.SemaphoreType.DMA((n, 7)), pltpu.SemaphoreType.DMA((n, 7))])(*bufs, *deps)


def _rs_sibling(name, fulls):
    n = len(fulls)

    def body(*refs):
        ins, outs = refs[:n], refs[n:2 * n]
        send, recv = refs[2 * n:]
        x, y, c = _position()
        copies = []
        for a in range(n):
            for j in range(4):
                copies.append(pltpu.make_async_remote_copy(
                    src_ref=ins[a].at[2 * j + (1 - c)], dst_ref=outs[a].at[j], send_sem=send.at[a, j],
                    recv_sem=recv.at[a, j], device_id=(x, y, 1 - c), device_id_type=MESH))
        for cp in copies:
            cp.start()
        for cp in copies:
            cp.wait()

    return _pcall(body, name=name, in_specs=[_ANY] * n, out_specs=[_ANY] * n,
                  out_shape=[S((4,) + f.shape[1:], f.dtype) for f in fulls],
                  scratch_shapes=[pltpu.SemaphoreType.DMA((n, 4)), pltpu.SemaphoreType.DMA((n, 4))])(*fulls)


_HBM = pl.BlockSpec(memory_space=pltpu.HBM)
_SEM = pl.BlockSpec(memory_space=pltpu.SEMAPHORE)
_EFFECT = pltpu.SideEffectType.DATAFLOW_SIDE_EFFECTING


def _remote_copies(copies_fn, srcs, lands, send, recv):
    x, y, c = _position()
    return [pltpu.make_async_remote_copy(src_ref=s, dst_ref=d, send_sem=send[i], recv_sem=recv[i], device_id=to,
                                         device_id_type=MESH)
            for i, (s, d, to) in enumerate(copies_fn(x, y, c, srcs, lands))]


def _split_start(name, srcs, lands, copies_fn, nc, after=()):
    n, nl, na = len(srcs), len(lands), len(after)

    def body(*refs):
        src_refs, land_refs = refs[:n], refs[n:n + nl]
        outs = refs[n + nl + na:]
        for cp in _remote_copies(copies_fn, src_refs, land_refs, outs[:nc], outs[nc:2 * nc]):
            cp.start()
        outs[-1][...] = jnp.zeros_like(outs[-1])

    hbm = lambda a: pltpu.with_memory_space_constraint(a, pltpu.HBM)
    res = _pcall(
        body, name=name, in_specs=[_HBM] * (n + nl) + [_ANY] * na,
        out_specs=[_SEM] * (2 * nc) + [_HBM] * (n + nl) + [pl.BlockSpec(memory_space=pltpu.VMEM)],
        out_shape=[pltpu.SemaphoreType.DMA(())] * (2 * nc) + [pltpu.HBM(s.shape, s.dtype) for s in srcs]
        + [pltpu.HBM(l.shape, l.dtype) for l in lands] + [S((SUBLANES, LANES), F32)],
        input_output_aliases={i: 2 * nc + i for i in range(n + nl)},
        compiler_params=pltpu.CompilerParams(has_side_effects=_EFFECT),
    )(*[hbm(s) for s in srcs], *[hbm(l) for l in lands], *after)
    return res[:2 * nc], res[2 * nc:2 * nc + n], res[2 * nc + n:2 * nc + n + nl], res[-1]


def _split_wait(name, sems, srcs, lands, after, copies_fn, nc):
    n, nl = len(srcs), len(lands)
    after = list(after) if isinstance(after, (list, tuple)) else [after]

    def body(*refs):
        src_refs, land_refs = refs[:n], refs[n:n + nl]
        sem_refs = refs[n + nl:n + nl + 2 * nc]
        for cp in _remote_copies(copies_fn, src_refs, land_refs, sem_refs[:nc], sem_refs[nc:]):
            cp.wait_send()
            cp.wait_recv()

    res = _pcall(
        body, name=name, in_specs=[_HBM] * (n + nl) + [_SEM] * (2 * nc) + [_ANY] * len(after),
        out_specs=[_HBM] * (n + nl), out_shape=[pltpu.HBM(a.shape, a.dtype) for a in list(srcs) + list(lands)],
        input_output_aliases={i: i for i in range(n + nl)},
        compiler_params=pltpu.CompilerParams(has_side_effects=_EFFECT),
    )(*srcs, *lands, *sems, *after)
    return res[:n], res[n:]


def _other_chips(x, y):
    return [(1 - x, y), (x, 1 - y), (1 - x, 1 - y)]


def _ag_copies(x, y, c, srcs, lands):
    out = []
    for land in lands:
        blk = land.at[4 * x + 2 * y + c]
        out.append((blk, blk, (x, y, 1 - c)))
        out += [(blk, blk, (px, py, c)) for px, py in _other_chips(x, y)]
    return out


def _rs_copies(x, y, c, srcs, lands):
    return [(s.at[2 * px + py], land.at[j], (px, py, c))
            for s, land in zip(srcs, lands) for j, (px, py) in enumerate(_other_chips(x, y))]


def _fwd_copies(x, y, c, srcs, lands):
    out = []
    for land in lands:
        for px, py in _other_chips(x, y):
            blk = land.at[4 * px + 2 * py + c]
            out.append((blk, blk, (x, y, 1 - c)))
    return out


def _direct_copies(x, y, c, srcs, lands):
    out = []
    for s, land in zip(srcs, lands):
        for r in range(1, N_DEV):
            px = 1 - x if r & 4 else x
            py = 1 - y if r & 2 else y
            pc = 1 - c if r & 1 else c
            out.append((s.at[4 * px + 2 * py + pc], land.at[r - 1], (px, py, pc)))
    return out


def _ag_finish(name, lands):
    n = len(lands)

    def body(*refs):
        outs = refs[n:2 * n]
        send, recv = refs[2 * n:]
        x, y, c = _position()

        def swap(a, j, px, py, pc):
            blk = outs[a].at[4 * px + 2 * py + pc]
            return pltpu.make_async_remote_copy(src_ref=blk, dst_ref=blk, send_sem=send.at[a, j], recv_sem=recv.at[a, j],
                                                device_id=(x, y, 1 - c), device_id_type=MESH)

        chips = _other_chips(x, y)
        sends = [swap(a, j, px, py, c) for a in range(n) for j, (px, py) in enumerate(chips)]
        for cp in sends:
            cp.start()
        for a in range(n):
            for j, (px, py) in enumerate(chips):
                swap(a, j, px, py, 1 - c).wait_recv()
        for cp in sends:
            cp.wait_send()

    return _pcall(body, name=name, in_specs=[_ANY] * n, out_specs=[_ANY] * n,
                  out_shape=[S(l.shape, l.dtype) for l in lands], input_output_aliases={a: a for a in range(n)},
                  scratch_shapes=[pltpu.SemaphoreType.DMA((n, 3)), pltpu.SemaphoreType.DMA((n, 3))])(*lands)


def _pad_lanes(v):
    return jnp.pad(v, ((0, 0), (0, LANES - v.shape[1])))


def _flat_rows(pieces):
    flat = jnp.concatenate([p.reshape(-1) for p in pieces])
    rows = -(-flat.shape[0] // (SMALL_W * SUBLANES)) * SUBLANES
    return jnp.pad(flat, (0, rows * SMALL_W - flat.shape[0])).reshape(rows, SMALL_W)


def _unflat(buf, shapes):
    flat = buf.reshape(-1)
    out, off = [], 0
    for sh in shapes:
        n = 1
        for d in sh:
            n *= d
        out.append(flat[off:off + n].reshape(sh))
        off += n
    return out


def kernel(x, pre_mix_norm, w_in, ssd_conv_w, ssd_conv_b, ssd_dt_bias, ssd_a_log, ssd_d, ssd_norm, lru_conv_w, lru_conv_b, lru_w_a, lru_b_a, lru_w_x, lru_b_x, lru_lambda, lru_norm, w_out, post_mix_norm, pre_mlp_norm, w_mlp_in, w_mlp_out, post_mlp_norm, loss_target, m_pre_mix_norm, m_w_in, m_ssd_conv_w, m_ssd_conv_b, m_ssd_dt_bias, m_ssd_a_log, m_ssd_d, m_ssd_norm, m_lru_conv_w, m_lru_conv_b, m_lru_w_a, m_lru_b_a, m_lru_w_x, m_lru_b_x, m_lru_lambda, m_lru_norm, m_w_out, m_post_mix_norm, m_pre_mlp_norm, m_w_mlp_in, m_w_mlp_out, m_post_mlp_norm, v_pre_mix_norm, v_w_in, v_ssd_conv_w, v_ssd_conv_b, v_ssd_dt_bias, v_ssd_a_log, v_ssd_d, v_ssd_norm, v_lru_conv_w, v_lru_conv_b, v_lru_w_a, v_lru_b_a, v_lru_w_x, v_lru_b_x, v_lru_lambda, v_lru_norm, v_w_out, v_post_mix_norm, v_pre_mlp_norm, v_w_mlp_in, v_w_mlp_out, v_post_mlp_norm):
    names = ['pre_mix_norm', 'w_in', 'ssd_conv_w', 'ssd_conv_b', 'ssd_dt_bias', 'ssd_a_log', 'ssd_d', 'ssd_norm',
             'lru_conv_w', 'lru_conv_b', 'lru_w_a', 'lru_b_a', 'lru_w_x', 'lru_b_x', 'lru_lambda', 'lru_norm',
             'w_out', 'post_mix_norm', 'pre_mlp_norm', 'w_mlp_in', 'w_mlp_out', 'post_mlp_norm']
    loc = locals()
    W = {n: loc[n] for n in names}
    Mo = {n: loc["m_" + n] for n in names}
    Vo = {n: loc["v_" + n] for n in names}
    big = ['w_in', 'w_out', 'w_mlp_in', 'w_mlp_out']

    px, py, pc = _position()
    dev = 4 * px + 2 * py + pc
    dev_idx = jnp.reshape(dev, (1,)).astype(jnp.int32)
    c_idx = jnp.reshape(pc, (1,)).astype(jnp.int32)
    chip_idx = jnp.reshape(2 * px + py, (1,)).astype(jnp.int32)

    _, T, D = x.shape
    x2 = x.reshape(T, D)
    tgt = loss_target.reshape(T, D)
    n_heads = ssd_dt_bias.shape[1]
    XBC = ssd_conv_b.shape[1]
    GN = XBC // 4
    DS = XBC - 2 * GN
    DL = lru_norm.shape[1]
    DFF = w_mlp_in.shape[2] * N_DEV
    DIN = w_in.shape[2] * N_DEV
    NP = XBC + DS + 2 * DL + LANES
    assert DS % GN == 0 and XBC % DS == 0 and DS == DL and n_heads <= LANES
    cb_z, cb_gate, cb_xl, cb_dt = XBC // DS, XBC // DS + 1, XBC // DS + 2, (XBC + DS + 2 * DL) // LANES
    tr = min(256, T // 2)
    nt = T // tr
    Q = min(256, T // 2)

    Dh = D // 2
    sh_a = _cast_bf16("cast_w_in_a", W['w_in'], dev_idx, 0, Dh)
    sh_b = _cast_bf16("cast_w_in_b", W['w_in'], dev_idx, Dh, Dh)
    later = big[1:]
    a_bufs = [sh_a, _own_block(ssd_conv_w[0], dev), _own_block(lru_conv_w[0], dev)]
    a_sems, _, a_lands, a_token = _split_start("allgather_w_in_a_start", [], a_bufs, _ag_copies, 4 * len(a_bufs))
    sh = {n: _cast_bf16("cast_" + n, W[n], dev_idx, deps=[a_token]) for n in later}

    def f_norm_in(first, last, xv, g):
        return (_rms(xv, g),), ()
    (h,) = _rows_call("norm_in", f_norm_in, nt, [x2, pre_mix_norm], [_rt(tr, D), _full(pre_mix_norm)],
                      [((T, D), BF16, _rt(tr, D))], [], 'tf', deps=[a_token])
    Mo['w_in'] = Mo['w_in'] + a_token[0, 0]
    Vo['w_in'] = Vo['w_in'] + a_token[0, 0]
    _, a_lands = _split_wait("allgather_w_in_a_wait", a_sems, [], a_lands,
                             [h, Mo['w_in'], Vo['w_in']] + [sh[n] for n in later], _ag_copies, 4 * len(a_bufs))
    g_in_a, g_cs, g_cl = _ag_finish("allgather_w_in_a_finish", a_lands)
    b_sems, _, b_lands, b_token = _split_start("allgather_w_in_b_start", [], [sh_b], _ag_copies, 4, after=[g_in_a])
    ag_sems, ag_srcs, ag_lands, ag_token = _split_start(
        "allgather_later_start", [], [sh[n] for n in later], _ag_copies, 4 * len(later), after=[b_token])
    conv_s = jnp.transpose(g_cs, (1, 0, 2)).reshape(CONV_WIDTH, XBC)
    conv_l = jnp.transpose(g_cl, (1, 0, 2)).reshape(CONV_WIDTH, DL)
    wb = DIN // N_DEV
    o_z, o_xbc, o_dt, o_gate, o_xl = 0, DS, DS + XBC, DS + XBC + n_heads, DS + XBC + n_heads + DL
    segs = [(o_xbc, o_xbc + XBC, 0), (o_z, o_z + DS, XBC), (o_gate, o_gate + DL, XBC + DS),
            (o_xl, o_xl + DL, XBC + DS + DL), (o_dt, o_dt + n_heads, NP - LANES)]

    def ref_cols(g, lo, hi):
        out = []
        while lo < hi:
            k = lo // wb
            e = min(hi, (k + 1) * wb)
            out.append(g[k, :, lo - k * wb:e - k * wb])
            lo = e
        return out

    def laid_out(g):
        return jnp.concatenate([p for a, b, _ in segs for p in ref_cols(g, a, b)]
                               + [jnp.zeros((g.shape[1], LANES - n_heads), BF16)], axis=1)

    def my_cols(g, lo, hi):
        out = []
        for a, b, m in sorted(segs):
            s, e = max(lo, a), min(hi, b)
            if s < e:
                out.append(g[:, m + s - a:m + e - a])
        return out

    wp_a = laid_out(g_in_a)
    dt_bias = _pad_lanes(ssd_dt_bias)
    a_log = _pad_lanes(ssd_a_log)
    d_skip = _pad_lanes(ssd_d)
    wa_b, wx_b = lru_w_a[0].astype(BF16), lru_w_x[0].astype(BF16)
    b_a, b_x = lru_b_a.reshape(1, DL), lru_b_x.reshape(1, DL)

    (proj_a,) = _mm("proj_a", h, wp_a, a_cols=(0, Dh), outs=((BF16, None),), deps=[ag_token])
    (dt_a,) = _mm("proj_dt_a", h, wp_a[:, NP - LANES:], a_cols=(0, Dh))
    _, b_lands = _split_wait("allgather_w_in_b_wait", b_sems, [], b_lands, proj_a, _ag_copies, 4)
    (g_in_b,) = _ag_finish("allgather_w_in_b_finish", b_lands)
    wp_b = laid_out(g_in_b)
    add = lambda r, e: r + e.astype(F32)
    (proj,) = _mm("proj_b", h, wp_b, a_cols=(Dh, Dh), extra=proj_a, outs=((BF16, add),))
    (dt_raw,) = _mm("proj_dt_b", h, wp_b[:, NP - LANES:], a_cols=(Dh, Dh), extra=dt_a, outs=((F32, add),))

    cwx = min(1024, XBC)

    def f_ssd_pre(first, last, xbc, halo, w, b):
        pre = _conv_pre(xbc, jnp.where(first, 0.0, halo), w, b)
        return (pre * jax.nn.sigmoid(pre),), ()
    (xbc_act,) = _rows_call(
        "ssd_pre", f_ssd_pre, nt, [proj, proj, conv_s, ssd_conv_b],
        [_rt(tr, XBC), _halo_prev(tr, XBC, rows=PACKED_ROWS), _full(conv_s), _full(ssd_conv_b)],
        [((T, XBC), F32, _rt(tr, XBC))], [], ['t', 'p0', 'f', 'f'], cw=cwx)

    def f_ssd_dt(first, last, dtr, dtb):
        return (jax.nn.softplus(dtr + dtb),), ()
    (dt,) = _rows_call("ssd_dt", f_ssd_dt, nt, [dt_raw, dt_bias], [_rt(tr, LANES), _full(dt_bias)],
                       [((T, LANES), F32, _rt(tr, LANES))], [], 'tf')

    y_ssd, h_prev = _ssd_fwd(xbc_act, dt, a_log, d_skip, n_heads, Q)

    gw = DS // SSD_GROUPS

    def ssd_post(y, z, g):
        yz = y * jax.nn.silu(z)
        parts = []
        for k in range(SSD_GROUPS):
            yk = yz[:, k * gw:(k + 1) * gw]
            parts.append(yk * lax.rsqrt(jnp.mean(yk * yk, axis=-1, keepdims=True) + EPS))
        return jnp.concatenate(parts, axis=-1) * g

    def f_ssd_post(first, last, y, z, g):
        return (ssd_post(y, z, g),), ()
    (mixcat,) = _rows_call("ssd_post", f_ssd_post, nt, [y_ssd, proj, ssd_norm],
                           [_rt(tr, DS), _rt(tr, DS, cb_z), _full(ssd_norm)], [((T, DS + DL), BF16, _rt(tr, DS))], [],
                           'ttf')

    def f_lru_pre(first, last, xv, halo, w, b):
        return (_conv_pre(xv, jnp.where(first, 0.0, halo), w, b),), ()
    (xl,) = _rows_call("lru_pre", f_lru_pre, nt, [proj, proj, conv_l, lru_conv_b],
                       [_rt(tr, DL, cb_xl), _halo_prev(tr, DL, cb_xl, rows=PACKED_ROWS), _full(conv_l),
                        _full(lru_conv_b)],
                       [((T, DL), F32, _rt(tr, DL))], [], ['t', 'p0', 'f', 'f'])

    a_lru, u_lru = _lru_gates_fwd(xl, wa_b, b_a, wx_b, b_x, lru_lambda, tr)
    h_lru = _lru_scan_fwd(a_lru, u_lru, tr)
    _, ag_lands = _split_wait("allgather_later_wait", ag_sems, ag_srcs, ag_lands, h_lru, _ag_copies, 4 * len(later))
    f_sems, _, ag_lands, f_token = _split_start("allgather_later_fwd_start", [], ag_lands, _fwd_copies, 3 * len(later))

    def lru_post(hv, gate, g):
        return _rms(hv * jax.nn.gelu(gate), g)

    def f_lru_post(first, last, hv, gate, g):
        return (lru_post(hv, gate, g),), ()
    cb_l = DS // DL
    (mixcat,) = _rows_call("lru_post", f_lru_post, nt, [h_lru, proj, lru_norm],
                           [_rt(tr, DL), _rt(tr, DL, cb_gate), _full(lru_norm)],
                           [((T, DS + DL), BF16, _rt(tr, DL, cb_l))], [], 'ttf', into=mixcat, deps=[f_token])

    _, (g_out, g_mi, g_mo) = _split_wait("allgather_later_fwd_wait", f_sems, [], ag_lands, mixcat, _fwd_copies,
                                         3 * len(later))
    w_out_f = g_out.reshape(DS + DL, D)
    w_mi_f = jnp.transpose(g_mi, (1, 0, 2)).reshape(D, DFF)
    w_mo_f = g_mo.reshape(DFF, D)
    (mix,) = _mm("mix", mixcat, w_out_f, outs=((BF16, None),))

    def f_post_mix(first, last, xv, mx, gpm, gpl):
        x1 = xv + _rms(mx, gpm)
        return (x1, _rms(x1, gpl)), ()
    x1, hn = _rows_call("post_mix", f_post_mix, nt, [x2, mix, post_mix_norm, pre_mlp_norm],
                        [_rt(tr, D), _rt(tr, D), _full(post_mix_norm), _full(pre_mlp_norm)],
                        [((T, D), F32, _rt(tr, D)), ((T, D), BF16, _rt(tr, D))], [], 'ttff')

    hm, act = _mm("mlp_in", hn, w_mi_f,
                  outs=((BF16, None), (BF16, lambda r, e: jnp.square(jnp.maximum(r, 0.0)))))
    (hm2,) = _mm("mlp_out", act, w_mo_f, outs=((BF16, None),))

    def f_final(first, last, x1v, hm2v, g, tg):
        err = x1v + _rms(hm2v, g) - tg
        dx2 = err * (1.0 / D)
        dh, dg = _rms_bwd(hm2v, g, dx2)
        loss = jnp.full((1, LANES), 0.5 / D, F32) * jnp.sum(err * err)
        return (dx2, dh), (dg, loss)
    dx1a, dhm2, g_post_mlp, loss_part = _rows_call(
        "loss_head", f_final, nt, [x1, hm2, post_mlp_norm, tgt],
        [_rt(tr, D), _rt(tr, D), _full(post_mlp_norm), _rt(tr, D)],
        [((T, D), BF16, _rt(tr, D)), ((T, D), BF16, _rt(tr, D))], [(1, D), (1, LANES)], 'ttft')

    def rs_begin(n, full):
        (from_sib,) = _rs_sibling("rs_sibling_" + n, [full])
        pair = _pair_sum("pair_sum_" + n, full, from_sib, c_idx)
        sems, srcs, lands, token = _split_start("rs_start_" + n, [pair], [lax.empty((3,) + pair.shape[1:], BF16)],
                                                _rs_copies, 3)
        return (sems, srcs, lands), token

    def rs_end(n, state, after):
        (pair,), (recv,) = _split_wait("rs_wait_" + n, *state, after, _rs_copies, 3)
        return pair, recv, chip_idx

    def rs_direct_begin(n, full):
        sems, srcs, lands, token = _split_start("rs_start_" + n, [full],
                                                [lax.empty((N_DEV - 1,) + full.shape[1:], BF16)], _direct_copies,
                                                N_DEV - 1)
        return (sems, srcs, lands), token

    def rs_direct_end(n, state, after):
        (full,), (recv,) = _split_wait("rs_wait_" + n, *state, after, _direct_copies, N_DEV - 1)
        return full, recv, dev_idx

    (gw_mo,) = _mm("dw_mlp_out", act, dhm2, ta=True, outs=((BF16, None),))
    rs_mo, tok = rs_direct_begin('w_mlp_out', gw_mo.reshape(N_DEV, DFF // N_DEV, D))
    (dhm,) = _mm("d_mlp_act", dhm2, w_mo_f, tb=True, extra=hm,
                 outs=((BF16, lambda r, e: r * (2.0 * jnp.maximum(e.astype(F32), 0.0))),), deps=[tok])
    (gw_mi,) = _mm("dw_mlp_in", hn, dhm, ta=True, outs=((BF16, None),), out_blocks=N_DEV)
    rs_mi, tok = rs_direct_begin('w_mlp_in', gw_mi)
    (dhn,) = _mm("d_mlp_in", dhm, w_mi_f, tb=True, outs=((BF16, None),), deps=[tok])

    def f_post_mix_bwd(first, last, x1v, mx, gpm, gpl, dhnv, dxa):
        dx1, dgpl = _rms_bwd(x1v, gpl, dhnv)
        dx1 = dx1 + dxa
        dmx, dgpm = _rms_bwd(mx, gpm, dx1)
        return (dx1, dmx), (dgpl, dgpm)
    dx1, dmix, g_pre_mlp, g_post_mix = _rows_call(
        "post_mix_bwd", f_post_mix_bwd, nt, [x1, mix, post_mix_norm, pre_mlp_norm, dhn, dx1a],
        [_rt(tr, D), _rt(tr, D), _full(post_mix_norm), _full(pre_mlp_norm), _rt(tr, D), _rt(tr, D)],
        [((T, D), BF16, _rt(tr, D)), ((T, D), BF16, _rt(tr, D))], [(1, D), (1, D)], 'ttfftt')

    (gw_out,) = _mm("dw_out", mixcat, dmix, ta=True, outs=((BF16, None),))
    rs_out, tok = rs_direct_begin('w_out', gw_out.reshape(N_DEV, -1, D))
    (dmixcat,) = _mm("d_mix", dmix, w_out_f, tb=True, outs=((BF16, None),), deps=[tok])

    def f_lru_post_bwd(first, last, hv, gate, g, dy):
        gl, dgl = _gelu_and_grad(gate)
        dv, dg = _rms_bwd(hv * gl, g, dy)
        return (dv * hv * dgl, dv * gl), (dg,)
    dproj, dh_lru, g_lru_norm = _rows_call(
        "lru_post_bwd", f_lru_post_bwd, nt, [h_lru, proj, lru_norm, dmixcat],
        [_rt(tr, DL), _rt(tr, DL, cb_gate), _full(lru_norm), _rt(tr, DL, cb_l)],
        [((T, NP), BF16, _rt(tr, DL, cb_gate)), ((T, DL), BF16, _rt(tr, DL))], [(1, DL)], 'ttft')

    du_lru, da_lru = _lru_scan_bwd(a_lru, h_lru, dh_lru, tr)
    dxl, g_wa, g_ba, g_wx, g_bx, g_lam = _lru_gates_bwd(xl, wa_b, b_a, wx_b, b_x, lru_lambda, da_lru, du_lru, tr)

    conv_bwd_kinds = ['t', 'p0', 'n0', 't', 'n3', 'f', 'f']

    def f_lru_pre_bwd(first, last, xv, hp, xn, d, dn, w, b):
        dx, dw8, db = _conv_bwd_tile(first, last, xv, hp, xn, d, dn, w, b, silu=False)
        return (dx,), (dw8, db)
    dproj, g_convl8, g_convl_b = _rows_call(
        "lru_pre_bwd", f_lru_pre_bwd, nt, [proj, proj, proj, dxl, dxl, conv_l, lru_conv_b],
        [_rt(tr, DL, cb_xl), _halo_prev(tr, DL, cb_xl, rows=PACKED_ROWS),
         _halo_next(tr, DL, nt, cb_xl, rows=PACKED_ROWS), _rt(tr, DL),
         _halo_next(tr, DL, nt, rows=PACKED_ROWS), _full(conv_l), _full(lru_conv_b)],
        [((T, NP), BF16, _rt(tr, DL, cb_xl))], [(SUBLANES, DL), (1, DL)], conv_bwd_kinds, into=dproj)

    def f_ssd_post_bwd(first, last, y, z, g, dy):
        s = jax.nn.sigmoid(z)
        sz = z * s
        yz = y * sz
        gdy = dy * g
        dyz, yn = [], []
        for k in range(SSD_GROUPS):
            cols = slice(k * gw, (k + 1) * gw)
            r = lax.rsqrt(jnp.mean(yz[:, cols] * yz[:, cols], axis=-1, keepdims=True) + EPS)
            xr = yz[:, cols] * r
            dyz.append((gdy[:, cols] - xr * jnp.mean(gdy[:, cols] * xr, axis=-1, keepdims=True)) * r)
            yn.append(xr)
        dyz = jnp.concatenate(dyz, axis=-1)
        dz = dyz * y * (s + sz * (1.0 - s))
        return (dz, dyz * sz), (_colsum(dy * jnp.concatenate(yn, axis=-1)),)
    dproj, dy_ssd, g_ssd_norm = _rows_call(
        "ssd_post_bwd", f_ssd_post_bwd, nt, [y_ssd, proj, ssd_norm, dmixcat],
        [_rt(tr, DS), _rt(tr, DS, cb_z), _full(ssd_norm), _rt(tr, DS, 0)],
        [((T, NP), BF16, _rt(tr, DS, cb_z)), ((T, DS), BF16, _rt(tr, DS))], [(1, DS)], 'ttft', into=dproj)

    dxbc_act, ddt, g_alog, g_dskip = _ssd_bwd(xbc_act, dt, a_log, d_skip, h_prev, dy_ssd, n_heads, Q)

    def f_ssd_pre_bwd(first, last, xv, hp, xn, d, dn, w, b):
        dx, dw8, db = _conv_bwd_tile(first, last, xv, hp, xn, d, dn, w, b, silu=True)
        return (dx,), (dw8, db)
    dproj, g_convs8, g_convs_b = _rows_call(
        "ssd_pre_bwd", f_ssd_pre_bwd, nt, [proj, proj, proj, dxbc_act, dxbc_act, conv_s, ssd_conv_b],
        [_rt(tr, XBC), _halo_prev(tr, XBC, rows=PACKED_ROWS), _halo_next(tr, XBC, nt, rows=PACKED_ROWS),
         _rt(tr, XBC), _halo_next(tr, XBC, nt),
         _full(conv_s), _full(ssd_conv_b)],
        [((T, NP), BF16, _rt(tr, XBC))], [(SUBLANES, XBC), (1, XBC)], conv_bwd_kinds, into=dproj, cw=cwx)

    def f_ssd_dt_bwd(first, last, ddtv, dtr, dtb):
        ddtr = ddtv * jax.nn.sigmoid(dtr + dtb)
        return (ddtr,), (_colsum(ddtr),)
    dproj, g_dtb = _rows_call(
        "ssd_dt_bwd", f_ssd_dt_bwd, nt, [ddt, dt_raw, dt_bias],
        [_rt(tr, LANES), _rt(tr, LANES), _full(dt_bias)],
        [((T, NP), BF16, _rt(tr, LANES, cb_dt))], [(1, LANES)], 'ttf', into=dproj)
    small = {
        'ssd_conv_w': g_convs8[:CONV_WIDTH], 'ssd_conv_b': g_convs_b,
        'ssd_dt_bias': g_dtb[:, :n_heads], 'ssd_a_log': g_alog[:, :n_heads], 'ssd_d': g_dskip[:, :n_heads],
        'ssd_norm': g_ssd_norm, 'lru_conv_w': g_convl8[:CONV_WIDTH], 'lru_conv_b': g_convl_b,
        'lru_w_a': g_wa, 'lru_b_a': g_ba, 'lru_w_x': g_wx, 'lru_b_x': g_bx, 'lru_lambda': g_lam,
        'lru_norm': g_lru_norm, 'post_mix_norm': g_post_mix, 'pre_mlp_norm': g_pre_mlp,
        'post_mlp_norm': g_post_mlp, 'loss': loss_part[:, :1],
    }
    wide = ['lru_w_a', 'lru_w_x']
    narrow = [n for n in small if n not in wide]
    lb = lru_w_a.shape[-1]
    s_srcs = [_flat_rows([small[n] for n in narrow]), g_wa.reshape(-1, lb), g_wx.reshape(-1, lb)]
    s_sems, s_srcs, s_lands, tok = _split_start(
        "small_grads_start", [], [_own_block(a, dev) for a in s_srcs], _ag_copies, 4 * len(s_srcs))

    (gwp,) = _mm("dw_proj", h, dproj, ta=True, outs=((BF16, None),), deps=[tok])
    rs_in, tok = rs_begin(
        'w_in', jnp.stack([jnp.concatenate(my_cols(gwp, k * wb, (k + 1) * wb), axis=1) for k in range(N_DEV)]))
    (dh_a,) = _mm("d_proj_a", dproj, wp_a, tb=True, outs=((BF16, None),), deps=[tok])
    (dh_b,) = _mm("d_proj_b", dproj, wp_b, tb=True, outs=((BF16, None),), deps=[tok])

    def f_norm_in_bwd(first, last, xv, g, dha, dhb, dxa):
        dx, dg = _rms_bwd(xv, g, jnp.concatenate([dha, dhb], axis=1))
        return (dx + dxa,), (dg,)
    grad_x, g_pre_mix = _rows_call(
        "norm_in_bwd", f_norm_in_bwd, nt, [x2, pre_mix_norm, dh_a, dh_b, dx1],
        [_rt(tr, D), _full(pre_mix_norm), _rt(tr, Dh), _rt(tr, Dh), _rt(tr, D)], [((T, D), F32, _rt(tr, D))],
        [(1, D)], 'tfttt')

    big_out = {}
    for n, state in (('w_mlp_out', rs_mo), ('w_mlp_in', rs_mi), ('w_out', rs_out)):
        big_out[n] = _adamw_big("adamw_" + n, W[n], Mo[n], Vo[n], *rs_direct_end(n, state, grad_x))

    (g_pm8,) = _allgather("allgather_pre_mix_grad", [_own_block(g_pre_mix, dev)], deps=[big_out['w_out'][0]])
    _, s_lands = _split_wait("small_grads_wait", s_sems, s_srcs, s_lands, g_pm8, _ag_copies, 4 * len(s_lands))
    g_narrow, g_wa8, g_wx8 = _ag_finish("small_grads_finish", s_lands)
    summed = dict(zip(narrow, _unflat(_sum8("sum_small_grads", g_narrow), [small[n].shape for n in narrow])))
    summed['pre_mix_norm'] = _sum8("sum_pre_mix_grad", g_pm8)
    summed['lru_w_a'] = _sum8("sum_lru_w_a_grads", g_wa8)
    summed['lru_w_x'] = _sum8("sum_lru_w_x_grads", g_wx8)
    loss = summed.pop('loss').reshape(())
    for n, full_w in (('ssd_conv_w', XBC), ('lru_conv_w', DL)):
        wdt = full_w // N_DEV
        summed[n] = lax.dynamic_slice_in_dim(summed[n], dev * wdt, wdt, axis=1)
    small_params = [n for n in names if n not in big]
    as2d = lambda a: a.reshape(-1, a.shape[-1])
    res = _adamw_small([as2d(W[n]) for n in small_params],
                       [summed[n].reshape(as2d(W[n]).shape) for n in small_params],
                       [as2d(Mo[n]) for n in small_params], [as2d(Vo[n]) for n in small_params])
    grads = {n: summed[n].reshape(W[n].shape) for n in small_params}
    delta, new_m, new_v = ({n: r.reshape(W[n].shape) for n, r in zip(small_params, rs)} for rs in res)

    big_out['w_in'] = _adamw_big("adamw_w_in", W['w_in'], Mo['w_in'], Vo['w_in'], *rs_end('w_in', rs_in, g_pm8))
    for n in big:
        grads[n], delta[n], new_m[n], new_v[n] = big_out[n]

    return (loss, grad_x.reshape(x.shape), *[grads[n] for n in names], *[delta[n] for n in names],
            *[new_m[n] for n in names], *[new_v[n] for n in names])
```
